```python
import math
import jax, jax.numpy as jnp
from jax import lax
import numpy as np

D_MODEL = 1024
BATCH = 8
SEQ = 2048
DEPTH = 1
DEC_BATCH = 32
DEC_SEQ = 1
PAST_LEN = 16384
PAGE_SIZE = 128

NSA_HEADS = 8
NSA_KV_HEADS = 2
NSA_GROUP = NSA_HEADS // NSA_KV_HEADS
NSA_HEAD_DIM = 64
NSA_WIDTH = NSA_HEADS * NSA_HEAD_DIM
NSA_KV_WIDTH = NSA_KV_HEADS * 2 * NSA_HEAD_DIM
ROT_DIM = NSA_HEAD_DIM // 4
ROPE_THETA = 500000.0
CMP_BLOCK = 32
SEL_BLOCK = 64
SEL_RATIO = SEL_BLOCK // CMP_BLOCK
N_SELECT = 16
WINDOW = 512
NSA_QBLOCK = 32
NSA_SCALE = NSA_HEAD_DIM ** -0.5

MLA_HEADS = 8
MLA_Q_LORA = 384
MLA_KV_LORA = 256
MLA_NOPE = 64
MLA_ROPE = 32
MLA_V = 64
MLA_WIDTH = MLA_HEADS * MLA_V
MLA_ROPE_THETA = 10000.0
MLA_QBLOCK = 128
MLA_SCALE = (MLA_NOPE + MLA_ROPE) ** -0.5

RMS_EPS = 1e-6
NEG_INF = -1e30
FORCE_SCORE = 1e4

IN_SPLITS = (NSA_WIDTH, NSA_KV_WIDTH, NSA_KV_WIDTH, NSA_KV_WIDTH, 3 * NSA_HEADS, NSA_WIDTH,
             MLA_Q_LORA, MLA_KV_LORA, MLA_ROPE, MLA_WIDTH, D_MODEL, D_MODEL)
IN_WIDTH = sum(IN_SPLITS)
IN_OFFSETS = tuple(int(v) for v in np.cumsum(IN_SPLITS)[:-1])

kernel_name = "hybrid_nsa_mla_gated_decode_step"


def _rmsnorm(x, g):
    x32 = x.astype(jnp.float32)
    y = x32 * lax.rsqrt(jnp.mean(x32 * x32, axis=-1, keepdims=True) + RMS_EPS)
    return (y * g.astype(jnp.float32)).astype(x.dtype)


def _rope(x, pos, theta):
    half = x.shape[-1] // 2
    inv = 1.0 / (jnp.float32(theta) ** (jnp.arange(half, dtype=jnp.float32) / half))
    ang = pos.astype(jnp.float32)[:, None] * inv[None, :]
    if x.ndim == 4:
        ang = ang[:, None, :]
    cos, sin = jnp.cos(ang), jnp.sin(ang)
    x1 = x[..., :half].astype(jnp.float32)
    x2 = x[..., half:].astype(jnp.float32)
    return jnp.concatenate([x1 * cos - x2 * sin, x1 * sin + x2 * cos], axis=-1).astype(x.dtype)


def _partial_rope(x, pos):
    return jnp.concatenate([_rope(x[..., :ROT_DIM], pos, ROPE_THETA), x[..., ROT_DIM:]], axis=-1)


def _rope_kv(kv, pos):
    return jnp.stack([_partial_rope(kv[..., 0, :], pos), kv[..., 1, :]], axis=-2)


def _masked_softmax(s, mask):
    s = jnp.where(mask, s.astype(jnp.float32), NEG_INF)
    p = jax.nn.softmax(s, axis=-1)
    return jnp.where(mask, p, 0.0)


def _to_blocks(a, qb):
    b, t = a.shape[:2]
    return a.reshape(b, t // qb, qb, *a.shape[2:]).swapaxes(0, 1)


def _from_blocks(a):
    a = a.swapaxes(0, 1)
    return a.reshape(a.shape[0], a.shape[1] * a.shape[2], *a.shape[3:])


def _compress(kv, pe_cmp):
    b, s = kv.shape[:2]
    s_pad = -(-s // SEL_BLOCK) * SEL_BLOCK
    kv = jnp.pad(kv, ((0, 0), (0, s_pad - s), (0, 0), (0, 0), (0, 0)))
    blocks = kv.reshape(b, s_pad // CMP_BLOCK, CMP_BLOCK, NSA_KV_HEADS, 2, NSA_HEAD_DIM)
    c = jnp.mean(blocks + pe_cmp[None, None], axis=2)
    return c[..., 0, :], c[..., 1, :]


def _nsa_attend(q, q_rot, g, t, kc, vc, gather, kvw, pos_w):
    b, nq = q.shape[:2]
    qg = q.reshape(b, nq, NSA_KV_HEADS, NSA_GROUP, NSA_HEAD_DIM)
    qr = q_rot.reshape(b, nq, NSA_KV_HEADS, NSA_GROUP, NSA_HEAD_DIM)
    nc = kc.shape[1]
    nb = nc // SEL_RATIO
    s_c = jnp.einsum('bqgrd,bngd->bqgrn', qg, kc) * NSA_SCALE
    end_c = (jnp.arange(nc) + 1) * CMP_BLOCK - 1
    mask_c = (end_c[None, :] <= t[:, None])[None, :, None, None, :]
    p_c = _masked_softmax(s_c, mask_c)
    o_c = jnp.einsum('bqgrn,bngd->bqgrd', p_c.astype(vc.dtype), vc)
    imp = p_c.sum(axis=3).reshape(b, nq, NSA_KV_HEADS, nb, SEL_RATIO).sum(-1)
    j = jnp.arange(nb)
    valid = (j[None, :] * SEL_BLOCK <= t[:, None])[None, :, None, :]
    forced = ((j[None, :] == (t // SEL_BLOCK)[:, None]) | (j[None, :] == 0))[None, :, None, :]
    score = jnp.where(forced, FORCE_SCORE, jnp.where(valid, imp, -FORCE_SCORE))
    top, idx = lax.top_k(score, min(N_SELECT, nb))
    pos_sel = idx[..., None] * SEL_BLOCK + jnp.arange(SEL_BLOCK)
    k_sel, v_sel = gather(pos_sel)
    n_sel = pos_sel.shape[3]
    s_s = jnp.einsum('bqgrd,bqgkld->bqgrkl', qr, k_sel) * NSA_SCALE
    mask_s = (top > -1.0)[..., None] & (pos_sel <= t[None, :, None, None, None])
    p_s = _masked_softmax(s_s.reshape(b, nq, NSA_KV_HEADS, NSA_GROUP, n_sel * SEL_BLOCK),
                          mask_s.reshape(b, nq, NSA_KV_HEADS, 1, n_sel * SEL_BLOCK))
    p_s = p_s.reshape(s_s.shape)
    o_s = jnp.einsum('bqgrkl,bqgkld->bqgrd', p_s.astype(v_sel.dtype), v_sel)
    s_w = jnp.einsum('bqgrd,bsgd->bqgrs', qr, kvw[..., 0, :]) * NSA_SCALE
    dist = t[:, None] - pos_w[None, :]
    mask_w = ((dist >= 0) & (dist <= WINDOW) & (pos_w[None, :] >= 0))[None, :, None, None, :]
    p_w = _masked_softmax(s_w, mask_w)
    o_w = jnp.einsum('bqgrs,bsgd->bqgrd', p_w.astype(kvw.dtype), kvw[..., 1, :])
    gg = g.reshape(b, nq, NSA_KV_HEADS, NSA_GROUP, 3)
    o = gg[..., 0:1] * o_c + gg[..., 1:2] * o_s + gg[..., 2:3] * o_w
    return o.reshape(b, nq, NSA_WIDTH)


def _mla_attend(q_lat, q_pe, t, c, kr, pos_k):
    s = (jnp.einsum('bqhc,bsc->bhqs', q_lat, c) + jnp.einsum('bqhr,bsr->bhqs', q_pe, kr)) * MLA_SCALE
    mask = (pos_k[None, :] <= t[:, None])[None, None]
    p = _masked_softmax(s, mask)
    return jnp.einsum('bhqs,bsc->bqhc', p.astype(c.dtype), c)


def _project_in(x, pos, norm_pre, w_in, q_norm, w_q_up, kv_norm, w_kv_up):
    b, t = x.shape[:2]
    xn = _rmsnorm(x, norm_pre)
    (qa, kvc, kvs, kvw, gn, ga, qd, kvd, krp, gb, ma, mb) = jnp.split(xn @ w_in, IN_OFFSETS, axis=-1)
    kv_shape = (b, t, NSA_KV_HEADS, 2, NSA_HEAD_DIM)
    q = qa.reshape(b, t, NSA_HEADS, NSA_HEAD_DIM)
    nsa = (q, _partial_rope(q, pos), jax.nn.sigmoid(gn.reshape(b, t, NSA_HEADS, 3)),
           kvc.reshape(kv_shape), _rope_kv(kvs.reshape(kv_shape), pos), _rope_kv(kvw.reshape(kv_shape), pos))
    qh = jnp.einsum('btc,chd->bthd', _rmsnorm(qd, q_norm), w_q_up)
    q_pe = _rope(qh[..., MLA_NOPE:], pos, MLA_ROPE_THETA)
    q_lat = jnp.einsum('bthn,chn->bthc', qh[..., :MLA_NOPE], w_kv_up[..., :MLA_NOPE])
    c = _rmsnorm(kvd, kv_norm)
    kr = _rope(krp, pos, MLA_ROPE_THETA)
    return nsa, (q_lat, q_pe, c, kr), (ga, gb, ma, mb)


def _merge_out(x, o_a, o_lat, gates, w_kv_up, w_proj_a, w_proj_b, w_out, norm_post):
    ga, gb, ma, mb = gates
    b, t = x.shape[:2]
    o_b = jnp.einsum('bqhc,chv->bqhv', o_lat, w_kv_up[..., MLA_NOPE:]).reshape(b, t, MLA_WIDTH)
    pa = (o_a * jax.nn.silu(ga)) @ w_proj_a
    pb = (o_b * jax.nn.silu(gb)) @ w_proj_b
    h = jax.nn.sigmoid(ma) * pa + jax.nn.sigmoid(mb) * pb
    return x + _rmsnorm(h @ w_out, norm_post)


def _prompt_layer(x, norm_pre, w_in, pe_cmp, q_norm, w_q_up, kv_norm, w_kv_up,
                  w_proj_a, w_proj_b, w_out, norm_post):
    b, t = x.shape[:2]
    pos = jnp.arange(t)
    nsa, mla, gates = _project_in(x, pos, norm_pre, w_in, q_norm, w_q_up, kv_norm, w_kv_up)
    q, q_rot, g, kvc, kvs, kvw = nsa
    kc, vc = _compress(kvc, pe_cmp)
    kvw_pad = jnp.pad(kvw, ((0, 0), (WINDOW, 0), (0, 0), (0, 0), (0, 0)))
    b_i = jnp.arange(b)[:, None, None, None, None]
    g_i = jnp.arange(NSA_KV_HEADS)[None, None, :, None, None]

    def gather(pos_sel):
        p = jnp.clip(pos_sel, 0, t - 1)
        return kvs[b_i, p, g_i, 0], kvs[b_i, p, g_i, 1]

    def nsa_block(args):
        i, qb, qrb, gbk = args
        s0 = i * NSA_QBLOCK
        tq = s0 + jnp.arange(NSA_QBLOCK)
        kvw_b = lax.dynamic_slice_in_dim(kvw_pad, s0, WINDOW + NSA_QBLOCK, axis=1)
        pos_w = s0 - WINDOW + jnp.arange(WINDOW + NSA_QBLOCK)
        return _nsa_attend(qb, qrb, gbk, tq, kc, vc, gather, kvw_b, pos_w)

    o_a = _from_blocks(lax.map(nsa_block, (jnp.arange(t // NSA_QBLOCK), _to_blocks(q, NSA_QBLOCK),
                                           _to_blocks(q_rot, NSA_QBLOCK), _to_blocks(g, NSA_QBLOCK))))
    q_lat, q_pe, c, kr = mla

    def mla_block(args):
        i, ql, qp = args
        tq = i * MLA_QBLOCK + jnp.arange(MLA_QBLOCK)
        return _mla_attend(ql, qp, tq, c, kr, pos)

    o_lat = _from_blocks(lax.map(mla_block, (jnp.arange(t // MLA_QBLOCK), _to_blocks(q_lat, MLA_QBLOCK),
                                             _to_blocks(q_pe, MLA_QBLOCK))))
    y = _merge_out(x, o_a, o_lat, gates, w_kv_up, w_proj_a, w_proj_b, w_out, norm_post)
    win_keep = min(WINDOW, t)
    return y, kvc, kvs, c, kr, kvw[:, t - win_keep:]


def _sample_layer(x, l, cache_nsa_cmp, cache_nsa_slc, cache_mla_latent, cache_mla_krope, state_win,
                  page_table, norm_pre, w_in, pe_cmp, q_norm, w_q_up, kv_norm, w_kv_up,
                  w_proj_a, w_proj_b, w_out, norm_post):
    b, s_new = x.shape[:2]
    n_pages = page_table.shape[1]
    page = cache_nsa_cmp.shape[2]
    past = n_pages * page
    pos = past + jnp.arange(s_new)
    nsa, mla, gates = _project_in(x, pos, norm_pre, w_in, q_norm, w_q_up, kv_norm, w_kv_up)
    q, q_rot, g, kvc, kvs, kvw = nsa
    cmp_past = cache_nsa_cmp[l, page_table].reshape(b, past, NSA_KV_HEADS, 2, NSA_HEAD_DIM)
    kc, vc = _compress(jnp.concatenate([cmp_past, kvc], axis=1), pe_cmp)
    b_i = jnp.arange(b)[:, None, None, None, None]
    g_i = jnp.arange(NSA_KV_HEADS)[None, None, :, None, None]

    def gather(pos_sel):
        in_past = (pos_sel < past)[..., None]
        phys = page_table[b_i, jnp.clip(pos_sel // page, 0, n_pages - 1)]
        off = pos_sel % page
        p_new = jnp.clip(pos_sel - past, 0, s_new - 1)
        k = jnp.where(in_past, cache_nsa_slc[l, phys, off, g_i, 0], kvs[b_i, p_new, g_i, 0])
        v = jnp.where(in_past, cache_nsa_slc[l, phys, off, g_i, 1], kvs[b_i, p_new, g_i, 1])
        return k, v

    win_len = state_win.shape[1]
    kvw_all = jnp.concatenate([state_win, kvw], axis=1)
    pos_w = past - win_len + jnp.arange(win_len + s_new)
    o_a = _nsa_attend(q, q_rot, g, pos, kc, vc, gather, kvw_all, pos_w)
    q_lat, q_pe, c, kr = mla
    c_all = jnp.concatenate([cache_mla_latent[l, page_table].reshape(b, past, MLA_KV_LORA), c], axis=1)
    kr_all = jnp.concatenate([cache_mla_krope[l, page_table].reshape(b, past, MLA_ROPE), kr], axis=1)
    o_lat = _mla_attend(q_lat, q_pe, pos, c_all, kr_all, jnp.arange(past + s_new))
    y = _merge_out(x, o_a, o_lat, gates, w_kv_up, w_proj_a, w_proj_b, w_out, norm_post)
    return y, kvc, kvs, c, kr, kvw_all[:, s_new:]


def setup_inputs(seed: int = 0) -> dict:
    key = jax.random.key(seed)
    ks = jax.random.split(key, 24)
    n_pages = PAST_LEN // PAGE_SIZE
    n_used = DEC_BATCH * n_pages
    n_phys = n_used + max(1, n_used // 4)
    win_len = min(WINDOW, PAST_LEN)

    def nrm(k, shape, scale):
        return scale * jax.random.normal(k, shape, jnp.float32)

    kvrow = (NSA_KV_HEADS, 2, NSA_HEAD_DIM)
    return {
        "x_prompt": nrm(ks[0], (BATCH, SEQ, D_MODEL), 1.0),
        "x_sample": nrm(ks[1], (DEC_BATCH, DEC_SEQ, D_MODEL), 1.0),
        "cache_nsa_cmp": nrm(ks[2], (DEPTH, n_phys, PAGE_SIZE) + kvrow, 1.0),
        "cache_nsa_slc": nrm(ks[3], (DEPTH, n_phys, PAGE_SIZE) + kvrow, 1.0),
        "cache_mla_latent": nrm(ks[4], (DEPTH, n_phys, PAGE_SIZE, MLA_KV_LORA), 1.0),
        "cache_mla_krope": nrm(ks[5], (DEPTH, n_phys, PAGE_SIZE, MLA_ROPE), 1.0),
        "state_nsa_win": nrm(ks[6], (DEPTH, DEC_BATCH, win_len) + kvrow, 1.0),
        "page_table": jax.random.permutation(ks[7], n_phys)[:n_used].reshape(DEC_BATCH, n_pages).astype(jnp.int32),
        "norm_pre": 1.0 + nrm(ks[8], (DEPTH, D_MODEL), 0.05),
        "w_in": nrm(ks[9], (DEPTH, D_MODEL, IN_WIDTH), D_MODEL ** -0.5),
        "pe_cmp": nrm(ks[10], (DEPTH, CMP_BLOCK) + kvrow, 0.1),
        "q_norm": 1.0 + nrm(ks[11], (DEPTH, MLA_Q_LORA), 0.05),
        "w_q_up": nrm(ks[12], (DEPTH, MLA_Q_LORA, MLA_HEADS, MLA_NOPE + MLA_ROPE), MLA_Q_LORA ** -0.5),
        "kv_norm": 1.0 + nrm(ks[13], (DEPTH, MLA_KV_LORA), 0.05),
        "w_kv_up": nrm(ks[14], (DEPTH, MLA_KV_LORA, MLA_HEADS, MLA_NOPE + MLA_V), MLA_KV_LORA ** -0.5),
        "w_proj_a": nrm(ks[15], (DEPTH, NSA_WIDTH, D_MODEL), NSA_WIDTH ** -0.5),
        "w_proj_b": nrm(ks[16], (DEPTH, MLA_WIDTH, D_MODEL), MLA_WIDTH ** -0.5),
        "w_out": nrm(ks[17], (DEPTH, D_MODEL, D_MODEL), D_MODEL ** -0.5),
        "norm_post": 1.0 + nrm(ks[18], (DEPTH, D_MODEL), 0.05),
    }


def reference(x_prompt, x_sample, cache_nsa_cmp, cache_nsa_slc, cache_mla_latent, cache_mla_krope,
              state_nsa_win, page_table, norm_pre, w_in, pe_cmp, q_norm, w_q_up, kv_norm, w_kv_up,
              w_proj_a, w_proj_b, w_out, norm_post):
    hp, hs = x_prompt, x_sample
    new_p, new_s = [], []
    for l in range(DEPTH):
        w = (norm_pre[l], w_in[l], pe_cmp[l], q_norm[l], w_q_up[l], kv_norm[l], w_kv_up[l],
             w_proj_a[l], w_proj_b[l], w_out[l], norm_post[l])
        hp, *sp = _prompt_layer(hp, *w)
        hs, *ss = _sample_layer(hs, l, cache_nsa_cmp, cache_nsa_slc, cache_mla_latent, cache_mla_krope,
                                state_nsa_win[l], page_table, *w)
        new_p.append(sp)
        new_s.append(ss)
    new_cmp_p = jnp.stack([s[0] for s in new_p])
    new_slc_p = jnp.stack([s[1] for s in new_p])
    new_lat_p = jnp.stack([s[2] for s in new_p])
    new_krope_p = jnp.stack([s[3] for s in new_p])
    new_win_p = jnp.stack([s[4] for s in new_p])
    new_cmp_s = jnp.stack([s[0] for s in new_s])
    new_slc_s = jnp.stack([s[1] for s in new_s])
    new_lat_s = jnp.stack([s[2] for s in new_s])
    new_krope_s = jnp.stack([s[3] for s in new_s])
    new_win_s = jnp.stack([s[4] for s in new_s])
    return (hp, hs, new_cmp_p, new_slc_p, new_lat_p, new_krope_p, new_win_p,
            new_cmp_s, new_slc_s, new_lat_s, new_krope_s, new_win_s)
```

```python
import functools

import numpy as np
import jax
import jax.numpy as jnp
from jax import lax
from jax.experimental import pallas as pl
from jax.experimental.pallas import tpu as pltpu

D_MODEL = 1024
NSA_HEADS = 8
NSA_KV_HEADS = 2
NSA_GROUP = NSA_HEADS // NSA_KV_HEADS
HD = 64
NSA_WIDTH = NSA_HEADS * HD
KV_WIDTH = NSA_KV_HEADS * 2 * HD
ROT_DIM = HD // 4
ROPE_THETA = 500000.0
CMP_BLOCK = 32
SEL_BLOCK = 64
N_SELECT = 16
WINDOW = 512
NSA_SCALE = HD ** -0.5

MLA_HEADS = 8
MLA_Q_LORA = 384
MLA_KV_LORA = 256
MLA_NOPE = 64
MLA_ROPE = 32
MLA_V = 64
MLA_WIDTH = MLA_HEADS * MLA_V
MLA_ROPE_THETA = 10000.0
MLA_SCALE = (MLA_NOPE + MLA_ROPE) ** -0.5

RMS_EPS = 1e-6
NEG = -1e30
FORCE_SCORE = 1e4

IN_SPLITS = (NSA_WIDTH, KV_WIDTH, KV_WIDTH, KV_WIDTH, 3 * NSA_HEADS, NSA_WIDTH,
             MLA_Q_LORA, MLA_KV_LORA, MLA_ROPE, MLA_WIDTH, D_MODEL, D_MODEL)
IN_OFFSETS = tuple(int(v) for v in np.cumsum((0,) + IN_SPLITS))

LANES = 128
HALF = LANES // 2

SEG_A = (0, 1280)
SEG_GN = (1280, 1536)
SEG_GA = (1536, 2048)
SEG_QD = (2048, 2432)
SEG_KVD = (2432, 2688)
SEG_KRP = (2688, 2816)
SEG_GB = (2816, 3328)
SEG_MA = (3328, 4352)
SEG_MB = (4352, 5376)
PACKED_WIDTH = 5376

VMEM_LIMIT = 48 * 1024 * 1024
BF16 = jnp.bfloat16
F32 = jnp.float32


def _full_spec(shape):
    nd = len(shape)
    return pl.BlockSpec(shape, lambda *_: (0,) * nd)


def _lane_lo(rows):
    return lax.broadcasted_iota(jnp.int32, (rows, LANES), 1) < HALF


def _dot(a, b):
    return jnp.dot(a, b, preferred_element_type=F32)


def _dot_nt(a, b):
    return lax.dot_general(a, b, (((1,), (1,)), ((), ())), preferred_element_type=F32)


def _rope_tiles(v, tab_ref, shift):
    c, s_lo, s_hi = tab_ref[0], tab_ref[1], tab_ref[2]
    out = []
    for k in range(v.shape[1] // LANES):
        a = v[:, k * LANES:(k + 1) * LANES]
        out.append(a * c + pltpu.roll(a, LANES - shift, 1) * s_lo + pltpu.roll(a, shift, 1) * s_hi)
    return out[0] if len(out) == 1 else jnp.concatenate(out, axis=1)


def _dup_kv(a, lo):
    r = pltpu.roll(a, HALF, 1)
    return jnp.where(lo, a, r), jnp.where(lo, r, a)


def _kv_pack(v):
    lo = _lane_lo(v.shape[0])
    parts = []
    for g in range(NSA_KV_HEADS):
        kk, vv = _dup_kv(v[:, g * LANES:(g + 1) * LANES], lo)
        parts += [kk, vv]
    return jnp.concatenate(parts, axis=1).astype(BF16)


def _rms(v, gain):
    return v * lax.rsqrt(jnp.mean(v * v, axis=-1, keepdims=True) + RMS_EPS) * gain


def _stack_heads(qv, lo):
    a, b = qv[:, :LANES], qv[:, LANES:]
    z = jnp.zeros_like(a)
    return jnp.concatenate([jnp.where(lo, a, z), jnp.where(lo, z, a),
                            jnp.where(lo, b, z), jnp.where(lo, z, b)], axis=0)


def _inproj_kernel(with_compress, with_qlat, x_ref, npre_ref, w_ref, tq_ref, tkv_ref, tm_ref, pe_ref,
                   qnorm_ref, wqup_ref, kvnorm_ref, w2uk_ref, *outs):
    (q_ref, qrot_ref, gate_ref, cmp_ref, slc_ref, win_ref, slckv_ref, winkv_ref,
     ga_ref, gb_ref, ma_ref, mb_ref, qn_ref, qpe_ref, c_ref, cb_ref, kr_ref, kr4_ref) = outs[:18]
    extra = list(outs[18:])
    x = x_ref[0]
    xb = _rms(x, npre_ref[...]).astype(BF16)

    def seg(lohi):
        return _dot(xb, w_ref[:, lohi[0]:lohi[1]])

    a = seg(SEG_A)
    q = a[:, :NSA_WIDTH]
    q_ref[0] = (q * NSA_SCALE).astype(BF16)
    qrot_ref[0] = (_rope_tiles(q, tq_ref, ROT_DIM // 2) * NSA_SCALE).astype(BF16)
    kvc = a[:, NSA_WIDTH:NSA_WIDTH + KV_WIDTH]
    cmp_ref[0] = kvc
    kvs = _rope_tiles(a[:, NSA_WIDTH + KV_WIDTH:NSA_WIDTH + 2 * KV_WIDTH], tkv_ref, ROT_DIM // 2)
    slc_ref[0] = kvs
    slckv_ref[0] = _kv_pack(kvs)
    kvw = _rope_tiles(a[:, NSA_WIDTH + 2 * KV_WIDTH:], tkv_ref, ROT_DIM // 2)
    win_ref[0] = kvw
    winkv_ref[0] = _kv_pack(kvw)

    gate_ref[0] = jax.nn.sigmoid(seg(SEG_GN))
    ga_ref[0] = seg(SEG_GA)
    gb_ref[0] = seg(SEG_GB)
    ma_ref[0] = seg(SEG_MA)
    mb_ref[0] = seg(SEG_MB)

    qd = _rms(seg(SEG_QD), qnorm_ref[...]).astype(BF16)
    qh = _dot(qd, wqup_ref[...])
    qn = qh[:, :MLA_HEADS * MLA_NOPE].astype(BF16)
    qn_ref[0] = qn
    qpe_ref[0] = (_rope_tiles(qh[:, MLA_HEADS * MLA_NOPE:], tm_ref, MLA_ROPE // 2) * MLA_SCALE).astype(BF16)
    c = _rms(seg(SEG_KVD), kvnorm_ref[...])
    c_ref[0] = c
    cb_ref[0] = c.astype(BF16)
    kr = _rope_tiles(seg(SEG_KRP), tm_ref, MLA_ROPE // 2)
    kr_ref[0] = kr[:, :MLA_ROPE]
    kr4_ref[0] = kr.astype(BF16)

    if with_compress:
        kc_ref = extra.pop(0)
        rows = kvc.shape[0]
        blocks = kvc.reshape(rows // CMP_BLOCK, CMP_BLOCK, KV_WIDTH) + pe_ref[...][None]
        kc_ref[0] = jnp.sum(blocks, axis=1) * (1.0 / CMP_BLOCK)
    if with_qlat:
        qlat_ref = extra.pop(0)
        qlat_ref[0] = _mla_qlat(qn, w2uk_ref)


def _mla_qlat(qn, w2uk_ref):
    rows = qn.shape[0]
    lo = _lane_lo(rows)
    z = jnp.zeros((rows, LANES), BF16)
    parts = []
    for j in range(MLA_HEADS // 2):
        a = qn[:, j * LANES:(j + 1) * LANES]
        parts.append(_dot(jnp.where(lo, a, z), w2uk_ref[j]) * MLA_SCALE)
        parts.append(_dot(jnp.where(lo, z, a), w2uk_ref[j]) * MLA_SCALE)
    return jnp.concatenate(parts, axis=1)


def _in_project(x3, tabs, wts, tm, with_compress, with_qlat):
    b, t, _ = x3.shape
    nt = t // tm
    tq, tkv, tmla = tabs
    row = lambda w, dt: jax.ShapeDtypeStruct((b, t, w), dt)
    out_shape = [row(512, BF16), row(512, BF16), row(256, F32), row(256, F32), row(256, F32), row(256, F32),
                 row(512, BF16), row(512, BF16), row(512, F32), row(512, F32), row(1024, F32), row(1024, F32),
                 row(512, BF16), row(256, BF16), row(256, F32), row(256, BF16), row(MLA_ROPE, F32), row(128, BF16)]
    names = ["q", "qrot", "gate", "cmp", "slc", "win", "slckv", "winkv", "ga", "gb", "ma", "mb",
             "qn", "qpe", "c", "cb", "kr", "kr4"]
    tok = lambda w: pl.BlockSpec((1, tm, w), lambda i, bb: (bb, i, 0))
    out_specs = [tok(s.shape[2]) for s in out_shape]
    if with_compress:
        out_shape.append(jax.ShapeDtypeStruct((b, t // CMP_BLOCK, KV_WIDTH), F32))
        out_specs.append(pl.BlockSpec((1, tm // CMP_BLOCK, KV_WIDTH), lambda i, bb: (bb, i, 0)))
        names.append("kc")
    if with_qlat:
        out_shape.append(row(MLA_HEADS * MLA_KV_LORA, F32))
        out_specs.append(tok(MLA_HEADS * MLA_KV_LORA))
        names.append("qlat")
    tab = pl.BlockSpec((3, tm, LANES), lambda i, bb: (0, i, 0))
    in_specs = [tok(D_MODEL), _full_spec((1, D_MODEL)), _full_spec((D_MODEL, PACKED_WIDTH)), tab, tab, tab,
                _full_spec((CMP_BLOCK, KV_WIDTH)), _full_spec((1, MLA_Q_LORA)),
                _full_spec((MLA_Q_LORA, MLA_HEADS * (MLA_NOPE + MLA_ROPE))), _full_spec((1, MLA_KV_LORA)),
                _full_spec((MLA_HEADS // 2, LANES, MLA_KV_LORA))]
    res = pl.pallas_call(
        functools.partial(_inproj_kernel, with_compress, with_qlat),
        grid=(nt, b), in_specs=in_specs, out_specs=out_specs, out_shape=out_shape,
        compiler_params=pltpu.CompilerParams(dimension_semantics=("arbitrary", "arbitrary"),
                                             vmem_limit_bytes=VMEM_LIMIT),
        name="in_project",
    )(x3, wts["npre"], wts["w_in"], tq, tkv, tmla, wts["pe"], wts["qnorm"], wts["wqup"], wts["kvnorm"],
      wts["w2uk"])
    return dict(zip(names, res))


NSA_TQ = 128
NSA_TK = 256


def _softmax_update(s, v_b, m, l, acc):
    m_new = jnp.maximum(m, jnp.max(s, axis=-1, keepdims=True))
    p = jnp.exp(s - m_new)
    alpha = jnp.exp(m - m_new)
    l_new = alpha * l + jnp.sum(p, axis=-1, keepdims=True)
    acc_new = alpha * acc + _dot(p.astype(BF16), v_b)
    return m_new, l_new, acc_new


def _nsa_prompt_kernel(n_cmp, q_ref, qr_ref, gate_ref, kc_ref, ks_ref, kw_ref, e_ref, o_ref):
    i = pl.program_id(2)
    tq, tk = NSA_TQ, NSA_TK
    lo = _lane_lo(tq)
    qs = _stack_heads(q_ref[0], lo)
    qr = _stack_heads(qr_ref[0], lo)

    kk_c, vv_c = _dup_kv(kc_ref[0], _lane_lo(n_cmp))
    pad = jnp.zeros((LANES - n_cmp, LANES), F32)
    kk_c = jnp.concatenate([kk_c, pad], axis=0).astype(BF16)
    vv_c = jnp.concatenate([vv_c, pad], axis=0).astype(BF16)
    s_c = _dot_nt(qs, kk_c)
    row4 = lax.broadcasted_iota(jnp.int32, (4 * tq, LANES), 0)
    lane4 = lax.broadcasted_iota(jnp.int32, (4 * tq, LANES), 1)
    t4 = i * tq + (row4 & (tq - 1))
    mask_c = (lane4 * CMP_BLOCK + (CMP_BLOCK - 1) <= t4) & (lane4 < n_cmp)
    s_c = jnp.where(mask_c, s_c, NEG)
    e_c = jnp.where(mask_c, jnp.exp(s_c - jnp.max(s_c, axis=-1, keepdims=True)), 0.0)
    den_c = jnp.sum(e_c, axis=-1, keepdims=True)
    p_c = e_c / jnp.maximum(den_c, 1e-30)
    o_c = _dot(p_c.astype(BF16), vv_c)

    imp = p_c[0:tq] + p_c[tq:2 * tq] + p_c[2 * tq:3 * tq] + p_c[3 * tq:4 * tq]
    lane = lax.broadcasted_iota(jnp.int32, (tq, LANES), 1)
    row = lax.broadcasted_iota(jnp.int32, (tq, LANES), 0)
    t_q = i * tq + row
    even = (lane & 1) == 0
    pair = imp + jnp.where(even, pltpu.roll(imp, LANES - 1, 1), pltpu.roll(imp, 1, 1))
    blk = lane >> 1
    n_blk = n_cmp // 2
    valid = (blk * SEL_BLOCK <= t_q) & (blk < n_blk)
    forced = ((blk == t_q // SEL_BLOCK) | (blk == 0)) & (blk < n_blk)
    score = jnp.where(forced, FORCE_SCORE, jnp.where(valid, pair, -FORCE_SCORE))
    rank = jnp.zeros((tq, LANES), jnp.int32)
    for j in range(n_blk):
        col = score[:, 2 * j:2 * j + 1]
        ahead = (col > score) | ((col == score) & (j < blk))
        rank = rank + jnp.where(ahead, 1, 0)
    sel = (rank < N_SELECT) & (score > -1.0) & (blk < n_blk)
    sel_b = jnp.where(sel, 1.0, 0.0).astype(BF16)

    colk = lax.broadcasted_iota(jnp.int32, (tq, tk), 1)
    rowk = i * tq + lax.broadcasted_iota(jnp.int32, (tq, tk), 0)
    n_chunks = ((i + 1) * tq + tk - 1) // tk

    def init():
        return (jnp.full((4 * tq, 1), NEG, F32), jnp.zeros((4 * tq, 1), F32), jnp.zeros((4 * tq, LANES), F32))

    def sel_body(c, carry):
        base = pl.multiple_of(c * tk, tk)
        k_b = ks_ref[0, pl.ds(base, tk), 0:LANES]
        v_b = ks_ref[0, pl.ds(base, tk), LANES:2 * LANES]
        keypos = base + colk
        picked = _dot(sel_b, e_ref[c]) > 0.5
        bias = jnp.where(picked & (keypos <= rowk), 0.0, NEG)
        s = _dot_nt(qr, k_b) + jnp.concatenate([bias] * 4, axis=0)
        return _softmax_update(s, v_b, *carry)

    m_s, l_s, a_s = lax.fori_loop(0, n_chunks, sel_body, init())
    o_s = a_s / l_s

    def win_body(c, carry):
        base = pl.multiple_of(c * tk, tk)
        k_b = kw_ref[0, pl.ds(base, tk), 0:LANES]
        v_b = kw_ref[0, pl.ds(base, tk), LANES:2 * LANES]
        dist = rowk - (base + colk)
        bias = jnp.where((dist >= 0) & (dist <= WINDOW), 0.0, NEG)
        s = _dot_nt(qr, k_b) + jnp.concatenate([bias] * 4, axis=0)
        return _softmax_update(s, v_b, *carry)

    c_lo = jnp.maximum(i * tq - WINDOW, 0) // tk
    m_w, l_w, a_w = lax.fori_loop(c_lo, n_chunks, win_body, init())
    o_w = a_w / l_w

    gate = gate_ref[0]
    heads = []
    for r in range(NSA_GROUP):
        sl = slice(r * tq, (r + 1) * tq)
        heads.append(gate[:, 3 * r:3 * r + 1] * o_c[sl] + gate[:, 3 * r + 1:3 * r + 2] * o_s[sl]
                     + gate[:, 3 * r + 2:3 * r + 3] * o_w[sl])
    o_ref[0] = jnp.concatenate([jnp.where(lo, heads[0], heads[1]), jnp.where(lo, heads[2], heads[3])], axis=1)


def _sel_expand(t, tk):
    e = np.zeros((t // tk, LANES, tk), np.float32)
    for s in range(t):
        e[s // tk, 2 * (s // SEL_BLOCK), s % tk] = 1.0
    return jnp.asarray(e, BF16)


def _nsa_prompt(p, b, t):
    tq = NSA_TQ
    n_cmp = t // CMP_BLOCK
    grp = lambda w: pl.BlockSpec((1, tq, w), lambda bb, g, i: (bb, i, g))
    kv = pl.BlockSpec((1, t, 2 * LANES), lambda bb, g, i: (bb, 0, g))
    return pl.pallas_call(
        functools.partial(_nsa_prompt_kernel, n_cmp),
        grid=(b, NSA_KV_HEADS, t // tq),
        in_specs=[grp(2 * LANES), grp(2 * LANES), grp(LANES),
                  pl.BlockSpec((1, n_cmp, LANES), lambda bb, g, i: (bb, 0, g)), kv, kv,
                  _full_spec((t // NSA_TK, LANES, NSA_TK))],
        out_specs=grp(2 * LANES),
        out_shape=jax.ShapeDtypeStruct((b, t, NSA_WIDTH), F32),
        compiler_params=pltpu.CompilerParams(dimension_semantics=("arbitrary",) * 3, vmem_limit_bytes=VMEM_LIMIT),
        name="nsa_prompt",
    )(p["q"], p["qrot"], p["gate"], p["kc"], p["slckv"], p["winkv"], _sel_expand(t, NSA_TK))


MLA_TQ = 128
MLA_TK = 256


def _uv_project(o_lat_heads, wuv_ref):
    parts = []
    for j in range(MLA_HEADS // 2):
        parts.append(_dot(o_lat_heads[2 * j], wuv_ref[2 * j]) + _dot(o_lat_heads[2 * j + 1], wuv_ref[2 * j + 1]))
    return jnp.concatenate(parts, axis=1)


def _mla_prompt_kernel(qn_ref, qpe_ref, c_ref, kr_ref, w2uk_ref, wuv_ref, o_ref, m_ref, l_ref, acc_ref):
    i = pl.program_id(1)
    tq, tk = MLA_TQ, MLA_TK
    nh = MLA_HEADS
    lo = _lane_lo(tq)
    qn = qn_ref[0]
    zb = jnp.zeros((tq, LANES), BF16)
    rows = []
    for j in range(nh // 2):
        a = qn[:, j * LANES:(j + 1) * LANES]
        st = jnp.concatenate([jnp.where(lo, a, zb), jnp.where(lo, zb, a)], axis=0)
        rows.append(_dot(st, w2uk_ref[j]) * MLA_SCALE)
    qlat = jnp.concatenate(rows, axis=0).astype(BF16)
    qpe = qpe_ref[0]
    quarter = lax.broadcasted_iota(jnp.int32, (tq, LANES), 1) // MLA_ROPE
    pes = []
    for h in range(nh):
        a = qpe[:, (h // 4) * LANES:(h // 4 + 1) * LANES]
        pes.append(jnp.where(quarter == (h % 4), a, zb))
    qpes = jnp.concatenate(pes, axis=0)

    m_ref[...] = jnp.full(m_ref.shape, NEG, F32)
    l_ref[...] = jnp.zeros(l_ref.shape, F32)
    acc_ref[...] = jnp.zeros(acc_ref.shape, F32)
    colk = lax.broadcasted_iota(jnp.int32, (nh * tq, tk), 1)
    rowk = i * tq + (lax.broadcasted_iota(jnp.int32, (nh * tq, tk), 0) & (tq - 1))
    n_chunks = ((i + 1) * tq + tk - 1) // tk

    def body(c, _):
        base = pl.multiple_of(c * tk, tk)
        c_b = c_ref[0, pl.ds(base, tk), :]
        s = _dot_nt(qlat, c_b) + _dot_nt(qpes, kr_ref[0, pl.ds(base, tk), :])
        s = jnp.where(base + colk <= rowk, s, NEG)
        m, l, acc = _softmax_update(s, c_b, m_ref[...], l_ref[...], acc_ref[...])
        m_ref[...] = m
        l_ref[...] = l
        acc_ref[...] = acc
        return 0

    lax.fori_loop(0, n_chunks, body, 0)
    o_lat = (acc_ref[...] / l_ref[...]).astype(BF16)
    o_ref[0] = _uv_project([o_lat[h * tq:(h + 1) * tq] for h in range(nh)], wuv_ref)


def _mla_prompt(p, wts, b, t):
    tq = MLA_TQ
    tok = lambda w: pl.BlockSpec((1, tq, w), lambda bb, i: (bb, i, 0))
    seq = lambda w: pl.BlockSpec((1, t, w), lambda bb, i: (bb, 0, 0))
    return pl.pallas_call(
        _mla_prompt_kernel,
        grid=(b, t // tq),
        in_specs=[tok(512), tok(256), seq(MLA_KV_LORA), seq(LANES),
                  _full_spec((MLA_HEADS // 2, LANES, MLA_KV_LORA)), _full_spec((MLA_HEADS, MLA_KV_LORA, LANES))],
        out_specs=tok(MLA_WIDTH),
        out_shape=jax.ShapeDtypeStruct((b, t, MLA_WIDTH), F32),
        scratch_shapes=[pltpu.VMEM((MLA_HEADS * tq, 1), F32), pltpu.VMEM((MLA_HEADS * tq, 1), F32),
                        pltpu.VMEM((MLA_HEADS * tq, MLA_KV_LORA), F32)],
        compiler_params=pltpu.CompilerParams(dimension_semantics=("arbitrary",) * 2, vmem_limit_bytes=VMEM_LIMIT),
        name="mla_prompt",
    )(p["qn"], p["qpe"], p["cb"], p["kr4"], wts["w2uk"], wts["wuv"])


def _merge_kernel(from_latent, x_ref, oa_ref, ob_ref, ga_ref, gb_ref, ma_ref, mb_ref, wpa_ref, wpb_ref, wout_ref,
                  npost_ref, wuv_ref, y_ref):
    if from_latent:
        lat = ob_ref[0].astype(BF16)
        o_b = _uv_project([lat[:, h * MLA_KV_LORA:(h + 1) * MLA_KV_LORA] for h in range(MLA_HEADS)], wuv_ref)
    else:
        o_b = ob_ref[0]
    ga = ga_ref[0]
    gb = gb_ref[0]
    pa = _dot((oa_ref[0] * (ga * jax.nn.sigmoid(ga))).astype(BF16), wpa_ref[...])
    pb = _dot((o_b * (gb * jax.nn.sigmoid(gb))).astype(BF16), wpb_ref[...])
    h = jax.nn.sigmoid(ma_ref[0]) * pa + jax.nn.sigmoid(mb_ref[0]) * pb
    z = _dot(h.astype(BF16), wout_ref[...])
    y_ref[0] = x_ref[0] + _rms(z, npost_ref[...])


def _merge(x3, o_a, o_b, p, wts, tm, from_latent):
    b, t, _ = x3.shape
    tok = lambda w: pl.BlockSpec((1, tm, w), lambda bb, i: (bb, i, 0))
    return pl.pallas_call(
        functools.partial(_merge_kernel, from_latent),
        grid=(b, t // tm),
        in_specs=[tok(D_MODEL), tok(NSA_WIDTH), tok(o_b.shape[2]), tok(NSA_WIDTH), tok(MLA_WIDTH), tok(D_MODEL),
                  tok(D_MODEL), _full_spec((NSA_WIDTH, D_MODEL)), _full_spec((MLA_WIDTH, D_MODEL)),
                  _full_spec((D_MODEL, D_MODEL)), _full_spec((1, D_MODEL)),
                  _full_spec((MLA_HEADS, MLA_KV_LORA, LANES))],
        out_specs=tok(D_MODEL),
        out_shape=jax.ShapeDtypeStruct((b, t, D_MODEL), F32),
        compiler_params=pltpu.CompilerParams(dimension_semantics=("arbitrary",) * 2, vmem_limit_bytes=VMEM_LIMIT),
        name="merge",
    )(x3, o_a, o_b, p["ga"], p["gb"], p["ma"], p["mb"], wts["wpa"], wts["wpb"], wts["wout"], wts["npost"],
      wts["wuv"])


PAGES_PER_STEP = 8
PAD_ROWS = 16


def _compress_pages_kernel(pt_ref, *refs):
    del pt_ref
    pages = refs[:PAGES_PER_STEP]
    pe_ref, o_ref = refs[PAGES_PER_STEP], refs[PAGES_PER_STEP + 1]
    pe_sum = jnp.sum(pe_ref[...], axis=0, keepdims=True)
    outs = []
    for pg in pages:
        v = pg[0]
        blocks = v.reshape(v.shape[0] // CMP_BLOCK, CMP_BLOCK, KV_WIDTH)
        outs.append((jnp.sum(blocks, axis=1) + pe_sum) * (1.0 / CMP_BLOCK))
    o_ref[0] = jnp.concatenate(outs, axis=0)


def _page_specs(n_pages, page, width, extra_grid=False):
    specs = []
    for k in range(PAGES_PER_STEP):
        specs.append(pl.BlockSpec(
            (1, page, width), lambda bb, c, pt, k=k: (pt[bb * n_pages + c * PAGES_PER_STEP + k], 0, 0)))
    return specs


def _compress_pages(cache, pt_flat, pe, b, n_pages):
    n_phys, page = cache.shape[0], cache.shape[1]
    per_step = PAGES_PER_STEP * page // CMP_BLOCK
    gs = pltpu.PrefetchScalarGridSpec(
        num_scalar_prefetch=1, grid=(b, n_pages // PAGES_PER_STEP),
        in_specs=_page_specs(n_pages, page, KV_WIDTH) + [pl.BlockSpec((CMP_BLOCK, KV_WIDTH), lambda bb, c, pt: (0, 0))],
        out_specs=pl.BlockSpec((1, per_step, KV_WIDTH), lambda bb, c, pt: (bb, c, 0)))
    return pl.pallas_call(
        _compress_pages_kernel, grid_spec=gs,
        out_shape=jax.ShapeDtypeStruct((b, n_pages * page // CMP_BLOCK, KV_WIDTH), F32),
        compiler_params=pltpu.CompilerParams(dimension_semantics=("arbitrary",) * 2, vmem_limit_bytes=VMEM_LIMIT),
        name="compress_pages",
    )(pt_flat, *([cache] * PAGES_PER_STEP), pe)


def _pad_rows(v, rows):
    return jnp.concatenate([v, jnp.zeros((rows - v.shape[0], v.shape[1]), v.dtype)], axis=0)


def _sample_select_kernel(n_cmp, q_ref, kc_ref, oc_ref, idx_ref):
    lo1 = _lane_lo(1)
    lo_c = _lane_lo(n_cmp)
    q = q_ref[0]
    kc = kc_ref[0]
    lane = lax.broadcasted_iota(jnp.int32, (1, n_cmp), 1)
    blk = lane >> 1
    slot = lax.broadcasted_iota(jnp.int32, (1, N_SELECT), 1)
    for g in range(NSA_KV_HEADS):
        kk, vv = _dup_kv(kc[:, g * LANES:(g + 1) * LANES], lo_c)
        qs = _pad_rows(_stack_heads(q[:, g * 2 * LANES:(g + 1) * 2 * LANES], lo1), PAD_ROWS)
        s = _dot_nt(qs, kk.astype(BF16))
        e = jnp.exp(s - jnp.max(s, axis=-1, keepdims=True))
        p = e / jnp.sum(e, axis=-1, keepdims=True)
        oc_ref[0, g] = _dot(p.astype(BF16), vv.astype(BF16))
        imp = p[0:1] + p[1:2] + p[2:3] + p[3:4]
        chunks = []
        for k in range(n_cmp // LANES):
            a = imp[:, k * LANES:(k + 1) * LANES]
            ev = (lax.broadcasted_iota(jnp.int32, (1, LANES), 1) & 1) == 0
            chunks.append(a + jnp.where(ev, pltpu.roll(a, LANES - 1, 1), pltpu.roll(a, 1, 1)))
        pair = jnp.concatenate(chunks, axis=1)
        v = jnp.where(blk == 0, -1.0, pair)
        idx = jnp.where(slot == N_SELECT - 1, n_cmp // 2, 0)
        for k in range(1, N_SELECT - 1):
            top = jnp.max(v, axis=-1, keepdims=True)
            jmin = jnp.min(jnp.where(v == top, blk.astype(F32), float(n_cmp)), axis=-1,
                           keepdims=True).astype(jnp.int32)
            idx = jnp.where(slot == k, jmin, idx)
            v = jnp.where(blk == jmin, -1.0, v)
        idx_ref[0, g:g + 1, :] = idx


def _sample_select(q, kc_all, b):
    n_cmp = kc_all.shape[1]
    return pl.pallas_call(
        functools.partial(_sample_select_kernel, n_cmp),
        grid=(b,),
        in_specs=[pl.BlockSpec((1, 1, NSA_WIDTH), lambda bb: (bb, 0, 0)),
                  pl.BlockSpec((1, n_cmp, KV_WIDTH), lambda bb: (bb, 0, 0))],
        out_specs=[pl.BlockSpec((1, NSA_KV_HEADS, PAD_ROWS, LANES), lambda bb: (bb, 0, 0, 0)),
                   pl.BlockSpec((1, NSA_KV_HEADS, N_SELECT), lambda bb: (bb, 0, 0))],
        out_shape=[jax.ShapeDtypeStruct((b, NSA_KV_HEADS, PAD_ROWS, LANES), F32),
                   jax.ShapeDtypeStruct((b, NSA_KV_HEADS, N_SELECT), jnp.int32)],
        compiler_params=pltpu.CompilerParams(dimension_semantics=("arbitrary",), vmem_limit_bytes=VMEM_LIMIT),
        name="sample_select",
    )(q, kc_all)


def _one_key_softmax(s_past, v_past_b, s_new, v_new):
    m = jnp.maximum(jnp.max(s_past, axis=-1, keepdims=True), s_new)
    e = jnp.exp(s_past - m)
    e_new = jnp.exp(s_new - m)
    den = jnp.sum(e, axis=-1, keepdims=True) + e_new
    return (_dot(e.astype(BF16), v_past_b) + e_new * v_new) / den


def _sample_attend_kernel(idx_ref, pt_ref, *refs):
    del idx_ref, pt_ref
    n_sel = N_SELECT
    blocks = refs[:n_sel]
    qr_ref, newkv_ref, neww_ref, newwf_ref, win_ref, gate_ref, oc_ref, o_ref, wout_ref = refs[n_sel:]
    lo1 = _lane_lo(1)
    qs = _pad_rows(_stack_heads(qr_ref[0], lo1), PAD_ROWS)
    qs_f = qs.astype(F32)

    sel = jnp.concatenate([blk[0] for blk in blocks], axis=0)
    kk, vv = _dup_kv(sel, _lane_lo(sel.shape[0]))
    s = _dot_nt(qs, kk.astype(BF16))
    pos = lax.broadcasted_iota(jnp.int32, s.shape, 1)
    s = jnp.where(pos < (n_sel - 1) * SEL_BLOCK, s, NEG)
    nk = newkv_ref[0].astype(F32)
    s_new = jnp.sum(qs_f * nk[:, :LANES], axis=-1, keepdims=True)
    o_s = _one_key_softmax(s, vv.astype(BF16), s_new, nk[:, LANES:])

    w = win_ref[0]
    kkw, vvw = _dup_kv(w, _lane_lo(w.shape[0]))
    nw = neww_ref[0].astype(F32)
    s_w = _dot_nt(qs, kkw.astype(BF16))
    s_wn = jnp.sum(qs_f * nw[:, :LANES], axis=-1, keepdims=True)
    o_w = _one_key_softmax(s_w, vvw.astype(BF16), s_wn, nw[:, LANES:])

    o_c = oc_ref[0, 0]
    gate = gate_ref[0]
    heads = []
    for r in range(NSA_GROUP):
        heads.append(gate[:, 3 * r:3 * r + 1] * o_c[r:r + 1] + gate[:, 3 * r + 1:3 * r + 2] * o_s[r:r + 1]
                     + gate[:, 3 * r + 2:3 * r + 3] * o_w[r:r + 1])
    o_ref[0] = jnp.concatenate([jnp.where(lo1, heads[0], heads[1]), jnp.where(lo1, heads[2], heads[3])], axis=1)

    n_w = w.shape[0]
    wout_ref[0, 0:n_w - 1, :] = win_ref[0, 1:n_w, :]
    wout_ref[0, n_w - 1:n_w, :] = newwf_ref[0]


def _sample_attend(p, oc, idx_flat, pt_flat, slc_half, state_win, b, n_pages):
    win_len = state_win.shape[1]

    def blk_spec(k):
        def imap(bb, g, idx, pt):
            j = idx[(bb * NSA_KV_HEADS + g) * N_SELECT + k]
            pg = pt[bb * n_pages + jnp.minimum(j >> 1, n_pages - 1)]
            return (pg * 2 + (j & 1), 0, g)
        return pl.BlockSpec((1, SEL_BLOCK, LANES), imap)

    row = lambda w: pl.BlockSpec((1, 1, w), lambda bb, g, idx, pt: (bb, 0, g))
    gs = pltpu.PrefetchScalarGridSpec(
        num_scalar_prefetch=2, grid=(b, NSA_KV_HEADS),
        in_specs=[blk_spec(k) for k in range(N_SELECT)] + [
            row(2 * LANES), row(2 * LANES), row(2 * LANES), row(LANES),
            pl.BlockSpec((1, win_len, LANES), lambda bb, g, idx, pt: (bb, 0, g)),
            row(LANES),
            pl.BlockSpec((1, 1, PAD_ROWS, LANES), lambda bb, g, idx, pt: (bb, g, 0, 0))],
        out_specs=[row(2 * LANES), pl.BlockSpec((1, win_len, LANES), lambda bb, g, idx, pt: (bb, 0, g))])
    return pl.pallas_call(
        _sample_attend_kernel, grid_spec=gs,
        out_shape=[jax.ShapeDtypeStruct((b, 1, NSA_WIDTH), F32),
                   jax.ShapeDtypeStruct((b, win_len, KV_WIDTH), F32)],
        compiler_params=pltpu.CompilerParams(dimension_semantics=("arbitrary",) * 2, vmem_limit_bytes=VMEM_LIMIT),
        name="sample_attend",
    )(idx_flat, pt_flat, *([slc_half] * N_SELECT), p["qrot"], p["slckv"], p["winkv"], p["win"], state_win,
      p["gate"], oc)


def _mla_sample_kernel(pt_ref, *refs):
    del pt_ref
    n = PAGES_PER_STEP
    lat_pages, kr_pages = refs[:n], refs[n:2 * n]
    qlat_ref, qpe_ref, cnew_ref, krnew_ref, o_ref, m_ref, l_ref, acc_ref = refs[2 * n:]
    step = pl.program_id(1)
    qlat = _pad_rows(qlat_ref[0], PAD_ROWS)
    qpe = _pad_rows(qpe_ref[0].astype(F32), PAD_ROWS)

    @pl.when(step == 0)
    def _():
        c_new = cnew_ref[0]
        s_new = (jnp.sum(qlat * c_new, axis=-1, keepdims=True)
                 + jnp.sum(qpe * krnew_ref[0], axis=-1, keepdims=True))
        m_ref[...] = s_new
        l_ref[...] = jnp.ones(l_ref.shape, F32)
        acc_ref[...] = jnp.broadcast_to(c_new, acc_ref.shape)

    c_b = jnp.concatenate([pg[0] for pg in lat_pages], axis=0).astype(BF16)
    kr_b = jnp.concatenate([pg[0] for pg in kr_pages], axis=0).astype(BF16)
    s = _dot_nt(qlat.astype(BF16), c_b) + _dot_nt(qpe.astype(BF16), kr_b)
    m, l, acc = _softmax_update(s, c_b, m_ref[...], l_ref[...], acc_ref[...])
    m_ref[...] = m
    l_ref[...] = l
    acc_ref[...] = acc

    @pl.when(step == pl.num_programs(1) - 1)
    def _():
        o_ref[0] = (acc / l)[0:MLA_HEADS]


def _mla_sample(p, lat_cache, kr_cache, pt_flat, b, n_pages):
    page = lat_cache.shape[1]
    head = lambda w: pl.BlockSpec((1, MLA_HEADS, w), lambda bb, c, pt: (bb, 0, 0))
    row = lambda w: pl.BlockSpec((1, 1, w), lambda bb, c, pt: (bb, 0, 0))
    gs = pltpu.PrefetchScalarGridSpec(
        num_scalar_prefetch=1, grid=(b, n_pages // PAGES_PER_STEP),
        in_specs=_page_specs(n_pages, page, MLA_KV_LORA) + _page_specs(n_pages, page, MLA_ROPE)
        + [head(MLA_KV_LORA), head(MLA_ROPE), row(MLA_KV_LORA), row(MLA_ROPE)],
        out_specs=head(MLA_KV_LORA),
        scratch_shapes=[pltpu.VMEM((PAD_ROWS, 1), F32), pltpu.VMEM((PAD_ROWS, 1), F32),
                        pltpu.VMEM((PAD_ROWS, MLA_KV_LORA), F32)])
    qlat = p["qlat"].reshape(b, MLA_HEADS, MLA_KV_LORA)
    qpe = p["qpe"].reshape(b, MLA_HEADS, MLA_ROPE).astype(F32)
    return pl.pallas_call(
        _mla_sample_kernel, grid_spec=gs,
        out_shape=jax.ShapeDtypeStruct((b, MLA_HEADS, MLA_KV_LORA), F32),
        compiler_params=pltpu.CompilerParams(dimension_semantics=("arbitrary",) * 2, vmem_limit_bytes=VMEM_LIMIT),
        name="mla_sample",
    )(pt_flat, *([lat_cache] * PAGES_PER_STEP), *([kr_cache] * PAGES_PER_STEP), qlat, qpe, p["c"], p["kr"])


def _rope_tables(pos, rows):
    posf = pos.astype(F32)[:, None]

    def table(theta, dim, period, active):
        half = dim // 2
        inv = 1.0 / (jnp.float32(theta) ** (jnp.arange(half, dtype=F32) / half))
        ang = posf * inv[None, :]
        cos, sin = jnp.cos(ang), jnp.sin(ang)
        lane = np.arange(LANES)
        d = lane % period
        is_lo = (d < half) & active(lane)
        is_hi = (d >= half) & (d < dim) & active(lane)
        fi = np.where(d < half, d, np.clip(d - half, 0, half - 1))
        cos_l, sin_l = cos[:, fi], sin[:, fi]
        c = jnp.where(is_lo | is_hi, cos_l, 1.0)
        s_lo = jnp.where(is_lo, -sin_l, 0.0)
        s_hi = jnp.where(is_hi, sin_l, 0.0)
        tab = jnp.stack([c, s_lo, s_hi])
        return jnp.broadcast_to(tab, (3, rows, LANES)) if tab.shape[1] != rows else tab

    every = lambda lane: np.ones_like(lane, bool)
    keys_only = lambda lane: (lane % LANES) < HD
    return (table(ROPE_THETA, ROT_DIM, HD, every), table(ROPE_THETA, ROT_DIM, HD, keys_only),
            table(MLA_ROPE_THETA, MLA_ROPE, MLA_ROPE, every))


def _pack_weights(l, norm_pre, w_in, pe_cmp, q_norm, w_q_up, kv_norm, w_kv_up, w_proj_a, w_proj_b, w_out, norm_post):
    w = w_in[l]
    o = IN_OFFSETS
    seg = lambda k: w[:, o[k]:o[k + 1]]
    gn = seg(4)
    per_group = 3 * NSA_GROUP
    gn_p = jnp.zeros((D_MODEL, NSA_KV_HEADS * LANES), w.dtype)
    for g in range(NSA_KV_HEADS):
        gn_p = gn_p.at[:, g * LANES:g * LANES + per_group].set(gn[:, g * per_group:(g + 1) * per_group])
    packed = jnp.concatenate([seg(0), seg(1), seg(2), seg(3), gn_p, seg(5), seg(6), seg(7),
                              jnp.tile(seg(8), (1, LANES // MLA_ROPE)), seg(9), seg(10), seg(11)], axis=1)
    wq = w_q_up[l]
    wqup = jnp.concatenate([wq[..., :MLA_NOPE].reshape(MLA_Q_LORA, -1), wq[..., MLA_NOPE:].reshape(MLA_Q_LORA, -1)],
                           axis=1)
    wkv = w_kv_up[l]
    w2uk = jnp.transpose(wkv[..., :MLA_NOPE], (1, 2, 0)).reshape(MLA_HEADS // 2, LANES, MLA_KV_LORA)
    wv = jnp.transpose(wkv[..., MLA_NOPE:], (1, 0, 2))
    zeros = jnp.zeros_like(wv)
    even = (jnp.arange(MLA_HEADS) % 2 == 0)[:, None, None]
    wuv = jnp.concatenate([jnp.where(even, wv, zeros), jnp.where(even, zeros, wv)], axis=2)
    return {
        "npre": norm_pre[l][None].astype(F32), "w_in": packed.astype(BF16),
        "pe": pe_cmp[l].reshape(CMP_BLOCK, KV_WIDTH).astype(F32),
        "qnorm": q_norm[l][None].astype(F32), "wqup": wqup.astype(BF16), "kvnorm": kv_norm[l][None].astype(F32),
        "w2uk": w2uk.astype(BF16), "wuv": wuv.astype(BF16), "wpa": w_proj_a[l].astype(BF16),
        "wpb": w_proj_b[l].astype(BF16), "wout": w_out[l].astype(BF16), "npost": norm_post[l][None].astype(F32),
    }


def _prompt_layer(x, wts):
    b, t, _ = x.shape
    tabs = _rope_tables(jnp.arange(t), t)
    p = _in_project(x, tabs, wts, 256, with_compress=True, with_qlat=False)
    o_a = _nsa_prompt(p, b, t)
    o_b = _mla_prompt(p, wts, b, t)
    y = _merge(x, o_a, o_b, p, wts, 512, from_latent=False)
    kv6 = lambda a: a.reshape(b, t, NSA_KV_HEADS, 2, HD)
    win_keep = min(WINDOW, t)
    return y, (kv6(p["cmp"]), kv6(p["slc"]), p["c"], p["kr"], kv6(p["win"])[:, t - win_keep:])


def _sample_layer(x, l, caches, state_win, page_table, wts):
    cache_cmp, cache_slc, cache_lat, cache_kr = caches
    b, s_new, _ = x.shape
    assert s_new == 1
    n_pages = page_table.shape[1]
    page = cache_cmp.shape[2]
    past = n_pages * page
    assert past % SEL_BLOCK == 0 and n_pages % PAGES_PER_STEP == 0 and page % SEL_BLOCK == 0
    win_len = state_win.shape[1]
    assert win_len == WINDOW and past >= WINDOW
    n_phys = cache_cmp.shape[1]
    pt_flat = page_table.reshape(-1).astype(jnp.int32)

    tabs = _rope_tables(jnp.full((1,), past, jnp.int32), b)
    p = _in_project(x.reshape(1, b, D_MODEL), tabs, wts, b, with_compress=False, with_qlat=True)
    p = {k: v.reshape(b, 1, v.shape[-1]) for k, v in p.items()}

    kc_all = _compress_pages(cache_cmp[l].reshape(n_phys, page, KV_WIDTH), pt_flat, wts["pe"], b, n_pages)
    oc, idx = _sample_select(p["q"], kc_all, b)
    slc_half = cache_slc[l].reshape(n_phys * (page // SEL_BLOCK), SEL_BLOCK, KV_WIDTH)
    o_a, new_win = _sample_attend(p, oc, idx.reshape(-1), pt_flat, slc_half,
                                  state_win.reshape(b, win_len, KV_WIDTH), b, n_pages)
    o_lat = _mla_sample(p, cache_lat[l], cache_kr[l], pt_flat, b, n_pages)
    pm = {k: p[k].reshape(1, b, -1) for k in ("ga", "gb", "ma", "mb")}
    y = _merge(x.reshape(1, b, D_MODEL), o_a.reshape(1, b, NSA_WIDTH),
               o_lat.reshape(1, b, MLA_HEADS * MLA_KV_LORA), pm, wts, b, from_latent=True)
    kv6 = lambda a: a.reshape(b, -1, NSA_KV_HEADS, 2, HD)
    return y.reshape(b, 1, D_MODEL), (kv6(p["cmp"]), kv6(p["slc"]), p["c"], p["kr"], kv6(new_win))


def kernel(x_prompt, x_sample, cache_nsa_cmp, cache_nsa_slc, cache_mla_latent, cache_mla_krope, state_nsa_win,
           page_table, norm_pre, w_in, pe_cmp, q_norm, w_q_up, kv_norm, w_kv_up, w_proj_a, w_proj_b, w_out,
           norm_post):
    depth = w_in.shape[0]
    hp, hs = x_prompt, x_sample
    new_p, new_s = [], []
    for l in range(depth):
        wts = _pack_weights(l, norm_pre, w_in, pe_cmp, q_norm, w_q_up, kv_norm, w_kv_up, w_proj_a, w_proj_b,
                            w_out, norm_post)
        hp, sp = _prompt_layer(hp, wts)
        hs, ss = _sample_layer(hs, l, (cache_nsa_cmp, cache_nsa_slc, cache_mla_latent, cache_mla_krope),
                               state_nsa_win[l], page_table, wts)
        new_p.append(sp)
        new_s.append(ss)
    stack = lambda items, k: jnp.stack([s[k] for s in items])
    return (hp, hs) + tuple(stack(new_p, k) for k in range(5)) + tuple(stack(new_s, k) for k in range(5))
```

```python
import functools

import numpy as np
import jax
import jax.numpy as jnp
from jax import lax
from jax.experimental import pallas as pl
from jax.experimental.pallas import tpu as pltpu

D_MODEL = 1024
NSA_HEADS = 8
NSA_KV_HEADS = 2
NSA_GROUP = NSA_HEADS // NSA_KV_HEADS
HD = 64
NSA_WIDTH = NSA_HEADS * HD
KV_WIDTH = NSA_KV_HEADS * 2 * HD
ROT_DIM = HD // 4
ROPE_THETA = 500000.0
CMP_BLOCK = 32
SEL_BLOCK = 64
N_SELECT = 16
WINDOW = 512
NSA_SCALE = HD ** -0.5

MLA_HEADS = 8
MLA_Q_LORA = 384
MLA_KV_LORA = 256
MLA_NOPE = 64
MLA_ROPE = 32
MLA_V = 64
MLA_WIDTH = MLA_HEADS * MLA_V
MLA_ROPE_THETA = 10000.0
MLA_SCALE = (MLA_NOPE + MLA_ROPE) ** -0.5
LOG2E = 1.4426950408889634

RMS_EPS = 1e-6
NEG = -1e30
FORCE_SCORE = 1e4

IN_SPLITS = (NSA_WIDTH, KV_WIDTH, KV_WIDTH, KV_WIDTH, 3 * NSA_HEADS, NSA_WIDTH,
             MLA_Q_LORA, MLA_KV_LORA, MLA_ROPE, MLA_WIDTH, D_MODEL, D_MODEL)
IN_OFFSETS = tuple(int(v) for v in np.cumsum((0,) + IN_SPLITS))

LANES = 128
HALF = LANES // 2
GATE_ROWS = 16

SEG_A = (0, 1280)
SEG_GN = (1280, 1536)
SEG_GA = (1536, 2048)
SEG_QD = (2048, 2432)
SEG_KVD = (2432, 2688)
SEG_KRP = (2688, 2816)
SEG_GB = (2816, 3328)
SEG_MA = (3328, 4352)
SEG_MB = (4352, 5376)
PACKED_WIDTH = 5376

PT_Q = (0, 512)
PT_CMP = (512, 768)
PT_SLC = (768, 1024)
PT_WIN = (1024, 1280)
PT_GN = (1280, 1280 + NSA_KV_HEADS * GATE_ROWS)
PT_KRP = (PT_GN[1], PT_GN[1] + MLA_ROPE)
PT_ROWS = PT_KRP[1]
PR_GA = (0, 512)
PR_QD = (512, 896)
PR_KVD = (896, 1152)
PR_GB = (1152, 1664)
PR_MA = (1664, 2688)
PR_MB = (2688, 3712)
PR_COLS = 3712

PROMPT_TM = 256
ATT_TQ = 256
ATT_TK = 256
MLA_TK = 256

VMEM_LIMIT = 48 * 1024 * 1024
BF16 = jnp.bfloat16
F32 = jnp.float32


def _full_spec(shape):
    nd = len(shape)
    return pl.BlockSpec(shape, lambda *_: (0,) * nd)


def _lane_lo(rows):
    return lax.broadcasted_iota(jnp.int32, (rows, LANES), 1) < HALF


def _dot(a, b):
    return jnp.dot(a, b, preferred_element_type=F32)


def _dot_nt(a, b):
    return lax.dot_general(a, b, (((1,), (1,)), ((), ())), preferred_element_type=F32)


def _rms(v, gain):
    return v * lax.rsqrt(jnp.mean(v * v, axis=-1, keepdims=True) + RMS_EPS) * gain


def _split_bf16(v):
    hi = v.astype(BF16)
    return hi, (v - hi.astype(F32)).astype(BF16)


def _rope_tiles(v, tab_ref, shift):
    c, s_lo, s_hi = tab_ref[0], tab_ref[1], tab_ref[2]
    out = []
    for k in range(v.shape[1] // LANES):
        a = v[:, k * LANES:(k + 1) * LANES]
        out.append(a * c + pltpu.roll(a, LANES - shift, 1) * s_lo + pltpu.roll(a, shift, 1) * s_hi)
    return out[0] if len(out) == 1 else jnp.concatenate(out, axis=1)


def _dup_kv(a, lo):
    r = pltpu.roll(a, HALF, 1)
    return jnp.where(lo, a, r), jnp.where(lo, r, a)


def _kv_pack(v):
    lo = _lane_lo(v.shape[0])
    parts = []
    for g in range(NSA_KV_HEADS):
        kk, vv = _dup_kv(v[:, g * LANES:(g + 1) * LANES], lo)
        parts += [kk, vv]
    return jnp.concatenate(parts, axis=1).astype(BF16)


def _inproj_rows_kernel(x_ref, npre_ref, w_ref, tq_ref, tkv_ref, tm_ref, qnorm_ref, wqup_ref, kvnorm_ref,
                        w2uk_ref, q_ref, qrot_ref, gate_ref, cmp_ref, slc_ref, win_ref, slckv_ref, winkv_ref,
                        ga_ref, gb_ref, ma_ref, mb_ref, qpe_ref, c_ref, kr_ref, qlat_ref):
    xb = _rms(x_ref[0], npre_ref[...]).astype(BF16)

    def seg(lohi):
        return _dot(xb, w_ref[:, lohi[0]:lohi[1]])

    a = seg(SEG_A)
    q = a[:, :NSA_WIDTH]
    q_ref[0] = (q * NSA_SCALE).astype(BF16)
    qrot_ref[0] = (_rope_tiles(q, tq_ref, ROT_DIM // 2) * NSA_SCALE).astype(BF16)
    cmp_ref[0] = a[:, NSA_WIDTH:NSA_WIDTH + KV_WIDTH]
    kvs = _rope_tiles(a[:, NSA_WIDTH + KV_WIDTH:NSA_WIDTH + 2 * KV_WIDTH], tkv_ref, ROT_DIM // 2)
    slc_ref[0] = kvs
    slckv_ref[0] = _kv_pack(kvs)
    kvw = _rope_tiles(a[:, NSA_WIDTH + 2 * KV_WIDTH:], tkv_ref, ROT_DIM // 2)
    win_ref[0] = kvw
    winkv_ref[0] = _kv_pack(kvw)

    gate_ref[0] = jax.nn.sigmoid(seg(SEG_GN))
    ga_ref[0] = seg(SEG_GA)
    gb_ref[0] = seg(SEG_GB)
    ma_ref[0] = seg(SEG_MA)
    mb_ref[0] = seg(SEG_MB)

    qd = _rms(seg(SEG_QD), qnorm_ref[...]).astype(BF16)
    qh = _dot(qd, wqup_ref[...])
    qn = qh[:, :MLA_HEADS * MLA_NOPE].astype(BF16)
    qpe_ref[0] = _rope_tiles(qh[:, MLA_HEADS * MLA_NOPE:], tm_ref, MLA_ROPE // 2) * MLA_SCALE
    c_ref[0] = _rms(seg(SEG_KVD), kvnorm_ref[...])
    kr_ref[0] = _rope_tiles(seg(SEG_KRP), tm_ref, MLA_ROPE // 2)[:, :MLA_ROPE]

    rows = qn.shape[0]
    lo = _lane_lo(rows)
    z = jnp.zeros((rows, LANES), BF16)
    parts = []
    for j in range(MLA_HEADS // 2):
        pair = qn[:, j * LANES:(j + 1) * LANES]
        parts.append(_dot(jnp.where(lo, pair, z), w2uk_ref[j]) * MLA_SCALE)
        parts.append(_dot(jnp.where(lo, z, pair), w2uk_ref[j]) * MLA_SCALE)
    qlat_ref[0] = jnp.concatenate(parts, axis=1)


def _in_project_rows(x3, tabs, wts):
    b, t, _ = x3.shape
    tq, tkv, tmla = tabs
    row = lambda w, dt: jax.ShapeDtypeStruct((b, t, w), dt)
    names = ["q", "qrot", "gate", "cmp", "slc", "win", "slckv", "winkv", "ga", "gb", "ma", "mb", "qpe", "c", "kr",
             "qlat"]
    out_shape = [row(512, BF16), row(512, BF16), row(256, F32), row(256, F32), row(256, F32), row(256, F32),
                 row(512, BF16), row(512, BF16), row(512, F32), row(512, F32), row(1024, F32), row(1024, F32),
                 row(MLA_HEADS * MLA_ROPE, F32), row(MLA_KV_LORA, F32), row(MLA_ROPE, F32),
                 row(MLA_HEADS * MLA_KV_LORA, F32)]
    tok = lambda w: pl.BlockSpec((1, t, w), lambda bb: (bb, 0, 0))
    tab = _full_spec((3, t, LANES))
    in_specs = [tok(D_MODEL), _full_spec((1, D_MODEL)), _full_spec((D_MODEL, PACKED_WIDTH)), tab, tab, tab,
                _full_spec((1, MLA_Q_LORA)), _full_spec((MLA_Q_LORA, MLA_HEADS * (MLA_NOPE + MLA_ROPE))),
                _full_spec((1, MLA_KV_LORA)), _full_spec((MLA_HEADS // 2, LANES, MLA_KV_LORA))]
    res = pl.pallas_call(
        _inproj_rows_kernel, grid=(b,), in_specs=in_specs, out_specs=[tok(s.shape[2]) for s in out_shape],
        out_shape=out_shape,
        compiler_params=pltpu.CompilerParams(dimension_semantics=("arbitrary",), vmem_limit_bytes=VMEM_LIMIT),
        name="in_project_rows",
    )(x3, wts["npre"], wts["w_rows"], tq, tkv, tmla, wts["qnorm"], wts["wqup"], wts["kvnorm"], wts["w2uk"])
    return dict(zip(names, res))


def _rope_rows(x, cos, sin, half):
    x1, x2 = x[0:half], x[half:2 * half]
    parts = [x1 * cos - x2 * sin, x1 * sin + x2 * cos]
    if x.shape[0] > 2 * half:
        parts.append(x[2 * half:])
    return jnp.concatenate(parts, axis=0)


def _store_chunks(ref, v):
    tk = ref.shape[3]
    for j in range(ref.shape[1]):
        ref[0, j] = v[:, j * tk:(j + 1) * tk]


def _inproj_cols_kernel(x_ref, npre_ref, wt_ref, w_ref, ropeq_ref, ropem_ref, pe_ref, qnorm_ref, wqupt_ref,
                        kvnorm_ref, pool_ref, wkx_ref, wvt_ref,
                        qt_ref, qrt_ref, gate_ref, cmp_ref, slc_ref, win_ref, slcb_ref, winb_ref, slcr_ref,
                        winr_ref, kc_ref, ga_ref, gb_ref, ma_ref, mb_ref, qmt_ref, c_ref, kmla_ref, vmt_ref,
                        kr_ref):
    tm = x_ref.shape[1]
    xb = _rms(x_ref[0], npre_ref[...]).astype(BF16)
    segt = lambda lohi: _dot_nt(wt_ref[lohi[0]:lohi[1], :], xb)
    seg = lambda lohi: _dot(xb, w_ref[:, lohi[0]:lohi[1]])
    cq, sq = ropeq_ref[0], ropeq_ref[1]
    cm, sm = ropem_ref[0], ropem_ref[1]
    hq, hm = ROT_DIM // 2, MLA_ROPE // 2

    qt = segt(PT_Q)
    qt_ref[0] = (qt * (NSA_SCALE * LOG2E)).astype(BF16)
    qrt = jnp.concatenate([_rope_rows(qt[h * HD:(h + 1) * HD], cq, sq, hq) for h in range(NSA_HEADS)], axis=0)
    qrt_ref[0] = (qrt * (NSA_SCALE * LOG2E)).astype(BF16)

    cmpt = segt(PT_CMP)
    cmp_ref[0] = cmpt
    hi, lo = _split_bf16(cmpt)
    pool = pool_ref[...]
    pooled = (_dot_nt(pool, hi) + _dot_nt(pool, lo))[0:tm // CMP_BLOCK]
    kc_ref[0] = (pooled + jnp.sum(pe_ref[...], axis=0, keepdims=True)) * (1.0 / CMP_BLOCK)

    def rope_kv(v):
        parts = []
        for g in range(NSA_KV_HEADS):
            parts.append(_rope_rows(v[g * LANES:g * LANES + HD], cq, sq, hq))
            parts.append(v[g * LANES + HD:(g + 1) * LANES])
        return jnp.concatenate(parts, axis=0)

    slct = rope_kv(segt(PT_SLC))
    slc_ref[0] = slct
    _store_chunks(slcb_ref, slct.astype(BF16))
    slcr_ref[0] = slct.T.astype(BF16)
    wint = rope_kv(segt(PT_WIN))
    win_ref[0] = wint
    _store_chunks(winb_ref, wint.astype(BF16))
    winr_ref[0] = wint.T.astype(BF16)

    gate_ref[0] = jax.nn.sigmoid(segt(PT_GN))
    krt = _rope_rows(segt(PT_KRP), cm, sm, hm)
    kr_ref[0] = krt
    c = _rms(seg(PR_KVD), kvnorm_ref[...])
    c_ref[0] = c
    c_b = c.astype(BF16)
    kr_rows = jnp.concatenate([krt, jnp.zeros((LANES - MLA_ROPE, tm), F32)], axis=0).T
    ckr = jnp.concatenate([c_b, kr_rows.astype(BF16)], axis=1)
    kmla_ref[0] = _dot(ckr, wkx_ref[...]).astype(BF16)
    _store_chunks(vmt_ref, _dot_nt(wvt_ref[...], c_b).astype(BF16))

    qd = _rms(seg(PR_QD), qnorm_ref[...]).astype(BF16)
    qht = _dot_nt(wqupt_ref[...], qd)
    n_nope = MLA_HEADS * MLA_NOPE
    zq = jnp.zeros((LANES - MLA_NOPE - MLA_ROPE, tm), F32)
    parts = []
    for h in range(MLA_HEADS):
        parts += [qht[h * MLA_NOPE:(h + 1) * MLA_NOPE],
                  _rope_rows(qht[n_nope + h * MLA_ROPE:n_nope + (h + 1) * MLA_ROPE], cm, sm, hm), zq]
    qmt_ref[0] = (jnp.concatenate(parts, axis=0) * (MLA_SCALE * LOG2E)).astype(BF16)

    ga_ref[0] = seg(PR_GA)
    gb_ref[0] = seg(PR_GB)
    ma_ref[0] = seg(PR_MA)
    mb_ref[0] = seg(PR_MB)


def _in_project_cols(x3, ropes, wts):
    b, t, _ = x3.shape
    tm = PROMPT_TM
    nt = t // tm
    ropeq, ropem = ropes
    pool = np.zeros((16, tm), np.float32)
    for s in range(tm):
        pool[s // CMP_BLOCK, s] = 1.0
    sds = jax.ShapeDtypeStruct
    rows = lambda w: pl.BlockSpec((1, tm, w), lambda i, bb: (bb, i, 0))
    cols = lambda w: pl.BlockSpec((1, w, tm), lambda i, bb: (bb, 0, i))
    chunk = lambda w, tk: pl.BlockSpec((1, tm // tk, w, tk), lambda i, bb: (bb, i, 0, 0))
    outs = [
        ("qt", sds((b, NSA_WIDTH, t), BF16), cols(NSA_WIDTH)),
        ("qrt", sds((b, NSA_WIDTH, t), BF16), cols(NSA_WIDTH)),
        ("gate", sds((b, NSA_KV_HEADS * GATE_ROWS, t), F32), cols(NSA_KV_HEADS * GATE_ROWS)),
        ("cmp", sds((b, KV_WIDTH, t), F32), cols(KV_WIDTH)),
        ("slc", sds((b, KV_WIDTH, t), F32), cols(KV_WIDTH)),
        ("win", sds((b, KV_WIDTH, t), F32), cols(KV_WIDTH)),
        ("slcb", sds((b, t // ATT_TK, KV_WIDTH, ATT_TK), BF16), chunk(KV_WIDTH, ATT_TK)),
        ("winb", sds((b, t // ATT_TK, KV_WIDTH, ATT_TK), BF16), chunk(KV_WIDTH, ATT_TK)),
        ("slcr", sds((b, t, KV_WIDTH), BF16), rows(KV_WIDTH)),
        ("winr", sds((b, t, KV_WIDTH), BF16), rows(KV_WIDTH)),
        ("kc", sds((b, t // CMP_BLOCK, KV_WIDTH), F32),
         pl.BlockSpec((1, tm // CMP_BLOCK, KV_WIDTH), lambda i, bb: (bb, i, 0))),
        ("ga", sds((b, t, NSA_WIDTH), F32), rows(NSA_WIDTH)),
        ("gb", sds((b, t, MLA_WIDTH), F32), rows(MLA_WIDTH)),
        ("ma", sds((b, t, D_MODEL), F32), rows(D_MODEL)),
        ("mb", sds((b, t, D_MODEL), F32), rows(D_MODEL)),
        ("qmt", sds((b, MLA_HEADS * LANES, t), BF16), cols(MLA_HEADS * LANES)),
        ("c", sds((b, t, MLA_KV_LORA), F32), rows(MLA_KV_LORA)),
        ("kmla", sds((b, t, MLA_HEADS * LANES), BF16), rows(MLA_HEADS * LANES)),
        ("vmt", sds((b, t // MLA_TK, MLA_WIDTH, MLA_TK), BF16), chunk(MLA_WIDTH, MLA_TK)),
        ("kr", sds((b, MLA_ROPE, t), F32), cols(MLA_ROPE)),
    ]
    rope_spec = lambda half: pl.BlockSpec((2, half, tm), lambda i, bb: (0, 0, i))
    in_specs = [rows(D_MODEL), _full_spec((1, D_MODEL)), _full_spec((PT_ROWS, D_MODEL)),
                _full_spec((D_MODEL, PR_COLS)), rope_spec(ROT_DIM // 2), rope_spec(MLA_ROPE // 2),
                _full_spec((CMP_BLOCK, KV_WIDTH)), _full_spec((1, MLA_Q_LORA)),
                _full_spec((MLA_HEADS * (MLA_NOPE + MLA_ROPE), MLA_Q_LORA)), _full_spec((1, MLA_KV_LORA)),
                _full_spec((16, tm)), _full_spec((MLA_KV_LORA + LANES, MLA_HEADS * LANES)),
                _full_spec((MLA_WIDTH, MLA_KV_LORA))]
    res = pl.pallas_call(
        _inproj_cols_kernel, grid=(nt, b), in_specs=in_specs, out_specs=[o[2] for o in outs],
        out_shape=[o[1] for o in outs],
        compiler_params=pltpu.CompilerParams(dimension_semantics=("arbitrary", "arbitrary"),
                                             vmem_limit_bytes=VMEM_LIMIT),
        name="in_project_cols",
    )(x3, wts["npre"], wts["w_t"], wts["w_cols"], ropeq, ropem, wts["pe"], wts["qnorm"], wts["wqupt"],
      wts["kvnorm"], jnp.asarray(pool, BF16), wts["wkx"], wts["wvt"])
    return dict(zip([o[0] for o in outs], res))


def _online_update(s, vt, m, l, acc):
    m_new = jnp.maximum(m, jnp.max(s, axis=0, keepdims=True))
    p = jnp.exp2(s - m_new)
    alpha = jnp.exp2(m - m_new)
    return m_new, alpha * l + jnp.sum(p, axis=0, keepdims=True), alpha * acc + _dot(vt, p.astype(BF16))


def _nsa_prompt_kernel(n_blk, qt_ref, qrt_ref, gate_ref, kc_ref, ks_ref, kst_ref, kw_ref, kwt_ref, o_ref, sb_ref):
    i = pl.program_id(2)
    tq, tk = ATT_TQ, ATT_TK
    nl = NSA_GROUP * tq
    zq = jnp.zeros((HD, tq), BF16)

    def widen(qt):
        return jnp.concatenate([jnp.concatenate([qt[r * HD:(r + 1) * HD], zq], axis=0)
                                for r in range(NSA_GROUP)], axis=1)

    qc = widen(qt_ref[0])
    qr = widen(qrt_ref[0])

    n_cmp = 2 * n_blk
    kc = kc_ref[0]
    s_c = _dot(kc.astype(BF16), qc)
    rho = lax.broadcasted_iota(jnp.int32, (n_cmp, nl), 0)
    cmp_idx = jnp.where(rho < n_blk, 2 * rho, 2 * (rho - n_blk) + 1)
    t_l = i * tq + (lax.broadcasted_iota(jnp.int32, (n_cmp, nl), 1) & (tq - 1))
    mask_c = cmp_idx * CMP_BLOCK + (CMP_BLOCK - 1) <= t_l
    s_c = jnp.where(mask_c, s_c, NEG)
    e_c = jnp.where(mask_c, jnp.exp2(s_c - jnp.max(s_c, axis=0, keepdims=True)), 0.0)
    p_c = e_c / jnp.maximum(jnp.sum(e_c, axis=0, keepdims=True), 1e-30)
    kct = jnp.concatenate([kc, jnp.zeros((LANES - n_cmp, LANES), F32)], axis=0).T
    p_pad = jnp.concatenate([p_c, jnp.zeros((LANES - n_cmp, nl), F32)], axis=0)
    o_c = _dot(kct[HD:2 * HD].astype(BF16), p_pad.astype(BF16))

    imp = p_c[:, 0:tq]
    for r in range(1, NSA_GROUP):
        imp = imp + p_c[:, r * tq:(r + 1) * tq]
    imp_blk = imp[0:n_blk] + imp[n_blk:n_cmp]
    blk = lax.broadcasted_iota(jnp.int32, (n_blk, tq), 0)
    t_q = i * tq + lax.broadcasted_iota(jnp.int32, (n_blk, tq), 1)
    valid = blk * SEL_BLOCK <= t_q
    in_blk = (blk * SEL_BLOCK <= t_q) & (t_q < (blk + 1) * SEL_BLOCK)
    forced = in_blk | (blk == 0)
    score = jnp.where(forced, FORCE_SCORE, jnp.where(valid, imp_blk, -FORCE_SCORE))
    rank = jnp.zeros((n_blk, tq), jnp.int32)
    for j in range(n_blk):
        other = score[j:j + 1, :]
        ahead = (other > score) | ((other == score) & (j < blk))
        rank = rank + jnp.where(ahead, 1, 0)
    sel = (rank < N_SELECT) & (score > -1.0)
    sb_ref[...] = jnp.where(sel, 0.0, NEG)

    key_r = lax.broadcasted_iota(jnp.int32, (tk, tq), 0)
    t_k = i * tq + lax.broadcasted_iota(jnp.int32, (tk, tq), 1)
    n_chunks = ((i + 1) * tq + tk - 1) // tk
    c_lo = jnp.maximum(i * tq - WINDOW, 0) // tk
    init = (jnp.full((1, nl), NEG, F32), jnp.zeros((1, nl), F32), jnp.zeros((HD, nl), F32))
    per_chunk = tk // SEL_BLOCK
    tile = lambda bias: jnp.concatenate([bias] * NSA_GROUP, axis=1)

    def sel_scores(c, base):
        s = _dot(ks_ref[0, pl.ds(base, tk), :], qr)
        rows = [jnp.broadcast_to(sb_ref[pl.ds(c * per_chunk + j, 1), :], (SEL_BLOCK, tq)) for j in range(per_chunk)]
        return s, jnp.concatenate(rows, axis=0)

    def past_body(c, carry):
        base = pl.multiple_of(c * tk, tk)
        s, bias = sel_scores(c, base)
        return _online_update(s + tile(bias), kst_ref[0, c, HD:2 * HD, :], *carry)

    def near_body(c, carry):
        base = pl.multiple_of(c * tk, tk)
        s, bias = sel_scores(c, base)
        w = _dot(kw_ref[0, pl.ds(base, tk), :], qr)
        dist = t_k - (base + key_r)
        causal = jnp.where(dist >= 0, 0.0, NEG)
        s = s + tile(bias + causal)
        w = w + tile(jnp.where(dist <= WINDOW, causal, NEG))
        return (_online_update(s, kst_ref[0, c, HD:2 * HD, :], *carry[:3])
                + _online_update(w, kwt_ref[0, c, HD:2 * HD, :], *carry[3:]))

    far = lax.fori_loop(0, c_lo, past_body, init)
    _, l_s, a_s, _, l_w, a_w = lax.fori_loop(c_lo, n_chunks, near_body, far + init)
    o_s = a_s / l_s
    o_w = a_w / l_w

    gate = gate_ref[0]
    heads = []
    for r in range(NSA_GROUP):
        sl = slice(r * tq, (r + 1) * tq)
        heads.append(gate[3 * r:3 * r + 1] * o_c[:, sl] + gate[3 * r + 1:3 * r + 2] * o_s[:, sl]
                     + gate[3 * r + 2:3 * r + 3] * o_w[:, sl])
    o_ref[0] = jnp.concatenate(heads, axis=0).T


def _nsa_prompt(p, b, t):
    tq, tk = ATT_TQ, ATT_TK
    n_blk = t // SEL_BLOCK
    assert 2 * n_blk <= LANES and t % tk == 0
    kc = p["kc"].reshape(b, n_blk, 2, KV_WIDTH).transpose(0, 2, 1, 3).reshape(b, 2 * n_blk, KV_WIDTH)
    qspec = pl.BlockSpec((1, NSA_GROUP * HD, tq), lambda bb, g, i: (bb, g, i))
    rm = pl.BlockSpec((1, t, LANES), lambda bb, g, i: (bb, 0, g))
    fm = pl.BlockSpec((1, t // tk, LANES, tk), lambda bb, g, i: (bb, 0, g, 0))
    return pl.pallas_call(
        functools.partial(_nsa_prompt_kernel, n_blk),
        grid=(b, NSA_KV_HEADS, t // tq),
        in_specs=[qspec, qspec, pl.BlockSpec((1, GATE_ROWS, tq), lambda bb, g, i: (bb, g, i)),
                  pl.BlockSpec((1, 2 * n_blk, LANES), lambda bb, g, i: (bb, 0, g)), rm, fm, rm, fm],
        out_specs=pl.BlockSpec((1, tq, NSA_GROUP * HD), lambda bb, g, i: (bb, i, g)),
        out_shape=jax.ShapeDtypeStruct((b, t, NSA_WIDTH), F32),
        scratch_shapes=[pltpu.VMEM((n_blk, tq), F32)],
        compiler_params=pltpu.CompilerParams(dimension_semantics=("arbitrary",) * 3, vmem_limit_bytes=VMEM_LIMIT),
        name="nsa_prompt",
    )(p["qt"], p["qrt"], p["gate"], kc, p["slcr"], p["slcb"], p["winr"], p["winb"])


MLA_TQ = 256


def _mla_prompt_kernel(qmt_ref, k_ref, vt_ref, o_ref, m_ref, l_ref, acc_ref):
    i = pl.program_id(1)
    tq, tk = MLA_TQ, MLA_TK
    m_ref[...] = jnp.full(m_ref.shape, NEG, F32)
    l_ref[...] = jnp.zeros(l_ref.shape, F32)
    acc_ref[...] = jnp.zeros(acc_ref.shape, F32)
    key_r = lax.broadcasted_iota(jnp.int32, (tk, tq), 0)
    t_k = i * tq + lax.broadcasted_iota(jnp.int32, (tk, tq), 1)
    n_chunks = ((i + 1) * tq + tk - 1) // tk

    def step(c, masked):
        base = pl.multiple_of(c * tk, tk)
        old = [(m_ref[h], l_ref[h], acc_ref[h]) for h in range(MLA_HEADS)]
        scores = []
        for h in range(MLA_HEADS):
            k_h = k_ref[0, pl.ds(base, tk), h * LANES:(h + 1) * LANES]
            s = _dot(k_h, qmt_ref[0, h * LANES:(h + 1) * LANES, :])
            scores.append(jnp.where(base + key_r <= t_k, s, NEG) if masked else s)
        new = [_online_update(scores[h], vt_ref[0, c, h * MLA_V:(h + 1) * MLA_V, :], *old[h])
               for h in range(MLA_HEADS)]
        for h in range(MLA_HEADS):
            m_ref[h], l_ref[h], acc_ref[h] = new[h]
        return 0

    n_past = (i * tq + 1) // tk
    lax.fori_loop(0, n_past, lambda c, z: step(c, False), 0)
    lax.fori_loop(n_past, n_chunks, lambda c, z: step(c, True), 0)
    outs = [acc_ref[h] / l_ref[h] for h in range(MLA_HEADS)]
    o_ref[0] = jnp.concatenate(outs, axis=0).T


def _mla_prompt(p, wts, b, t):
    tq, tk = MLA_TQ, MLA_TK
    assert t % tq == 0
    return pl.pallas_call(
        _mla_prompt_kernel,
        grid=(b, t // tq),
        in_specs=[pl.BlockSpec((1, MLA_HEADS * LANES, tq), lambda bb, i: (bb, 0, i)),
                  pl.BlockSpec((1, t, MLA_HEADS * LANES), lambda bb, i: (bb, 0, 0)),
                  pl.BlockSpec((1, t // tk, MLA_WIDTH, tk), lambda bb, i: (bb, 0, 0, 0))],
        out_specs=pl.BlockSpec((1, tq, MLA_WIDTH), lambda bb, i: (bb, i, 0)),
        out_shape=jax.ShapeDtypeStruct((b, t, MLA_WIDTH), F32),
        scratch_shapes=[pltpu.VMEM((MLA_HEADS, 1, tq), F32), pltpu.VMEM((MLA_HEADS, 1, tq), F32),
                        pltpu.VMEM((MLA_HEADS, MLA_V, tq), F32)],
        compiler_params=pltpu.CompilerParams(dimension_semantics=("arbitrary",) * 2, vmem_limit_bytes=VMEM_LIMIT),
        name="mla_prompt",
    )(p["qmt"], p["kmla"], p["vmt"])


def _merge_kernel(from_latent, x_ref, oa_ref, ob_ref, ga_ref, gb_ref, ma_ref, mb_ref, wpa_ref, wpb_ref, wout_ref,
                  npost_ref, wuv_ref, y_ref):
    if from_latent:
        lat = ob_ref[0].astype(BF16)
        parts = []
        for j in range(MLA_HEADS // 2):
            parts.append(_dot(lat[:, 2 * j * MLA_KV_LORA:(2 * j + 1) * MLA_KV_LORA], wuv_ref[2 * j])
                         + _dot(lat[:, (2 * j + 1) * MLA_KV_LORA:(2 * j + 2) * MLA_KV_LORA], wuv_ref[2 * j + 1]))
        o_b = jnp.concatenate(parts, axis=1)
    else:
        o_b = ob_ref[0]
    ga = ga_ref[0]
    gb = gb_ref[0]
    pa = _dot((oa_ref[0] * (ga * jax.nn.sigmoid(ga))).astype(BF16), wpa_ref[...])
    pb = _dot((o_b * (gb * jax.nn.sigmoid(gb))).astype(BF16), wpb_ref[...])
    h = jax.nn.sigmoid(ma_ref[0]) * pa + jax.nn.sigmoid(mb_ref[0]) * pb
    z = _dot(h.astype(BF16), wout_ref[...])
    y_ref[0] = x_ref[0] + _rms(z, npost_ref[...])


def _merge(x3, o_a, o_b, p, wts, tm, from_latent):
    b, t, _ = x3.shape
    tok = lambda w: pl.BlockSpec((1, tm, w), lambda bb, i: (bb, i, 0))
    return pl.pallas_call(
        functools.partial(_merge_kernel, from_latent),
        grid=(b, t // tm),
        in_specs=[tok(D_MODEL), tok(NSA_WIDTH), tok(o_b.shape[2]), tok(NSA_WIDTH), tok(MLA_WIDTH), tok(D_MODEL),
                  tok(D_MODEL), _full_spec((NSA_WIDTH, D_MODEL)), _full_spec((MLA_WIDTH, D_MODEL)),
                  _full_spec((D_MODEL, D_MODEL)), _full_spec((1, D_MODEL)),
                  _full_spec((MLA_HEADS, MLA_KV_LORA, LANES))],
        out_specs=tok(D_MODEL),
        out_shape=jax.ShapeDtypeStruct((b, t, D_MODEL), F32),
        compiler_params=pltpu.CompilerParams(dimension_semantics=("arbitrary",) * 2, vmem_limit_bytes=VMEM_LIMIT),
        name="merge",
    )(x3, o_a, o_b, p["ga"], p["gb"], p["ma"], p["mb"], wts["wpa"], wts["wpb"], wts["wout"], wts["npost"],
      wts["wuv"])


PAGES_PER_STEP = 8
PAD_ROWS = 16


def _compress_pages_kernel(pt_ref, *refs):
    del pt_ref
    pages = refs[:PAGES_PER_STEP]
    pool_ref, pe_ref, o_ref = refs[PAGES_PER_STEP:]
    x = jnp.concatenate([pg[0] for pg in pages], axis=1)
    hi, lo = _split_bf16(x)
    pool = pool_ref[...]
    pooled = _dot_nt(pool, hi) + _dot_nt(pool, lo)
    o_ref[0] = (pooled + jnp.sum(pe_ref[...], axis=0, keepdims=True)) * (1.0 / CMP_BLOCK)


def _page_specs(n_pages, rows, page):
    return [pl.BlockSpec((1, rows, page),
                         lambda bb, c, pt, k=k: (pt[bb * n_pages + c * PAGES_PER_STEP + k], 0, 0))
            for k in range(PAGES_PER_STEP)]


def _compress_pages(cache_t, pt_flat, pe, b, n_pages):
    page = cache_t.shape[2]
    per_step = PAGES_PER_STEP * page // CMP_BLOCK
    pool = np.zeros((per_step, PAGES_PER_STEP * page), np.float32)
    for s in range(PAGES_PER_STEP * page):
        pool[s // CMP_BLOCK, s] = 1.0
    gs = pltpu.PrefetchScalarGridSpec(
        num_scalar_prefetch=1, grid=(b, n_pages // PAGES_PER_STEP),
        in_specs=_page_specs(n_pages, KV_WIDTH, page) + [
            pl.BlockSpec(pool.shape, lambda bb, c, pt: (0, 0)),
            pl.BlockSpec((CMP_BLOCK, KV_WIDTH), lambda bb, c, pt: (0, 0))],
        out_specs=pl.BlockSpec((1, per_step, KV_WIDTH), lambda bb, c, pt: (bb, c, 0)))
    return pl.pallas_call(
        _compress_pages_kernel, grid_spec=gs,
        out_shape=jax.ShapeDtypeStruct((b, n_pages * page // CMP_BLOCK, KV_WIDTH), F32),
        compiler_params=pltpu.CompilerParams(dimension_semantics=("arbitrary",) * 2, vmem_limit_bytes=VMEM_LIMIT),
        name="compress_pages",
    )(pt_flat, *([cache_t] * PAGES_PER_STEP), jnp.asarray(pool, BF16), pe)


def _pad_rows(v, rows):
    return jnp.concatenate([v, jnp.zeros((rows - v.shape[0], v.shape[1]), v.dtype)], axis=0)


def _stack_heads(qv, lo):
    a, b = qv[:, :LANES], qv[:, LANES:]
    z = jnp.zeros_like(a)
    return jnp.concatenate([jnp.where(lo, a, z), jnp.where(lo, z, a),
                            jnp.where(lo, b, z), jnp.where(lo, z, b)], axis=0)


def _sample_select_kernel(n_cmp, q_ref, kc_ref, oc_ref, idx_ref):
    lo1 = _lane_lo(1)
    lo_c = _lane_lo(n_cmp)
    q = q_ref[0]
    kc = kc_ref[0]
    lane = lax.broadcasted_iota(jnp.int32, (1, n_cmp), 1)
    blk = lane >> 1
    slot = lax.broadcasted_iota(jnp.int32, (1, N_SELECT), 1)
    for g in range(NSA_KV_HEADS):
        kk, vv = _dup_kv(kc[:, g * LANES:(g + 1) * LANES], lo_c)
        qs = _pad_rows(_stack_heads(q[:, g * 2 * LANES:(g + 1) * 2 * LANES], lo1), PAD_ROWS)
        s = _dot_nt(qs, kk.astype(BF16))
        e = jnp.exp(s - jnp.max(s, axis=-1, keepdims=True))
        p = e / jnp.sum(e, axis=-1, keepdims=True)
        oc_ref[0, g] = _dot(p.astype(BF16), vv.astype(BF16))
        imp = p[0:1] + p[1:2] + p[2:3] + p[3:4]
        chunks = []
        for k in range(n_cmp // LANES):
            a = imp[:, k * LANES:(k + 1) * LANES]
            ev = (lax.broadcasted_iota(jnp.int32, (1, LANES), 1) & 1) == 0
            chunks.append(a + jnp.where(ev, pltpu.roll(a, LANES - 1, 1), pltpu.roll(a, 1, 1)))
        pair = jnp.concatenate(chunks, axis=1)
        v = jnp.where(blk == 0, -1.0, pair)
        idx = jnp.where(slot == N_SELECT - 1, n_cmp // 2, 0)
        for k in range(1, N_SELECT - 1):
            top = jnp.max(v, axis=-1, keepdims=True)
            jmin = jnp.min(jnp.where(v == top, blk.astype(F32), float(n_cmp)), axis=-1,
                           keepdims=True).astype(jnp.int32)
            idx = jnp.where(slot == k, jmin, idx)
            v = jnp.where(blk == jmin, -1.0, v)
        idx_ref[0, g:g + 1, :] = idx


def _sample_select(q, kc_all, b):
    n_cmp = kc_all.shape[1]
    return pl.pallas_call(
        functools.partial(_sample_select_kernel, n_cmp),
        grid=(b,),
        in_specs=[pl.BlockSpec((1, 1, NSA_WIDTH), lambda bb: (bb, 0, 0)),
                  pl.BlockSpec((1, n_cmp, KV_WIDTH), lambda bb: (bb, 0, 0))],
        out_specs=[pl.BlockSpec((1, NSA_KV_HEADS, PAD_ROWS, LANES), lambda bb: (bb, 0, 0, 0)),
                   pl.BlockSpec((1, NSA_KV_HEADS, N_SELECT), lambda bb: (bb, 0, 0))],
        out_shape=[jax.ShapeDtypeStruct((b, NSA_KV_HEADS, PAD_ROWS, LANES), F32),
                   jax.ShapeDtypeStruct((b, NSA_KV_HEADS, N_SELECT), jnp.int32)],
        compiler_params=pltpu.CompilerParams(dimension_semantics=("arbitrary",), vmem_limit_bytes=VMEM_LIMIT),
        name="sample_select",
    )(q, kc_all)


def _extra_key_softmax(s_past, vt4_b, s_new, v_new):
    m = jnp.maximum(jnp.max(s_past, axis=-1, keepdims=True), s_new)
    e = jnp.exp(s_past - m)
    e_new = jnp.exp(s_new - m)
    den = jnp.sum(e, axis=-1, keepdims=True) + e_new
    return (_dot_nt(e.astype(BF16), vt4_b) + e_new * v_new) / den


def _sample_attend_kernel(idx_ref, pt_ref, *refs):
    del pt_ref
    n_sel = N_SELECT
    pages = refs[:n_sel]
    qr_ref, newkv_ref, neww_ref, newwf_ref, win_ref, gate_ref, oc_ref, o_ref, wout_ref = refs[n_sel:]
    bb, g = pl.program_id(0), pl.program_id(1)
    width = NSA_GROUP * HD
    lane_head = lax.broadcasted_iota(jnp.int32, (PAD_ROWS, width), 1) // HD
    row = lax.broadcasted_iota(jnp.int32, (PAD_ROWS, width), 0)
    own = lane_head == row
    qs_f = jnp.where(own, jnp.broadcast_to(qr_ref[0].astype(F32), (PAD_ROWS, width)), 0.0)
    qs = qs_f.astype(BF16)
    four = lambda a: jnp.concatenate([a] * NSA_GROUP, axis=0)
    twice = lambda a: jnp.concatenate([a, a], axis=1)

    s_t = jnp.concatenate([pg[0] for pg in pages], axis=1)
    s_sel = _dot(qs, four(s_t[0:HD]).astype(BF16))
    lo1 = _lane_lo(1)
    base = (bb * NSA_KV_HEADS + g) * n_sel
    biases = []
    for k in range(n_sel - 1):
        odd = (idx_ref[base + k] & 1) == 1
        biases.append(jnp.where(lo1, jnp.where(odd, NEG, 0.0), jnp.where(odd, 0.0, NEG)))
    biases.append(jnp.full((1, LANES), NEG, F32))
    s_sel = s_sel + jnp.concatenate(biases, axis=1)
    nk = newkv_ref[0].astype(F32)
    s_new = jnp.sum(qs_f * twice(nk[:, :LANES]), axis=-1, keepdims=True)
    o_s = _extra_key_softmax(s_sel, four(s_t[HD:2 * HD]).astype(BF16), s_new, twice(nk[:, LANES:]))

    w = win_ref[0]
    nw = neww_ref[0].astype(F32)
    s_w = _dot(qs, four(w[0:HD]).astype(BF16))
    s_wn = jnp.sum(qs_f * twice(nw[:, :LANES]), axis=-1, keepdims=True)
    o_w = _extra_key_softmax(s_w, four(w[HD:2 * HD]).astype(BF16), s_wn, twice(nw[:, LANES:]))

    o_c = twice(oc_ref[0, 0])
    gate = gate_ref[0]
    out = jnp.zeros((1, width), F32)
    for r in range(NSA_GROUP):
        o_r = (gate[:, 3 * r:3 * r + 1] * o_c[r:r + 1] + gate[:, 3 * r + 1:3 * r + 2] * o_s[r:r + 1]
               + gate[:, 3 * r + 2:3 * r + 3] * o_w[r:r + 1])
        out = out + jnp.where(lane_head[0:1] == r, o_r, 0.0)
    o_ref[0] = out

    n_feat, n_w = w.shape
    new_row = jnp.broadcast_to(newwf_ref[0], (n_feat, n_feat))
    diag = (lax.broadcasted_iota(jnp.int32, (n_feat, n_feat), 0)
            == lax.broadcasted_iota(jnp.int32, (n_feat, n_feat), 1))
    new_col = jnp.sum(jnp.where(diag, new_row, 0.0), axis=1, keepdims=True)
    last = lax.broadcasted_iota(jnp.int32, (n_feat, LANES), 1) == LANES - 1
    chunks = []
    n_ch = n_w // LANES
    for c in range(n_ch):
        cur = pltpu.roll(w[:, c * LANES:(c + 1) * LANES], LANES - 1, 1)
        if c + 1 < n_ch:
            nxt = pltpu.roll(w[:, (c + 1) * LANES:(c + 2) * LANES], LANES - 1, 1)
        else:
            nxt = jnp.broadcast_to(new_col, (n_feat, LANES))
        chunks.append(jnp.where(last, nxt, cur))
    wout_ref[0] = jnp.concatenate(chunks, axis=1)


def _sample_attend(p, oc, idx_flat, pt_flat, slc_t, win_t, b, n_pages):
    page = slc_t.shape[2]
    win_len = win_t.shape[2]
    per_page = page // SEL_BLOCK

    def page_spec(k):
        def imap(bb, g, idx, pt):
            j = idx[(bb * NSA_KV_HEADS + g) * N_SELECT + k]
            return (pt[bb * n_pages + jnp.minimum(j // per_page, n_pages - 1)], g, 0)
        return pl.BlockSpec((1, LANES, page), imap)

    row = lambda w: pl.BlockSpec((1, 1, w), lambda bb, g, idx, pt: (bb, 0, g))
    wspec = pl.BlockSpec((1, LANES, win_len), lambda bb, g, idx, pt: (bb, g, 0))
    gs = pltpu.PrefetchScalarGridSpec(
        num_scalar_prefetch=2, grid=(b, NSA_KV_HEADS),
        in_specs=[page_spec(k) for k in range(N_SELECT)] + [
            row(2 * LANES), row(2 * LANES), row(2 * LANES), row(LANES), wspec, row(LANES),
            pl.BlockSpec((1, 1, PAD_ROWS, LANES), lambda bb, g, idx, pt: (bb, g, 0, 0))],
        out_specs=[row(2 * LANES), wspec])
    return pl.pallas_call(
        _sample_attend_kernel, grid_spec=gs,
        out_shape=[jax.ShapeDtypeStruct((b, 1, NSA_WIDTH), F32),
                   jax.ShapeDtypeStruct((b, KV_WIDTH, win_len), F32)],
        compiler_params=pltpu.CompilerParams(dimension_semantics=("arbitrary",) * 2, vmem_limit_bytes=VMEM_LIMIT),
        name="sample_attend",
    )(idx_flat, pt_flat, *([slc_t] * N_SELECT), p["qrot"], p["slckv"], p["winkv"], p["win"], win_t,
      p["gate"], oc)


def _softmax_update(s, v_b, m, l, acc):
    m_new = jnp.maximum(m, jnp.max(s, axis=-1, keepdims=True))
    p = jnp.exp(s - m_new)
    alpha = jnp.exp(m - m_new)
    return m_new, alpha * l + jnp.sum(p, axis=-1, keepdims=True), alpha * acc + _dot(p.astype(BF16), v_b)


def _mla_sample_kernel(pt_ref, *refs):
    del pt_ref
    n = PAGES_PER_STEP
    lat_pages, kr_pages = refs[:n], refs[n:2 * n]
    qlat_ref, qpe_ref, cnew_ref, krnew_ref, o_ref, m_ref, l_ref, acc_ref = refs[2 * n:]
    step = pl.program_id(1)
    qlat = _pad_rows(qlat_ref[0], PAD_ROWS)
    qpe = _pad_rows(qpe_ref[0], PAD_ROWS)

    @pl.when(step == 0)
    def _():
        c_new = cnew_ref[0]
        s_new = (jnp.sum(qlat * c_new, axis=-1, keepdims=True)
                 + jnp.sum(qpe * krnew_ref[0], axis=-1, keepdims=True))
        m_ref[...] = s_new
        l_ref[...] = jnp.ones(l_ref.shape, F32)
        acc_ref[...] = jnp.broadcast_to(c_new, acc_ref.shape)

    c_b = jnp.concatenate([pg[0] for pg in lat_pages], axis=0).astype(BF16)
    krt_b = jnp.concatenate([pg[0] for pg in kr_pages], axis=1).astype(BF16)
    s = _dot_nt(qlat.astype(BF16), c_b) + _dot(qpe.astype(BF16), krt_b)
    m, l, acc = _softmax_update(s, c_b, m_ref[...], l_ref[...], acc_ref[...])
    m_ref[...] = m
    l_ref[...] = l
    acc_ref[...] = acc

    @pl.when(step == pl.num_programs(1) - 1)
    def _():
        o_ref[0] = (acc / l)[0:MLA_HEADS]


def _mla_sample(p, lat_cache, kr_t, pt_flat, b, n_pages):
    page = lat_cache.shape[1]
    head = lambda w: pl.BlockSpec((1, MLA_HEADS, w), lambda bb, c, pt: (bb, 0, 0))
    row = lambda w: pl.BlockSpec((1, 1, w), lambda bb, c, pt: (bb, 0, 0))
    lat_specs = [pl.BlockSpec((1, page, MLA_KV_LORA),
                              lambda bb, c, pt, k=k: (pt[bb * n_pages + c * PAGES_PER_STEP + k], 0, 0))
                 for k in range(PAGES_PER_STEP)]
    gs = pltpu.PrefetchScalarGridSpec(
        num_scalar_prefetch=1, grid=(b, n_pages // PAGES_PER_STEP),
        in_specs=lat_specs + _page_specs(n_pages, MLA_ROPE, page)
        + [head(MLA_KV_LORA), head(MLA_ROPE), row(MLA_KV_LORA), row(MLA_ROPE)],
        out_specs=head(MLA_KV_LORA),
        scratch_shapes=[pltpu.VMEM((PAD_ROWS, 1), F32), pltpu.VMEM((PAD_ROWS, 1), F32),
                        pltpu.VMEM((PAD_ROWS, MLA_KV_LORA), F32)])
    qlat = p["qlat"].reshape(b, MLA_HEADS, MLA_KV_LORA)
    qpe = p["qpe"].reshape(b, MLA_HEADS, MLA_ROPE)
    return pl.pallas_call(
        _mla_sample_kernel, grid_spec=gs,
        out_shape=jax.ShapeDtypeStruct((b, MLA_HEADS, MLA_KV_LORA), F32),
        compiler_params=pltpu.CompilerParams(dimension_semantics=("arbitrary",) * 2, vmem_limit_bytes=VMEM_LIMIT),
        name="mla_sample",
    )(pt_flat, *([lat_cache] * PAGES_PER_STEP), *([kr_t] * PAGES_PER_STEP), qlat, qpe, p["c"], p["kr"])


def _rope_angles(pos, theta, dim):
    half = dim // 2
    inv = 1.0 / (jnp.float32(theta) ** (jnp.arange(half, dtype=F32) / half))
    ang = pos.astype(F32)[:, None] * inv[None, :]
    return jnp.cos(ang), jnp.sin(ang)


def _rope_lane_tables(pos, rows):
    def table(theta, dim, period, active):
        half = dim // 2
        cos, sin = _rope_angles(pos, theta, dim)
        lane = np.arange(LANES)
        d = lane % period
        is_lo = (d < half) & active(lane)
        is_hi = (d >= half) & (d < dim) & active(lane)
        fi = np.where(d < half, d, np.clip(d - half, 0, half - 1))
        cos_l, sin_l = cos[:, fi], sin[:, fi]
        tab = jnp.stack([jnp.where(is_lo | is_hi, cos_l, 1.0), jnp.where(is_lo, -sin_l, 0.0),
                         jnp.where(is_hi, sin_l, 0.0)])
        return jnp.broadcast_to(tab, (3, rows, LANES))

    every = lambda lane: np.ones_like(lane, bool)
    keys_only = lambda lane: (lane % LANES) < HD
    return (table(ROPE_THETA, ROT_DIM, HD, every), table(ROPE_THETA, ROT_DIM, HD, keys_only),
            table(MLA_ROPE_THETA, MLA_ROPE, MLA_ROPE, every))


def _rope_row_tables(pos):
    cq, sq = _rope_angles(pos, ROPE_THETA, ROT_DIM)
    cm, sm = _rope_angles(pos, MLA_ROPE_THETA, MLA_ROPE)
    return jnp.stack([cq.T, sq.T]), jnp.stack([cm.T, sm.T])


def _pack_weights(l, norm_pre, w_in, pe_cmp, q_norm, w_q_up, kv_norm, w_kv_up, w_proj_a, w_proj_b, w_out, norm_post):
    w = w_in[l]
    o = IN_OFFSETS
    seg = lambda k: w[:, o[k]:o[k + 1]]
    gn = seg(4)
    per_group = 3 * NSA_GROUP
    gn_p = jnp.zeros((D_MODEL, NSA_KV_HEADS * LANES), w.dtype)
    gn_t = jnp.zeros((NSA_KV_HEADS * GATE_ROWS, D_MODEL), w.dtype)
    for g in range(NSA_KV_HEADS):
        gn_g = gn[:, g * per_group:(g + 1) * per_group]
        gn_p = gn_p.at[:, g * LANES:g * LANES + per_group].set(gn_g)
        gn_t = gn_t.at[g * GATE_ROWS:g * GATE_ROWS + per_group].set(gn_g.T)
    w_rows = jnp.concatenate([seg(0), seg(1), seg(2), seg(3), gn_p, seg(5), seg(6), seg(7),
                              jnp.tile(seg(8), (1, LANES // MLA_ROPE)), seg(9), seg(10), seg(11)], axis=1)
    w_t = jnp.concatenate([seg(0).T, seg(1).T, seg(2).T, seg(3).T, gn_t, seg(8).T], axis=0)
    w_cols = jnp.concatenate([seg(5), seg(6), seg(7), seg(9), seg(10), seg(11)], axis=1)
    wq = w_q_up[l]
    wqup = jnp.concatenate([wq[..., :MLA_NOPE].reshape(MLA_Q_LORA, -1), wq[..., MLA_NOPE:].reshape(MLA_Q_LORA, -1)],
                           axis=1)
    wkv = w_kv_up[l]
    wuk_pad = jnp.pad(wkv[..., :MLA_NOPE], ((0, 0), (0, 0), (0, LANES - MLA_NOPE)))
    rope_copy = jnp.pad(jnp.eye(MLA_ROPE, dtype=w.dtype), ((0, LANES - MLA_ROPE), (MLA_NOPE, MLA_ROPE)))
    wkx = jnp.concatenate([wuk_pad.reshape(MLA_KV_LORA, MLA_HEADS * LANES), jnp.tile(rope_copy, (1, MLA_HEADS))],
                          axis=0)
    wvt = jnp.transpose(wkv[..., MLA_NOPE:], (1, 2, 0)).reshape(MLA_WIDTH, MLA_KV_LORA)
    w2uk =jnp.transpose(wkv[..., :MLA_NOPE], (1, 2, 0)).reshape(MLA_HEADS // 2, LANES, MLA_KV_LORA)
    wv = jnp.transpose(wkv[..., MLA_NOPE:], (1, 0, 2))
    zeros = jnp.zeros_like(wv)
    even = (jnp.arange(MLA_HEADS) % 2 == 0)[:, None, None]
    wuv = jnp.concatenate([jnp.where(even, wv, zeros), jnp.where(even, zeros, wv)], axis=2)
    return {
        "npre": norm_pre[l][None].astype(F32), "w_rows": w_rows.astype(BF16), "w_t": w_t.astype(BF16),
        "w_cols": w_cols.astype(BF16), "pe": pe_cmp[l].reshape(CMP_BLOCK, KV_WIDTH).astype(F32),
        "qnorm": q_norm[l][None].astype(F32), "wqup": wqup.astype(BF16), "wqupt": wqup.T.astype(BF16),
        "kvnorm": kv_norm[l][None].astype(F32), "w2uk": w2uk.astype(BF16), "wkx": wkx.astype(BF16),
        "wvt": wvt.astype(BF16), "wuv": wuv.astype(BF16),
        "wpa": w_proj_a[l].astype(BF16), "wpb": w_proj_b[l].astype(BF16), "wout": w_out[l].astype(BF16),
        "npost": norm_post[l][None].astype(F32),
    }


def _rows_from_cols(a):
    b, _, t = a.shape
    return a.reshape(b, NSA_KV_HEADS, 2, HD, t).transpose(0, 4, 1, 2, 3)


def _cols_from_rows(a):
    n, t = a.shape[:2]
    return a.transpose(0, 2, 3, 4, 1).reshape(n, KV_WIDTH, t)


def _prompt_layer(x, wts):
    b, t, _ = x.shape
    assert t % PROMPT_TM == 0
    p = _in_project_cols(x, _rope_row_tables(jnp.arange(t)), wts)
    o_a = _nsa_prompt(p, b, t)
    o_b = _mla_prompt(p, wts, b, t)
    y = _merge(x, o_a, o_b, p, wts, 512, from_latent=False)
    win_keep = min(WINDOW, t)
    return y, (_rows_from_cols(p["cmp"]), _rows_from_cols(p["slc"]), p["c"], p["kr"].transpose(0, 2, 1),
               _rows_from_cols(p["win"][:, :, t - win_keep:]))


def _sample_layer(x, l, caches, state_win, page_table, wts):
    cache_cmp, cache_slc, cache_lat, cache_kr = caches
    b, s_new, _ = x.shape
    assert s_new == 1
    n_pages = page_table.shape[1]
    page = cache_cmp.shape[2]
    past = n_pages * page
    assert past % SEL_BLOCK == 0 and n_pages % PAGES_PER_STEP == 0 and page == LANES
    win_len = state_win.shape[1]
    assert win_len == WINDOW and past >= WINDOW
    pt_flat = page_table.reshape(-1).astype(jnp.int32)

    tabs = _rope_lane_tables(jnp.full((1,), past, jnp.int32), b)
    p = _in_project_rows(x.reshape(1, b, D_MODEL), tabs, wts)
    p = {k: v.reshape(b, 1, v.shape[-1]) for k, v in p.items()}

    kc_all = _compress_pages(_cols_from_rows(cache_cmp[l]), pt_flat, wts["pe"], b, n_pages)
    oc, idx = _sample_select(p["q"], kc_all, b)
    o_a, new_win = _sample_attend(p, oc, idx.reshape(-1), pt_flat, _cols_from_rows(cache_slc[l]),
                                  _cols_from_rows(state_win), b, n_pages)
    o_lat = _mla_sample(p, cache_lat[l], cache_kr[l].transpose(0, 2, 1), pt_flat, b, n_pages)
    pm = {k: p[k].reshape(1, b, -1) for k in ("ga", "gb", "ma", "mb")}
    y = _merge(x.reshape(1, b, D_MODEL), o_a.reshape(1, b, NSA_WIDTH),
               o_lat.reshape(1, b, MLA_HEADS * MLA_KV_LORA), pm, wts, b, from_latent=True)
    kv6 = lambda a: a.reshape(b, 1, NSA_KV_HEADS, 2, HD)
    return y.reshape(b, 1, D_MODEL), (kv6(p["cmp"]), kv6(p["slc"]), p["c"], p["kr"], _rows_from_cols(new_win))


def kernel(x_prompt, x_sample, cache_nsa_cmp, cache_nsa_slc, cache_mla_latent, cache_mla_krope, state_nsa_win,
           page_table, norm_pre, w_in, pe_cmp, q_norm, w_q_up, kv_norm, w_kv_up, w_proj_a, w_proj_b, w_out,
           norm_post):
    depth = w_in.shape[0]
    hp, hs = x_prompt, x_sample
    new_p, new_s = [], []
    for l in range(depth):
        wts = _pack_weights(l, norm_pre, w_in, pe_cmp, q_norm, w_q_up, kv_norm, w_kv_up, w_proj_a, w_proj_b,
                            w_out, norm_post)
        hp, sp = _prompt_layer(hp, wts)
        hs, ss = _sample_layer(hs, l, (cache_nsa_cmp, cache_nsa_slc, cache_mla_latent, cache_mla_krope),
                               state_nsa_win[l], page_table, wts)
        new_p.append(sp)
        new_s.append(ss)
    stack = lambda items, k: jnp.stack([s[k] for s in items])
    return (hp, hs) + tuple(stack(new_p, k) for k in range(5)) + tuple(stack(new_s, k) for k in range(5))
```

```python
import functools

import numpy as np
import jax
import jax.numpy as jnp
from jax import lax
from jax.experimental import pallas as pl
from jax.experimental.pallas import tpu as pltpu

D_MODEL = 1024
NSA_HEADS = 8
NSA_KV_HEADS = 2
NSA_GROUP = NSA_HEADS // NSA_KV_HEADS
HD = 64
NSA_WIDTH = NSA_HEADS * HD
KV_WIDTH = NSA_KV_HEADS * 2 * HD
ROT_DIM = HD // 4
ROPE_THETA = 500000.0
CMP_BLOCK = 32
SEL_BLOCK = 64
N_SELECT = 16
WINDOW = 512
NSA_SCALE = HD ** -0.5

MLA_HEADS = 8
MLA_Q_LORA = 384
MLA_KV_LORA = 256
MLA_NOPE = 64
MLA_ROPE = 32
MLA_V = 64
MLA_WIDTH = MLA_HEADS * MLA_V
MLA_ROPE_THETA = 10000.0
MLA_SCALE = (MLA_NOPE + MLA_ROPE) ** -0.5
LOG2E = 1.4426950408889634

RMS_EPS = 1e-6
NEG = -1e30
FORCE_SCORE = 1e4

IN_SPLITS = (NSA_WIDTH, KV_WIDTH, KV_WIDTH, KV_WIDTH, 3 * NSA_HEADS, NSA_WIDTH,
             MLA_Q_LORA, MLA_KV_LORA, MLA_ROPE, MLA_WIDTH, D_MODEL, D_MODEL)
IN_OFFSETS = tuple(int(v) for v in np.cumsum((0,) + IN_SPLITS))

LANES = 128
HALF = LANES // 2
GATE_ROWS = 16
PAD_ROWS = 16

SEG_A = (0, 1280)
SEG_GN = (1280, 1536)
SEG_GA = (1536, 2048)
SEG_QD = (2048, 2432)
SEG_KVD = (2432, 2688)
SEG_KRP = (2688, 2816)
SEG_GB = (2816, 3328)
SEG_MA = (3328, 4352)
SEG_MB = (4352, 5376)
PACKED_WIDTH = 5376

PT_Q = (0, 512)
PT_CMP = (512, 768)
PT_SLC = (768, 1024)
PT_WIN = (1024, 1280)
PT_GN = (1280, 1280 + NSA_KV_HEADS * GATE_ROWS)
PT_KRP = (PT_GN[1], PT_GN[1] + MLA_ROPE)
PT_ROWS = PT_KRP[1]
PR_GA = (0, 512)
PR_QD = (512, 896)
PR_KVD = (896, 1152)
PR_GB = (1152, 1664)
PR_MA = (1664, 2688)
PR_MB = (2688, 3712)
PR_COLS = 3712

PROMPT_TM = 256
ATT_TQ = 256
ATT_TK = 256
MLA_TK = 256

VMEM_LIMIT = 48 * 1024 * 1024
BF16 = jnp.bfloat16
F32 = jnp.float32


def _full_spec(shape):
    nd = len(shape)
    return pl.BlockSpec(shape, lambda *_: (0,) * nd)


def _lane_lo(rows):
    return lax.broadcasted_iota(jnp.int32, (rows, LANES), 1) < HALF


def _dot(a, b):
    return jnp.dot(a, b, preferred_element_type=F32)


def _dot_nt(a, b):
    return lax.dot_general(a, b, (((1,), (1,)), ((), ())), preferred_element_type=F32)


def _rms(v, gain):
    return v * lax.rsqrt(jnp.mean(v * v, axis=-1, keepdims=True) + RMS_EPS) * gain


def _split_bf16(v):
    hi = v.astype(BF16)
    return hi, (v - hi.astype(F32)).astype(BF16)


def _rope_tiles(v, tab_ref, shift):
    c, s_lo, s_hi = tab_ref[0], tab_ref[1], tab_ref[2]
    out = []
    for k in range(v.shape[1] // LANES):
        a = v[:, k * LANES:(k + 1) * LANES]
        out.append(a * c + pltpu.roll(a, LANES - shift, 1) * s_lo + pltpu.roll(a, shift, 1) * s_hi)
    return out[0] if len(out) == 1 else jnp.concatenate(out, axis=1)


def _dup_kv(a, lo):
    r = pltpu.roll(a, HALF, 1)
    return jnp.where(lo, a, r), jnp.where(lo, r, a)


def _kv_pack(v):
    lo = _lane_lo(v.shape[0])
    parts = []
    for g in range(NSA_KV_HEADS):
        kk, vv = _dup_kv(v[:, g * LANES:(g + 1) * LANES], lo)
        parts += [kk, vv]
    return jnp.concatenate(parts, axis=1).astype(BF16)


def _inproj_rows_kernel(x_ref, npre_ref, w_ref, tq_ref, tkv_ref, tm_ref, qnorm_ref, wqup_ref, kvnorm_ref,
                        w2uk_ref, q_ref, qrot_ref, gate_ref, cmp_ref, slc_ref, win_ref, slckv_ref, winkv_ref,
                        ga_ref, gb_ref, ma_ref, mb_ref, qpe_ref, c_ref, kr_ref, qlat_ref):
    xb = _rms(x_ref[0], npre_ref[...]).astype(BF16)

    def seg(lohi):
        return _dot(xb, w_ref[:, lohi[0]:lohi[1]])

    a = seg(SEG_A)
    q = a[:, :NSA_WIDTH]
    q_ref[0] = (q * NSA_SCALE).astype(BF16)
    qrot_ref[0] = (_rope_tiles(q, tq_ref, ROT_DIM // 2) * NSA_SCALE).astype(BF16)
    cmp_ref[0] = a[:, NSA_WIDTH:NSA_WIDTH + KV_WIDTH]
    kvs = _rope_tiles(a[:, NSA_WIDTH + KV_WIDTH:NSA_WIDTH + 2 * KV_WIDTH], tkv_ref, ROT_DIM // 2)
    slc_ref[0] = kvs
    slckv_ref[0] = _kv_pack(kvs)
    kvw = _rope_tiles(a[:, NSA_WIDTH + 2 * KV_WIDTH:], tkv_ref, ROT_DIM // 2)
    win_ref[0] = kvw
    winkv_ref[0] = _kv_pack(kvw)

    gate_ref[0] = jax.nn.sigmoid(seg(SEG_GN))
    ga_ref[0] = seg(SEG_GA)
    gb_ref[0] = seg(SEG_GB)
    ma_ref[0] = seg(SEG_MA)
    mb_ref[0] = seg(SEG_MB)

    qd = _rms(seg(SEG_QD), qnorm_ref[...]).astype(BF16)
    qh = _dot(qd, wqup_ref[...])
    qn = qh[:, :MLA_HEADS * MLA_NOPE].astype(BF16)
    qpe_ref[0] = _rope_tiles(qh[:, MLA_HEADS * MLA_NOPE:], tm_ref, MLA_ROPE // 2) * MLA_SCALE
    c_ref[0] = _rms(seg(SEG_KVD), kvnorm_ref[...])
    kr_ref[0] = _rope_tiles(seg(SEG_KRP), tm_ref, MLA_ROPE // 2)[:, :MLA_ROPE]

    rows = qn.shape[0]
    lo = _lane_lo(rows)
    z = jnp.zeros((rows, LANES), BF16)
    parts = []
    for j in range(MLA_HEADS // 2):
        pair = qn[:, j * LANES:(j + 1) * LANES]
        parts.append(_dot(jnp.where(lo, pair, z), w2uk_ref[j]) * MLA_SCALE)
        parts.append(_dot(jnp.where(lo, z, pair), w2uk_ref[j]) * MLA_SCALE)
    qlat_ref[0] = jnp.concatenate(parts, axis=1)


def _in_project_rows(x3, tabs, wts):
    b, t, _ = x3.shape
    tq, tkv, tmla = tabs
    row = lambda w, dt: jax.ShapeDtypeStruct((b, t, w), dt)
    names = ["q", "qrot", "gate", "cmp", "slc", "win", "slckv", "winkv", "ga", "gb", "ma", "mb", "qpe", "c", "kr",
             "qlat"]
    out_shape = [row(512, BF16), row(512, BF16), row(256, F32), row(256, F32), row(256, F32), row(256, F32),
                 row(512, BF16), row(512, BF16), row(512, F32), row(512, F32), row(1024, F32), row(1024, F32),
                 row(MLA_HEADS * MLA_ROPE, F32), row(MLA_KV_LORA, F32), row(MLA_ROPE, F32),
                 row(MLA_HEADS * MLA_KV_LORA, F32)]
    tok = lambda w: pl.BlockSpec((1, t, w), lambda bb: (bb, 0, 0))
    tab = _full_spec((3, t, LANES))
    in_specs = [tok(D_MODEL), _full_spec((1, D_MODEL)), _full_spec((D_MODEL, PACKED_WIDTH)), tab, tab, tab,
                _full_spec((1, MLA_Q_LORA)), _full_spec((MLA_Q_LORA, MLA_HEADS * (MLA_NOPE + MLA_ROPE))),
                _full_spec((1, MLA_KV_LORA)), _full_spec((MLA_HEADS // 2, LANES, MLA_KV_LORA))]
    res = pl.pallas_call(
        _inproj_rows_kernel, grid=(b,), in_specs=in_specs, out_specs=[tok(s.shape[2]) for s in out_shape],
        out_shape=out_shape,
        compiler_params=pltpu.CompilerParams(dimension_semantics=("arbitrary",), vmem_limit_bytes=VMEM_LIMIT),
        name="in_project_rows",
    )(x3, wts["npre"], wts["w_rows"], tq, tkv, tmla, wts["qnorm"], wts["wqup"], wts["kvnorm"], wts["w2uk"])
    return dict(zip(names, res))


def _rope_rows(x, cos, sin, half):
    x1, x2 = x[0:half], x[half:2 * half]
    parts = [x1 * cos - x2 * sin, x1 * sin + x2 * cos]
    if x.shape[0] > 2 * half:
        parts.append(x[2 * half:])
    return jnp.concatenate(parts, axis=0)


def _store_chunks(ref, v):
    tk = ref.shape[3]
    for j in range(ref.shape[1]):
        ref[0, j] = v[:, j * tk:(j + 1) * tk]


def _inproj_cols_kernel(x_ref, npre_ref, wt_ref, w_ref, ropeq_ref, ropem_ref, pe_ref, qnorm_ref, wqupt_ref,
                        kvnorm_ref, pool_ref, wkx_ref, wvt_ref,
                        qt_ref, qrt_ref, gate_ref, cmp_ref, slc_ref, win_ref, slcb_ref, winb_ref, slcr_ref,
                        winr_ref, kc_ref, ga_ref, gb_ref, ma_ref, mb_ref, qmt_ref, c_ref, kmla_ref, vmt_ref,
                        kr_ref):
    tm = x_ref.shape[1]
    xb = _rms(x_ref[0], npre_ref[...]).astype(BF16)
    segt = lambda lohi: _dot_nt(wt_ref[lohi[0]:lohi[1], :], xb)
    seg = lambda lohi: _dot(xb, w_ref[:, lohi[0]:lohi[1]])
    cq, sq = ropeq_ref[0], ropeq_ref[1]
    cm, sm = ropem_ref[0], ropem_ref[1]
    hq, hm = ROT_DIM // 2, MLA_ROPE // 2

    qt = segt(PT_Q)
    qt_ref[0] = (qt * (NSA_SCALE * LOG2E)).astype(BF16)
    qrt = jnp.concatenate([_rope_rows(qt[h * HD:(h + 1) * HD], cq, sq, hq) for h in range(NSA_HEADS)], axis=0)
    qrt_ref[0] = (qrt * (NSA_SCALE * LOG2E)).astype(BF16)

    cmpt = segt(PT_CMP)
    cmp_ref[0] = cmpt
    hi, lo = _split_bf16(cmpt)
    pool = pool_ref[...]
    pooled = (_dot_nt(pool, hi) + _dot_nt(pool, lo))[0:tm // CMP_BLOCK]
    kc_ref[0] = (pooled + jnp.sum(pe_ref[...], axis=0, keepdims=True)) * (1.0 / CMP_BLOCK)

    def rope_kv(v):
        parts = []
        for g in range(NSA_KV_HEADS):
            parts.append(_rope_rows(v[g * LANES:g * LANES + HD], cq, sq, hq))
            parts.append(v[g * LANES + HD:(g + 1) * LANES])
        return jnp.concatenate(parts, axis=0)

    slct = rope_kv(segt(PT_SLC))
    slc_ref[0] = slct
    _store_chunks(slcb_ref, slct.astype(BF16))
    slcr_ref[0] = slct.T.astype(BF16)
    wint = rope_kv(segt(PT_WIN))
    win_ref[0] = wint
    _store_chunks(winb_ref, wint.astype(BF16))
    winr_ref[0] = wint.T.astype(BF16)

    gate_ref[0] = jax.nn.sigmoid(segt(PT_GN))
    krt = _rope_rows(segt(PT_KRP), cm, sm, hm)
    kr_ref[0] = krt
    c = _rms(seg(PR_KVD), kvnorm_ref[...])
    c_ref[0] = c
    c_b = c.astype(BF16)
    kr_rows = jnp.concatenate([krt, jnp.zeros((LANES - MLA_ROPE, tm), F32)], axis=0).T
    ckr = jnp.concatenate([c_b, kr_rows.astype(BF16)], axis=1)
    kmla_ref[0] = _dot(ckr, wkx_ref[...]).astype(BF16)
    _store_chunks(vmt_ref, _dot_nt(wvt_ref[...], c_b).astype(BF16))

    qd = _rms(seg(PR_QD), qnorm_ref[...]).astype(BF16)
    qht = _dot_nt(wqupt_ref[...], qd)
    n_nope = MLA_HEADS * MLA_NOPE
    zq = jnp.zeros((LANES - MLA_NOPE - MLA_ROPE, tm), F32)
    parts = []
    for h in range(MLA_HEADS):
        parts += [qht[h * MLA_NOPE:(h + 1) * MLA_NOPE],
                  _rope_rows(qht[n_nope + h * MLA_ROPE:n_nope + (h + 1) * MLA_ROPE], cm, sm, hm), zq]
    qmt_ref[0] = (jnp.concatenate(parts, axis=0) * (MLA_SCALE * LOG2E)).astype(BF16)

    ga_ref[0] = seg(PR_GA)
    gb_ref[0] = seg(PR_GB)
    ma_ref[0] = seg(PR_MA)
    mb_ref[0] = seg(PR_MB)


def _in_project_cols(x3, ropes, wts):
    b, t, _ = x3.shape
    tm = PROMPT_TM
    nt = t // tm
    ropeq, ropem = ropes
    pool = np.zeros((16, tm), np.float32)
    for s in range(tm):
        pool[s // CMP_BLOCK, s] = 1.0
    sds = jax.ShapeDtypeStruct
    rows = lambda w: pl.BlockSpec((1, tm, w), lambda i, bb: (bb, i, 0))
    cols = lambda w: pl.BlockSpec((1, w, tm), lambda i, bb: (bb, 0, i))
    chunk = lambda w, tk: pl.BlockSpec((1, tm // tk, w, tk), lambda i, bb: (bb, i, 0, 0))
    outs = [
        ("qt", sds((b, NSA_WIDTH, t), BF16), cols(NSA_WIDTH)),
        ("qrt", sds((b, NSA_WIDTH, t), BF16), cols(NSA_WIDTH)),
        ("gate", sds((b, NSA_KV_HEADS * GATE_ROWS, t), F32), cols(NSA_KV_HEADS * GATE_ROWS)),
        ("cmp", sds((b, KV_WIDTH, t), F32), cols(KV_WIDTH)),
        ("slc", sds((b, KV_WIDTH, t), F32), cols(KV_WIDTH)),
        ("win", sds((b, KV_WIDTH, t), F32), cols(KV_WIDTH)),
        ("slcb", sds((b, t // ATT_TK, KV_WIDTH, ATT_TK), BF16), chunk(KV_WIDTH, ATT_TK)),
        ("winb", sds((b, t // ATT_TK, KV_WIDTH, ATT_TK), BF16), chunk(KV_WIDTH, ATT_TK)),
        ("slcr", sds((b, t, KV_WIDTH), BF16), rows(KV_WIDTH)),
        ("winr", sds((b, t, KV_WIDTH), BF16), rows(KV_WIDTH)),
        ("kc", sds((b, t // CMP_BLOCK, KV_WIDTH), F32),
         pl.BlockSpec((1, tm // CMP_BLOCK, KV_WIDTH), lambda i, bb: (bb, i, 0))),
        ("ga", sds((b, t, NSA_WIDTH), F32), rows(NSA_WIDTH)),
        ("gb", sds((b, t, MLA_WIDTH), F32), rows(MLA_WIDTH)),
        ("ma", sds((b, t, D_MODEL), F32), rows(D_MODEL)),
        ("mb", sds((b, t, D_MODEL), F32), rows(D_MODEL)),
        ("qmt", sds((b, MLA_HEADS * LANES, t), BF16), cols(MLA_HEADS * LANES)),
        ("c", sds((b, t, MLA_KV_LORA), F32), rows(MLA_KV_LORA)),
        ("kmla", sds((b, t, MLA_HEADS * LANES), BF16), rows(MLA_HEADS * LANES)),
        ("vmt", sds((b, t // MLA_TK, MLA_WIDTH, MLA_TK), BF16), chunk(MLA_WIDTH, MLA_TK)),
        ("kr", sds((b, MLA_ROPE, t), F32), cols(MLA_ROPE)),
    ]
    rope_spec = lambda half: pl.BlockSpec((2, half, tm), lambda i, bb: (0, 0, i))
    in_specs = [rows(D_MODEL), _full_spec((1, D_MODEL)), _full_spec((PT_ROWS, D_MODEL)),
                _full_spec((D_MODEL, PR_COLS)), rope_spec(ROT_DIM // 2), rope_spec(MLA_ROPE // 2),
                _full_spec((CMP_BLOCK, KV_WIDTH)), _full_spec((1, MLA_Q_LORA)),
                _full_spec((MLA_HEADS * (MLA_NOPE + MLA_ROPE), MLA_Q_LORA)), _full_spec((1, MLA_KV_LORA)),
                _full_spec((16, tm)), _full_spec((MLA_KV_LORA + LANES, MLA_HEADS * LANES)),
                _full_spec((MLA_WIDTH, MLA_KV_LORA))]
    res = pl.pallas_call(
        _inproj_cols_kernel, grid=(nt, b), in_specs=in_specs, out_specs=[o[2] for o in outs],
        out_shape=[o[1] for o in outs],
        compiler_params=pltpu.CompilerParams(dimension_semantics=("arbitrary", "arbitrary"),
                                             vmem_limit_bytes=VMEM_LIMIT),
        name="in_project_cols",
    )(x3, wts["npre"], wts["w_t"], wts["w_cols"], ropeq, ropem, wts["pe"], wts["qnorm"], wts["wqupt"],
      wts["kvnorm"], jnp.asarray(pool, BF16), wts["wkx"], wts["wvt"])
    return dict(zip([o[0] for o in outs], res))


def _online_update(s, vt, m, l, acc):
    d, keys = vt.shape
    m_new = jnp.maximum(m, jnp.max(s, axis=0, keepdims=True))
    p = jnp.exp2(s - m_new).astype(BF16)
    alpha = jnp.exp2(m - m_new)
    pv = _dot(jnp.concatenate([vt, jnp.ones((PAD_ROWS, keys), BF16)], axis=0), p)
    return m_new, alpha * l + pv[d:d + 1], alpha * acc + pv[0:d]


def _nsa_prompt_kernel(n_blk, qt_ref, qrt_ref, gate_ref, kc_ref, ks_ref, kst_ref, kw_ref, kwt_ref, o_ref, sb_ref):
    i = pl.program_id(2)
    tq, tk = ATT_TQ, ATT_TK
    nl = NSA_GROUP * tq
    zq = jnp.zeros((HD, tq), BF16)

    def widen(qt):
        return jnp.concatenate([jnp.concatenate([qt[r * HD:(r + 1) * HD], zq], axis=0)
                                for r in range(NSA_GROUP)], axis=1)

    qc = widen(qt_ref[0])
    qr = widen(qrt_ref[0])

    n_cmp = 2 * n_blk
    kc = kc_ref[0]
    s_c = _dot(kc.astype(BF16), qc)
    rho = lax.broadcasted_iota(jnp.int32, (n_cmp, nl), 0)
    cmp_idx = jnp.where(rho < n_blk, 2 * rho, 2 * (rho - n_blk) + 1)
    t_l = i * tq + (lax.broadcasted_iota(jnp.int32, (n_cmp, nl), 1) & (tq - 1))
    mask_c = cmp_idx * CMP_BLOCK + (CMP_BLOCK - 1) <= t_l
    s_c = jnp.where(mask_c, s_c, NEG)
    e_c = jnp.where(mask_c, jnp.exp2(s_c - jnp.max(s_c, axis=0, keepdims=True)), 0.0)
    p_c = e_c / jnp.maximum(jnp.sum(e_c, axis=0, keepdims=True), 1e-30)
    kct = jnp.concatenate([kc, jnp.zeros((LANES - n_cmp, LANES), F32)], axis=0).T
    p_pad = jnp.concatenate([p_c, jnp.zeros((LANES - n_cmp, nl), F32)], axis=0)
    o_c = _dot(kct[HD:2 * HD].astype(BF16), p_pad.astype(BF16))

    imp = p_c[:, 0:tq]
    for r in range(1, NSA_GROUP):
        imp = imp + p_c[:, r * tq:(r + 1) * tq]
    imp_blk = imp[0:n_blk] + imp[n_blk:n_cmp]
    blk = lax.broadcasted_iota(jnp.int32, (n_blk, tq), 0)
    t_q = i * tq + lax.broadcasted_iota(jnp.int32, (n_blk, tq), 1)
    valid = blk * SEL_BLOCK <= t_q
    in_blk = (blk * SEL_BLOCK <= t_q) & (t_q < (blk + 1) * SEL_BLOCK)
    forced = in_blk | (blk == 0)
    score = jnp.where(forced, FORCE_SCORE, jnp.where(valid, imp_blk, -FORCE_SCORE))
    rank = jnp.zeros((n_blk, tq), jnp.int32)
    for j in range(n_blk):
        other = score[j:j + 1, :]
        ahead = (other > score) | ((other == score) & (j < blk))
        rank = rank + jnp.where(ahead, 1, 0)
    sel = (rank < N_SELECT) & (score > -1.0)
    sb_ref[...] = jnp.where(sel, 0.0, NEG)

    key_r = lax.broadcasted_iota(jnp.int32, (tk, tq), 0)
    t_k = i * tq + lax.broadcasted_iota(jnp.int32, (tk, tq), 1)
    n_chunks = ((i + 1) * tq + tk - 1) // tk
    c_lo = jnp.maximum(i * tq - WINDOW, 0) // tk
    init = (jnp.full((1, nl), NEG, F32), jnp.zeros((1, nl), F32), jnp.zeros((HD, nl), F32))
    per_chunk = tk // SEL_BLOCK
    tile = lambda bias: jnp.concatenate([bias] * NSA_GROUP, axis=1)

    def sel_scores(c, base):
        s = _dot(ks_ref[0, pl.ds(base, tk), :], qr)
        rows = [jnp.broadcast_to(sb_ref[pl.ds(c * per_chunk + j, 1), :], (SEL_BLOCK, tq)) for j in range(per_chunk)]
        return s, jnp.concatenate(rows, axis=0)

    def past_body(c, carry):
        base = pl.multiple_of(c * tk, tk)
        s, bias = sel_scores(c, base)
        return _online_update(s + tile(bias), kst_ref[0, c, HD:2 * HD, :], *carry)

    def near_body(c, carry):
        base = pl.multiple_of(c * tk, tk)
        s, bias = sel_scores(c, base)
        w = _dot(kw_ref[0, pl.ds(base, tk), :], qr)
        dist = t_k - (base + key_r)
        causal = jnp.where(dist >= 0, 0.0, NEG)
        s = s + tile(bias + causal)
        w = w + tile(jnp.where(dist <= WINDOW, causal, NEG))
        return (_online_update(s, kst_ref[0, c, HD:2 * HD, :], *carry[:3])
                + _online_update(w, kwt_ref[0, c, HD:2 * HD, :], *carry[3:]))

    far = lax.fori_loop(0, c_lo, past_body, init)
    _, l_s, a_s, _, l_w, a_w = lax.fori_loop(c_lo, n_chunks, near_body, far + init)
    o_s = a_s / l_s
    o_w = a_w / l_w

    gate = gate_ref[0]
    heads = []
    for r in range(NSA_GROUP):
        sl = slice(r * tq, (r + 1) * tq)
        heads.append(gate[3 * r:3 * r + 1] * o_c[:, sl] + gate[3 * r + 1:3 * r + 2] * o_s[:, sl]
                     + gate[3 * r + 2:3 * r + 3] * o_w[:, sl])
    o_ref[0] = jnp.concatenate(heads, axis=0).T


def _nsa_prompt(p, b, t):
    tq, tk = ATT_TQ, ATT_TK
    n_blk = t // SEL_BLOCK
    assert 2 * n_blk <= LANES and t % tk == 0
    kc = p["kc"].reshape(b, n_blk, 2, KV_WIDTH).transpose(0, 2, 1, 3).reshape(b, 2 * n_blk, KV_WIDTH)
    qspec = pl.BlockSpec((1, NSA_GROUP * HD, tq), lambda bb, g, i: (bb, g, i))
    rm = pl.BlockSpec((1, t, LANES), lambda bb, g, i: (bb, 0, g))
    fm = pl.BlockSpec((1, t // tk, LANES, tk), lambda bb, g, i: (bb, 0, g, 0))
    return pl.pallas_call(
        functools.partial(_nsa_prompt_kernel, n_blk),
        grid=(b, NSA_KV_HEADS, t // tq),
        in_specs=[qspec, qspec, pl.BlockSpec((1, GATE_ROWS, tq), lambda bb, g, i: (bb, g, i)),
                  pl.BlockSpec((1, 2 * n_blk, LANES), lambda bb, g, i: (bb, 0, g)), rm, fm, rm, fm],
        out_specs=pl.BlockSpec((1, tq, NSA_GROUP * HD), lambda bb, g, i: (bb, i, g)),
        out_shape=jax.ShapeDtypeStruct((b, t, NSA_WIDTH), F32),
        scratch_shapes=[pltpu.VMEM((n_blk, tq), F32)],
        compiler_params=pltpu.CompilerParams(dimension_semantics=("arbitrary",) * 3, vmem_limit_bytes=VMEM_LIMIT),
        name="nsa_prompt",
    )(p["qt"], p["qrt"], p["gate"], kc, p["slcr"], p["slcb"], p["winr"], p["winb"])


MLA_TQ = 256


def _mla_prompt_kernel(qmt_ref, k_ref, vt_ref, o_ref, m_ref, l_ref, acc_ref):
    i = pl.program_id(1)
    tq, tk = MLA_TQ, MLA_TK
    m_ref[...] = jnp.full(m_ref.shape, NEG, F32)
    l_ref[...] = jnp.zeros(l_ref.shape, F32)
    acc_ref[...] = jnp.zeros(acc_ref.shape, F32)
    key_r = lax.broadcasted_iota(jnp.int32, (tk, tq), 0)
    t_k = i * tq + lax.broadcasted_iota(jnp.int32, (tk, tq), 1)
    n_chunks = ((i + 1) * tq + tk - 1) // tk

    def step(c, masked):
        base = pl.multiple_of(c * tk, tk)
        old = [(m_ref[h], l_ref[h], acc_ref[h]) for h in range(MLA_HEADS)]
        scores = []
        for h in range(MLA_HEADS):
            k_h = k_ref[0, pl.ds(base, tk), h * LANES:(h + 1) * LANES]
            s = _dot(k_h, qmt_ref[0, h * LANES:(h + 1) * LANES, :])
            scores.append(jnp.where(base + key_r <= t_k, s, NEG) if masked else s)
        new = [_online_update(scores[h], vt_ref[0, c, h * MLA_V:(h + 1) * MLA_V, :], *old[h])
               for h in range(MLA_HEADS)]
        for h in range(MLA_HEADS):
            m_ref[h], l_ref[h], acc_ref[h] = new[h]
        return 0

    n_past = (i * tq + 1) // tk
    lax.fori_loop(0, n_past, lambda c, z: step(c, False), 0)
    lax.fori_loop(n_past, n_chunks, lambda c, z: step(c, True), 0)
    outs = [acc_ref[h] / l_ref[h] for h in range(MLA_HEADS)]
    o_ref[0] = jnp.concatenate(outs, axis=0).T


def _mla_prompt(p, wts, b, t):
    tq, tk = MLA_TQ, MLA_TK
    assert t % tq == 0
    return pl.pallas_call(
        _mla_prompt_kernel,
        grid=(b, t // tq),
        in_specs=[pl.BlockSpec((1, MLA_HEADS * LANES, tq), lambda bb, i: (bb, 0, i)),
                  pl.BlockSpec((1, t, MLA_HEADS * LANES), lambda bb, i: (bb, 0, 0)),
                  pl.BlockSpec((1, t // tk, MLA_WIDTH, tk), lambda bb, i: (bb, 0, 0, 0))],
        out_specs=pl.BlockSpec((1, tq, MLA_WIDTH), lambda bb, i: (bb, i, 0)),
        out_shape=jax.ShapeDtypeStruct((b, t, MLA_WIDTH), F32),
        scratch_shapes=[pltpu.VMEM((MLA_HEADS, 1, tq), F32), pltpu.VMEM((MLA_HEADS, 1, tq), F32),
                        pltpu.VMEM((MLA_HEADS, MLA_V, tq), F32)],
        compiler_params=pltpu.CompilerParams(dimension_semantics=("arbitrary",) * 2, vmem_limit_bytes=VMEM_LIMIT),
        name="mla_prompt",
    )(p["qmt"], p["kmla"], p["vmt"])


def _merge_kernel(from_latent, x_ref, oa_ref, ob_ref, ga_ref, gb_ref, ma_ref, mb_ref, wpa_ref, wpb_ref, wout_ref,
                  npost_ref, wuv_ref, y_ref):
    if from_latent:
        lat = ob_ref[0].astype(BF16)
        parts = []
        for j in range(MLA_HEADS // 2):
            parts.append(_dot(lat[:, 2 * j * MLA_KV_LORA:(2 * j + 1) * MLA_KV_LORA], wuv_ref[2 * j])
                         + _dot(lat[:, (2 * j + 1) * MLA_KV_LORA:(2 * j + 2) * MLA_KV_LORA], wuv_ref[2 * j + 1]))
        o_b = jnp.concatenate(parts, axis=1)
    else:
        o_b = ob_ref[0]
    ga = ga_ref[0]
    gb = gb_ref[0]
    pa = _dot((oa_ref[0] * (ga * jax.nn.sigmoid(ga))).astype(BF16), wpa_ref[...])
    pb = _dot((o_b * (gb * jax.nn.sigmoid(gb))).astype(BF16), wpb_ref[...])
    h = jax.nn.sigmoid(ma_ref[0]) * pa + jax.nn.sigmoid(mb_ref[0]) * pb
    z = _dot(h.astype(BF16), wout_ref[...])
    y_ref[0] = x_ref[0] + _rms(z, npost_ref[...])


def _merge(x3, o_a, o_b, p, wts, tm, from_latent):
    b, t, _ = x3.shape
    tok = lambda w: pl.BlockSpec((1, tm, w), lambda bb, i: (bb, i, 0))
    return pl.pallas_call(
        functools.partial(_merge_kernel, from_latent),
        grid=(b, t // tm),
        in_specs=[tok(D_MODEL), tok(NSA_WIDTH), tok(o_b.shape[2]), tok(NSA_WIDTH), tok(MLA_WIDTH), tok(D_MODEL),
                  tok(D_MODEL), _full_spec((NSA_WIDTH, D_MODEL)), _full_spec((MLA_WIDTH, D_MODEL)),
                  _full_spec((D_MODEL, D_MODEL)), _full_spec((1, D_MODEL)),
                  _full_spec((MLA_HEADS, MLA_KV_LORA, LANES))],
        out_specs=tok(D_MODEL),
        out_shape=jax.ShapeDtypeStruct((b, t, D_MODEL), F32),
        compiler_params=pltpu.CompilerParams(dimension_semantics=("arbitrary",) * 2, vmem_limit_bytes=VMEM_LIMIT),
        name="merge",
    )(x3, o_a, o_b, p["ga"], p["gb"], p["ma"], p["mb"], wts["wpa"], wts["wpb"], wts["wout"], wts["npost"],
      wts["wuv"])


PAGES_PER_STEP = 32
POOL_PAGES = 4


def _page_copy(pt_ref, cache_ref, buf_ref, sem_ref, step, slot, k):
    return pltpu.make_async_copy(cache_ref.at[pt_ref[step * PAGES_PER_STEP + k]], buf_ref.at[slot, k],
                                 sem_ref.at[slot])


def _paged_pipeline(pt_ref, streams):
    step = pl.program_id(0) * pl.num_programs(1) + pl.program_id(1)
    last = pl.num_programs(0) * pl.num_programs(1) - 1
    slot = step % 2

    @pl.when(step == 0)
    def _():
        for cache_ref, buf_ref, sem_ref in streams:
            for k in range(PAGES_PER_STEP):
                _page_copy(pt_ref, cache_ref, buf_ref, sem_ref, 0, 0, k).start()

    @pl.when(step < last)
    def _():
        for cache_ref, buf_ref, sem_ref in streams:
            for k in range(PAGES_PER_STEP):
                _page_copy(pt_ref, cache_ref, buf_ref, sem_ref, step + 1, 1 - slot, k).start()

    for cache_ref, buf_ref, sem_ref in streams:
        for k in range(PAGES_PER_STEP):
            _page_copy(pt_ref, cache_ref, buf_ref, sem_ref, step, slot, k).wait()
    return slot


def _compress_pages_kernel(pt_ref, cache_ref, pe_ref, o_ref, buf_ref, sem_ref):
    slot = _paged_pipeline(pt_ref, [(cache_ref, buf_ref, sem_ref)])
    pe_sum = jnp.sum(pe_ref[...], axis=0, keepdims=True)
    pooled = []
    for k in range(PAGES_PER_STEP):
        rows = buf_ref[slot, k].T
        pooled.append(jnp.sum(rows.reshape(rows.shape[0] // CMP_BLOCK, CMP_BLOCK, KV_WIDTH), axis=1))
    o_ref[0] = (jnp.concatenate(pooled, axis=0) + pe_sum) * (1.0 / CMP_BLOCK)


def _compress_pages(cache_t, pt_flat, pe, b, n_pages):
    page = cache_t.shape[2]
    per_step = PAGES_PER_STEP * page // CMP_BLOCK
    gs = pltpu.PrefetchScalarGridSpec(
        num_scalar_prefetch=1, grid=(b, n_pages // PAGES_PER_STEP),
        in_specs=[pl.BlockSpec(memory_space=pl.ANY),
                  pl.BlockSpec((CMP_BLOCK, KV_WIDTH), lambda bb, c, pt: (0, 0))],
        out_specs=pl.BlockSpec((1, per_step, KV_WIDTH), lambda bb, c, pt: (bb, c, 0)),
        scratch_shapes=[pltpu.VMEM((2, PAGES_PER_STEP, KV_WIDTH, page), F32), pltpu.SemaphoreType.DMA((2,))])
    return pl.pallas_call(
        _compress_pages_kernel, grid_spec=gs,
        out_shape=jax.ShapeDtypeStruct((b, n_pages * page // CMP_BLOCK, KV_WIDTH), F32),
        compiler_params=pltpu.CompilerParams(dimension_semantics=("arbitrary",) * 2, vmem_limit_bytes=VMEM_LIMIT),
        name="compress_pages",
    )(pt_flat, cache_t, pe)


def _pad_rows(v, rows):
    return jnp.concatenate([v, jnp.zeros((rows - v.shape[0], v.shape[1]), v.dtype)], axis=0)


def _stack_heads(qv, lo):
    a, b = qv[:, :LANES], qv[:, LANES:]
    z = jnp.zeros_like(a)
    return jnp.concatenate([jnp.where(lo, a, z), jnp.where(lo, z, a),
                            jnp.where(lo, b, z), jnp.where(lo, z, b)], axis=0)


def _sample_select_kernel(n_cmp, q_ref, kc_ref, oc_ref, idx_ref):
    lo1 = _lane_lo(1)
    lo_c = _lane_lo(n_cmp)
    q = q_ref[0]
    kc = kc_ref[0]
    lane = lax.broadcasted_iota(jnp.int32, (1, n_cmp), 1)
    blk = lane >> 1
    slot = lax.broadcasted_iota(jnp.int32, (1, N_SELECT), 1)
    for g in range(NSA_KV_HEADS):
        kk, vv = _dup_kv(kc[:, g * LANES:(g + 1) * LANES], lo_c)
        qs = _pad_rows(_stack_heads(q[:, g * 2 * LANES:(g + 1) * 2 * LANES], lo1), PAD_ROWS)
        s = _dot_nt(qs, kk.astype(BF16))
        e = jnp.exp(s - jnp.max(s, axis=-1, keepdims=True))
        p = e / jnp.sum(e, axis=-1, keepdims=True)
        oc_ref[0, g] = _dot(p.astype(BF16), vv.astype(BF16))
        imp = p[0:1] + p[1:2] + p[2:3] + p[3:4]
        chunks = []
        for k in range(n_cmp // LANES):
            a = imp[:, k * LANES:(k + 1) * LANES]
            ev = (lax.broadcasted_iota(jnp.int32, (1, LANES), 1) & 1) == 0
            chunks.append(a + jnp.where(ev, pltpu.roll(a, LANES - 1, 1), pltpu.roll(a, 1, 1)))
        pair = jnp.concatenate(chunks, axis=1)
        v = jnp.where(blk == 0, -1.0, pair)
        idx = jnp.where(slot == N_SELECT - 1, n_cmp // 2, 0)
        for k in range(1, N_SELECT - 1):
            top = jnp.max(v, axis=-1, keepdims=True)
            jmin = jnp.min(jnp.where(v == top, blk.astype(F32), float(n_cmp)), axis=-1,
                           keepdims=True).astype(jnp.int32)
            idx = jnp.where(slot == k, jmin, idx)
            v = jnp.where(blk == jmin, -1.0, v)
        idx_ref[0, g:g + 1, :] = idx


def _sample_select(q, kc_all, b):
    n_cmp = kc_all.shape[1]
    return pl.pallas_call(
        functools.partial(_sample_select_kernel, n_cmp),
        grid=(b,),
        in_specs=[pl.BlockSpec((1, 1, NSA_WIDTH), lambda bb: (bb, 0, 0)),
                  pl.BlockSpec((1, n_cmp, KV_WIDTH), lambda bb: (bb, 0, 0))],
        out_specs=[pl.BlockSpec((1, NSA_KV_HEADS, PAD_ROWS, LANES), lambda bb: (bb, 0, 0, 0)),
                   pl.BlockSpec((1, NSA_KV_HEADS, N_SELECT), lambda bb: (bb, 0, 0))],
        out_shape=[jax.ShapeDtypeStruct((b, NSA_KV_HEADS, PAD_ROWS, LANES), F32),
                   jax.ShapeDtypeStruct((b, NSA_KV_HEADS, N_SELECT), jnp.int32)],
        compiler_params=pltpu.CompilerParams(dimension_semantics=("arbitrary",), vmem_limit_bytes=VMEM_LIMIT),
        name="sample_select",
    )(q, kc_all)


def _extra_key_softmax(s_past, vt4_b, s_new, v_new):
    m = jnp.maximum(jnp.max(s_past, axis=-1, keepdims=True), s_new)
    e = jnp.exp(s_past - m)
    e_new = jnp.exp(s_new - m)
    den = jnp.sum(e, axis=-1, keepdims=True) + e_new
    return (_dot_nt(e.astype(BF16), vt4_b) + e_new * v_new) / den


def _sample_attend_kernel(idx_ref, pt_ref, *refs):
    del pt_ref
    n_sel = N_SELECT
    pages = refs[:n_sel]
    qr_ref, newkv_ref, neww_ref, newwf_ref, win_ref, gate_ref, oc_ref, o_ref, wout_ref = refs[n_sel:]
    bb, g = pl.program_id(0), pl.program_id(1)
    width = NSA_GROUP * HD
    lane_head = lax.broadcasted_iota(jnp.int32, (PAD_ROWS, width), 1) // HD
    row = lax.broadcasted_iota(jnp.int32, (PAD_ROWS, width), 0)
    own = lane_head == row
    qs_f = jnp.where(own, jnp.broadcast_to(qr_ref[0].astype(F32), (PAD_ROWS, width)), 0.0)
    qs = qs_f.astype(BF16)
    four = lambda a: jnp.concatenate([a] * NSA_GROUP, axis=0)
    twice = lambda a: jnp.concatenate([a, a], axis=1)

    s_t = jnp.concatenate([pg[0] for pg in pages], axis=1)
    s_sel = _dot(qs, four(s_t[0:HD]).astype(BF16))
    lo1 = _lane_lo(1)
    base = (bb * NSA_KV_HEADS + g) * n_sel
    biases = []
    for k in range(n_sel - 1):
        odd = (idx_ref[base + k] & 1) == 1
        biases.append(jnp.where(lo1, jnp.where(odd, NEG, 0.0), jnp.where(odd, 0.0, NEG)))
    biases.append(jnp.full((1, LANES), NEG, F32))
    s_sel = s_sel + jnp.concatenate(biases, axis=1)
    nk = newkv_ref[0].astype(F32)
    s_new = jnp.sum(qs_f * twice(nk[:, :LANES]), axis=-1, keepdims=True)
    o_s = _extra_key_softmax(s_sel, four(s_t[HD:2 * HD]).astype(BF16), s_new, twice(nk[:, LANES:]))

    w = win_ref[0]
    nw = neww_ref[0].astype(F32)
    s_w = _dot(qs, four(w[0:HD]).astype(BF16))
    s_wn = jnp.sum(qs_f * twice(nw[:, :LANES]), axis=-1, keepdims=True)
    o_w = _extra_key_softmax(s_w, four(w[HD:2 * HD]).astype(BF16), s_wn, twice(nw[:, LANES:]))

    o_c = twice(oc_ref[0, 0])
    gate = gate_ref[0]
    out = jnp.zeros((1, width), F32)
    for r in range(NSA_GROUP):
        o_r = (gate[:, 3 * r:3 * r + 1] * o_c[r:r + 1] + gate[:, 3 * r + 1:3 * r + 2] * o_s[r:r + 1]
               + gate[:, 3 * r + 2:3 * r + 3] * o_w[r:r + 1])
        out = out + jnp.where(lane_head[0:1] == r, o_r, 0.0)
    o_ref[0] = out

    n_feat, n_w = w.shape
    new_row = jnp.broadcast_to(newwf_ref[0], (n_feat, n_feat))
    diag = (lax.broadcasted_iota(jnp.int32, (n_feat, n_feat), 0)
            == lax.broadcasted_iota(jnp.int32, (n_feat, n_feat), 1))
    new_col = jnp.sum(jnp.where(diag, new_row, 0.0), axis=1, keepdims=True)
    last = lax.broadcasted_iota(jnp.int32, (n_feat, LANES), 1) == LANES - 1
    chunks = []
    n_ch = n_w // LANES
    for c in range(n_ch):
        cur = pltpu.roll(w[:, c * LANES:(c + 1) * LANES], LANES - 1, 1)
        if c + 1 < n_ch:
            nxt = pltpu.roll(w[:, (c + 1) * LANES:(c + 2) * LANES], LANES - 1, 1)
        else:
            nxt = jnp.broadcast_to(new_col, (n_feat, LANES))
        chunks.append(jnp.where(last, nxt, cur))
    wout_ref[0] = jnp.concatenate(chunks, axis=1)


def _sample_attend(p, oc, idx_flat, pt_flat, slc_t, win_t, b, n_pages):
    page = slc_t.shape[2]
    win_len = win_t.shape[2]
    per_page = page // SEL_BLOCK

    def page_spec(k):
        def imap(bb, g, idx, pt):
            j = idx[(bb * NSA_KV_HEADS + g) * N_SELECT + k]
            return (pt[bb * n_pages + jnp.minimum(j // per_page, n_pages - 1)], g, 0)
        return pl.BlockSpec((1, LANES, page), imap)

    row = lambda w: pl.BlockSpec((1, 1, w), lambda bb, g, idx, pt: (bb, 0, g))
    wspec = pl.BlockSpec((1, LANES, win_len), lambda bb, g, idx, pt: (bb, g, 0))
    gs = pltpu.PrefetchScalarGridSpec(
        num_scalar_prefetch=2, grid=(b, NSA_KV_HEADS),
        in_specs=[page_spec(k) for k in range(N_SELECT)] + [
            row(2 * LANES), row(2 * LANES), row(2 * LANES), row(LANES), wspec, row(LANES),
            pl.BlockSpec((1, 1, PAD_ROWS, LANES), lambda bb, g, idx, pt: (bb, g, 0, 0))],
        out_specs=[row(2 * LANES), wspec])
    return pl.pallas_call(
        _sample_attend_kernel, grid_spec=gs,
        out_shape=[jax.ShapeDtypeStruct((b, 1, NSA_WIDTH), F32),
                   jax.ShapeDtypeStruct((b, KV_WIDTH, win_len), F32)],
        compiler_params=pltpu.CompilerParams(dimension_semantics=("arbitrary",) * 2, vmem_limit_bytes=VMEM_LIMIT),
        name="sample_attend",
    )(idx_flat, pt_flat, *([slc_t] * N_SELECT), p["qrot"], p["slckv"], p["winkv"], p["win"], win_t,
      p["gate"], oc)


def _softmax_update(s, v_b, m, l, acc):
    m_new = jnp.maximum(m, jnp.max(s, axis=-1, keepdims=True))
    p = jnp.exp(s - m_new)
    alpha = jnp.exp(m - m_new)
    return m_new, alpha * l + jnp.sum(p, axis=-1, keepdims=True), alpha * acc + _dot(p.astype(BF16), v_b)


def _mla_sample_kernel(pt_ref, lat_ref, krt_ref, qlat_ref, qpe_ref, cnew_ref, krnew_ref, o_ref,
                       m_ref, l_ref, acc_ref, lat_buf, kr_buf, lat_sem, kr_sem):
    n = PAGES_PER_STEP
    slot = _paged_pipeline(pt_ref, [(lat_ref, lat_buf, lat_sem), (krt_ref, kr_buf, kr_sem)])
    lat_pages = [lat_buf.at[slot, k] for k in range(n)]
    kr_pages = [kr_buf.at[slot, k] for k in range(n)]
    step = pl.program_id(1)
    qlat = _pad_rows(qlat_ref[0], PAD_ROWS)
    qpe = _pad_rows(qpe_ref[0], PAD_ROWS)

    @pl.when(step == 0)
    def _():
        c_new = cnew_ref[0]
        s_new = (jnp.sum(qlat * c_new, axis=-1, keepdims=True)
                 + jnp.sum(qpe * krnew_ref[0], axis=-1, keepdims=True))
        m_ref[...] = s_new
        l_ref[...] = jnp.ones(l_ref.shape, F32)
        acc_ref[...] = jnp.broadcast_to(c_new, acc_ref.shape)

    qlat_b, qpe_b = qlat.astype(BF16), qpe.astype(BF16)
    groups = range(0, n, POOL_PAGES)
    c_bs = [jnp.concatenate([pg[...] for pg in lat_pages[k:k + POOL_PAGES]], axis=0).astype(BF16) for k in groups]
    kr_bs = [jnp.concatenate([pg[...] for pg in kr_pages[k:k + POOL_PAGES]], axis=1).astype(BF16) for k in groups]
    s_lat = [_dot_nt(qlat_b, c_b) for c_b in c_bs]
    s_pe = [_dot(qpe_b, kr_b) for kr_b in kr_bs]
    scores = [a + r for a, r in zip(s_lat, s_pe)]
    maxes = [jnp.max(s, axis=-1, keepdims=True) for s in scores]
    ps = [jnp.exp(s - mx) for s, mx in zip(scores, maxes)]
    sums = [jnp.sum(p_g, axis=-1, keepdims=True) for p_g in ps]
    accs = [_dot(p_g.astype(BF16), c_b) for p_g, c_b in zip(ps, c_bs)]
    m_old = m_ref[...]
    m = m_old
    for mx in maxes:
        m = jnp.maximum(m, mx)
    alpha = jnp.exp(m_old - m)
    l = alpha * l_ref[...]
    acc = alpha * acc_ref[...]
    for mx, l_g, acc_g in zip(maxes, sums, accs):
        w = jnp.exp(mx - m)
        l = l + w * l_g
        acc = acc + w * acc_g
    m_ref[...] = m
    l_ref[...] = l
    acc_ref[...] = acc

    @pl.when(step == pl.num_programs(1) - 1)
    def _():
        o_ref[0] = (acc / l)[0:MLA_HEADS]


def _mla_sample(p, lat_cache, kr_t, pt_flat, b, n_pages):
    page = lat_cache.shape[1]
    head = lambda w: pl.BlockSpec((1, MLA_HEADS, w), lambda bb, c, pt: (bb, 0, 0))
    row = lambda w: pl.BlockSpec((1, 1, w), lambda bb, c, pt: (bb, 0, 0))
    gs = pltpu.PrefetchScalarGridSpec(
        num_scalar_prefetch=1, grid=(b, n_pages // PAGES_PER_STEP),
        in_specs=[pl.BlockSpec(memory_space=pl.ANY), pl.BlockSpec(memory_space=pl.ANY),
                  head(MLA_KV_LORA), head(MLA_ROPE), row(MLA_KV_LORA), row(MLA_ROPE)],
        out_specs=head(MLA_KV_LORA),
        scratch_shapes=[pltpu.VMEM((PAD_ROWS, 1), F32), pltpu.VMEM((PAD_ROWS, 1), F32),
                        pltpu.VMEM((PAD_ROWS, MLA_KV_LORA), F32),
                        pltpu.VMEM((2, PAGES_PER_STEP, page, MLA_KV_LORA), F32),
                        pltpu.VMEM((2, PAGES_PER_STEP, MLA_ROPE, page), F32),
                        pltpu.SemaphoreType.DMA((2,)), pltpu.SemaphoreType.DMA((2,))])
    qlat = p["qlat"].reshape(b, MLA_HEADS, MLA_KV_LORA)
    qpe = p["qpe"].reshape(b, MLA_HEADS, MLA_ROPE)
    return pl.pallas_call(
        _mla_sample_kernel, grid_spec=gs,
        out_shape=jax.ShapeDtypeStruct((b, MLA_HEADS, MLA_KV_LORA), F32),
        compiler_params=pltpu.CompilerParams(dimension_semantics=("arbitrary",) * 2, vmem_limit_bytes=VMEM_LIMIT),
        name="mla_sample",
    )(pt_flat, lat_cache, kr_t, qlat, qpe, p["c"], p["kr"])


def _rope_angles(pos, theta, dim):
    half = dim // 2
    inv = 1.0 / (jnp.float32(theta) ** (jnp.arange(half, dtype=F32) / half))
    ang = pos.astype(F32)[:, None] * inv[None, :]
    return jnp.cos(ang), jnp.sin(ang)


def _rope_lane_tables(pos, rows):
    def table(theta, dim, period, active):
        half = dim // 2
        cos, sin = _rope_angles(pos, theta, dim)
        lane = np.arange(LANES)
        d = lane % period
        is_lo = (d < half) & active(lane)
        is_hi = (d >= half) & (d < dim) & active(lane)
        fi = np.where(d < half, d, np.clip(d - half, 0, half - 1))
        cos_l, sin_l = cos[:, fi], sin[:, fi]
        tab = jnp.stack([jnp.where(is_lo | is_hi, cos_l, 1.0), jnp.where(is_lo, -sin_l, 0.0),
                         jnp.where(is_hi, sin_l, 0.0)])
        return jnp.broadcast_to(tab, (3, rows, LANES))

    every = lambda lane: np.ones_like(lane, bool)
    keys_only = lambda lane: (lane % LANES) < HD
    return (table(ROPE_THETA, ROT_DIM, HD, every), table(ROPE_THETA, ROT_DIM, HD, keys_only),
            table(MLA_ROPE_THETA, MLA_ROPE, MLA_ROPE, every))


def _rope_row_tables(pos):
    cq, sq = _rope_angles(pos, ROPE_THETA, ROT_DIM)
    cm, sm = _rope_angles(pos, MLA_ROPE_THETA, MLA_ROPE)
    return jnp.stack([cq.T, sq.T]), jnp.stack([cm.T, sm.T])


def _pack_weights(l, norm_pre, w_in, pe_cmp, q_norm, w_q_up, kv_norm, w_kv_up, w_proj_a, w_proj_b, w_out, norm_post):
    w = w_in[l]
    o = IN_OFFSETS
    seg = lambda k: w[:, o[k]:o[k + 1]]
    gn = seg(4)
    per_group = 3 * NSA_GROUP
    gn_p = jnp.zeros((D_MODEL, NSA_KV_HEADS * LANES), w.dtype)
    gn_t = jnp.zeros((NSA_KV_HEADS * GATE_ROWS, D_MODEL), w.dtype)
    for g in range(NSA_KV_HEADS):
        gn_g = gn[:, g * per_group:(g + 1) * per_group]
        gn_p = gn_p.at[:, g * LANES:g * LANES + per_group].set(gn_g)
        gn_t = gn_t.at[g * GATE_ROWS:g * GATE_ROWS + per_group].set(gn_g.T)
    w_rows = jnp.concatenate([seg(0), seg(1), seg(2), seg(3), gn_p, seg(5), seg(6), seg(7),
                              jnp.tile(seg(8), (1, LANES // MLA_ROPE)), seg(9), seg(10), seg(11)], axis=1)
    w_t = jnp.concatenate([seg(0).T, seg(1).T, seg(2).T, seg(3).T, gn_t, seg(8).T], axis=0)
    w_cols = jnp.concatenate([seg(5), seg(6), seg(7), seg(9), seg(10), seg(11)], axis=1)
    wq = w_q_up[l]
    wqup = jnp.concatenate([wq[..., :MLA_NOPE].reshape(MLA_Q_LORA, -1), wq[..., MLA_NOPE:].reshape(MLA_Q_LORA, -1)],
                           axis=1)
    wkv = w_kv_up[l]
    wuk_pad = jnp.pad(wkv[..., :MLA_NOPE], ((0, 0), (0, 0), (0, LANES - MLA_NOPE)))
    rope_copy = jnp.pad(jnp.eye(MLA_ROPE, dtype=w.dtype), ((0, LANES - MLA_ROPE), (MLA_NOPE, MLA_ROPE)))
    wkx = jnp.concatenate([wuk_pad.reshape(MLA_KV_LORA, MLA_HEADS * LANES), jnp.tile(rope_copy, (1, MLA_HEADS))],
                          axis=0)
    wvt = jnp.transpose(wkv[..., MLA_NOPE:], (1, 2, 0)).reshape(MLA_WIDTH, MLA_KV_LORA)
    w2uk =jnp.transpose(wkv[..., :MLA_NOPE], (1, 2, 0)).reshape(MLA_HEADS // 2, LANES, MLA_KV_LORA)
    wv = jnp.transpose(wkv[..., MLA_NOPE:], (1, 0, 2))
    zeros = jnp.zeros_like(wv)
    even = (jnp.arange(MLA_HEADS) % 2 == 0)[:, None, None]
    wuv = jnp.concatenate([jnp.where(even, wv, zeros), jnp.where(even, zeros, wv)], axis=2)
    return {
        "npre": norm_pre[l][None].astype(F32), "w_rows": w_rows.astype(BF16), "w_t": w_t.astype(BF16),
        "w_cols": w_cols.astype(BF16), "pe": pe_cmp[l].reshape(CMP_BLOCK, KV_WIDTH).astype(F32),
        "qnorm": q_norm[l][None].astype(F32), "wqup": wqup.astype(BF16), "wqupt": wqup.T.astype(BF16),
        "kvnorm": kv_norm[l][None].astype(F32), "w2uk": w2uk.astype(BF16), "wkx": wkx.astype(BF16),
        "wvt": wvt.astype(BF16), "wuv": wuv.astype(BF16),
        "wpa": w_proj_a[l].astype(BF16), "wpb": w_proj_b[l].astype(BF16), "wout": w_out[l].astype(BF16),
        "npost": norm_post[l][None].astype(F32),
    }


def _rows_from_cols(a):
    b, _, t = a.shape
    return a.reshape(b, NSA_KV_HEADS, 2, HD, t).transpose(0, 4, 1, 2, 3)


def _cols_from_rows(a):
    n, t = a.shape[:2]
    return a.transpose(0, 2, 3, 4, 1).reshape(n, KV_WIDTH, t)


def _prompt_layer(x, wts):
    b, t, _ = x.shape
    assert t % PROMPT_TM == 0
    p = _in_project_cols(x, _rope_row_tables(jnp.arange(t)), wts)
    o_a = _nsa_prompt(p, b, t)
    o_b = _mla_prompt(p, wts, b, t)
    y = _merge(x, o_a, o_b, p, wts, 512, from_latent=False)
    win_keep = min(WINDOW, t)
    return y, (_rows_from_cols(p["cmp"]), _rows_from_cols(p["slc"]), p["c"], p["kr"].transpose(0, 2, 1),
               _rows_from_cols(p["win"][:, :, t - win_keep:]))


def _sample_layer(x, l, caches, state_win, page_table, wts):
    cache_cmp, cache_slc, cache_lat, cache_kr = caches
    b, s_new, _ = x.shape
    assert s_new == 1
    n_pages = page_table.shape[1]
    page = cache_cmp.shape[2]
    past = n_pages * page
    assert past % SEL_BLOCK == 0 and n_pages % PAGES_PER_STEP == 0 and page == LANES
    win_len = state_win.shape[1]
    assert win_len == WINDOW and past >= WINDOW
    pt_flat = page_table.reshape(-1).astype(jnp.int32)

    tabs = _rope_lane_tables(jnp.full((1,), past, jnp.int32), b)
    p = _in_project_rows(x.reshape(1, b, D_MODEL), tabs, wts)
    p = {k: v.reshape(b, 1, v.shape[-1]) for k, v in p.items()}

    kc_all = _compress_pages(_cols_from_rows(cache_cmp[l]), pt_flat, wts["pe"], b, n_pages)
    oc, idx = _sample_select(p["q"], kc_all, b)
    o_a, new_win = _sample_attend(p, oc, idx.reshape(-1), pt_flat, _cols_from_rows(cache_slc[l]),
                                  _cols_from_rows(state_win), b, n_pages)
    o_lat = _mla_sample(p, cache_lat[l], cache_kr[l].transpose(0, 2, 1), pt_flat, b, n_pages)
    pm = {k: p[k].reshape(1, b, -1) for k in ("ga", "gb", "ma", "mb")}
    y = _merge(x.reshape(1, b, D_MODEL), o_a.reshape(1, b, NSA_WIDTH),
               o_lat.reshape(1, b, MLA_HEADS * MLA_KV_LORA), pm, wts, b, from_latent=True)
    kv6 = lambda a: a.reshape(b, 1, NSA_KV_HEADS, 2, HD)
    return y.reshape(b, 1, D_MODEL), (kv6(p["cmp"]), kv6(p["slc"]), p["c"], p["kr"], _rows_from_cols(new_win))


def kernel(x_prompt, x_sample, cache_nsa_cmp, cache_nsa_slc, cache_mla_latent, cache_mla_krope, state_nsa_win,
           page_table, norm_pre, w_in, pe_cmp, q_norm, w_q_up, kv_norm, w_kv_up, w_proj_a, w_proj_b, w_out,
           norm_post):
    depth = w_in.shape[0]
    hp, hs = x_prompt, x_sample
    new_p, new_s = [], []
    for l in range(depth):
        wts = _pack_weights(l, norm_pre, w_in, pe_cmp, q_norm, w_q_up, kv_norm, w_kv_up, w_proj_a, w_proj_b,
                            w_out, norm_post)
        hp, sp = _prompt_layer(hp, wts)
        hs, ss = _sample_layer(hs, l, (cache_nsa_cmp, cache_nsa_slc, cache_mla_latent, cache_mla_krope),
                               state_nsa_win[l], page_table, wts)
        new_p.append(sp)
        new_s.append(ss)
    stack = lambda items, k: jnp.stack([s[k] for s in items])
    return (hp, hs) + tuple(stack(new_p, k) for k in range(5)) + tuple(stack(new_s, k) for k in range(5))
```

```python
import functools

import numpy as np
import jax
import jax.numpy as jnp
from jax import lax
from jax.experimental import pallas as pl
from jax.experimental.pallas import tpu as pltpu

D_MODEL = 1024
NSA_HEADS = 8
NSA_KV_HEADS = 2
NSA_GROUP = NSA_HEADS // NSA_KV_HEADS
HD = 64
NSA_WIDTH = NSA_HEADS * HD
KV_WIDTH = NSA_KV_HEADS * 2 * HD
ROT_DIM = HD // 4
ROPE_THETA = 500000.0
CMP_BLOCK = 32
SEL_BLOCK = 64
N_SELECT = 16
WINDOW = 512
NSA_SCALE = HD ** -0.5

MLA_HEADS = 8
MLA_Q_LORA = 384
MLA_KV_LORA = 256
MLA_NOPE = 64
MLA_ROPE = 32
MLA_V = 64
MLA_WIDTH = MLA_HEADS * MLA_V
MLA_ROPE_THETA = 10000.0
MLA_SCALE = (MLA_NOPE + MLA_ROPE) ** -0.5
LOG2E = 1.4426950408889634

RMS_EPS = 1e-6
NEG = -1e30
FORCE_SCORE = 1e4

IN_SPLITS = (NSA_WIDTH, KV_WIDTH, KV_WIDTH, KV_WIDTH, 3 * NSA_HEADS, NSA_WIDTH,
             MLA_Q_LORA, MLA_KV_LORA, MLA_ROPE, MLA_WIDTH, D_MODEL, D_MODEL)
IN_OFFSETS = tuple(int(v) for v in np.cumsum((0,) + IN_SPLITS))

LANES = 128
HALF = LANES // 2
GATE_ROWS = 16
PAD_ROWS = 16

SEG_A = (0, 1280)
SEG_GN = (1280, 1536)
SEG_GA = (1536, 2048)
SEG_QD = (2048, 2432)
SEG_KVD = (2432, 2688)
SEG_KRP = (2688, 2816)
SEG_GB = (2816, 3328)
SEG_MA = (3328, 4352)
SEG_MB = (4352, 5376)
PACKED_WIDTH = 5376

PT_Q = (0, 512)
PT_CMP = (512, 768)
PT_SLC = (768, 1024)
PT_WIN = (1024, 1280)
PT_GN = (1280, 1280 + NSA_KV_HEADS * GATE_ROWS)
PT_KRP = (PT_GN[1], PT_GN[1] + MLA_ROPE)
PT_ROWS = PT_KRP[1]
PR_GA = (0, 512)
PR_QD = (512, 896)
PR_KVD = (896, 1152)
PR_GB = (1152, 1664)
PR_MA = (1664, 2688)
PR_MB = (2688, 3712)
PR_COLS = 3712

PROMPT_TM = 512
ATT_TQ = 256
ATT_TK = 256
MLA_TK = 256

VMEM_LIMIT = 48 * 1024 * 1024
BF16 = jnp.bfloat16
F32 = jnp.float32


def _full_spec(shape):
    nd = len(shape)
    return pl.BlockSpec(shape, lambda *_: (0,) * nd)


def _lane_lo(rows):
    return lax.broadcasted_iota(jnp.int32, (rows, LANES), 1) < HALF


def _dot(a, b):
    return jnp.dot(a, b, preferred_element_type=F32)


def _dot_nt(a, b):
    return lax.dot_general(a, b, (((1,), (1,)), ((), ())), preferred_element_type=F32)


def _rms(v, gain):
    return v * lax.rsqrt(jnp.mean(v * v, axis=-1, keepdims=True) + RMS_EPS) * gain


def _split_bf16(v):
    hi = v.astype(BF16)
    return hi, (v - hi.astype(F32)).astype(BF16)


def _rope_tiles(v, tab_ref, shift):
    c, s_lo, s_hi = tab_ref[0], tab_ref[1], tab_ref[2]
    out = []
    for k in range(v.shape[1] // LANES):
        a = v[:, k * LANES:(k + 1) * LANES]
        out.append(a * c + pltpu.roll(a, LANES - shift, 1) * s_lo + pltpu.roll(a, shift, 1) * s_hi)
    return out[0] if len(out) == 1 else jnp.concatenate(out, axis=1)


def _dup_kv(a, lo):
    r = pltpu.roll(a, HALF, 1)
    return jnp.where(lo, a, r), jnp.where(lo, r, a)


def _kv_pack(v):
    lo = _lane_lo(v.shape[0])
    parts = []
    for g in range(NSA_KV_HEADS):
        kk, vv = _dup_kv(v[:, g * LANES:(g + 1) * LANES], lo)
        parts += [kk, vv]
    return jnp.concatenate(parts, axis=1).astype(BF16)


def _inproj_rows_kernel(x_ref, npre_ref, w_ref, tq_ref, tkv_ref, tm_ref, qnorm_ref, wqup_ref, kvnorm_ref,
                        w2uk_ref, q_ref, qrot_ref, gate_ref, cmp_ref, slc_ref, win_ref, slckv_ref, winkv_ref,
                        ga_ref, gb_ref, ma_ref, mb_ref, qpe_ref, c_ref, kr_ref, qlat_ref):
    xb = _rms(x_ref[0], npre_ref[...]).astype(BF16)

    def seg(lohi):
        return _dot(xb, w_ref[:, lohi[0]:lohi[1]])

    a = seg(SEG_A)
    q = a[:, :NSA_WIDTH]
    q_ref[0] = (q * NSA_SCALE).astype(BF16)
    qrot_ref[0] = (_rope_tiles(q, tq_ref, ROT_DIM // 2) * NSA_SCALE).astype(BF16)
    cmp_ref[0] = a[:, NSA_WIDTH:NSA_WIDTH + KV_WIDTH]
    kvs = _rope_tiles(a[:, NSA_WIDTH + KV_WIDTH:NSA_WIDTH + 2 * KV_WIDTH], tkv_ref, ROT_DIM // 2)
    slc_ref[0] = kvs
    slckv_ref[0] = _kv_pack(kvs)
    kvw = _rope_tiles(a[:, NSA_WIDTH + 2 * KV_WIDTH:], tkv_ref, ROT_DIM // 2)
    win_ref[0] = kvw
    winkv_ref[0] = _kv_pack(kvw)

    gate_ref[0] = jax.nn.sigmoid(seg(SEG_GN))
    ga_ref[0] = seg(SEG_GA)
    gb_ref[0] = seg(SEG_GB)
    ma_ref[0] = seg(SEG_MA)
    mb_ref[0] = seg(SEG_MB)

    qd = _rms(seg(SEG_QD), qnorm_ref[...]).astype(BF16)
    qh = _dot(qd, wqup_ref[...])
    qn = qh[:, :MLA_HEADS * MLA_NOPE].astype(BF16)
    qpe_ref[0] = _rope_tiles(qh[:, MLA_HEADS * MLA_NOPE:], tm_ref, MLA_ROPE // 2) * MLA_SCALE
    c_ref[0] = _rms(seg(SEG_KVD), kvnorm_ref[...])
    kr_ref[0] = _rope_tiles(seg(SEG_KRP), tm_ref, MLA_ROPE // 2)[:, :MLA_ROPE]

    rows = qn.shape[0]
    lo = _lane_lo(rows)
    z = jnp.zeros((rows, LANES), BF16)
    parts = []
    for j in range(MLA_HEADS // 2):
        pair = qn[:, j * LANES:(j + 1) * LANES]
        parts.append(_dot(jnp.where(lo, pair, z), w2uk_ref[j]) * MLA_SCALE)
        parts.append(_dot(jnp.where(lo, z, pair), w2uk_ref[j]) * MLA_SCALE)
    qlat_ref[0] = jnp.concatenate(parts, axis=1)


def _in_project_rows(x3, tabs, wts):
    b, t, _ = x3.shape
    tq, tkv, tmla = tabs
    row = lambda w, dt: jax.ShapeDtypeStruct((b, t, w), dt)
    names = ["q", "qrot", "gate", "cmp", "slc", "win", "slckv", "winkv", "ga", "gb", "ma", "mb", "qpe", "c", "kr",
             "qlat"]
    out_shape = [row(512, BF16), row(512, BF16), row(256, F32), row(256, F32), row(256, F32), row(256, F32),
                 row(512, BF16), row(512, BF16), row(512, F32), row(512, F32), row(1024, F32), row(1024, F32),
                 row(MLA_HEADS * MLA_ROPE, F32), row(MLA_KV_LORA, F32), row(MLA_ROPE, F32),
                 row(MLA_HEADS * MLA_KV_LORA, F32)]
    tok = lambda w: pl.BlockSpec((1, t, w), lambda bb: (bb, 0, 0))
    tab = _full_spec((3, t, LANES))
    in_specs = [tok(D_MODEL), _full_spec((1, D_MODEL)), _full_spec((D_MODEL, PACKED_WIDTH)), tab, tab, tab,
                _full_spec((1, MLA_Q_LORA)), _full_spec((MLA_Q_LORA, MLA_HEADS * (MLA_NOPE + MLA_ROPE))),
                _full_spec((1, MLA_KV_LORA)), _full_spec((MLA_HEADS // 2, LANES, MLA_KV_LORA))]
    res = pl.pallas_call(
        _inproj_rows_kernel, grid=(b,), in_specs=in_specs, out_specs=[tok(s.shape[2]) for s in out_shape],
        out_shape=out_shape,
        compiler_params=pltpu.CompilerParams(dimension_semantics=("arbitrary",), vmem_limit_bytes=VMEM_LIMIT),
        name="in_project_rows",
    )(x3, wts["npre"], wts["w_rows"], tq, tkv, tmla, wts["qnorm"], wts["wqup"], wts["kvnorm"], wts["w2uk"])
    return dict(zip(names, res))


def _rope_rows(x, cos, sin, half):
    x1, x2 = x[0:half], x[half:2 * half]
    parts = [x1 * cos - x2 * sin, x1 * sin + x2 * cos]
    if x.shape[0] > 2 * half:
        parts.append(x[2 * half:])
    return jnp.concatenate(parts, axis=0)


def _store_chunks(ref, v):
    tk = ref.shape[3]
    for j in range(ref.shape[1]):
        ref[0, j] = v[:, j * tk:(j + 1) * tk]


def _inproj_cols_kernel(x_ref, npre_ref, wt_ref, w_ref, ropeq_ref, ropem_ref, pe_ref, qnorm_ref, wqupt_ref,
                        kvnorm_ref, pool_ref, wkx_ref, wvt_ref,
                        qt_ref, qrt_ref, gate_ref, cmp_ref, slc_ref, win_ref, slcb_ref, winb_ref, slcr_ref,
                        winr_ref, kc_ref, ga_ref, gb_ref, ma_ref, mb_ref, qmt_ref, c_ref, kmla_ref, vmt_ref,
                        kr_ref):
    tm = x_ref.shape[1]
    xb = _rms(x_ref[0], npre_ref[...]).astype(BF16)
    segt = lambda lohi: _dot_nt(wt_ref[lohi[0]:lohi[1], :], xb)
    seg = lambda lohi: _dot(xb, w_ref[:, lohi[0]:lohi[1]])
    cq, sq = ropeq_ref[0], ropeq_ref[1]
    cm, sm = ropem_ref[0], ropem_ref[1]
    hq, hm = ROT_DIM // 2, MLA_ROPE // 2

    qt = segt(PT_Q)
    qt_ref[0] = (qt * (NSA_SCALE * LOG2E)).astype(BF16)
    qrt = jnp.concatenate([_rope_rows(qt[h * HD:(h + 1) * HD], cq, sq, hq) for h in range(NSA_HEADS)], axis=0)
    qrt_ref[0] = (qrt * (NSA_SCALE * LOG2E)).astype(BF16)

    cmpt = segt(PT_CMP)
    cmp_ref[0] = cmpt
    hi, lo = _split_bf16(cmpt)
    pool = pool_ref[...]
    pooled = (_dot_nt(pool, hi) + _dot_nt(pool, lo))[0:tm // CMP_BLOCK]
    kc_ref[0] = (pooled + jnp.sum(pe_ref[...], axis=0, keepdims=True)) * (1.0 / CMP_BLOCK)

    def rope_kv(v):
        parts = []
        for g in range(NSA_KV_HEADS):
            parts.append(_rope_rows(v[g * LANES:g * LANES + HD], cq, sq, hq))
            parts.append(v[g * LANES + HD:(g + 1) * LANES])
        return jnp.concatenate(parts, axis=0)

    slct = rope_kv(segt(PT_SLC))
    slc_ref[0] = slct
    _store_chunks(slcb_ref, slct.astype(BF16))
    slcr_ref[0] = slct.T.astype(BF16)
    wint = rope_kv(segt(PT_WIN))
    win_ref[0] = wint
    _store_chunks(winb_ref, wint.astype(BF16))
    winr_ref[0] = wint.T.astype(BF16)

    gate_ref[0] = jax.nn.sigmoid(segt(PT_GN))
    krt = _rope_rows(segt(PT_KRP), cm, sm, hm)
    kr_ref[0] = krt
    c = _rms(seg(PR_KVD), kvnorm_ref[...])
    c_ref[0] = c
    c_b = c.astype(BF16)
    kr_rows = jnp.concatenate([krt, jnp.zeros((LANES - MLA_ROPE, tm), F32)], axis=0).T
    ckr = jnp.concatenate([c_b, kr_rows.astype(BF16)], axis=1)
    kmla_ref[0] = _dot(ckr, wkx_ref[...]).astype(BF16)
    _store_chunks(vmt_ref, _dot_nt(wvt_ref[...], c_b).astype(BF16))

    qd = _rms(seg(PR_QD), qnorm_ref[...]).astype(BF16)
    qht = _dot_nt(wqupt_ref[...], qd)
    n_nope = MLA_HEADS * MLA_NOPE
    zq = jnp.zeros((LANES - MLA_NOPE - MLA_ROPE, tm), F32)
    parts = []
    for h in range(MLA_HEADS):
        parts += [qht[h * MLA_NOPE:(h + 1) * MLA_NOPE],
                  _rope_rows(qht[n_nope + h * MLA_ROPE:n_nope + (h + 1) * MLA_ROPE], cm, sm, hm), zq]
    qmt_ref[0] = (jnp.concatenate(parts, axis=0) * (MLA_SCALE * LOG2E)).astype(BF16)

    ga_ref[0] = seg(PR_GA)
    gb_ref[0] = seg(PR_GB)
    ma_ref[0] = seg(PR_MA)
    mb_ref[0] = seg(PR_MB)


def _in_project_cols(x3, ropes, wts):
    b, t, _ = x3.shape
    tm = PROMPT_TM
    nt = t // tm
    ropeq, ropem = ropes
    pool = np.zeros((16, tm), np.float32)
    for s in range(tm):
        pool[s // CMP_BLOCK, s] = 1.0
    sds = jax.ShapeDtypeStruct
    rows = lambda w: pl.BlockSpec((1, tm, w), lambda i, bb: (bb, i, 0))
    cols = lambda w: pl.BlockSpec((1, w, tm), lambda i, bb: (bb, 0, i))
    chunk = lambda w, tk: pl.BlockSpec((1, tm // tk, w, tk), lambda i, bb: (bb, i, 0, 0))
    outs = [
        ("qt", sds((b, NSA_WIDTH, t), BF16), cols(NSA_WIDTH)),
        ("qrt", sds((b, NSA_WIDTH, t), BF16), cols(NSA_WIDTH)),
        ("gate", sds((b, NSA_KV_HEADS * GATE_ROWS, t), F32), cols(NSA_KV_HEADS * GATE_ROWS)),
        ("cmp", sds((b, KV_WIDTH, t), F32), cols(KV_WIDTH)),
        ("slc", sds((b, KV_WIDTH, t), F32), cols(KV_WIDTH)),
        ("win", sds((b, KV_WIDTH, t), F32), cols(KV_WIDTH)),
        ("slcb", sds((b, t // ATT_TK, KV_WIDTH, ATT_TK), BF16), chunk(KV_WIDTH, ATT_TK)),
        ("winb", sds((b, t // ATT_TK, KV_WIDTH, ATT_TK), BF16), chunk(KV_WIDTH, ATT_TK)),
        ("slcr", sds((b, t, KV_WIDTH), BF16), rows(KV_WIDTH)),
        ("winr", sds((b, t, KV_WIDTH), BF16), rows(KV_WIDTH)),
        ("kc", sds((b, t // CMP_BLOCK, KV_WIDTH), F32),
         pl.BlockSpec((1, tm // CMP_BLOCK, KV_WIDTH), lambda i, bb: (bb, i, 0))),
        ("ga", sds((b, t, NSA_WIDTH), F32), rows(NSA_WIDTH)),
        ("gb", sds((b, t, MLA_WIDTH), F32), rows(MLA_WIDTH)),
        ("ma", sds((b, t, D_MODEL), F32), rows(D_MODEL)),
        ("mb", sds((b, t, D_MODEL), F32), rows(D_MODEL)),
        ("qmt", sds((b, MLA_HEADS * LANES, t), BF16), cols(MLA_HEADS * LANES)),
        ("c", sds((b, t, MLA_KV_LORA), F32), rows(MLA_KV_LORA)),
        ("kmla", sds((b, t, MLA_HEADS * LANES), BF16), rows(MLA_HEADS * LANES)),
        ("vmt", sds((b, t // MLA_TK, MLA_WIDTH, MLA_TK), BF16), chunk(MLA_WIDTH, MLA_TK)),
        ("kr", sds((b, MLA_ROPE, t), F32), cols(MLA_ROPE)),
    ]
    rope_spec = lambda half: pl.BlockSpec((2, half, tm), lambda i, bb: (0, 0, i))
    in_specs = [rows(D_MODEL), _full_spec((1, D_MODEL)), _full_spec((PT_ROWS, D_MODEL)),
                _full_spec((D_MODEL, PR_COLS)), rope_spec(ROT_DIM // 2), rope_spec(MLA_ROPE // 2),
                _full_spec((CMP_BLOCK, KV_WIDTH)), _full_spec((1, MLA_Q_LORA)),
                _full_spec((MLA_HEADS * (MLA_NOPE + MLA_ROPE), MLA_Q_LORA)), _full_spec((1, MLA_KV_LORA)),
                _full_spec((16, tm)), _full_spec((MLA_KV_LORA + LANES, MLA_HEADS * LANES)),
                _full_spec((MLA_WIDTH, MLA_KV_LORA))]
    res = pl.pallas_call(
        _inproj_cols_kernel, grid=(nt, b), in_specs=in_specs, out_specs=[o[2] for o in outs],
        out_shape=[o[1] for o in outs],
        compiler_params=pltpu.CompilerParams(dimension_semantics=("arbitrary", "arbitrary"),
                                             vmem_limit_bytes=VMEM_LIMIT),
        name="in_project_cols",
    )(x3, wts["npre"], wts["w_t"], wts["w_cols"], ropeq, ropem, wts["pe"], wts["qnorm"], wts["wqupt"],
      wts["kvnorm"], jnp.asarray(pool, BF16), wts["wkx"], wts["wvt"])
    return dict(zip([o[0] for o in outs], res))


def _online_update(s, vt, m, l, acc):
    d, keys = vt.shape
    m_new = jnp.maximum(m, jnp.max(s, axis=0, keepdims=True))
    p = jnp.exp2(s - m_new).astype(BF16)
    alpha = jnp.exp2(m - m_new)
    pv = _dot(jnp.concatenate([vt, jnp.ones((PAD_ROWS, keys), BF16)], axis=0), p)
    return m_new, alpha * l + pv[d:d + 1], alpha * acc + pv[0:d]


def _nsa_prompt_kernel(n_blk, qt_ref, qrt_ref, gate_ref, kc_ref, ks_ref, kst_ref, kw_ref, kwt_ref, o_ref, sb_ref):
    i = pl.program_id(2)
    tq, tk = ATT_TQ, ATT_TK
    nl = NSA_GROUP * tq
    zq = jnp.zeros((HD, tq), BF16)

    def widen(qt):
        return jnp.concatenate([jnp.concatenate([qt[r * HD:(r + 1) * HD], zq], axis=0)
                                for r in range(NSA_GROUP)], axis=1)

    qc = widen(qt_ref[0])
    qr = widen(qrt_ref[0])

    n_cmp = 2 * n_blk
    kc = kc_ref[0]
    s_c = _dot(kc.astype(BF16), qc)
    rho = lax.broadcasted_iota(jnp.int32, (n_cmp, nl), 0)
    cmp_idx = jnp.where(rho < n_blk, 2 * rho, 2 * (rho - n_blk) + 1)
    t_l = i * tq + (lax.broadcasted_iota(jnp.int32, (n_cmp, nl), 1) & (tq - 1))
    mask_c = cmp_idx * CMP_BLOCK + (CMP_BLOCK - 1) <= t_l
    s_c = jnp.where(mask_c, s_c, NEG)
    e_c = jnp.where(mask_c, jnp.exp2(s_c - jnp.max(s_c, axis=0, keepdims=True)), 0.0)
    p_c = e_c / jnp.maximum(jnp.sum(e_c, axis=0, keepdims=True), 1e-30)
    kct = jnp.concatenate([kc, jnp.zeros((LANES - n_cmp, LANES), F32)], axis=0).T
    p_pad = jnp.concatenate([p_c, jnp.zeros((LANES - n_cmp, nl), F32)], axis=0)
    o_c = _dot(kct[HD:2 * HD].astype(BF16), p_pad.astype(BF16))

    imp = p_c[:, 0:tq]
    for r in range(1, NSA_GROUP):
        imp = imp + p_c[:, r * tq:(r + 1) * tq]
    imp_blk = imp[0:n_blk] + imp[n_blk:n_cmp]
    blk = lax.broadcasted_iota(jnp.int32, (n_blk, tq), 0)
    t_q = i * tq + lax.broadcasted_iota(jnp.int32, (n_blk, tq), 1)
    valid = blk * SEL_BLOCK <= t_q
    in_blk = (blk * SEL_BLOCK <= t_q) & (t_q < (blk + 1) * SEL_BLOCK)
    forced = in_blk | (blk == 0)
    score = jnp.where(forced, FORCE_SCORE, jnp.where(valid, imp_blk, -FORCE_SCORE))
    rank = jnp.zeros((n_blk, tq), jnp.int32)
    for j in range(n_blk):
        other = score[j:j + 1, :]
        ahead = (other > score) | ((other == score) & (j < blk))
        rank = rank + jnp.where(ahead, 1, 0)
    sel = (rank < N_SELECT) & (score > -1.0)
    sb_ref[...] = jnp.where(sel, 0.0, NEG)

    key_r = lax.broadcasted_iota(jnp.int32, (tk, tq), 0)
    t_k = i * tq + lax.broadcasted_iota(jnp.int32, (tk, tq), 1)
    n_chunks = ((i + 1) * tq + tk - 1) // tk
    c_lo = jnp.maximum(i * tq - WINDOW, 0) // tk
    init = (jnp.full((1, nl), NEG, F32), jnp.zeros((1, nl), F32), jnp.zeros((HD, nl), F32))
    per_chunk = tk // SEL_BLOCK
    tile = lambda bias: jnp.concatenate([bias] * NSA_GROUP, axis=1)

    def sel_bias(c):
        rows = [jnp.broadcast_to(sb_ref[pl.ds(c * per_chunk + j, 1), :], (SEL_BLOCK, tq)) for j in range(per_chunk)]
        return jnp.concatenate(rows, axis=0)

    def past_body(c, carry):
        base = pl.multiple_of(c * tk, tk)
        s = _dot(ks_ref[0, pl.ds(base, tk), :], qr) + tile(sel_bias(c))
        return _online_update(s, kst_ref[0, c, HD:2 * HD, :], *carry)

    def near_body(c, carry):
        base = pl.multiple_of(c * tk, tk)
        s = _dot(ks_ref[0, pl.ds(base, tk), :], qr)
        w = _dot(kw_ref[0, pl.ds(base, tk), :], qr)
        dist = t_k - (base + key_r)
        causal = jnp.where(dist >= 0, 0.0, NEG)
        s = s + tile(sel_bias(c) + causal)
        w = w + tile(jnp.where(dist <= WINDOW, causal, NEG))
        (m_s, l_s, a_s), (m_w, l_w, a_w) = carry[:3], carry[3:]
        ms_new = jnp.maximum(m_s, jnp.max(s, axis=0, keepdims=True))
        mw_new = jnp.maximum(m_w, jnp.max(w, axis=0, keepdims=True))
        p_s = jnp.exp2(s - ms_new).astype(BF16)
        p_w = jnp.exp2(w - mw_new).astype(BF16)
        ones = jnp.ones((PAD_ROWS, tk), BF16)
        pv_s = _dot(jnp.concatenate([kst_ref[0, c, HD:2 * HD, :], ones], axis=0), p_s)
        pv_w = _dot(jnp.concatenate([kwt_ref[0, c, HD:2 * HD, :], ones], axis=0), p_w)
        al_s = jnp.exp2(m_s - ms_new)
        al_w = jnp.exp2(m_w - mw_new)
        return (ms_new, al_s * l_s + pv_s[HD:HD + 1], al_s * a_s + pv_s[0:HD],
                mw_new, al_w * l_w + pv_w[HD:HD + 1], al_w * a_w + pv_w[0:HD])

    far = lax.fori_loop(0, c_lo, past_body, init)
    _, l_s, a_s, _, l_w, a_w = lax.fori_loop(c_lo, n_chunks, near_body, far + init)
    o_s = a_s / l_s
    o_w = a_w / l_w

    gate = gate_ref[0]
    heads = []
    for r in range(NSA_GROUP):
        sl = slice(r * tq, (r + 1) * tq)
        heads.append(gate[3 * r:3 * r + 1] * o_c[:, sl] + gate[3 * r + 1:3 * r + 2] * o_s[:, sl]
                     + gate[3 * r + 2:3 * r + 3] * o_w[:, sl])
    o_ref[0] = jnp.concatenate(heads, axis=0).T


def _nsa_prompt(p, b, t):
    tq, tk = ATT_TQ, ATT_TK
    n_blk = t // SEL_BLOCK
    assert 2 * n_blk <= LANES and t % tk == 0
    kc = p["kc"].reshape(b, n_blk, 2, KV_WIDTH).transpose(0, 2, 1, 3).reshape(b, 2 * n_blk, KV_WIDTH)
    qspec = pl.BlockSpec((1, NSA_GROUP * HD, tq), lambda bb, g, i: (bb, g, i))
    rm = pl.BlockSpec((1, t, LANES), lambda bb, g, i: (bb, 0, g))
    fm = pl.BlockSpec((1, t // tk, LANES, tk), lambda bb, g, i: (bb, 0, g, 0))
    return pl.pallas_call(
        functools.partial(_nsa_prompt_kernel, n_blk),
        grid=(b, NSA_KV_HEADS, t // tq),
        in_specs=[qspec, qspec, pl.BlockSpec((1, GATE_ROWS, tq), lambda bb, g, i: (bb, g, i)),
                  pl.BlockSpec((1, 2 * n_blk, LANES), lambda bb, g, i: (bb, 0, g)), rm, fm, rm, fm],
        out_specs=pl.BlockSpec((1, tq, NSA_GROUP * HD), lambda bb, g, i: (bb, i, g)),
        out_shape=jax.ShapeDtypeStruct((b, t, NSA_WIDTH), F32),
        scratch_shapes=[pltpu.VMEM((n_blk, tq), F32)],
        compiler_params=pltpu.CompilerParams(dimension_semantics=("arbitrary",) * 3, vmem_limit_bytes=VMEM_LIMIT),
        name="nsa_prompt",
    )(p["qt"], p["qrt"], p["gate"], kc, p["slcr"], p["slcb"], p["winr"], p["winb"])


MLA_TQ = 256


def _mla_prompt_kernel(qmt_ref, k_ref, vt_ref, o_ref, m_ref, l_ref, acc_ref):
    i = pl.program_id(1)
    tq, tk = MLA_TQ, MLA_TK
    m_ref[...] = jnp.full(m_ref.shape, NEG, F32)
    l_ref[...] = jnp.zeros(l_ref.shape, F32)
    acc_ref[...] = jnp.zeros(acc_ref.shape, F32)
    key_r = lax.broadcasted_iota(jnp.int32, (tk, tq), 0)
    t_k = i * tq + lax.broadcasted_iota(jnp.int32, (tk, tq), 1)
    n_chunks = ((i + 1) * tq + tk - 1) // tk

    def step(c, masked):
        base = pl.multiple_of(c * tk, tk)
        old = [(m_ref[h], l_ref[h], acc_ref[h]) for h in range(MLA_HEADS)]
        scores = []
        for h in range(MLA_HEADS):
            k_h = k_ref[0, pl.ds(base, tk), h * LANES:(h + 1) * LANES]
            s = _dot(k_h, qmt_ref[0, h * LANES:(h + 1) * LANES, :])
            scores.append(jnp.where(base + key_r <= t_k, s, NEG) if masked else s)
        new = [_online_update(scores[h], vt_ref[0, c, h * MLA_V:(h + 1) * MLA_V, :], *old[h])
               for h in range(MLA_HEADS)]
        for h in range(MLA_HEADS):
            m_ref[h], l_ref[h], acc_ref[h] = new[h]
        return 0

    n_past = (i * tq + 1) // tk
    lax.fori_loop(0, n_past, lambda c, z: step(c, False), 0)
    lax.fori_loop(n_past, n_chunks, lambda c, z: step(c, True), 0)
    outs = [acc_ref[h] / l_ref[h] for h in range(MLA_HEADS)]
    o_ref[0] = jnp.concatenate(outs, axis=0).T


def _mla_prompt(p, wts, b, t):
    tq, tk = MLA_TQ, MLA_TK
    assert t % tq == 0
    return pl.pallas_call(
        _mla_prompt_kernel,
        grid=(b, t // tq),
        in_specs=[pl.BlockSpec((1, MLA_HEADS * LANES, tq), lambda bb, i: (bb, 0, i)),
                  pl.BlockSpec((1, t, MLA_HEADS * LANES), lambda bb, i: (bb, 0, 0)),
                  pl.BlockSpec((1, t // tk, MLA_WIDTH, tk), lambda bb, i: (bb, 0, 0, 0))],
        out_specs=pl.BlockSpec((1, tq, MLA_WIDTH), lambda bb, i: (bb, i, 0)),
        out_shape=jax.ShapeDtypeStruct((b, t, MLA_WIDTH), F32),
        scratch_shapes=[pltpu.VMEM((MLA_HEADS, 1, tq), F32), pltpu.VMEM((MLA_HEADS, 1, tq), F32),
                        pltpu.VMEM((MLA_HEADS, MLA_V, tq), F32)],
        compiler_params=pltpu.CompilerParams(dimension_semantics=("arbitrary",) * 2, vmem_limit_bytes=VMEM_LIMIT),
        name="mla_prompt",
    )(p["qmt"], p["kmla"], p["vmt"])


def _merge_kernel(from_latent, x_ref, oa_ref, ob_ref, ga_ref, gb_ref, ma_ref, mb_ref, wpa_ref, wpb_ref, wout_ref,
                  npost_ref, wuv_ref, y_ref):
    if from_latent:
        lat = ob_ref[0].astype(BF16)
        parts = []
        for j in range(MLA_HEADS // 2):
            parts.append(_dot(lat[:, 2 * j * MLA_KV_LORA:(2 * j + 1) * MLA_KV_LORA], wuv_ref[2 * j])
                         + _dot(lat[:, (2 * j + 1) * MLA_KV_LORA:(2 * j + 2) * MLA_KV_LORA], wuv_ref[2 * j + 1]))
        o_b = jnp.concatenate(parts, axis=1)
    else:
        o_b = ob_ref[0]
    ga = ga_ref[0]
    gb = gb_ref[0]
    pa = _dot((oa_ref[0] * (ga * jax.nn.sigmoid(ga))).astype(BF16), wpa_ref[...])
    pb = _dot((o_b * (gb * jax.nn.sigmoid(gb))).astype(BF16), wpb_ref[...])
    h = jax.nn.sigmoid(ma_ref[0]) * pa + jax.nn.sigmoid(mb_ref[0]) * pb
    z = _dot(h.astype(BF16), wout_ref[...])
    y_ref[0] = x_ref[0] + _rms(z, npost_ref[...])


def _merge(x3, o_a, o_b, p, wts, tm, from_latent):
    b, t, _ = x3.shape
    tok = lambda w: pl.BlockSpec((1, tm, w), lambda bb, i: (bb, i, 0))
    return pl.pallas_call(
        functools.partial(_merge_kernel, from_latent),
        grid=(b, t // tm),
        in_specs=[tok(D_MODEL), tok(NSA_WIDTH), tok(o_b.shape[2]), tok(NSA_WIDTH), tok(MLA_WIDTH), tok(D_MODEL),
                  tok(D_MODEL), _full_spec((NSA_WIDTH, D_MODEL)), _full_spec((MLA_WIDTH, D_MODEL)),
                  _full_spec((D_MODEL, D_MODEL)), _full_spec((1, D_MODEL)),
                  _full_spec((MLA_HEADS, MLA_KV_LORA, LANES))],
        out_specs=tok(D_MODEL),
        out_shape=jax.ShapeDtypeStruct((b, t, D_MODEL), F32),
        compiler_params=pltpu.CompilerParams(dimension_semantics=("arbitrary",) * 2, vmem_limit_bytes=VMEM_LIMIT),
        name="merge",
    )(x3, o_a, o_b, p["ga"], p["gb"], p["ma"], p["mb"], wts["wpa"], wts["wpb"], wts["wout"], wts["npost"],
      wts["wuv"])


PAGES_PER_STEP = 32
POOL_PAGES = 4


def _page_copy(pt_ref, cache_ref, buf_ref, sem_ref, step, slot, k):
    return pltpu.make_async_copy(cache_ref.at[pt_ref[step * PAGES_PER_STEP + k]], buf_ref.at[slot, k],
                                 sem_ref.at[slot])


def _paged_pipeline(pt_ref, streams):
    step = pl.program_id(0) * pl.num_programs(1) + pl.program_id(1)
    last = pl.num_programs(0) * pl.num_programs(1) - 1
    slot = step % 2

    @pl.when(step == 0)
    def _():
        for cache_ref, buf_ref, sem_ref in streams:
            for k in range(PAGES_PER_STEP):
                _page_copy(pt_ref, cache_ref, buf_ref, sem_ref, 0, 0, k).start()

    @pl.when(step < last)
    def _():
        for cache_ref, buf_ref, sem_ref in streams:
            for k in range(PAGES_PER_STEP):
                _page_copy(pt_ref, cache_ref, buf_ref, sem_ref, step + 1, 1 - slot, k).start()

    for cache_ref, buf_ref, sem_ref in streams:
        for k in range(PAGES_PER_STEP):
            _page_copy(pt_ref, cache_ref, buf_ref, sem_ref, step, slot, k).wait()
    return slot


def _compress_pages_kernel(pt_ref, cache_ref, pe_ref, o_ref, buf_ref, sem_ref):
    slot = _paged_pipeline(pt_ref, [(cache_ref, buf_ref, sem_ref)])
    pe_sum = jnp.sum(pe_ref[...], axis=0, keepdims=True)
    pooled = []
    for k in range(PAGES_PER_STEP):
        rows = buf_ref[slot, k].T
        pooled.append(jnp.sum(rows.reshape(rows.shape[0] // CMP_BLOCK, CMP_BLOCK, KV_WIDTH), axis=1))
    o_ref[0] = (jnp.concatenate(pooled, axis=0) + pe_sum) * (1.0 / CMP_BLOCK)


def _compress_pages(cache_t, pt_flat, pe, b, n_pages):
    page = cache_t.shape[2]
    per_step = PAGES_PER_STEP * page // CMP_BLOCK
    gs = pltpu.PrefetchScalarGridSpec(
        num_scalar_prefetch=1, grid=(b, n_pages // PAGES_PER_STEP),
        in_specs=[pl.BlockSpec(memory_space=pl.ANY),
                  pl.BlockSpec((CMP_BLOCK, KV_WIDTH), lambda bb, c, pt: (0, 0))],
        out_specs=pl.BlockSpec((1, per_step, KV_WIDTH), lambda bb, c, pt: (bb, c, 0)),
        scratch_shapes=[pltpu.VMEM((2, PAGES_PER_STEP, KV_WIDTH, page), F32), pltpu.SemaphoreType.DMA((2,))])
    return pl.pallas_call(
        _compress_pages_kernel, grid_spec=gs,
        out_shape=jax.ShapeDtypeStruct((b, n_pages * page // CMP_BLOCK, KV_WIDTH), F32),
        compiler_params=pltpu.CompilerParams(dimension_semantics=("arbitrary",) * 2, vmem_limit_bytes=VMEM_LIMIT),
        name="compress_pages",
    )(pt_flat, cache_t, pe)


def _pad_rows(v, rows):
    return jnp.concatenate([v, jnp.zeros((rows - v.shape[0], v.shape[1]), v.dtype)], axis=0)


def _stack_heads(qv, lo):
    a, b = qv[:, :LANES], qv[:, LANES:]
    z = jnp.zeros_like(a)
    return jnp.concatenate([jnp.where(lo, a, z), jnp.where(lo, z, a),
                            jnp.where(lo, b, z), jnp.where(lo, z, b)], axis=0)


def _sample_select_kernel(n_cmp, q_ref, kc_ref, oc_ref, idx_ref, imp_ref):
    bb = pl.program_id(0)
    lo1 = _lane_lo(1)
    lo_c = _lane_lo(n_cmp)
    q = q_ref[0]
    kc = kc_ref[0]
    even = (lax.broadcasted_iota(jnp.int32, (1, LANES), 1) & 1) == 0
    for g in range(NSA_KV_HEADS):
        kk, vv = _dup_kv(kc[:, g * LANES:(g + 1) * LANES], lo_c)
        qs = _pad_rows(_stack_heads(q[:, g * 2 * LANES:(g + 1) * 2 * LANES], lo1), PAD_ROWS)
        s = _dot_nt(qs, kk.astype(BF16))
        e = jnp.exp(s - jnp.max(s, axis=-1, keepdims=True))
        p = e / jnp.sum(e, axis=-1, keepdims=True)
        oc_ref[0, g] = _dot(p.astype(BF16), vv.astype(BF16))
        imp = p[0:1] + p[1:2] + p[2:3] + p[3:4]
        chunks = []
        for k in range(n_cmp // LANES):
            a = imp[:, k * LANES:(k + 1) * LANES]
            chunks.append(a + jnp.where(even, pltpu.roll(a, LANES - 1, 1), pltpu.roll(a, 1, 1)))
        imp_ref[pl.ds(bb * NSA_KV_HEADS + g, 1), :] = jnp.concatenate(chunks, axis=1)

    @pl.when(bb == pl.num_programs(0) - 1)
    def _():
        rows = imp_ref.shape[0]
        blk = lax.broadcasted_iota(jnp.int32, (rows, n_cmp), 1) >> 1
        blk_f = blk.astype(F32)
        slot = lax.broadcasted_iota(jnp.int32, (rows, N_SELECT), 1)
        v = jnp.where(blk == 0, -1.0, imp_ref[...])
        idx = jnp.where(slot == N_SELECT - 1, n_cmp // 2, 0)
        for k in range(1, N_SELECT - 1):
            top = jnp.max(v, axis=-1, keepdims=True)
            jmin = jnp.min(jnp.where(v == top, blk_f, float(n_cmp)), axis=-1, keepdims=True).astype(jnp.int32)
            idx = jnp.where(slot == k, jmin, idx)
            v = jnp.where(blk == jmin, -1.0, v)
        idx_ref[...] = idx


def _sample_select(q, kc_all, b):
    n_cmp = kc_all.shape[1]
    rows = b * NSA_KV_HEADS
    return pl.pallas_call(
        functools.partial(_sample_select_kernel, n_cmp),
        grid=(b,),
        in_specs=[pl.BlockSpec((1, 1, NSA_WIDTH), lambda bb: (bb, 0, 0)),
                  pl.BlockSpec((1, n_cmp, KV_WIDTH), lambda bb: (bb, 0, 0))],
        out_specs=[pl.BlockSpec((1, NSA_KV_HEADS, PAD_ROWS, LANES), lambda bb: (bb, 0, 0, 0)),
                   pl.BlockSpec((rows, N_SELECT), lambda bb: (0, 0))],
        out_shape=[jax.ShapeDtypeStruct((b, NSA_KV_HEADS, PAD_ROWS, LANES), F32),
                   jax.ShapeDtypeStruct((rows, N_SELECT), jnp.int32)],
        scratch_shapes=[pltpu.VMEM((rows, n_cmp), F32)],
        compiler_params=pltpu.CompilerParams(dimension_semantics=("arbitrary",), vmem_limit_bytes=VMEM_LIMIT),
        name="sample_select",
    )(q, kc_all)


def _extra_key_softmax(s_past, vt4_b, s_new, v_new):
    m = jnp.maximum(jnp.max(s_past, axis=-1, keepdims=True), s_new)
    e = jnp.exp(s_past - m)
    e_new = jnp.exp(s_new - m)
    den = jnp.sum(e, axis=-1, keepdims=True) + e_new
    return (_dot_nt(e.astype(BF16), vt4_b) + e_new * v_new) / den


def _sample_attend_kernel(idx_ref, pt_ref, *refs):
    del pt_ref
    n_sel = N_SELECT
    pages = refs[:n_sel]
    qr_ref, newkv_ref, neww_ref, newwf_ref, win_ref, gate_ref, oc_ref, o_ref, wout_ref = refs[n_sel:]
    bb, g = pl.program_id(0), pl.program_id(1)
    lo1 = _lane_lo(1)
    q = qr_ref[0].astype(F32)
    halves = [q[:, 0:LANES], q[:, LANES:2 * LANES]]
    rows = []
    for r in range(NSA_GROUP):
        a = halves[r // 2]
        rows.append(jnp.where(lo1, a if r % 2 == 0 else pltpu.roll(a, HALF, 1), 0.0))
    qs_f = _pad_rows(jnp.concatenate(rows, axis=0), PAD_ROWS)
    qs = qs_f.astype(BF16)
    twice_rows = lambda a: jnp.concatenate([a, a], axis=0)

    s_t = jnp.concatenate([pg[0] for pg in pages], axis=1)
    s_sel = _dot(qs, s_t.astype(BF16))
    base = (bb * NSA_KV_HEADS + g) * n_sel
    biases = []
    for k in range(n_sel - 1):
        odd = (idx_ref[base + k] & 1) == 1
        biases.append(jnp.where(lo1, jnp.where(odd, NEG, 0.0), jnp.where(odd, 0.0, NEG)))
    biases.append(jnp.full((1, LANES), NEG, F32))
    s_sel = s_sel + jnp.concatenate(biases, axis=1)
    nk = newkv_ref[0].astype(F32)
    s_new = jnp.sum(qs_f * nk[:, :LANES], axis=-1, keepdims=True)
    o_s = _extra_key_softmax(s_sel, twice_rows(s_t[HD:2 * HD]).astype(BF16), s_new, nk[:, LANES:])

    w = win_ref[0]
    nw = neww_ref[0].astype(F32)
    s_w = _dot(qs, w.astype(BF16))
    s_wn = jnp.sum(qs_f * nw[:, :LANES], axis=-1, keepdims=True)
    o_w = _extra_key_softmax(s_w, twice_rows(w[HD:2 * HD]).astype(BF16), s_wn, nw[:, LANES:])

    o_c = oc_ref[0, 0]
    gate = gate_ref[0]
    heads = []
    for r in range(NSA_GROUP):
        heads.append(gate[:, 3 * r:3 * r + 1] * o_c[r:r + 1] + gate[:, 3 * r + 1:3 * r + 2] * o_s[r:r + 1]
                     + gate[:, 3 * r + 2:3 * r + 3] * o_w[r:r + 1])
    o_ref[0] = jnp.concatenate([jnp.where(lo1, heads[0], heads[1]), jnp.where(lo1, heads[2], heads[3])], axis=1)

    n_feat, n_w = w.shape
    new_row = jnp.broadcast_to(newwf_ref[0], (n_feat, n_feat))
    diag = (lax.broadcasted_iota(jnp.int32, (n_feat, n_feat), 0)
            == lax.broadcasted_iota(jnp.int32, (n_feat, n_feat), 1))
    new_col = jnp.sum(jnp.where(diag, new_row, 0.0), axis=1, keepdims=True)
    last = lax.broadcasted_iota(jnp.int32, (n_feat, LANES), 1) == LANES - 1
    chunks = []
    n_ch = n_w // LANES
    for c in range(n_ch):
        cur = pltpu.roll(w[:, c * LANES:(c + 1) * LANES], LANES - 1, 1)
        if c + 1 < n_ch:
            nxt = pltpu.roll(w[:, (c + 1) * LANES:(c + 2) * LANES], LANES - 1, 1)
        else:
            nxt = jnp.broadcast_to(new_col, (n_feat, LANES))
        chunks.append(jnp.where(last, nxt, cur))
    wout_ref[0] = jnp.concatenate(chunks, axis=1)


def _sample_attend(p, oc, idx_flat, pt_flat, slc_t, win_t, b, n_pages):
    page = slc_t.shape[2]
    win_len = win_t.shape[2]
    per_page = page // SEL_BLOCK

    def page_spec(k):
        def imap(bb, g, idx, pt):
            j = idx[(bb * NSA_KV_HEADS + g) * N_SELECT + k]
            return (pt[bb * n_pages + jnp.minimum(j // per_page, n_pages - 1)], g, 0)
        return pl.BlockSpec((1, LANES, page), imap)

    row = lambda w: pl.BlockSpec((1, 1, w), lambda bb, g, idx, pt: (bb, 0, g))
    wspec = pl.BlockSpec((1, LANES, win_len), lambda bb, g, idx, pt: (bb, g, 0))
    gs = pltpu.PrefetchScalarGridSpec(
        num_scalar_prefetch=2, grid=(b, NSA_KV_HEADS),
        in_specs=[page_spec(k) for k in range(N_SELECT)] + [
            row(2 * LANES), row(2 * LANES), row(2 * LANES), row(LANES), wspec, row(LANES),
            pl.BlockSpec((1, 1, PAD_ROWS, LANES), lambda bb, g, idx, pt: (bb, g, 0, 0))],
        out_specs=[row(2 * LANES), wspec])
    return pl.pallas_call(
        _sample_attend_kernel, grid_spec=gs,
        out_shape=[jax.ShapeDtypeStruct((b, 1, NSA_WIDTH), F32),
                   jax.ShapeDtypeStruct((b, KV_WIDTH, win_len), F32)],
        compiler_params=pltpu.CompilerParams(dimension_semantics=("arbitrary",) * 2, vmem_limit_bytes=VMEM_LIMIT),
        name="sample_attend",
    )(idx_flat, pt_flat, *([slc_t] * N_SELECT), p["qrot"], p["slckv"], p["winkv"], p["win"], win_t,
      p["gate"], oc)


def _softmax_update(s, v_b, m, l, acc):
    m_new = jnp.maximum(m, jnp.max(s, axis=-1, keepdims=True))
    p = jnp.exp(s - m_new)
    alpha = jnp.exp(m - m_new)
    return m_new, alpha * l + jnp.sum(p, axis=-1, keepdims=True), alpha * acc + _dot(p.astype(BF16), v_b)


def _mla_sample_kernel(pt_ref, lat_ref, krt_ref, qlat_ref, qpe_ref, cnew_ref, krnew_ref, o_ref,
                       m_ref, l_ref, acc_ref, lat_buf, kr_buf, lat_sem, kr_sem):
    n = PAGES_PER_STEP
    slot = _paged_pipeline(pt_ref, [(lat_ref, lat_buf, lat_sem), (krt_ref, kr_buf, kr_sem)])
    lat_pages = [lat_buf.at[slot, k] for k in range(n)]
    kr_pages = [kr_buf.at[slot, k] for k in range(n)]
    step = pl.program_id(1)
    qlat = _pad_rows(qlat_ref[0], PAD_ROWS)
    qpe = _pad_rows(qpe_ref[0], PAD_ROWS)

    @pl.when(step == 0)
    def _():
        c_new = cnew_ref[0]
        s_new = (jnp.sum(qlat * c_new, axis=-1, keepdims=True)
                 + jnp.sum(qpe * krnew_ref[0], axis=-1, keepdims=True))
        m_ref[...] = s_new
        l_ref[...] = jnp.ones(l_ref.shape, F32)
        acc_ref[...] = jnp.broadcast_to(c_new, acc_ref.shape)

    qlat_b, qpe_b = qlat.astype(BF16), qpe.astype(BF16)
    groups = range(0, n, POOL_PAGES)
    c_bs = [jnp.concatenate([pg[...] for pg in lat_pages[k:k + POOL_PAGES]], axis=0).astype(BF16) for k in groups]
    kr_bs = [jnp.concatenate([pg[...] for pg in kr_pages[k:k + POOL_PAGES]], axis=1).astype(BF16) for k in groups]
    s_lat = [_dot_nt(qlat_b, c_b) for c_b in c_bs]
    s_pe = [_dot(qpe_b, kr_b) for kr_b in kr_bs]
    scores = [a + r for a, r in zip(s_lat, s_pe)]
    maxes = [jnp.max(s, axis=-1, keepdims=True) for s in scores]
    ps = [jnp.exp(s - mx) for s, mx in zip(scores, maxes)]
    sums = [jnp.sum(p_g, axis=-1, keepdims=True) for p_g in ps]
    accs = [_dot(p_g.astype(BF16), c_b) for p_g, c_b in zip(ps, c_bs)]
    m_old = m_ref[...]
    m = m_old
    for mx in maxes:
        m = jnp.maximum(m, mx)
    alpha = jnp.exp(m_old - m)
    l = alpha * l_ref[...]
    acc = alpha * acc_ref[...]
    for mx, l_g, acc_g in zip(maxes, sums, accs):
        w = jnp.exp(mx - m)
        l = l + w * l_g
        acc = acc + w * acc_g
    m_ref[...] = m
    l_ref[...] = l
    acc_ref[...] = acc

    @pl.when(step == pl.num_programs(1) - 1)
    def _():
        o_ref[0] = (acc / l)[0:MLA_HEADS]


def _mla_sample(p, lat_cache, kr_t, pt_flat, b, n_pages):
    page = lat_cache.shape[1]
    head = lambda w: pl.BlockSpec((1, MLA_HEADS, w), lambda bb, c, pt: (bb, 0, 0))
    row = lambda w: pl.BlockSpec((1, 1, w), lambda bb, c, pt: (bb, 0, 0))
    gs = pltpu.PrefetchScalarGridSpec(
        num_scalar_prefetch=1, grid=(b, n_pages // PAGES_PER_STEP),
        in_specs=[pl.BlockSpec(memory_space=pl.ANY), pl.BlockSpec(memory_space=pl.ANY),
                  head(MLA_KV_LORA), head(MLA_ROPE), row(MLA_KV_LORA), row(MLA_ROPE)],
        out_specs=head(MLA_KV_LORA),
        scratch_shapes=[pltpu.VMEM((PAD_ROWS, 1), F32), pltpu.VMEM((PAD_ROWS, 1), F32),
                        pltpu.VMEM((PAD_ROWS, MLA_KV_LORA), F32),
                        pltpu.VMEM((2, PAGES_PER_STEP, page, MLA_KV_LORA), F32),
                        pltpu.VMEM((2, PAGES_PER_STEP, MLA_ROPE, page), F32),
                        pltpu.SemaphoreType.DMA((2,)), pltpu.SemaphoreType.DMA((2,))])
    qlat = p["qlat"].reshape(b, MLA_HEADS, MLA_KV_LORA)
    qpe = p["qpe"].reshape(b, MLA_HEADS, MLA_ROPE)
    return pl.pallas_call(
        _mla_sample_kernel, grid_spec=gs,
        out_shape=jax.ShapeDtypeStruct((b, MLA_HEADS, MLA_KV_LORA), F32),
        compiler_params=pltpu.CompilerParams(dimension_semantics=("arbitrary",) * 2, vmem_limit_bytes=VMEM_LIMIT),
        name="mla_sample",
    )(pt_flat, lat_cache, kr_t, qlat, qpe, p["c"], p["kr"])


def _rope_angles(pos, theta, dim):
    half = dim // 2
    inv = 1.0 / (jnp.float32(theta) ** (jnp.arange(half, dtype=F32) / half))
    ang = pos.astype(F32)[:, None] * inv[None, :]
    return jnp.cos(ang), jnp.sin(ang)


def _rope_lane_tables(pos, rows):
    def table(theta, dim, period, active):
        half = dim // 2
        cos, sin = _rope_angles(pos, theta, dim)
        lane = np.arange(LANES)
        d = lane % period
        is_lo = (d < half) & active(lane)
        is_hi = (d >= half) & (d < dim) & active(lane)
        fi = np.where(d < half, d, np.clip(d - half, 0, half - 1))
        cos_l, sin_l = cos[:, fi], sin[:, fi]
        tab = jnp.stack([jnp.where(is_lo | is_hi, cos_l, 1.0), jnp.where(is_lo, -sin_l, 0.0),
                         jnp.where(is_hi, sin_l, 0.0)])
        return jnp.broadcast_to(tab, (3, rows, LANES))

    every = lambda lane: np.ones_like(lane, bool)
    keys_only = lambda lane: (lane % LANES) < HD
    return (table(ROPE_THETA, ROT_DIM, HD, every), table(ROPE_THETA, ROT_DIM, HD, keys_only),
            table(MLA_ROPE_THETA, MLA_ROPE, MLA_ROPE, every))


def _rope_row_tables(pos):
    cq, sq = _rope_angles(pos, ROPE_THETA, ROT_DIM)
    cm, sm = _rope_angles(pos, MLA_ROPE_THETA, MLA_ROPE)
    return jnp.stack([cq.T, sq.T]), jnp.stack([cm.T, sm.T])


def _pack_weights(l, norm_pre, w_in, pe_cmp, q_norm, w_q_up, kv_norm, w_kv_up, w_proj_a, w_proj_b, w_out, norm_post):
    w = w_in[l]
    o = IN_OFFSETS
    seg = lambda k: w[:, o[k]:o[k + 1]]
    gn = seg(4)
    per_group = 3 * NSA_GROUP
    gn_p = jnp.zeros((D_MODEL, NSA_KV_HEADS * LANES), w.dtype)
    gn_t = jnp.zeros((NSA_KV_HEADS * GATE_ROWS, D_MODEL), w.dtype)
    for g in range(NSA_KV_HEADS):
        gn_g = gn[:, g * per_group:(g + 1) * per_group]
        gn_p = gn_p.at[:, g * LANES:g * LANES + per_group].set(gn_g)
        gn_t = gn_t.at[g * GATE_ROWS:g * GATE_ROWS + per_group].set(gn_g.T)
    w_rows = jnp.concatenate([seg(0), seg(1), seg(2), seg(3), gn_p, seg(5), seg(6), seg(7),
                              jnp.tile(seg(8), (1, LANES // MLA_ROPE)), seg(9), seg(10), seg(11)], axis=1)
    w_t = jnp.concatenate([seg(0).T, seg(1).T, seg(2).T, seg(3).T, gn_t, seg(8).T], axis=0)
    w_cols = jnp.concatenate([seg(5), seg(6), seg(7), seg(9), seg(10), seg(11)], axis=1)
    wq = w_q_up[l]
    wqup = jnp.concatenate([wq[..., :MLA_NOPE].reshape(MLA_Q_LORA, -1), wq[..., MLA_NOPE:].reshape(MLA_Q_LORA, -1)],
                           axis=1)
    wkv = w_kv_up[l]
    wuk_pad = jnp.pad(wkv[..., :MLA_NOPE], ((0, 0), (0, 0), (0, LANES - MLA_NOPE)))
    rope_copy = jnp.pad(jnp.eye(MLA_ROPE, dtype=w.dtype), ((0, LANES - MLA_ROPE), (MLA_NOPE, MLA_ROPE)))
    wkx = jnp.concatenate([wuk_pad.reshape(MLA_KV_LORA, MLA_HEADS * LANES), jnp.tile(rope_copy, (1, MLA_HEADS))],
                          axis=0)
    wvt = jnp.transpose(wkv[..., MLA_NOPE:], (1, 2, 0)).reshape(MLA_WIDTH, MLA_KV_LORA)
    w2uk =jnp.transpose(wkv[..., :MLA_NOPE], (1, 2, 0)).reshape(MLA_HEADS // 2, LANES, MLA_KV_LORA)
    wv = jnp.transpose(wkv[..., MLA_NOPE:], (1, 0, 2))
    zeros = jnp.zeros_like(wv)
    even = (jnp.arange(MLA_HEADS) % 2 == 0)[:, None, None]
    wuv = jnp.concatenate([jnp.where(even, wv, zeros), jnp.where(even, zeros, wv)], axis=2)
    return {
        "npre": norm_pre[l][None].astype(F32), "w_rows": w_rows.astype(BF16), "w_t": w_t.astype(BF16),
        "w_cols": w_cols.astype(BF16), "pe": pe_cmp[l].reshape(CMP_BLOCK, KV_WIDTH).astype(F32),
        "qnorm": q_norm[l][None].astype(F32), "wqup": wqup.astype(BF16), "wqupt": wqup.T.astype(BF16),
        "kvnorm": kv_norm[l][None].astype(F32), "w2uk": w2uk.astype(BF16), "wkx": wkx.astype(BF16),
        "wvt": wvt.astype(BF16), "wuv": wuv.astype(BF16),
        "wpa": w_proj_a[l].astype(BF16), "wpb": w_proj_b[l].astype(BF16), "wout": w_out[l].astype(BF16),
        "npost": norm_post[l][None].astype(F32),
    }


def _rows_from_cols(a):
    b, _, t = a.shape
    return a.reshape(b, NSA_KV_HEADS, 2, HD, t).transpose(0, 4, 1, 2, 3)


def _cols_from_rows(a):
    n, t = a.shape[:2]
    return a.transpose(0, 2, 3, 4, 1).reshape(n, KV_WIDTH, t)


def _prompt_layer(x, wts):
    b, t, _ = x.shape
    assert t % PROMPT_TM == 0
    p = _in_project_cols(x, _rope_row_tables(jnp.arange(t)), wts)
    o_a = _nsa_prompt(p, b, t)
    o_b = _mla_prompt(p, wts, b, t)
    y = _merge(x, o_a, o_b, p, wts, 512, from_latent=False)
    win_keep = min(WINDOW, t)
    return y, (_rows_from_cols(p["cmp"]), _rows_from_cols(p["slc"]), p["c"], p["kr"].transpose(0, 2, 1),
               _rows_from_cols(p["win"][:, :, t - win_keep:]))


def _sample_layer(x, l, caches, state_win, page_table, wts):
    cache_cmp, cache_slc, cache_lat, cache_kr = caches
    b, s_new, _ = x.shape
    assert s_new == 1
    n_pages = page_table.shape[1]
    page = cache_cmp.shape[2]
    past = n_pages * page
    assert past % SEL_BLOCK == 0 and n_pages % PAGES_PER_STEP == 0 and page == LANES
    win_len = state_win.shape[1]
    assert win_len == WINDOW and past >= WINDOW
    pt_flat = page_table.reshape(-1).astype(jnp.int32)

    tabs = _rope_lane_tables(jnp.full((1,), past, jnp.int32), b)
    p = _in_project_rows(x.reshape(1, b, D_MODEL), tabs, wts)
    p = {k: v.reshape(b, 1, v.shape[-1]) for k, v in p.items()}

    kc_all = _compress_pages(_cols_from_rows(cache_cmp[l]), pt_flat, wts["pe"], b, n_pages)
    oc, idx = _sample_select(p["q"], kc_all, b)
    o_a, new_win = _sample_attend(p, oc, idx.reshape(-1), pt_flat, _cols_from_rows(cache_slc[l]),
                                  _cols_from_rows(state_win), b, n_pages)
    o_lat = _mla_sample(p, cache_lat[l], cache_kr[l].transpose(0, 2, 1), pt_flat, b, n_pages)
    pm = {k: p[k].reshape(1, b, -1) for k in ("ga", "gb", "ma", "mb")}
    y = _merge(x.reshape(1, b, D_MODEL), o_a.reshape(1, b, NSA_WIDTH),
               o_lat.reshape(1, b, MLA_HEADS * MLA_KV_LORA), pm, wts, b, from_latent=True)
    kv6 = lambda a: a.reshape(b, 1, NSA_KV_HEADS, 2, HD)
    return y.reshape(b, 1, D_MODEL), (kv6(p["cmp"]), kv6(p["slc"]), p["c"], p["kr"], _rows_from_cols(new_win))


def kernel(x_prompt, x_sample, cache_nsa_cmp, cache_nsa_slc, cache_mla_latent, cache_mla_krope, state_nsa_win,
           page_table, norm_pre, w_in, pe_cmp, q_norm, w_q_up, kv_norm, w_kv_up, w_proj_a, w_proj_b, w_out,
           norm_post):
    depth = w_in.shape[0]
    hp, hs = x_prompt, x_sample
    new_p, new_s = [], []
    for l in range(depth):
        wts = _pack_weights(l, norm_pre, w_in, pe_cmp, q_norm, w_q_up, kv_norm, w_kv_up, w_proj_a, w_proj_b,
                            w_out, norm_post)
        hp, sp = _prompt_layer(hp, wts)
        hs, ss = _sample_layer(hs, l, (cache_nsa_cmp, cache_nsa_slc, cache_mla_latent, cache_mla_krope),
                               state_nsa_win[l], page_table, wts)
        new_p.append(sp)
        new_s.append(ss)
    stack = lambda items, k: jnp.stack([s[k] for s in items])
    return (hp, hs) + tuple(stack(new_p, k) for k in range(5)) + tuple(stack(new_s, k) for k in range(5))
```

```python
import functools

import numpy as np
import jax
import jax.numpy as jnp
from jax import lax
from jax.experimental import pallas as pl
from jax.experimental.pallas import tpu as pltpu

D_MODEL = 1024
NSA_HEADS = 8
NSA_KV_HEADS = 2
NSA_GROUP = NSA_HEADS // NSA_KV_HEADS
HD = 64
NSA_WIDTH = NSA_HEADS * HD
KV_WIDTH = NSA_KV_HEADS * 2 * HD
ROT_DIM = HD // 4
ROPE_THETA = 500000.0
CMP_BLOCK = 32
SEL_BLOCK = 64
N_SELECT = 16
WINDOW = 512
NSA_SCALE = HD ** -0.5

MLA_HEADS = 8
MLA_Q_LORA = 384
MLA_KV_LORA = 256
MLA_NOPE = 64
MLA_ROPE = 32
MLA_V = 64
MLA_WIDTH = MLA_HEADS * MLA_V
MLA_ROPE_THETA = 10000.0
MLA_SCALE = (MLA_NOPE + MLA_ROPE) ** -0.5
LOG2E = 1.4426950408889634

RMS_EPS = 1e-6
NEG = -1e30
FORCE_SCORE = 1e4

IN_SPLITS = (NSA_WIDTH, KV_WIDTH, KV_WIDTH, KV_WIDTH, 3 * NSA_HEADS, NSA_WIDTH,
             MLA_Q_LORA, MLA_KV_LORA, MLA_ROPE, MLA_WIDTH, D_MODEL, D_MODEL)
IN_OFFSETS = tuple(int(v) for v in np.cumsum((0,) + IN_SPLITS))

LANES = 128
HALF = LANES // 2
GATE_ROWS = 16
PAD_ROWS = 16

PT_Q = (0, 512)
PT_CMP = (512, 768)
PT_SLC = (768, 1024)
PT_WIN = (1024, 1280)
PT_GN = (1280, 1280 + NSA_KV_HEADS * GATE_ROWS)
PT_KRP = (PT_GN[1], PT_GN[1] + MLA_ROPE)
PT_ROWS = PT_KRP[1]
PR_GA = (0, 512)
PR_QD = (512, 896)
PR_KVD = (896, 1152)
PR_GB = (1152, 1664)
PR_MA = (1664, 2688)
PR_MB = (2688, 3712)
PR_COLS = 3712

PROMPT_TM = 512
ATT_TQ = 256
ATT_TK = 256
MLA_TK = 256

VMEM_LIMIT = 48 * 1024 * 1024
BF16 = jnp.bfloat16
F32 = jnp.float32


def _full_spec(shape):
    nd = len(shape)
    return pl.BlockSpec(shape, lambda *_: (0,) * nd)


def _lane_lo(rows):
    return lax.broadcasted_iota(jnp.int32, (rows, LANES), 1) < HALF


def _dot(a, b):
    return jnp.dot(a, b, preferred_element_type=F32)


def _dot_nt(a, b):
    return lax.dot_general(a, b, (((1,), (1,)), ((), ())), preferred_element_type=F32)


def _rms(v, gain):
    return v * lax.rsqrt(jnp.mean(v * v, axis=-1, keepdims=True) + RMS_EPS) * gain


def _split_bf16(v):
    hi = v.astype(BF16)
    return hi, (v - hi.astype(F32)).astype(BF16)


def _rope_tiles(v, tab_ref, shift):
    c, s_lo, s_hi = tab_ref[0], tab_ref[1], tab_ref[2]
    out = []
    for k in range(v.shape[1] // LANES):
        a = v[:, k * LANES:(k + 1) * LANES]
        out.append(a * c + pltpu.roll(a, LANES - shift, 1) * s_lo + pltpu.roll(a, shift, 1) * s_hi)
    return out[0] if len(out) == 1 else jnp.concatenate(out, axis=1)


def _dup_kv(a, lo):
    r = pltpu.roll(a, HALF, 1)
    return jnp.where(lo, a, r), jnp.where(lo, r, a)


def _kv_pack(v):
    lo = _lane_lo(v.shape[0])
    parts = []
    for g in range(NSA_KV_HEADS):
        kk, vv = _dup_kv(v[:, g * LANES:(g + 1) * LANES], lo)
        parts += [kk, vv]
    return jnp.concatenate(parts, axis=1).astype(BF16)


def _inproj_rows_kernel(x_ref, npre_ref, wt_ref, wkrp_ref, w_ref, tq_ref, tkv_ref, tm_ref, qnorm_ref, wqup_ref,
                        kvnorm_ref, w2uk_ref, q_ref, qrot_ref, gate_ref, cmp_ref, slc_ref, win_ref, slckv_ref,
                        winkv_ref, ga_ref, gb_ref, ma_ref, mb_ref, qpe_ref, c_ref, kr_ref, qlat_ref):
    xb = _rms(x_ref[0], npre_ref[...]).astype(BF16)
    segt = lambda lohi: _dot_nt(xb, wt_ref[lohi[0]:lohi[1], :])
    seg = lambda lohi: _dot(xb, w_ref[:, lohi[0]:lohi[1]])

    a = segt((PT_Q[0], PT_WIN[1]))
    q = a[:, :NSA_WIDTH]
    q_ref[0] = (q * NSA_SCALE).astype(BF16)
    qrot_ref[0] = (_rope_tiles(q, tq_ref, ROT_DIM // 2) * NSA_SCALE).astype(BF16)
    cmp_ref[0] = a[:, NSA_WIDTH:NSA_WIDTH + KV_WIDTH]
    kvs = _rope_tiles(a[:, NSA_WIDTH + KV_WIDTH:NSA_WIDTH + 2 * KV_WIDTH], tkv_ref, ROT_DIM // 2)
    slc_ref[0] = kvs
    slckv_ref[0] = _kv_pack(kvs)
    kvw = _rope_tiles(a[:, NSA_WIDTH + 2 * KV_WIDTH:], tkv_ref, ROT_DIM // 2)
    win_ref[0] = kvw
    winkv_ref[0] = _kv_pack(kvw)

    gate_ref[0] = jax.nn.sigmoid(segt(PT_GN))
    ga_ref[0] = seg(PR_GA)
    gb_ref[0] = seg(PR_GB)
    ma_ref[0] = seg(PR_MA)
    mb_ref[0] = seg(PR_MB)

    qd = _rms(seg(PR_QD), qnorm_ref[...]).astype(BF16)
    qh = _dot(qd, wqup_ref[...])
    qn = qh[:, :MLA_HEADS * MLA_NOPE].astype(BF16)
    qpe_ref[0] = _rope_tiles(qh[:, MLA_HEADS * MLA_NOPE:], tm_ref, MLA_ROPE // 2) * MLA_SCALE
    c_ref[0] = _rms(seg(PR_KVD), kvnorm_ref[...])
    kr_ref[0] = _rope_tiles(_dot_nt(xb, wkrp_ref[...]), tm_ref, MLA_ROPE // 2)[:, :MLA_ROPE]

    rows = qn.shape[0]
    lo = _lane_lo(rows)
    z = jnp.zeros((rows, LANES), BF16)
    parts = []
    for j in range(MLA_HEADS // 2):
        pair = qn[:, j * LANES:(j + 1) * LANES]
        parts.append(_dot(jnp.where(lo, pair, z), w2uk_ref[j]) * MLA_SCALE)
        parts.append(_dot(jnp.where(lo, z, pair), w2uk_ref[j]) * MLA_SCALE)
    qlat_ref[0] = jnp.concatenate(parts, axis=1)


def _in_project_rows(x3, tabs, wts):
    b, t, _ = x3.shape
    tq, tkv, tmla = tabs
    row = lambda w, dt: jax.ShapeDtypeStruct((b, t, w), dt)
    names = ["q", "qrot", "gate", "cmp", "slc", "win", "slckv", "winkv", "ga", "gb", "ma", "mb", "qpe", "c", "kr",
             "qlat"]
    out_shape = [row(512, BF16), row(512, BF16), row(NSA_KV_HEADS * GATE_ROWS, F32), row(256, F32), row(256, F32),
                 row(256, F32),
                 row(512, BF16), row(512, BF16), row(512, F32), row(512, F32), row(1024, F32), row(1024, F32),
                 row(MLA_HEADS * MLA_ROPE, F32), row(MLA_KV_LORA, F32), row(MLA_ROPE, F32),
                 row(MLA_HEADS * MLA_KV_LORA, F32)]
    tok = lambda w: pl.BlockSpec((1, t, w), lambda bb: (bb, 0, 0))
    tab = _full_spec((3, t, LANES))
    in_specs = [tok(D_MODEL), _full_spec((1, D_MODEL)), _full_spec((PT_ROWS, D_MODEL)), _full_spec((LANES, D_MODEL)),
                _full_spec((D_MODEL, PR_COLS)), tab, tab, tab, _full_spec((1, MLA_Q_LORA)), _full_spec((MLA_Q_LORA, MLA_HEADS * (MLA_NOPE + MLA_ROPE))),
                _full_spec((1, MLA_KV_LORA)), _full_spec((MLA_HEADS // 2, LANES, MLA_KV_LORA))]
    res = pl.pallas_call(
        _inproj_rows_kernel, grid=(b,), in_specs=in_specs, out_specs=[tok(s.shape[2]) for s in out_shape],
        out_shape=out_shape,
        compiler_params=pltpu.CompilerParams(dimension_semantics=("arbitrary",), vmem_limit_bytes=VMEM_LIMIT),
        name="in_project_rows",
    )(x3, wts["npre"], wts["w_t"], wts["w_krp4"], wts["w_cols"], tq, tkv, tmla, wts["qnorm"], wts["wqup"],
      wts["kvnorm"], wts["w2uk"])
    return dict(zip(names, res))


def _rope_rows(x, cos, sin, half):
    x1, x2 = x[0:half], x[half:2 * half]
    parts = [x1 * cos - x2 * sin, x1 * sin + x2 * cos]
    if x.shape[0] > 2 * half:
        parts.append(x[2 * half:])
    return jnp.concatenate(parts, axis=0)


def _store_chunks(ref, v):
    tk = ref.shape[3]
    for j in range(ref.shape[1]):
        ref[0, j] = v[:, j * tk:(j + 1) * tk]


def _inproj_cols_kernel(x_ref, npre_ref, wt_ref, w_ref, ropeq_ref, ropem_ref, pe_ref, qnorm_ref, wqupt_ref,
                        kvnorm_ref, pool_ref, wkx_ref, wvt_ref,
                        qt_ref, qrt_ref, gate_ref, cmp_ref, slc_ref, win_ref, slcb_ref, winb_ref, slcr_ref,
                        winr_ref, kc_ref, ga_ref, gb_ref, ma_ref, mb_ref, qmt_ref, c_ref, kmla_ref, vmt_ref,
                        kr_ref):
    tm = x_ref.shape[1]
    xb = _rms(x_ref[0], npre_ref[...]).astype(BF16)
    segt = lambda lohi: _dot_nt(wt_ref[lohi[0]:lohi[1], :], xb)
    seg = lambda lohi: _dot(xb, w_ref[:, lohi[0]:lohi[1]])
    cq, sq = ropeq_ref[0], ropeq_ref[1]
    cm, sm = ropem_ref[0], ropem_ref[1]
    hq, hm = ROT_DIM // 2, MLA_ROPE // 2

    qt = segt(PT_Q)
    qt_ref[0] = (qt * (NSA_SCALE * LOG2E)).astype(BF16)
    qrt = jnp.concatenate([_rope_rows(qt[h * HD:(h + 1) * HD], cq, sq, hq) for h in range(NSA_HEADS)], axis=0)
    qrt_ref[0] = (qrt * (NSA_SCALE * LOG2E)).astype(BF16)

    cmpt = segt(PT_CMP)
    cmp_ref[0] = cmpt
    hi, lo = _split_bf16(cmpt)
    pool = pool_ref[...]
    pooled = (_dot_nt(pool, hi) + _dot_nt(pool, lo))[0:tm // CMP_BLOCK]
    kc_ref[0] = (pooled + jnp.sum(pe_ref[...], axis=0, keepdims=True)) * (1.0 / CMP_BLOCK)

    def rope_kv(v):
        parts = []
        for g in range(NSA_KV_HEADS):
            parts.append(_rope_rows(v[g * LANES:g * LANES + HD], cq, sq, hq))
            parts.append(v[g * LANES + HD:(g + 1) * LANES])
        return jnp.concatenate(parts, axis=0)

    slct = rope_kv(segt(PT_SLC))
    slc_ref[0] = slct
    _store_chunks(slcb_ref, slct.astype(BF16))
    slcr_ref[0] = slct.T.astype(BF16)
    wint = rope_kv(segt(PT_WIN))
    win_ref[0] = wint
    _store_chunks(winb_ref, wint.astype(BF16))
    winr_ref[0] = wint.T.astype(BF16)

    gate_ref[0] = jax.nn.sigmoid(segt(PT_GN))
    krt = _rope_rows(segt(PT_KRP), cm, sm, hm)
    kr_ref[0] = krt
    c = _rms(seg(PR_KVD), kvnorm_ref[...])
    c_ref[0] = c
    c_b = c.astype(BF16)
    kr_rows = jnp.concatenate([krt, jnp.zeros((LANES - MLA_ROPE, tm), F32)], axis=0).T
    ckr = jnp.concatenate([c_b, kr_rows.astype(BF16)], axis=1)
    kmla_ref[0] = _dot(ckr, wkx_ref[...]).astype(BF16)
    _store_chunks(vmt_ref, _dot_nt(wvt_ref[...], c_b).astype(BF16))

    qd = _rms(seg(PR_QD), qnorm_ref[...]).astype(BF16)
    qht = _dot_nt(wqupt_ref[...], qd)
    n_nope = MLA_HEADS * MLA_NOPE
    zq = jnp.zeros((LANES - MLA_NOPE - MLA_ROPE, tm), F32)
    parts = []
    for h in range(MLA_HEADS):
        parts += [qht[h * MLA_NOPE:(h + 1) * MLA_NOPE],
                  _rope_rows(qht[n_nope + h * MLA_ROPE:n_nope + (h + 1) * MLA_ROPE], cm, sm, hm), zq]
    qmt_ref[0] = (jnp.concatenate(parts, axis=0) * (MLA_SCALE * LOG2E)).astype(BF16)

    ga_ref[0] = seg(PR_GA)
    gb_ref[0] = seg(PR_GB)
    ma_ref[0] = seg(PR_MA)
    mb_ref[0] = seg(PR_MB)


def _in_project_cols(x3, ropes, wts):
    b, t, _ = x3.shape
    tm = PROMPT_TM
    nt = t // tm
    ropeq, ropem = ropes
    pool = np.zeros((16, tm), np.float32)
    for s in range(tm):
        pool[s // CMP_BLOCK, s] = 1.0
    sds = jax.ShapeDtypeStruct
    rows = lambda w: pl.BlockSpec((1, tm, w), lambda i, bb: (bb, i, 0))
    cols = lambda w: pl.BlockSpec((1, w, tm), lambda i, bb: (bb, 0, i))
    chunk = lambda w, tk: pl.BlockSpec((1, tm // tk, w, tk), lambda i, bb: (bb, i, 0, 0))
    outs = [
        ("qt", sds((b, NSA_WIDTH, t), BF16), cols(NSA_WIDTH)),
        ("qrt", sds((b, NSA_WIDTH, t), BF16), cols(NSA_WIDTH)),
        ("gate", sds((b, NSA_KV_HEADS * GATE_ROWS, t), F32), cols(NSA_KV_HEADS * GATE_ROWS)),
        ("cmp", sds((b, KV_WIDTH, t), F32), cols(KV_WIDTH)),
        ("slc", sds((b, KV_WIDTH, t), F32), cols(KV_WIDTH)),
        ("win", sds((b, KV_WIDTH, t), F32), cols(KV_WIDTH)),
        ("slcb", sds((b, t // ATT_TK, KV_WIDTH, ATT_TK), BF16), chunk(KV_WIDTH, ATT_TK)),
        ("winb", sds((b, t // ATT_TK, KV_WIDTH, ATT_TK), BF16), chunk(KV_WIDTH, ATT_TK)),
        ("slcr", sds((b, t, KV_WIDTH), BF16), rows(KV_WIDTH)),
        ("winr", sds((b, t, KV_WIDTH), BF16), rows(KV_WIDTH)),
        ("kc", sds((b, t // CMP_BLOCK, KV_WIDTH), F32),
         pl.BlockSpec((1, tm // CMP_BLOCK, KV_WIDTH), lambda i, bb: (bb, i, 0))),
        ("ga", sds((b, t, NSA_WIDTH), F32), rows(NSA_WIDTH)),
        ("gb", sds((b, t, MLA_WIDTH), F32), rows(MLA_WIDTH)),
        ("ma", sds((b, t, D_MODEL), F32), rows(D_MODEL)),
        ("mb", sds((b, t, D_MODEL), F32), rows(D_MODEL)),
        ("qmt", sds((b, MLA_HEADS * LANES, t), BF16), cols(MLA_HEADS * LANES)),
        ("c", sds((b, t, MLA_KV_LORA), F32), rows(MLA_KV_LORA)),
        ("kmla", sds((b, t, MLA_HEADS * LANES), BF16), rows(MLA_HEADS * LANES)),
        ("vmt", sds((b, t // MLA_TK, MLA_WIDTH, MLA_TK), BF16), chunk(MLA_WIDTH, MLA_TK)),
        ("kr", sds((b, MLA_ROPE, t), F32), cols(MLA_ROPE)),
    ]
    rope_spec = lambda half: pl.BlockSpec((2, half, tm), lambda i, bb: (0, 0, i))
    in_specs = [rows(D_MODEL), _full_spec((1, D_MODEL)), _full_spec((PT_ROWS, D_MODEL)),
                _full_spec((D_MODEL, PR_COLS)), rope_spec(ROT_DIM // 2), rope_spec(MLA_ROPE // 2),
                _full_spec((CMP_BLOCK, KV_WIDTH)), _full_spec((1, MLA_Q_LORA)),
                _full_spec((MLA_HEADS * (MLA_NOPE + MLA_ROPE), MLA_Q_LORA)), _full_spec((1, MLA_KV_LORA)),
                _full_spec((16, tm)), _full_spec((MLA_KV_LORA + LANES, MLA_HEADS * LANES)),
                _full_spec((MLA_WIDTH, MLA_KV_LORA))]
    res = pl.pallas_call(
        _inproj_cols_kernel, grid=(nt, b), in_specs=in_specs, out_specs=[o[2] for o in outs],
        out_shape=[o[1] for o in outs],
        compiler_params=pltpu.CompilerParams(dimension_semantics=("arbitrary", "arbitrary"),
                                             vmem_limit_bytes=VMEM_LIMIT),
        name="in_project_cols",
    )(x3, wts["npre"], wts["w_t"], wts["w_cols"], ropeq, ropem, wts["pe"], wts["qnorm"], wts["wqupt"],
      wts["kvnorm"], jnp.asarray(pool, BF16), wts["wkx"], wts["wvt"])
    return dict(zip([o[0] for o in outs], res))


def _online_update(s, vt, m, l, acc):
    d, keys = vt.shape
    m_new = jnp.maximum(m, jnp.max(s, axis=0, keepdims=True))
    p = jnp.exp2(s - m_new).astype(BF16)
    alpha = jnp.exp2(m - m_new)
    pv = _dot(jnp.concatenate([vt, jnp.ones((PAD_ROWS, keys), BF16)], axis=0), p)
    return m_new, alpha * l + pv[d:d + 1], alpha * acc + pv[0:d]


def _nsa_prompt_kernel(n_blk, qt_ref, qrt_ref, gate_ref, kc_ref, ks_ref, kst_ref, kw_ref, kwt_ref, o_ref, sb_ref):
    i = pl.program_id(2)
    tq, tk = ATT_TQ, ATT_TK
    nl = NSA_GROUP * tq
    zq = jnp.zeros((HD, tq), BF16)

    def widen(qt):
        return jnp.concatenate([jnp.concatenate([qt[r * HD:(r + 1) * HD], zq], axis=0)
                                for r in range(NSA_GROUP)], axis=1)

    qc = widen(qt_ref[0])
    qr = widen(qrt_ref[0])

    n_cmp = 2 * n_blk
    kc = kc_ref[0]
    s_c = _dot(kc.astype(BF16), qc)
    rho = lax.broadcasted_iota(jnp.int32, (n_cmp, nl), 0)
    cmp_idx = jnp.where(rho < n_blk, 2 * rho, 2 * (rho - n_blk) + 1)
    t_l = i * tq + (lax.broadcasted_iota(jnp.int32, (n_cmp, nl), 1) & (tq - 1))
    mask_c = cmp_idx * CMP_BLOCK + (CMP_BLOCK - 1) <= t_l
    s_c = jnp.where(mask_c, s_c, NEG)
    e_c = jnp.where(mask_c, jnp.exp2(s_c - jnp.max(s_c, axis=0, keepdims=True)), 0.0)
    p_c = e_c / jnp.maximum(jnp.sum(e_c, axis=0, keepdims=True), 1e-30)
    kct = jnp.concatenate([kc, jnp.zeros((LANES - n_cmp, LANES), F32)], axis=0).T
    p_pad = jnp.concatenate([p_c, jnp.zeros((LANES - n_cmp, nl), F32)], axis=0)
    o_c = _dot(kct[HD:2 * HD].astype(BF16), p_pad.astype(BF16))

    imp = p_c[:, 0:tq]
    for r in range(1, NSA_GROUP):
        imp = imp + p_c[:, r * tq:(r + 1) * tq]
    imp_blk = imp[0:n_blk] + imp[n_blk:n_cmp]
    blk = lax.broadcasted_iota(jnp.int32, (n_blk, tq), 0)
    t_q = i * tq + lax.broadcasted_iota(jnp.int32, (n_blk, tq), 1)
    ahead_of = t_q - blk * SEL_BLOCK
    score = jnp.where(blk == 0, FORCE_SCORE,
                      jnp.where(ahead_of < 0, -FORCE_SCORE, jnp.where(ahead_of < SEL_BLOCK, FORCE_SCORE, imp_blk)))
    rank = jnp.zeros((n_blk, tq), F32)
    for j in range(n_blk):
        other = score[j:j + 1, :]
        tie = jnp.where(blk > j, 1.0, 0.0)
        rank = rank + jnp.where(other > score, 1.0, jnp.where(other == score, tie, 0.0))
    sb_ref[...] = jnp.where(rank < N_SELECT, jnp.where(score > -1.0, 0.0, NEG), NEG)

    key_r = lax.broadcasted_iota(jnp.int32, (tk, tq), 0)
    t_k = i * tq + lax.broadcasted_iota(jnp.int32, (tk, tq), 1)
    n_chunks = ((i + 1) * tq + tk - 1) // tk
    c_lo = jnp.maximum(i * tq - WINDOW, 0) // tk
    init = (jnp.full((1, nl), NEG, F32), jnp.zeros((1, nl), F32), jnp.zeros((HD, nl), F32))
    per_chunk = tk // SEL_BLOCK
    tile = lambda bias: jnp.concatenate([bias] * NSA_GROUP, axis=1)

    def sel_bias(c):
        rows = [jnp.broadcast_to(sb_ref[pl.ds(c * per_chunk + j, 1), :], (SEL_BLOCK, tq)) for j in range(per_chunk)]
        return jnp.concatenate(rows, axis=0)

    def past_body(c, carry):
        base = pl.multiple_of(c * tk, tk)
        s = _dot(ks_ref[0, pl.ds(base, tk), :], qr) + tile(sel_bias(c))
        return _online_update(s, kst_ref[0, c, HD:2 * HD, :], *carry)

    def near_body(c, carry):
        base = pl.multiple_of(c * tk, tk)
        s = _dot(ks_ref[0, pl.ds(base, tk), :], qr)
        w = _dot(kw_ref[0, pl.ds(base, tk), :], qr)
        dist = t_k - (base + key_r)
        causal = jnp.where(dist >= 0, 0.0, NEG)
        s = s + tile(sel_bias(c) + causal)
        w = w + tile(jnp.where(dist <= WINDOW, causal, NEG))
        (m_s, l_s, a_s), (m_w, l_w, a_w) = carry[:3], carry[3:]
        ms_new = jnp.maximum(m_s, jnp.max(s, axis=0, keepdims=True))
        mw_new = jnp.maximum(m_w, jnp.max(w, axis=0, keepdims=True))
        p_s = jnp.exp2(s - ms_new).astype(BF16)
        p_w = jnp.exp2(w - mw_new).astype(BF16)
        ones = jnp.ones((PAD_ROWS, tk), BF16)
        pv_s = _dot(jnp.concatenate([kst_ref[0, c, HD:2 * HD, :], ones], axis=0), p_s)
        pv_w = _dot(jnp.concatenate([kwt_ref[0, c, HD:2 * HD, :], ones], axis=0), p_w)
        al_s = jnp.exp2(m_s - ms_new)
        al_w = jnp.exp2(m_w - mw_new)
        return (ms_new, al_s * l_s + pv_s[HD:HD + 1], al_s * a_s + pv_s[0:HD],
                mw_new, al_w * l_w + pv_w[HD:HD + 1], al_w * a_w + pv_w[0:HD])

    far = lax.fori_loop(0, c_lo, past_body, init)
    _, l_s, a_s, _, l_w, a_w = lax.fori_loop(c_lo, n_chunks, near_body, far + init)
    o_s = a_s / l_s
    o_w = a_w / l_w

    gate = gate_ref[0]
    heads = []
    for r in range(NSA_GROUP):
        sl = slice(r * tq, (r + 1) * tq)
        heads.append(gate[3 * r:3 * r + 1] * o_c[:, sl] + gate[3 * r + 1:3 * r + 2] * o_s[:, sl]
                     + gate[3 * r + 2:3 * r + 3] * o_w[:, sl])
    o_ref[0] = jnp.concatenate(heads, axis=0).T


def _nsa_prompt(p, b, t):
    tq, tk = ATT_TQ, ATT_TK
    n_blk = t // SEL_BLOCK
    assert 2 * n_blk <= LANES and t % tk == 0
    kc = p["kc"].reshape(b, n_blk, 2, KV_WIDTH).transpose(0, 2, 1, 3).reshape(b, 2 * n_blk, KV_WIDTH)
    qspec = pl.BlockSpec((1, NSA_GROUP * HD, tq), lambda bb, g, i: (bb, g, i))
    rm = pl.BlockSpec((1, t, LANES), lambda bb, g, i: (bb, 0, g))
    fm = pl.BlockSpec((1, t // tk, LANES, tk), lambda bb, g, i: (bb, 0, g, 0))
    return pl.pallas_call(
        functools.partial(_nsa_prompt_kernel, n_blk),
        grid=(b, NSA_KV_HEADS, t // tq),
        in_specs=[qspec, qspec, pl.BlockSpec((1, GATE_ROWS, tq), lambda bb, g, i: (bb, g, i)),
                  pl.BlockSpec((1, 2 * n_blk, LANES), lambda bb, g, i: (bb, 0, g)), rm, fm, rm, fm],
        out_specs=pl.BlockSpec((1, tq, NSA_GROUP * HD), lambda bb, g, i: (bb, i, g)),
        out_shape=jax.ShapeDtypeStruct((b, t, NSA_WIDTH), F32),
        scratch_shapes=[pltpu.VMEM((n_blk, tq), F32)],
        compiler_params=pltpu.CompilerParams(dimension_semantics=("arbitrary",) * 3, vmem_limit_bytes=VMEM_LIMIT),
        name="nsa_prompt",
    )(p["qt"], p["qrt"], p["gate"], kc, p["slcr"], p["slcb"], p["winr"], p["winb"])


MLA_TQ = 256


def _mla_prompt_kernel(qmt_ref, k_ref, vt_ref, o_ref, m_ref, l_ref, acc_ref):
    i = pl.program_id(1)
    tq, tk = MLA_TQ, MLA_TK
    m_ref[...] = jnp.full(m_ref.shape, NEG, F32)
    l_ref[...] = jnp.zeros(l_ref.shape, F32)
    acc_ref[...] = jnp.zeros(acc_ref.shape, F32)
    key_r = lax.broadcasted_iota(jnp.int32, (tk, tq), 0)
    t_k = i * tq + lax.broadcasted_iota(jnp.int32, (tk, tq), 1)
    n_chunks = ((i + 1) * tq + tk - 1) // tk

    def step(c, masked):
        base = pl.multiple_of(c * tk, tk)
        old = [(m_ref[h], l_ref[h], acc_ref[h]) for h in range(MLA_HEADS)]
        scores = []
        for h in range(MLA_HEADS):
            k_h = k_ref[0, pl.ds(base, tk), h * LANES:(h + 1) * LANES]
            s = _dot(k_h, qmt_ref[0, h * LANES:(h + 1) * LANES, :])
            scores.append(jnp.where(base + key_r <= t_k, s, NEG) if masked else s)
        new = [_online_update(scores[h], vt_ref[0, c, h * MLA_V:(h + 1) * MLA_V, :], *old[h])
               for h in range(MLA_HEADS)]
        for h in range(MLA_HEADS):
            m_ref[h], l_ref[h], acc_ref[h] = new[h]
        return 0

    n_past = (i * tq + 1) // tk
    lax.fori_loop(0, n_past, lambda c, z: step(c, False), 0)
    lax.fori_loop(n_past, n_chunks, lambda c, z: step(c, True), 0)
    outs = [acc_ref[h] / l_ref[h] for h in range(MLA_HEADS)]
    o_ref[0] = jnp.concatenate(outs, axis=0).T


def _mla_prompt(p, wts, b, t):
    tq, tk = MLA_TQ, MLA_TK
    assert t % tq == 0
    return pl.pallas_call(
        _mla_prompt_kernel,
        grid=(b, t // tq),
        in_specs=[pl.BlockSpec((1, MLA_HEADS * LANES, tq), lambda bb, i: (bb, 0, i)),
                  pl.BlockSpec((1, t, MLA_HEADS * LANES), lambda bb, i: (bb, 0, 0)),
                  pl.BlockSpec((1, t // tk, MLA_WIDTH, tk), lambda bb, i: (bb, 0, 0, 0))],
        out_specs=pl.BlockSpec((1, tq, MLA_WIDTH), lambda bb, i: (bb, i, 0)),
        out_shape=jax.ShapeDtypeStruct((b, t, MLA_WIDTH), F32),
        scratch_shapes=[pltpu.VMEM((MLA_HEADS, 1, tq), F32), pltpu.VMEM((MLA_HEADS, 1, tq), F32),
                        pltpu.VMEM((MLA_HEADS, MLA_V, tq), F32)],
        compiler_params=pltpu.CompilerParams(dimension_semantics=("arbitrary",) * 2, vmem_limit_bytes=VMEM_LIMIT),
        name="mla_prompt",
    )(p["qmt"], p["kmla"], p["vmt"])


def _merge_kernel(from_latent, x_ref, oa_ref, ob_ref, ga_ref, gb_ref, ma_ref, mb_ref, wpa_ref, wpb_ref, wout_ref,
                  npost_ref, wuv_ref, y_ref):
    if from_latent:
        lat = ob_ref[0].astype(BF16)
        parts = []
        for j in range(MLA_HEADS // 2):
            parts.append(_dot(lat[:, 2 * j * MLA_KV_LORA:(2 * j + 1) * MLA_KV_LORA], wuv_ref[2 * j])
                         + _dot(lat[:, (2 * j + 1) * MLA_KV_LORA:(2 * j + 2) * MLA_KV_LORA], wuv_ref[2 * j + 1]))
        o_b = jnp.concatenate(parts, axis=1)
    else:
        o_b = ob_ref[0]
    ga = ga_ref[0]
    gb = gb_ref[0]
    pa = _dot((oa_ref[0] * (ga * jax.nn.sigmoid(ga))).astype(BF16), wpa_ref[...])
    pb = _dot((o_b * (gb * jax.nn.sigmoid(gb))).astype(BF16), wpb_ref[...])
    h = jax.nn.sigmoid(ma_ref[0]) * pa + jax.nn.sigmoid(mb_ref[0]) * pb
    z = _dot(h.astype(BF16), wout_ref[...])
    y_ref[0] = x_ref[0] + _rms(z, npost_ref[...])


def _merge(x3, o_a, o_b, p, wts, tm, from_latent):
    b, t, _ = x3.shape
    tok = lambda w: pl.BlockSpec((1, tm, w), lambda bb, i: (bb, i, 0))
    return pl.pallas_call(
        functools.partial(_merge_kernel, from_latent),
        grid=(b, t // tm),
        in_specs=[tok(D_MODEL), tok(NSA_WIDTH), tok(o_b.shape[2]), tok(NSA_WIDTH), tok(MLA_WIDTH), tok(D_MODEL),
                  tok(D_MODEL), _full_spec((NSA_WIDTH, D_MODEL)), _full_spec((MLA_WIDTH, D_MODEL)),
                  _full_spec((D_MODEL, D_MODEL)), _full_spec((1, D_MODEL)),
                  _full_spec((MLA_HEADS, MLA_KV_LORA, LANES))],
        out_specs=tok(D_MODEL),
        out_shape=jax.ShapeDtypeStruct((b, t, D_MODEL), F32),
        compiler_params=pltpu.CompilerParams(dimension_semantics=("arbitrary",) * 2, vmem_limit_bytes=VMEM_LIMIT),
        name="merge",
    )(x3, o_a, o_b, p["ga"], p["gb"], p["ma"], p["mb"], wts["wpa"], wts["wpb"], wts["wout"], wts["npost"],
      wts["wuv"])


PAGES_PER_STEP = 32
POOL_PAGES = 4


def _page_copy(pt_ref, cache_ref, buf_ref, sem_ref, step, slot, k):
    return pltpu.make_async_copy(cache_ref.at[pt_ref[step * PAGES_PER_STEP + k]], buf_ref.at[slot, k],
                                 sem_ref.at[slot])


def _paged_pipeline(pt_ref, streams):
    step = pl.program_id(0) * pl.num_programs(1) + pl.program_id(1)
    last = pl.num_programs(0) * pl.num_programs(1) - 1
    slot = step % 2

    @pl.when(step == 0)
    def _():
        for cache_ref, buf_ref, sem_ref in streams:
            for k in range(PAGES_PER_STEP):
                _page_copy(pt_ref, cache_ref, buf_ref, sem_ref, 0, 0, k).start()

    @pl.when(step < last)
    def _():
        for cache_ref, buf_ref, sem_ref in streams:
            for k in range(PAGES_PER_STEP):
                _page_copy(pt_ref, cache_ref, buf_ref, sem_ref, step + 1, 1 - slot, k).start()

    for cache_ref, buf_ref, sem_ref in streams:
        for k in range(PAGES_PER_STEP):
            _page_copy(pt_ref, cache_ref, buf_ref, sem_ref, step, slot, k).wait()
    return slot


XLU_POOL_PAGES = 16
MXU_POOL_GROUP = 8


def _compress_pages_kernel(pt_ref, cache_ref, pool_ref, pe_ref, o_ref, buf_ref, sem_ref):
    slot = _paged_pipeline(pt_ref, [(cache_ref, buf_ref, sem_ref)])
    pe_sum = jnp.sum(pe_ref[...], axis=0, keepdims=True)
    pool = pool_ref[...]
    groups = [jnp.concatenate([buf_ref[slot, k + j] for j in range(MXU_POOL_GROUP)], axis=1)
              for k in range(XLU_POOL_PAGES, PAGES_PER_STEP, MXU_POOL_GROUP)]
    splits = [_split_bf16(x) for x in groups]
    by_mxu = [_dot_nt(pool, hi) + _dot_nt(pool, lo) for hi, lo in splits]
    by_xlu = []
    for k in range(XLU_POOL_PAGES):
        rows = buf_ref[slot, k].T
        by_xlu.append(jnp.sum(rows.reshape(rows.shape[0] // CMP_BLOCK, CMP_BLOCK, KV_WIDTH), axis=1))
    o_ref[0] = (jnp.concatenate(by_xlu + by_mxu, axis=0) + pe_sum) * (1.0 / CMP_BLOCK)


def _compress_pages(cache_t, pt_flat, pe, b, n_pages):
    page = cache_t.shape[2]
    per_step = PAGES_PER_STEP * page // CMP_BLOCK
    pool = np.zeros((MXU_POOL_GROUP * page // CMP_BLOCK, MXU_POOL_GROUP * page), np.float32)
    for s in range(pool.shape[1]):
        pool[s // CMP_BLOCK, s] = 1.0
    gs = pltpu.PrefetchScalarGridSpec(
        num_scalar_prefetch=1, grid=(b, n_pages // PAGES_PER_STEP),
        in_specs=[pl.BlockSpec(memory_space=pl.ANY),
                  pl.BlockSpec(pool.shape, lambda bb, c, pt: (0, 0)),
                  pl.BlockSpec((CMP_BLOCK, KV_WIDTH), lambda bb, c, pt: (0, 0))],
        out_specs=pl.BlockSpec((1, per_step, KV_WIDTH), lambda bb, c, pt: (bb, c, 0)),
        scratch_shapes=[pltpu.VMEM((2, PAGES_PER_STEP, KV_WIDTH, page), F32), pltpu.SemaphoreType.DMA((2,))])
    return pl.pallas_call(
        _compress_pages_kernel, grid_spec=gs,
        out_shape=jax.ShapeDtypeStruct((b, n_pages * page // CMP_BLOCK, KV_WIDTH), F32),
        compiler_params=pltpu.CompilerParams(dimension_semantics=("arbitrary",) * 2, vmem_limit_bytes=VMEM_LIMIT),
        name="compress_pages",
    )(pt_flat, cache_t, jnp.asarray(pool, BF16), pe)


def _pad_rows(v, rows):
    return jnp.concatenate([v, jnp.zeros((rows - v.shape[0], v.shape[1]), v.dtype)], axis=0)


def _stack_heads(qv, lo):
    a, b = qv[:, :LANES], qv[:, LANES:]
    z = jnp.zeros_like(a)
    return jnp.concatenate([jnp.where(lo, a, z), jnp.where(lo, z, a),
                            jnp.where(lo, b, z), jnp.where(lo, z, b)], axis=0)


def _sample_select_kernel(n_cmp, q_ref, kc_ref, oc_ref, idx_ref, imp_ref):
    bb = pl.program_id(0)
    lo1 = _lane_lo(1)
    lo_c = _lane_lo(n_cmp)
    q = q_ref[0]
    kc = kc_ref[0]
    even = (lax.broadcasted_iota(jnp.int32, (1, LANES), 1) & 1) == 0
    for g in range(NSA_KV_HEADS):
        kk, vv = _dup_kv(kc[:, g * LANES:(g + 1) * LANES], lo_c)
        qs = _pad_rows(_stack_heads(q[:, g * 2 * LANES:(g + 1) * 2 * LANES], lo1), PAD_ROWS)
        s = _dot_nt(qs, kk.astype(BF16))
        e = jnp.exp(s - jnp.max(s, axis=-1, keepdims=True))
        p = e / jnp.sum(e, axis=-1, keepdims=True)
        oc_ref[0, g] = _dot(p.astype(BF16), vv.astype(BF16))
        imp = p[0:1] + p[1:2] + p[2:3] + p[3:4]
        chunks = []
        for k in range(n_cmp // LANES):
            a = imp[:, k * LANES:(k + 1) * LANES]
            chunks.append(a + jnp.where(even, pltpu.roll(a, LANES - 1, 1), pltpu.roll(a, 1, 1)))
        imp_ref[pl.ds(bb * NSA_KV_HEADS + g, 1), :] = jnp.concatenate(chunks, axis=1)

    @pl.when(bb == pl.num_programs(0) - 1)
    def _():
        rows = imp_ref.shape[0]
        blk = lax.broadcasted_iota(jnp.int32, (rows, n_cmp), 1) >> 1
        blk_f = blk.astype(F32)
        slot = lax.broadcasted_iota(jnp.int32, (rows, N_SELECT), 1)
        v = jnp.where(blk == 0, -1.0, imp_ref[...])
        idx = jnp.where(slot == N_SELECT - 1, n_cmp // 2, 0)
        for k in range(1, N_SELECT - 1):
            top = jnp.max(v, axis=-1, keepdims=True)
            jmin = jnp.min(jnp.where(v == top, blk_f, float(n_cmp)), axis=-1, keepdims=True).astype(jnp.int32)
            idx = jnp.where(slot == k, jmin, idx)
            v = jnp.where(blk == jmin, -1.0, v)
        idx_ref[...] = idx


def _sample_select(q, kc_all, b):
    n_cmp = kc_all.shape[1]
    rows = b * NSA_KV_HEADS
    return pl.pallas_call(
        functools.partial(_sample_select_kernel, n_cmp),
        grid=(b,),
        in_specs=[pl.BlockSpec((1, 1, NSA_WIDTH), lambda bb: (bb, 0, 0)),
                  pl.BlockSpec((1, n_cmp, KV_WIDTH), lambda bb: (bb, 0, 0))],
        out_specs=[pl.BlockSpec((1, NSA_KV_HEADS, PAD_ROWS, LANES), lambda bb: (bb, 0, 0, 0)),
                   pl.BlockSpec((rows, N_SELECT), lambda bb: (0, 0))],
        out_shape=[jax.ShapeDtypeStruct((b, NSA_KV_HEADS, PAD_ROWS, LANES), F32),
                   jax.ShapeDtypeStruct((rows, N_SELECT), jnp.int32)],
        scratch_shapes=[pltpu.VMEM((rows, n_cmp), F32)],
        compiler_params=pltpu.CompilerParams(dimension_semantics=("arbitrary",), vmem_limit_bytes=VMEM_LIMIT),
        name="sample_select",
    )(q, kc_all)


def _extra_key_softmax(s_past, vt4_b, s_new, v_new):
    m = jnp.maximum(jnp.max(s_past, axis=-1, keepdims=True), s_new)
    e = jnp.exp(s_past - m)
    e_new = jnp.exp(s_new - m)
    den = jnp.sum(e, axis=-1, keepdims=True) + e_new
    return (_dot_nt(e.astype(BF16), vt4_b) + e_new * v_new) / den


def _sample_attend_kernel(idx_ref, pt_ref, *refs):
    del pt_ref
    n_sel = N_SELECT
    pages = refs[:n_sel]
    qr_ref, newkv_ref, neww_ref, newwf_ref, win_ref, gate_ref, oc_ref, o_ref, wout_ref = refs[n_sel:]
    bb, g = pl.program_id(0), pl.program_id(1)
    lo1 = _lane_lo(1)
    q = qr_ref[0].astype(F32)
    halves = [q[:, 0:LANES], q[:, LANES:2 * LANES]]
    rows = []
    for r in range(NSA_GROUP):
        a = halves[r // 2]
        rows.append(jnp.where(lo1, a if r % 2 == 0 else pltpu.roll(a, HALF, 1), 0.0))
    qs_f = _pad_rows(jnp.concatenate(rows, axis=0), PAD_ROWS)
    qs = qs_f.astype(BF16)
    twice_rows = lambda a: jnp.concatenate([a, a], axis=0)

    s_t = jnp.concatenate([pg[0] for pg in pages], axis=1)
    s_sel = _dot(qs, s_t.astype(BF16))
    base = (bb * NSA_KV_HEADS + g) * n_sel
    biases = []
    for k in range(n_sel - 1):
        odd = (idx_ref[base + k] & 1) == 1
        biases.append(jnp.where(lo1, jnp.where(odd, NEG, 0.0), jnp.where(odd, 0.0, NEG)))
    biases.append(jnp.full((1, LANES), NEG, F32))
    s_sel = s_sel + jnp.concatenate(biases, axis=1)
    nk = newkv_ref[0].astype(F32)
    s_new = jnp.sum(qs_f * nk[:, :LANES], axis=-1, keepdims=True)
    o_s = _extra_key_softmax(s_sel, twice_rows(s_t[HD:2 * HD]).astype(BF16), s_new, nk[:, LANES:])

    w = win_ref[0]
    nw = neww_ref[0].astype(F32)
    s_w = _dot(qs, w.astype(BF16))
    s_wn = jnp.sum(qs_f * nw[:, :LANES], axis=-1, keepdims=True)
    o_w = _extra_key_softmax(s_w, twice_rows(w[HD:2 * HD]).astype(BF16), s_wn, nw[:, LANES:])

    o_c = oc_ref[0, 0]
    gates = gate_ref[0]
    gate = jnp.where(g == 0, gates[:, 0:GATE_ROWS], gates[:, GATE_ROWS:2 * GATE_ROWS])
    heads = []
    for r in range(NSA_GROUP):
        heads.append(gate[:, 3 * r:3 * r + 1] * o_c[r:r + 1] + gate[:, 3 * r + 1:3 * r + 2] * o_s[r:r + 1]
                     + gate[:, 3 * r + 2:3 * r + 3] * o_w[r:r + 1])
    o_ref[0] = jnp.concatenate([jnp.where(lo1, heads[0], heads[1]), jnp.where(lo1, heads[2], heads[3])], axis=1)

    n_feat, n_w = w.shape
    new_row = jnp.broadcast_to(newwf_ref[0], (n_feat, n_feat))
    diag = (lax.broadcasted_iota(jnp.int32, (n_feat, n_feat), 0)
            == lax.broadcasted_iota(jnp.int32, (n_feat, n_feat), 1))
    new_col = jnp.sum(jnp.where(diag, new_row, 0.0), axis=1, keepdims=True)
    last = lax.broadcasted_iota(jnp.int32, (n_feat, LANES), 1) == LANES - 1
    chunks = []
    n_ch = n_w // LANES
    for c in range(n_ch):
        cur = pltpu.roll(w[:, c * LANES:(c + 1) * LANES], LANES - 1, 1)
        if c + 1 < n_ch:
            nxt = pltpu.roll(w[:, (c + 1) * LANES:(c + 2) * LANES], LANES - 1, 1)
        else:
            nxt = jnp.broadcast_to(new_col, (n_feat, LANES))
        chunks.append(jnp.where(last, nxt, cur))
    wout_ref[0] = jnp.concatenate(chunks, axis=1)


def _sample_attend(p, oc, idx_flat, pt_flat, slc_t, win_t, b, n_pages):
    page = slc_t.shape[2]
    win_len = win_t.shape[2]
    per_page = page // SEL_BLOCK

    def page_spec(k):
        def imap(bb, g, idx, pt):
            j = idx[(bb * NSA_KV_HEADS + g) * N_SELECT + k]
            return (pt[bb * n_pages + jnp.minimum(j // per_page, n_pages - 1)], g, 0)
        return pl.BlockSpec((1, LANES, page), imap)

    row = lambda w: pl.BlockSpec((1, 1, w), lambda bb, g, idx, pt: (bb, 0, g))
    wspec = pl.BlockSpec((1, LANES, win_len), lambda bb, g, idx, pt: (bb, g, 0))
    gs = pltpu.PrefetchScalarGridSpec(
        num_scalar_prefetch=2, grid=(b, NSA_KV_HEADS),
        in_specs=[page_spec(k) for k in range(N_SELECT)] + [
            row(2 * LANES), row(2 * LANES), row(2 * LANES), row(LANES), wspec,
            pl.BlockSpec((1, 1, NSA_KV_HEADS * GATE_ROWS), lambda bb, g, idx, pt: (bb, 0, 0)),
            pl.BlockSpec((1, 1, PAD_ROWS, LANES), lambda bb, g, idx, pt: (bb, g, 0, 0))],
        out_specs=[row(2 * LANES), wspec])
    return pl.pallas_call(
        _sample_attend_kernel, grid_spec=gs,
        out_shape=[jax.ShapeDtypeStruct((b, 1, NSA_WIDTH), F32),
                   jax.ShapeDtypeStruct((b, KV_WIDTH, win_len), F32)],
        compiler_params=pltpu.CompilerParams(dimension_semantics=("arbitrary",) * 2, vmem_limit_bytes=VMEM_LIMIT),
        name="sample_attend",
    )(idx_flat, pt_flat, *([slc_t] * N_SELECT), p["qrot"], p["slckv"], p["winkv"], p["win"], win_t,
      p["gate"], oc)


def _softmax_update(s, v_b, m, l, acc):
    m_new = jnp.maximum(m, jnp.max(s, axis=-1, keepdims=True))
    p = jnp.exp(s - m_new)
    alpha = jnp.exp(m - m_new)
    return m_new, alpha * l + jnp.sum(p, axis=-1, keepdims=True), alpha * acc + _dot(p.astype(BF16), v_b)


def _mla_sample_kernel(pt_ref, lat_ref, krt_ref, qlat_ref, qpe_ref, cnew_ref, krnew_ref, o_ref,
                       m_ref, l_ref, acc_ref, lat_buf, kr_buf, lat_sem, kr_sem):
    n = PAGES_PER_STEP
    slot = _paged_pipeline(pt_ref, [(lat_ref, lat_buf, lat_sem), (krt_ref, kr_buf, kr_sem)])
    lat_pages = [lat_buf.at[slot, k] for k in range(n)]
    kr_pages = [kr_buf.at[slot, k] for k in range(n)]
    step = pl.program_id(1)
    qlat = _pad_rows(qlat_ref[0], PAD_ROWS)
    qpe = _pad_rows(qpe_ref[0], PAD_ROWS)

    @pl.when(step == 0)
    def _():
        c_new = cnew_ref[0]
        s_new = (jnp.sum(qlat * c_new, axis=-1, keepdims=True)
                 + jnp.sum(qpe * krnew_ref[0], axis=-1, keepdims=True))
        m_ref[...] = s_new
        l_ref[...] = jnp.ones(l_ref.shape, F32)
        acc_ref[...] = jnp.broadcast_to(c_new, acc_ref.shape)

    qlat_b, qpe_b = qlat.astype(BF16), qpe.astype(BF16)
    groups = range(0, n, POOL_PAGES)
    c_bs = [jnp.concatenate([pg[...] for pg in lat_pages[k:k + POOL_PAGES]], axis=0).astype(BF16) for k in groups]
    kr_bs = [jnp.concatenate([pg[...] for pg in kr_pages[k:k + POOL_PAGES]], axis=1).astype(BF16) for k in groups]
    s_lat = [_dot_nt(qlat_b, c_b) for c_b in c_bs]
    s_pe = [_dot(qpe_b, kr_b) for kr_b in kr_bs]
    scores = [a + r for a, r in zip(s_lat, s_pe)]
    maxes = [jnp.max(s, axis=-1, keepdims=True) for s in scores]
    ps = [jnp.exp(s - mx) for s, mx in zip(scores, maxes)]
    sums = [jnp.sum(p_g, axis=-1, keepdims=True) for p_g in ps]
    accs = [_dot(p_g.astype(BF16), c_b) for p_g, c_b in zip(ps, c_bs)]
    m_old = m_ref[...]
    m = m_old
    for mx in maxes:
        m = jnp.maximum(m, mx)
    alpha = jnp.exp(m_old - m)
    l = alpha * l_ref[...]
    acc = alpha * acc_ref[...]
    for mx, l_g, acc_g in zip(maxes, sums, accs):
        w = jnp.exp(mx - m)
        l = l + w * l_g
        acc = acc + w * acc_g
    m_ref[...] = m
    l_ref[...] = l
    acc_ref[...] = acc

    @pl.when(step == pl.num_programs(1) - 1)
    def _():
        o_ref[0] = (acc / l)[0:MLA_HEADS]


def _mla_sample(p, lat_cache, kr_t, pt_flat, b, n_pages):
    page = lat_cache.shape[1]
    head = lambda w: pl.BlockSpec((1, MLA_HEADS, w), lambda bb, c, pt: (bb, 0, 0))
    row = lambda w: pl.BlockSpec((1, 1, w), lambda bb, c, pt: (bb, 0, 0))
    gs = pltpu.PrefetchScalarGridSpec(
        num_scalar_prefetch=1, grid=(b, n_pages // PAGES_PER_STEP),
        in_specs=[pl.BlockSpec(memory_space=pl.ANY), pl.BlockSpec(memory_space=pl.ANY),
                  head(MLA_KV_LORA), head(MLA_ROPE), row(MLA_KV_LORA), row(MLA_ROPE)],
        out_specs=head(MLA_KV_LORA),
        scratch_shapes=[pltpu.VMEM((PAD_ROWS, 1), F32), pltpu.VMEM((PAD_ROWS, 1), F32),
                        pltpu.VMEM((PAD_ROWS, MLA_KV_LORA), F32),
                        pltpu.VMEM((2, PAGES_PER_STEP, page, MLA_KV_LORA), F32),
                        pltpu.VMEM((2, PAGES_PER_STEP, MLA_ROPE, page), F32),
                        pltpu.SemaphoreType.DMA((2,)), pltpu.SemaphoreType.DMA((2,))])
    qlat = p["qlat"].reshape(b, MLA_HEADS, MLA_KV_LORA)
    qpe = p["qpe"].reshape(b, MLA_HEADS, MLA_ROPE)
    return pl.pallas_call(
        _mla_sample_kernel, grid_spec=gs,
        out_shape=jax.ShapeDtypeStruct((b, MLA_HEADS, MLA_KV_LORA), F32),
        compiler_params=pltpu.CompilerParams(dimension_semantics=("arbitrary",) * 2, vmem_limit_bytes=VMEM_LIMIT),
        name="mla_sample",
    )(pt_flat, lat_cache, kr_t, qlat, qpe, p["c"], p["kr"])


def _rope_angles(pos, theta, dim):
    half = dim // 2
    inv = 1.0 / (float(theta) ** (np.arange(half, dtype=np.float64) / half))
    ang = np.asarray(pos, np.float64)[:, None] * inv[None, :]
    return np.cos(ang).astype(np.float32), np.sin(ang).astype(np.float32)


def _rope_lane_tables(pos, rows):
    def table(theta, dim, period, active):
        half = dim // 2
        cos, sin = _rope_angles(pos, theta, dim)
        lane = np.arange(LANES)
        d = lane % period
        is_lo = (d < half) & active(lane)
        is_hi = (d >= half) & (d < dim) & active(lane)
        fi = np.where(d < half, d, np.clip(d - half, 0, half - 1))
        cos_l, sin_l = cos[:, fi], sin[:, fi]
        tab = np.stack([np.where(is_lo | is_hi, cos_l, 1.0), np.where(is_lo, -sin_l, 0.0),
                        np.where(is_hi, sin_l, 0.0)]).astype(np.float32)
        return jnp.asarray(np.broadcast_to(tab, (3, rows, LANES)))

    every = lambda lane: np.ones_like(lane, bool)
    keys_only = lambda lane: (lane % LANES) < HD
    return (table(ROPE_THETA, ROT_DIM, HD, every), table(ROPE_THETA, ROT_DIM, HD, keys_only),
            table(MLA_ROPE_THETA, MLA_ROPE, MLA_ROPE, every))


def _rope_row_tables(pos):
    cq, sq = _rope_angles(pos, ROPE_THETA, ROT_DIM)
    cm, sm = _rope_angles(pos, MLA_ROPE_THETA, MLA_ROPE)
    return jnp.asarray(np.stack([cq.T, sq.T])), jnp.asarray(np.stack([cm.T, sm.T]))


def _pack_weights(l, norm_pre, w_in, pe_cmp, q_norm, w_q_up, kv_norm, w_kv_up, w_proj_a, w_proj_b, w_out, norm_post):
    w = w_in[l].astype(BF16)
    o = IN_OFFSETS
    seg = lambda k: w[:, o[k]:o[k + 1]]
    gn = seg(4)
    per_group = 3 * NSA_GROUP
    gn_t = jnp.zeros((NSA_KV_HEADS * GATE_ROWS, D_MODEL), w.dtype)
    for g in range(NSA_KV_HEADS):
        gn_t = gn_t.at[g * GATE_ROWS:g * GATE_ROWS + per_group].set(gn[:, g * per_group:(g + 1) * per_group].T)
    w_t = jnp.concatenate([seg(0).T, seg(1).T, seg(2).T, seg(3).T, gn_t, seg(8).T], axis=0)
    w_krp4 = jnp.tile(seg(8).T, (LANES // MLA_ROPE, 1))
    w_cols = jnp.concatenate([seg(5), seg(6), seg(7), seg(9), seg(10), seg(11)], axis=1)
    wq = w_q_up[l]
    wqup = jnp.concatenate([wq[..., :MLA_NOPE].reshape(MLA_Q_LORA, -1), wq[..., MLA_NOPE:].reshape(MLA_Q_LORA, -1)],
                           axis=1)
    wkv = w_kv_up[l]
    wuk_pad = jnp.pad(wkv[..., :MLA_NOPE], ((0, 0), (0, 0), (0, LANES - MLA_NOPE)))
    rope_copy = jnp.pad(jnp.eye(MLA_ROPE, dtype=w.dtype), ((0, LANES - MLA_ROPE), (MLA_NOPE, MLA_ROPE)))
    wkx = jnp.concatenate([wuk_pad.reshape(MLA_KV_LORA, MLA_HEADS * LANES), jnp.tile(rope_copy, (1, MLA_HEADS))],
                          axis=0)
    wvt = jnp.transpose(wkv[..., MLA_NOPE:], (1, 2, 0)).reshape(MLA_WIDTH, MLA_KV_LORA)
    w2uk =jnp.transpose(wkv[..., :MLA_NOPE], (1, 2, 0)).reshape(MLA_HEADS // 2, LANES, MLA_KV_LORA)
    wv = jnp.transpose(wkv[..., MLA_NOPE:], (1, 0, 2))
    zeros = jnp.zeros_like(wv)
    even = (jnp.arange(MLA_HEADS) % 2 == 0)[:, None, None]
    wuv = jnp.concatenate([jnp.where(even, wv, zeros), jnp.where(even, zeros, wv)], axis=2)
    return {
        "npre": norm_pre[l][None].astype(F32), "w_t": w_t, "w_krp4": w_krp4, "w_cols": w_cols, "pe": pe_cmp[l].reshape(CMP_BLOCK, KV_WIDTH).astype(F32),
        "qnorm": q_norm[l][None].astype(F32), "wqup": wqup.astype(BF16), "wqupt": wqup.T.astype(BF16),
        "kvnorm": kv_norm[l][None].astype(F32), "w2uk": w2uk.astype(BF16), "wkx": wkx.astype(BF16),
        "wvt": wvt.astype(BF16), "wuv": wuv.astype(BF16),
        "wpa": w_proj_a[l].astype(BF16), "wpb": w_proj_b[l].astype(BF16), "wout": w_out[l].astype(BF16),
        "npost": norm_post[l][None].astype(F32),
    }


def _rows_from_cols(a):
    b, _, t = a.shape
    return a.reshape(b, NSA_KV_HEADS, 2, HD, t).transpose(0, 4, 1, 2, 3)


def _cols_from_rows(a):
    n, t = a.shape[:2]
    return a.transpose(0, 2, 3, 4, 1).reshape(n, KV_WIDTH, t)


def _prompt_layer(x, wts):
    b, t, _ = x.shape
    assert t % PROMPT_TM == 0
    p = _in_project_cols(x, _rope_row_tables(np.arange(t)), wts)
    o_a = _nsa_prompt(p, b, t)
    o_b = _mla_prompt(p, wts, b, t)
    y = _merge(x, o_a, o_b, p, wts, 512, from_latent=False)
    win_keep = min(WINDOW, t)
    return y, (_rows_from_cols(p["cmp"]), _rows_from_cols(p["slc"]), p["c"], p["kr"].transpose(0, 2, 1),
               _rows_from_cols(p["win"][:, :, t - win_keep:]))


def _sample_layer(x, l, caches, state_win, page_table, wts):
    cache_cmp, cache_slc, cache_lat, cache_kr = caches
    b, s_new, _ = x.shape
    assert s_new == 1
    n_pages = page_table.shape[1]
    page = cache_cmp.shape[2]
    past = n_pages * page
    assert past % SEL_BLOCK == 0 and n_pages % PAGES_PER_STEP == 0 and page == LANES
    win_len = state_win.shape[1]
    assert win_len == WINDOW and past >= WINDOW
    pt_flat = page_table.reshape(-1).astype(jnp.int32)

    tabs = _rope_lane_tables(np.full((1,), past), b)
    p = _in_project_rows(x.reshape(1, b, D_MODEL), tabs, wts)
    p = {k: v.reshape(b, 1, v.shape[-1]) for k, v in p.items()}

    kc_all = _compress_pages(_cols_from_rows(cache_cmp[l]), pt_flat, wts["pe"], b, n_pages)
    oc, idx = _sample_select(p["q"], kc_all, b)
    o_a, new_win = _sample_attend(p, oc, idx.reshape(-1), pt_flat, _cols_from_rows(cache_slc[l]),
                                  _cols_from_rows(state_win), b, n_pages)
    o_lat = _mla_sample(p, cache_lat[l], cache_kr[l].transpose(0, 2, 1), pt_flat, b, n_pages)
    pm = {k: p[k].reshape(1, b, -1) for k in ("ga", "gb", "ma", "mb")}
    y = _merge(x.reshape(1, b, D_MODEL), o_a.reshape(1, b, NSA_WIDTH),
               o_lat.reshape(1, b, MLA_HEADS * MLA_KV_LORA), pm, wts, b, from_latent=True)
    kv6 = lambda a: a.reshape(b, 1, NSA_KV_HEADS, 2, HD)
    return y.reshape(b, 1, D_MODEL), (kv6(p["cmp"]), kv6(p["slc"]), p["c"], p["kr"], _rows_from_cols(new_win))


def kernel(x_prompt, x_sample, cache_nsa_cmp, cache_nsa_slc, cache_mla_latent, cache_mla_krope, state_nsa_win,
           page_table, norm_pre, w_in, pe_cmp, q_norm, w_q_up, kv_norm, w_kv_up, w_proj_a, w_proj_b, w_out,
           norm_post):
    depth = w_in.shape[0]
    hp, hs = x_prompt, x_sample
    new_p, new_s = [], []
    for l in range(depth):
        wts = _pack_weights(l, norm_pre, w_in, pe_cmp, q_norm, w_q_up, kv_norm, w_kv_up, w_proj_a, w_proj_b,
                            w_out, norm_post)
        hp, sp = _prompt_layer(hp, wts)
        hs, ss = _sample_layer(hs, l, (cache_nsa_cmp, cache_nsa_slc, cache_mla_latent, cache_mla_krope),
                               state_nsa_win[l], page_table, wts)
        new_p.append(sp)
        new_s.append(ss)
    stack = lambda items, k: jnp.stack([s[k] for s in items])
    return (hp, hs) + tuple(stack(new_p, k) for k in range(5)) + tuple(stack(new_s, k) for k in range(5))
```

```python
import functools

import numpy as np
import jax
import jax.numpy as jnp
from jax import lax
from jax.experimental import pallas as pl
from jax.experimental.pallas import tpu as pltpu

D_MODEL = 1024
NSA_HEADS = 8
NSA_KV_HEADS = 2
NSA_GROUP = NSA_HEADS // NSA_KV_HEADS
HD = 64
NSA_WIDTH = NSA_HEADS * HD
KV_WIDTH = NSA_KV_HEADS * 2 * HD
ROT_DIM = HD // 4
ROPE_THETA = 500000.0
CMP_BLOCK = 32
SEL_BLOCK = 64
N_SELECT = 16
WINDOW = 512
NSA_SCALE = HD ** -0.5

MLA_HEADS = 8
MLA_Q_LORA = 384
MLA_KV_LORA = 256
MLA_NOPE = 64
MLA_ROPE = 32
MLA_V = 64
MLA_WIDTH = MLA_HEADS * MLA_V
MLA_ROPE_THETA = 10000.0
MLA_SCALE = (MLA_NOPE + MLA_ROPE) ** -0.5
LOG2E = 1.4426950408889634
SHIFT_LIMIT = 40.0

RMS_EPS = 1e-6
NEG = -1e30
FORCE_SCORE = 1e4

IN_SPLITS = (NSA_WIDTH, KV_WIDTH, KV_WIDTH, KV_WIDTH, 3 * NSA_HEADS, NSA_WIDTH,
             MLA_Q_LORA, MLA_KV_LORA, MLA_ROPE, MLA_WIDTH, D_MODEL, D_MODEL)
IN_OFFSETS = tuple(int(v) for v in np.cumsum((0,) + IN_SPLITS))

LANES = 128
HALF = LANES // 2
GATE_ROWS = 16
PAD_ROWS = 16

PT_Q = (0, 512)
PT_CMP = (512, 768)
PT_SLC = (768, 1024)
PT_WIN = (1024, 1280)
PT_GN = (1280, 1280 + NSA_KV_HEADS * GATE_ROWS)
PT_KRP = (PT_GN[1], PT_GN[1] + MLA_ROPE)
PT_ROWS = PT_KRP[1]
PR_GA = (0, 512)
PR_QD = (512, 896)
PR_KVD = (896, 1152)
PR_GB = (1152, 1664)
PR_MA = (1664, 2688)
PR_MB = (2688, 3712)
PR_COLS = 3712

PROMPT_TM = 512
ATT_TQ = 256
ATT_TK = 256
MLA_TK = 256

VMEM_LIMIT = 48 * 1024 * 1024
BF16 = jnp.bfloat16
F32 = jnp.float32


def _full_spec(shape):
    nd = len(shape)
    return pl.BlockSpec(shape, lambda *_: (0,) * nd)


def _lane_lo(rows):
    return lax.broadcasted_iota(jnp.int32, (rows, LANES), 1) < HALF


def _dot(a, b):
    return jnp.dot(a, b, preferred_element_type=F32)


def _dot_nt(a, b):
    return lax.dot_general(a, b, (((1,), (1,)), ((), ())), preferred_element_type=F32)


def _rms(v, gain):
    return v * lax.rsqrt(jnp.mean(v * v, axis=-1, keepdims=True) + RMS_EPS) * gain


def _split_bf16(v):
    hi = v.astype(BF16)
    return hi, (v - hi.astype(F32)).astype(BF16)


def _rope_tiles(v, tab_ref, shift):
    c, s_lo, s_hi = tab_ref[0], tab_ref[1], tab_ref[2]
    out = []
    for k in range(v.shape[1] // LANES):
        a = v[:, k * LANES:(k + 1) * LANES]
        out.append(a * c + pltpu.roll(a, LANES - shift, 1) * s_lo + pltpu.roll(a, shift, 1) * s_hi)
    return out[0] if len(out) == 1 else jnp.concatenate(out, axis=1)


def _dup_kv(a, lo):
    r = pltpu.roll(a, HALF, 1)
    return jnp.where(lo, a, r), jnp.where(lo, r, a)


def _kv_pack(v):
    lo = _lane_lo(v.shape[0])
    parts = []
    for g in range(NSA_KV_HEADS):
        kk, vv = _dup_kv(v[:, g * LANES:(g + 1) * LANES], lo)
        parts += [kk, vv]
    return jnp.concatenate(parts, axis=1).astype(BF16)


def _inproj_rows_kernel(x_ref, npre_ref, wt_ref, wkrp_ref, w_ref, tq_ref, tkv_ref, tm_ref, qnorm_ref, wqup_ref,
                        kvnorm_ref, w2uk_ref, q_ref, qrot_ref, gate_ref, cmp_ref, slc_ref, win_ref, slckv_ref,
                        winkv_ref, ga_ref, gb_ref, ma_ref, mb_ref, qpe_ref, c_ref, kr_ref, qlat_ref):
    xb = _rms(x_ref[0], npre_ref[...]).astype(BF16)
    segt = lambda lohi: _dot_nt(xb, wt_ref[lohi[0]:lohi[1], :])
    seg = lambda lohi: _dot(xb, w_ref[:, lohi[0]:lohi[1]])

    a = segt((PT_Q[0], PT_WIN[1]))
    q = a[:, :NSA_WIDTH]
    q_ref[0] = (q * NSA_SCALE).astype(BF16)
    qrot_ref[0] = (_rope_tiles(q, tq_ref, ROT_DIM // 2) * NSA_SCALE).astype(BF16)
    cmp_ref[0] = a[:, NSA_WIDTH:NSA_WIDTH + KV_WIDTH]
    kvs = _rope_tiles(a[:, NSA_WIDTH + KV_WIDTH:NSA_WIDTH + 2 * KV_WIDTH], tkv_ref, ROT_DIM // 2)
    slc_ref[0] = kvs
    slckv_ref[0] = _kv_pack(kvs)
    kvw = _rope_tiles(a[:, NSA_WIDTH + 2 * KV_WIDTH:], tkv_ref, ROT_DIM // 2)
    win_ref[0] = kvw
    winkv_ref[0] = _kv_pack(kvw)

    gate_ref[0] = jax.nn.sigmoid(segt(PT_GN))
    ga_ref[0] = seg(PR_GA)
    gb_ref[0] = seg(PR_GB)
    ma_ref[0] = seg(PR_MA)
    mb_ref[0] = seg(PR_MB)

    qd = _rms(seg(PR_QD), qnorm_ref[...]).astype(BF16)
    qh = _dot(qd, wqup_ref[...])
    qn = qh[:, :MLA_HEADS * MLA_NOPE].astype(BF16)
    qpe_ref[0] = _rope_tiles(qh[:, MLA_HEADS * MLA_NOPE:], tm_ref, MLA_ROPE // 2) * MLA_SCALE
    c_ref[0] = _rms(seg(PR_KVD), kvnorm_ref[...])
    kr_ref[0] = _rope_tiles(_dot_nt(xb, wkrp_ref[...]), tm_ref, MLA_ROPE // 2)[:, :MLA_ROPE]

    rows = qn.shape[0]
    lo = _lane_lo(rows)
    z = jnp.zeros((rows, LANES), BF16)
    parts = []
    for j in range(MLA_HEADS // 2):
        pair = qn[:, j * LANES:(j + 1) * LANES]
        parts.append(_dot(jnp.where(lo, pair, z), w2uk_ref[j]) * MLA_SCALE)
        parts.append(_dot(jnp.where(lo, z, pair), w2uk_ref[j]) * MLA_SCALE)
    qlat_ref[0] = jnp.concatenate(parts, axis=1)


def _in_project_rows(x3, tabs, wts):
    b, t, _ = x3.shape
    tq, tkv, tmla = tabs
    row = lambda w, dt: jax.ShapeDtypeStruct((b, t, w), dt)
    names = ["q", "qrot", "gate", "cmp", "slc", "win", "slckv", "winkv", "ga", "gb", "ma", "mb", "qpe", "c", "kr",
             "qlat"]
    out_shape = [row(512, BF16), row(512, BF16), row(NSA_KV_HEADS * GATE_ROWS, F32), row(256, F32), row(256, F32),
                 row(256, F32),
                 row(512, BF16), row(512, BF16), row(512, F32), row(512, F32), row(1024, F32), row(1024, F32),
                 row(MLA_HEADS * MLA_ROPE, F32), row(MLA_KV_LORA, F32), row(MLA_ROPE, F32),
                 row(MLA_HEADS * MLA_KV_LORA, F32)]
    tok = lambda w: pl.BlockSpec((1, t, w), lambda bb: (bb, 0, 0))
    tab = _full_spec((3, t, LANES))
    in_specs = [tok(D_MODEL), _full_spec((1, D_MODEL)), _full_spec((PT_ROWS, D_MODEL)), _full_spec((LANES, D_MODEL)),
                _full_spec((D_MODEL, PR_COLS)), tab, tab, tab, _full_spec((1, MLA_Q_LORA)), _full_spec((MLA_Q_LORA, MLA_HEADS * (MLA_NOPE + MLA_ROPE))),
                _full_spec((1, MLA_KV_LORA)), _full_spec((MLA_HEADS // 2, LANES, MLA_KV_LORA))]
    res = pl.pallas_call(
        _inproj_rows_kernel, grid=(b,), in_specs=in_specs, out_specs=[tok(s.shape[2]) for s in out_shape],
        out_shape=out_shape,
        compiler_params=pltpu.CompilerParams(dimension_semantics=("arbitrary",), vmem_limit_bytes=VMEM_LIMIT),
        name="in_project_rows",
    )(x3, wts["npre"], wts["w_t"], wts["w_krp4"], wts["w_cols"], tq, tkv, tmla, wts["qnorm"], wts["wqup"],
      wts["kvnorm"], wts["w2uk"])
    return dict(zip(names, res))


def _rope_rows(x, cos, sin, half):
    x1, x2 = x[0:half], x[half:2 * half]
    parts = [x1 * cos - x2 * sin, x1 * sin + x2 * cos]
    if x.shape[0] > 2 * half:
        parts.append(x[2 * half:])
    return jnp.concatenate(parts, axis=0)


def _store_chunks(ref, v):
    tk = ref.shape[3]
    for j in range(ref.shape[1]):
        ref[0, j] = v[:, j * tk:(j + 1) * tk]


def _inproj_cols_kernel(x_ref, npre_ref, wt_ref, w_ref, ropeq_ref, ropem_ref, pe_ref, qnorm_ref, wqupt_ref,
                        kvnorm_ref, pool_ref, wkx_ref, wvt_ref,
                        qt_ref, qrt_ref, gate_ref, cmp_ref, slc_ref, win_ref, slcb_ref, winb_ref, slcr_ref,
                        winr_ref, kc_ref, ga_ref, gb_ref, ma_ref, mb_ref, qmt_ref, c_ref, kmla_ref, vmt_ref,
                        kr_ref, kn_ref):
    tm = x_ref.shape[1]
    xb = _rms(x_ref[0], npre_ref[...]).astype(BF16)
    segt = lambda lohi: _dot_nt(wt_ref[lohi[0]:lohi[1], :], xb)
    seg = lambda lohi: _dot(xb, w_ref[:, lohi[0]:lohi[1]])
    cq, sq = ropeq_ref[0], ropeq_ref[1]
    cm, sm = ropem_ref[0], ropem_ref[1]
    hq, hm = ROT_DIM // 2, MLA_ROPE // 2

    qt = segt(PT_Q)
    qt_ref[0] = (qt * (NSA_SCALE * LOG2E)).astype(BF16)
    qrt = jnp.concatenate([_rope_rows(qt[h * HD:(h + 1) * HD], cq, sq, hq) for h in range(NSA_HEADS)], axis=0)
    qrt_ref[0] = (qrt * (NSA_SCALE * LOG2E)).astype(BF16)

    cmpt = segt(PT_CMP)
    cmp_ref[0] = cmpt
    hi, lo = _split_bf16(cmpt)
    pool = pool_ref[...]
    pooled = (_dot_nt(pool, hi) + _dot_nt(pool, lo))[0:tm // CMP_BLOCK]
    kc_ref[0] = (pooled + jnp.sum(pe_ref[...], axis=0, keepdims=True)) * (1.0 / CMP_BLOCK)

    def rope_kv(v):
        parts = []
        for g in range(NSA_KV_HEADS):
            parts.append(_rope_rows(v[g * LANES:g * LANES + HD], cq, sq, hq))
            parts.append(v[g * LANES + HD:(g + 1) * LANES])
        return jnp.concatenate(parts, axis=0)

    slct = rope_kv(segt(PT_SLC))
    slc_ref[0] = slct
    _store_chunks(slcb_ref, slct.astype(BF16))
    slcr_ref[0] = slct.T.astype(BF16)
    wint = rope_kv(segt(PT_WIN))
    win_ref[0] = wint
    _store_chunks(winb_ref, wint.astype(BF16))
    winr_ref[0] = wint.T.astype(BF16)

    gate_ref[0] = jax.nn.sigmoid(segt(PT_GN))
    krt = _rope_rows(segt(PT_KRP), cm, sm, hm)
    kr_ref[0] = krt
    c = _rms(seg(PR_KVD), kvnorm_ref[...])
    c_ref[0] = c
    c_b = c.astype(BF16)
    kr_rows = jnp.concatenate([krt, jnp.zeros((LANES - MLA_ROPE, tm), F32)], axis=0).T
    ckr = jnp.concatenate([c_b, kr_rows.astype(BF16)], axis=1)
    kmla = _dot(ckr, wkx_ref[...]).astype(BF16)
    kmla_ref[0] = kmla
    kf = kmla.astype(F32)
    norms = []
    for h in range(MLA_HEADS):
        n2 = jnp.sum(kf[:, h * LANES:(h + 1) * LANES] ** 2, axis=1, keepdims=True)
        norms.append(jnp.broadcast_to(jnp.sqrt(jnp.max(n2, axis=0, keepdims=True)), (1, LANES)))
    kn_ref[0] = jnp.concatenate(norms, axis=0)
    _store_chunks(vmt_ref, _dot_nt(wvt_ref[...], c_b).astype(BF16))

    qd = _rms(seg(PR_QD), qnorm_ref[...]).astype(BF16)
    qht = _dot_nt(wqupt_ref[...], qd)
    n_nope = MLA_HEADS * MLA_NOPE
    zq = jnp.zeros((LANES - MLA_NOPE - MLA_ROPE, tm), F32)
    parts = []
    for h in range(MLA_HEADS):
        parts += [qht[h * MLA_NOPE:(h + 1) * MLA_NOPE],
                  _rope_rows(qht[n_nope + h * MLA_ROPE:n_nope + (h + 1) * MLA_ROPE], cm, sm, hm), zq]
    qmt_ref[0] = (jnp.concatenate(parts, axis=0) * (MLA_SCALE * LOG2E)).astype(BF16)

    ga_ref[0] = seg(PR_GA)
    gb_ref[0] = seg(PR_GB)
    ma_ref[0] = seg(PR_MA)
    mb_ref[0] = seg(PR_MB)


def _in_project_cols(x3, ropes, wts):
    b, t, _ = x3.shape
    tm = PROMPT_TM
    nt = t // tm
    ropeq, ropem = ropes
    pool = np.zeros((16, tm), np.float32)
    for s in range(tm):
        pool[s // CMP_BLOCK, s] = 1.0
    sds = jax.ShapeDtypeStruct
    rows = lambda w: pl.BlockSpec((1, tm, w), lambda i, bb: (bb, i, 0))
    cols = lambda w: pl.BlockSpec((1, w, tm), lambda i, bb: (bb, 0, i))
    chunk = lambda w, tk: pl.BlockSpec((1, tm // tk, w, tk), lambda i, bb: (bb, i, 0, 0))
    outs = [
        ("qt", sds((b, NSA_WIDTH, t), BF16), cols(NSA_WIDTH)),
        ("qrt", sds((b, NSA_WIDTH, t), BF16), cols(NSA_WIDTH)),
        ("gate", sds((b, NSA_KV_HEADS * GATE_ROWS, t), F32), cols(NSA_KV_HEADS * GATE_ROWS)),
        ("cmp", sds((b, KV_WIDTH, t), F32), cols(KV_WIDTH)),
        ("slc", sds((b, KV_WIDTH, t), F32), cols(KV_WIDTH)),
        ("win", sds((b, KV_WIDTH, t), F32), cols(KV_WIDTH)),
        ("slcb", sds((b, t // ATT_TK, KV_WIDTH, ATT_TK), BF16), chunk(KV_WIDTH, ATT_TK)),
        ("winb", sds((b, t // ATT_TK, KV_WIDTH, ATT_TK), BF16), chunk(KV_WIDTH, ATT_TK)),
        ("slcr", sds((b, t, KV_WIDTH), BF16), rows(KV_WIDTH)),
        ("winr", sds((b, t, KV_WIDTH), BF16), rows(KV_WIDTH)),
        ("kc", sds((b, t // CMP_BLOCK, KV_WIDTH), F32),
         pl.BlockSpec((1, tm // CMP_BLOCK, KV_WIDTH), lambda i, bb: (bb, i, 0))),
        ("ga", sds((b, t, NSA_WIDTH), F32), rows(NSA_WIDTH)),
        ("gb", sds((b, t, MLA_WIDTH), F32), rows(MLA_WIDTH)),
        ("ma", sds((b, t, D_MODEL), F32), rows(D_MODEL)),
        ("mb", sds((b, t, D_MODEL), F32), rows(D_MODEL)),
        ("qmt", sds((b, MLA_HEADS * LANES, t), BF16), cols(MLA_HEADS * LANES)),
        ("c", sds((b, t, MLA_KV_LORA), F32), rows(MLA_KV_LORA)),
        ("kmla", sds((b, t, MLA_HEADS * LANES), BF16), rows(MLA_HEADS * LANES)),
        ("vmt", sds((b, t // MLA_TK, MLA_WIDTH, MLA_TK), BF16), chunk(MLA_WIDTH, MLA_TK)),
        ("kr", sds((b, MLA_ROPE, t), F32), cols(MLA_ROPE)),
        ("kn", sds((b, nt * MLA_HEADS, LANES), F32), pl.BlockSpec((1, MLA_HEADS, LANES), lambda i, bb: (bb, i, 0))),
    ]
    rope_spec = lambda half: pl.BlockSpec((2, half, tm), lambda i, bb: (0, 0, i))
    in_specs = [rows(D_MODEL), _full_spec((1, D_MODEL)), _full_spec((PT_ROWS, D_MODEL)),
                _full_spec((D_MODEL, PR_COLS)), rope_spec(ROT_DIM // 2), rope_spec(MLA_ROPE // 2),
                _full_spec((CMP_BLOCK, KV_WIDTH)), _full_spec((1, MLA_Q_LORA)),
                _full_spec((MLA_HEADS * (MLA_NOPE + MLA_ROPE), MLA_Q_LORA)), _full_spec((1, MLA_KV_LORA)),
                _full_spec((16, tm)), _full_spec((MLA_KV_LORA + LANES, MLA_HEADS * LANES)),
                _full_spec((MLA_WIDTH, MLA_KV_LORA))]
    res = pl.pallas_call(
        _inproj_cols_kernel, grid=(nt, b), in_specs=in_specs, out_specs=[o[2] for o in outs],
        out_shape=[o[1] for o in outs],
        compiler_params=pltpu.CompilerParams(dimension_semantics=("arbitrary", "arbitrary"),
                                             vmem_limit_bytes=VMEM_LIMIT),
        name="in_project_cols",
    )(x3, wts["npre"], wts["w_t"], wts["w_cols"], ropeq, ropem, wts["pe"], wts["qnorm"], wts["wqupt"],
      wts["kvnorm"], jnp.asarray(pool, BF16), wts["wkx"], wts["wvt"])
    return dict(zip([o[0] for o in outs], res))


def _online_update(s, vt, m, l, acc):
    d, keys = vt.shape
    m_new = jnp.maximum(m, jnp.max(s, axis=0, keepdims=True))
    p = jnp.exp2(s - m_new).astype(BF16)
    alpha = jnp.exp2(m - m_new)
    pv = _dot(jnp.concatenate([vt, jnp.ones((PAD_ROWS, keys), BF16)], axis=0), p)
    return m_new, alpha * l + pv[d:d + 1], alpha * acc + pv[0:d]


def _nsa_prompt_kernel(n_blk, qt_ref, qrt_ref, gate_ref, kc_ref, ks_ref, kst_ref, kw_ref, kwt_ref, o_ref, sb_ref):
    i = pl.program_id(2)
    tq, tk = ATT_TQ, ATT_TK
    nl = NSA_GROUP * tq
    zq = jnp.zeros((HD, tq), BF16)

    def widen(qt):
        return jnp.concatenate([jnp.concatenate([qt[r * HD:(r + 1) * HD], zq], axis=0)
                                for r in range(NSA_GROUP)], axis=1)

    qc = widen(qt_ref[0])
    qr = widen(qrt_ref[0])

    n_cmp = 2 * n_blk
    kc = kc_ref[0]
    s_c = _dot(kc.astype(BF16), qc)
    rho = lax.broadcasted_iota(jnp.int32, (n_cmp, nl), 0)
    cmp_idx = jnp.where(rho < n_blk, 2 * rho, 2 * (rho - n_blk) + 1)
    t_l = i * tq + (lax.broadcasted_iota(jnp.int32, (n_cmp, nl), 1) & (tq - 1))
    mask_c = cmp_idx * CMP_BLOCK + (CMP_BLOCK - 1) <= t_l
    s_c = jnp.where(mask_c, s_c, NEG)
    e_c = jnp.where(mask_c, jnp.exp2(s_c - jnp.max(s_c, axis=0, keepdims=True)), 0.0)
    p_c = e_c / jnp.maximum(jnp.sum(e_c, axis=0, keepdims=True), 1e-30)
    kct = jnp.concatenate([kc, jnp.zeros((LANES - n_cmp, LANES), F32)], axis=0).T
    p_pad = jnp.concatenate([p_c, jnp.zeros((LANES - n_cmp, nl), F32)], axis=0)
    o_c = _dot(kct[HD:2 * HD].astype(BF16), p_pad.astype(BF16))

    imp = p_c[:, 0:tq]
    for r in range(1, NSA_GROUP):
        imp = imp + p_c[:, r * tq:(r + 1) * tq]
    imp_blk = imp[0:n_blk] + imp[n_blk:n_cmp]
    blk = lax.broadcasted_iota(jnp.int32, (n_blk, tq), 0)
    t_q = i * tq + lax.broadcasted_iota(jnp.int32, (n_blk, tq), 1)
    ahead_of = t_q - blk * SEL_BLOCK
    score = jnp.where(blk == 0, FORCE_SCORE,
                      jnp.where(ahead_of < 0, -FORCE_SCORE, jnp.where(ahead_of < SEL_BLOCK, FORCE_SCORE, imp_blk)))
    rank = jnp.zeros((n_blk, tq), F32)
    for j in range(n_blk):
        other = score[j:j + 1, :]
        tie = jnp.where(blk > j, 1.0, 0.0)
        rank = rank + jnp.where(other > score, 1.0, jnp.where(other == score, tie, 0.0))
    sb_ref[...] = jnp.where(rank < N_SELECT, jnp.where(score > -1.0, 0.0, NEG), NEG)

    key_r = lax.broadcasted_iota(jnp.int32, (tk, tq), 0)
    t_k = i * tq + lax.broadcasted_iota(jnp.int32, (tk, tq), 1)
    n_chunks = ((i + 1) * tq + tk - 1) // tk
    c_lo = jnp.maximum(i * tq - WINDOW, 0) // tk
    per_chunk = tk // SEL_BLOCK
    tile = lambda bias: jnp.concatenate([bias] * NSA_GROUP, axis=1)
    ones = jnp.ones((PAD_ROWS, tk), BF16)

    def sel_bias(c):
        rows = [jnp.broadcast_to(sb_ref[pl.ds(c * per_chunk + j, 1), :], (SEL_BLOCK, tq)) for j in range(per_chunk)]
        return jnp.concatenate(rows, axis=0)

    def values(ref, c):
        return jnp.concatenate([ref[0, c, HD:2 * HD, :], ones], axis=0)

    qf = qrt_ref[0].astype(F32)
    q_norm2 = [jnp.sum(qf[r * HD:(r + 1) * HD] ** 2, axis=0, keepdims=True) for r in range(NSA_GROUP)]
    q_norm = jnp.sqrt(jnp.maximum(jnp.maximum(q_norm2[0], q_norm2[1]), jnp.maximum(q_norm2[2], q_norm2[3])))

    def key_norm(ref):
        kf = ref[0, :, 0:HD, :].astype(F32)
        return jnp.sqrt(jnp.max(jnp.sum(kf * kf, axis=1)))

    bound_s = q_norm * (key_norm(kst_ref) * 1.001) + 1e-3
    bound_w = q_norm * (key_norm(kwt_ref) * 1.001) + 1e-3
    safe = jnp.maximum(jnp.max(bound_s), jnp.max(bound_w)) <= SHIFT_LIMIT

    def attend_shifted():
        def past_body(c, acc):
            base = pl.multiple_of(c * tk, tk)
            s = _dot(ks_ref[0, pl.ds(base, tk), :], qr) + tile(sel_bias(c) - bound_s)
            return acc + _dot(values(kst_ref, c), jnp.exp2(s).astype(BF16))

        def near_body(c, carry):
            base = pl.multiple_of(c * tk, tk)
            s = _dot(ks_ref[0, pl.ds(base, tk), :], qr)
            w = _dot(kw_ref[0, pl.ds(base, tk), :], qr)
            dist = t_k - (base + key_r)
            causal = jnp.where(dist >= 0, 0.0, NEG)
            p_s = jnp.exp2(s + tile(sel_bias(c) + causal - bound_s)).astype(BF16)
            p_w = jnp.exp2(w + tile(jnp.where(dist <= WINDOW, causal, NEG) - bound_w)).astype(BF16)
            return carry[0] + _dot(values(kst_ref, c), p_s), carry[1] + _dot(values(kwt_ref, c), p_w)

        zero = jnp.zeros((HD + PAD_ROWS, nl), F32)
        far = lax.fori_loop(0, c_lo, past_body, zero)
        a_s, a_w = lax.fori_loop(c_lo, n_chunks, near_body, (far, zero))
        return a_s[0:HD] / a_s[HD:HD + 1], a_w[0:HD] / a_w[HD:HD + 1]

    def attend_online():
        init = (jnp.full((1, nl), NEG, F32), jnp.zeros((1, nl), F32), jnp.zeros((HD, nl), F32))

        def past_body(c, carry):
            base = pl.multiple_of(c * tk, tk)
            s = _dot(ks_ref[0, pl.ds(base, tk), :], qr) + tile(sel_bias(c))
            return _online_update(s, kst_ref[0, c, HD:2 * HD, :], *carry)

        def near_body(c, carry):
            base = pl.multiple_of(c * tk, tk)
            s = _dot(ks_ref[0, pl.ds(base, tk), :], qr)
            w = _dot(kw_ref[0, pl.ds(base, tk), :], qr)
            dist = t_k - (base + key_r)
            causal = jnp.where(dist >= 0, 0.0, NEG)
            s = s + tile(sel_bias(c) + causal)
            w = w + tile(jnp.where(dist <= WINDOW, causal, NEG))
            return (_online_update(s, kst_ref[0, c, HD:2 * HD, :], *carry[:3])
                    + _online_update(w, kwt_ref[0, c, HD:2 * HD, :], *carry[3:]))

        far = lax.fori_loop(0, c_lo, past_body, init)
        _, l_s, a_s, _, l_w, a_w = lax.fori_loop(c_lo, n_chunks, near_body, far + init)
        return a_s / l_s, a_w / l_w

    o_s, o_w = lax.cond(safe, attend_shifted, attend_online)

    gate = gate_ref[0]
    heads = []
    for r in range(NSA_GROUP):
        sl = slice(r * tq, (r + 1) * tq)
        heads.append(gate[3 * r:3 * r + 1] * o_c[:, sl] + gate[3 * r + 1:3 * r + 2] * o_s[:, sl]
                     + gate[3 * r + 2:3 * r + 3] * o_w[:, sl])
    o_ref[0] = jnp.concatenate(heads, axis=0).T


def _nsa_prompt(p, b, t):
    tq, tk = ATT_TQ, ATT_TK
    n_blk = t // SEL_BLOCK
    assert 2 * n_blk <= LANES and t % tk == 0
    kc = p["kc"].reshape(b, n_blk, 2, KV_WIDTH).transpose(0, 2, 1, 3).reshape(b, 2 * n_blk, KV_WIDTH)
    qspec = pl.BlockSpec((1, NSA_GROUP * HD, tq), lambda bb, g, i: (bb, g, i))
    rm = pl.BlockSpec((1, t, LANES), lambda bb, g, i: (bb, 0, g))
    fm = pl.BlockSpec((1, t // tk, LANES, tk), lambda bb, g, i: (bb, 0, g, 0))
    return pl.pallas_call(
        functools.partial(_nsa_prompt_kernel, n_blk),
        grid=(b, NSA_KV_HEADS, t // tq),
        in_specs=[qspec, qspec, pl.BlockSpec((1, GATE_ROWS, tq), lambda bb, g, i: (bb, g, i)),
                  pl.BlockSpec((1, 2 * n_blk, LANES), lambda bb, g, i: (bb, 0, g)), rm, fm, rm, fm],
        out_specs=pl.BlockSpec((1, tq, NSA_GROUP * HD), lambda bb, g, i: (bb, i, g)),
        out_shape=jax.ShapeDtypeStruct((b, t, NSA_WIDTH), F32),
        scratch_shapes=[pltpu.VMEM((n_blk, tq), F32)],
        compiler_params=pltpu.CompilerParams(dimension_semantics=("arbitrary",) * 3, vmem_limit_bytes=VMEM_LIMIT),
        name="nsa_prompt",
    )(p["qt"], p["qrt"], p["gate"], kc, p["slcr"], p["slcb"], p["winr"], p["winb"])


MLA_TQ = 256


def _mla_prompt_kernel(qmt_ref, k_ref, vt_ref, kn_ref, o_ref, m_ref, l_ref, acc_ref):
    i = pl.program_id(1)
    tq, tk = MLA_TQ, MLA_TK
    heads = range(MLA_HEADS)
    m_ref[...] = jnp.full(m_ref.shape, NEG, F32)
    l_ref[...] = jnp.zeros(l_ref.shape, F32)
    acc_ref[...] = jnp.zeros(acc_ref.shape, F32)
    key_r = lax.broadcasted_iota(jnp.int32, (tk, tq), 0)
    t_k = i * tq + lax.broadcasted_iota(jnp.int32, (tk, tq), 1)
    n_chunks = ((i + 1) * tq + tk - 1) // tk
    n_past = (i * tq + 1) // tk
    ones = jnp.ones((PAD_ROWS, tk), BF16)

    def keys(c_base, h):
        return k_ref[0, pl.ds(c_base, tk), h * LANES:(h + 1) * LANES]

    kn = kn_ref[0]
    k_max = kn[0:MLA_HEADS]
    for j in range(1, kn.shape[0] // MLA_HEADS):
        k_max = jnp.maximum(k_max, kn[j * MLA_HEADS:(j + 1) * MLA_HEADS])
    bounds = []
    for h in heads:
        qf = qmt_ref[0, h * LANES:(h + 1) * LANES, :].astype(F32)
        q_norm = jnp.sqrt(jnp.sum(qf * qf, axis=0, keepdims=True))
        k_row = jnp.concatenate([k_max[h:h + 1]] * (tq // LANES), axis=1)
        bounds.append(q_norm * (k_row * 1.001) + 1e-3)
    worst = bounds[0]
    for bd in bounds[1:]:
        worst = jnp.maximum(worst, bd)
    safe = jnp.max(worst) <= SHIFT_LIMIT

    def shifted_step(c, masked):
        base = pl.multiple_of(c * tk, tk)
        old = [acc_ref[h] for h in heads]
        ps = []
        for h in heads:
            s = _dot(keys(base, h), qmt_ref[0, h * LANES:(h + 1) * LANES, :]) - bounds[h]
            ps.append(jnp.exp2(jnp.where(base + key_r <= t_k, s, NEG) if masked else s).astype(BF16))
        new = [old[h] + _dot(jnp.concatenate([vt_ref[0, c, h * MLA_V:(h + 1) * MLA_V, :], ones], axis=0), ps[h])
               for h in heads]
        for h in heads:
            acc_ref[h] = new[h]
        return 0

    def online_step(c, masked):
        base = pl.multiple_of(c * tk, tk)
        old = [(m_ref[h], l_ref[h], acc_ref[h, 0:MLA_V]) for h in heads]
        scores = []
        for h in heads:
            s = _dot(keys(base, h), qmt_ref[0, h * LANES:(h + 1) * LANES, :])
            scores.append(jnp.where(base + key_r <= t_k, s, NEG) if masked else s)
        new = [_online_update(scores[h], vt_ref[0, c, h * MLA_V:(h + 1) * MLA_V, :], *old[h]) for h in heads]
        for h in heads:
            m_ref[h], l_ref[h], acc_ref[h, 0:MLA_V] = new[h]
        return 0

    def run(step):
        lax.fori_loop(0, n_past, lambda c, z: step(c, False), 0)
        lax.fori_loop(n_past, n_chunks, lambda c, z: step(c, True), 0)

    def shifted():
        run(shifted_step)
        return tuple(acc_ref[h, 0:MLA_V] / acc_ref[h, MLA_V:MLA_V + 1] for h in heads)

    def online():
        run(online_step)
        return tuple(acc_ref[h, 0:MLA_V] / l_ref[h] for h in heads)

    outs = lax.cond(safe, shifted, online)
    o_ref[0] = jnp.concatenate(outs, axis=0).T


def _mla_prompt(p, wts, b, t):
    tq, tk = MLA_TQ, MLA_TK
    assert t % tq == 0
    return pl.pallas_call(
        _mla_prompt_kernel,
        grid=(b, t // tq),
        in_specs=[pl.BlockSpec((1, MLA_HEADS * LANES, tq), lambda bb, i: (bb, 0, i)),
                  pl.BlockSpec((1, t, MLA_HEADS * LANES), lambda bb, i: (bb, 0, 0)),
                  pl.BlockSpec((1, t // tk, MLA_WIDTH, tk), lambda bb, i: (bb, 0, 0, 0)),
                  pl.BlockSpec((1,) + p["kn"].shape[1:], lambda bb, i: (bb, 0, 0))],
        out_specs=pl.BlockSpec((1, tq, MLA_WIDTH), lambda bb, i: (bb, i, 0)),
        out_shape=jax.ShapeDtypeStruct((b, t, MLA_WIDTH), F32),
        scratch_shapes=[pltpu.VMEM((MLA_HEADS, 1, tq), F32), pltpu.VMEM((MLA_HEADS, 1, tq), F32),
                        pltpu.VMEM((MLA_HEADS, MLA_V + PAD_ROWS, tq), F32)],
        compiler_params=pltpu.CompilerParams(dimension_semantics=("arbitrary",) * 2, vmem_limit_bytes=VMEM_LIMIT),
        name="mla_prompt",
    )(p["qmt"], p["kmla"], p["vmt"], p["kn"])


def _merge_kernel(from_latent, x_ref, oa_ref, ob_ref, ga_ref, gb_ref, ma_ref, mb_ref, wpa_ref, wpb_ref, wout_ref,
                  npost_ref, wuv_ref, y_ref):
    if from_latent:
        lat = ob_ref[0].astype(BF16)
        parts = []
        for j in range(MLA_HEADS // 2):
            parts.append(_dot(lat[:, 2 * j * MLA_KV_LORA:(2 * j + 1) * MLA_KV_LORA], wuv_ref[2 * j])
                         + _dot(lat[:, (2 * j + 1) * MLA_KV_LORA:(2 * j + 2) * MLA_KV_LORA], wuv_ref[2 * j + 1]))
        o_b = jnp.concatenate(parts, axis=1)
    else:
        o_b = ob_ref[0]
    ga = ga_ref[0]
    gb = gb_ref[0]
    pa = _dot((oa_ref[0] * (ga * jax.nn.sigmoid(ga))).astype(BF16), wpa_ref[...])
    pb = _dot((o_b * (gb * jax.nn.sigmoid(gb))).astype(BF16), wpb_ref[...])
    h = jax.nn.sigmoid(ma_ref[0]) * pa + jax.nn.sigmoid(mb_ref[0]) * pb
    z = _dot(h.astype(BF16), wout_ref[...])
    y_ref[0] = x_ref[0] + _rms(z, npost_ref[...])


def _merge(x3, o_a, o_b, p, wts, tm, from_latent):
    b, t, _ = x3.shape
    tok = lambda w: pl.BlockSpec((1, tm, w), lambda bb, i: (bb, i, 0))
    return pl.pallas_call(
        functools.partial(_merge_kernel, from_latent),
        grid=(b, t // tm),
        in_specs=[tok(D_MODEL), tok(NSA_WIDTH), tok(o_b.shape[2]), tok(NSA_WIDTH), tok(MLA_WIDTH), tok(D_MODEL),
                  tok(D_MODEL), _full_spec((NSA_WIDTH, D_MODEL)), _full_spec((MLA_WIDTH, D_MODEL)),
                  _full_spec((D_MODEL, D_MODEL)), _full_spec((1, D_MODEL)),
                  _full_spec((MLA_HEADS, MLA_KV_LORA, LANES))],
        out_specs=tok(D_MODEL),
        out_shape=jax.ShapeDtypeStruct((b, t, D_MODEL), F32),
        compiler_params=pltpu.CompilerParams(dimension_semantics=("arbitrary",) * 2, vmem_limit_bytes=VMEM_LIMIT),
        name="merge",
    )(x3, o_a, o_b, p["ga"], p["gb"], p["ma"], p["mb"], wts["wpa"], wts["wpb"], wts["wout"], wts["npost"],
      wts["wuv"])


PAGES_PER_STEP = 32
POOL_PAGES = 4


def _page_copy(pt_ref, cache_ref, buf_ref, sem_ref, step, slot, k):
    return pltpu.make_async_copy(cache_ref.at[pt_ref[step * PAGES_PER_STEP + k]], buf_ref.at[slot, k],
                                 sem_ref.at[slot])


def _paged_pipeline(pt_ref, streams):
    step = pl.program_id(0) * pl.num_programs(1) + pl.program_id(1)
    last = pl.num_programs(0) * pl.num_programs(1) - 1
    slot = step % 2

    @pl.when(step == 0)
    def _():
        for cache_ref, buf_ref, sem_ref in streams:
            for k in range(PAGES_PER_STEP):
                _page_copy(pt_ref, cache_ref, buf_ref, sem_ref, 0, 0, k).start()

    @pl.when(step < last)
    def _():
        for cache_ref, buf_ref, sem_ref in streams:
            for k in range(PAGES_PER_STEP):
                _page_copy(pt_ref, cache_ref, buf_ref, sem_ref, step + 1, 1 - slot, k).start()

    for cache_ref, buf_ref, sem_ref in streams:
        for k in range(PAGES_PER_STEP):
            _page_copy(pt_ref, cache_ref, buf_ref, sem_ref, step, slot, k).wait()
    return slot


XLU_POOL_PAGES = 16
MXU_POOL_GROUP = 8


def _compress_pages_kernel(pt_ref, cache_ref, pool_ref, pe_ref, o_ref, buf_ref, sem_ref):
    slot = _paged_pipeline(pt_ref, [(cache_ref, buf_ref, sem_ref)])
    pe_sum = jnp.sum(pe_ref[...], axis=0, keepdims=True)
    pool = pool_ref[...]
    groups = [jnp.concatenate([buf_ref[slot, k + j] for j in range(MXU_POOL_GROUP)], axis=1)
              for k in range(XLU_POOL_PAGES, PAGES_PER_STEP, MXU_POOL_GROUP)]
    splits = [_split_bf16(x) for x in groups]
    by_mxu = [_dot_nt(pool, hi) + _dot_nt(pool, lo) for hi, lo in splits]
    by_xlu = []
    for k in range(XLU_POOL_PAGES):
        rows = buf_ref[slot, k].T
        by_xlu.append(jnp.sum(rows.reshape(rows.shape[0] // CMP_BLOCK, CMP_BLOCK, KV_WIDTH), axis=1))
    o_ref[0] = (jnp.concatenate(by_xlu + by_mxu, axis=0) + pe_sum) * (1.0 / CMP_BLOCK)


def _compress_pages(cache_t, pt_flat, pe, b, n_pages):
    page = cache_t.shape[2]
    per_step = PAGES_PER_STEP * page // CMP_BLOCK
    pool = np.zeros((MXU_POOL_GROUP * page // CMP_BLOCK, MXU_POOL_GROUP * page), np.float32)
    for s in range(pool.shape[1]):
        pool[s // CMP_BLOCK, s] = 1.0
    gs = pltpu.PrefetchScalarGridSpec(
        num_scalar_prefetch=1, grid=(b, n_pages // PAGES_PER_STEP),
        in_specs=[pl.BlockSpec(memory_space=pl.ANY),
                  pl.BlockSpec(pool.shape, lambda bb, c, pt: (0, 0)),
                  pl.BlockSpec((CMP_BLOCK, KV_WIDTH), lambda bb, c, pt: (0, 0))],
        out_specs=pl.BlockSpec((1, per_step, KV_WIDTH), lambda bb, c, pt: (bb, c, 0)),
        scratch_shapes=[pltpu.VMEM((2, PAGES_PER_STEP, KV_WIDTH, page), F32), pltpu.SemaphoreType.DMA((2,))])
    return pl.pallas_call(
        _compress_pages_kernel, grid_spec=gs,
        out_shape=jax.ShapeDtypeStruct((b, n_pages * page // CMP_BLOCK, KV_WIDTH), F32),
        compiler_params=pltpu.CompilerParams(dimension_semantics=("arbitrary",) * 2, vmem_limit_bytes=VMEM_LIMIT),
        name="compress_pages",
    )(pt_flat, cache_t, jnp.asarray(pool, BF16), pe)


def _pad_rows(v, rows):
    return jnp.concatenate([v, jnp.zeros((rows - v.shape[0], v.shape[1]), v.dtype)], axis=0)


def _stack_heads(qv, lo):
    a, b = qv[:, :LANES], qv[:, LANES:]
    z = jnp.zeros_like(a)
    return jnp.concatenate([jnp.where(lo, a, z), jnp.where(lo, z, a),
                            jnp.where(lo, b, z), jnp.where(lo, z, b)], axis=0)


def _sample_select_kernel(n_cmp, q_ref, kc_ref, oc_ref, idx_ref, imp_ref):
    bb = pl.program_id(0)
    lo1 = _lane_lo(1)
    lo_c = _lane_lo(n_cmp)
    q = q_ref[0]
    kc = kc_ref[0]
    even = (lax.broadcasted_iota(jnp.int32, (1, LANES), 1) & 1) == 0
    for g in range(NSA_KV_HEADS):
        kk, vv = _dup_kv(kc[:, g * LANES:(g + 1) * LANES], lo_c)
        qs = _pad_rows(_stack_heads(q[:, g * 2 * LANES:(g + 1) * 2 * LANES], lo1), PAD_ROWS)
        s = _dot_nt(qs, kk.astype(BF16))
        e = jnp.exp(s - jnp.max(s, axis=-1, keepdims=True))
        p = e / jnp.sum(e, axis=-1, keepdims=True)
        oc_ref[0, g] = _dot(p.astype(BF16), vv.astype(BF16))
        imp = p[0:1] + p[1:2] + p[2:3] + p[3:4]
        chunks = []
        for k in range(n_cmp // LANES):
            a = imp[:, k * LANES:(k + 1) * LANES]
            chunks.append(a + jnp.where(even, pltpu.roll(a, LANES - 1, 1), pltpu.roll(a, 1, 1)))
        imp_ref[pl.ds(bb * NSA_KV_HEADS + g, 1), :] = jnp.concatenate(chunks, axis=1)

    @pl.when(bb == pl.num_programs(0) - 1)
    def _():
        rows = imp_ref.shape[0]
        blk = lax.broadcasted_iota(jnp.int32, (rows, n_cmp), 1) >> 1
        blk_f = blk.astype(F32)
        slot = lax.broadcasted_iota(jnp.int32, (rows, N_SELECT), 1)
        v = jnp.where(blk == 0, -1.0, imp_ref[...])
        idx = jnp.where(slot == N_SELECT - 1, n_cmp // 2, 0)
        for k in range(1, N_SELECT - 1):
            top = jnp.max(v, axis=-1, keepdims=True)
            jmin = jnp.min(jnp.where(v == top, blk_f, float(n_cmp)), axis=-1, keepdims=True).astype(jnp.int32)
            idx = jnp.where(slot == k, jmin, idx)
            v = jnp.where(blk == jmin, -1.0, v)
        idx_ref[...] = idx


def _sample_select(q, kc_all, b):
    n_cmp = kc_all.shape[1]
    rows = b * NSA_KV_HEADS
    return pl.pallas_call(
        functools.partial(_sample_select_kernel, n_cmp),
        grid=(b,),
        in_specs=[pl.BlockSpec((1, 1, NSA_WIDTH), lambda bb: (bb, 0, 0)),
                  pl.BlockSpec((1, n_cmp, KV_WIDTH), lambda bb: (bb, 0, 0))],
        out_specs=[pl.BlockSpec((1, NSA_KV_HEADS, PAD_ROWS, LANES), lambda bb: (bb, 0, 0, 0)),
                   pl.BlockSpec((rows, N_SELECT), lambda bb: (0, 0))],
        out_shape=[jax.ShapeDtypeStruct((b, NSA_KV_HEADS, PAD_ROWS, LANES), F32),
                   jax.ShapeDtypeStruct((rows, N_SELECT), jnp.int32)],
        scratch_shapes=[pltpu.VMEM((rows, n_cmp), F32)],
        compiler_params=pltpu.CompilerParams(dimension_semantics=("arbitrary",), vmem_limit_bytes=VMEM_LIMIT),
        name="sample_select",
    )(q, kc_all)


def _extra_key_softmax(s_past, vt4_b, s_new, v_new):
    m = jnp.maximum(jnp.max(s_past, axis=-1, keepdims=True), s_new)
    e = jnp.exp(s_past - m)
    e_new = jnp.exp(s_new - m)
    den = jnp.sum(e, axis=-1, keepdims=True) + e_new
    return (_dot_nt(e.astype(BF16), vt4_b) + e_new * v_new) / den


def _sample_attend_kernel(idx_ref, pt_ref, *refs):
    del pt_ref
    n_sel = N_SELECT
    pages = refs[:n_sel]
    qr_ref, newkv_ref, neww_ref, newwf_ref, win_ref, gate_ref, oc_ref, o_ref, wout_ref = refs[n_sel:]
    bb, g = pl.program_id(0), pl.program_id(1)
    lo1 = _lane_lo(1)
    q = qr_ref[0].astype(F32)
    halves = [q[:, 0:LANES], q[:, LANES:2 * LANES]]
    rows = []
    for r in range(NSA_GROUP):
        a = halves[r // 2]
        rows.append(jnp.where(lo1, a if r % 2 == 0 else pltpu.roll(a, HALF, 1), 0.0))
    qs_f = _pad_rows(jnp.concatenate(rows, axis=0), PAD_ROWS)
    qs = qs_f.astype(BF16)
    twice_rows = lambda a: jnp.concatenate([a, a], axis=0)

    s_t = jnp.concatenate([pg[0] for pg in pages], axis=1)
    s_sel = _dot(qs, s_t.astype(BF16))
    base = (bb * NSA_KV_HEADS + g) * n_sel
    biases = []
    for k in range(n_sel - 1):
        odd = (idx_ref[base + k] & 1) == 1
        biases.append(jnp.where(lo1, jnp.where(odd, NEG, 0.0), jnp.where(odd, 0.0, NEG)))
    biases.append(jnp.full((1, LANES), NEG, F32))
    s_sel = s_sel + jnp.concatenate(biases, axis=1)
    nk = newkv_ref[0].astype(F32)
    s_new = jnp.sum(qs_f * nk[:, :LANES], axis=-1, keepdims=True)
    o_s = _extra_key_softmax(s_sel, twice_rows(s_t[HD:2 * HD]).astype(BF16), s_new, nk[:, LANES:])

    w = win_ref[0]
    nw = neww_ref[0].astype(F32)
    s_w = _dot(qs, w.astype(BF16))
    s_wn = jnp.sum(qs_f * nw[:, :LANES], axis=-1, keepdims=True)
    o_w = _extra_key_softmax(s_w, twice_rows(w[HD:2 * HD]).astype(BF16), s_wn, nw[:, LANES:])

    o_c = oc_ref[0, 0]
    gates = gate_ref[0]
    gate = jnp.where(g == 0, gates[:, 0:GATE_ROWS], gates[:, GATE_ROWS:2 * GATE_ROWS])
    heads = []
    for r in range(NSA_GROUP):
        heads.append(gate[:, 3 * r:3 * r + 1] * o_c[r:r + 1] + gate[:, 3 * r + 1:3 * r + 2] * o_s[r:r + 1]
                     + gate[:, 3 * r + 2:3 * r + 3] * o_w[r:r + 1])
    o_ref[0] = jnp.concatenate([jnp.where(lo1, heads[0], heads[1]), jnp.where(lo1, heads[2], heads[3])], axis=1)

    n_feat, n_w = w.shape
    new_row = jnp.broadcast_to(newwf_ref[0], (n_feat, n_feat))
    diag = (lax.broadcasted_iota(jnp.int32, (n_feat, n_feat), 0)
            == lax.broadcasted_iota(jnp.int32, (n_feat, n_feat), 1))
    new_col = jnp.sum(jnp.where(diag, new_row, 0.0), axis=1, keepdims=True)
    last = lax.broadcasted_iota(jnp.int32, (n_feat, LANES), 1) == LANES - 1
    chunks = []
    n_ch = n_w // LANES
    for c in range(n_ch):
        cur = pltpu.roll(w[:, c * LANES:(c + 1) * LANES], LANES - 1, 1)
        if c + 1 < n_ch:
            nxt = pltpu.roll(w[:, (c + 1) * LANES:(c + 2) * LANES], LANES - 1, 1)
        else:
            nxt = jnp.broadcast_to(new_col, (n_feat, LANES))
        chunks.append(jnp.where(last, nxt, cur))
    wout_ref[0] = jnp.concatenate(chunks, axis=1)


def _sample_attend(p, oc, idx_flat, pt_flat, slc_t, win_t, b, n_pages):
    page = slc_t.shape[2]
    win_len = win_t.shape[2]
    per_page = page // SEL_BLOCK

    def page_spec(k):
        def imap(bb, g, idx, pt):
            j = idx[(bb * NSA_KV_HEADS + g) * N_SELECT + k]
            return (pt[bb * n_pages + jnp.minimum(j // per_page, n_pages - 1)], g, 0)
        return pl.BlockSpec((1, LANES, page), imap)

    row = lambda w: pl.BlockSpec((1, 1, w), lambda bb, g, idx, pt: (bb, 0, g))
    wspec = pl.BlockSpec((1, LANES, win_len), lambda bb, g, idx, pt: (bb, g, 0))
    gs = pltpu.PrefetchScalarGridSpec(
        num_scalar_prefetch=2, grid=(b, NSA_KV_HEADS),
        in_specs=[page_spec(k) for k in range(N_SELECT)] + [
            row(2 * LANES), row(2 * LANES), row(2 * LANES), row(LANES), wspec,
            pl.BlockSpec((1, 1, NSA_KV_HEADS * GATE_ROWS), lambda bb, g, idx, pt: (bb, 0, 0)),
            pl.BlockSpec((1, 1, PAD_ROWS, LANES), lambda bb, g, idx, pt: (bb, g, 0, 0))],
        out_specs=[row(2 * LANES), wspec])
    return pl.pallas_call(
        _sample_attend_kernel, grid_spec=gs,
        out_shape=[jax.ShapeDtypeStruct((b, 1, NSA_WIDTH), F32),
                   jax.ShapeDtypeStruct((b, KV_WIDTH, win_len), F32)],
        compiler_params=pltpu.CompilerParams(dimension_semantics=("arbitrary",) * 2, vmem_limit_bytes=VMEM_LIMIT),
        name="sample_attend",
    )(idx_flat, pt_flat, *([slc_t] * N_SELECT), p["qrot"], p["slckv"], p["winkv"], p["win"], win_t,
      p["gate"], oc)


def _softmax_update(s, v_b, m, l, acc):
    m_new = jnp.maximum(m, jnp.max(s, axis=-1, keepdims=True))
    p = jnp.exp(s - m_new)
    alpha = jnp.exp(m - m_new)
    return m_new, alpha * l + jnp.sum(p, axis=-1, keepdims=True), alpha * acc + _dot(p.astype(BF16), v_b)


def _mla_sample_kernel(pt_ref, lat_ref, krt_ref, qlat_ref, qpe_ref, cnew_ref, krnew_ref, o_ref,
                       m_ref, l_ref, acc_ref, lat_buf, kr_buf, lat_sem, kr_sem):
    n = PAGES_PER_STEP
    slot = _paged_pipeline(pt_ref, [(lat_ref, lat_buf, lat_sem), (krt_ref, kr_buf, kr_sem)])
    lat_pages = [lat_buf.at[slot, k] for k in range(n)]
    kr_pages = [kr_buf.at[slot, k] for k in range(n)]
    step = pl.program_id(1)
    qlat = _pad_rows(qlat_ref[0], PAD_ROWS)
    qpe = _pad_rows(qpe_ref[0], PAD_ROWS)

    @pl.when(step == 0)
    def _():
        c_new = cnew_ref[0]
        s_new = (jnp.sum(qlat * c_new, axis=-1, keepdims=True)
                 + jnp.sum(qpe * krnew_ref[0], axis=-1, keepdims=True))
        m_ref[...] = s_new
        l_ref[...] = jnp.ones(l_ref.shape, F32)
        acc_ref[...] = jnp.broadcast_to(c_new, acc_ref.shape)

    qlat_b, qpe_b = qlat.astype(BF16), qpe.astype(BF16)
    groups = range(0, n, POOL_PAGES)
    c_bs = [jnp.concatenate([pg[...] for pg in lat_pages[k:k + POOL_PAGES]], axis=0).astype(BF16) for k in groups]
    kr_bs = [jnp.concatenate([pg[...] for pg in kr_pages[k:k + POOL_PAGES]], axis=1).astype(BF16) for k in groups]
    s_lat = [_dot_nt(qlat_b, c_b) for c_b in c_bs]
    s_pe = [_dot(qpe_b, kr_b) for kr_b in kr_bs]
    scores = [a + r for a, r in zip(s_lat, s_pe)]
    maxes = [jnp.max(s, axis=-1, keepdims=True) for s in scores]
    ps = [jnp.exp(s - mx) for s, mx in zip(scores, maxes)]
    sums = [jnp.sum(p_g, axis=-1, keepdims=True) for p_g in ps]
    accs = [_dot(p_g.astype(BF16), c_b) for p_g, c_b in zip(ps, c_bs)]
    m_old = m_ref[...]
    m = m_old
    for mx in maxes:
        m = jnp.maximum(m, mx)
    alpha = jnp.exp(m_old - m)
    l = alpha * l_ref[...]
    acc = alpha * acc_ref[...]
    for mx, l_g, acc_g in zip(maxes, sums, accs):
        w = jnp.exp(mx - m)
        l = l + w * l_g
        acc = acc + w * acc_g
    m_ref[...] = m
    l_ref[...] = l
    acc_ref[...] = acc

    @pl.when(step == pl.num_programs(1) - 1)
    def _():
        o_ref[0] = (acc / l)[0:MLA_HEADS]


def _mla_sample(p, lat_cache, kr_t, pt_flat, b, n_pages):
    page = lat_cache.shape[1]
    head = lambda w: pl.BlockSpec((1, MLA_HEADS, w), lambda bb, c, pt: (bb, 0, 0))
    row = lambda w: pl.BlockSpec((1, 1, w), lambda bb, c, pt: (bb, 0, 0))
    gs = pltpu.PrefetchScalarGridSpec(
        num_scalar_prefetch=1, grid=(b, n_pages // PAGES_PER_STEP),
        in_specs=[pl.BlockSpec(memory_space=pl.ANY), pl.BlockSpec(memory_space=pl.ANY),
                  head(MLA_KV_LORA), head(MLA_ROPE), row(MLA_KV_LORA), row(MLA_ROPE)],
        out_specs=head(MLA_KV_LORA),
        scratch_shapes=[pltpu.VMEM((PAD_ROWS, 1), F32), pltpu.VMEM((PAD_ROWS, 1), F32),
                        pltpu.VMEM((PAD_ROWS, MLA_KV_LORA), F32),
                        pltpu.VMEM((2, PAGES_PER_STEP, page, MLA_KV_LORA), F32),
                        pltpu.VMEM((2, PAGES_PER_STEP, MLA_ROPE, page), F32),
                        pltpu.SemaphoreType.DMA((2,)), pltpu.SemaphoreType.DMA((2,))])
    qlat = p["qlat"].reshape(b, MLA_HEADS, MLA_KV_LORA)
    qpe = p["qpe"].reshape(b, MLA_HEADS, MLA_ROPE)
    return pl.pallas_call(
        _mla_sample_kernel, grid_spec=gs,
        out_shape=jax.ShapeDtypeStruct((b, MLA_HEADS, MLA_KV_LORA), F32),
        compiler_params=pltpu.CompilerParams(dimension_semantics=("arbitrary",) * 2, vmem_limit_bytes=VMEM_LIMIT),
        name="mla_sample",
    )(pt_flat, lat_cache, kr_t, qlat, qpe, p["c"], p["kr"])


def _rope_angles(pos, theta, dim):
    half = dim // 2
    inv = 1.0 / (float(theta) ** (np.arange(half, dtype=np.float64) / half))
    ang = np.asarray(pos, np.float64)[:, None] * inv[None, :]
    return np.cos(ang).astype(np.float32), np.sin(ang).astype(np.float32)


def _rope_lane_tables(pos, rows):
    def table(theta, dim, period, active):
        half = dim // 2
        cos, sin = _rope_angles(pos, theta, dim)
        lane = np.arange(LANES)
        d = lane % period
        is_lo = (d < half) & active(lane)
        is_hi = (d >= half) & (d < dim) & active(lane)
        fi = np.where(d < half, d, np.clip(d - half, 0, half - 1))
        cos_l, sin_l = cos[:, fi], sin[:, fi]
        tab = np.stack([np.where(is_lo | is_hi, cos_l, 1.0), np.where(is_lo, -sin_l, 0.0),
                        np.where(is_hi, sin_l, 0.0)]).astype(np.float32)
        return jnp.asarray(np.broadcast_to(tab, (3, rows, LANES)))

    every = lambda lane: np.ones_like(lane, bool)
    keys_only = lambda lane: (lane % LANES) < HD
    return (table(ROPE_THETA, ROT_DIM, HD, every), table(ROPE_THETA, ROT_DIM, HD, keys_only),
            table(MLA_ROPE_THETA, MLA_ROPE, MLA_ROPE, every))


def _rope_row_tables(pos):
    cq, sq = _rope_angles(pos, ROPE_THETA, ROT_DIM)
    cm, sm = _rope_angles(pos, MLA_ROPE_THETA, MLA_ROPE)
    return jnp.asarray(np.stack([cq.T, sq.T])), jnp.asarray(np.stack([cm.T, sm.T]))


def _pack_weights(l, norm_pre, w_in, pe_cmp, q_norm, w_q_up, kv_norm, w_kv_up, w_proj_a, w_proj_b, w_out, norm_post):
    w = w_in[l].astype(BF16)
    o = IN_OFFSETS
    seg = lambda k: w[:, o[k]:o[k + 1]]
    gn = seg(4)
    per_group = 3 * NSA_GROUP
    gn_t = jnp.zeros((NSA_KV_HEADS * GATE_ROWS, D_MODEL), w.dtype)
    for g in range(NSA_KV_HEADS):
        gn_t = gn_t.at[g * GATE_ROWS:g * GATE_ROWS + per_group].set(gn[:, g * per_group:(g + 1) * per_group].T)
    w_t = jnp.concatenate([seg(0).T, seg(1).T, seg(2).T, seg(3).T, gn_t, seg(8).T], axis=0)
    w_krp4 = jnp.tile(seg(8).T, (LANES // MLA_ROPE, 1))
    w_cols = jnp.concatenate([seg(5), seg(6), seg(7), seg(9), seg(10), seg(11)], axis=1)
    wq = w_q_up[l]
    wqup = jnp.concatenate([wq[..., :MLA_NOPE].reshape(MLA_Q_LORA, -1), wq[..., MLA_NOPE:].reshape(MLA_Q_LORA, -1)],
                           axis=1)
    wkv = w_kv_up[l]
    wuk_pad = jnp.pad(wkv[..., :MLA_NOPE], ((0, 0), (0, 0), (0, LANES - MLA_NOPE)))
    rope_copy = jnp.pad(jnp.eye(MLA_ROPE, dtype=w.dtype), ((0, LANES - MLA_ROPE), (MLA_NOPE, MLA_ROPE)))
    wkx = jnp.concatenate([wuk_pad.reshape(MLA_KV_LORA, MLA_HEADS * LANES), jnp.tile(rope_copy, (1, MLA_HEADS))],
                          axis=0)
    wvt = jnp.transpose(wkv[..., MLA_NOPE:], (1, 2, 0)).reshape(MLA_WIDTH, MLA_KV_LORA)
    w2uk =jnp.transpose(wkv[..., :MLA_NOPE], (1, 2, 0)).reshape(MLA_HEADS // 2, LANES, MLA_KV_LORA)
    wv = jnp.transpose(wkv[..., MLA_NOPE:], (1, 0, 2))
    zeros = jnp.zeros_like(wv)
    even = (jnp.arange(MLA_HEADS) % 2 == 0)[:, None, None]
    wuv = jnp.concatenate([jnp.where(even, wv, zeros), jnp.where(even, zeros, wv)], axis=2)
    return {
        "npre": norm_pre[l][None].astype(F32), "w_t": w_t, "w_krp4": w_krp4, "w_cols": w_cols, "pe": pe_cmp[l].reshape(CMP_BLOCK, KV_WIDTH).astype(F32),
        "qnorm": q_norm[l][None].astype(F32), "wqup": wqup.astype(BF16), "wqupt": wqup.T.astype(BF16),
        "kvnorm": kv_norm[l][None].astype(F32), "w2uk": w2uk.astype(BF16), "wkx": wkx.astype(BF16),
        "wvt": wvt.astype(BF16), "wuv": wuv.astype(BF16),
        "wpa": w_proj_a[l].astype(BF16), "wpb": w_proj_b[l].astype(BF16), "wout": w_out[l].astype(BF16),
        "npost": norm_post[l][None].astype(F32),
    }


def _rows_from_cols(a):
    b, _, t = a.shape
    return a.reshape(b, NSA_KV_HEADS, 2, HD, t).transpose(0, 4, 1, 2, 3)


def _cols_from_rows(a):
    n, t = a.shape[:2]
    return a.transpose(0, 2, 3, 4, 1).reshape(n, KV_WIDTH, t)


def _prompt_layer(x, wts):
    b, t, _ = x.shape
    assert t % PROMPT_TM == 0
    p = _in_project_cols(x, _rope_row_tables(np.arange(t)), wts)
    o_a = _nsa_prompt(p, b, t)
    o_b = _mla_prompt(p, wts, b, t)
    y = _merge(x, o_a, o_b, p, wts, 512, from_latent=False)
    win_keep = min(WINDOW, t)
    return y, (_rows_from_cols(p["cmp"]), _rows_from_cols(p["slc"]), p["c"], p["kr"].transpose(0, 2, 1),
               _rows_from_cols(p["win"][:, :, t - win_keep:]))


def _sample_layer(x, l, caches, state_win, page_table, wts):
    cache_cmp, cache_slc, cache_lat, cache_kr = caches
    b, s_new, _ = x.shape
    assert s_new == 1
    n_pages = page_table.shape[1]
    page = cache_cmp.shape[2]
    past = n_pages * page
    assert past % SEL_BLOCK == 0 and n_pages % PAGES_PER_STEP == 0 and page == LANES
    win_len = state_win.shape[1]
    assert win_len == WINDOW and past >= WINDOW
    pt_flat = page_table.reshape(-1).astype(jnp.int32)

    tabs = _rope_lane_tables(np.full((1,), past), b)
    p = _in_project_rows(x.reshape(1, b, D_MODEL), tabs, wts)
    p = {k: v.reshape(b, 1, v.shape[-1]) for k, v in p.items()}

    kc_all = _compress_pages(_cols_from_rows(cache_cmp[l]), pt_flat, wts["pe"], b, n_pages)
    oc, idx = _sample_select(p["q"], kc_all, b)
    o_a, new_win = _sample_attend(p, oc, idx.reshape(-1), pt_flat, _cols_from_rows(cache_slc[l]),
                                  _cols_from_rows(state_win), b, n_pages)
    o_lat = _mla_sample(p, cache_lat[l], cache_kr[l].transpose(0, 2, 1), pt_flat, b, n_pages)
    pm = {k: p[k].reshape(1, b, -1) for k in ("ga", "gb", "ma", "mb")}
    y = _merge(x.reshape(1, b, D_MODEL), o_a.reshape(1, b, NSA_WIDTH),
               o_lat.reshape(1, b, MLA_HEADS * MLA_KV_LORA), pm, wts, b, from_latent=True)
    kv6 = lambda a: a.reshape(b, 1, NSA_KV_HEADS, 2, HD)
    return y.reshape(b, 1, D_MODEL), (kv6(p["cmp"]), kv6(p["slc"]), p["c"], p["kr"], _rows_from_cols(new_win))


def kernel(x_prompt, x_sample, cache_nsa_cmp, cache_nsa_slc, cache_mla_latent, cache_mla_krope, state_nsa_win,
           page_table, norm_pre, w_in, pe_cmp, q_norm, w_q_up, kv_norm, w_kv_up, w_proj_a, w_proj_b, w_out,
           norm_post):
    depth = w_in.shape[0]
    hp, hs = x_prompt, x_sample
    new_p, new_s = [], []
    for l in range(depth):
        wts = _pack_weights(l, norm_pre, w_in, pe_cmp, q_norm, w_q_up, kv_norm, w_kv_up, w_proj_a, w_proj_b,
                            w_out, norm_post)
        hp, sp = _prompt_layer(hp, wts)
        hs, ss = _sample_layer(hs, l, (cache_nsa_cmp, cache_nsa_slc, cache_mla_latent, cache_mla_krope),
                               state_nsa_win[l], page_table, wts)
        new_p.append(sp)
        new_s.append(ss)
    stack = lambda items, k: jnp.stack([s[k] for s in items])
    return (hp, hs) + tuple(stack(new_p, k) for k in range(5)) + tuple(stack(new_s, k) for k in range(5))
```

```python
import functools

import numpy as np
import jax
import jax.numpy as jnp
from jax import lax
from jax.experimental import pallas as pl
from jax.experimental.pallas import tpu as pltpu

D_MODEL = 1024
NSA_HEADS = 8
NSA_KV_HEADS = 2
NSA_GROUP = NSA_HEADS // NSA_KV_HEADS
HD = 64
NSA_WIDTH = NSA_HEADS * HD
KV_WIDTH = NSA_KV_HEADS * 2 * HD
ROT_DIM = HD // 4
ROPE_THETA = 500000.0
CMP_BLOCK = 32
SEL_BLOCK = 64
N_SELECT = 16
WINDOW = 512
NSA_SCALE = HD ** -0.5

MLA_HEADS = 8
MLA_Q_LORA = 384
MLA_KV_LORA = 256
MLA_NOPE = 64
MLA_ROPE = 32
MLA_V = 64
MLA_WIDTH = MLA_HEADS * MLA_V
MLA_ROPE_THETA = 10000.0
MLA_SCALE = (MLA_NOPE + MLA_ROPE) ** -0.5
LOG2E = 1.4426950408889634
SHIFT_LIMIT = 40.0

RMS_EPS = 1e-6
NEG = -1e30
FORCE_SCORE = 1e4

IN_SPLITS = (NSA_WIDTH, KV_WIDTH, KV_WIDTH, KV_WIDTH, 3 * NSA_HEADS, NSA_WIDTH,
             MLA_Q_LORA, MLA_KV_LORA, MLA_ROPE, MLA_WIDTH, D_MODEL, D_MODEL)
IN_OFFSETS = tuple(int(v) for v in np.cumsum((0,) + IN_SPLITS))

LANES = 128
HALF = LANES // 2
GATE_ROWS = 16
PAD_ROWS = 16

PT_Q = (0, 512)
PT_CMP = (512, 768)
PT_SLC = (768, 1024)
PT_WIN = (1024, 1280)
PT_GN = (1280, 1280 + NSA_KV_HEADS * GATE_ROWS)
PT_KRP = (PT_GN[1], PT_GN[1] + MLA_ROPE)
PT_ROWS = PT_KRP[1]
PR_GA = (0, 512)
PR_QD = (512, 896)
PR_KVD = (896, 1152)
PR_GB = (1152, 1664)
PR_MA = (1664, 2688)
PR_MB = (2688, 3712)
PR_COLS = 3712

PROMPT_TM = 512
ATT_TQ = 256
ATT_TK = 256
MLA_TK = 256

VMEM_LIMIT = 48 * 1024 * 1024
BF16 = jnp.bfloat16
F32 = jnp.float32


def _full_spec(shape):
    nd = len(shape)
    return pl.BlockSpec(shape, lambda *_: (0,) * nd)


def _lane_lo(rows):
    return lax.broadcasted_iota(jnp.int32, (rows, LANES), 1) < HALF


def _dot(a, b):
    return jnp.dot(a, b, preferred_element_type=F32)


def _dot_nt(a, b):
    return lax.dot_general(a, b, (((1,), (1,)), ((), ())), preferred_element_type=F32)


def _rms(v, gain):
    return v * lax.rsqrt(jnp.mean(v * v, axis=-1, keepdims=True) + RMS_EPS) * gain


def _split_bf16(v):
    hi = v.astype(BF16)
    return hi, (v - hi.astype(F32)).astype(BF16)


def _rope_tiles(v, tab_ref, shift):
    c, s_lo, s_hi = tab_ref[0], tab_ref[1], tab_ref[2]
    out = []
    for k in range(v.shape[1] // LANES):
        a = v[:, k * LANES:(k + 1) * LANES]
        out.append(a * c + pltpu.roll(a, LANES - shift, 1) * s_lo + pltpu.roll(a, shift, 1) * s_hi)
    return out[0] if len(out) == 1 else jnp.concatenate(out, axis=1)


def _dup_kv(a, lo):
    r = pltpu.roll(a, HALF, 1)
    return jnp.where(lo, a, r), jnp.where(lo, r, a)


def _kv_pack(v):
    lo = _lane_lo(v.shape[0])
    parts = []
    for g in range(NSA_KV_HEADS):
        kk, vv = _dup_kv(v[:, g * LANES:(g + 1) * LANES], lo)
        parts += [kk, vv]
    return jnp.concatenate(parts, axis=1).astype(BF16)


def _inproj_rows_kernel(x_ref, npre_ref, wt_ref, wkrp_ref, w_ref, tq_ref, tkv_ref, tm_ref, qnorm_ref, wqup_ref,
                        kvnorm_ref, w2uk_ref, q_ref, qrot_ref, gate_ref, cmp_ref, slc_ref, win_ref, slckv_ref,
                        winkv_ref, ga_ref, gb_ref, ma_ref, mb_ref, qpe_ref, c_ref, kr_ref, qlat_ref):
    xb = _rms(x_ref[0], npre_ref[...]).astype(BF16)
    segt = lambda lohi: _dot_nt(xb, wt_ref[lohi[0]:lohi[1], :])
    seg = lambda lohi: _dot(xb, w_ref[:, lohi[0]:lohi[1]])

    a = segt((PT_Q[0], PT_WIN[1]))
    q = a[:, :NSA_WIDTH]
    q_ref[0] = (q * NSA_SCALE).astype(BF16)
    qrot_ref[0] = (_rope_tiles(q, tq_ref, ROT_DIM // 2) * NSA_SCALE).astype(BF16)
    cmp_ref[0] = a[:, NSA_WIDTH:NSA_WIDTH + KV_WIDTH]
    kvs = _rope_tiles(a[:, NSA_WIDTH + KV_WIDTH:NSA_WIDTH + 2 * KV_WIDTH], tkv_ref, ROT_DIM // 2)
    slc_ref[0] = kvs
    slckv_ref[0] = _kv_pack(kvs)
    kvw = _rope_tiles(a[:, NSA_WIDTH + 2 * KV_WIDTH:], tkv_ref, ROT_DIM // 2)
    win_ref[0] = kvw
    winkv_ref[0] = _kv_pack(kvw)

    gate_ref[0] = jax.nn.sigmoid(segt(PT_GN))
    ga_ref[0] = seg(PR_GA)
    gb_ref[0] = seg(PR_GB)
    ma_ref[0] = seg(PR_MA)
    mb_ref[0] = seg(PR_MB)

    qd = _rms(seg(PR_QD), qnorm_ref[...]).astype(BF16)
    qh = _dot(qd, wqup_ref[...])
    qn = qh[:, :MLA_HEADS * MLA_NOPE].astype(BF16)
    qpe_ref[0] = _rope_tiles(qh[:, MLA_HEADS * MLA_NOPE:], tm_ref, MLA_ROPE // 2) * MLA_SCALE
    c_ref[0] = _rms(seg(PR_KVD), kvnorm_ref[...])
    kr_ref[0] = _rope_tiles(_dot_nt(xb, wkrp_ref[...]), tm_ref, MLA_ROPE // 2)[:, :MLA_ROPE]

    rows = qn.shape[0]
    lo = _lane_lo(rows)
    z = jnp.zeros((rows, LANES), BF16)
    parts = []
    for j in range(MLA_HEADS // 2):
        pair = qn[:, j * LANES:(j + 1) * LANES]
        parts.append(_dot(jnp.where(lo, pair, z), w2uk_ref[j]) * MLA_SCALE)
        parts.append(_dot(jnp.where(lo, z, pair), w2uk_ref[j]) * MLA_SCALE)
    qlat_ref[0] = jnp.concatenate(parts, axis=1)


def _in_project_rows(x3, tabs, wts):
    b, t, _ = x3.shape
    tq, tkv, tmla = tabs
    row = lambda w, dt: jax.ShapeDtypeStruct((b, t, w), dt)
    names = ["q", "qrot", "gate", "cmp", "slc", "win", "slckv", "winkv", "ga", "gb", "ma", "mb", "qpe", "c", "kr",
             "qlat"]
    out_shape = [row(512, BF16), row(512, BF16), row(NSA_KV_HEADS * GATE_ROWS, F32), row(256, F32), row(256, F32),
                 row(256, F32),
                 row(512, BF16), row(512, BF16), row(512, F32), row(512, F32), row(1024, F32), row(1024, F32),
                 row(MLA_HEADS * MLA_ROPE, F32), row(MLA_KV_LORA, F32), row(MLA_ROPE, F32),
                 row(MLA_HEADS * MLA_KV_LORA, F32)]
    tok = lambda w: pl.BlockSpec((1, t, w), lambda bb: (bb, 0, 0))
    tab = _full_spec((3, t, LANES))
    in_specs = [tok(D_MODEL), _full_spec((1, D_MODEL)), _full_spec((PT_ROWS, D_MODEL)), _full_spec((LANES, D_MODEL)),
                _full_spec((D_MODEL, PR_COLS)), tab, tab, tab, _full_spec((1, MLA_Q_LORA)), _full_spec((MLA_Q_LORA, MLA_HEADS * (MLA_NOPE + MLA_ROPE))),
                _full_spec((1, MLA_KV_LORA)), _full_spec((MLA_HEADS // 2, LANES, MLA_KV_LORA))]
    res = pl.pallas_call(
        _inproj_rows_kernel, grid=(b,), in_specs=in_specs, out_specs=[tok(s.shape[2]) for s in out_shape],
        out_shape=out_shape,
        compiler_params=pltpu.CompilerParams(dimension_semantics=("arbitrary",), vmem_limit_bytes=VMEM_LIMIT),
        name="in_project_rows",
    )(x3, wts["npre"], wts["w_t"], wts["w_krp4"], wts["w_cols"], tq, tkv, tmla, wts["qnorm"], wts["wqup"],
      wts["kvnorm"], wts["w2uk"])
    return dict(zip(names, res))


def _rope_rows(x, cos, sin, half):
    x1, x2 = x[0:half], x[half:2 * half]
    parts = [x1 * cos - x2 * sin, x1 * sin + x2 * cos]
    if x.shape[0] > 2 * half:
        parts.append(x[2 * half:])
    return jnp.concatenate(parts, axis=0)


def _store_chunks(ref, v):
    tk = ref.shape[3]
    for j in range(ref.shape[1]):
        ref[0, j] = v[:, j * tk:(j + 1) * tk]


def _inproj_cols_kernel(x_ref, npre_ref, wt_ref, w_ref, ropeq_ref, ropem_ref, pe_ref, qnorm_ref, wqupt_ref,
                        kvnorm_ref, pool_ref, wkx_ref, wvt_ref,
                        qt_ref, qrt_ref, gate_ref, cmp_ref, slc_ref, win_ref, slcb_ref, winb_ref, slcr_ref,
                        winr_ref, kc_ref, ga_ref, gb_ref, ma_ref, mb_ref, qmt_ref, c_ref, kmla_ref, vmt_ref,
                        kr_ref, kn_ref):
    tm = x_ref.shape[1]
    xb = _rms(x_ref[0], npre_ref[...]).astype(BF16)
    segt = lambda lohi: _dot_nt(wt_ref[lohi[0]:lohi[1], :], xb)
    seg = lambda lohi: _dot(xb, w_ref[:, lohi[0]:lohi[1]])
    cq, sq = ropeq_ref[0], ropeq_ref[1]
    cm, sm = ropem_ref[0], ropem_ref[1]
    hq, hm = ROT_DIM // 2, MLA_ROPE // 2

    qt = segt(PT_Q)
    qt_ref[0] = (qt * (NSA_SCALE * LOG2E)).astype(BF16)
    qrt = jnp.concatenate([_rope_rows(qt[h * HD:(h + 1) * HD], cq, sq, hq) for h in range(NSA_HEADS)], axis=0)
    qrt_ref[0] = (qrt * (NSA_SCALE * LOG2E)).astype(BF16)

    cmpt = segt(PT_CMP)
    cmp_ref[0] = cmpt
    hi, lo = _split_bf16(cmpt)
    pool = pool_ref[...]
    pooled = (_dot_nt(pool, hi) + _dot_nt(pool, lo))[0:tm // CMP_BLOCK]
    kc_ref[0] = (pooled + jnp.sum(pe_ref[...], axis=0, keepdims=True)) * (1.0 / CMP_BLOCK)

    def rope_kv(v):
        parts = []
        for g in range(NSA_KV_HEADS):
            parts.append(_rope_rows(v[g * LANES:g * LANES + HD], cq, sq, hq))
            parts.append(v[g * LANES + HD:(g + 1) * LANES])
        return jnp.concatenate(parts, axis=0)

    slct = rope_kv(segt(PT_SLC))
    slc_ref[0] = slct
    _store_chunks(slcb_ref, slct.astype(BF16))
    slcr_ref[0] = slct.T.astype(BF16)
    wint = rope_kv(segt(PT_WIN))
    win_ref[0] = wint
    _store_chunks(winb_ref, wint.astype(BF16))
    winr_ref[0] = wint.T.astype(BF16)

    gate_ref[0] = jax.nn.sigmoid(segt(PT_GN))
    krt = _rope_rows(segt(PT_KRP), cm, sm, hm)
    kr_ref[0] = krt
    c = _rms(seg(PR_KVD), kvnorm_ref[...])
    c_ref[0] = c
    c_b = c.astype(BF16)
    kr_rows = jnp.concatenate([krt, jnp.zeros((LANES - MLA_ROPE, tm), F32)], axis=0).T
    ckr = jnp.concatenate([c_b, kr_rows.astype(BF16)], axis=1)
    kmla = _dot(ckr, wkx_ref[...]).astype(BF16)
    kmla_ref[0] = kmla
    kf = kmla.astype(F32)
    norms = []
    for h in range(MLA_HEADS):
        n2 = jnp.sum(kf[:, h * LANES:(h + 1) * LANES] ** 2, axis=1, keepdims=True)
        norms.append(jnp.broadcast_to(jnp.sqrt(jnp.max(n2, axis=0, keepdims=True)), (1, LANES)))
    kn_ref[0] = jnp.concatenate(norms, axis=0)
    _store_chunks(vmt_ref, _dot_nt(wvt_ref[...], c_b).astype(BF16))

    qd = _rms(seg(PR_QD), qnorm_ref[...]).astype(BF16)
    qht = _dot_nt(wqupt_ref[...], qd)
    n_nope = MLA_HEADS * MLA_NOPE
    zq = jnp.zeros((LANES - MLA_NOPE - MLA_ROPE, tm), F32)
    parts = []
    for h in range(MLA_HEADS):
        parts += [qht[h * MLA_NOPE:(h + 1) * MLA_NOPE],
                  _rope_rows(qht[n_nope + h * MLA_ROPE:n_nope + (h + 1) * MLA_ROPE], cm, sm, hm), zq]
    qmt_ref[0] = (jnp.concatenate(parts, axis=0) * (MLA_SCALE * LOG2E)).astype(BF16)

    ga_ref[0] = seg(PR_GA)
    gb_ref[0] = seg(PR_GB)
    ma_ref[0] = seg(PR_MA)
    mb_ref[0] = seg(PR_MB)


def _in_project_cols(x3, ropes, wts):
    b, t, _ = x3.shape
    tm = PROMPT_TM
    nt = t // tm
    ropeq, ropem = ropes
    pool = np.zeros((16, tm), np.float32)
    for s in range(tm):
        pool[s // CMP_BLOCK, s] = 1.0
    sds = jax.ShapeDtypeStruct
    rows = lambda w: pl.BlockSpec((1, tm, w), lambda i, bb: (bb, i, 0))
    cols = lambda w: pl.BlockSpec((1, w, tm), lambda i, bb: (bb, 0, i))
    chunk = lambda w, tk: pl.BlockSpec((1, tm // tk, w, tk), lambda i, bb: (bb, i, 0, 0))
    outs = [
        ("qt", sds((b, NSA_WIDTH, t), BF16), cols(NSA_WIDTH)),
        ("qrt", sds((b, NSA_WIDTH, t), BF16), cols(NSA_WIDTH)),
        ("gate", sds((b, NSA_KV_HEADS * GATE_ROWS, t), F32), cols(NSA_KV_HEADS * GATE_ROWS)),
        ("cmp", sds((b, KV_WIDTH, t), F32), cols(KV_WIDTH)),
        ("slc", sds((b, KV_WIDTH, t), F32), cols(KV_WIDTH)),
        ("win", sds((b, KV_WIDTH, t), F32), cols(KV_WIDTH)),
        ("slcb", sds((b, t // ATT_TK, KV_WIDTH, ATT_TK), BF16), chunk(KV_WIDTH, ATT_TK)),
        ("winb", sds((b, t // ATT_TK, KV_WIDTH, ATT_TK), BF16), chunk(KV_WIDTH, ATT_TK)),
        ("slcr", sds((b, t, KV_WIDTH), BF16), rows(KV_WIDTH)),
        ("winr", sds((b, t, KV_WIDTH), BF16), rows(KV_WIDTH)),
        ("kc", sds((b, t // CMP_BLOCK, KV_WIDTH), F32),
         pl.BlockSpec((1, tm // CMP_BLOCK, KV_WIDTH), lambda i, bb: (bb, i, 0))),
        ("ga", sds((b, t, NSA_WIDTH), F32), rows(NSA_WIDTH)),
        ("gb", sds((b, t, MLA_WIDTH), F32), rows(MLA_WIDTH)),
        ("ma", sds((b, t, D_MODEL), F32), rows(D_MODEL)),
        ("mb", sds((b, t, D_MODEL), F32), rows(D_MODEL)),
        ("qmt", sds((b, MLA_HEADS * LANES, t), BF16), cols(MLA_HEADS * LANES)),
        ("c", sds((b, t, MLA_KV_LORA), F32), rows(MLA_KV_LORA)),
        ("kmla", sds((b, t, MLA_HEADS * LANES), BF16), rows(MLA_HEADS * LANES)),
        ("vmt", sds((b, t // MLA_TK, MLA_WIDTH, MLA_TK), BF16), chunk(MLA_WIDTH, MLA_TK)),
        ("kr", sds((b, MLA_ROPE, t), F32), cols(MLA_ROPE)),
        ("kn", sds((b, nt * MLA_HEADS, LANES), F32), pl.BlockSpec((1, MLA_HEADS, LANES), lambda i, bb: (bb, i, 0))),
    ]
    rope_spec = lambda half: pl.BlockSpec((2, half, tm), lambda i, bb: (0, 0, i))
    in_specs = [rows(D_MODEL), _full_spec((1, D_MODEL)), _full_spec((PT_ROWS, D_MODEL)),
                _full_spec((D_MODEL, PR_COLS)), rope_spec(ROT_DIM // 2), rope_spec(MLA_ROPE // 2),
                _full_spec((CMP_BLOCK, KV_WIDTH)), _full_spec((1, MLA_Q_LORA)),
                _full_spec((MLA_HEADS * (MLA_NOPE + MLA_ROPE), MLA_Q_LORA)), _full_spec((1, MLA_KV_LORA)),
                _full_spec((16, tm)), _full_spec((MLA_KV_LORA + LANES, MLA_HEADS * LANES)),
                _full_spec((MLA_WIDTH, MLA_KV_LORA))]
    res = pl.pallas_call(
        _inproj_cols_kernel, grid=(nt, b), in_specs=in_specs, out_specs=[o[2] for o in outs],
        out_shape=[o[1] for o in outs],
        compiler_params=pltpu.CompilerParams(dimension_semantics=("arbitrary", "arbitrary"),
                                             vmem_limit_bytes=VMEM_LIMIT),
        name="in_project_cols",
    )(x3, wts["npre"], wts["w_t"], wts["w_cols"], ropeq, ropem, wts["pe"], wts["qnorm"], wts["wqupt"],
      wts["kvnorm"], jnp.asarray(pool, BF16), wts["wkx"], wts["wvt"])
    return dict(zip([o[0] for o in outs], res))


def _online_update(s, vt, m, l, acc):
    d, keys = vt.shape
    m_new = jnp.maximum(m, jnp.max(s, axis=0, keepdims=True))
    p = jnp.exp2(s - m_new).astype(BF16)
    alpha = jnp.exp2(m - m_new)
    pv = _dot(jnp.concatenate([vt, jnp.ones((PAD_ROWS, keys), BF16)], axis=0), p)
    return m_new, alpha * l + pv[d:d + 1], alpha * acc + pv[0:d]


def _nsa_prompt_kernel(n_blk, qt_ref, qrt_ref, gate_ref, kc_ref, ks_ref, kst_ref, kw_ref, kwt_ref, o_ref, sb_ref,
                       kn_ref):
    i = pl.program_id(2)
    tq, tk = ATT_TQ, ATT_TK
    nl = NSA_GROUP * tq
    zq = jnp.zeros((HD, tq), BF16)

    def widen(qt):
        return jnp.concatenate([jnp.concatenate([qt[r * HD:(r + 1) * HD], zq], axis=0)
                                for r in range(NSA_GROUP)], axis=1)

    qc = widen(qt_ref[0])
    qr = widen(qrt_ref[0])

    n_cmp = 2 * n_blk
    kc = kc_ref[0]
    s_c = _dot(kc.astype(BF16), qc)
    rho = lax.broadcasted_iota(jnp.int32, (n_cmp, nl), 0)
    cmp_idx = jnp.where(rho < n_blk, 2 * rho, 2 * (rho - n_blk) + 1)
    t_l = i * tq + (lax.broadcasted_iota(jnp.int32, (n_cmp, nl), 1) & (tq - 1))
    mask_c = cmp_idx * CMP_BLOCK + (CMP_BLOCK - 1) <= t_l
    s_c = jnp.where(mask_c, s_c, NEG)
    e_c = jnp.where(mask_c, jnp.exp2(s_c - jnp.max(s_c, axis=0, keepdims=True)), 0.0)
    p_c = e_c / jnp.maximum(jnp.sum(e_c, axis=0, keepdims=True), 1e-30)
    kct = jnp.concatenate([kc, jnp.zeros((LANES - n_cmp, LANES), F32)], axis=0).T
    p_pad = jnp.concatenate([p_c, jnp.zeros((LANES - n_cmp, nl), F32)], axis=0)
    o_c = _dot(kct[HD:2 * HD].astype(BF16), p_pad.astype(BF16))

    imp = p_c[:, 0:tq]
    for r in range(1, NSA_GROUP):
        imp = imp + p_c[:, r * tq:(r + 1) * tq]
    imp_blk = imp[0:n_blk] + imp[n_blk:n_cmp]
    blk = lax.broadcasted_iota(jnp.int32, (n_blk, tq), 0)
    t_q = i * tq + lax.broadcasted_iota(jnp.int32, (n_blk, tq), 1)
    ahead_of = t_q - blk * SEL_BLOCK
    score = jnp.where(blk == 0, FORCE_SCORE,
                      jnp.where(ahead_of < 0, -FORCE_SCORE, jnp.where(ahead_of < SEL_BLOCK, FORCE_SCORE, imp_blk)))
    rank = jnp.zeros((n_blk, tq), F32)
    for j in range(n_blk):
        other = score[j:j + 1, :]
        tie = jnp.where(blk > j, 1.0, 0.0)
        rank = rank + jnp.where(other > score, 1.0, jnp.where(other == score, tie, 0.0))
    sb_ref[...] = jnp.where(rank < N_SELECT, jnp.where(score > -1.0, 0.0, NEG), NEG)

    key_r = lax.broadcasted_iota(jnp.int32, (tk, tq), 0)
    t_k = i * tq + lax.broadcasted_iota(jnp.int32, (tk, tq), 1)
    n_chunks = ((i + 1) * tq + tk - 1) // tk
    c_lo = jnp.maximum(i * tq - WINDOW, 0) // tk
    per_chunk = tk // SEL_BLOCK
    tile = lambda bias: jnp.concatenate([bias] * NSA_GROUP, axis=1)
    ones = jnp.ones((PAD_ROWS, tk), BF16)

    def sel_bias(c):
        rows = [jnp.broadcast_to(sb_ref[pl.ds(c * per_chunk + j, 1), :], (SEL_BLOCK, tq)) for j in range(per_chunk)]
        return jnp.concatenate(rows, axis=0)

    def values(ref, c):
        return jnp.concatenate([ref[0, c, HD:2 * HD, :], ones], axis=0)

    qf = qrt_ref[0].astype(F32)
    q_norm2 = [jnp.sum(qf[r * HD:(r + 1) * HD] ** 2, axis=0, keepdims=True) for r in range(NSA_GROUP)]
    q_norm = jnp.sqrt(jnp.maximum(jnp.maximum(q_norm2[0], q_norm2[1]), jnp.maximum(q_norm2[2], q_norm2[3])))

    def key_norm(ref):
        kf = ref[0, :, 0:HD, :].astype(F32)
        n2 = jnp.max(jnp.sum(kf * kf, axis=1, keepdims=True), axis=0)
        return jnp.broadcast_to(jnp.sqrt(jnp.max(n2, axis=1, keepdims=True)), (1, LANES))

    @pl.when(i == 0)
    def _():
        kn_ref[0:1, :] = key_norm(kst_ref)
        kn_ref[1:2, :] = key_norm(kwt_ref)

    wide = lambda row: jnp.concatenate([row] * (tq // LANES), axis=1)
    bound_s = q_norm * (wide(kn_ref[0:1, :]) * 1.001) + 1e-3
    bound_w = q_norm * (wide(kn_ref[1:2, :]) * 1.001) + 1e-3
    safe = jnp.maximum(jnp.max(bound_s), jnp.max(bound_w)) <= SHIFT_LIMIT

    def attend_shifted():
        def past_probs(c):
            base = pl.multiple_of(c * tk, tk)
            s = _dot(ks_ref[0, pl.ds(base, tk), :], qr) + tile(sel_bias(c) - bound_s)
            return jnp.exp2(s).astype(BF16)

        def past_body(c, acc):
            return acc + _dot(values(kst_ref, c), past_probs(c))

        def past_pair(cc, acc):
            p0, p1 = past_probs(2 * cc), past_probs(2 * cc + 1)
            return acc + _dot(values(kst_ref, 2 * cc), p0) + _dot(values(kst_ref, 2 * cc + 1), p1)

        def near_body(c, carry):
            base = pl.multiple_of(c * tk, tk)
            s = _dot(ks_ref[0, pl.ds(base, tk), :], qr)
            w = _dot(kw_ref[0, pl.ds(base, tk), :], qr)
            dist = t_k - (base + key_r)
            causal = jnp.where(dist >= 0, 0.0, NEG)
            p_s = jnp.exp2(s + tile(sel_bias(c) + causal - bound_s)).astype(BF16)
            p_w = jnp.exp2(w + tile(jnp.where(dist <= WINDOW, causal, NEG) - bound_w)).astype(BF16)
            return carry[0] + _dot(values(kst_ref, c), p_s), carry[1] + _dot(values(kwt_ref, c), p_w)

        zero = jnp.zeros((HD + PAD_ROWS, nl), F32)
        far = lax.fori_loop(0, c_lo // 2, past_pair, zero)
        far = lax.fori_loop(2 * (c_lo // 2), c_lo, past_body, far)
        a_s, a_w = lax.fori_loop(c_lo, n_chunks, near_body, (far, zero))
        return a_s[0:HD] / a_s[HD:HD + 1], a_w[0:HD] / a_w[HD:HD + 1]

    def attend_online():
        init = (jnp.full((1, nl), NEG, F32), jnp.zeros((1, nl), F32), jnp.zeros((HD, nl), F32))

        def past_body(c, carry):
            base = pl.multiple_of(c * tk, tk)
            s = _dot(ks_ref[0, pl.ds(base, tk), :], qr) + tile(sel_bias(c))
            return _online_update(s, kst_ref[0, c, HD:2 * HD, :], *carry)

        def near_body(c, carry):
            base = pl.multiple_of(c * tk, tk)
            s = _dot(ks_ref[0, pl.ds(base, tk), :], qr)
            w = _dot(kw_ref[0, pl.ds(base, tk), :], qr)
            dist = t_k - (base + key_r)
            causal = jnp.where(dist >= 0, 0.0, NEG)
            s = s + tile(sel_bias(c) + causal)
            w = w + tile(jnp.where(dist <= WINDOW, causal, NEG))
            return (_online_update(s, kst_ref[0, c, HD:2 * HD, :], *carry[:3])
                    + _online_update(w, kwt_ref[0, c, HD:2 * HD, :], *carry[3:]))

        far = lax.fori_loop(0, c_lo, past_body, init)
        _, l_s, a_s, _, l_w, a_w = lax.fori_loop(c_lo, n_chunks, near_body, far + init)
        return a_s / l_s, a_w / l_w

    o_s, o_w = lax.cond(safe, attend_shifted, attend_online)

    gate = gate_ref[0]
    heads = []
    for r in range(NSA_GROUP):
        sl = slice(r * tq, (r + 1) * tq)
        heads.append(gate[3 * r:3 * r + 1] * o_c[:, sl] + gate[3 * r + 1:3 * r + 2] * o_s[:, sl]
                     + gate[3 * r + 2:3 * r + 3] * o_w[:, sl])
    o_ref[0] = jnp.concatenate(heads, axis=0).T


def _nsa_prompt(p, b, t):
    tq, tk = ATT_TQ, ATT_TK
    n_blk = t // SEL_BLOCK
    assert 2 * n_blk <= LANES and t % tk == 0
    kc = p["kc"].reshape(b, n_blk, 2, KV_WIDTH).transpose(0, 2, 1, 3).reshape(b, 2 * n_blk, KV_WIDTH)
    qspec = pl.BlockSpec((1, NSA_GROUP * HD, tq), lambda bb, g, i: (bb, g, i))
    rm = pl.BlockSpec((1, t, LANES), lambda bb, g, i: (bb, 0, g))
    fm = pl.BlockSpec((1, t // tk, LANES, tk), lambda bb, g, i: (bb, 0, g, 0))
    return pl.pallas_call(
        functools.partial(_nsa_prompt_kernel, n_blk),
        grid=(b, NSA_KV_HEADS, t // tq),
        in_specs=[qspec, qspec, pl.BlockSpec((1, GATE_ROWS, tq), lambda bb, g, i: (bb, g, i)),
                  pl.BlockSpec((1, 2 * n_blk, LANES), lambda bb, g, i: (bb, 0, g)), rm, fm, rm, fm],
        out_specs=pl.BlockSpec((1, tq, NSA_GROUP * HD), lambda bb, g, i: (bb, i, g)),
        out_shape=jax.ShapeDtypeStruct((b, t, NSA_WIDTH), F32),
        scratch_shapes=[pltpu.VMEM((n_blk, tq), F32), pltpu.VMEM((8, LANES), F32)],
        compiler_params=pltpu.CompilerParams(dimension_semantics=("arbitrary",) * 3, vmem_limit_bytes=VMEM_LIMIT),
        name="nsa_prompt",
    )(p["qt"], p["qrt"], p["gate"], kc, p["slcr"], p["slcb"], p["winr"], p["winb"])


MLA_TQ = 256


def _mla_prompt_kernel(qmt_ref, k_ref, vt_ref, kn_ref, o_ref, m_ref, l_ref, acc_ref):
    i = pl.program_id(1)
    tq, tk = MLA_TQ, MLA_TK
    heads = range(MLA_HEADS)
    m_ref[...] = jnp.full(m_ref.shape, NEG, F32)
    l_ref[...] = jnp.zeros(l_ref.shape, F32)
    acc_ref[...] = jnp.zeros(acc_ref.shape, F32)
    key_r = lax.broadcasted_iota(jnp.int32, (tk, tq), 0)
    t_k = i * tq + lax.broadcasted_iota(jnp.int32, (tk, tq), 1)
    n_chunks = ((i + 1) * tq + tk - 1) // tk
    n_past = (i * tq + 1) // tk
    ones = jnp.ones((PAD_ROWS, tk), BF16)

    def keys(c_base, h):
        return k_ref[0, pl.ds(c_base, tk), h * LANES:(h + 1) * LANES]

    kn = kn_ref[0]
    k_max = kn[0:MLA_HEADS]
    for j in range(1, kn.shape[0] // MLA_HEADS):
        k_max = jnp.maximum(k_max, kn[j * MLA_HEADS:(j + 1) * MLA_HEADS])
    bounds = []
    for h in heads:
        qf = qmt_ref[0, h * LANES:(h + 1) * LANES, :].astype(F32)
        q_norm = jnp.sqrt(jnp.sum(qf * qf, axis=0, keepdims=True))
        k_row = jnp.concatenate([k_max[h:h + 1]] * (tq // LANES), axis=1)
        bounds.append(q_norm * (k_row * 1.001) + 1e-3)
    worst = bounds[0]
    for bd in bounds[1:]:
        worst = jnp.maximum(worst, bd)
    safe = jnp.max(worst) <= SHIFT_LIMIT

    def shifted_step(c, masked):
        base = pl.multiple_of(c * tk, tk)
        old = [acc_ref[h] for h in heads]
        ps = []
        for h in heads:
            s = _dot(keys(base, h), qmt_ref[0, h * LANES:(h + 1) * LANES, :]) - bounds[h]
            ps.append(jnp.exp2(jnp.where(base + key_r <= t_k, s, NEG) if masked else s).astype(BF16))
        new = [old[h] + _dot(jnp.concatenate([vt_ref[0, c, h * MLA_V:(h + 1) * MLA_V, :], ones], axis=0), ps[h])
               for h in heads]
        for h in heads:
            acc_ref[h] = new[h]
        return 0

    def online_step(c, masked):
        base = pl.multiple_of(c * tk, tk)
        old = [(m_ref[h], l_ref[h], acc_ref[h, 0:MLA_V]) for h in heads]
        scores = []
        for h in heads:
            s = _dot(keys(base, h), qmt_ref[0, h * LANES:(h + 1) * LANES, :])
            scores.append(jnp.where(base + key_r <= t_k, s, NEG) if masked else s)
        new = [_online_update(scores[h], vt_ref[0, c, h * MLA_V:(h + 1) * MLA_V, :], *old[h]) for h in heads]
        for h in heads:
            m_ref[h], l_ref[h], acc_ref[h, 0:MLA_V] = new[h]
        return 0

    def run(step):
        lax.fori_loop(0, n_past, lambda c, z: step(c, False), 0)
        lax.fori_loop(n_past, n_chunks, lambda c, z: step(c, True), 0)

    def shifted():
        run(shifted_step)
        return tuple(acc_ref[h, 0:MLA_V] / acc_ref[h, MLA_V:MLA_V + 1] for h in heads)

    def online():
        run(online_step)
        return tuple(acc_ref[h, 0:MLA_V] / l_ref[h] for h in heads)

    outs = lax.cond(safe, shifted, online)
    o_ref[0] = jnp.concatenate(outs, axis=0).T


def _mla_prompt(p, wts, b, t):
    tq, tk = MLA_TQ, MLA_TK
    assert t % tq == 0
    return pl.pallas_call(
        _mla_prompt_kernel,
        grid=(b, t // tq),
        in_specs=[pl.BlockSpec((1, MLA_HEADS * LANES, tq), lambda bb, i: (bb, 0, i)),
                  pl.BlockSpec((1, t, MLA_HEADS * LANES), lambda bb, i: (bb, 0, 0)),
                  pl.BlockSpec((1, t // tk, MLA_WIDTH, tk), lambda bb, i: (bb, 0, 0, 0)),
                  pl.BlockSpec((1,) + p["kn"].shape[1:], lambda bb, i: (bb, 0, 0))],
        out_specs=pl.BlockSpec((1, tq, MLA_WIDTH), lambda bb, i: (bb, i, 0)),
        out_shape=jax.ShapeDtypeStruct((b, t, MLA_WIDTH), F32),
        scratch_shapes=[pltpu.VMEM((MLA_HEADS, 1, tq), F32), pltpu.VMEM((MLA_HEADS, 1, tq), F32),
                        pltpu.VMEM((MLA_HEADS, MLA_V + PAD_ROWS, tq), F32)],
        compiler_params=pltpu.CompilerParams(dimension_semantics=("arbitrary",) * 2, vmem_limit_bytes=VMEM_LIMIT),
        name="mla_prompt",
    )(p["qmt"], p["kmla"], p["vmt"], p["kn"])


def _merge_kernel(from_latent, x_ref, oa_ref, ob_ref, ga_ref, gb_ref, ma_ref, mb_ref, wpa_ref, wpb_ref, wout_ref,
                  npost_ref, wuv_ref, y_ref):
    if from_latent:
        lat = ob_ref[0].astype(BF16)
        parts = []
        for j in range(MLA_HEADS // 2):
            parts.append(_dot(lat[:, 2 * j * MLA_KV_LORA:(2 * j + 1) * MLA_KV_LORA], wuv_ref[2 * j])
                         + _dot(lat[:, (2 * j + 1) * MLA_KV_LORA:(2 * j + 2) * MLA_KV_LORA], wuv_ref[2 * j + 1]))
        o_b = jnp.concatenate(parts, axis=1)
    else:
        o_b = ob_ref[0]
    ga = ga_ref[0]
    gb = gb_ref[0]
    pa = _dot((oa_ref[0] * (ga * jax.nn.sigmoid(ga))).astype(BF16), wpa_ref[...])
    pb = _dot((o_b * (gb * jax.nn.sigmoid(gb))).astype(BF16), wpb_ref[...])
    h = jax.nn.sigmoid(ma_ref[0]) * pa + jax.nn.sigmoid(mb_ref[0]) * pb
    z = _dot(h.astype(BF16), wout_ref[...])
    y_ref[0] = x_ref[0] + _rms(z, npost_ref[...])


def _merge(x3, o_a, o_b, p, wts, tm, from_latent):
    b, t, _ = x3.shape
    tok = lambda w: pl.BlockSpec((1, tm, w), lambda bb, i: (bb, i, 0))
    return pl.pallas_call(
        functools.partial(_merge_kernel, from_latent),
        grid=(b, t // tm),
        in_specs=[tok(D_MODEL), tok(NSA_WIDTH), tok(o_b.shape[2]), tok(NSA_WIDTH), tok(MLA_WIDTH), tok(D_MODEL),
                  tok(D_MODEL), _full_spec((NSA_WIDTH, D_MODEL)), _full_spec((MLA_WIDTH, D_MODEL)),
                  _full_spec((D_MODEL, D_MODEL)), _full_spec((1, D_MODEL)),
                  _full_spec((MLA_HEADS, MLA_KV_LORA, LANES))],
        out_specs=tok(D_MODEL),
        out_shape=jax.ShapeDtypeStruct((b, t, D_MODEL), F32),
        compiler_params=pltpu.CompilerParams(dimension_semantics=("arbitrary",) * 2, vmem_limit_bytes=VMEM_LIMIT),
        name="merge",
    )(x3, o_a, o_b, p["ga"], p["gb"], p["ma"], p["mb"], wts["wpa"], wts["wpb"], wts["wout"], wts["npost"],
      wts["wuv"])


PAGES_PER_STEP = 32
POOL_PAGES = 4


def _page_copy(source, buf_ref, sem_ref, step, slot, k):
    return pltpu.make_async_copy(source(step, k), buf_ref.at[slot, k], sem_ref.at[slot])


def _table_pages(pt_ref, cache_ref):
    return lambda step, k: cache_ref.at[pt_ref[step * PAGES_PER_STEP + k]]


def _paged_pipeline(streams):
    step = pl.program_id(0) * pl.num_programs(1) + pl.program_id(1)
    last = pl.num_programs(0) * pl.num_programs(1) - 1
    slot = step % 2

    @pl.when(step == 0)
    def _():
        for source, buf_ref, sem_ref in streams:
            for k in range(buf_ref.shape[1]):
                _page_copy(source, buf_ref, sem_ref, 0, 0, k).start()

    @pl.when(step < last)
    def _():
        for source, buf_ref, sem_ref in streams:
            for k in range(buf_ref.shape[1]):
                _page_copy(source, buf_ref, sem_ref, step + 1, 1 - slot, k).start()

    for source, buf_ref, sem_ref in streams:
        for k in range(buf_ref.shape[1]):
            _page_copy(source, buf_ref, sem_ref, step, slot, k).wait()
    return slot


XLU_POOL_PAGES = 16
MXU_POOL_GROUP = 8


def _compress_pages_kernel(pt_ref, cache_ref, pool_ref, pe_ref, o_ref, buf_ref, sem_ref):
    slot = _paged_pipeline([(_table_pages(pt_ref, cache_ref), buf_ref, sem_ref)])
    pe_sum = jnp.sum(pe_ref[...], axis=0, keepdims=True)
    pool = pool_ref[...]
    groups = [jnp.concatenate([buf_ref[slot, k + j] for j in range(MXU_POOL_GROUP)], axis=1)
              for k in range(XLU_POOL_PAGES, PAGES_PER_STEP, MXU_POOL_GROUP)]
    splits = [_split_bf16(x) for x in groups]
    by_mxu = [_dot_nt(pool, hi) + _dot_nt(pool, lo) for hi, lo in splits]
    by_xlu = []
    for k in range(XLU_POOL_PAGES):
        rows = buf_ref[slot, k].T
        by_xlu.append(jnp.sum(rows.reshape(rows.shape[0] // CMP_BLOCK, CMP_BLOCK, KV_WIDTH), axis=1))
    o_ref[0] = (jnp.concatenate(by_xlu + by_mxu, axis=0) + pe_sum) * (1.0 / CMP_BLOCK)


def _compress_pages(cache_t, pt_flat, pe, b, n_pages):
    page = cache_t.shape[2]
    per_step = PAGES_PER_STEP * page // CMP_BLOCK
    pool = np.zeros((MXU_POOL_GROUP * page // CMP_BLOCK, MXU_POOL_GROUP * page), np.float32)
    for s in range(pool.shape[1]):
        pool[s // CMP_BLOCK, s] = 1.0
    gs = pltpu.PrefetchScalarGridSpec(
        num_scalar_prefetch=1, grid=(b, n_pages // PAGES_PER_STEP),
        in_specs=[pl.BlockSpec(memory_space=pl.ANY),
                  pl.BlockSpec(pool.shape, lambda bb, c, pt: (0, 0)),
                  pl.BlockSpec((CMP_BLOCK, KV_WIDTH), lambda bb, c, pt: (0, 0))],
        out_specs=pl.BlockSpec((1, per_step, KV_WIDTH), lambda bb, c, pt: (bb, c, 0)),
        scratch_shapes=[pltpu.VMEM((2, PAGES_PER_STEP, KV_WIDTH, page), F32), pltpu.SemaphoreType.DMA((2,))])
    return pl.pallas_call(
        _compress_pages_kernel, grid_spec=gs,
        out_shape=jax.ShapeDtypeStruct((b, n_pages * page // CMP_BLOCK, KV_WIDTH), F32),
        compiler_params=pltpu.CompilerParams(dimension_semantics=("arbitrary",) * 2, vmem_limit_bytes=VMEM_LIMIT),
        name="compress_pages",
    )(pt_flat, cache_t, jnp.asarray(pool, BF16), pe)


def _pad_rows(v, rows):
    return jnp.concatenate([v, jnp.zeros((rows - v.shape[0], v.shape[1]), v.dtype)], axis=0)


def _stack_heads(qv, lo):
    a, b = qv[:, :LANES], qv[:, LANES:]
    z = jnp.zeros_like(a)
    return jnp.concatenate([jnp.where(lo, a, z), jnp.where(lo, z, a),
                            jnp.where(lo, b, z), jnp.where(lo, z, b)], axis=0)


def _sample_select_kernel(n_cmp, q_ref, kc_ref, oc_ref, idx_ref, imp_ref):
    bb = pl.program_id(0)
    lo1 = _lane_lo(1)
    lo_c = _lane_lo(n_cmp)
    q = q_ref[0]
    kc = kc_ref[0]
    even = (lax.broadcasted_iota(jnp.int32, (1, LANES), 1) & 1) == 0
    for g in range(NSA_KV_HEADS):
        kk, vv = _dup_kv(kc[:, g * LANES:(g + 1) * LANES], lo_c)
        qs = _pad_rows(_stack_heads(q[:, g * 2 * LANES:(g + 1) * 2 * LANES], lo1), PAD_ROWS)
        s = _dot_nt(qs, kk.astype(BF16))
        e = jnp.exp(s - jnp.max(s, axis=-1, keepdims=True))
        p = e / jnp.sum(e, axis=-1, keepdims=True)
        oc_ref[0, g] = _dot(p.astype(BF16), vv.astype(BF16))
        imp = p[0:1] + p[1:2] + p[2:3] + p[3:4]
        chunks = []
        for k in range(n_cmp // LANES):
            a = imp[:, k * LANES:(k + 1) * LANES]
            chunks.append(a + jnp.where(even, pltpu.roll(a, LANES - 1, 1), pltpu.roll(a, 1, 1)))
        imp_ref[pl.ds(bb * NSA_KV_HEADS + g, 1), :] = jnp.concatenate(chunks, axis=1)

    @pl.when(bb == pl.num_programs(0) - 1)
    def _():
        rows = imp_ref.shape[0]
        blk = lax.broadcasted_iota(jnp.int32, (rows, n_cmp), 1) >> 1
        blk_f = blk.astype(F32)
        slot = lax.broadcasted_iota(jnp.int32, (rows, N_SELECT), 1)
        v = jnp.where(blk == 0, -1.0, imp_ref[...])
        idx = jnp.where(slot == N_SELECT - 1, n_cmp // 2, 0)
        for k in range(1, N_SELECT - 1):
            top = jnp.max(v, axis=-1, keepdims=True)
            jmin = jnp.min(jnp.where(v == top, blk_f, float(n_cmp)), axis=-1, keepdims=True).astype(jnp.int32)
            idx = jnp.where(slot == k, jmin, idx)
            v = jnp.where(blk == jmin, -1.0, v)
        idx_ref[...] = idx


def _sample_select(q, kc_all, b):
    n_cmp = kc_all.shape[1]
    rows = b * NSA_KV_HEADS
    return pl.pallas_call(
        functools.partial(_sample_select_kernel, n_cmp),
        grid=(b,),
        in_specs=[pl.BlockSpec((1, 1, NSA_WIDTH), lambda bb: (bb, 0, 0)),
                  pl.BlockSpec((1, n_cmp, KV_WIDTH), lambda bb: (bb, 0, 0))],
        out_specs=[pl.BlockSpec((1, NSA_KV_HEADS, PAD_ROWS, LANES), lambda bb: (bb, 0, 0, 0)),
                   pl.BlockSpec((rows, N_SELECT), lambda bb: (0, 0))],
        out_shape=[jax.ShapeDtypeStruct((b, NSA_KV_HEADS, PAD_ROWS, LANES), F32),
                   jax.ShapeDtypeStruct((rows, N_SELECT), jnp.int32)],
        scratch_shapes=[pltpu.VMEM((rows, n_cmp), F32)],
        compiler_params=pltpu.CompilerParams(dimension_semantics=("arbitrary",), vmem_limit_bytes=VMEM_LIMIT),
        name="sample_select",
    )(q, kc_all)


def _extra_key_softmax(s_past, vt4_b, s_new, v_new):
    m = jnp.maximum(jnp.max(s_past, axis=-1, keepdims=True), s_new)
    e = jnp.exp(s_past - m)
    e_new = jnp.exp(s_new - m)
    den = jnp.sum(e, axis=-1, keepdims=True) + e_new
    return (_dot_nt(e.astype(BF16), vt4_b) + e_new * v_new) / den


def _sample_attend_kernel(n_pages, idx_ref, pt_ref, slc_ref, qr_ref, newkv_ref, neww_ref, newwf_ref, win_ref,
                          gate_ref, oc_ref, o_ref, wout_ref, buf_ref, sem_ref):
    n_sel = N_SELECT
    bb, g = pl.program_id(0), pl.program_id(1)

    def selected_page(step, k):
        j = idx_ref[step * n_sel + k]
        page = pt_ref[(step // NSA_KV_HEADS) * n_pages + jnp.minimum(j // 2, n_pages - 1)]
        return slc_ref.at[page, pl.ds((step % NSA_KV_HEADS) * LANES, LANES), :]

    slot = _paged_pipeline([(selected_page, buf_ref, sem_ref)])
    pages = [buf_ref.at[slot, k] for k in range(n_sel)]
    lo1 = _lane_lo(1)
    q = qr_ref[0].astype(F32)
    halves = [q[:, 0:LANES], q[:, LANES:2 * LANES]]
    rows = []
    for r in range(NSA_GROUP):
        a = halves[r // 2]
        rows.append(jnp.where(lo1, a if r % 2 == 0 else pltpu.roll(a, HALF, 1), 0.0))
    qs_f = _pad_rows(jnp.concatenate(rows, axis=0), PAD_ROWS)
    qs = qs_f.astype(BF16)
    twice_rows = lambda a: jnp.concatenate([a, a], axis=0)

    s_t = jnp.concatenate([pg[...] for pg in pages], axis=1)
    s_sel = _dot(qs, s_t.astype(BF16))
    base = (bb * NSA_KV_HEADS + g) * n_sel
    biases = []
    for k in range(n_sel - 1):
        odd = (idx_ref[base + k] & 1) == 1
        biases.append(jnp.where(lo1, jnp.where(odd, NEG, 0.0), jnp.where(odd, 0.0, NEG)))
    biases.append(jnp.full((1, LANES), NEG, F32))
    s_sel = s_sel + jnp.concatenate(biases, axis=1)
    nk = newkv_ref[0].astype(F32)
    s_new = jnp.sum(qs_f * nk[:, :LANES], axis=-1, keepdims=True)
    o_s = _extra_key_softmax(s_sel, twice_rows(s_t[HD:2 * HD]).astype(BF16), s_new, nk[:, LANES:])

    w = win_ref[0]
    nw = neww_ref[0].astype(F32)
    s_w = _dot(qs, w.astype(BF16))
    s_wn = jnp.sum(qs_f * nw[:, :LANES], axis=-1, keepdims=True)
    o_w = _extra_key_softmax(s_w, twice_rows(w[HD:2 * HD]).astype(BF16), s_wn, nw[:, LANES:])

    o_c = oc_ref[0, 0]
    gates = gate_ref[0]
    gate = jnp.where(g == 0, gates[:, 0:GATE_ROWS], gates[:, GATE_ROWS:2 * GATE_ROWS])
    heads = []
    for r in range(NSA_GROUP):
        heads.append(gate[:, 3 * r:3 * r + 1] * o_c[r:r + 1] + gate[:, 3 * r + 1:3 * r + 2] * o_s[r:r + 1]
                     + gate[:, 3 * r + 2:3 * r + 3] * o_w[r:r + 1])
    o_ref[0] = jnp.concatenate([jnp.where(lo1, heads[0], heads[1]), jnp.where(lo1, heads[2], heads[3])], axis=1)

    n_feat, n_w = w.shape
    new_row = jnp.broadcast_to(newwf_ref[0], (n_feat, n_feat))
    diag = (lax.broadcasted_iota(jnp.int32, (n_feat, n_feat), 0)
            == lax.broadcasted_iota(jnp.int32, (n_feat, n_feat), 1))
    new_col = jnp.sum(jnp.where(diag, new_row, 0.0), axis=1, keepdims=True)
    last = lax.broadcasted_iota(jnp.int32, (n_feat, LANES), 1) == LANES - 1
    chunks = []
    n_ch = n_w // LANES
    for c in range(n_ch):
        cur = pltpu.roll(w[:, c * LANES:(c + 1) * LANES], LANES - 1, 1)
        if c + 1 < n_ch:
            nxt = pltpu.roll(w[:, (c + 1) * LANES:(c + 2) * LANES], LANES - 1, 1)
        else:
            nxt = jnp.broadcast_to(new_col, (n_feat, LANES))
        chunks.append(jnp.where(last, nxt, cur))
    wout_ref[0] = jnp.concatenate(chunks, axis=1)


def _sample_attend(p, oc, idx_flat, pt_flat, slc_t, win_t, b, n_pages):
    page = slc_t.shape[2]
    win_len = win_t.shape[2]
    assert page == 2 * SEL_BLOCK
    row = lambda w: pl.BlockSpec((1, 1, w), lambda bb, g, idx, pt: (bb, 0, g))
    wspec = pl.BlockSpec((1, LANES, win_len), lambda bb, g, idx, pt: (bb, g, 0))
    gs = pltpu.PrefetchScalarGridSpec(
        num_scalar_prefetch=2, grid=(b, NSA_KV_HEADS),
        in_specs=[pl.BlockSpec(memory_space=pl.ANY),
                  row(2 * LANES), row(2 * LANES), row(2 * LANES), row(LANES), wspec,
                  pl.BlockSpec((1, 1, NSA_KV_HEADS * GATE_ROWS), lambda bb, g, idx, pt: (bb, 0, 0)),
                  pl.BlockSpec((1, 1, PAD_ROWS, LANES), lambda bb, g, idx, pt: (bb, g, 0, 0))],
        out_specs=[row(2 * LANES), wspec],
        scratch_shapes=[pltpu.VMEM((2, N_SELECT, LANES, page), F32), pltpu.SemaphoreType.DMA((2,))])
    return pl.pallas_call(
        functools.partial(_sample_attend_kernel, n_pages), grid_spec=gs,
        out_shape=[jax.ShapeDtypeStruct((b, 1, NSA_WIDTH), F32),
                   jax.ShapeDtypeStruct((b, KV_WIDTH, win_len), F32)],
        compiler_params=pltpu.CompilerParams(dimension_semantics=("arbitrary",) * 2, vmem_limit_bytes=VMEM_LIMIT),
        name="sample_attend",
    )(idx_flat, pt_flat, slc_t, p["qrot"], p["slckv"], p["winkv"], p["win"], win_t, p["gate"], oc)


def _softmax_update(s, v_b, m, l, acc):
    m_new = jnp.maximum(m, jnp.max(s, axis=-1, keepdims=True))
    p = jnp.exp(s - m_new)
    alpha = jnp.exp(m - m_new)
    return m_new, alpha * l + jnp.sum(p, axis=-1, keepdims=True), alpha * acc + _dot(p.astype(BF16), v_b)


def _mla_sample_kernel(pt_ref, lat_ref, krt_ref, qlat_ref, qpe_ref, cnew_ref, krnew_ref, o_ref,
                       m_ref, l_ref, acc_ref, lat_buf, kr_buf, lat_sem, kr_sem):
    n = PAGES_PER_STEP
    slot = _paged_pipeline([(_table_pages(pt_ref, lat_ref), lat_buf, lat_sem),
                            (_table_pages(pt_ref, krt_ref), kr_buf, kr_sem)])
    lat_pages = [lat_buf.at[slot, k] for k in range(n)]
    kr_pages = [kr_buf.at[slot, k] for k in range(n)]
    step = pl.program_id(1)
    qlat = _pad_rows(qlat_ref[0], PAD_ROWS)
    qpe = _pad_rows(qpe_ref[0], PAD_ROWS)

    @pl.when(step == 0)
    def _():
        c_new = cnew_ref[0]
        s_new = (jnp.sum(qlat * c_new, axis=-1, keepdims=True)
                 + jnp.sum(qpe * krnew_ref[0], axis=-1, keepdims=True))
        m_ref[...] = s_new
        l_ref[...] = jnp.ones(l_ref.shape, F32)
        acc_ref[...] = jnp.broadcast_to(c_new, acc_ref.shape)

    qlat_b, qpe_b = qlat.astype(BF16), qpe.astype(BF16)
    groups = range(0, n, POOL_PAGES)
    c_bs = [jnp.concatenate([pg[...] for pg in lat_pages[k:k + POOL_PAGES]], axis=0).astype(BF16) for k in groups]
    kr_bs = [jnp.concatenate([pg[...] for pg in kr_pages[k:k + POOL_PAGES]], axis=1).astype(BF16) for k in groups]
    s_lat = [_dot_nt(qlat_b, c_b) for c_b in c_bs]
    s_pe = [_dot(qpe_b, kr_b) for kr_b in kr_bs]
    scores = [a + r for a, r in zip(s_lat, s_pe)]
    maxes = [jnp.max(s, axis=-1, keepdims=True) for s in scores]
    ps = [jnp.exp(s - mx) for s, mx in zip(scores, maxes)]
    sums = [jnp.sum(p_g, axis=-1, keepdims=True) for p_g in ps]
    accs = [_dot(p_g.astype(BF16), c_b) for p_g, c_b in zip(ps, c_bs)]
    m_old = m_ref[...]
    m = m_old
    for mx in maxes:
        m = jnp.maximum(m, mx)
    alpha = jnp.exp(m_old - m)
    l = alpha * l_ref[...]
    acc = alpha * acc_ref[...]
    for mx, l_g, acc_g in zip(maxes, sums, accs):
        w = jnp.exp(mx - m)
        l = l + w * l_g
        acc = acc + w * acc_g
    m_ref[...] = m
    l_ref[...] = l
    acc_ref[...] = acc

    @pl.when(step == pl.num_programs(1) - 1)
    def _():
        o_ref[0] = (acc / l)[0:MLA_HEADS]


def _mla_sample(p, lat_cache, kr_t, pt_flat, b, n_pages):
    page = lat_cache.shape[1]
    head = lambda w: pl.BlockSpec((1, MLA_HEADS, w), lambda bb, c, pt: (bb, 0, 0))
    row = lambda w: pl.BlockSpec((1, 1, w), lambda bb, c, pt: (bb, 0, 0))
    gs = pltpu.PrefetchScalarGridSpec(
        num_scalar_prefetch=1, grid=(b, n_pages // PAGES_PER_STEP),
        in_specs=[pl.BlockSpec(memory_space=pl.ANY), pl.BlockSpec(memory_space=pl.ANY),
                  head(MLA_KV_LORA), head(MLA_ROPE), row(MLA_KV_LORA), row(MLA_ROPE)],
        out_specs=head(MLA_KV_LORA),
        scratch_shapes=[pltpu.VMEM((PAD_ROWS, 1), F32), pltpu.VMEM((PAD_ROWS, 1), F32),
                        pltpu.VMEM((PAD_ROWS, MLA_KV_LORA), F32),
                        pltpu.VMEM((2, PAGES_PER_STEP, page, MLA_KV_LORA), F32),
                        pltpu.VMEM((2, PAGES_PER_STEP, MLA_ROPE, page), F32),
                        pltpu.SemaphoreType.DMA((2,)), pltpu.SemaphoreType.DMA((2,))])
    qlat = p["qlat"].reshape(b, MLA_HEADS, MLA_KV_LORA)
    qpe = p["qpe"].reshape(b, MLA_HEADS, MLA_ROPE)
    return pl.pallas_call(
        _mla_sample_kernel, grid_spec=gs,
        out_shape=jax.ShapeDtypeStruct((b, MLA_HEADS, MLA_KV_LORA), F32),
        compiler_params=pltpu.CompilerParams(dimension_semantics=("arbitrary",) * 2, vmem_limit_bytes=VMEM_LIMIT),
        name="mla_sample",
    )(pt_flat, lat_cache, kr_t, qlat, qpe, p["c"], p["kr"])


def _rope_angles(pos, theta, dim):
    half = dim // 2
    inv = 1.0 / (float(theta) ** (np.arange(half, dtype=np.float64) / half))
    ang = np.asarray(pos, np.float64)[:, None] * inv[None, :]
    return np.cos(ang).astype(np.float32), np.sin(ang).astype(np.float32)


def _rope_lane_tables(pos, rows):
    def table(theta, dim, period, active):
        half = dim // 2
        cos, sin = _rope_angles(pos, theta, dim)
        lane = np.arange(LANES)
        d = lane % period
        is_lo = (d < half) & active(lane)
        is_hi = (d >= half) & (d < dim) & active(lane)
        fi = np.where(d < half, d, np.clip(d - half, 0, half - 1))
        cos_l, sin_l = cos[:, fi], sin[:, fi]
        tab = np.stack([np.where(is_lo | is_hi, cos_l, 1.0), np.where(is_lo, -sin_l, 0.0),
                        np.where(is_hi, sin_l, 0.0)]).astype(np.float32)
        return jnp.asarray(np.broadcast_to(tab, (3, rows, LANES)))

    every = lambda lane: np.ones_like(lane, bool)
    keys_only = lambda lane: (lane % LANES) < HD
    return (table(ROPE_THETA, ROT_DIM, HD, every), table(ROPE_THETA, ROT_DIM, HD, keys_only),
            table(MLA_ROPE_THETA, MLA_ROPE, MLA_ROPE, every))


def _rope_row_tables(pos):
    cq, sq = _rope_angles(pos, ROPE_THETA, ROT_DIM)
    cm, sm = _rope_angles(pos, MLA_ROPE_THETA, MLA_ROPE)
    return jnp.asarray(np.stack([cq.T, sq.T])), jnp.asarray(np.stack([cm.T, sm.T]))


def _pack_weights(l, norm_pre, w_in, pe_cmp, q_norm, w_q_up, kv_norm, w_kv_up, w_proj_a, w_proj_b, w_out, norm_post):
    w = w_in[l].astype(BF16)
    o = IN_OFFSETS
    seg = lambda k: w[:, o[k]:o[k + 1]]
    gn = seg(4)
    per_group = 3 * NSA_GROUP
    gn_t = jnp.zeros((NSA_KV_HEADS * GATE_ROWS, D_MODEL), w.dtype)
    for g in range(NSA_KV_HEADS):
        gn_t = gn_t.at[g * GATE_ROWS:g * GATE_ROWS + per_group].set(gn[:, g * per_group:(g + 1) * per_group].T)
    w_t = jnp.concatenate([seg(0).T, seg(1).T, seg(2).T, seg(3).T, gn_t, seg(8).T], axis=0)
    w_krp4 = jnp.tile(seg(8).T, (LANES // MLA_ROPE, 1))
    w_cols = jnp.concatenate([seg(5), seg(6), seg(7), seg(9), seg(10), seg(11)], axis=1)
    wq = w_q_up[l]
    wqup = jnp.concatenate([wq[..., :MLA_NOPE].reshape(MLA_Q_LORA, -1), wq[..., MLA_NOPE:].reshape(MLA_Q_LORA, -1)],
                           axis=1)
    wkv = w_kv_up[l]
    wuk_pad = jnp.pad(wkv[..., :MLA_NOPE], ((0, 0), (0, 0), (0, LANES - MLA_NOPE)))
    rope_copy = jnp.pad(jnp.eye(MLA_ROPE, dtype=w.dtype), ((0, LANES - MLA_ROPE), (MLA_NOPE, MLA_ROPE)))
    wkx = jnp.concatenate([wuk_pad.reshape(MLA_KV_LORA, MLA_HEADS * LANES), jnp.tile(rope_copy, (1, MLA_HEADS))],
                          axis=0)
    wvt = jnp.transpose(wkv[..., MLA_NOPE:], (1, 2, 0)).reshape(MLA_WIDTH, MLA_KV_LORA)
    w2uk =jnp.transpose(wkv[..., :MLA_NOPE], (1, 2, 0)).reshape(MLA_HEADS // 2, LANES, MLA_KV_LORA)
    wv = jnp.transpose(wkv[..., MLA_NOPE:], (1, 0, 2))
    zeros = jnp.zeros_like(wv)
    even = (jnp.arange(MLA_HEADS) % 2 == 0)[:, None, None]
    wuv = jnp.concatenate([jnp.where(even, wv, zeros), jnp.where(even, zeros, wv)], axis=2)
    return {
        "npre": norm_pre[l][None].astype(F32), "w_t": w_t, "w_krp4": w_krp4, "w_cols": w_cols, "pe": pe_cmp[l].reshape(CMP_BLOCK, KV_WIDTH).astype(F32),
        "qnorm": q_norm[l][None].astype(F32), "wqup": wqup.astype(BF16), "wqupt": wqup.T.astype(BF16),
        "kvnorm": kv_norm[l][None].astype(F32), "w2uk": w2uk.astype(BF16), "wkx": wkx.astype(BF16),
        "wvt": wvt.astype(BF16), "wuv": wuv.astype(BF16),
        "wpa": w_proj_a[l].astype(BF16), "wpb": w_proj_b[l].astype(BF16), "wout": w_out[l].astype(BF16),
        "npost": norm_post[l][None].astype(F32),
    }


def _rows_from_cols(a):
    b, _, t = a.shape
    return a.reshape(b, NSA_KV_HEADS, 2, HD, t).transpose(0, 4, 1, 2, 3)


def _cols_from_rows(a):
    n, t = a.shape[:2]
    return a.transpose(0, 2, 3, 4, 1).reshape(n, KV_WIDTH, t)


def _prompt_layer(x, wts):
    b, t, _ = x.shape
    assert t % PROMPT_TM == 0
    p = _in_project_cols(x, _rope_row_tables(np.arange(t)), wts)
    o_a = _nsa_prompt(p, b, t)
    o_b = _mla_prompt(p, wts, b, t)
    y = _merge(x, o_a, o_b, p, wts, 512, from_latent=False)
    win_keep = min(WINDOW, t)
    return y, (_rows_from_cols(p["cmp"]), _rows_from_cols(p["slc"]), p["c"], p["kr"].transpose(0, 2, 1),
               _rows_from_cols(p["win"][:, :, t - win_keep:]))


def _sample_layer(x, l, caches, state_win, page_table, wts):
    cache_cmp, cache_slc, cache_lat, cache_kr = caches
    b, s_new, _ = x.shape
    assert s_new == 1
    n_pages = page_table.shape[1]
    page = cache_cmp.shape[2]
    past = n_pages * page
    assert past % SEL_BLOCK == 0 and n_pages % PAGES_PER_STEP == 0 and page == LANES
    win_len = state_win.shape[1]
    assert win_len == WINDOW and past >= WINDOW
    pt_flat = page_table.reshape(-1).astype(jnp.int32)

    tabs = _rope_lane_tables(np.full((1,), past), b)
    p = _in_project_rows(x.reshape(1, b, D_MODEL), tabs, wts)
    p = {k: v.reshape(b, 1, v.shape[-1]) for k, v in p.items()}

    kc_all = _compress_pages(_cols_from_rows(cache_cmp[l]), pt_flat, wts["pe"], b, n_pages)
    oc, idx = _sample_select(p["q"], kc_all, b)
    o_a, new_win = _sample_attend(p, oc, idx.reshape(-1), pt_flat, _cols_from_rows(cache_slc[l]),
                                  _cols_from_rows(state_win), b, n_pages)
    o_lat = _mla_sample(p, cache_lat[l], cache_kr[l].transpose(0, 2, 1), pt_flat, b, n_pages)
    pm = {k: p[k].reshape(1, b, -1) for k in ("ga", "gb", "ma", "mb")}
    y = _merge(x.reshape(1, b, D_MODEL), o_a.reshape(1, b, NSA_WIDTH),
               o_lat.reshape(1, b, MLA_HEADS * MLA_KV_LORA), pm, wts, b, from_latent=True)
    kv6 = lambda a: a.reshape(b, 1, NSA_KV_HEADS, 2, HD)
    return y.reshape(b, 1, D_MODEL), (kv6(p["cmp"]), kv6(p["slc"]), p["c"], p["kr"], _rows_from_cols(new_win))


def kernel(x_prompt, x_sample, cache_nsa_cmp, cache_nsa_slc, cache_mla_latent, cache_mla_krope, state_nsa_win,
           page_table, norm_pre, w_in, pe_cmp, q_norm, w_q_up, kv_norm, w_kv_up, w_proj_a, w_proj_b, w_out,
           norm_post):
    depth = w_in.shape[0]
    hp, hs = x_prompt, x_sample
    new_p, new_s = [], []
    for l in range(depth):
        wts = _pack_weights(l, norm_pre, w_in, pe_cmp, q_norm, w_q_up, kv_norm, w_kv_up, w_proj_a, w_proj_b,
                            w_out, norm_post)
        hp, sp = _prompt_layer(hp, wts)
        hs, ss = _sample_layer(hs, l, (cache_nsa_cmp, cache_nsa_slc, cache_mla_latent, cache_mla_krope),
                               state_nsa_win[l], page_table, wts)
        new_p.append(sp)
        new_s.append(ss)
    stack = lambda items, k: jnp.stack([s[k] for s in items])
    return (hp, hs) + tuple(stack(new_p, k) for k in range(5)) + tuple(stack(new_s, k) for k in range(5))
```

```python
import functools

import numpy as np
import jax
import jax.numpy as jnp
from jax import lax
from jax.experimental import pallas as pl
from jax.experimental.pallas import tpu as pltpu

D_MODEL = 1024
NSA_HEADS = 8
NSA_KV_HEADS = 2
NSA_GROUP = NSA_HEADS // NSA_KV_HEADS
HD = 64
NSA_WIDTH = NSA_HEADS * HD
KV_WIDTH = NSA_KV_HEADS * 2 * HD
ROT_DIM = HD // 4
ROPE_THETA = 500000.0
CMP_BLOCK = 32
SEL_BLOCK = 64
N_SELECT = 16
WINDOW = 512
NSA_SCALE = HD ** -0.5

MLA_HEADS = 8
MLA_Q_LORA = 384
MLA_KV_LORA = 256
MLA_NOPE = 64
MLA_ROPE = 32
MLA_V = 64
MLA_WIDTH = MLA_HEADS * MLA_V
MLA_ROPE_THETA = 10000.0
MLA_SCALE = (MLA_NOPE + MLA_ROPE) ** -0.5
LOG2E = 1.4426950408889634
SHIFT_LIMIT = 40.0

RMS_EPS = 1e-6
NEG = -1e30
FORCE_SCORE = 1e4

IN_SPLITS = (NSA_WIDTH, KV_WIDTH, KV_WIDTH, KV_WIDTH, 3 * NSA_HEADS, NSA_WIDTH,
             MLA_Q_LORA, MLA_KV_LORA, MLA_ROPE, MLA_WIDTH, D_MODEL, D_MODEL)
IN_OFFSETS = tuple(int(v) for v in np.cumsum((0,) + IN_SPLITS))

LANES = 128
HALF = LANES // 2
GATE_ROWS = 16
PAD_ROWS = 16

PT_Q = (0, 512)
PT_CMP = (512, 768)
PT_SLC = (768, 1024)
PT_WIN = (1024, 1280)
PT_GN = (1280, 1280 + NSA_KV_HEADS * GATE_ROWS)
PT_KRP = (PT_GN[1], PT_GN[1] + MLA_ROPE)
PT_ROWS = PT_KRP[1]
PR_GA = (0, 512)
PR_QD = (512, 896)
PR_KVD = (896, 1152)
PR_GB = (1152, 1664)
PR_MA = (1664, 2688)
PR_MB = (2688, 3712)
PR_COLS = 3712

PROMPT_TM = 512
ATT_TQ = 256
ATT_TK = 256
MLA_TK = 256

VMEM_LIMIT = 48 * 1024 * 1024
BF16 = jnp.bfloat16
F32 = jnp.float32


def _full_spec(shape):
    nd = len(shape)
    return pl.BlockSpec(shape, lambda *_: (0,) * nd)


def _lane_lo(rows):
    return lax.broadcasted_iota(jnp.int32, (rows, LANES), 1) < HALF


def _dot(a, b):
    return jnp.dot(a, b, preferred_element_type=F32)


def _dot_nt(a, b):
    return lax.dot_general(a, b, (((1,), (1,)), ((), ())), preferred_element_type=F32)


def _rms(v, gain):
    return v * lax.rsqrt(jnp.mean(v * v, axis=-1, keepdims=True) + RMS_EPS) * gain


def _split_bf16(v):
    hi = v.astype(BF16)
    return hi, (v - hi.astype(F32)).astype(BF16)


def _rope_tiles(v, tab_ref, shift):
    c, s_lo, s_hi = tab_ref[0], tab_ref[1], tab_ref[2]
    out = []
    for k in range(v.shape[1] // LANES):
        a = v[:, k * LANES:(k + 1) * LANES]
        out.append(a * c + pltpu.roll(a, LANES - shift, 1) * s_lo + pltpu.roll(a, shift, 1) * s_hi)
    return out[0] if len(out) == 1 else jnp.concatenate(out, axis=1)


def _dup_kv(a, lo):
    r = pltpu.roll(a, HALF, 1)
    return jnp.where(lo, a, r), jnp.where(lo, r, a)


def _kv_pack(v):
    lo = _lane_lo(v.shape[0])
    parts = []
    for g in range(NSA_KV_HEADS):
        kk, vv = _dup_kv(v[:, g * LANES:(g + 1) * LANES], lo)
        parts += [kk, vv]
    return jnp.concatenate(parts, axis=1).astype(BF16)


def _inproj_rows_kernel(x_ref, npre_ref, wt_ref, wkrp_ref, w_ref, tq_ref, tkv_ref, tm_ref, qnorm_ref, wqup_ref,
                        kvnorm_ref, w2uk_ref, q_ref, qrot_ref, gate_ref, cmp_ref, slc_ref, win_ref, slckv_ref,
                        winkv_ref, ga_ref, gb_ref, ma_ref, mb_ref, qpe_ref, c_ref, kr_ref, qlat_ref):
    xb = _rms(x_ref[0], npre_ref[...]).astype(BF16)
    segt = lambda lohi: _dot_nt(xb, wt_ref[lohi[0]:lohi[1], :])
    seg = lambda lohi: _dot(xb, w_ref[:, lohi[0]:lohi[1]])

    a = segt((PT_Q[0], PT_WIN[1]))
    q = a[:, :NSA_WIDTH]
    q_ref[0] = (q * NSA_SCALE).astype(BF16)
    qrot_ref[0] = (_rope_tiles(q, tq_ref, ROT_DIM // 2) * NSA_SCALE).astype(BF16)
    cmp_ref[0] = a[:, NSA_WIDTH:NSA_WIDTH + KV_WIDTH]
    kvs = _rope_tiles(a[:, NSA_WIDTH + KV_WIDTH:NSA_WIDTH + 2 * KV_WIDTH], tkv_ref, ROT_DIM // 2)
    slc_ref[0] = kvs
    slckv_ref[0] = _kv_pack(kvs)
    kvw = _rope_tiles(a[:, NSA_WIDTH + 2 * KV_WIDTH:], tkv_ref, ROT_DIM // 2)
    win_ref[0] = kvw
    winkv_ref[0] = _kv_pack(kvw)

    gate_ref[0] = jax.nn.sigmoid(segt(PT_GN))
    ga_ref[0] = seg(PR_GA).astype(BF16)
    gb_ref[0] = seg(PR_GB).astype(BF16)
    ma_ref[0] = seg(PR_MA).astype(BF16)
    mb_ref[0] = seg(PR_MB).astype(BF16)

    qd = _rms(seg(PR_QD), qnorm_ref[...]).astype(BF16)
    qh = _dot(qd, wqup_ref[...])
    qn = qh[:, :MLA_HEADS * MLA_NOPE].astype(BF16)
    qpe_ref[0] = _rope_tiles(qh[:, MLA_HEADS * MLA_NOPE:], tm_ref, MLA_ROPE // 2) * MLA_SCALE
    c_ref[0] = _rms(seg(PR_KVD), kvnorm_ref[...])
    kr_ref[0] = _rope_tiles(_dot_nt(xb, wkrp_ref[...]), tm_ref, MLA_ROPE // 2)[:, :MLA_ROPE]

    rows = qn.shape[0]
    lo = _lane_lo(rows)
    z = jnp.zeros((rows, LANES), BF16)
    parts = []
    for j in range(MLA_HEADS // 2):
        pair = qn[:, j * LANES:(j + 1) * LANES]
        parts.append(_dot(jnp.where(lo, pair, z), w2uk_ref[j]) * MLA_SCALE)
        parts.append(_dot(jnp.where(lo, z, pair), w2uk_ref[j]) * MLA_SCALE)
    qlat_ref[0] = jnp.concatenate(parts, axis=1)


def _in_project_rows(x3, tabs, wts):
    b, t, _ = x3.shape
    tq, tkv, tmla = tabs
    row = lambda w, dt: jax.ShapeDtypeStruct((b, t, w), dt)
    names = ["q", "qrot", "gate", "cmp", "slc", "win", "slckv", "winkv", "ga", "gb", "ma", "mb", "qpe", "c", "kr",
             "qlat"]
    out_shape = [row(512, BF16), row(512, BF16), row(NSA_KV_HEADS * GATE_ROWS, F32), row(256, F32), row(256, F32),
                 row(256, F32),
                 row(512, BF16), row(512, BF16), row(512, BF16), row(512, BF16), row(1024, BF16), row(1024, BF16),
                 row(MLA_HEADS * MLA_ROPE, F32), row(MLA_KV_LORA, F32), row(MLA_ROPE, F32),
                 row(MLA_HEADS * MLA_KV_LORA, F32)]
    tok = lambda w: pl.BlockSpec((1, t, w), lambda bb: (bb, 0, 0))
    tab = _full_spec((3, t, LANES))
    in_specs = [tok(D_MODEL), _full_spec((1, D_MODEL)), _full_spec((PT_ROWS, D_MODEL)), _full_spec((LANES, D_MODEL)),
                _full_spec((D_MODEL, PR_COLS)), tab, tab, tab, _full_spec((1, MLA_Q_LORA)), _full_spec((MLA_Q_LORA, MLA_HEADS * (MLA_NOPE + MLA_ROPE))),
                _full_spec((1, MLA_KV_LORA)), _full_spec((MLA_HEADS // 2, LANES, MLA_KV_LORA))]
    res = pl.pallas_call(
        _inproj_rows_kernel, grid=(b,), in_specs=in_specs, out_specs=[tok(s.shape[2]) for s in out_shape],
        out_shape=out_shape,
        compiler_params=pltpu.CompilerParams(dimension_semantics=("arbitrary",), vmem_limit_bytes=VMEM_LIMIT),
        name="in_project_rows",
    )(x3, wts["npre"], wts["w_t"], wts["w_krp4"], wts["w_cols"], tq, tkv, tmla, wts["qnorm"], wts["wqup"],
      wts["kvnorm"], wts["w2uk"])
    return dict(zip(names, res))


def _rope_rows(x, cos, sin, half):
    x1, x2 = x[0:half], x[half:2 * half]
    parts = [x1 * cos - x2 * sin, x1 * sin + x2 * cos]
    if x.shape[0] > 2 * half:
        parts.append(x[2 * half:])
    return jnp.concatenate(parts, axis=0)


def _store_chunks(ref, v):
    tk = ref.shape[3]
    for j in range(ref.shape[1]):
        ref[0, j] = v[:, j * tk:(j + 1) * tk]


def _inproj_cols_kernel(x_ref, npre_ref, wt_ref, w_ref, ropeq_ref, ropem_ref, pe_ref, qnorm_ref, wqupt_ref,
                        kvnorm_ref, pool_ref, wkx_ref, wvt_ref,
                        qt_ref, qrt_ref, gate_ref, cmp_ref, slc_ref, win_ref, slcb_ref, winb_ref, slcr_ref,
                        winr_ref, kc_ref, ga_ref, gb_ref, ma_ref, mb_ref, qmt_ref, c_ref, kmla_ref, vmt_ref,
                        kr_ref, kn_ref):
    tm = x_ref.shape[1]
    xb = _rms(x_ref[0], npre_ref[...]).astype(BF16)
    segt = lambda lohi: _dot_nt(wt_ref[lohi[0]:lohi[1], :], xb)
    seg = lambda lohi: _dot(xb, w_ref[:, lohi[0]:lohi[1]])
    cq, sq = ropeq_ref[0], ropeq_ref[1]
    cm, sm = ropem_ref[0], ropem_ref[1]
    hq, hm = ROT_DIM // 2, MLA_ROPE // 2

    qt = segt(PT_Q)
    qt_ref[0] = (qt * (NSA_SCALE * LOG2E)).astype(BF16)
    qrt = jnp.concatenate([_rope_rows(qt[h * HD:(h + 1) * HD], cq, sq, hq) for h in range(NSA_HEADS)], axis=0)
    qrt_ref[0] = (qrt * (NSA_SCALE * LOG2E)).astype(BF16)

    cmpt = segt(PT_CMP)
    cmp_ref[0] = cmpt
    hi, lo = _split_bf16(cmpt)
    pool = pool_ref[...]
    pooled = (_dot_nt(pool, hi) + _dot_nt(pool, lo))[0:tm // CMP_BLOCK]
    kc_ref[0] = (pooled + jnp.sum(pe_ref[...], axis=0, keepdims=True)) * (1.0 / CMP_BLOCK)

    def rope_kv(v):
        parts = []
        for g in range(NSA_KV_HEADS):
            parts.append(_rope_rows(v[g * LANES:g * LANES + HD], cq, sq, hq))
            parts.append(v[g * LANES + HD:(g + 1) * LANES])
        return jnp.concatenate(parts, axis=0)

    slct = rope_kv(segt(PT_SLC))
    slc_ref[0] = slct
    _store_chunks(slcb_ref, slct.astype(BF16))
    slcr_ref[0] = slct.T.astype(BF16)
    wint = rope_kv(segt(PT_WIN))
    win_ref[0] = wint
    _store_chunks(winb_ref, wint.astype(BF16))
    winr_ref[0] = wint.T.astype(BF16)

    gate_ref[0] = jax.nn.sigmoid(segt(PT_GN))
    krt = _rope_rows(segt(PT_KRP), cm, sm, hm)
    kr_ref[0] = krt
    c = _rms(seg(PR_KVD), kvnorm_ref[...])
    c_ref[0] = c
    c_b = c.astype(BF16)
    kr_rows = jnp.concatenate([krt, jnp.zeros((LANES - MLA_ROPE, tm), F32)], axis=0).T
    ckr = jnp.concatenate([c_b, kr_rows.astype(BF16)], axis=1)
    kmla = _dot(ckr, wkx_ref[...]).astype(BF16)
    kmla_ref[0] = kmla
    kf = kmla.astype(F32)
    norms = []
    for h in range(MLA_HEADS):
        n2 = jnp.sum(kf[:, h * LANES:(h + 1) * LANES] ** 2, axis=1, keepdims=True)
        norms.append(jnp.broadcast_to(jnp.sqrt(jnp.max(n2, axis=0, keepdims=True)), (1, LANES)))
    kn_ref[0] = jnp.concatenate(norms, axis=0)
    _store_chunks(vmt_ref, _dot_nt(wvt_ref[...], c_b).astype(BF16))

    qd = _rms(seg(PR_QD), qnorm_ref[...]).astype(BF16)
    qht = _dot_nt(wqupt_ref[...], qd)
    n_nope = MLA_HEADS * MLA_NOPE
    zq = jnp.zeros((LANES - MLA_NOPE - MLA_ROPE, tm), F32)
    parts = []
    for h in range(MLA_HEADS):
        parts += [qht[h * MLA_NOPE:(h + 1) * MLA_NOPE],
                  _rope_rows(qht[n_nope + h * MLA_ROPE:n_nope + (h + 1) * MLA_ROPE], cm, sm, hm), zq]
    qmt_ref[0] = (jnp.concatenate(parts, axis=0) * (MLA_SCALE * LOG2E)).astype(BF16)

    ga_ref[0] = seg(PR_GA).astype(BF16)
    gb_ref[0] = seg(PR_GB).astype(BF16)
    ma_ref[0] = seg(PR_MA).astype(BF16)
    mb_ref[0] = seg(PR_MB).astype(BF16)


def _in_project_cols(x3, ropes, wts):
    b, t, _ = x3.shape
    tm = PROMPT_TM
    nt = t // tm
    ropeq, ropem = ropes
    pool = np.zeros((16, tm), np.float32)
    for s in range(tm):
        pool[s // CMP_BLOCK, s] = 1.0
    sds = jax.ShapeDtypeStruct
    rows = lambda w: pl.BlockSpec((1, tm, w), lambda i, bb: (bb, i, 0))
    cols = lambda w: pl.BlockSpec((1, w, tm), lambda i, bb: (bb, 0, i))
    chunk = lambda w, tk: pl.BlockSpec((1, tm // tk, w, tk), lambda i, bb: (bb, i, 0, 0))
    outs = [
        ("qt", sds((b, NSA_WIDTH, t), BF16), cols(NSA_WIDTH)),
        ("qrt", sds((b, NSA_WIDTH, t), BF16), cols(NSA_WIDTH)),
        ("gate", sds((b, NSA_KV_HEADS * GATE_ROWS, t), F32), cols(NSA_KV_HEADS * GATE_ROWS)),
        ("cmp", sds((b, KV_WIDTH, t), F32), cols(KV_WIDTH)),
        ("slc", sds((b, KV_WIDTH, t), F32), cols(KV_WIDTH)),
        ("win", sds((b, KV_WIDTH, t), F32), cols(KV_WIDTH)),
        ("slcb", sds((b, t // ATT_TK, KV_WIDTH, ATT_TK), BF16), chunk(KV_WIDTH, ATT_TK)),
        ("winb", sds((b, t // ATT_TK, KV_WIDTH, ATT_TK), BF16), chunk(KV_WIDTH, ATT_TK)),
        ("slcr", sds((b, t, KV_WIDTH), BF16), rows(KV_WIDTH)),
        ("winr", sds((b, t, KV_WIDTH), BF16), rows(KV_WIDTH)),
        ("kc", sds((b, t // CMP_BLOCK, KV_WIDTH), F32),
         pl.BlockSpec((1, tm // CMP_BLOCK, KV_WIDTH), lambda i, bb: (bb, i, 0))),
        ("ga", sds((b, t, NSA_WIDTH), BF16), rows(NSA_WIDTH)),
        ("gb", sds((b, t, MLA_WIDTH), BF16), rows(MLA_WIDTH)),
        ("ma", sds((b, t, D_MODEL), BF16), rows(D_MODEL)),
        ("mb", sds((b, t, D_MODEL), BF16), rows(D_MODEL)),
        ("qmt", sds((b, MLA_HEADS * LANES, t), BF16), cols(MLA_HEADS * LANES)),
        ("c", sds((b, t, MLA_KV_LORA), F32), rows(MLA_KV_LORA)),
        ("kmla", sds((b, t, MLA_HEADS * LANES), BF16), rows(MLA_HEADS * LANES)),
        ("vmt", sds((b, t // MLA_TK, MLA_WIDTH, MLA_TK), BF16), chunk(MLA_WIDTH, MLA_TK)),
        ("kr", sds((b, MLA_ROPE, t), F32), cols(MLA_ROPE)),
        ("kn", sds((b, nt * MLA_HEADS, LANES), F32), pl.BlockSpec((1, MLA_HEADS, LANES), lambda i, bb: (bb, i, 0))),
    ]
    rope_spec = lambda half: pl.BlockSpec((2, half, tm), lambda i, bb: (0, 0, i))
    in_specs = [rows(D_MODEL), _full_spec((1, D_MODEL)), _full_spec((PT_ROWS, D_MODEL)),
                _full_spec((D_MODEL, PR_COLS)), rope_spec(ROT_DIM // 2), rope_spec(MLA_ROPE // 2),
                _full_spec((CMP_BLOCK, KV_WIDTH)), _full_spec((1, MLA_Q_LORA)),
                _full_spec((MLA_HEADS * (MLA_NOPE + MLA_ROPE), MLA_Q_LORA)), _full_spec((1, MLA_KV_LORA)),
                _full_spec((16, tm)), _full_spec((MLA_KV_LORA + LANES, MLA_HEADS * LANES)),
                _full_spec((MLA_WIDTH, MLA_KV_LORA))]
    res = pl.pallas_call(
        _inproj_cols_kernel, grid=(nt, b), in_specs=in_specs, out_specs=[o[2] for o in outs],
        out_shape=[o[1] for o in outs],
        compiler_params=pltpu.CompilerParams(dimension_semantics=("arbitrary", "arbitrary"),
                                             vmem_limit_bytes=VMEM_LIMIT),
        name="in_project_cols",
    )(x3, wts["npre"], wts["w_t"], wts["w_cols"], ropeq, ropem, wts["pe"], wts["qnorm"], wts["wqupt"],
      wts["kvnorm"], jnp.asarray(pool, BF16), wts["wkx"], wts["wvt"])
    return dict(zip([o[0] for o in outs], res))


def _online_update(s, vt, m, l, acc):
    d, keys = vt.shape
    m_new = jnp.maximum(m, jnp.max(s, axis=0, keepdims=True))
    p = jnp.exp2(s - m_new).astype(BF16)
    alpha = jnp.exp2(m - m_new)
    pv = _dot(jnp.concatenate([vt, jnp.ones((PAD_ROWS, keys), BF16)], axis=0), p)
    return m_new, alpha * l + pv[d:d + 1], alpha * acc + pv[0:d]


def _nsa_prompt_kernel(n_blk, qt_ref, qrt_ref, gate_ref, kc_ref, ks_ref, kst_ref, kw_ref, kwt_ref, o_ref, sb_ref,
                       kn_ref):
    i = pl.program_id(2)
    tq, tk = ATT_TQ, ATT_TK
    nl = NSA_GROUP * tq
    zq = jnp.zeros((HD, tq), BF16)

    def widen(qt):
        return jnp.concatenate([jnp.concatenate([qt[r * HD:(r + 1) * HD], zq], axis=0)
                                for r in range(NSA_GROUP)], axis=1)

    qc = widen(qt_ref[0])
    qr = widen(qrt_ref[0])

    n_cmp = 2 * n_blk
    kc = kc_ref[0]
    s_c = _dot(kc.astype(BF16), qc)
    rho = lax.broadcasted_iota(jnp.int32, (n_cmp, nl), 0)
    cmp_idx = jnp.where(rho < n_blk, 2 * rho, 2 * (rho - n_blk) + 1)
    t_l = i * tq + (lax.broadcasted_iota(jnp.int32, (n_cmp, nl), 1) & (tq - 1))
    mask_c = cmp_idx * CMP_BLOCK + (CMP_BLOCK - 1) <= t_l
    s_c = jnp.where(mask_c, s_c, NEG)
    e_c = jnp.where(mask_c, jnp.exp2(s_c - jnp.max(s_c, axis=0, keepdims=True)), 0.0)
    p_c = e_c / jnp.maximum(jnp.sum(e_c, axis=0, keepdims=True), 1e-30)
    kct = jnp.concatenate([kc, jnp.zeros((LANES - n_cmp, LANES), F32)], axis=0).T
    p_pad = jnp.concatenate([p_c, jnp.zeros((LANES - n_cmp, nl), F32)], axis=0)
    o_c = _dot(kct[HD:2 * HD].astype(BF16), p_pad.astype(BF16))

    imp = p_c[:, 0:tq]
    for r in range(1, NSA_GROUP):
        imp = imp + p_c[:, r * tq:(r + 1) * tq]
    imp_blk = imp[0:n_blk] + imp[n_blk:n_cmp]
    blk = lax.broadcasted_iota(jnp.int32, (n_blk, tq), 0)
    t_q = i * tq + lax.broadcasted_iota(jnp.int32, (n_blk, tq), 1)
    ahead_of = t_q - blk * SEL_BLOCK
    score = jnp.where(blk == 0, FORCE_SCORE,
                      jnp.where(ahead_of < 0, -FORCE_SCORE, jnp.where(ahead_of < SEL_BLOCK, FORCE_SCORE, imp_blk)))
    rank = jnp.zeros((n_blk, tq), F32)
    for j in range(n_blk):
        other = score[j:j + 1, :]
        tie = jnp.where(blk > j, 1.0, 0.0)
        rank = rank + jnp.where(other > score, 1.0, jnp.where(other == score, tie, 0.0))
    sb_ref[...] = jnp.where(rank < N_SELECT, jnp.where(score > -1.0, 0.0, NEG), NEG)

    key_r = lax.broadcasted_iota(jnp.int32, (tk, tq), 0)
    t_k = i * tq + lax.broadcasted_iota(jnp.int32, (tk, tq), 1)
    n_chunks = ((i + 1) * tq + tk - 1) // tk
    c_lo = jnp.maximum(i * tq - WINDOW, 0) // tk
    per_chunk = tk // SEL_BLOCK
    tile = lambda bias: jnp.concatenate([bias] * NSA_GROUP, axis=1)
    ones = jnp.ones((PAD_ROWS, tk), BF16)

    def sel_bias(c):
        rows = [jnp.broadcast_to(sb_ref[pl.ds(c * per_chunk + j, 1), :], (SEL_BLOCK, tq)) for j in range(per_chunk)]
        return jnp.concatenate(rows, axis=0)

    def values(ref, c):
        return jnp.concatenate([ref[0, c, HD:2 * HD, :], ones], axis=0)

    qf = qrt_ref[0].astype(F32)
    q_norm2 = [jnp.sum(qf[r * HD:(r + 1) * HD] ** 2, axis=0, keepdims=True) for r in range(NSA_GROUP)]
    q_norm = jnp.sqrt(jnp.maximum(jnp.maximum(q_norm2[0], q_norm2[1]), jnp.maximum(q_norm2[2], q_norm2[3])))

    def key_norm(ref):
        kf = ref[0, :, 0:HD, :].astype(F32)
        n2 = jnp.max(jnp.sum(kf * kf, axis=1, keepdims=True), axis=0)
        return jnp.broadcast_to(jnp.sqrt(jnp.max(n2, axis=1, keepdims=True)), (1, LANES))

    @pl.when(i == 0)
    def _():
        kn_ref[0:1, :] = key_norm(kst_ref)
        kn_ref[1:2, :] = key_norm(kwt_ref)

    wide = lambda row: jnp.concatenate([row] * (tq // LANES), axis=1)
    bound_s = q_norm * (wide(kn_ref[0:1, :]) * 1.001) + 1e-3
    bound_w = q_norm * (wide(kn_ref[1:2, :]) * 1.001) + 1e-3
    safe = jnp.maximum(jnp.max(bound_s), jnp.max(bound_w)) <= SHIFT_LIMIT

    def attend_shifted():
        def past_probs(c):
            base = pl.multiple_of(c * tk, tk)
            s = _dot(ks_ref[0, pl.ds(base, tk), :], qr) + tile(sel_bias(c) - bound_s)
            return jnp.exp2(s).astype(BF16)

        def past_body(c, acc):
            return acc + _dot(values(kst_ref, c), past_probs(c))

        def past_pair(cc, acc):
            p0, p1 = past_probs(2 * cc), past_probs(2 * cc + 1)
            return acc + _dot(values(kst_ref, 2 * cc), p0) + _dot(values(kst_ref, 2 * cc + 1), p1)

        def near_body(c, carry):
            base = pl.multiple_of(c * tk, tk)
            s = _dot(ks_ref[0, pl.ds(base, tk), :], qr)
            w = _dot(kw_ref[0, pl.ds(base, tk), :], qr)
            dist = t_k - (base + key_r)
            causal = jnp.where(dist >= 0, 0.0, NEG)
            p_s = jnp.exp2(s + tile(sel_bias(c) + causal - bound_s)).astype(BF16)
            p_w = jnp.exp2(w + tile(jnp.where(dist <= WINDOW, causal, NEG) - bound_w)).astype(BF16)
            return carry[0] + _dot(values(kst_ref, c), p_s), carry[1] + _dot(values(kwt_ref, c), p_w)

        zero = jnp.zeros((HD + PAD_ROWS, nl), F32)
        far = lax.fori_loop(0, c_lo // 2, past_pair, zero)
        far = lax.fori_loop(2 * (c_lo // 2), c_lo, past_body, far)
        a_s, a_w = lax.fori_loop(c_lo, n_chunks, near_body, (far, zero))
        return a_s[0:HD] / a_s[HD:HD + 1], a_w[0:HD] / a_w[HD:HD + 1]

    def attend_online():
        init = (jnp.full((1, nl), NEG, F32), jnp.zeros((1, nl), F32), jnp.zeros((HD, nl), F32))

        def past_body(c, carry):
            base = pl.multiple_of(c * tk, tk)
            s = _dot(ks_ref[0, pl.ds(base, tk), :], qr) + tile(sel_bias(c))
            return _online_update(s, kst_ref[0, c, HD:2 * HD, :], *carry)

        def near_body(c, carry):
            base = pl.multiple_of(c * tk, tk)
            s = _dot(ks_ref[0, pl.ds(base, tk), :], qr)
            w = _dot(kw_ref[0, pl.ds(base, tk), :], qr)
            dist = t_k - (base + key_r)
            causal = jnp.where(dist >= 0, 0.0, NEG)
            s = s + tile(sel_bias(c) + causal)
            w = w + tile(jnp.where(dist <= WINDOW, causal, NEG))
            return (_online_update(s, kst_ref[0, c, HD:2 * HD, :], *carry[:3])
                    + _online_update(w, kwt_ref[0, c, HD:2 * HD, :], *carry[3:]))

        far = lax.fori_loop(0, c_lo, past_body, init)
        _, l_s, a_s, _, l_w, a_w = lax.fori_loop(c_lo, n_chunks, near_body, far + init)
        return a_s / l_s, a_w / l_w

    o_s, o_w = lax.cond(safe, attend_shifted, attend_online)

    gate = gate_ref[0]
    heads = []
    for r in range(NSA_GROUP):
        sl = slice(r * tq, (r + 1) * tq)
        heads.append(gate[3 * r:3 * r + 1] * o_c[:, sl] + gate[3 * r + 1:3 * r + 2] * o_s[:, sl]
                     + gate[3 * r + 2:3 * r + 3] * o_w[:, sl])
    o_ref[0] = jnp.concatenate(heads, axis=0).T.astype(BF16)


def _nsa_prompt(p, b, t):
    tq, tk = ATT_TQ, ATT_TK
    n_blk = t // SEL_BLOCK
    assert 2 * n_blk <= LANES and t % tk == 0
    kc = p["kc"].reshape(b, n_blk, 2, KV_WIDTH).transpose(0, 2, 1, 3).reshape(b, 2 * n_blk, KV_WIDTH)
    qspec = pl.BlockSpec((1, NSA_GROUP * HD, tq), lambda bb, g, i: (bb, g, i))
    rm = pl.BlockSpec((1, t, LANES), lambda bb, g, i: (bb, 0, g))
    fm = pl.BlockSpec((1, t // tk, LANES, tk), lambda bb, g, i: (bb, 0, g, 0))
    return pl.pallas_call(
        functools.partial(_nsa_prompt_kernel, n_blk),
        grid=(b, NSA_KV_HEADS, t // tq),
        in_specs=[qspec, qspec, pl.BlockSpec((1, GATE_ROWS, tq), lambda bb, g, i: (bb, g, i)),
                  pl.BlockSpec((1, 2 * n_blk, LANES), lambda bb, g, i: (bb, 0, g)), rm, fm, rm, fm],
        out_specs=pl.BlockSpec((1, tq, NSA_GROUP * HD), lambda bb, g, i: (bb, i, g)),
        out_shape=jax.ShapeDtypeStruct((b, t, NSA_WIDTH), BF16),
        scratch_shapes=[pltpu.VMEM((n_blk, tq), F32), pltpu.VMEM((8, LANES), F32)],
        compiler_params=pltpu.CompilerParams(dimension_semantics=("arbitrary",) * 3, vmem_limit_bytes=VMEM_LIMIT),
        name="nsa_prompt",
    )(p["qt"], p["qrt"], p["gate"], kc, p["slcr"], p["slcb"], p["winr"], p["winb"])


MLA_TQ = 256


def _mla_prompt_kernel(qmt_ref, k_ref, vt_ref, kn_ref, o_ref, m_ref, l_ref, acc_ref):
    i = pl.program_id(1)
    tq, tk = MLA_TQ, MLA_TK
    heads = range(MLA_HEADS)
    m_ref[...] = jnp.full(m_ref.shape, NEG, F32)
    l_ref[...] = jnp.zeros(l_ref.shape, F32)
    acc_ref[...] = jnp.zeros(acc_ref.shape, F32)
    key_r = lax.broadcasted_iota(jnp.int32, (tk, tq), 0)
    t_k = i * tq + lax.broadcasted_iota(jnp.int32, (tk, tq), 1)
    n_chunks = ((i + 1) * tq + tk - 1) // tk
    n_past = (i * tq + 1) // tk
    ones = jnp.ones((PAD_ROWS, tk), BF16)

    def keys(c_base, h):
        return k_ref[0, pl.ds(c_base, tk), h * LANES:(h + 1) * LANES]

    kn = kn_ref[0]
    k_max = kn[0:MLA_HEADS]
    for j in range(1, kn.shape[0] // MLA_HEADS):
        k_max = jnp.maximum(k_max, kn[j * MLA_HEADS:(j + 1) * MLA_HEADS])
    bounds = []
    for h in heads:
        qf = qmt_ref[0, h * LANES:(h + 1) * LANES, :].astype(F32)
        q_norm = jnp.sqrt(jnp.sum(qf * qf, axis=0, keepdims=True))
        k_row = jnp.concatenate([k_max[h:h + 1]] * (tq // LANES), axis=1)
        bounds.append(q_norm * (k_row * 1.001) + 1e-3)
    worst = bounds[0]
    for bd in bounds[1:]:
        worst = jnp.maximum(worst, bd)
    safe = jnp.max(worst) <= SHIFT_LIMIT

    def shifted_step(c, masked):
        base = pl.multiple_of(c * tk, tk)
        old = [acc_ref[h] for h in heads]
        ps = []
        for h in heads:
            s = _dot(keys(base, h), qmt_ref[0, h * LANES:(h + 1) * LANES, :]) - bounds[h]
            ps.append(jnp.exp2(jnp.where(base + key_r <= t_k, s, NEG) if masked else s).astype(BF16))
        new = [old[h] + _dot(jnp.concatenate([vt_ref[0, c, h * MLA_V:(h + 1) * MLA_V, :], ones], axis=0), ps[h])
               for h in heads]
        for h in heads:
            acc_ref[h] = new[h]
        return 0

    def online_step(c, masked):
        base = pl.multiple_of(c * tk, tk)
        old = [(m_ref[h], l_ref[h], acc_ref[h, 0:MLA_V]) for h in heads]
        scores = []
        for h in heads:
            s = _dot(keys(base, h), qmt_ref[0, h * LANES:(h + 1) * LANES, :])
            scores.append(jnp.where(base + key_r <= t_k, s, NEG) if masked else s)
        new = [_online_update(scores[h], vt_ref[0, c, h * MLA_V:(h + 1) * MLA_V, :], *old[h]) for h in heads]
        for h in heads:
            m_ref[h], l_ref[h], acc_ref[h, 0:MLA_V] = new[h]
        return 0

    def run(step):
        lax.fori_loop(0, n_past, lambda c, z: step(c, False), 0)
        lax.fori_loop(n_past, n_chunks, lambda c, z: step(c, True), 0)

    def shifted():
        run(shifted_step)
        return tuple(acc_ref[h, 0:MLA_V] / acc_ref[h, MLA_V:MLA_V + 1] for h in heads)

    def online():
        run(online_step)
        return tuple(acc_ref[h, 0:MLA_V] / l_ref[h] for h in heads)

    outs = lax.cond(safe, shifted, online)
    o_ref[0] = jnp.concatenate(outs, axis=0).T.astype(BF16)


def _mla_prompt(p, wts, b, t):
    tq, tk = MLA_TQ, MLA_TK
    assert t % tq == 0
    return pl.pallas_call(
        _mla_prompt_kernel,
        grid=(b, t // tq),
        in_specs=[pl.BlockSpec((1, MLA_HEADS * LANES, tq), lambda bb, i: (bb, 0, i)),
                  pl.BlockSpec((1, t, MLA_HEADS * LANES), lambda bb, i: (bb, 0, 0)),
                  pl.BlockSpec((1, t // tk, MLA_WIDTH, tk), lambda bb, i: (bb, 0, 0, 0)),
                  pl.BlockSpec((1,) + p["kn"].shape[1:], lambda bb, i: (bb, 0, 0))],
        out_specs=pl.BlockSpec((1, tq, MLA_WIDTH), lambda bb, i: (bb, i, 0)),
        out_shape=jax.ShapeDtypeStruct((b, t, MLA_WIDTH), BF16),
        scratch_shapes=[pltpu.VMEM((MLA_HEADS, 1, tq), F32), pltpu.VMEM((MLA_HEADS, 1, tq), F32),
                        pltpu.VMEM((MLA_HEADS, MLA_V + PAD_ROWS, tq), F32)],
        compiler_params=pltpu.CompilerParams(dimension_semantics=("arbitrary",) * 2, vmem_limit_bytes=VMEM_LIMIT),
        name="mla_prompt",
    )(p["qmt"], p["kmla"], p["vmt"], p["kn"])


def _merge_kernel(from_latent, x_ref, oa_ref, ob_ref, ga_ref, gb_ref, ma_ref, mb_ref, wpa_ref, wpb_ref, wout_ref,
                  npost_ref, wuv_ref, y_ref):
    if from_latent:
        lat = ob_ref[0].astype(BF16)
        parts = []
        for j in range(MLA_HEADS // 2):
            parts.append(_dot(lat[:, 2 * j * MLA_KV_LORA:(2 * j + 1) * MLA_KV_LORA], wuv_ref[2 * j])
                         + _dot(lat[:, (2 * j + 1) * MLA_KV_LORA:(2 * j + 2) * MLA_KV_LORA], wuv_ref[2 * j + 1]))
        o_b = jnp.concatenate(parts, axis=1)
    else:
        o_b = ob_ref[0].astype(F32)
    ga = ga_ref[0].astype(F32)
    gb = gb_ref[0].astype(F32)
    pa = _dot((oa_ref[0].astype(F32) * (ga * jax.nn.sigmoid(ga))).astype(BF16), wpa_ref[...])
    pb = _dot((o_b * (gb * jax.nn.sigmoid(gb))).astype(BF16), wpb_ref[...])
    h = jax.nn.sigmoid(ma_ref[0].astype(F32)) * pa + jax.nn.sigmoid(mb_ref[0].astype(F32)) * pb
    z = _dot(h.astype(BF16), wout_ref[...])
    y_ref[0] = x_ref[0] + _rms(z, npost_ref[...])


def _merge(x3, o_a, o_b, p, wts, tm, from_latent):
    b, t, _ = x3.shape
    tok = lambda w: pl.BlockSpec((1, tm, w), lambda bb, i: (bb, i, 0))
    return pl.pallas_call(
        functools.partial(_merge_kernel, from_latent),
        grid=(b, t // tm),
        in_specs=[tok(D_MODEL), tok(NSA_WIDTH), tok(o_b.shape[2]), tok(NSA_WIDTH), tok(MLA_WIDTH), tok(D_MODEL),
                  tok(D_MODEL), _full_spec((NSA_WIDTH, D_MODEL)), _full_spec((MLA_WIDTH, D_MODEL)),
                  _full_spec((D_MODEL, D_MODEL)), _full_spec((1, D_MODEL)),
                  _full_spec((MLA_HEADS, MLA_KV_LORA, LANES))],
        out_specs=tok(D_MODEL),
        out_shape=jax.ShapeDtypeStruct((b, t, D_MODEL), F32),
        compiler_params=pltpu.CompilerParams(dimension_semantics=("arbitrary",) * 2, vmem_limit_bytes=VMEM_LIMIT),
        name="merge",
    )(x3, o_a, o_b, p["ga"], p["gb"], p["ma"], p["mb"], wts["wpa"], wts["wpb"], wts["wout"], wts["npost"],
      wts["wuv"])


PAGES_PER_STEP = 32
POOL_PAGES = 4


def _page_copy(source, buf_ref, sem_ref, step, slot, k):
    return pltpu.make_async_copy(source(step, k), buf_ref.at[slot, k], sem_ref.at[slot])


def _table_pages(pt_ref, cache_ref):
    return lambda step, k: cache_ref.at[pt_ref[step * PAGES_PER_STEP + k]]


def _paged_pipeline(streams):
    step = pl.program_id(0) * pl.num_programs(1) + pl.program_id(1)
    last = pl.num_programs(0) * pl.num_programs(1) - 1
    slot = step % 2

    @pl.when(step == 0)
    def _():
        for source, buf_ref, sem_ref in streams:
            for k in range(buf_ref.shape[1]):
                _page_copy(source, buf_ref, sem_ref, 0, 0, k).start()

    @pl.when(step < last)
    def _():
        for source, buf_ref, sem_ref in streams:
            for k in range(buf_ref.shape[1]):
                _page_copy(source, buf_ref, sem_ref, step + 1, 1 - slot, k).start()

    for source, buf_ref, sem_ref in streams:
        for k in range(buf_ref.shape[1]):
            _page_copy(source, buf_ref, sem_ref, step, slot, k).wait()
    return slot


XLU_POOL_PAGES = 16
MXU_POOL_GROUP = 8


def _compress_pages_kernel(pt_ref, cache_ref, pool_ref, pe_ref, o_ref, buf_ref, sem_ref):
    slot = _paged_pipeline([(_table_pages(pt_ref, cache_ref), buf_ref, sem_ref)])
    pe_sum = jnp.sum(pe_ref[...], axis=0, keepdims=True)
    pool = pool_ref[...]
    groups = [jnp.concatenate([buf_ref[slot, k + j] for j in range(MXU_POOL_GROUP)], axis=1)
              for k in range(XLU_POOL_PAGES, PAGES_PER_STEP, MXU_POOL_GROUP)]
    splits = [_split_bf16(x) for x in groups]
    by_mxu = [_dot_nt(pool, hi) + _dot_nt(pool, lo) for hi, lo in splits]
    by_xlu = []
    for k in range(XLU_POOL_PAGES):
        rows = buf_ref[slot, k].T
        by_xlu.append(jnp.sum(rows.reshape(rows.shape[0] // CMP_BLOCK, CMP_BLOCK, KV_WIDTH), axis=1))
    o_ref[0] = (jnp.concatenate(by_xlu + by_mxu, axis=0) + pe_sum) * (1.0 / CMP_BLOCK)


def _compress_pages(cache_t, pt_flat, pe, b, n_pages):
    page = cache_t.shape[2]
    per_step = PAGES_PER_STEP * page // CMP_BLOCK
    pool = np.zeros((MXU_POOL_GROUP * page // CMP_BLOCK, MXU_POOL_GROUP * page), np.float32)
    for s in range(pool.shape[1]):
        pool[s // CMP_BLOCK, s] = 1.0
    gs = pltpu.PrefetchScalarGridSpec(
        num_scalar_prefetch=1, grid=(b, n_pages // PAGES_PER_STEP),
        in_specs=[pl.BlockSpec(memory_space=pl.ANY),
                  pl.BlockSpec(pool.shape, lambda bb, c, pt: (0, 0)),
                  pl.BlockSpec((CMP_BLOCK, KV_WIDTH), lambda bb, c, pt: (0, 0))],
        out_specs=pl.BlockSpec((1, per_step, KV_WIDTH), lambda bb, c, pt: (bb, c, 0)),
        scratch_shapes=[pltpu.VMEM((2, PAGES_PER_STEP, KV_WIDTH, page), F32), pltpu.SemaphoreType.DMA((2,))])
    return pl.pallas_call(
        _compress_pages_kernel, grid_spec=gs,
        out_shape=jax.ShapeDtypeStruct((b, n_pages * page // CMP_BLOCK, KV_WIDTH), F32),
        compiler_params=pltpu.CompilerParams(dimension_semantics=("arbitrary",) * 2, vmem_limit_bytes=VMEM_LIMIT),
        name="compress_pages",
    )(pt_flat, cache_t, jnp.asarray(pool, BF16), pe)


def _pad_rows(v, rows):
    return jnp.concatenate([v, jnp.zeros((rows - v.shape[0], v.shape[1]), v.dtype)], axis=0)


def _stack_heads(qv, lo):
    a, b = qv[:, :LANES], qv[:, LANES:]
    z = jnp.zeros_like(a)
    return jnp.concatenate([jnp.where(lo, a, z), jnp.where(lo, z, a),
                            jnp.where(lo, b, z), jnp.where(lo, z, b)], axis=0)


def _sample_select_kernel(n_cmp, q_ref, kc_ref, oc_ref, idx_ref, imp_ref):
    bb = pl.program_id(0)
    lo1 = _lane_lo(1)
    lo_c = _lane_lo(n_cmp)
    q = q_ref[0]
    kc = kc_ref[0]
    even = (lax.broadcasted_iota(jnp.int32, (1, LANES), 1) & 1) == 0
    for g in range(NSA_KV_HEADS):
        kk, vv = _dup_kv(kc[:, g * LANES:(g + 1) * LANES], lo_c)
        qs = _pad_rows(_stack_heads(q[:, g * 2 * LANES:(g + 1) * 2 * LANES], lo1), PAD_ROWS)
        s = _dot_nt(qs, kk.astype(BF16))
        e = jnp.exp(s - jnp.max(s, axis=-1, keepdims=True))
        p = e / jnp.sum(e, axis=-1, keepdims=True)
        oc_ref[0, g] = _dot(p.astype(BF16), vv.astype(BF16))
        imp = p[0:1] + p[1:2] + p[2:3] + p[3:4]
        chunks = []
        for k in range(n_cmp // LANES):
            a = imp[:, k * LANES:(k + 1) * LANES]
            chunks.append(a + jnp.where(even, pltpu.roll(a, LANES - 1, 1), pltpu.roll(a, 1, 1)))
        imp_ref[pl.ds(bb * NSA_KV_HEADS + g, 1), :] = jnp.concatenate(chunks, axis=1)

    @pl.when(bb == pl.num_programs(0) - 1)
    def _():
        rows = imp_ref.shape[0]
        blk = lax.broadcasted_iota(jnp.int32, (rows, n_cmp), 1) >> 1
        blk_f = blk.astype(F32)
        slot = lax.broadcasted_iota(jnp.int32, (rows, N_SELECT), 1)
        v = jnp.where(blk == 0, -1.0, imp_ref[...])
        idx = jnp.where(slot == N_SELECT - 1, n_cmp // 2, 0)
        for k in range(1, N_SELECT - 1):
            top = jnp.max(v, axis=-1, keepdims=True)
            jmin = jnp.min(jnp.where(v == top, blk_f, float(n_cmp)), axis=-1, keepdims=True).astype(jnp.int32)
            idx = jnp.where(slot == k, jmin, idx)
            v = jnp.where(blk == jmin, -1.0, v)
        idx_ref[...] = idx


def _sample_select(q, kc_all, b):
    n_cmp = kc_all.shape[1]
    rows = b * NSA_KV_HEADS
    return pl.pallas_call(
        functools.partial(_sample_select_kernel, n_cmp),
        grid=(b,),
        in_specs=[pl.BlockSpec((1, 1, NSA_WIDTH), lambda bb: (bb, 0, 0)),
                  pl.BlockSpec((1, n_cmp, KV_WIDTH), lambda bb: (bb, 0, 0))],
        out_specs=[pl.BlockSpec((1, NSA_KV_HEADS, PAD_ROWS, LANES), lambda bb: (bb, 0, 0, 0)),
                   pl.BlockSpec((rows, N_SELECT), lambda bb: (0, 0))],
        out_shape=[jax.ShapeDtypeStruct((b, NSA_KV_HEADS, PAD_ROWS, LANES), F32),
                   jax.ShapeDtypeStruct((rows, N_SELECT), jnp.int32)],
        scratch_shapes=[pltpu.VMEM((rows, n_cmp), F32)],
        compiler_params=pltpu.CompilerParams(dimension_semantics=("arbitrary",), vmem_limit_bytes=VMEM_LIMIT),
        name="sample_select",
    )(q, kc_all)


def _extra_key_softmax(s_past, vt4_b, s_new, v_new):
    m = jnp.maximum(jnp.max(s_past, axis=-1, keepdims=True), s_new)
    e = jnp.exp(s_past - m)
    e_new = jnp.exp(s_new - m)
    den = jnp.sum(e, axis=-1, keepdims=True) + e_new
    return (_dot_nt(e.astype(BF16), vt4_b) + e_new * v_new) / den


def _sample_attend_kernel(n_pages, idx_ref, pt_ref, slc_ref, qr_ref, newkv_ref, neww_ref, newwf_ref, win_ref,
                          gate_ref, oc_ref, o_ref, wout_ref, buf_ref, sem_ref):
    n_sel = N_SELECT
    bb, g = pl.program_id(0), pl.program_id(1)

    def selected_page(step, k):
        j = idx_ref[step * n_sel + k]
        page = pt_ref[(step // NSA_KV_HEADS) * n_pages + jnp.minimum(j // 2, n_pages - 1)]
        return slc_ref.at[page, pl.ds((step % NSA_KV_HEADS) * LANES, LANES), :]

    slot = _paged_pipeline([(selected_page, buf_ref, sem_ref)])
    pages = [buf_ref.at[slot, k] for k in range(n_sel)]
    lo1 = _lane_lo(1)
    q = qr_ref[0].astype(F32)
    halves = [q[:, 0:LANES], q[:, LANES:2 * LANES]]
    rows = []
    for r in range(NSA_GROUP):
        a = halves[r // 2]
        rows.append(jnp.where(lo1, a if r % 2 == 0 else pltpu.roll(a, HALF, 1), 0.0))
    qs_f = _pad_rows(jnp.concatenate(rows, axis=0), PAD_ROWS)
    qs = qs_f.astype(BF16)
    twice_rows = lambda a: jnp.concatenate([a, a], axis=0)

    s_t = jnp.concatenate([pg[...] for pg in pages], axis=1)
    s_sel = _dot(qs, s_t.astype(BF16))
    base = (bb * NSA_KV_HEADS + g) * n_sel
    biases = []
    for k in range(n_sel - 1):
        odd = (idx_ref[base + k] & 1) == 1
        biases.append(jnp.where(lo1, jnp.where(odd, NEG, 0.0), jnp.where(odd, 0.0, NEG)))
    biases.append(jnp.full((1, LANES), NEG, F32))
    s_sel = s_sel + jnp.concatenate(biases, axis=1)
    nk = newkv_ref[0].astype(F32)
    s_new = jnp.sum(qs_f * nk[:, :LANES], axis=-1, keepdims=True)
    o_s = _extra_key_softmax(s_sel, twice_rows(s_t[HD:2 * HD]).astype(BF16), s_new, nk[:, LANES:])

    w = win_ref[0]
    nw = neww_ref[0].astype(F32)
    s_w = _dot(qs, w.astype(BF16))
    s_wn = jnp.sum(qs_f * nw[:, :LANES], axis=-1, keepdims=True)
    o_w = _extra_key_softmax(s_w, twice_rows(w[HD:2 * HD]).astype(BF16), s_wn, nw[:, LANES:])

    o_c = oc_ref[0, 0]
    gates = gate_ref[0]
    gate = jnp.where(g == 0, gates[:, 0:GATE_ROWS], gates[:, GATE_ROWS:2 * GATE_ROWS])
    heads = []
    for r in range(NSA_GROUP):
        heads.append(gate[:, 3 * r:3 * r + 1] * o_c[r:r + 1] + gate[:, 3 * r + 1:3 * r + 2] * o_s[r:r + 1]
                     + gate[:, 3 * r + 2:3 * r + 3] * o_w[r:r + 1])
    o_ref[0] = jnp.concatenate([jnp.where(lo1, heads[0], heads[1]), jnp.where(lo1, heads[2], heads[3])], axis=1)

    n_feat, n_w = w.shape
    new_row = jnp.broadcast_to(newwf_ref[0], (n_feat, n_feat))
    diag = (lax.broadcasted_iota(jnp.int32, (n_feat, n_feat), 0)
            == lax.broadcasted_iota(jnp.int32, (n_feat, n_feat), 1))
    new_col = jnp.sum(jnp.where(diag, new_row, 0.0), axis=1, keepdims=True)
    last = lax.broadcasted_iota(jnp.int32, (n_feat, LANES), 1) == LANES - 1
    chunks = []
    n_ch = n_w // LANES
    for c in range(n_ch):
        cur = pltpu.roll(w[:, c * LANES:(c + 1) * LANES], LANES - 1, 1)
        if c + 1 < n_ch:
            nxt = pltpu.roll(w[:, (c + 1) * LANES:(c + 2) * LANES], LANES - 1, 1)
        else:
            nxt = jnp.broadcast_to(new_col, (n_feat, LANES))
        chunks.append(jnp.where(last, nxt, cur))
    wout_ref[0] = jnp.concatenate(chunks, axis=1)


def _sample_attend(p, oc, idx_flat, pt_flat, slc_t, win_t, b, n_pages):
    page = slc_t.shape[2]
    win_len = win_t.shape[2]
    assert page == 2 * SEL_BLOCK
    row = lambda w: pl.BlockSpec((1, 1, w), lambda bb, g, idx, pt: (bb, 0, g))
    wspec = pl.BlockSpec((1, LANES, win_len), lambda bb, g, idx, pt: (bb, g, 0))
    gs = pltpu.PrefetchScalarGridSpec(
        num_scalar_prefetch=2, grid=(b, NSA_KV_HEADS),
        in_specs=[pl.BlockSpec(memory_space=pl.ANY),
                  row(2 * LANES), row(2 * LANES), row(2 * LANES), row(LANES), wspec,
                  pl.BlockSpec((1, 1, NSA_KV_HEADS * GATE_ROWS), lambda bb, g, idx, pt: (bb, 0, 0)),
                  pl.BlockSpec((1, 1, PAD_ROWS, LANES), lambda bb, g, idx, pt: (bb, g, 0, 0))],
        out_specs=[row(2 * LANES), wspec],
        scratch_shapes=[pltpu.VMEM((2, N_SELECT, LANES, page), F32), pltpu.SemaphoreType.DMA((2,))])
    return pl.pallas_call(
        functools.partial(_sample_attend_kernel, n_pages), grid_spec=gs,
        out_shape=[jax.ShapeDtypeStruct((b, 1, NSA_WIDTH), F32),
                   jax.ShapeDtypeStruct((b, KV_WIDTH, win_len), F32)],
        compiler_params=pltpu.CompilerParams(dimension_semantics=("arbitrary",) * 2, vmem_limit_bytes=VMEM_LIMIT),
        name="sample_attend",
    )(idx_flat, pt_flat, slc_t, p["qrot"], p["slckv"], p["winkv"], p["win"], win_t, p["gate"], oc)


def _softmax_update(s, v_b, m, l, acc):
    m_new = jnp.maximum(m, jnp.max(s, axis=-1, keepdims=True))
    p = jnp.exp(s - m_new)
    alpha = jnp.exp(m - m_new)
    return m_new, alpha * l + jnp.sum(p, axis=-1, keepdims=True), alpha * acc + _dot(p.astype(BF16), v_b)


def _mla_sample_kernel(pt_ref, lat_ref, krt_ref, qlat_ref, qpe_ref, cnew_ref, krnew_ref, o_ref,
                       m_ref, l_ref, acc_ref, lat_buf, kr_buf, lat_sem, kr_sem):
    n = PAGES_PER_STEP
    slot = _paged_pipeline([(_table_pages(pt_ref, lat_ref), lat_buf, lat_sem),
                            (_table_pages(pt_ref, krt_ref), kr_buf, kr_sem)])
    lat_pages = [lat_buf.at[slot, k] for k in range(n)]
    kr_pages = [kr_buf.at[slot, k] for k in range(n)]
    step = pl.program_id(1)
    qlat = _pad_rows(qlat_ref[0], PAD_ROWS)
    qpe = _pad_rows(qpe_ref[0], PAD_ROWS)

    @pl.when(step == 0)
    def _():
        c_new = cnew_ref[0]
        s_new = (jnp.sum(qlat * c_new, axis=-1, keepdims=True)
                 + jnp.sum(qpe * krnew_ref[0], axis=-1, keepdims=True))
        m_ref[...] = s_new
        l_ref[...] = jnp.ones(l_ref.shape, F32)
        acc_ref[...] = jnp.broadcast_to(c_new, acc_ref.shape)

    qlat_b, qpe_b = qlat.astype(BF16), qpe.astype(BF16)
    groups = range(0, n, POOL_PAGES)
    c_bs = [jnp.concatenate([pg[...] for pg in lat_pages[k:k + POOL_PAGES]], axis=0).astype(BF16) for k in groups]
    kr_bs = [jnp.concatenate([pg[...] for pg in kr_pages[k:k + POOL_PAGES]], axis=1).astype(BF16) for k in groups]
    s_lat = [_dot_nt(qlat_b, c_b) for c_b in c_bs]
    s_pe = [_dot(qpe_b, kr_b) for kr_b in kr_bs]
    scores = [a + r for a, r in zip(s_lat, s_pe)]
    maxes = [jnp.max(s, axis=-1, keepdims=True) for s in scores]
    ps = [jnp.exp(s - mx) for s, mx in zip(scores, maxes)]
    sums = [jnp.sum(p_g, axis=-1, keepdims=True) for p_g in ps]
    accs = [_dot(p_g.astype(BF16), c_b) for p_g, c_b in zip(ps, c_bs)]
    m_old = m_ref[...]
    m = m_old
    for mx in maxes:
        m = jnp.maximum(m, mx)
    alpha = jnp.exp(m_old - m)
    l = alpha * l_ref[...]
    acc = alpha * acc_ref[...]
    for mx, l_g, acc_g in zip(maxes, sums, accs):
        w = jnp.exp(mx - m)
        l = l + w * l_g
        acc = acc + w * acc_g
    m_ref[...] = m
    l_ref[...] = l
    acc_ref[...] = acc

    @pl.when(step == pl.num_programs(1) - 1)
    def _():
        o_ref[0] = (acc / l)[0:MLA_HEADS]


def _mla_sample(p, lat_cache, kr_t, pt_flat, b, n_pages):
    page = lat_cache.shape[1]
    head = lambda w: pl.BlockSpec((1, MLA_HEADS, w), lambda bb, c, pt: (bb, 0, 0))
    row = lambda w: pl.BlockSpec((1, 1, w), lambda bb, c, pt: (bb, 0, 0))
    gs = pltpu.PrefetchScalarGridSpec(
        num_scalar_prefetch=1, grid=(b, n_pages // PAGES_PER_STEP),
        in_specs=[pl.BlockSpec(memory_space=pl.ANY), pl.BlockSpec(memory_space=pl.ANY),
                  head(MLA_KV_LORA), head(MLA_ROPE), row(MLA_KV_LORA), row(MLA_ROPE)],
        out_specs=head(MLA_KV_LORA),
        scratch_shapes=[pltpu.VMEM((PAD_ROWS, 1), F32), pltpu.VMEM((PAD_ROWS, 1), F32),
                        pltpu.VMEM((PAD_ROWS, MLA_KV_LORA), F32),
                        pltpu.VMEM((2, PAGES_PER_STEP, page, MLA_KV_LORA), F32),
                        pltpu.VMEM((2, PAGES_PER_STEP, MLA_ROPE, page), F32),
                        pltpu.SemaphoreType.DMA((2,)), pltpu.SemaphoreType.DMA((2,))])
    qlat = p["qlat"].reshape(b, MLA_HEADS, MLA_KV_LORA)
    qpe = p["qpe"].reshape(b, MLA_HEADS, MLA_ROPE)
    return pl.pallas_call(
        _mla_sample_kernel, grid_spec=gs,
        out_shape=jax.ShapeDtypeStruct((b, MLA_HEADS, MLA_KV_LORA), F32),
        compiler_params=pltpu.CompilerParams(dimension_semantics=("arbitrary",) * 2, vmem_limit_bytes=VMEM_LIMIT),
        name="mla_sample",
    )(pt_flat, lat_cache, kr_t, qlat, qpe, p["c"], p["kr"])


def _rope_angles(pos, theta, dim):
    half = dim // 2
    inv = 1.0 / (float(theta) ** (np.arange(half, dtype=np.float64) / half))
    ang = np.asarray(pos, np.float64)[:, None] * inv[None, :]
    return np.cos(ang).astype(np.float32), np.sin(ang).astype(np.float32)


def _rope_lane_tables(pos, rows):
    def table(theta, dim, period, active):
        half = dim // 2
        cos, sin = _rope_angles(pos, theta, dim)
        lane = np.arange(LANES)
        d = lane % period
        is_lo = (d < half) & active(lane)
        is_hi = (d >= half) & (d < dim) & active(lane)
        fi = np.where(d < half, d, np.clip(d - half, 0, half - 1))
        cos_l, sin_l = cos[:, fi], sin[:, fi]
        tab = np.stack([np.where(is_lo | is_hi, cos_l, 1.0), np.where(is_lo, -sin_l, 0.0),
                        np.where(is_hi, sin_l, 0.0)]).astype(np.float32)
        return jnp.asarray(np.broadcast_to(tab, (3, rows, LANES)))

    every = lambda lane: np.ones_like(lane, bool)
    keys_only = lambda lane: (lane % LANES) < HD
    return (table(ROPE_THETA, ROT_DIM, HD, every), table(ROPE_THETA, ROT_DIM, HD, keys_only),
            table(MLA_ROPE_THETA, MLA_ROPE, MLA_ROPE, every))


def _rope_row_tables(pos):
    cq, sq = _rope_angles(pos, ROPE_THETA, ROT_DIM)
    cm, sm = _rope_angles(pos, MLA_ROPE_THETA, MLA_ROPE)
    return jnp.asarray(np.stack([cq.T, sq.T])), jnp.asarray(np.stack([cm.T, sm.T]))


def _pack_weights(l, norm_pre, w_in, pe_cmp, q_norm, w_q_up, kv_norm, w_kv_up, w_proj_a, w_proj_b, w_out, norm_post):
    w = w_in[l].astype(BF16)
    o = IN_OFFSETS
    seg = lambda k: w[:, o[k]:o[k + 1]]
    gn = seg(4)
    per_group = 3 * NSA_GROUP
    gn_t = jnp.zeros((NSA_KV_HEADS * GATE_ROWS, D_MODEL), w.dtype)
    for g in range(NSA_KV_HEADS):
        gn_t = gn_t.at[g * GATE_ROWS:g * GATE_ROWS + per_group].set(gn[:, g * per_group:(g + 1) * per_group].T)
    w_t = jnp.concatenate([seg(0).T, seg(1).T, seg(2).T, seg(3).T, gn_t, seg(8).T], axis=0)
    w_krp4 = jnp.tile(seg(8).T, (LANES // MLA_ROPE, 1))
    w_cols = jnp.concatenate([seg(5), seg(6), seg(7), seg(9), seg(10), seg(11)], axis=1)
    wq = w_q_up[l]
    wqup = jnp.concatenate([wq[..., :MLA_NOPE].reshape(MLA_Q_LORA, -1), wq[..., MLA_NOPE:].reshape(MLA_Q_LORA, -1)],
                           axis=1)
    wkv = w_kv_up[l]
    wuk_pad = jnp.pad(wkv[..., :MLA_NOPE], ((0, 0), (0, 0), (0, LANES - MLA_NOPE)))
    rope_copy = jnp.pad(jnp.eye(MLA_ROPE, dtype=w.dtype), ((0, LANES - MLA_ROPE), (MLA_NOPE, MLA_ROPE)))
    wkx = jnp.concatenate([wuk_pad.reshape(MLA_KV_LORA, MLA_HEADS * LANES), jnp.tile(rope_copy, (1, MLA_HEADS))],
                          axis=0)
    wvt = jnp.transpose(wkv[..., MLA_NOPE:], (1, 2, 0)).reshape(MLA_WIDTH, MLA_KV_LORA)
    w2uk =jnp.transpose(wkv[..., :MLA_NOPE], (1, 2, 0)).reshape(MLA_HEADS // 2, LANES, MLA_KV_LORA)
    wv = jnp.transpose(wkv[..., MLA_NOPE:], (1, 0, 2))
    zeros = jnp.zeros_like(wv)
    even = (jnp.arange(MLA_HEADS) % 2 == 0)[:, None, None]
    wuv = jnp.concatenate([jnp.where(even, wv, zeros), jnp.where(even, zeros, wv)], axis=2)
    return {
        "npre": norm_pre[l][None].astype(F32), "w_t": w_t, "w_krp4": w_krp4, "w_cols": w_cols, "pe": pe_cmp[l].reshape(CMP_BLOCK, KV_WIDTH).astype(F32),
        "qnorm": q_norm[l][None].astype(F32), "wqup": wqup.astype(BF16), "wqupt": wqup.T.astype(BF16),
        "kvnorm": kv_norm[l][None].astype(F32), "w2uk": w2uk.astype(BF16), "wkx": wkx.astype(BF16),
        "wvt": wvt.astype(BF16), "wuv": wuv.astype(BF16),
        "wpa": w_proj_a[l].astype(BF16), "wpb": w_proj_b[l].astype(BF16), "wout": w_out[l].astype(BF16),
        "npost": norm_post[l][None].astype(F32),
    }


def _rows_from_cols(a):
    b, _, t = a.shape
    return a.reshape(b, NSA_KV_HEADS, 2, HD, t).transpose(0, 4, 1, 2, 3)


def _cols_from_rows(a):
    n, t = a.shape[:2]
    return a.transpose(0, 2, 3, 4, 1).reshape(n, KV_WIDTH, t)


def _prompt_layer(x, wts):
    b, t, _ = x.shape
    assert t % PROMPT_TM == 0
    p = _in_project_cols(x, _rope_row_tables(np.arange(t)), wts)
    o_a = _nsa_prompt(p, b, t)
    o_b = _mla_prompt(p, wts, b, t)
    y = _merge(x, o_a, o_b, p, wts, 512, from_latent=False)
    win_keep = min(WINDOW, t)
    return y, (_rows_from_cols(p["cmp"]), _rows_from_cols(p["slc"]), p["c"], p["kr"].transpose(0, 2, 1),
               _rows_from_cols(p["win"][:, :, t - win_keep:]))


def _sample_layer(x, l, caches, state_win, page_table, wts):
    cache_cmp, cache_slc, cache_lat, cache_kr = caches
    b, s_new, _ = x.shape
    assert s_new == 1
    n_pages = page_table.shape[1]
    page = cache_cmp.shape[2]
    past = n_pages * page
    assert past % SEL_BLOCK == 0 and n_pages % PAGES_PER_STEP == 0 and page == LANES
    win_len = state_win.shape[1]
    assert win_len == WINDOW and past >= WINDOW
    pt_flat = page_table.reshape(-1).astype(jnp.int32)

    tabs = _rope_lane_tables(np.full((1,), past), b)
    p = _in_project_rows(x.reshape(1, b, D_MODEL), tabs, wts)
    p = {k: v.reshape(b, 1, v.shape[-1]) for k, v in p.items()}

    kc_all = _compress_pages(_cols_from_rows(cache_cmp[l]), pt_flat, wts["pe"], b, n_pages)
    oc, idx = _sample_select(p["q"], kc_all, b)
    o_a, new_win = _sample_attend(p, oc, idx.reshape(-1), pt_flat, _cols_from_rows(cache_slc[l]),
                                  _cols_from_rows(state_win), b, n_pages)
    o_lat = _mla_sample(p, cache_lat[l], cache_kr[l].transpose(0, 2, 1), pt_flat, b, n_pages)
    pm = {k: p[k].reshape(1, b, -1) for k in ("ga", "gb", "ma", "mb")}
    y = _merge(x.reshape(1, b, D_MODEL), o_a.reshape(1, b, NSA_WIDTH),
               o_lat.reshape(1, b, MLA_HEADS * MLA_KV_LORA), pm, wts, b, from_latent=True)
    kv6 = lambda a: a.reshape(b, 1, NSA_KV_HEADS, 2, HD)
    return y.reshape(b, 1, D_MODEL), (kv6(p["cmp"]), kv6(p["slc"]), p["c"], p["kr"], _rows_from_cols(new_win))


def kernel(x_prompt, x_sample, cache_nsa_cmp, cache_nsa_slc, cache_mla_latent, cache_mla_krope, state_nsa_win,
           page_table, norm_pre, w_in, pe_cmp, q_norm, w_q_up, kv_norm, w_kv_up, w_proj_a, w_proj_b, w_out,
           norm_post):
    depth = w_in.shape[0]
    hp, hs = x_prompt, x_sample
    new_p, new_s = [], []
    for l in range(depth):
        wts = _pack_weights(l, norm_pre, w_in, pe_cmp, q_norm, w_q_up, kv_norm, w_kv_up, w_proj_a, w_proj_b,
                            w_out, norm_post)
        hp, sp = _prompt_layer(hp, wts)
        hs, ss = _sample_layer(hs, l, (cache_nsa_cmp, cache_nsa_slc, cache_mla_latent, cache_mla_krope),
                               state_nsa_win[l], page_table, wts)
        new_p.append(sp)
        new_s.append(ss)
    stack = lambda items, k: jnp.stack([s[k] for s in items])
    return (hp, hs) + tuple(stack(new_p, k) for k in range(5)) + tuple(stack(new_s, k) for k in range(5))
```

```python
import functools

import numpy as np
import jax
import jax.numpy as jnp
from jax import lax
from jax.experimental import pallas as pl
from jax.experimental.pallas import tpu as pltpu

D_MODEL = 1024
NSA_HEADS = 8
NSA_KV_HEADS = 2
NSA_GROUP = NSA_HEADS // NSA_KV_HEADS
HD = 64
NSA_WIDTH = NSA_HEADS * HD
KV_WIDTH = NSA_KV_HEADS * 2 * HD
ROT_DIM = HD // 4
ROPE_THETA = 500000.0
CMP_BLOCK = 32
SEL_BLOCK = 64
N_SELECT = 16
WINDOW = 512
NSA_SCALE = HD ** -0.5

MLA_HEADS = 8
MLA_Q_LORA = 384
MLA_KV_LORA = 256
MLA_NOPE = 64
MLA_ROPE = 32
MLA_V = 64
MLA_WIDTH = MLA_HEADS * MLA_V
MLA_ROPE_THETA = 10000.0
MLA_SCALE = (MLA_NOPE + MLA_ROPE) ** -0.5
LOG2E = 1.4426950408889634
SHIFT_LIMIT = 40.0

RMS_EPS = 1e-6
NEG = -1e30
FORCE_SCORE = 1e4

IN_SPLITS = (NSA_WIDTH, KV_WIDTH, KV_WIDTH, KV_WIDTH, 3 * NSA_HEADS, NSA_WIDTH,
             MLA_Q_LORA, MLA_KV_LORA, MLA_ROPE, MLA_WIDTH, D_MODEL, D_MODEL)
IN_OFFSETS = tuple(int(v) for v in np.cumsum((0,) + IN_SPLITS))

LANES = 128
HALF = LANES // 2
GATE_ROWS = 16
PAD_ROWS = 16

PT_Q = (0, 512)
PT_CMP = (512, 768)
PT_SLC = (768, 1024)
PT_WIN = (1024, 1280)
PT_GN = (1280, 1280 + NSA_KV_HEADS * GATE_ROWS)
PT_KRP = (PT_GN[1], PT_GN[1] + MLA_ROPE)
PT_ROWS = PT_KRP[1]
PR_GA = (0, 512)
PR_QD = (512, 896)
PR_KVD = (896, 1152)
PR_GB = (1152, 1664)
PR_MA = (1664, 2688)
PR_MB = (2688, 3712)
PR_COLS = 3712

PROMPT_TM = 512
ATT_TQ = 256
ATT_TK = 256
MLA_TK = 256

VMEM_LIMIT = 48 * 1024 * 1024
BF16 = jnp.bfloat16
F32 = jnp.float32


def _full_spec(shape):
    nd = len(shape)
    return pl.BlockSpec(shape, lambda *_: (0,) * nd)


def _lane_lo(rows):
    return lax.broadcasted_iota(jnp.int32, (rows, LANES), 1) < HALF


def _dot(a, b):
    return jnp.dot(a, b, preferred_element_type=F32)


def _dot_nt(a, b):
    return lax.dot_general(a, b, (((1,), (1,)), ((), ())), preferred_element_type=F32)


def _rms(v, gain):
    return v * lax.rsqrt(jnp.mean(v * v, axis=-1, keepdims=True) + RMS_EPS) * gain


def _split_bf16(v):
    hi = v.astype(BF16)
    return hi, (v - hi.astype(F32)).astype(BF16)


def _rope_tiles(v, tab_ref, shift):
    c, s_lo, s_hi = tab_ref[0], tab_ref[1], tab_ref[2]
    out = []
    for k in range(v.shape[1] // LANES):
        a = v[:, k * LANES:(k + 1) * LANES]
        out.append(a * c + pltpu.roll(a, LANES - shift, 1) * s_lo + pltpu.roll(a, shift, 1) * s_hi)
    return out[0] if len(out) == 1 else jnp.concatenate(out, axis=1)


def _dup_kv(a, lo):
    r = pltpu.roll(a, HALF, 1)
    return jnp.where(lo, a, r), jnp.where(lo, r, a)


def _kv_pack(v):
    lo = _lane_lo(v.shape[0])
    parts = []
    for g in range(NSA_KV_HEADS):
        kk, vv = _dup_kv(v[:, g * LANES:(g + 1) * LANES], lo)
        parts += [kk, vv]
    return jnp.concatenate(parts, axis=1).astype(BF16)


def _inproj_rows_kernel(x_ref, npre_ref, wt_ref, wkrp_ref, w_ref, tq_ref, tkv_ref, tm_ref, qnorm_ref, wqup_ref,
                        kvnorm_ref, w2uk_ref, q_ref, qrot_ref, gate_ref, cmp_ref, slc_ref, win_ref, slckv_ref,
                        winkv_ref, ga_ref, gb_ref, ma_ref, mb_ref, qpe_ref, c_ref, kr_ref, qlat_ref):
    xb = _rms(x_ref[0], npre_ref[...]).astype(BF16)
    segt = lambda lohi: _dot_nt(xb, wt_ref[lohi[0]:lohi[1], :])
    seg = lambda lohi: _dot(xb, w_ref[:, lohi[0]:lohi[1]])

    a = segt((PT_Q[0], PT_WIN[1]))
    q = a[:, :NSA_WIDTH]
    q_ref[0] = (q * NSA_SCALE).astype(BF16)
    qrot_ref[0] = (_rope_tiles(q, tq_ref, ROT_DIM // 2) * NSA_SCALE).astype(BF16)
    cmp_ref[0] = a[:, NSA_WIDTH:NSA_WIDTH + KV_WIDTH]
    kvs = _rope_tiles(a[:, NSA_WIDTH + KV_WIDTH:NSA_WIDTH + 2 * KV_WIDTH], tkv_ref, ROT_DIM // 2)
    slc_ref[0] = kvs
    slckv_ref[0] = _kv_pack(kvs)
    kvw = _rope_tiles(a[:, NSA_WIDTH + 2 * KV_WIDTH:], tkv_ref, ROT_DIM // 2)
    win_ref[0] = kvw
    winkv_ref[0] = _kv_pack(kvw)

    gate_ref[0] = jax.nn.sigmoid(segt(PT_GN))
    ga_ref[0] = seg(PR_GA).astype(BF16)
    gb_ref[0] = seg(PR_GB).astype(BF16)
    ma_ref[0] = seg(PR_MA).astype(BF16)
    mb_ref[0] = seg(PR_MB).astype(BF16)

    qd = _rms(seg(PR_QD), qnorm_ref[...]).astype(BF16)
    qh = _dot(qd, wqup_ref[...])
    qn = qh[:, :MLA_HEADS * MLA_NOPE].astype(BF16)
    qpe_ref[0] = _rope_tiles(qh[:, MLA_HEADS * MLA_NOPE:], tm_ref, MLA_ROPE // 2) * MLA_SCALE
    c_ref[0] = _rms(seg(PR_KVD), kvnorm_ref[...])
    kr_ref[0] = _rope_tiles(_dot_nt(xb, wkrp_ref[...]), tm_ref, MLA_ROPE // 2)[:, :MLA_ROPE]

    rows = qn.shape[0]
    lo = _lane_lo(rows)
    z = jnp.zeros((rows, LANES), BF16)
    parts = []
    for j in range(MLA_HEADS // 2):
        pair = qn[:, j * LANES:(j + 1) * LANES]
        parts.append(_dot(jnp.where(lo, pair, z), w2uk_ref[j]) * MLA_SCALE)
        parts.append(_dot(jnp.where(lo, z, pair), w2uk_ref[j]) * MLA_SCALE)
    qlat_ref[0] = jnp.concatenate(parts, axis=1)


def _in_project_rows(x3, tabs, wts):
    b, t, _ = x3.shape
    tq, tkv, tmla = tabs
    row = lambda w, dt: jax.ShapeDtypeStruct((b, t, w), dt)
    names = ["q", "qrot", "gate", "cmp", "slc", "win", "slckv", "winkv", "ga", "gb", "ma", "mb", "qpe", "c", "kr",
             "qlat"]
    out_shape = [row(512, BF16), row(512, BF16), row(NSA_KV_HEADS * GATE_ROWS, F32), row(256, F32), row(256, F32),
                 row(256, F32),
                 row(512, BF16), row(512, BF16), row(512, BF16), row(512, BF16), row(1024, BF16), row(1024, BF16),
                 row(MLA_HEADS * MLA_ROPE, F32), row(MLA_KV_LORA, F32), row(MLA_ROPE, F32),
                 row(MLA_HEADS * MLA_KV_LORA, F32)]
    tok = lambda w: pl.BlockSpec((1, t, w), lambda bb: (bb, 0, 0))
    tab = _full_spec((3, t, LANES))
    in_specs = [tok(D_MODEL), _full_spec((1, D_MODEL)), _full_spec((PT_ROWS, D_MODEL)), _full_spec((LANES, D_MODEL)),
                _full_spec((D_MODEL, PR_COLS)), tab, tab, tab, _full_spec((1, MLA_Q_LORA)), _full_spec((MLA_Q_LORA, MLA_HEADS * (MLA_NOPE + MLA_ROPE))),
                _full_spec((1, MLA_KV_LORA)), _full_spec((MLA_HEADS // 2, LANES, MLA_KV_LORA))]
    res = pl.pallas_call(
        _inproj_rows_kernel, grid=(b,), in_specs=in_specs, out_specs=[tok(s.shape[2]) for s in out_shape],
        out_shape=out_shape,
        compiler_params=pltpu.CompilerParams(dimension_semantics=("arbitrary",), vmem_limit_bytes=VMEM_LIMIT),
        name="in_project_rows",
    )(x3, wts["npre"], wts["w_t"], wts["w_krp4"], wts["w_cols"], tq, tkv, tmla, wts["qnorm"], wts["wqup"],
      wts["kvnorm"], wts["w2uk"])
    return dict(zip(names, res))


def _rope_rows(x, cos, sin, half):
    x1, x2 = x[0:half], x[half:2 * half]
    parts = [x1 * cos - x2 * sin, x1 * sin + x2 * cos]
    if x.shape[0] > 2 * half:
        parts.append(x[2 * half:])
    return jnp.concatenate(parts, axis=0)


def _store_chunks(ref, v):
    tk = ref.shape[3]
    for j in range(ref.shape[1]):
        ref[0, j] = v[:, j * tk:(j + 1) * tk]


def _inproj_cols_kernel(x_ref, npre_ref, wt_ref, w_ref, ropeq_ref, ropem_ref, pe_ref, qnorm_ref, wqupt_ref,
                        kvnorm_ref, pool_ref, wkx_ref, wvt_ref,
                        qt_ref, qrt_ref, gate_ref, cmp_ref, slc_ref, win_ref, slcb_ref, winb_ref, slcr_ref,
                        winr_ref, kc_ref, ga_ref, gb_ref, ma_ref, mb_ref, qmt_ref, c_ref, kmla_ref, vmt_ref,
                        kr_ref, kn_ref):
    tm = x_ref.shape[1]
    xb = _rms(x_ref[0], npre_ref[...]).astype(BF16)
    segt = lambda lohi: _dot_nt(wt_ref[lohi[0]:lohi[1], :], xb)
    seg = lambda lohi: _dot(xb, w_ref[:, lohi[0]:lohi[1]])
    cq, sq = ropeq_ref[0], ropeq_ref[1]
    cm, sm = ropem_ref[0], ropem_ref[1]
    hq, hm = ROT_DIM // 2, MLA_ROPE // 2

    qt = segt(PT_Q)
    qt_ref[0] = (qt * (NSA_SCALE * LOG2E)).astype(BF16)
    qrt = jnp.concatenate([_rope_rows(qt[h * HD:(h + 1) * HD], cq, sq, hq) for h in range(NSA_HEADS)], axis=0)
    qrt_ref[0] = (qrt * (NSA_SCALE * LOG2E)).astype(BF16)

    cmpt = segt(PT_CMP)
    cmp_ref[0] = cmpt
    hi, lo = _split_bf16(cmpt)
    pool = pool_ref[...]
    pooled = (_dot_nt(pool, hi) + _dot_nt(pool, lo))[0:tm // CMP_BLOCK]
    kc_ref[0] = (pooled + jnp.sum(pe_ref[...], axis=0, keepdims=True)) * (1.0 / CMP_BLOCK)

    def rope_kv(v):
        parts = []
        for g in range(NSA_KV_HEADS):
            parts.append(_rope_rows(v[g * LANES:g * LANES + HD], cq, sq, hq))
            parts.append(v[g * LANES + HD:(g + 1) * LANES])
        return jnp.concatenate(parts, axis=0)

    slct = rope_kv(segt(PT_SLC))
    slc_ref[0] = slct
    _store_chunks(slcb_ref, slct.astype(BF16))
    slcr_ref[0] = slct.T.astype(BF16)
    wint = rope_kv(segt(PT_WIN))
    win_ref[0] = wint
    _store_chunks(winb_ref, wint.astype(BF16))
    winr_ref[0] = wint.T.astype(BF16)

    gate_ref[0] = jax.nn.sigmoid(segt(PT_GN))
    krt = _rope_rows(segt(PT_KRP), cm, sm, hm)
    kr_ref[0] = krt
    c = _rms(seg(PR_KVD), kvnorm_ref[...])
    c_ref[0] = c
    c_b = c.astype(BF16)
    kr_rows = jnp.concatenate([krt, jnp.zeros((LANES - MLA_ROPE, tm), F32)], axis=0).T
    ckr = jnp.concatenate([c_b, kr_rows.astype(BF16)], axis=1)
    kmla = _dot(ckr, wkx_ref[...]).astype(BF16)
    kmla_ref[0] = kmla
    kf = kmla.astype(F32)
    norms = []
    for h in range(MLA_HEADS):
        n2 = jnp.sum(kf[:, h * LANES:(h + 1) * LANES] ** 2, axis=1, keepdims=True)
        norms.append(jnp.broadcast_to(jnp.sqrt(jnp.max(n2, axis=0, keepdims=True)), (1, LANES)))
    kn_ref[0] = jnp.concatenate(norms, axis=0)
    _store_chunks(vmt_ref, _dot_nt(wvt_ref[...], c_b).astype(BF16))

    qd = _rms(seg(PR_QD), qnorm_ref[...]).astype(BF16)
    qht = _dot_nt(wqupt_ref[...], qd)
    n_nope = MLA_HEADS * MLA_NOPE
    zq = jnp.zeros((LANES - MLA_NOPE - MLA_ROPE, tm), F32)
    parts = []
    for h in range(MLA_HEADS):
        parts += [qht[h * MLA_NOPE:(h + 1) * MLA_NOPE],
                  _rope_rows(qht[n_nope + h * MLA_ROPE:n_nope + (h + 1) * MLA_ROPE], cm, sm, hm), zq]
    qmt_ref[0] = (jnp.concatenate(parts, axis=0) * (MLA_SCALE * LOG2E)).astype(BF16)

    ga_ref[0] = seg(PR_GA).astype(BF16)
    gb_ref[0] = seg(PR_GB).astype(BF16)
    ma_ref[0] = seg(PR_MA).astype(BF16)
    mb_ref[0] = seg(PR_MB).astype(BF16)


def _in_project_cols(x3, ropes, wts):
    b, t, _ = x3.shape
    tm = PROMPT_TM
    nt = t // tm
    ropeq, ropem = ropes
    pool = np.zeros((16, tm), np.float32)
    for s in range(tm):
        pool[s // CMP_BLOCK, s] = 1.0
    sds = jax.ShapeDtypeStruct
    rows = lambda w: pl.BlockSpec((1, tm, w), lambda i, bb: (bb, i, 0))
    cols = lambda w: pl.BlockSpec((1, w, tm), lambda i, bb: (bb, 0, i))
    chunk = lambda w, tk: pl.BlockSpec((1, tm // tk, w, tk), lambda i, bb: (bb, i, 0, 0))
    outs = [
        ("qt", sds((b, NSA_WIDTH, t), BF16), cols(NSA_WIDTH)),
        ("qrt", sds((b, NSA_WIDTH, t), BF16), cols(NSA_WIDTH)),
        ("gate", sds((b, NSA_KV_HEADS * GATE_ROWS, t), F32), cols(NSA_KV_HEADS * GATE_ROWS)),
        ("cmp", sds((b, KV_WIDTH, t), F32), cols(KV_WIDTH)),
        ("slc", sds((b, KV_WIDTH, t), F32), cols(KV_WIDTH)),
        ("win", sds((b, KV_WIDTH, t), F32), cols(KV_WIDTH)),
        ("slcb", sds((b, t // ATT_TK, KV_WIDTH, ATT_TK), BF16), chunk(KV_WIDTH, ATT_TK)),
        ("winb", sds((b, t // ATT_TK, KV_WIDTH, ATT_TK), BF16), chunk(KV_WIDTH, ATT_TK)),
        ("slcr", sds((b, t, KV_WIDTH), BF16), rows(KV_WIDTH)),
        ("winr", sds((b, t, KV_WIDTH), BF16), rows(KV_WIDTH)),
        ("kc", sds((b, t // CMP_BLOCK, KV_WIDTH), F32),
         pl.BlockSpec((1, tm // CMP_BLOCK, KV_WIDTH), lambda i, bb: (bb, i, 0))),
        ("ga", sds((b, t, NSA_WIDTH), BF16), rows(NSA_WIDTH)),
        ("gb", sds((b, t, MLA_WIDTH), BF16), rows(MLA_WIDTH)),
        ("ma", sds((b, t, D_MODEL), BF16), rows(D_MODEL)),
        ("mb", sds((b, t, D_MODEL), BF16), rows(D_MODEL)),
        ("qmt", sds((b, MLA_HEADS * LANES, t), BF16), cols(MLA_HEADS * LANES)),
        ("c", sds((b, t, MLA_KV_LORA), F32), rows(MLA_KV_LORA)),
        ("kmla", sds((b, t, MLA_HEADS * LANES), BF16), rows(MLA_HEADS * LANES)),
        ("vmt", sds((b, t // MLA_TK, MLA_WIDTH, MLA_TK), BF16), chunk(MLA_WIDTH, MLA_TK)),
        ("kr", sds((b, MLA_ROPE, t), F32), cols(MLA_ROPE)),
        ("kn", sds((b, nt * MLA_HEADS, LANES), F32), pl.BlockSpec((1, MLA_HEADS, LANES), lambda i, bb: (bb, i, 0))),
    ]
    rope_spec = lambda half: pl.BlockSpec((2, half, tm), lambda i, bb: (0, 0, i))
    in_specs = [rows(D_MODEL), _full_spec((1, D_MODEL)), _full_spec((PT_ROWS, D_MODEL)),
                _full_spec((D_MODEL, PR_COLS)), rope_spec(ROT_DIM // 2), rope_spec(MLA_ROPE // 2),
                _full_spec((CMP_BLOCK, KV_WIDTH)), _full_spec((1, MLA_Q_LORA)),
                _full_spec((MLA_HEADS * (MLA_NOPE + MLA_ROPE), MLA_Q_LORA)), _full_spec((1, MLA_KV_LORA)),
                _full_spec((16, tm)), _full_spec((MLA_KV_LORA + LANES, MLA_HEADS * LANES)),
                _full_spec((MLA_WIDTH, MLA_KV_LORA))]
    res = pl.pallas_call(
        _inproj_cols_kernel, grid=(nt, b), in_specs=in_specs, out_specs=[o[2] for o in outs],
        out_shape=[o[1] for o in outs],
        compiler_params=pltpu.CompilerParams(dimension_semantics=("arbitrary", "arbitrary"),
                                             vmem_limit_bytes=VMEM_LIMIT),
        name="in_project_cols",
    )(x3, wts["npre"], wts["w_t"], wts["w_cols"], ropeq, ropem, wts["pe"], wts["qnorm"], wts["wqupt"],
      wts["kvnorm"], jnp.asarray(pool, BF16), wts["wkx"], wts["wvt"])
    return dict(zip([o[0] for o in outs], res))


def _online_update(s, vt, m, l, acc):
    d, keys = vt.shape
    m_new = jnp.maximum(m, jnp.max(s, axis=0, keepdims=True))
    p = jnp.exp2(s - m_new).astype(BF16)
    alpha = jnp.exp2(m - m_new)
    pv = _dot(jnp.concatenate([vt, jnp.ones((PAD_ROWS, keys), BF16)], axis=0), p)
    return m_new, alpha * l + pv[d:d + 1], alpha * acc + pv[0:d]


def _nsa_prompt_kernel(n_blk, qt_ref, qrt_ref, gate_ref, kc_ref, ks_ref, kst_ref, kw_ref, kwt_ref, o_ref, sb_ref,
                       kn_ref):
    i = pl.program_id(2)
    tq, tk = ATT_TQ, ATT_TK
    nl = NSA_GROUP * tq
    zq = jnp.zeros((HD, tq), BF16)

    def widen(qt):
        return jnp.concatenate([jnp.concatenate([qt[r * HD:(r + 1) * HD], zq], axis=0)
                                for r in range(NSA_GROUP)], axis=1)

    qc = widen(qt_ref[0])
    qr = widen(qrt_ref[0])

    n_cmp = 2 * n_blk
    kc = kc_ref[0]
    s_c = _dot(kc.astype(BF16), qc)
    rho = lax.broadcasted_iota(jnp.int32, (n_cmp, nl), 0)
    cmp_idx = jnp.where(rho < n_blk, 2 * rho, 2 * (rho - n_blk) + 1)
    t_l = i * tq + (lax.broadcasted_iota(jnp.int32, (n_cmp, nl), 1) & (tq - 1))
    mask_c = cmp_idx * CMP_BLOCK + (CMP_BLOCK - 1) <= t_l
    s_c = jnp.where(mask_c, s_c, NEG)
    e_c = jnp.where(mask_c, jnp.exp2(s_c - jnp.max(s_c, axis=0, keepdims=True)), 0.0)
    p_c = e_c / jnp.maximum(jnp.sum(e_c, axis=0, keepdims=True), 1e-30)
    kct = jnp.concatenate([kc, jnp.zeros((LANES - n_cmp, LANES), F32)], axis=0).T
    p_pad = jnp.concatenate([p_c, jnp.zeros((LANES - n_cmp, nl), F32)], axis=0)
    o_c = _dot(kct[HD:2 * HD].astype(BF16), p_pad.astype(BF16))

    imp = p_c[:, 0:tq]
    for r in range(1, NSA_GROUP):
        imp = imp + p_c[:, r * tq:(r + 1) * tq]
    imp_blk = imp[0:n_blk] + imp[n_blk:n_cmp]
    blk = lax.broadcasted_iota(jnp.int32, (n_blk, tq), 0)
    t_q = i * tq + lax.broadcasted_iota(jnp.int32, (n_blk, tq), 1)
    ahead_of = t_q - blk * SEL_BLOCK
    score = jnp.where(blk == 0, FORCE_SCORE,
                      jnp.where(ahead_of < 0, -FORCE_SCORE, jnp.where(ahead_of < SEL_BLOCK, FORCE_SCORE, imp_blk)))
    def count_ahead():
        rank = jnp.zeros((n_blk, tq), F32)
        for j in range(n_blk):
            other = score[j:j + 1, :]
            tie = jnp.where(blk > j, 1.0, 0.0)
            rank = rank + jnp.where(other > score, 1.0, jnp.where(other == score, tie, 0.0))
        return rank

    few_blocks = (i + 1) * tq <= N_SELECT * SEL_BLOCK
    rank = lax.cond(few_blocks, lambda: jnp.zeros((n_blk, tq), F32), count_ahead)
    sb_ref[...] = jnp.where(rank < N_SELECT, jnp.where(score > -1.0, 0.0, NEG), NEG)

    key_r = lax.broadcasted_iota(jnp.int32, (tk, tq), 0)
    t_k = i * tq + lax.broadcasted_iota(jnp.int32, (tk, tq), 1)
    n_chunks = ((i + 1) * tq + tk - 1) // tk
    c_lo = jnp.maximum(i * tq - WINDOW, 0) // tk
    per_chunk = tk // SEL_BLOCK
    tile = lambda bias: jnp.concatenate([bias] * NSA_GROUP, axis=1)
    ones = jnp.ones((PAD_ROWS, tk), BF16)

    def sel_bias(c):
        rows = [jnp.broadcast_to(sb_ref[pl.ds(c * per_chunk + j, 1), :], (SEL_BLOCK, tq)) for j in range(per_chunk)]
        return jnp.concatenate(rows, axis=0)

    def values(ref, c):
        return jnp.concatenate([ref[0, c, HD:2 * HD, :], ones], axis=0)

    qf = qrt_ref[0].astype(F32)
    q_norm2 = [jnp.sum(qf[r * HD:(r + 1) * HD] ** 2, axis=0, keepdims=True) for r in range(NSA_GROUP)]
    q_norm = jnp.sqrt(jnp.maximum(jnp.maximum(q_norm2[0], q_norm2[1]), jnp.maximum(q_norm2[2], q_norm2[3])))

    def key_norm(ref):
        kf = ref[0, :, 0:HD, :].astype(F32)
        n2 = jnp.max(jnp.sum(kf * kf, axis=1, keepdims=True), axis=0)
        return jnp.broadcast_to(jnp.sqrt(jnp.max(n2, axis=1, keepdims=True)), (1, LANES))

    @pl.when(i == 0)
    def _():
        kn_ref[0:1, :] = key_norm(kst_ref)
        kn_ref[1:2, :] = key_norm(kwt_ref)

    wide = lambda row: jnp.concatenate([row] * (tq // LANES), axis=1)
    bound_s = q_norm * (wide(kn_ref[0:1, :]) * 1.001) + 1e-3
    bound_w = q_norm * (wide(kn_ref[1:2, :]) * 1.001) + 1e-3
    safe = jnp.maximum(jnp.max(bound_s), jnp.max(bound_w)) <= SHIFT_LIMIT

    def attend_shifted():
        def past_probs(c):
            base = pl.multiple_of(c * tk, tk)
            s = _dot(ks_ref[0, pl.ds(base, tk), :], qr) + tile(sel_bias(c) - bound_s)
            return jnp.exp2(s).astype(BF16)

        def past_body(c, acc):
            return acc + _dot(values(kst_ref, c), past_probs(c))

        def past_pair(cc, acc):
            p0, p1 = past_probs(2 * cc), past_probs(2 * cc + 1)
            return acc + _dot(values(kst_ref, 2 * cc), p0) + _dot(values(kst_ref, 2 * cc + 1), p1)

        def near_probs(c):
            base = pl.multiple_of(c * tk, tk)
            s = _dot(ks_ref[0, pl.ds(base, tk), :], qr)
            w = _dot(kw_ref[0, pl.ds(base, tk), :], qr)
            dist = t_k - (base + key_r)
            causal = jnp.where(dist >= 0, 0.0, NEG)
            p_s = jnp.exp2(s + tile(sel_bias(c) + causal - bound_s)).astype(BF16)
            p_w = jnp.exp2(w + tile(jnp.where(dist <= WINDOW, causal, NEG) - bound_w)).astype(BF16)
            return p_s, p_w

        def near_body(c, carry):
            p_s, p_w = near_probs(c)
            return carry[0] + _dot(values(kst_ref, c), p_s), carry[1] + _dot(values(kwt_ref, c), p_w)

        def near_pair(cc, carry):
            c0 = c_lo + 2 * cc
            (ps0, pw0), (ps1, pw1) = near_probs(c0), near_probs(c0 + 1)
            return (carry[0] + _dot(values(kst_ref, c0), ps0) + _dot(values(kst_ref, c0 + 1), ps1),
                    carry[1] + _dot(values(kwt_ref, c0), pw0) + _dot(values(kwt_ref, c0 + 1), pw1))

        zero = jnp.zeros((HD + PAD_ROWS, nl), F32)
        far = lax.fori_loop(0, c_lo // 2, past_pair, zero)
        far = lax.fori_loop(2 * (c_lo // 2), c_lo, past_body, far)
        n_pairs = (n_chunks - c_lo) // 2
        acc = lax.fori_loop(0, n_pairs, near_pair, (far, zero))
        a_s, a_w = lax.fori_loop(c_lo + 2 * n_pairs, n_chunks, near_body, acc)
        inv_s, inv_w = 1.0 / a_s[HD:HD + 1], 1.0 / a_w[HD:HD + 1]
        return a_s[0:HD] * inv_s, a_w[0:HD] * inv_w

    def attend_online():
        init = (jnp.full((1, nl), NEG, F32), jnp.zeros((1, nl), F32), jnp.zeros((HD, nl), F32))

        def past_body(c, carry):
            base = pl.multiple_of(c * tk, tk)
            s = _dot(ks_ref[0, pl.ds(base, tk), :], qr) + tile(sel_bias(c))
            return _online_update(s, kst_ref[0, c, HD:2 * HD, :], *carry)

        def near_body(c, carry):
            base = pl.multiple_of(c * tk, tk)
            s = _dot(ks_ref[0, pl.ds(base, tk), :], qr)
            w = _dot(kw_ref[0, pl.ds(base, tk), :], qr)
            dist = t_k - (base + key_r)
            causal = jnp.where(dist >= 0, 0.0, NEG)
            s = s + tile(sel_bias(c) + causal)
            w = w + tile(jnp.where(dist <= WINDOW, causal, NEG))
            return (_online_update(s, kst_ref[0, c, HD:2 * HD, :], *carry[:3])
                    + _online_update(w, kwt_ref[0, c, HD:2 * HD, :], *carry[3:]))

        far = lax.fori_loop(0, c_lo, past_body, init)
        _, l_s, a_s, _, l_w, a_w = lax.fori_loop(c_lo, n_chunks, near_body, far + init)
        return a_s / l_s, a_w / l_w

    o_s, o_w = lax.cond(safe, attend_shifted, attend_online)

    gate = gate_ref[0]
    heads = []
    for r in range(NSA_GROUP):
        sl = slice(r * tq, (r + 1) * tq)
        heads.append(gate[3 * r:3 * r + 1] * o_c[:, sl] + gate[3 * r + 1:3 * r + 2] * o_s[:, sl]
                     + gate[3 * r + 2:3 * r + 3] * o_w[:, sl])
    o_ref[0] = jnp.concatenate(heads, axis=0).T.astype(BF16)


def _nsa_prompt(p, b, t):
    tq, tk = ATT_TQ, ATT_TK
    n_blk = t // SEL_BLOCK
    assert 2 * n_blk <= LANES and t % tk == 0
    kc = p["kc"].reshape(b, n_blk, 2, KV_WIDTH).transpose(0, 2, 1, 3).reshape(b, 2 * n_blk, KV_WIDTH)
    qspec = pl.BlockSpec((1, NSA_GROUP * HD, tq), lambda bb, g, i: (bb, g, i))
    rm = pl.BlockSpec((1, t, LANES), lambda bb, g, i: (bb, 0, g))
    fm = pl.BlockSpec((1, t // tk, LANES, tk), lambda bb, g, i: (bb, 0, g, 0))
    return pl.pallas_call(
        functools.partial(_nsa_prompt_kernel, n_blk),
        grid=(b, NSA_KV_HEADS, t // tq),
        in_specs=[qspec, qspec, pl.BlockSpec((1, GATE_ROWS, tq), lambda bb, g, i: (bb, g, i)),
                  pl.BlockSpec((1, 2 * n_blk, LANES), lambda bb, g, i: (bb, 0, g)), rm, fm, rm, fm],
        out_specs=pl.BlockSpec((1, tq, NSA_GROUP * HD), lambda bb, g, i: (bb, i, g)),
        out_shape=jax.ShapeDtypeStruct((b, t, NSA_WIDTH), BF16),
        scratch_shapes=[pltpu.VMEM((n_blk, tq), F32), pltpu.VMEM((8, LANES), F32)],
        compiler_params=pltpu.CompilerParams(dimension_semantics=("arbitrary",) * 3, vmem_limit_bytes=VMEM_LIMIT),
        name="nsa_prompt",
    )(p["qt"], p["qrt"], p["gate"], kc, p["slcr"], p["slcb"], p["winr"], p["winb"])


MLA_TQ = 256


def _mla_prompt_kernel(qmt_ref, k_ref, vt_ref, kn_ref, o_ref, m_ref, l_ref, acc_ref):
    i = pl.program_id(1)
    tq, tk = MLA_TQ, MLA_TK
    heads = range(MLA_HEADS)
    m_ref[...] = jnp.full(m_ref.shape, NEG, F32)
    l_ref[...] = jnp.zeros(l_ref.shape, F32)
    acc_ref[...] = jnp.zeros(acc_ref.shape, F32)
    key_r = lax.broadcasted_iota(jnp.int32, (tk, tq), 0)
    t_k = i * tq + lax.broadcasted_iota(jnp.int32, (tk, tq), 1)
    n_chunks = ((i + 1) * tq + tk - 1) // tk
    n_past = (i * tq + 1) // tk
    ones = jnp.ones((PAD_ROWS, tk), BF16)

    def keys(c_base, h):
        return k_ref[0, pl.ds(c_base, tk), h * LANES:(h + 1) * LANES]

    kn = kn_ref[0]
    k_max = kn[0:MLA_HEADS]
    for j in range(1, kn.shape[0] // MLA_HEADS):
        k_max = jnp.maximum(k_max, kn[j * MLA_HEADS:(j + 1) * MLA_HEADS])
    bounds = []
    for h in heads:
        qf = qmt_ref[0, h * LANES:(h + 1) * LANES, :].astype(F32)
        q_norm = jnp.sqrt(jnp.sum(qf * qf, axis=0, keepdims=True))
        k_row = jnp.concatenate([k_max[h:h + 1]] * (tq // LANES), axis=1)
        bounds.append(q_norm * (k_row * 1.001) + 1e-3)
    worst = bounds[0]
    for bd in bounds[1:]:
        worst = jnp.maximum(worst, bd)
    safe = jnp.max(worst) <= SHIFT_LIMIT

    def shifted_step(c, masked):
        base = pl.multiple_of(c * tk, tk)
        old = [acc_ref[h] for h in heads]
        ps = []
        for h in heads:
            s = _dot(keys(base, h), qmt_ref[0, h * LANES:(h + 1) * LANES, :]) - bounds[h]
            ps.append(jnp.exp2(jnp.where(base + key_r <= t_k, s, NEG) if masked else s).astype(BF16))
        new = [old[h] + _dot(jnp.concatenate([vt_ref[0, c, h * MLA_V:(h + 1) * MLA_V, :], ones], axis=0), ps[h])
               for h in heads]
        for h in heads:
            acc_ref[h] = new[h]
        return 0

    def online_step(c, masked):
        base = pl.multiple_of(c * tk, tk)
        old = [(m_ref[h], l_ref[h], acc_ref[h, 0:MLA_V]) for h in heads]
        scores = []
        for h in heads:
            s = _dot(keys(base, h), qmt_ref[0, h * LANES:(h + 1) * LANES, :])
            scores.append(jnp.where(base + key_r <= t_k, s, NEG) if masked else s)
        new = [_online_update(scores[h], vt_ref[0, c, h * MLA_V:(h + 1) * MLA_V, :], *old[h]) for h in heads]
        for h in heads:
            m_ref[h], l_ref[h], acc_ref[h, 0:MLA_V] = new[h]
        return 0

    def run(step):
        lax.fori_loop(0, n_past, lambda c, z: step(c, False), 0)
        lax.fori_loop(n_past, n_chunks, lambda c, z: step(c, True), 0)

    def shifted():
        run(shifted_step)
        return tuple(acc_ref[h, 0:MLA_V] * (1.0 / acc_ref[h, MLA_V:MLA_V + 1]) for h in heads)

    def online():
        run(online_step)
        return tuple(acc_ref[h, 0:MLA_V] / l_ref[h] for h in heads)

    outs = lax.cond(safe, shifted, online)
    o_ref[0] = jnp.concatenate(outs, axis=0).T.astype(BF16)


def _mla_prompt(p, wts, b, t):
    tq, tk = MLA_TQ, MLA_TK
    assert t % tq == 0
    return pl.pallas_call(
        _mla_prompt_kernel,
        grid=(b, t // tq),
        in_specs=[pl.BlockSpec((1, MLA_HEADS * LANES, tq), lambda bb, i: (bb, 0, i)),
                  pl.BlockSpec((1, t, MLA_HEADS * LANES), lambda bb, i: (bb, 0, 0)),
                  pl.BlockSpec((1, t // tk, MLA_WIDTH, tk), lambda bb, i: (bb, 0, 0, 0)),
                  pl.BlockSpec((1,) + p["kn"].shape[1:], lambda bb, i: (bb, 0, 0))],
        out_specs=pl.BlockSpec((1, tq, MLA_WIDTH), lambda bb, i: (bb, i, 0)),
        out_shape=jax.ShapeDtypeStruct((b, t, MLA_WIDTH), BF16),
        scratch_shapes=[pltpu.VMEM((MLA_HEADS, 1, tq), F32), pltpu.VMEM((MLA_HEADS, 1, tq), F32),
                        pltpu.VMEM((MLA_HEADS, MLA_V + PAD_ROWS, tq), F32)],
        compiler_params=pltpu.CompilerParams(dimension_semantics=("arbitrary",) * 2, vmem_limit_bytes=VMEM_LIMIT),
        name="mla_prompt",
    )(p["qmt"], p["kmla"], p["vmt"], p["kn"])


def _merge_kernel(from_latent, x_ref, oa_ref, ob_ref, ga_ref, gb_ref, ma_ref, mb_ref, wpa_ref, wpb_ref, wout_ref,
                  npost_ref, wuv_ref, y_ref):
    if from_latent:
        lat = ob_ref[0].astype(BF16)
        parts = []
        for j in range(MLA_HEADS // 2):
            parts.append(_dot(lat[:, 2 * j * MLA_KV_LORA:(2 * j + 1) * MLA_KV_LORA], wuv_ref[2 * j])
                         + _dot(lat[:, (2 * j + 1) * MLA_KV_LORA:(2 * j + 2) * MLA_KV_LORA], wuv_ref[2 * j + 1]))
        o_b = jnp.concatenate(parts, axis=1)
    else:
        o_b = ob_ref[0].astype(F32)
    ga = ga_ref[0].astype(F32)
    gb = gb_ref[0].astype(F32)
    pa = _dot((oa_ref[0].astype(F32) * (ga * jax.nn.sigmoid(ga))).astype(BF16), wpa_ref[...])
    pb = _dot((o_b * (gb * jax.nn.sigmoid(gb))).astype(BF16), wpb_ref[...])
    h = jax.nn.sigmoid(ma_ref[0].astype(F32)) * pa + jax.nn.sigmoid(mb_ref[0].astype(F32)) * pb
    z = _dot(h.astype(BF16), wout_ref[...])
    y_ref[0] = x_ref[0] + _rms(z, npost_ref[...])


def _merge(x3, o_a, o_b, p, wts, tm, from_latent):
    b, t, _ = x3.shape
    tok = lambda w: pl.BlockSpec((1, tm, w), lambda bb, i: (bb, i, 0))
    return pl.pallas_call(
        functools.partial(_merge_kernel, from_latent),
        grid=(b, t // tm),
        in_specs=[tok(D_MODEL), tok(NSA_WIDTH), tok(o_b.shape[2]), tok(NSA_WIDTH), tok(MLA_WIDTH), tok(D_MODEL),
                  tok(D_MODEL), _full_spec((NSA_WIDTH, D_MODEL)), _full_spec((MLA_WIDTH, D_MODEL)),
                  _full_spec((D_MODEL, D_MODEL)), _full_spec((1, D_MODEL)),
                  _full_spec((MLA_HEADS, MLA_KV_LORA, LANES))],
        out_specs=tok(D_MODEL),
        out_shape=jax.ShapeDtypeStruct((b, t, D_MODEL), F32),
        compiler_params=pltpu.CompilerParams(dimension_semantics=("arbitrary",) * 2, vmem_limit_bytes=VMEM_LIMIT),
        name="merge",
    )(x3, o_a, o_b, p["ga"], p["gb"], p["ma"], p["mb"], wts["wpa"], wts["wpb"], wts["wout"], wts["npost"],
      wts["wuv"])


PAGES_PER_STEP = 32
POOL_PAGES = 4


def _page_copy(source, buf_ref, sem_ref, step, slot, k):
    return pltpu.make_async_copy(source(step, k), buf_ref.at[slot, k], sem_ref.at[slot])


def _table_pages(pt_ref, cache_ref):
    return lambda step, k: cache_ref.at[pt_ref[step * PAGES_PER_STEP + k]]


def _paged_pipeline(streams):
    step = pl.program_id(0) * pl.num_programs(1) + pl.program_id(1)
    last = pl.num_programs(0) * pl.num_programs(1) - 1
    slot = step % 2

    @pl.when(step == 0)
    def _():
        for source, buf_ref, sem_ref in streams:
            for k in range(buf_ref.shape[1]):
                _page_copy(source, buf_ref, sem_ref, 0, 0, k).start()

    @pl.when(step < last)
    def _():
        for source, buf_ref, sem_ref in streams:
            for k in range(buf_ref.shape[1]):
                _page_copy(source, buf_ref, sem_ref, step + 1, 1 - slot, k).start()

    for source, buf_ref, sem_ref in streams:
        for k in range(buf_ref.shape[1]):
            _page_copy(source, buf_ref, sem_ref, step, slot, k).wait()
    return slot


XLU_POOL_PAGES = 16
MXU_POOL_GROUP = 8


def _compress_pages_kernel(pt_ref, cache_ref, pool_ref, pe_ref, o_ref, buf_ref, sem_ref):
    slot = _paged_pipeline([(_table_pages(pt_ref, cache_ref), buf_ref, sem_ref)])
    pe_sum = jnp.sum(pe_ref[...], axis=0, keepdims=True)
    pool = pool_ref[...]
    groups = [jnp.concatenate([buf_ref[slot, k + j] for j in range(MXU_POOL_GROUP)], axis=1)
              for k in range(XLU_POOL_PAGES, PAGES_PER_STEP, MXU_POOL_GROUP)]
    splits = [_split_bf16(x) for x in groups]
    by_mxu = [_dot_nt(pool, hi) + _dot_nt(pool, lo) for hi, lo in splits]
    by_xlu = []
    for k in range(XLU_POOL_PAGES):
        rows = buf_ref[slot, k].T
        by_xlu.append(jnp.sum(rows.reshape(rows.shape[0] // CMP_BLOCK, CMP_BLOCK, KV_WIDTH), axis=1))
    o_ref[0] = (jnp.concatenate(by_xlu + by_mxu, axis=0) + pe_sum) * (1.0 / CMP_BLOCK)


def _compress_pages(cache_t, pt_flat, pe, b, n_pages):
    page = cache_t.shape[2]
    per_step = PAGES_PER_STEP * page // CMP_BLOCK
    pool = np.zeros((MXU_POOL_GROUP * page // CMP_BLOCK, MXU_POOL_GROUP * page), np.float32)
    for s in range(pool.shape[1]):
        pool[s // CMP_BLOCK, s] = 1.0
    gs = pltpu.PrefetchScalarGridSpec(
        num_scalar_prefetch=1, grid=(b, n_pages // PAGES_PER_STEP),
        in_specs=[pl.BlockSpec(memory_space=pl.ANY),
                  pl.BlockSpec(pool.shape, lambda bb, c, pt: (0, 0)),
                  pl.BlockSpec((CMP_BLOCK, KV_WIDTH), lambda bb, c, pt: (0, 0))],
        out_specs=pl.BlockSpec((1, per_step, KV_WIDTH), lambda bb, c, pt: (bb, c, 0)),
        scratch_shapes=[pltpu.VMEM((2, PAGES_PER_STEP, KV_WIDTH, page), F32), pltpu.SemaphoreType.DMA((2,))])
    return pl.pallas_call(
        _compress_pages_kernel, grid_spec=gs,
        out_shape=jax.ShapeDtypeStruct((b, n_pages * page // CMP_BLOCK, KV_WIDTH), F32),
        compiler_params=pltpu.CompilerParams(dimension_semantics=("arbitrary",) * 2, vmem_limit_bytes=VMEM_LIMIT),
        name="compress_pages",
    )(pt_flat, cache_t, jnp.asarray(pool, BF16), pe)


def _pad_rows(v, rows):
    return jnp.concatenate([v, jnp.zeros((rows - v.shape[0], v.shape[1]), v.dtype)], axis=0)


def _stack_heads(qv, lo):
    a, b = qv[:, :LANES], qv[:, LANES:]
    z = jnp.zeros_like(a)
    return jnp.concatenate([jnp.where(lo, a, z), jnp.where(lo, z, a),
                            jnp.where(lo, b, z), jnp.where(lo, z, b)], axis=0)


def _sample_select_kernel(n_cmp, q_ref, kc_ref, oc_ref, idx_ref, imp_ref):
    bb = pl.program_id(0)
    lo1 = _lane_lo(1)
    lo_c = _lane_lo(n_cmp)
    q = q_ref[0]
    kc = kc_ref[0]
    even = (lax.broadcasted_iota(jnp.int32, (1, LANES), 1) & 1) == 0
    for g in range(NSA_KV_HEADS):
        kk, vv = _dup_kv(kc[:, g * LANES:(g + 1) * LANES], lo_c)
        qs = _pad_rows(_stack_heads(q[:, g * 2 * LANES:(g + 1) * 2 * LANES], lo1), PAD_ROWS)
        s = _dot_nt(qs, kk.astype(BF16))
        e = jnp.exp(s - jnp.max(s, axis=-1, keepdims=True))
        p = e / jnp.sum(e, axis=-1, keepdims=True)
        oc_ref[0, g] = _dot(p.astype(BF16), vv.astype(BF16))
        imp = p[0:1] + p[1:2] + p[2:3] + p[3:4]
        chunks = []
        for k in range(n_cmp // LANES):
            a = imp[:, k * LANES:(k + 1) * LANES]
            chunks.append(a + jnp.where(even, pltpu.roll(a, LANES - 1, 1), pltpu.roll(a, 1, 1)))
        imp_ref[pl.ds(bb * NSA_KV_HEADS + g, 1), :] = jnp.concatenate(chunks, axis=1)

    @pl.when(bb == pl.num_programs(0) - 1)
    def _():
        rows = imp_ref.shape[0]
        blk = lax.broadcasted_iota(jnp.int32, (rows, n_cmp), 1) >> 1
        blk_f = blk.astype(F32)
        slot = lax.broadcasted_iota(jnp.int32, (rows, N_SELECT), 1)
        v = jnp.where(blk == 0, -1.0, imp_ref[...])
        idx = jnp.where(slot == N_SELECT - 1, n_cmp // 2, 0)
        for k in range(1, N_SELECT - 1):
            top = jnp.max(v, axis=-1, keepdims=True)
            jmin = jnp.min(jnp.where(v == top, blk_f, float(n_cmp)), axis=-1, keepdims=True).astype(jnp.int32)
            idx = jnp.where(slot == k, jmin, idx)
            v = jnp.where(blk == jmin, -1.0, v)
        idx_ref[...] = idx


def _sample_select(q, kc_all, b):
    n_cmp = kc_all.shape[1]
    rows = b * NSA_KV_HEADS
    return pl.pallas_call(
        functools.partial(_sample_select_kernel, n_cmp),
        grid=(b,),
        in_specs=[pl.BlockSpec((1, 1, NSA_WIDTH), lambda bb: (bb, 0, 0)),
                  pl.BlockSpec((1, n_cmp, KV_WIDTH), lambda bb: (bb, 0, 0))],
        out_specs=[pl.BlockSpec((1, NSA_KV_HEADS, PAD_ROWS, LANES), lambda bb: (bb, 0, 0, 0)),
                   pl.BlockSpec((rows, N_SELECT), lambda bb: (0, 0))],
        out_shape=[jax.ShapeDtypeStruct((b, NSA_KV_HEADS, PAD_ROWS, LANES), F32),
                   jax.ShapeDtypeStruct((rows, N_SELECT), jnp.int32)],
        scratch_shapes=[pltpu.VMEM((rows, n_cmp), F32)],
        compiler_params=pltpu.CompilerParams(dimension_semantics=("arbitrary",), vmem_limit_bytes=VMEM_LIMIT),
        name="sample_select",
    )(q, kc_all)


def _extra_key_softmax(s_past, vt4_b, s_new, v_new):
    m = jnp.maximum(jnp.max(s_past, axis=-1, keepdims=True), s_new)
    e = jnp.exp(s_past - m)
    e_new = jnp.exp(s_new - m)
    den = jnp.sum(e, axis=-1, keepdims=True) + e_new
    return (_dot_nt(e.astype(BF16), vt4_b) + e_new * v_new) / den


def _sample_attend_kernel(n_pages, idx_ref, pt_ref, slc_ref, qr_ref, newkv_ref, neww_ref, newwf_ref, win_ref,
                          gate_ref, oc_ref, o_ref, wout_ref, buf_ref, sem_ref):
    n_sel = N_SELECT
    bb, g = pl.program_id(0), pl.program_id(1)

    def selected_page(step, k):
        j = idx_ref[step * n_sel + k]
        page = pt_ref[(step // NSA_KV_HEADS) * n_pages + jnp.minimum(j // 2, n_pages - 1)]
        return slc_ref.at[page, pl.ds((step % NSA_KV_HEADS) * LANES, LANES), :]

    slot = _paged_pipeline([(selected_page, buf_ref, sem_ref)])
    pages = [buf_ref.at[slot, k] for k in range(n_sel)]
    lo1 = _lane_lo(1)
    q = qr_ref[0].astype(F32)
    halves = [q[:, 0:LANES], q[:, LANES:2 * LANES]]
    rows = []
    for r in range(NSA_GROUP):
        a = halves[r // 2]
        rows.append(jnp.where(lo1, a if r % 2 == 0 else pltpu.roll(a, HALF, 1), 0.0))
    qs_f = _pad_rows(jnp.concatenate(rows, axis=0), PAD_ROWS)
    qs = qs_f.astype(BF16)
    twice_rows = lambda a: jnp.concatenate([a, a], axis=0)

    s_t = jnp.concatenate([pg[...] for pg in pages], axis=1)
    s_sel = _dot(qs, s_t.astype(BF16))
    base = (bb * NSA_KV_HEADS + g) * n_sel
    biases = []
    for k in range(n_sel - 1):
        odd = (idx_ref[base + k] & 1) == 1
        biases.append(jnp.where(lo1, jnp.where(odd, NEG, 0.0), jnp.where(odd, 0.0, NEG)))
    biases.append(jnp.full((1, LANES), NEG, F32))
    s_sel = s_sel + jnp.concatenate(biases, axis=1)
    nk = newkv_ref[0].astype(F32)
    s_new = jnp.sum(qs_f * nk[:, :LANES], axis=-1, keepdims=True)
    o_s = _extra_key_softmax(s_sel, twice_rows(s_t[HD:2 * HD]).astype(BF16), s_new, nk[:, LANES:])

    w = win_ref[0]
    nw = neww_ref[0].astype(F32)
    s_w = _dot(qs, w.astype(BF16))
    s_wn = jnp.sum(qs_f * nw[:, :LANES], axis=-1, keepdims=True)
    o_w = _extra_key_softmax(s_w, twice_rows(w[HD:2 * HD]).astype(BF16), s_wn, nw[:, LANES:])

    o_c = oc_ref[0, 0]
    gates = gate_ref[0]
    gate = jnp.where(g == 0, gates[:, 0:GATE_ROWS], gates[:, GATE_ROWS:2 * GATE_ROWS])
    heads = []
    for r in range(NSA_GROUP):
        heads.append(gate[:, 3 * r:3 * r + 1] * o_c[r:r + 1] + gate[:, 3 * r + 1:3 * r + 2] * o_s[r:r + 1]
                     + gate[:, 3 * r + 2:3 * r + 3] * o_w[r:r + 1])
    o_ref[0] = jnp.concatenate([jnp.where(lo1, heads[0], heads[1]), jnp.where(lo1, heads[2], heads[3])], axis=1)

    n_feat, n_w = w.shape
    new_row = jnp.broadcast_to(newwf_ref[0], (n_feat, n_feat))
    diag = (lax.broadcasted_iota(jnp.int32, (n_feat, n_feat), 0)
            == lax.broadcasted_iota(jnp.int32, (n_feat, n_feat), 1))
    new_col = jnp.sum(jnp.where(diag, new_row, 0.0), axis=1, keepdims=True)
    last = lax.broadcasted_iota(jnp.int32, (n_feat, LANES), 1) == LANES - 1
    chunks = []
    n_ch = n_w // LANES
    for c in range(n_ch):
        cur = pltpu.roll(w[:, c * LANES:(c + 1) * LANES], LANES - 1, 1)
        if c + 1 < n_ch:
            nxt = pltpu.roll(w[:, (c + 1) * LANES:(c + 2) * LANES], LANES - 1, 1)
        else:
            nxt = jnp.broadcast_to(new_col, (n_feat, LANES))
        chunks.append(jnp.where(last, nxt, cur))
    wout_ref[0] = jnp.concatenate(chunks, axis=1)


def _sample_attend(p, oc, idx_flat, pt_flat, slc_t, win_t, b, n_pages):
    page = slc_t.shape[2]
    win_len = win_t.shape[2]
    assert page == 2 * SEL_BLOCK
    row = lambda w: pl.BlockSpec((1, 1, w), lambda bb, g, idx, pt: (bb, 0, g))
    wspec = pl.BlockSpec((1, LANES, win_len), lambda bb, g, idx, pt: (bb, g, 0))
    gs = pltpu.PrefetchScalarGridSpec(
        num_scalar_prefetch=2, grid=(b, NSA_KV_HEADS),
        in_specs=[pl.BlockSpec(memory_space=pl.ANY),
                  row(2 * LANES), row(2 * LANES), row(2 * LANES), row(LANES), wspec,
                  pl.BlockSpec((1, 1, NSA_KV_HEADS * GATE_ROWS), lambda bb, g, idx, pt: (bb, 0, 0)),
                  pl.BlockSpec((1, 1, PAD_ROWS, LANES), lambda bb, g, idx, pt: (bb, g, 0, 0))],
        out_specs=[row(2 * LANES), wspec],
        scratch_shapes=[pltpu.VMEM((2, N_SELECT, LANES, page), F32), pltpu.SemaphoreType.DMA((2,))])
    return pl.pallas_call(
        functools.partial(_sample_attend_kernel, n_pages), grid_spec=gs,
        out_shape=[jax.ShapeDtypeStruct((b, 1, NSA_WIDTH), F32),
                   jax.ShapeDtypeStruct((b, KV_WIDTH, win_len), F32)],
        compiler_params=pltpu.CompilerParams(dimension_semantics=("arbitrary",) * 2, vmem_limit_bytes=VMEM_LIMIT),
        name="sample_attend",
    )(idx_flat, pt_flat, slc_t, p["qrot"], p["slckv"], p["winkv"], p["win"], win_t, p["gate"], oc)


def _softmax_update(s, v_b, m, l, acc):
    m_new = jnp.maximum(m, jnp.max(s, axis=-1, keepdims=True))
    p = jnp.exp(s - m_new)
    alpha = jnp.exp(m - m_new)
    return m_new, alpha * l + jnp.sum(p, axis=-1, keepdims=True), alpha * acc + _dot(p.astype(BF16), v_b)


def _mla_sample_kernel(pt_ref, lat_ref, krt_ref, qlat_ref, qpe_ref, cnew_ref, krnew_ref, o_ref,
                       m_ref, l_ref, acc_ref, lat_buf, kr_buf, lat_sem, kr_sem):
    n = PAGES_PER_STEP
    slot = _paged_pipeline([(_table_pages(pt_ref, lat_ref), lat_buf, lat_sem),
                            (_table_pages(pt_ref, krt_ref), kr_buf, kr_sem)])
    lat_pages = [lat_buf.at[slot, k] for k in range(n)]
    kr_pages = [kr_buf.at[slot, k] for k in range(n)]
    step = pl.program_id(1)
    qlat = _pad_rows(qlat_ref[0], PAD_ROWS)
    qpe = _pad_rows(qpe_ref[0], PAD_ROWS)

    @pl.when(step == 0)
    def _():
        c_new = cnew_ref[0]
        s_new = (jnp.sum(qlat * c_new, axis=-1, keepdims=True)
                 + jnp.sum(qpe * krnew_ref[0], axis=-1, keepdims=True))
        m_ref[...] = s_new
        l_ref[...] = jnp.ones(l_ref.shape, F32)
        acc_ref[...] = jnp.broadcast_to(c_new, acc_ref.shape)

    qlat_b, qpe_b = qlat.astype(BF16), qpe.astype(BF16)
    groups = range(0, n, POOL_PAGES)
    c_bs = [jnp.concatenate([pg[...] for pg in lat_pages[k:k + POOL_PAGES]], axis=0).astype(BF16) for k in groups]
    kr_bs = [jnp.concatenate([pg[...] for pg in kr_pages[k:k + POOL_PAGES]], axis=1).astype(BF16) for k in groups]
    s_lat = [_dot_nt(qlat_b, c_b) for c_b in c_bs]
    s_pe = [_dot(qpe_b, kr_b) for kr_b in kr_bs]
    scores = [a + r for a, r in zip(s_lat, s_pe)]
    maxes = [jnp.max(s, axis=-1, keepdims=True) for s in scores]
    ps = [jnp.exp(s - mx) for s, mx in zip(scores, maxes)]
    sums = [jnp.sum(p_g, axis=-1, keepdims=True) for p_g in ps]
    accs = [_dot(p_g.astype(BF16), c_b) for p_g, c_b in zip(ps, c_bs)]
    m_old = m_ref[...]
    m = m_old
    for mx in maxes:
        m = jnp.maximum(m, mx)
    alpha = jnp.exp(m_old - m)
    l = alpha * l_ref[...]
    acc = alpha * acc_ref[...]
    for mx, l_g, acc_g in zip(maxes, sums, accs):
        w = jnp.exp(mx - m)
        l = l + w * l_g
        acc = acc + w * acc_g
    m_ref[...] = m
    l_ref[...] = l
    acc_ref[...] = acc

    @pl.when(step == pl.num_programs(1) - 1)
    def _():
        o_ref[0] = (acc / l)[0:MLA_HEADS]


def _mla_sample(p, lat_cache, kr_t, pt_flat, b, n_pages):
    page = lat_cache.shape[1]
    head = lambda w: pl.BlockSpec((1, MLA_HEADS, w), lambda bb, c, pt: (bb, 0, 0))
    row = lambda w: pl.BlockSpec((1, 1, w), lambda bb, c, pt: (bb, 0, 0))
    gs = pltpu.PrefetchScalarGridSpec(
        num_scalar_prefetch=1, grid=(b, n_pages // PAGES_PER_STEP),
        in_specs=[pl.BlockSpec(memory_space=pl.ANY), pl.BlockSpec(memory_space=pl.ANY),
                  head(MLA_KV_LORA), head(MLA_ROPE), row(MLA_KV_LORA), row(MLA_ROPE)],
        out_specs=head(MLA_KV_LORA),
        scratch_shapes=[pltpu.VMEM((PAD_ROWS, 1), F32), pltpu.VMEM((PAD_ROWS, 1), F32),
                        pltpu.VMEM((PAD_ROWS, MLA_KV_LORA), F32),
                        pltpu.VMEM((2, PAGES_PER_STEP, page, MLA_KV_LORA), F32),
                        pltpu.VMEM((2, PAGES_PER_STEP, MLA_ROPE, page), F32),
                        pltpu.SemaphoreType.DMA((2,)), pltpu.SemaphoreType.DMA((2,))])
    qlat = p["qlat"].reshape(b, MLA_HEADS, MLA_KV_LORA)
    qpe = p["qpe"].reshape(b, MLA_HEADS, MLA_ROPE)
    return pl.pallas_call(
        _mla_sample_kernel, grid_spec=gs,
        out_shape=jax.ShapeDtypeStruct((b, MLA_HEADS, MLA_KV_LORA), F32),
        compiler_params=pltpu.CompilerParams(dimension_semantics=("arbitrary",) * 2, vmem_limit_bytes=VMEM_LIMIT),
        name="mla_sample",
    )(pt_flat, lat_cache, kr_t, qlat, qpe, p["c"], p["kr"])


def _rope_angles(pos, theta, dim):
    half = dim // 2
    inv = 1.0 / (float(theta) ** (np.arange(half, dtype=np.float64) / half))
    ang = np.asarray(pos, np.float64)[:, None] * inv[None, :]
    return np.cos(ang).astype(np.float32), np.sin(ang).astype(np.float32)


def _rope_lane_tables(pos, rows):
    def table(theta, dim, period, active):
        half = dim // 2
        cos, sin = _rope_angles(pos, theta, dim)
        lane = np.arange(LANES)
        d = lane % period
        is_lo = (d < half) & active(lane)
        is_hi = (d >= half) & (d < dim) & active(lane)
        fi = np.where(d < half, d, np.clip(d - half, 0, half - 1))
        cos_l, sin_l = cos[:, fi], sin[:, fi]
        tab = np.stack([np.where(is_lo | is_hi, cos_l, 1.0), np.where(is_lo, -sin_l, 0.0),
                        np.where(is_hi, sin_l, 0.0)]).astype(np.float32)
        return jnp.asarray(np.broadcast_to(tab, (3, rows, LANES)))

    every = lambda lane: np.ones_like(lane, bool)
    keys_only = lambda lane: (lane % LANES) < HD
    return (table(ROPE_THETA, ROT_DIM, HD, every), table(ROPE_THETA, ROT_DIM, HD, keys_only),
            table(MLA_ROPE_THETA, MLA_ROPE, MLA_ROPE, every))


def _rope_row_tables(pos):
    cq, sq = _rope_angles(pos, ROPE_THETA, ROT_DIM)
    cm, sm = _rope_angles(pos, MLA_ROPE_THETA, MLA_ROPE)
    return jnp.asarray(np.stack([cq.T, sq.T])), jnp.asarray(np.stack([cm.T, sm.T]))


def _pack_weights(l, norm_pre, w_in, pe_cmp, q_norm, w_q_up, kv_norm, w_kv_up, w_proj_a, w_proj_b, w_out, norm_post):
    w = w_in[l].astype(BF16)
    o = IN_OFFSETS
    seg = lambda k: w[:, o[k]:o[k + 1]]
    gn = seg(4)
    per_group = 3 * NSA_GROUP
    gn_t = jnp.zeros((NSA_KV_HEADS * GATE_ROWS, D_MODEL), w.dtype)
    for g in range(NSA_KV_HEADS):
        gn_t = gn_t.at[g * GATE_ROWS:g * GATE_ROWS + per_group].set(gn[:, g * per_group:(g + 1) * per_group].T)
    w_t = jnp.concatenate([seg(0).T, seg(1).T, seg(2).T, seg(3).T, gn_t, seg(8).T], axis=0)
    w_krp4 = jnp.tile(seg(8).T, (LANES // MLA_ROPE, 1))
    w_cols = jnp.concatenate([seg(5), seg(6), seg(7), seg(9), seg(10), seg(11)], axis=1)
    wq = w_q_up[l]
    wqup = jnp.concatenate([wq[..., :MLA_NOPE].reshape(MLA_Q_LORA, -1), wq[..., MLA_NOPE:].reshape(MLA_Q_LORA, -1)],
                           axis=1)
    wkv = w_kv_up[l]
    wuk_pad = jnp.pad(wkv[..., :MLA_NOPE], ((0, 0), (0, 0), (0, LANES - MLA_NOPE)))
    rope_copy = jnp.pad(jnp.eye(MLA_ROPE, dtype=w.dtype), ((0, LANES - MLA_ROPE), (MLA_NOPE, MLA_ROPE)))
    wkx = jnp.concatenate([wuk_pad.reshape(MLA_KV_LORA, MLA_HEADS * LANES), jnp.tile(rope_copy, (1, MLA_HEADS))],
                          axis=0)
    wvt = jnp.transpose(wkv[..., MLA_NOPE:], (1, 2, 0)).reshape(MLA_WIDTH, MLA_KV_LORA)
    w2uk =jnp.transpose(wkv[..., :MLA_NOPE], (1, 2, 0)).reshape(MLA_HEADS // 2, LANES, MLA_KV_LORA)
    wv = jnp.transpose(wkv[..., MLA_NOPE:], (1, 0, 2))
    zeros = jnp.zeros_like(wv)
    even = (jnp.arange(MLA_HEADS) % 2 == 0)[:, None, None]
    wuv = jnp.concatenate([jnp.where(even, wv, zeros), jnp.where(even, zeros, wv)], axis=2)
    return {
        "npre": norm_pre[l][None].astype(F32), "w_t": w_t, "w_krp4": w_krp4, "w_cols": w_cols, "pe": pe_cmp[l].reshape(CMP_BLOCK, KV_WIDTH).astype(F32),
        "qnorm": q_norm[l][None].astype(F32), "wqup": wqup.astype(BF16), "wqupt": wqup.T.astype(BF16),
        "kvnorm": kv_norm[l][None].astype(F32), "w2uk": w2uk.astype(BF16), "wkx": wkx.astype(BF16),
        "wvt": wvt.astype(BF16), "wuv": wuv.astype(BF16),
        "wpa": w_proj_a[l].astype(BF16), "wpb": w_proj_b[l].astype(BF16), "wout": w_out[l].astype(BF16),
        "npost": norm_post[l][None].astype(F32),
    }


def _rows_from_cols(a):
    b, _, t = a.shape
    return a.reshape(b, NSA_KV_HEADS, 2, HD, t).transpose(0, 4, 1, 2, 3)


def _cols_from_rows(a):
    n, t = a.shape[:2]
    return a.transpose(0, 2, 3, 4, 1).reshape(n, KV_WIDTH, t)


def _prompt_layer(x, wts):
    b, t, _ = x.shape
    assert t % PROMPT_TM == 0
    p = _in_project_cols(x, _rope_row_tables(np.arange(t)), wts)
    o_a = _nsa_prompt(p, b, t)
    o_b = _mla_prompt(p, wts, b, t)
    y = _merge(x, o_a, o_b, p, wts, 512, from_latent=False)
    win_keep = min(WINDOW, t)
    return y, (_rows_from_cols(p["cmp"]), _rows_from_cols(p["slc"]), p["c"], p["kr"].transpose(0, 2, 1),
               _rows_from_cols(p["win"][:, :, t - win_keep:]))


def _sample_layer(x, l, caches, state_win, page_table, wts):
    cache_cmp, cache_slc, cache_lat, cache_kr = caches
    b, s_new, _ = x.shape
    assert s_new == 1
    n_pages = page_table.shape[1]
    page = cache_cmp.shape[2]
    past = n_pages * page
    assert past % SEL_BLOCK == 0 and n_pages % PAGES_PER_STEP == 0 and page == LANES
    win_len = state_win.shape[1]
    assert win_len == WINDOW and past >= WINDOW
    pt_flat = page_table.reshape(-1).astype(jnp.int32)

    tabs = _rope_lane_tables(np.full((1,), past), b)
    p = _in_project_rows(x.reshape(1, b, D_MODEL), tabs, wts)
    p = {k: v.reshape(b, 1, v.shape[-1]) for k, v in p.items()}

    kc_all = _compress_pages(_cols_from_rows(cache_cmp[l]), pt_flat, wts["pe"], b, n_pages)
    oc, idx = _sample_select(p["q"], kc_all, b)
    o_a, new_win = _sample_attend(p, oc, idx.reshape(-1), pt_flat, _cols_from_rows(cache_slc[l]),
                                  _cols_from_rows(state_win), b, n_pages)
    o_lat = _mla_sample(p, cache_lat[l], cache_kr[l].transpose(0, 2, 1), pt_flat, b, n_pages)
    pm = {k: p[k].reshape(1, b, -1) for k in ("ga", "gb", "ma", "mb")}
    y = _merge(x.reshape(1, b, D_MODEL), o_a.reshape(1, b, NSA_WIDTH),
               o_lat.reshape(1, b, MLA_HEADS * MLA_KV_LORA), pm, wts, b, from_latent=True)
    kv6 = lambda a: a.reshape(b, 1, NSA_KV_HEADS, 2, HD)
    return y.reshape(b, 1, D_MODEL), (kv6(p["cmp"]), kv6(p["slc"]), p["c"], p["kr"], _rows_from_cols(new_win))


def kernel(x_prompt, x_sample, cache_nsa_cmp, cache_nsa_slc, cache_mla_latent, cache_mla_krope, state_nsa_win,
           page_table, norm_pre, w_in, pe_cmp, q_norm, w_q_up, kv_norm, w_kv_up, w_proj_a, w_proj_b, w_out,
           norm_post):
    depth = w_in.shape[0]
    hp, hs = x_prompt, x_sample
    new_p, new_s = [], []
    for l in range(depth):
        wts = _pack_weights(l, norm_pre, w_in, pe_cmp, q_norm, w_q_up, kv_norm, w_kv_up, w_proj_a, w_proj_b,
                            w_out, norm_post)
        hp, sp = _prompt_layer(hp, wts)
        hs, ss = _sample_layer(hs, l, (cache_nsa_cmp, cache_nsa_slc, cache_mla_latent, cache_mla_krope),
                               state_nsa_win[l], page_table, wts)
        new_p.append(sp)
        new_s.append(ss)
    stack = lambda items, k: jnp.stack([s[k] for s in items])
    return (hp, hs) + tuple(stack(new_p, k) for k in range(5)) + tuple(stack(new_s, k) for k in range(5))
```

```python
import functools

import numpy as np
import jax
import jax.numpy as jnp
from jax import lax
from jax.experimental import pallas as pl
from jax.experimental.pallas import tpu as pltpu

D_MODEL = 1024
NSA_HEADS = 8
NSA_KV_HEADS = 2
NSA_GROUP = NSA_HEADS // NSA_KV_HEADS
HD = 64
NSA_WIDTH = NSA_HEADS * HD
KV_WIDTH = NSA_KV_HEADS * 2 * HD
ROT_DIM = HD // 4
ROPE_THETA = 500000.0
CMP_BLOCK = 32
SEL_BLOCK = 64
N_SELECT = 16
WINDOW = 512
NSA_SCALE = HD ** -0.5

MLA_HEADS = 8
MLA_Q_LORA = 384
MLA_KV_LORA = 256
MLA_NOPE = 64
MLA_ROPE = 32
MLA_V = 64
MLA_WIDTH = MLA_HEADS * MLA_V
MLA_ROPE_THETA = 10000.0
MLA_SCALE = (MLA_NOPE + MLA_ROPE) ** -0.5
LOG2E = 1.4426950408889634
SHIFT_LIMIT = 40.0

RMS_EPS = 1e-6
NEG = -1e30
FORCE_SCORE = 1e4

IN_SPLITS = (NSA_WIDTH, KV_WIDTH, KV_WIDTH, KV_WIDTH, 3 * NSA_HEADS, NSA_WIDTH,
             MLA_Q_LORA, MLA_KV_LORA, MLA_ROPE, MLA_WIDTH, D_MODEL, D_MODEL)
IN_OFFSETS = tuple(int(v) for v in np.cumsum((0,) + IN_SPLITS))

LANES = 128
HALF = LANES // 2
GATE_ROWS = 16
PAD_ROWS = 16

PT_Q = (0, 512)
PT_CMP = (512, 768)
PT_SLC = (768, 1024)
PT_WIN = (1024, 1280)
PT_GN = (1280, 1280 + NSA_KV_HEADS * GATE_ROWS)
PT_KRP = (PT_GN[1], PT_GN[1] + MLA_ROPE)
PT_ROWS = PT_KRP[1]
PR_GA = (0, 512)
PR_QD = (512, 896)
PR_KVD = (896, 1152)
PR_GB = (1152, 1664)
PR_MA = (1664, 2688)
PR_MB = (2688, 3712)
PR_COLS = 3712

PROMPT_TM = 512
ATT_TQ = 256
ATT_TK = 256
MLA_TK = 256

VMEM_LIMIT = 48 * 1024 * 1024
BF16 = jnp.bfloat16
F32 = jnp.float32


def _full_spec(shape):
    nd = len(shape)
    return pl.BlockSpec(shape, lambda *_: (0,) * nd)


def _lane_lo(rows):
    return lax.broadcasted_iota(jnp.int32, (rows, LANES), 1) < HALF


def _dot(a, b):
    return jnp.dot(a, b, preferred_element_type=F32)


def _dot_nt(a, b):
    return lax.dot_general(a, b, (((1,), (1,)), ((), ())), preferred_element_type=F32)


def _rms(v, gain):
    return v * lax.rsqrt(jnp.mean(v * v, axis=-1, keepdims=True) + RMS_EPS) * gain


def _split_bf16(v):
    hi = v.astype(BF16)
    return hi, (v - hi.astype(F32)).astype(BF16)


def _rope_tiles(v, tab_ref, shift):
    c, s_lo, s_hi = tab_ref[0], tab_ref[1], tab_ref[2]
    out = []
    for k in range(v.shape[1] // LANES):
        a = v[:, k * LANES:(k + 1) * LANES]
        out.append(a * c + pltpu.roll(a, LANES - shift, 1) * s_lo + pltpu.roll(a, shift, 1) * s_hi)
    return out[0] if len(out) == 1 else jnp.concatenate(out, axis=1)


def _dup_kv(a, lo):
    r = pltpu.roll(a, HALF, 1)
    return jnp.where(lo, a, r), jnp.where(lo, r, a)


def _kv_pack(v):
    lo = _lane_lo(v.shape[0])
    parts = []
    for g in range(NSA_KV_HEADS):
        kk, vv = _dup_kv(v[:, g * LANES:(g + 1) * LANES], lo)
        parts += [kk, vv]
    return jnp.concatenate(parts, axis=1).astype(BF16)


def _inproj_rows_kernel(x_ref, npre_ref, wt_ref, wkrp_ref, w_ref, tq_ref, tkv_ref, tm_ref, qnorm_ref, wqup_ref,
                        kvnorm_ref, w2uk_ref, q_ref, qrot_ref, gate_ref, cmp_ref, slc_ref, win_ref, slckv_ref,
                        winkv_ref, ga_ref, gb_ref, ma_ref, mb_ref, qpe_ref, c_ref, kr_ref, qlat_ref):
    xb = _rms(x_ref[0], npre_ref[...]).astype(BF16)
    segt = lambda lohi: _dot_nt(xb, wt_ref[lohi[0]:lohi[1], :])
    seg = lambda lohi: _dot(xb, w_ref[:, lohi[0]:lohi[1]])

    a = segt((PT_Q[0], PT_WIN[1]))
    q = a[:, :NSA_WIDTH]
    q_ref[0] = (q * NSA_SCALE).astype(BF16)
    qrot_ref[0] = (_rope_tiles(q, tq_ref, ROT_DIM // 2) * NSA_SCALE).astype(BF16)
    cmp_ref[0] = a[:, NSA_WIDTH:NSA_WIDTH + KV_WIDTH]
    kvs = _rope_tiles(a[:, NSA_WIDTH + KV_WIDTH:NSA_WIDTH + 2 * KV_WIDTH], tkv_ref, ROT_DIM // 2)
    slc_ref[0] = kvs
    slckv_ref[0] = _kv_pack(kvs)
    kvw = _rope_tiles(a[:, NSA_WIDTH + 2 * KV_WIDTH:], tkv_ref, ROT_DIM // 2)
    win_ref[0] = kvw
    winkv_ref[0] = _kv_pack(kvw)

    gate_ref[0] = jax.nn.sigmoid(segt(PT_GN))
    ga_ref[0] = seg(PR_GA).astype(BF16)
    gb_ref[0] = seg(PR_GB).astype(BF16)
    ma_ref[0] = seg(PR_MA).astype(BF16)
    mb_ref[0] = seg(PR_MB).astype(BF16)

    qd = _rms(seg(PR_QD), qnorm_ref[...]).astype(BF16)
    qh = _dot(qd, wqup_ref[...])
    qn = qh[:, :MLA_HEADS * MLA_NOPE].astype(BF16)
    qpe_ref[0] = _rope_tiles(qh[:, MLA_HEADS * MLA_NOPE:], tm_ref, MLA_ROPE // 2) * MLA_SCALE
    c_ref[0] = _rms(seg(PR_KVD), kvnorm_ref[...])
    kr_ref[0] = _rope_tiles(_dot_nt(xb, wkrp_ref[...]), tm_ref, MLA_ROPE // 2)[:, :MLA_ROPE]

    rows = qn.shape[0]
    lo = _lane_lo(rows)
    z = jnp.zeros((rows, LANES), BF16)
    parts = []
    for j in range(MLA_HEADS // 2):
        pair = qn[:, j * LANES:(j + 1) * LANES]
        parts.append(_dot(jnp.where(lo, pair, z), w2uk_ref[j]) * MLA_SCALE)
        parts.append(_dot(jnp.where(lo, z, pair), w2uk_ref[j]) * MLA_SCALE)
    qlat_ref[0] = jnp.concatenate(parts, axis=1)


def _in_project_rows(x3, tabs, wts):
    b, t, _ = x3.shape
    tq, tkv, tmla = tabs
    row = lambda w, dt: jax.ShapeDtypeStruct((b, t, w), dt)
    names = ["q", "qrot", "gate", "cmp", "slc", "win", "slckv", "winkv", "ga", "gb", "ma", "mb", "qpe", "c", "kr",
             "qlat"]
    out_shape = [row(512, BF16), row(512, BF16), row(NSA_KV_HEADS * GATE_ROWS, F32), row(256, F32), row(256, F32),
                 row(256, F32),
                 row(512, BF16), row(512, BF16), row(512, BF16), row(512, BF16), row(1024, BF16), row(1024, BF16),
                 row(MLA_HEADS * MLA_ROPE, F32), row(MLA_KV_LORA, F32), row(MLA_ROPE, F32),
                 row(MLA_HEADS * MLA_KV_LORA, F32)]
    tok = lambda w: pl.BlockSpec((1, t, w), lambda bb: (bb, 0, 0))
    tab = _full_spec((3, t, LANES))
    in_specs = [tok(D_MODEL), _full_spec((1, D_MODEL)), _full_spec((PT_ROWS, D_MODEL)), _full_spec((LANES, D_MODEL)),
                _full_spec((D_MODEL, PR_COLS)), tab, tab, tab, _full_spec((1, MLA_Q_LORA)), _full_spec((MLA_Q_LORA, MLA_HEADS * (MLA_NOPE + MLA_ROPE))),
                _full_spec((1, MLA_KV_LORA)), _full_spec((MLA_HEADS // 2, LANES, MLA_KV_LORA))]
    res = pl.pallas_call(
        _inproj_rows_kernel, grid=(b,), in_specs=in_specs, out_specs=[tok(s.shape[2]) for s in out_shape],
        out_shape=out_shape,
        compiler_params=pltpu.CompilerParams(dimension_semantics=("arbitrary",), vmem_limit_bytes=VMEM_LIMIT),
        name="in_project_rows",
    )(x3, wts["npre"], wts["w_t"], wts["w_krp4"], wts["w_cols"], tq, tkv, tmla, wts["qnorm"], wts["wqup"],
      wts["kvnorm"], wts["w2uk"])
    return dict(zip(names, res))


def _rope_rows(x, cos, sin, half):
    x1, x2 = x[0:half], x[half:2 * half]
    parts = [x1 * cos - x2 * sin, x1 * sin + x2 * cos]
    if x.shape[0] > 2 * half:
        parts.append(x[2 * half:])
    return jnp.concatenate(parts, axis=0)


def _store_chunks(ref, v):
    tk = ref.shape[3]
    for j in range(ref.shape[1]):
        ref[0, j] = v[:, j * tk:(j + 1) * tk]


def _inproj_cols_kernel(x_ref, npre_ref, wt_ref, w_ref, ropeq_ref, ropem_ref, pe_ref, qnorm_ref, wqupt_ref,
                        kvnorm_ref, pool_ref, wkx_ref, wvt_ref,
                        qt_ref, qrt_ref, gate_ref, cmp_ref, slc_ref, win_ref, slcb_ref, winb_ref, slcr_ref,
                        winr_ref, kc_ref, ga_ref, gb_ref, ma_ref, mb_ref, qmt_ref, c_ref, kmla_ref, vmt_ref,
                        kr_ref, kn_ref):
    tm = x_ref.shape[1]
    xb = _rms(x_ref[0], npre_ref[...]).astype(BF16)
    segt = lambda lohi: _dot_nt(wt_ref[lohi[0]:lohi[1], :], xb)
    seg = lambda lohi: _dot(xb, w_ref[:, lohi[0]:lohi[1]])
    cq, sq = ropeq_ref[0], ropeq_ref[1]
    cm, sm = ropem_ref[0], ropem_ref[1]
    hq, hm = ROT_DIM // 2, MLA_ROPE // 2

    qt = segt(PT_Q)
    qt_ref[0] = (qt * (NSA_SCALE * LOG2E)).astype(BF16)
    qrt = jnp.concatenate([_rope_rows(qt[h * HD:(h + 1) * HD], cq, sq, hq) for h in range(NSA_HEADS)], axis=0)
    qrt_ref[0] = (qrt * (NSA_SCALE * LOG2E)).astype(BF16)

    cmpt = segt(PT_CMP)
    cmp_ref[0] = cmpt
    hi, lo = _split_bf16(cmpt)
    pool = pool_ref[...]
    pooled = (_dot_nt(pool, hi) + _dot_nt(pool, lo))[0:tm // CMP_BLOCK]
    kc_ref[0] = (pooled + jnp.sum(pe_ref[...], axis=0, keepdims=True)) * (1.0 / CMP_BLOCK)

    def rope_kv(v):
        parts = []
        for g in range(NSA_KV_HEADS):
            parts.append(_rope_rows(v[g * LANES:g * LANES + HD], cq, sq, hq))
            parts.append(v[g * LANES + HD:(g + 1) * LANES])
        return jnp.concatenate(parts, axis=0)

    slct = rope_kv(segt(PT_SLC))
    slc_ref[0] = slct
    _store_chunks(slcb_ref, slct.astype(BF16))
    slcr_ref[0] = slct.T.astype(BF16)
    wint = rope_kv(segt(PT_WIN))
    win_ref[0] = wint
    _store_chunks(winb_ref, wint.astype(BF16))
    winr_ref[0] = wint.T.astype(BF16)

    gate_ref[0] = jax.nn.sigmoid(segt(PT_GN))
    krt = _rope_rows(segt(PT_KRP), cm, sm, hm)
    kr_ref[0] = krt
    c = _rms(seg(PR_KVD), kvnorm_ref[...])
    c_ref[0] = c
    c_b = c.astype(BF16)
    kr_rows = jnp.concatenate([krt, jnp.zeros((LANES - MLA_ROPE, tm), F32)], axis=0).T
    ckr = jnp.concatenate([c_b, kr_rows.astype(BF16)], axis=1)
    kmla = _dot(ckr, wkx_ref[...]).astype(BF16)
    kmla_ref[0] = kmla
    kf = kmla.astype(F32)
    norms = []
    for h in range(MLA_HEADS):
        n2 = jnp.sum(kf[:, h * LANES:(h + 1) * LANES] ** 2, axis=1, keepdims=True)
        norms.append(jnp.broadcast_to(jnp.sqrt(jnp.max(n2, axis=0, keepdims=True)), (1, LANES)))
    kn_ref[0] = jnp.concatenate(norms, axis=0)
    _store_chunks(vmt_ref, _dot_nt(wvt_ref[...], c_b).astype(BF16))

    qd = _rms(seg(PR_QD), qnorm_ref[...]).astype(BF16)
    qht = _dot_nt(wqupt_ref[...], qd)
    n_nope = MLA_HEADS * MLA_NOPE
    zq = jnp.zeros((LANES - MLA_NOPE - MLA_ROPE, tm), F32)
    parts = []
    for h in range(MLA_HEADS):
        parts += [qht[h * MLA_NOPE:(h + 1) * MLA_NOPE],
                  _rope_rows(qht[n_nope + h * MLA_ROPE:n_nope + (h + 1) * MLA_ROPE], cm, sm, hm), zq]
    qmt_ref[0] = (jnp.concatenate(parts, axis=0) * (MLA_SCALE * LOG2E)).astype(BF16)

    ga_ref[0] = seg(PR_GA).astype(BF16)
    gb_ref[0] = seg(PR_GB).astype(BF16)
    ma_ref[0] = seg(PR_MA).astype(BF16)
    mb_ref[0] = seg(PR_MB).astype(BF16)


def _in_project_cols(x3, ropes, wts):
    b, t, _ = x3.shape
    tm = PROMPT_TM
    nt = t // tm
    ropeq, ropem = ropes
    pool = np.zeros((16, tm), np.float32)
    for s in range(tm):
        pool[s // CMP_BLOCK, s] = 1.0
    sds = jax.ShapeDtypeStruct
    rows = lambda w: pl.BlockSpec((1, tm, w), lambda i, bb: (bb, i, 0))
    cols = lambda w: pl.BlockSpec((1, w, tm), lambda i, bb: (bb, 0, i))
    chunk = lambda w, tk: pl.BlockSpec((1, tm // tk, w, tk), lambda i, bb: (bb, i, 0, 0))
    outs = [
        ("qt", sds((b, NSA_WIDTH, t), BF16), cols(NSA_WIDTH)),
        ("qrt", sds((b, NSA_WIDTH, t), BF16), cols(NSA_WIDTH)),
        ("gate", sds((b, NSA_KV_HEADS * GATE_ROWS, t), F32), cols(NSA_KV_HEADS * GATE_ROWS)),
        ("cmp", sds((b, KV_WIDTH, t), F32), cols(KV_WIDTH)),
        ("slc", sds((b, KV_WIDTH, t), F32), cols(KV_WIDTH)),
        ("win", sds((b, KV_WIDTH, t), F32), cols(KV_WIDTH)),
        ("slcb", sds((b, t // ATT_TK, KV_WIDTH, ATT_TK), BF16), chunk(KV_WIDTH, ATT_TK)),
        ("winb", sds((b, t // ATT_TK, KV_WIDTH, ATT_TK), BF16), chunk(KV_WIDTH, ATT_TK)),
        ("slcr", sds((b, t, KV_WIDTH), BF16), rows(KV_WIDTH)),
        ("winr", sds((b, t, KV_WIDTH), BF16), rows(KV_WIDTH)),
        ("kc", sds((b, t // CMP_BLOCK, KV_WIDTH), F32),
         pl.BlockSpec((1, tm // CMP_BLOCK, KV_WIDTH), lambda i, bb: (bb, i, 0))),
        ("ga", sds((b, t, NSA_WIDTH), BF16), rows(NSA_WIDTH)),
        ("gb", sds((b, t, MLA_WIDTH), BF16), rows(MLA_WIDTH)),
        ("ma", sds((b, t, D_MODEL), BF16), rows(D_MODEL)),
        ("mb", sds((b, t, D_MODEL), BF16), rows(D_MODEL)),
        ("qmt", sds((b, MLA_HEADS * LANES, t), BF16), cols(MLA_HEADS * LANES)),
        ("c", sds((b, t, MLA_KV_LORA), F32), rows(MLA_KV_LORA)),
        ("kmla", sds((b, t, MLA_HEADS * LANES), BF16), rows(MLA_HEADS * LANES)),
        ("vmt", sds((b, t // MLA_TK, MLA_WIDTH, MLA_TK), BF16), chunk(MLA_WIDTH, MLA_TK)),
        ("kr", sds((b, MLA_ROPE, t), F32), cols(MLA_ROPE)),
        ("kn", sds((b, nt * MLA_HEADS, LANES), F32), pl.BlockSpec((1, MLA_HEADS, LANES), lambda i, bb: (bb, i, 0))),
    ]
    rope_spec = lambda half: pl.BlockSpec((2, half, tm), lambda i, bb: (0, 0, i))
    in_specs = [rows(D_MODEL), _full_spec((1, D_MODEL)), _full_spec((PT_ROWS, D_MODEL)),
                _full_spec((D_MODEL, PR_COLS)), rope_spec(ROT_DIM // 2), rope_spec(MLA_ROPE // 2),
                _full_spec((CMP_BLOCK, KV_WIDTH)), _full_spec((1, MLA_Q_LORA)),
                _full_spec((MLA_HEADS * (MLA_NOPE + MLA_ROPE), MLA_Q_LORA)), _full_spec((1, MLA_KV_LORA)),
                _full_spec((16, tm)), _full_spec((MLA_KV_LORA + LANES, MLA_HEADS * LANES)),
                _full_spec((MLA_WIDTH, MLA_KV_LORA))]
    res = pl.pallas_call(
        _inproj_cols_kernel, grid=(nt, b), in_specs=in_specs, out_specs=[o[2] for o in outs],
        out_shape=[o[1] for o in outs],
        compiler_params=pltpu.CompilerParams(dimension_semantics=("arbitrary", "arbitrary"),
                                             vmem_limit_bytes=VMEM_LIMIT),
        name="in_project_cols",
    )(x3, wts["npre"], wts["w_t"], wts["w_cols"], ropeq, ropem, wts["pe"], wts["qnorm"], wts["wqupt"],
      wts["kvnorm"], jnp.asarray(pool, BF16), wts["wkx"], wts["wvt"])
    return dict(zip([o[0] for o in outs], res))


def _online_update(s, vt, m, l, acc):
    d, keys = vt.shape
    m_new = jnp.maximum(m, jnp.max(s, axis=0, keepdims=True))
    p = jnp.exp2(s - m_new).astype(BF16)
    alpha = jnp.exp2(m - m_new)
    pv = _dot(jnp.concatenate([vt, jnp.ones((PAD_ROWS, keys), BF16)], axis=0), p)
    return m_new, alpha * l + pv[d:d + 1], alpha * acc + pv[0:d]


def _nsa_prompt_kernel(n_blk, with_compress, *refs):
    if with_compress:
        (pt_ref, qt_ref, qrt_ref, gate_ref, kc_ref, ks_ref, kst_ref, kw_ref, kwt_ref, cache_ref, pool_ref, pe_ref,
         o_ref, kcall_ref, sb_ref, kn_ref, buf_ref, sem_ref) = refs
        step = (pl.program_id(0) * pl.num_programs(1) + pl.program_id(1)) * pl.num_programs(2) + pl.program_id(2)
        last = pl.num_programs(0) * pl.num_programs(1) * pl.num_programs(2) - 1
        _compress_step(pt_ref, cache_ref, pool_ref, pe_ref, kcall_ref, buf_ref, sem_ref, step, last)
    else:
        qt_ref, qrt_ref, gate_ref, kc_ref, ks_ref, kst_ref, kw_ref, kwt_ref, o_ref, sb_ref, kn_ref = refs
    i = pl.program_id(2)
    tq, tk = ATT_TQ, ATT_TK
    nl = NSA_GROUP * tq
    zq = jnp.zeros((HD, tq), BF16)

    def widen(qt):
        return jnp.concatenate([jnp.concatenate([qt[r * HD:(r + 1) * HD], zq], axis=0)
                                for r in range(NSA_GROUP)], axis=1)

    qc = widen(qt_ref[0])
    qr = widen(qrt_ref[0])

    n_cmp = 2 * n_blk
    kc = kc_ref[0]
    s_c = _dot(kc.astype(BF16), qc)
    rho = lax.broadcasted_iota(jnp.int32, (n_cmp, nl), 0)
    cmp_idx = jnp.where(rho < n_blk, 2 * rho, 2 * (rho - n_blk) + 1)
    t_l = i * tq + (lax.broadcasted_iota(jnp.int32, (n_cmp, nl), 1) & (tq - 1))
    mask_c = cmp_idx * CMP_BLOCK + (CMP_BLOCK - 1) <= t_l
    s_c = jnp.where(mask_c, s_c, NEG)
    e_c = jnp.where(mask_c, jnp.exp2(s_c - jnp.max(s_c, axis=0, keepdims=True)), 0.0)
    p_c = e_c / jnp.maximum(jnp.sum(e_c, axis=0, keepdims=True), 1e-30)
    kct = jnp.concatenate([kc, jnp.zeros((LANES - n_cmp, LANES), F32)], axis=0).T
    p_pad = jnp.concatenate([p_c, jnp.zeros((LANES - n_cmp, nl), F32)], axis=0)
    o_c = _dot(kct[HD:2 * HD].astype(BF16), p_pad.astype(BF16))

    imp = p_c[:, 0:tq]
    for r in range(1, NSA_GROUP):
        imp = imp + p_c[:, r * tq:(r + 1) * tq]
    imp_blk = imp[0:n_blk] + imp[n_blk:n_cmp]
    blk = lax.broadcasted_iota(jnp.int32, (n_blk, tq), 0)
    t_q = i * tq + lax.broadcasted_iota(jnp.int32, (n_blk, tq), 1)
    ahead_of = t_q - blk * SEL_BLOCK
    score = jnp.where(blk == 0, FORCE_SCORE,
                      jnp.where(ahead_of < 0, -FORCE_SCORE, jnp.where(ahead_of < SEL_BLOCK, FORCE_SCORE, imp_blk)))
    def count_ahead():
        rank = jnp.zeros((n_blk, tq), F32)
        for j in range(n_blk):
            other = score[j:j + 1, :]
            tie = jnp.where(blk > j, 1.0, 0.0)
            rank = rank + jnp.where(other > score, 1.0, jnp.where(other == score, tie, 0.0))
        return rank

    few_blocks = (i + 1) * tq <= N_SELECT * SEL_BLOCK
    rank = lax.cond(few_blocks, lambda: jnp.zeros((n_blk, tq), F32), count_ahead)
    sb_ref[...] = jnp.where(rank < N_SELECT, jnp.where(score > -1.0, 0.0, NEG), NEG)

    key_r = lax.broadcasted_iota(jnp.int32, (tk, tq), 0)
    t_k = i * tq + lax.broadcasted_iota(jnp.int32, (tk, tq), 1)
    n_chunks = ((i + 1) * tq + tk - 1) // tk
    c_lo = jnp.maximum(i * tq - WINDOW, 0) // tk
    per_chunk = tk // SEL_BLOCK
    tile = lambda bias: jnp.concatenate([bias] * NSA_GROUP, axis=1)
    ones = jnp.ones((PAD_ROWS, tk), BF16)

    def sel_bias(c):
        rows = [jnp.broadcast_to(sb_ref[pl.ds(c * per_chunk + j, 1), :], (SEL_BLOCK, tq)) for j in range(per_chunk)]
        return jnp.concatenate(rows, axis=0)

    def values(ref, c):
        return jnp.concatenate([ref[0, c, HD:2 * HD, :], ones], axis=0)

    qf = qrt_ref[0].astype(F32)
    q_norm2 = [jnp.sum(qf[r * HD:(r + 1) * HD] ** 2, axis=0, keepdims=True) for r in range(NSA_GROUP)]
    q_norm = jnp.sqrt(jnp.maximum(jnp.maximum(q_norm2[0], q_norm2[1]), jnp.maximum(q_norm2[2], q_norm2[3])))

    def key_norm(ref):
        kf = ref[0, :, 0:HD, :].astype(F32)
        n2 = jnp.max(jnp.sum(kf * kf, axis=1, keepdims=True), axis=0)
        return jnp.broadcast_to(jnp.sqrt(jnp.max(n2, axis=1, keepdims=True)), (1, LANES))

    @pl.when(i == 0)
    def _():
        kn_ref[0:1, :] = key_norm(kst_ref)
        kn_ref[1:2, :] = key_norm(kwt_ref)

    wide = lambda row: jnp.concatenate([row] * (tq // LANES), axis=1)
    bound_s = q_norm * (wide(kn_ref[0:1, :]) * 1.001) + 1e-3
    bound_w = q_norm * (wide(kn_ref[1:2, :]) * 1.001) + 1e-3
    safe = jnp.maximum(jnp.max(bound_s), jnp.max(bound_w)) <= SHIFT_LIMIT

    def attend_shifted():
        def past_probs(c):
            base = pl.multiple_of(c * tk, tk)
            s = _dot(ks_ref[0, pl.ds(base, tk), :], qr) + tile(sel_bias(c) - bound_s)
            return jnp.exp2(s).astype(BF16)

        def past_body(c, acc):
            return acc + _dot(values(kst_ref, c), past_probs(c))

        def past_pair(cc, acc):
            p0, p1 = past_probs(2 * cc), past_probs(2 * cc + 1)
            return acc + _dot(values(kst_ref, 2 * cc), p0) + _dot(values(kst_ref, 2 * cc + 1), p1)

        def near_probs(c):
            base = pl.multiple_of(c * tk, tk)
            s = _dot(ks_ref[0, pl.ds(base, tk), :], qr)
            w = _dot(kw_ref[0, pl.ds(base, tk), :], qr)
            dist = t_k - (base + key_r)
            causal = jnp.where(dist >= 0, 0.0, NEG)
            p_s = jnp.exp2(s + tile(sel_bias(c) + causal - bound_s)).astype(BF16)
            p_w = jnp.exp2(w + tile(jnp.where(dist <= WINDOW, causal, NEG) - bound_w)).astype(BF16)
            return p_s, p_w

        def near_body(c, carry):
            p_s, p_w = near_probs(c)
            return carry[0] + _dot(values(kst_ref, c), p_s), carry[1] + _dot(values(kwt_ref, c), p_w)

        def near_pair(cc, carry):
            c0 = c_lo + 2 * cc
            (ps0, pw0), (ps1, pw1) = near_probs(c0), near_probs(c0 + 1)
            return (carry[0] + _dot(values(kst_ref, c0), ps0) + _dot(values(kst_ref, c0 + 1), ps1),
                    carry[1] + _dot(values(kwt_ref, c0), pw0) + _dot(values(kwt_ref, c0 + 1), pw1))

        zero = jnp.zeros((HD + PAD_ROWS, nl), F32)
        far = lax.fori_loop(0, c_lo // 2, past_pair, zero)
        far = lax.fori_loop(2 * (c_lo // 2), c_lo, past_body, far)
        n_pairs = (n_chunks - c_lo) // 2
        acc = lax.fori_loop(0, n_pairs, near_pair, (far, zero))
        a_s, a_w = lax.fori_loop(c_lo + 2 * n_pairs, n_chunks, near_body, acc)
        inv_s, inv_w = 1.0 / a_s[HD:HD + 1], 1.0 / a_w[HD:HD + 1]
        return a_s[0:HD] * inv_s, a_w[0:HD] * inv_w

    def attend_online():
        init = (jnp.full((1, nl), NEG, F32), jnp.zeros((1, nl), F32), jnp.zeros((HD, nl), F32))

        def past_body(c, carry):
            base = pl.multiple_of(c * tk, tk)
            s = _dot(ks_ref[0, pl.ds(base, tk), :], qr) + tile(sel_bias(c))
            return _online_update(s, kst_ref[0, c, HD:2 * HD, :], *carry)

        def near_body(c, carry):
            base = pl.multiple_of(c * tk, tk)
            s = _dot(ks_ref[0, pl.ds(base, tk), :], qr)
            w = _dot(kw_ref[0, pl.ds(base, tk), :], qr)
            dist = t_k - (base + key_r)
            causal = jnp.where(dist >= 0, 0.0, NEG)
            s = s + tile(sel_bias(c) + causal)
            w = w + tile(jnp.where(dist <= WINDOW, causal, NEG))
            return (_online_update(s, kst_ref[0, c, HD:2 * HD, :], *carry[:3])
                    + _online_update(w, kwt_ref[0, c, HD:2 * HD, :], *carry[3:]))

        far = lax.fori_loop(0, c_lo, past_body, init)
        _, l_s, a_s, _, l_w, a_w = lax.fori_loop(c_lo, n_chunks, near_body, far + init)
        return a_s / l_s, a_w / l_w

    o_s, o_w = lax.cond(safe, attend_shifted, attend_online)

    gate = gate_ref[0]
    heads = []
    for r in range(NSA_GROUP):
        sl = slice(r * tq, (r + 1) * tq)
        heads.append(gate[3 * r:3 * r + 1] * o_c[:, sl] + gate[3 * r + 1:3 * r + 2] * o_s[:, sl]
                     + gate[3 * r + 2:3 * r + 3] * o_w[:, sl])
    o_ref[0] = jnp.concatenate(heads, axis=0).T.astype(BF16)


def _nsa_prompt(p, b, t, compress_job=None):
    tq, tk = ATT_TQ, ATT_TK
    n_blk = t // SEL_BLOCK
    assert 2 * n_blk <= LANES and t % tk == 0
    kc = p["kc"].reshape(b, n_blk, 2, KV_WIDTH).transpose(0, 2, 1, 3).reshape(b, 2 * n_blk, KV_WIDTH)
    nq = t // tq
    grid = (b, NSA_KV_HEADS, nq)
    qspec = pl.BlockSpec((1, NSA_GROUP * HD, tq), lambda bb, g, i, *_: (bb, g, i))
    rm = pl.BlockSpec((1, t, LANES), lambda bb, g, i, *_: (bb, 0, g))
    fm = pl.BlockSpec((1, t // tk, LANES, tk), lambda bb, g, i, *_: (bb, 0, g, 0))
    in_specs = [qspec, qspec, pl.BlockSpec((1, GATE_ROWS, tq), lambda bb, g, i, *_: (bb, g, i)),
                pl.BlockSpec((1, 2 * n_blk, LANES), lambda bb, g, i, *_: (bb, 0, g)), rm, fm, rm, fm]
    out_specs = [pl.BlockSpec((1, tq, NSA_GROUP * HD), lambda bb, g, i, *_: (bb, i, g))]
    out_shape = [jax.ShapeDtypeStruct((b, t, NSA_WIDTH), BF16)]
    scratch = [pltpu.VMEM((n_blk, tq), F32), pltpu.VMEM((8, LANES), F32)]
    args = [p["qt"], p["qrt"], p["gate"], kc, p["slcr"], p["slcb"], p["winr"], p["winb"]]
    params = pltpu.CompilerParams(dimension_semantics=("arbitrary",) * 3, vmem_limit_bytes=VMEM_LIMIT)
    if compress_job is not None and compress_job[3] * compress_job[4] // PAGES_PER_STEP != b * NSA_KV_HEADS * nq:
        compress_job = None
    if compress_job is None:
        o_a, = pl.pallas_call(
            functools.partial(_nsa_prompt_kernel, n_blk, False), grid=grid, in_specs=in_specs, out_specs=out_specs,
            out_shape=out_shape, scratch_shapes=scratch, compiler_params=params, name="nsa_prompt")(*args)
        return o_a, None
    cache_t, pt_flat, pe, sb, n_pages = compress_job
    page = cache_t.shape[2]
    per_step = PAGES_PER_STEP * page // CMP_BLOCK
    per_seq = n_pages // PAGES_PER_STEP
    pool = _pool_matrix(page)
    flat = lambda bb, g, i: (bb * NSA_KV_HEADS + g) * nq + i
    gs = pltpu.PrefetchScalarGridSpec(
        num_scalar_prefetch=1, grid=grid,
        in_specs=in_specs + [pl.BlockSpec(memory_space=pl.ANY), pl.BlockSpec(pool.shape, lambda *_: (0, 0)),
                             pl.BlockSpec((CMP_BLOCK, KV_WIDTH), lambda *_: (0, 0))],
        out_specs=out_specs + [pl.BlockSpec((1, per_step, KV_WIDTH),
                                            lambda bb, g, i, *_: (flat(bb, g, i) // per_seq, flat(bb, g, i) % per_seq, 0))],
        scratch_shapes=scratch + [pltpu.VMEM((2, PAGES_PER_STEP, KV_WIDTH, page), F32), pltpu.SemaphoreType.DMA((2,))])
    return pl.pallas_call(
        functools.partial(_nsa_prompt_kernel, n_blk, True), grid_spec=gs,
        out_shape=out_shape + [jax.ShapeDtypeStruct((sb, n_pages * page // CMP_BLOCK, KV_WIDTH), F32)],
        compiler_params=params, name="nsa_prompt",
    )(pt_flat, *args, cache_t, jnp.asarray(pool, BF16), pe)


MLA_TQ = 256


def _mla_prompt_kernel(qmt_ref, k_ref, vt_ref, kn_ref, o_ref, m_ref, l_ref, acc_ref):
    i = pl.program_id(1)
    tq, tk = MLA_TQ, MLA_TK
    heads = range(MLA_HEADS)
    m_ref[...] = jnp.full(m_ref.shape, NEG, F32)
    l_ref[...] = jnp.zeros(l_ref.shape, F32)
    acc_ref[...] = jnp.zeros(acc_ref.shape, F32)
    key_r = lax.broadcasted_iota(jnp.int32, (tk, tq), 0)
    t_k = i * tq + lax.broadcasted_iota(jnp.int32, (tk, tq), 1)
    n_chunks = ((i + 1) * tq + tk - 1) // tk
    n_past = (i * tq + 1) // tk
    ones = jnp.ones((PAD_ROWS, tk), BF16)

    def keys(c_base, h):
        return k_ref[0, pl.ds(c_base, tk), h * LANES:(h + 1) * LANES]

    kn = kn_ref[0]
    k_max = kn[0:MLA_HEADS]
    for j in range(1, kn.shape[0] // MLA_HEADS):
        k_max = jnp.maximum(k_max, kn[j * MLA_HEADS:(j + 1) * MLA_HEADS])
    bounds = []
    for h in heads:
        qf = qmt_ref[0, h * LANES:(h + 1) * LANES, :].astype(F32)
        q_norm = jnp.sqrt(jnp.sum(qf * qf, axis=0, keepdims=True))
        k_row = jnp.concatenate([k_max[h:h + 1]] * (tq // LANES), axis=1)
        bounds.append(q_norm * (k_row * 1.001) + 1e-3)
    worst = bounds[0]
    for bd in bounds[1:]:
        worst = jnp.maximum(worst, bd)
    safe = jnp.max(worst) <= SHIFT_LIMIT

    def shifted_step(c, masked):
        base = pl.multiple_of(c * tk, tk)
        old = [acc_ref[h] for h in heads]
        ps = []
        for h in heads:
            s = _dot(keys(base, h), qmt_ref[0, h * LANES:(h + 1) * LANES, :]) - bounds[h]
            ps.append(jnp.exp2(jnp.where(base + key_r <= t_k, s, NEG) if masked else s).astype(BF16))
        new = [old[h] + _dot(jnp.concatenate([vt_ref[0, c, h * MLA_V:(h + 1) * MLA_V, :], ones], axis=0), ps[h])
               for h in heads]
        for h in heads:
            acc_ref[h] = new[h]
        return 0

    def online_step(c, masked):
        base = pl.multiple_of(c * tk, tk)
        old = [(m_ref[h], l_ref[h], acc_ref[h, 0:MLA_V]) for h in heads]
        scores = []
        for h in heads:
            s = _dot(keys(base, h), qmt_ref[0, h * LANES:(h + 1) * LANES, :])
            scores.append(jnp.where(base + key_r <= t_k, s, NEG) if masked else s)
        new = [_online_update(scores[h], vt_ref[0, c, h * MLA_V:(h + 1) * MLA_V, :], *old[h]) for h in heads]
        for h in heads:
            m_ref[h], l_ref[h], acc_ref[h, 0:MLA_V] = new[h]
        return 0

    def run(step):
        lax.fori_loop(0, n_past, lambda c, z: step(c, False), 0)
        lax.fori_loop(n_past, n_chunks, lambda c, z: step(c, True), 0)

    def shifted():
        run(shifted_step)
        return tuple(acc_ref[h, 0:MLA_V] * (1.0 / acc_ref[h, MLA_V:MLA_V + 1]) for h in heads)

    def online():
        run(online_step)
        return tuple(acc_ref[h, 0:MLA_V] / l_ref[h] for h in heads)

    outs = lax.cond(safe, shifted, online)
    o_ref[0] = jnp.concatenate(outs, axis=0).T.astype(BF16)


def _mla_prompt(p, wts, b, t):
    tq, tk = MLA_TQ, MLA_TK
    assert t % tq == 0
    return pl.pallas_call(
        _mla_prompt_kernel,
        grid=(b, t // tq),
        in_specs=[pl.BlockSpec((1, MLA_HEADS * LANES, tq), lambda bb, i: (bb, 0, i)),
                  pl.BlockSpec((1, t, MLA_HEADS * LANES), lambda bb, i: (bb, 0, 0)),
                  pl.BlockSpec((1, t // tk, MLA_WIDTH, tk), lambda bb, i: (bb, 0, 0, 0)),
                  pl.BlockSpec((1,) + p["kn"].shape[1:], lambda bb, i: (bb, 0, 0))],
        out_specs=pl.BlockSpec((1, tq, MLA_WIDTH), lambda bb, i: (bb, i, 0)),
        out_shape=jax.ShapeDtypeStruct((b, t, MLA_WIDTH), BF16),
        scratch_shapes=[pltpu.VMEM((MLA_HEADS, 1, tq), F32), pltpu.VMEM((MLA_HEADS, 1, tq), F32),
                        pltpu.VMEM((MLA_HEADS, MLA_V + PAD_ROWS, tq), F32)],
        compiler_params=pltpu.CompilerParams(dimension_semantics=("arbitrary",) * 2, vmem_limit_bytes=VMEM_LIMIT),
        name="mla_prompt",
    )(p["qmt"], p["kmla"], p["vmt"], p["kn"])


def _merge_kernel(from_latent, x_ref, oa_ref, ob_ref, ga_ref, gb_ref, ma_ref, mb_ref, wpa_ref, wpb_ref, wout_ref,
                  npost_ref, wuv_ref, y_ref):
    if from_latent:
        lat = ob_ref[0].astype(BF16)
        parts = []
        for j in range(MLA_HEADS // 2):
            parts.append(_dot(lat[:, 2 * j * MLA_KV_LORA:(2 * j + 1) * MLA_KV_LORA], wuv_ref[2 * j])
                         + _dot(lat[:, (2 * j + 1) * MLA_KV_LORA:(2 * j + 2) * MLA_KV_LORA], wuv_ref[2 * j + 1]))
        o_b = jnp.concatenate(parts, axis=1)
    else:
        o_b = ob_ref[0].astype(F32)
    ga = ga_ref[0].astype(F32)
    gb = gb_ref[0].astype(F32)
    pa = _dot((oa_ref[0].astype(F32) * (ga * jax.nn.sigmoid(ga))).astype(BF16), wpa_ref[...])
    pb = _dot((o_b * (gb * jax.nn.sigmoid(gb))).astype(BF16), wpb_ref[...])
    h = jax.nn.sigmoid(ma_ref[0].astype(F32)) * pa + jax.nn.sigmoid(mb_ref[0].astype(F32)) * pb
    z = _dot(h.astype(BF16), wout_ref[...])
    y_ref[0] = x_ref[0] + _rms(z, npost_ref[...])


def _merge(x3, o_a, o_b, p, wts, tm, from_latent):
    b, t, _ = x3.shape
    tok = lambda w: pl.BlockSpec((1, tm, w), lambda bb, i: (bb, i, 0))
    return pl.pallas_call(
        functools.partial(_merge_kernel, from_latent),
        grid=(b, t // tm),
        in_specs=[tok(D_MODEL), tok(NSA_WIDTH), tok(o_b.shape[2]), tok(NSA_WIDTH), tok(MLA_WIDTH), tok(D_MODEL),
                  tok(D_MODEL), _full_spec((NSA_WIDTH, D_MODEL)), _full_spec((MLA_WIDTH, D_MODEL)),
                  _full_spec((D_MODEL, D_MODEL)), _full_spec((1, D_MODEL)),
                  _full_spec((MLA_HEADS, MLA_KV_LORA, LANES))],
        out_specs=tok(D_MODEL),
        out_shape=jax.ShapeDtypeStruct((b, t, D_MODEL), F32),
        compiler_params=pltpu.CompilerParams(dimension_semantics=("arbitrary",) * 2, vmem_limit_bytes=VMEM_LIMIT),
        name="merge",
    )(x3, o_a, o_b, p["ga"], p["gb"], p["ma"], p["mb"], wts["wpa"], wts["wpb"], wts["wout"], wts["npost"],
      wts["wuv"])


PAGES_PER_STEP = 32
POOL_PAGES = 4


def _page_copy(source, buf_ref, sem_ref, step, slot, k):
    return pltpu.make_async_copy(source(step, k), buf_ref.at[slot, k], sem_ref.at[slot])


def _table_pages(pt_ref, cache_ref):
    return lambda step, k: cache_ref.at[pt_ref[step * PAGES_PER_STEP + k]]


def _paged_pipeline(streams, step=None, last=None):
    if step is None:
        step = pl.program_id(0) * pl.num_programs(1) + pl.program_id(1)
        last = pl.num_programs(0) * pl.num_programs(1) - 1
    slot = step % 2

    @pl.when(step == 0)
    def _():
        for source, buf_ref, sem_ref in streams:
            for k in range(buf_ref.shape[1]):
                _page_copy(source, buf_ref, sem_ref, 0, 0, k).start()

    @pl.when(step < last)
    def _():
        for source, buf_ref, sem_ref in streams:
            for k in range(buf_ref.shape[1]):
                _page_copy(source, buf_ref, sem_ref, step + 1, 1 - slot, k).start()

    for source, buf_ref, sem_ref in streams:
        for k in range(buf_ref.shape[1]):
            _page_copy(source, buf_ref, sem_ref, step, slot, k).wait()
    return slot


XLU_POOL_PAGES = 16
MXU_POOL_GROUP = 8


def _compress_pages_kernel(pt_ref, cache_ref, pool_ref, pe_ref, o_ref, buf_ref, sem_ref):
    _compress_step(pt_ref, cache_ref, pool_ref, pe_ref, o_ref, buf_ref, sem_ref)


def _compress_step(pt_ref, cache_ref, pool_ref, pe_ref, o_ref, buf_ref, sem_ref, step=None, last=None):
    slot = _paged_pipeline([(_table_pages(pt_ref, cache_ref), buf_ref, sem_ref)], step, last)
    pe_sum = jnp.sum(pe_ref[...], axis=0, keepdims=True)
    pool = pool_ref[...]
    groups = [jnp.concatenate([buf_ref[slot, k + j] for j in range(MXU_POOL_GROUP)], axis=1)
              for k in range(XLU_POOL_PAGES, PAGES_PER_STEP, MXU_POOL_GROUP)]
    splits = [_split_bf16(x) for x in groups]
    by_mxu = [_dot_nt(pool, hi) + _dot_nt(pool, lo) for hi, lo in splits]
    by_xlu = []
    for k in range(XLU_POOL_PAGES):
        rows = buf_ref[slot, k].T
        by_xlu.append(jnp.sum(rows.reshape(rows.shape[0] // CMP_BLOCK, CMP_BLOCK, KV_WIDTH), axis=1))
    o_ref[0] = (jnp.concatenate(by_xlu + by_mxu, axis=0) + pe_sum) * (1.0 / CMP_BLOCK)


def _pool_matrix(page):
    pool = np.zeros((MXU_POOL_GROUP * page // CMP_BLOCK, MXU_POOL_GROUP * page), np.float32)
    for s in range(pool.shape[1]):
        pool[s // CMP_BLOCK, s] = 1.0
    return pool


def _compress_pages(cache_t, pt_flat, pe, b, n_pages):
    page = cache_t.shape[2]
    per_step = PAGES_PER_STEP * page // CMP_BLOCK
    pool = _pool_matrix(page)
    gs = pltpu.PrefetchScalarGridSpec(
        num_scalar_prefetch=1, grid=(b, n_pages // PAGES_PER_STEP),
        in_specs=[pl.BlockSpec(memory_space=pl.ANY),
                  pl.BlockSpec(pool.shape, lambda bb, c, pt: (0, 0)),
                  pl.BlockSpec((CMP_BLOCK, KV_WIDTH), lambda bb, c, pt: (0, 0))],
        out_specs=pl.BlockSpec((1, per_step, KV_WIDTH), lambda bb, c, pt: (bb, c, 0)),
        scratch_shapes=[pltpu.VMEM((2, PAGES_PER_STEP, KV_WIDTH, page), F32), pltpu.SemaphoreType.DMA((2,))])
    return pl.pallas_call(
        _compress_pages_kernel, grid_spec=gs,
        out_shape=jax.ShapeDtypeStruct((b, n_pages * page // CMP_BLOCK, KV_WIDTH), F32),
        compiler_params=pltpu.CompilerParams(dimension_semantics=("arbitrary",) * 2, vmem_limit_bytes=VMEM_LIMIT),
        name="compress_pages",
    )(pt_flat, cache_t, jnp.asarray(pool, BF16), pe)


def _pad_rows(v, rows):
    return jnp.concatenate([v, jnp.zeros((rows - v.shape[0], v.shape[1]), v.dtype)], axis=0)


def _stack_heads(qv, lo):
    a, b = qv[:, :LANES], qv[:, LANES:]
    z = jnp.zeros_like(a)
    return jnp.concatenate([jnp.where(lo, a, z), jnp.where(lo, z, a),
                            jnp.where(lo, b, z), jnp.where(lo, z, b)], axis=0)


def _sample_select_kernel(n_cmp, q_ref, kc_ref, oc_ref, idx_ref, imp_ref):
    bb = pl.program_id(0)
    lo1 = _lane_lo(1)
    lo_c = _lane_lo(n_cmp)
    q = q_ref[0]
    kc = kc_ref[0]
    even = (lax.broadcasted_iota(jnp.int32, (1, LANES), 1) & 1) == 0
    for g in range(NSA_KV_HEADS):
        kk, vv = _dup_kv(kc[:, g * LANES:(g + 1) * LANES], lo_c)
        qs = _pad_rows(_stack_heads(q[:, g * 2 * LANES:(g + 1) * 2 * LANES], lo1), PAD_ROWS)
        s = _dot_nt(qs, kk.astype(BF16))
        e = jnp.exp(s - jnp.max(s, axis=-1, keepdims=True))
        p = e / jnp.sum(e, axis=-1, keepdims=True)
        oc_ref[0, g] = _dot(p.astype(BF16), vv.astype(BF16))
        imp = p[0:1] + p[1:2] + p[2:3] + p[3:4]
        chunks = []
        for k in range(n_cmp // LANES):
            a = imp[:, k * LANES:(k + 1) * LANES]
            chunks.append(a + jnp.where(even, pltpu.roll(a, LANES - 1, 1), pltpu.roll(a, 1, 1)))
        imp_ref[pl.ds(bb * NSA_KV_HEADS + g, 1), :] = jnp.concatenate(chunks, axis=1)

    @pl.when(bb == pl.num_programs(0) - 1)
    def _():
        rows = imp_ref.shape[0]
        blk = lax.broadcasted_iota(jnp.int32, (rows, n_cmp), 1) >> 1
        blk_f = blk.astype(F32)
        slot = lax.broadcasted_iota(jnp.int32, (rows, N_SELECT), 1)
        v = jnp.where(blk == 0, -1.0, imp_ref[...])
        idx = jnp.where(slot == N_SELECT - 1, n_cmp // 2, 0)
        for k in range(1, N_SELECT - 1):
            top = jnp.max(v, axis=-1, keepdims=True)
            jmin = jnp.min(jnp.where(v == top, blk_f, float(n_cmp)), axis=-1, keepdims=True).astype(jnp.int32)
            idx = jnp.where(slot == k, jmin, idx)
            v = jnp.where(blk == jmin, -1.0, v)
        idx_ref[...] = idx


def _sample_select(q, kc_all, b):
    n_cmp = kc_all.shape[1]
    rows = b * NSA_KV_HEADS
    return pl.pallas_call(
        functools.partial(_sample_select_kernel, n_cmp),
        grid=(b,),
        in_specs=[pl.BlockSpec((1, 1, NSA_WIDTH), lambda bb: (bb, 0, 0)),
                  pl.BlockSpec((1, n_cmp, KV_WIDTH), lambda bb: (bb, 0, 0))],
        out_specs=[pl.BlockSpec((1, NSA_KV_HEADS, PAD_ROWS, LANES), lambda bb: (bb, 0, 0, 0)),
                   pl.BlockSpec((rows, N_SELECT), lambda bb: (0, 0))],
        out_shape=[jax.ShapeDtypeStruct((b, NSA_KV_HEADS, PAD_ROWS, LANES), F32),
                   jax.ShapeDtypeStruct((rows, N_SELECT), jnp.int32)],
        scratch_shapes=[pltpu.VMEM((rows, n_cmp), F32)],
        compiler_params=pltpu.CompilerParams(dimension_semantics=("arbitrary",), vmem_limit_bytes=VMEM_LIMIT),
        name="sample_select",
    )(q, kc_all)


def _extra_key_softmax(s_past, vt4_b, s_new, v_new):
    m = jnp.maximum(jnp.max(s_past, axis=-1, keepdims=True), s_new)
    e = jnp.exp(s_past - m)
    e_new = jnp.exp(s_new - m)
    den = jnp.sum(e, axis=-1, keepdims=True) + e_new
    return (_dot_nt(e.astype(BF16), vt4_b) + e_new * v_new) / den


def _sample_attend_kernel(n_pages, idx_ref, pt_ref, slc_ref, qr_ref, newkv_ref, neww_ref, newwf_ref, win_ref,
                          gate_ref, oc_ref, o_ref, wout_ref, buf_ref, sem_ref):
    n_sel = N_SELECT
    bb, g = pl.program_id(0), pl.program_id(1)

    def selected_page(step, k):
        j = idx_ref[step * n_sel + k]
        page = pt_ref[(step // NSA_KV_HEADS) * n_pages + jnp.minimum(j // 2, n_pages - 1)]
        return slc_ref.at[page, pl.ds((step % NSA_KV_HEADS) * LANES, LANES), :]

    slot = _paged_pipeline([(selected_page, buf_ref, sem_ref)])
    pages = [buf_ref.at[slot, k] for k in range(n_sel)]
    lo1 = _lane_lo(1)
    q = qr_ref[0].astype(F32)
    halves = [q[:, 0:LANES], q[:, LANES:2 * LANES]]
    rows = []
    for r in range(NSA_GROUP):
        a = halves[r // 2]
        rows.append(jnp.where(lo1, a if r % 2 == 0 else pltpu.roll(a, HALF, 1), 0.0))
    qs_f = _pad_rows(jnp.concatenate(rows, axis=0), PAD_ROWS)
    qs = qs_f.astype(BF16)
    twice_rows = lambda a: jnp.concatenate([a, a], axis=0)

    s_t = jnp.concatenate([pg[...] for pg in pages], axis=1)
    s_sel = _dot(qs, s_t.astype(BF16))
    base = (bb * NSA_KV_HEADS + g) * n_sel
    biases = []
    for k in range(n_sel - 1):
        odd = (idx_ref[base + k] & 1) == 1
        biases.append(jnp.where(lo1, jnp.where(odd, NEG, 0.0), jnp.where(odd, 0.0, NEG)))
    biases.append(jnp.full((1, LANES), NEG, F32))
    s_sel = s_sel + jnp.concatenate(biases, axis=1)
    nk = newkv_ref[0].astype(F32)
    s_new = jnp.sum(qs_f * nk[:, :LANES], axis=-1, keepdims=True)
    o_s = _extra_key_softmax(s_sel, twice_rows(s_t[HD:2 * HD]).astype(BF16), s_new, nk[:, LANES:])

    w = win_ref[0]
    nw = neww_ref[0].astype(F32)
    s_w = _dot(qs, w.astype(BF16))
    s_wn = jnp.sum(qs_f * nw[:, :LANES], axis=-1, keepdims=True)
    o_w = _extra_key_softmax(s_w, twice_rows(w[HD:2 * HD]).astype(BF16), s_wn, nw[:, LANES:])

    o_c = oc_ref[0, 0]
    gates = gate_ref[0]
    gate = jnp.where(g == 0, gates[:, 0:GATE_ROWS], gates[:, GATE_ROWS:2 * GATE_ROWS])
    heads = []
    for r in range(NSA_GROUP):
        heads.append(gate[:, 3 * r:3 * r + 1] * o_c[r:r + 1] + gate[:, 3 * r + 1:3 * r + 2] * o_s[r:r + 1]
                     + gate[:, 3 * r + 2:3 * r + 3] * o_w[r:r + 1])
    o_ref[0] = jnp.concatenate([jnp.where(lo1, heads[0], heads[1]), jnp.where(lo1, heads[2], heads[3])], axis=1)

    n_feat, n_w = w.shape
    new_row = jnp.broadcast_to(newwf_ref[0], (n_feat, n_feat))
    diag = (lax.broadcasted_iota(jnp.int32, (n_feat, n_feat), 0)
            == lax.broadcasted_iota(jnp.int32, (n_feat, n_feat), 1))
    new_col = jnp.sum(jnp.where(diag, new_row, 0.0), axis=1, keepdims=True)
    last = lax.broadcasted_iota(jnp.int32, (n_feat, LANES), 1) == LANES - 1
    chunks = []
    n_ch = n_w // LANES
    for c in range(n_ch):
        cur = pltpu.roll(w[:, c * LANES:(c + 1) * LANES], LANES - 1, 1)
        if c + 1 < n_ch:
            nxt = pltpu.roll(w[:, (c + 1) * LANES:(c + 2) * LANES], LANES - 1, 1)
        else:
            nxt = jnp.broadcast_to(new_col, (n_feat, LANES))
        chunks.append(jnp.where(last, nxt, cur))
    wout_ref[0] = jnp.concatenate(chunks, axis=1)


def _sample_attend(p, oc, idx_flat, pt_flat, slc_t, win_t, b, n_pages):
    page = slc_t.shape[2]
    win_len = win_t.shape[2]
    assert page == 2 * SEL_BLOCK
    row = lambda w: pl.BlockSpec((1, 1, w), lambda bb, g, idx, pt: (bb, 0, g))
    wspec = pl.BlockSpec((1, LANES, win_len), lambda bb, g, idx, pt: (bb, g, 0))
    gs = pltpu.PrefetchScalarGridSpec(
        num_scalar_prefetch=2, grid=(b, NSA_KV_HEADS),
        in_specs=[pl.BlockSpec(memory_space=pl.ANY),
                  row(2 * LANES), row(2 * LANES), row(2 * LANES), row(LANES), wspec,
                  pl.BlockSpec((1, 1, NSA_KV_HEADS * GATE_ROWS), lambda bb, g, idx, pt: (bb, 0, 0)),
                  pl.BlockSpec((1, 1, PAD_ROWS, LANES), lambda bb, g, idx, pt: (bb, g, 0, 0))],
        out_specs=[row(2 * LANES), wspec],
        scratch_shapes=[pltpu.VMEM((2, N_SELECT, LANES, page), F32), pltpu.SemaphoreType.DMA((2,))])
    return pl.pallas_call(
        functools.partial(_sample_attend_kernel, n_pages), grid_spec=gs,
        out_shape=[jax.ShapeDtypeStruct((b, 1, NSA_WIDTH), F32),
                   jax.ShapeDtypeStruct((b, KV_WIDTH, win_len), F32)],
        compiler_params=pltpu.CompilerParams(dimension_semantics=("arbitrary",) * 2, vmem_limit_bytes=VMEM_LIMIT),
        name="sample_attend",
    )(idx_flat, pt_flat, slc_t, p["qrot"], p["slckv"], p["winkv"], p["win"], win_t, p["gate"], oc)


def _softmax_update(s, v_b, m, l, acc):
    m_new = jnp.maximum(m, jnp.max(s, axis=-1, keepdims=True))
    p = jnp.exp(s - m_new)
    alpha = jnp.exp(m - m_new)
    return m_new, alpha * l + jnp.sum(p, axis=-1, keepdims=True), alpha * acc + _dot(p.astype(BF16), v_b)


def _mla_sample_kernel(pt_ref, lat_ref, krt_ref, qlat_ref, qpe_ref, cnew_ref, krnew_ref, o_ref,
                       m_ref, l_ref, acc_ref, lat_buf, kr_buf, lat_sem, kr_sem):
    n = PAGES_PER_STEP
    slot = _paged_pipeline([(_table_pages(pt_ref, lat_ref), lat_buf, lat_sem),
                            (_table_pages(pt_ref, krt_ref), kr_buf, kr_sem)])
    lat_pages = [lat_buf.at[slot, k] for k in range(n)]
    kr_pages = [kr_buf.at[slot, k] for k in range(n)]
    step = pl.program_id(1)
    qlat = _pad_rows(qlat_ref[0], PAD_ROWS)
    qpe = _pad_rows(qpe_ref[0], PAD_ROWS)

    @pl.when(step == 0)
    def _():
        c_new = cnew_ref[0]
        s_new = (jnp.sum(qlat * c_new, axis=-1, keepdims=True)
                 + jnp.sum(qpe * krnew_ref[0], axis=-1, keepdims=True))
        m_ref[...] = s_new
        l_ref[...] = jnp.ones(l_ref.shape, F32)
        acc_ref[...] = jnp.broadcast_to(c_new, acc_ref.shape)

    qlat_b, qpe_b = qlat.astype(BF16), qpe.astype(BF16)
    groups = range(0, n, POOL_PAGES)
    c_bs = [jnp.concatenate([pg[...] for pg in lat_pages[k:k + POOL_PAGES]], axis=0).astype(BF16) for k in groups]
    kr_bs = [jnp.concatenate([pg[...] for pg in kr_pages[k:k + POOL_PAGES]], axis=1).astype(BF16) for k in groups]
    s_lat = [_dot_nt(qlat_b, c_b) for c_b in c_bs]
    s_pe = [_dot(qpe_b, kr_b) for kr_b in kr_bs]
    scores = [a + r for a, r in zip(s_lat, s_pe)]
    maxes = [jnp.max(s, axis=-1, keepdims=True) for s in scores]
    ps = [jnp.exp(s - mx) for s, mx in zip(scores, maxes)]
    sums = [jnp.sum(p_g, axis=-1, keepdims=True) for p_g in ps]
    accs = [_dot(p_g.astype(BF16), c_b) for p_g, c_b in zip(ps, c_bs)]
    m_old = m_ref[...]
    m = m_old
    for mx in maxes:
        m = jnp.maximum(m, mx)
    alpha = jnp.exp(m_old - m)
    l = alpha * l_ref[...]
    acc = alpha * acc_ref[...]
    for mx, l_g, acc_g in zip(maxes, sums, accs):
        w = jnp.exp(mx - m)
        l = l + w * l_g
        acc = acc + w * acc_g
    m_ref[...] = m
    l_ref[...] = l
    acc_ref[...] = acc

    @pl.when(step == pl.num_programs(1) - 1)
    def _():
        o_ref[0] = (acc / l)[0:MLA_HEADS]


def _mla_sample(p, lat_cache, kr_t, pt_flat, b, n_pages):
    page = lat_cache.shape[1]
    head = lambda w: pl.BlockSpec((1, MLA_HEADS, w), lambda bb, c, pt: (bb, 0, 0))
    row = lambda w: pl.BlockSpec((1, 1, w), lambda bb, c, pt: (bb, 0, 0))
    gs = pltpu.PrefetchScalarGridSpec(
        num_scalar_prefetch=1, grid=(b, n_pages // PAGES_PER_STEP),
        in_specs=[pl.BlockSpec(memory_space=pl.ANY), pl.BlockSpec(memory_space=pl.ANY),
                  head(MLA_KV_LORA), head(MLA_ROPE), row(MLA_KV_LORA), row(MLA_ROPE)],
        out_specs=head(MLA_KV_LORA),
        scratch_shapes=[pltpu.VMEM((PAD_ROWS, 1), F32), pltpu.VMEM((PAD_ROWS, 1), F32),
                        pltpu.VMEM((PAD_ROWS, MLA_KV_LORA), F32),
                        pltpu.VMEM((2, PAGES_PER_STEP, page, MLA_KV_LORA), F32),
                        pltpu.VMEM((2, PAGES_PER_STEP, MLA_ROPE, page), F32),
                        pltpu.SemaphoreType.DMA((2,)), pltpu.SemaphoreType.DMA((2,))])
    qlat = p["qlat"].reshape(b, MLA_HEADS, MLA_KV_LORA)
    qpe = p["qpe"].reshape(b, MLA_HEADS, MLA_ROPE)
    return pl.pallas_call(
        _mla_sample_kernel, grid_spec=gs,
        out_shape=jax.ShapeDtypeStruct((b, MLA_HEADS, MLA_KV_LORA), F32),
        compiler_params=pltpu.CompilerParams(dimension_semantics=("arbitrary",) * 2, vmem_limit_bytes=VMEM_LIMIT),
        name="mla_sample",
    )(pt_flat, lat_cache, kr_t, qlat, qpe, p["c"], p["kr"])


def _rope_angles(pos, theta, dim):
    half = dim // 2
    inv = 1.0 / (float(theta) ** (np.arange(half, dtype=np.float64) / half))
    ang = np.asarray(pos, np.float64)[:, None] * inv[None, :]
    return np.cos(ang).astype(np.float32), np.sin(ang).astype(np.float32)


def _rope_lane_tables(pos, rows):
    def table(theta, dim, period, active):
        half = dim // 2
        cos, sin = _rope_angles(pos, theta, dim)
        lane = np.arange(LANES)
        d = lane % period
        is_lo = (d < half) & active(lane)
        is_hi = (d >= half) & (d < dim) & active(lane)
        fi = np.where(d < half, d, np.clip(d - half, 0, half - 1))
        cos_l, sin_l = cos[:, fi], sin[:, fi]
        tab = np.stack([np.where(is_lo | is_hi, cos_l, 1.0), np.where(is_lo, -sin_l, 0.0),
                        np.where(is_hi, sin_l, 0.0)]).astype(np.float32)
        return jnp.asarray(np.broadcast_to(tab, (3, rows, LANES)))

    every = lambda lane: np.ones_like(lane, bool)
    keys_only = lambda lane: (lane % LANES) < HD
    return (table(ROPE_THETA, ROT_DIM, HD, every), table(ROPE_THETA, ROT_DIM, HD, keys_only),
            table(MLA_ROPE_THETA, MLA_ROPE, MLA_ROPE, every))


def _rope_row_tables(pos):
    cq, sq = _rope_angles(pos, ROPE_THETA, ROT_DIM)
    cm, sm = _rope_angles(pos, MLA_ROPE_THETA, MLA_ROPE)
    return jnp.asarray(np.stack([cq.T, sq.T])), jnp.asarray(np.stack([cm.T, sm.T]))


def _pack_weights(l, norm_pre, w_in, pe_cmp, q_norm, w_q_up, kv_norm, w_kv_up, w_proj_a, w_proj_b, w_out, norm_post):
    w = w_in[l].astype(BF16)
    o = IN_OFFSETS
    seg = lambda k: w[:, o[k]:o[k + 1]]
    gn = seg(4)
    per_group = 3 * NSA_GROUP
    gn_t = jnp.zeros((NSA_KV_HEADS * GATE_ROWS, D_MODEL), w.dtype)
    for g in range(NSA_KV_HEADS):
        gn_t = gn_t.at[g * GATE_ROWS:g * GATE_ROWS + per_group].set(gn[:, g * per_group:(g + 1) * per_group].T)
    w_t = jnp.concatenate([seg(0).T, seg(1).T, seg(2).T, seg(3).T, gn_t, seg(8).T], axis=0)
    w_krp4 = jnp.tile(seg(8).T, (LANES // MLA_ROPE, 1))
    w_cols = jnp.concatenate([seg(5), seg(6), seg(7), seg(9), seg(10), seg(11)], axis=1)
    wq = w_q_up[l]
    wqup = jnp.concatenate([wq[..., :MLA_NOPE].reshape(MLA_Q_LORA, -1), wq[..., MLA_NOPE:].reshape(MLA_Q_LORA, -1)],
                           axis=1)
    wkv = w_kv_up[l]
    wuk_pad = jnp.pad(wkv[..., :MLA_NOPE], ((0, 0), (0, 0), (0, LANES - MLA_NOPE)))
    rope_copy = jnp.pad(jnp.eye(MLA_ROPE, dtype=w.dtype), ((0, LANES - MLA_ROPE), (MLA_NOPE, MLA_ROPE)))
    wkx = jnp.concatenate([wuk_pad.reshape(MLA_KV_LORA, MLA_HEADS * LANES), jnp.tile(rope_copy, (1, MLA_HEADS))],
                          axis=0)
    wvt = jnp.transpose(wkv[..., MLA_NOPE:], (1, 2, 0)).reshape(MLA_WIDTH, MLA_KV_LORA)
    w2uk =jnp.transpose(wkv[..., :MLA_NOPE], (1, 2, 0)).reshape(MLA_HEADS // 2, LANES, MLA_KV_LORA)
    wv = jnp.transpose(wkv[..., MLA_NOPE:], (1, 0, 2))
    zeros = jnp.zeros_like(wv)
    even = (jnp.arange(MLA_HEADS) % 2 == 0)[:, None, None]
    wuv = jnp.concatenate([jnp.where(even, wv, zeros), jnp.where(even, zeros, wv)], axis=2)
    return {
        "npre": norm_pre[l][None].astype(F32), "w_t": w_t, "w_krp4": w_krp4, "w_cols": w_cols, "pe": pe_cmp[l].reshape(CMP_BLOCK, KV_WIDTH).astype(F32),
        "qnorm": q_norm[l][None].astype(F32), "wqup": wqup.astype(BF16), "wqupt": wqup.T.astype(BF16),
        "kvnorm": kv_norm[l][None].astype(F32), "w2uk": w2uk.astype(BF16), "wkx": wkx.astype(BF16),
        "wvt": wvt.astype(BF16), "wuv": wuv.astype(BF16),
        "wpa": w_proj_a[l].astype(BF16), "wpb": w_proj_b[l].astype(BF16), "wout": w_out[l].astype(BF16),
        "npost": norm_post[l][None].astype(F32),
    }


def _rows_from_cols(a):
    b, _, t = a.shape
    return a.reshape(b, NSA_KV_HEADS, 2, HD, t).transpose(0, 4, 1, 2, 3)


def _cols_from_rows(a):
    n, t = a.shape[:2]
    return a.transpose(0, 2, 3, 4, 1).reshape(n, KV_WIDTH, t)


def _prompt_layer(x, wts, compress_job):
    b, t, _ = x.shape
    assert t % PROMPT_TM == 0
    p = _in_project_cols(x, _rope_row_tables(np.arange(t)), wts)
    o_a, kc_all = _nsa_prompt(p, b, t, compress_job)
    o_b = _mla_prompt(p, wts, b, t)
    y = _merge(x, o_a, o_b, p, wts, 512, from_latent=False)
    win_keep = min(WINDOW, t)
    return y, (_rows_from_cols(p["cmp"]), _rows_from_cols(p["slc"]), p["c"], p["kr"].transpose(0, 2, 1),
               _rows_from_cols(p["win"][:, :, t - win_keep:])), kc_all


def _sample_layer(x, l, caches, state_win, page_table, wts, kc_all):
    cache_cmp, cache_slc, cache_lat, cache_kr = caches
    b, s_new, _ = x.shape
    assert s_new == 1
    n_pages = page_table.shape[1]
    page = cache_cmp.shape[2]
    past = n_pages * page
    assert past % SEL_BLOCK == 0 and n_pages % PAGES_PER_STEP == 0 and page == LANES
    win_len = state_win.shape[1]
    assert win_len == WINDOW and past >= WINDOW
    pt_flat = page_table.reshape(-1).astype(jnp.int32)

    tabs = _rope_lane_tables(np.full((1,), past), b)
    p = _in_project_rows(x.reshape(1, b, D_MODEL), tabs, wts)
    p = {k: v.reshape(b, 1, v.shape[-1]) for k, v in p.items()}

    if kc_all is None:
        kc_all = _compress_pages(_cols_from_rows(cache_cmp[l]), pt_flat, wts["pe"], b, n_pages)
    oc, idx = _sample_select(p["q"], kc_all, b)
    o_a, new_win = _sample_attend(p, oc, idx.reshape(-1), pt_flat, _cols_from_rows(cache_slc[l]),
                                  _cols_from_rows(state_win), b, n_pages)
    o_lat = _mla_sample(p, cache_lat[l], cache_kr[l].transpose(0, 2, 1), pt_flat, b, n_pages)
    pm = {k: p[k].reshape(1, b, -1) for k in ("ga", "gb", "ma", "mb")}
    y = _merge(x.reshape(1, b, D_MODEL), o_a.reshape(1, b, NSA_WIDTH),
               o_lat.reshape(1, b, MLA_HEADS * MLA_KV_LORA), pm, wts, b, from_latent=True)
    kv6 = lambda a: a.reshape(b, 1, NSA_KV_HEADS, 2, HD)
    return y.reshape(b, 1, D_MODEL), (kv6(p["cmp"]), kv6(p["slc"]), p["c"], p["kr"], _rows_from_cols(new_win))


def kernel(x_prompt, x_sample, cache_nsa_cmp, cache_nsa_slc, cache_mla_latent, cache_mla_krope, state_nsa_win,
           page_table, norm_pre, w_in, pe_cmp, q_norm, w_q_up, kv_norm, w_kv_up, w_proj_a, w_proj_b, w_out,
           norm_post):
    depth = w_in.shape[0]
    hp, hs = x_prompt, x_sample
    new_p, new_s = [], []
    for l in range(depth):
        wts = _pack_weights(l, norm_pre, w_in, pe_cmp, q_norm, w_q_up, kv_norm, w_kv_up, w_proj_a, w_proj_b,
                            w_out, norm_post)
        n_pages = page_table.shape[1]
        compress_job = None
        if n_pages % PAGES_PER_STEP == 0:
            compress_job = (_cols_from_rows(cache_nsa_cmp[l]), page_table.reshape(-1).astype(jnp.int32), wts["pe"],
                            page_table.shape[0], n_pages)
        hp, sp, kc_all = _prompt_layer(hp, wts, compress_job)
        hs, ss = _sample_layer(hs, l, (cache_nsa_cmp, cache_nsa_slc, cache_mla_latent, cache_mla_krope),
                               state_nsa_win[l], page_table, wts, kc_all)
        new_p.append(sp)
        new_s.append(ss)
    stack = lambda items, k: jnp.stack([s[k] for s in items])
    return (hp, hs) + tuple(stack(new_p, k) for k in range(5)) + tuple(stack(new_s, k) for k in range(5))
```

```python
import functools

import numpy as np
import jax
import jax.numpy as jnp
from jax import lax
from jax.experimental import pallas as pl
from jax.experimental.pallas import tpu as pltpu

D_MODEL = 1024
NSA_HEADS = 8
NSA_KV_HEADS = 2
NSA_GROUP = NSA_HEADS // NSA_KV_HEADS
HD = 64
NSA_WIDTH = NSA_HEADS * HD
KV_WIDTH = NSA_KV_HEADS * 2 * HD
ROT_DIM = HD // 4
ROPE_THETA = 500000.0
CMP_BLOCK = 32
SEL_BLOCK = 64
N_SELECT = 16
WINDOW = 512
NSA_SCALE = HD ** -0.5

MLA_HEADS = 8
MLA_Q_LORA = 384
MLA_KV_LORA = 256
MLA_NOPE = 64
MLA_ROPE = 32
MLA_V = 64
MLA_WIDTH = MLA_HEADS * MLA_V
MLA_ROPE_THETA = 10000.0
MLA_SCALE = (MLA_NOPE + MLA_ROPE) ** -0.5
LOG2E = 1.4426950408889634
SHIFT_LIMIT = 40.0

RMS_EPS = 1e-6
NEG = -1e30
FORCE_SCORE = 1e4

IN_SPLITS = (NSA_WIDTH, KV_WIDTH, KV_WIDTH, KV_WIDTH, 3 * NSA_HEADS, NSA_WIDTH,
             MLA_Q_LORA, MLA_KV_LORA, MLA_ROPE, MLA_WIDTH, D_MODEL, D_MODEL)
IN_OFFSETS = tuple(int(v) for v in np.cumsum((0,) + IN_SPLITS))

LANES = 128
HALF = LANES // 2
GATE_ROWS = 16
PAD_ROWS = 16

PT_Q = (0, 512)
PT_CMP = (512, 768)
PT_SLC = (768, 1024)
PT_WIN = (1024, 1280)
PT_GN = (1280, 1280 + NSA_KV_HEADS * GATE_ROWS)
PT_KRP = (PT_GN[1], PT_GN[1] + MLA_ROPE)
PT_ROWS = PT_KRP[1]
PR_GA = (0, 512)
PR_QD = (512, 896)
PR_KVD = (896, 1152)
PR_GB = (1152, 1664)
PR_MA = (1664, 2688)
PR_MB = (2688, 3712)
PR_COLS = 3712

PROMPT_TM = 512
ATT_TQ = 256
ATT_TK = 256
MLA_TK = 256

VMEM_LIMIT = 48 * 1024 * 1024
BF16 = jnp.bfloat16
F32 = jnp.float32


def _full_spec(shape):
    nd = len(shape)
    return pl.BlockSpec(shape, lambda *_: (0,) * nd)


def _lane_lo(rows):
    return lax.broadcasted_iota(jnp.int32, (rows, LANES), 1) < HALF


def _dot(a, b):
    return jnp.dot(a, b, preferred_element_type=F32)


def _dot_nt(a, b):
    return lax.dot_general(a, b, (((1,), (1,)), ((), ())), preferred_element_type=F32)


def _rms(v, gain):
    return v * lax.rsqrt(jnp.mean(v * v, axis=-1, keepdims=True) + RMS_EPS) * gain


def _split_bf16(v):
    hi = v.astype(BF16)
    return hi, (v - hi.astype(F32)).astype(BF16)


def _rope_tiles(v, tab_ref, shift):
    c, s_lo, s_hi = tab_ref[0], tab_ref[1], tab_ref[2]
    out = []
    for k in range(v.shape[1] // LANES):
        a = v[:, k * LANES:(k + 1) * LANES]
        out.append(a * c + pltpu.roll(a, LANES - shift, 1) * s_lo + pltpu.roll(a, shift, 1) * s_hi)
    return out[0] if len(out) == 1 else jnp.concatenate(out, axis=1)


def _dup_kv(a, lo):
    r = pltpu.roll(a, HALF, 1)
    return jnp.where(lo, a, r), jnp.where(lo, r, a)


def _kv_pack(v):
    lo = _lane_lo(v.shape[0])
    parts = []
    for g in range(NSA_KV_HEADS):
        kk, vv = _dup_kv(v[:, g * LANES:(g + 1) * LANES], lo)
        parts += [kk, vv]
    return jnp.concatenate(parts, axis=1).astype(BF16)


def _inproj_rows_kernel(x_ref, npre_ref, wt_ref, wkrp_ref, w_ref, tq_ref, tkv_ref, tm_ref, qnorm_ref, wqup_ref,
                        kvnorm_ref, w2uk_ref, q_ref, qrot_ref, gate_ref, cmp_ref, slc_ref, win_ref, slckv_ref,
                        winkv_ref, ga_ref, gb_ref, ma_ref, mb_ref, qpe_ref, c_ref, kr_ref, qlat_ref):
    xb = _rms(x_ref[0], npre_ref[...]).astype(BF16)
    segt = lambda lohi: _dot_nt(xb, wt_ref[lohi[0]:lohi[1], :])
    seg = lambda lohi: _dot(xb, w_ref[:, lohi[0]:lohi[1]])

    a = segt((PT_Q[0], PT_WIN[1]))
    q = a[:, :NSA_WIDTH]
    q_ref[0] = (q * NSA_SCALE).astype(BF16)
    qrot_ref[0] = (_rope_tiles(q, tq_ref, ROT_DIM // 2) * NSA_SCALE).astype(BF16)
    cmp_ref[0] = a[:, NSA_WIDTH:NSA_WIDTH + KV_WIDTH]
    kvs = _rope_tiles(a[:, NSA_WIDTH + KV_WIDTH:NSA_WIDTH + 2 * KV_WIDTH], tkv_ref, ROT_DIM // 2)
    slc_ref[0] = kvs
    slckv_ref[0] = _kv_pack(kvs)
    kvw = _rope_tiles(a[:, NSA_WIDTH + 2 * KV_WIDTH:], tkv_ref, ROT_DIM // 2)
    win_ref[0] = kvw
    winkv_ref[0] = _kv_pack(kvw)

    gate_ref[0] = jax.nn.sigmoid(segt(PT_GN))
    ga_ref[0] = seg(PR_GA).astype(BF16)
    gb_ref[0] = seg(PR_GB).astype(BF16)
    ma_ref[0] = seg(PR_MA).astype(BF16)
    mb_ref[0] = seg(PR_MB).astype(BF16)

    qd = _rms(seg(PR_QD), qnorm_ref[...]).astype(BF16)
    qh = _dot(qd, wqup_ref[...])
    qn = qh[:, :MLA_HEADS * MLA_NOPE].astype(BF16)
    qpe_ref[0] = _rope_tiles(qh[:, MLA_HEADS * MLA_NOPE:], tm_ref, MLA_ROPE // 2) * MLA_SCALE
    c_ref[0] = _rms(seg(PR_KVD), kvnorm_ref[...])
    kr_ref[0] = _rope_tiles(_dot_nt(xb, wkrp_ref[...]), tm_ref, MLA_ROPE // 2)[:, :MLA_ROPE]

    rows = qn.shape[0]
    lo = _lane_lo(rows)
    z = jnp.zeros((rows, LANES), BF16)
    parts = []
    for j in range(MLA_HEADS // 2):
        pair = qn[:, j * LANES:(j + 1) * LANES]
        parts.append(_dot(jnp.where(lo, pair, z), w2uk_ref[j]) * MLA_SCALE)
        parts.append(_dot(jnp.where(lo, z, pair), w2uk_ref[j]) * MLA_SCALE)
    qlat_ref[0] = jnp.concatenate(parts, axis=1)


def _in_project_rows(x3, tabs, wts):
    b, t, _ = x3.shape
    tq, tkv, tmla = tabs
    row = lambda w, dt: jax.ShapeDtypeStruct((b, t, w), dt)
    names = ["q", "qrot", "gate", "cmp", "slc", "win", "slckv", "winkv", "ga", "gb", "ma", "mb", "qpe", "c", "kr",
             "qlat"]
    out_shape = [row(512, BF16), row(512, BF16), row(NSA_KV_HEADS * GATE_ROWS, F32), row(256, F32), row(256, F32),
                 row(256, F32),
                 row(512, BF16), row(512, BF16), row(512, BF16), row(512, BF16), row(1024, BF16), row(1024, BF16),
                 row(MLA_HEADS * MLA_ROPE, F32), row(MLA_KV_LORA, F32), row(MLA_ROPE, F32),
                 row(MLA_HEADS * MLA_KV_LORA, F32)]
    tok = lambda w: pl.BlockSpec((1, t, w), lambda bb: (bb, 0, 0))
    tab = _full_spec((3, t, LANES))
    in_specs = [tok(D_MODEL), _full_spec((1, D_MODEL)), _full_spec((PT_ROWS, D_MODEL)), _full_spec((LANES, D_MODEL)),
                _full_spec((D_MODEL, PR_COLS)), tab, tab, tab, _full_spec((1, MLA_Q_LORA)), _full_spec((MLA_Q_LORA, MLA_HEADS * (MLA_NOPE + MLA_ROPE))),
                _full_spec((1, MLA_KV_LORA)), _full_spec((MLA_HEADS // 2, LANES, MLA_KV_LORA))]
    res = pl.pallas_call(
        _inproj_rows_kernel, grid=(b,), in_specs=in_specs, out_specs=[tok(s.shape[2]) for s in out_shape],
        out_shape=out_shape,
        compiler_params=pltpu.CompilerParams(dimension_semantics=("arbitrary",), vmem_limit_bytes=VMEM_LIMIT),
        name="in_project_rows",
    )(x3, wts["npre"], wts["w_t"], wts["w_krp4"], wts["w_cols"], tq, tkv, tmla, wts["qnorm"], wts["wqup"],
      wts["kvnorm"], wts["w2uk"])
    return dict(zip(names, res))


def _rope_rows(x, cos, sin, half):
    x1, x2 = x[0:half], x[half:2 * half]
    parts = [x1 * cos - x2 * sin, x1 * sin + x2 * cos]
    if x.shape[0] > 2 * half:
        parts.append(x[2 * half:])
    return jnp.concatenate(parts, axis=0)


def _store_chunks(ref, v):
    tk = ref.shape[3]
    for j in range(ref.shape[1]):
        ref[0, j] = v[:, j * tk:(j + 1) * tk]


def _inproj_cols_kernel(x_ref, npre_ref, wt_ref, w_ref, ropeq_ref, ropem_ref, pe_ref, qnorm_ref, wqupt_ref,
                        kvnorm_ref, pool_ref, wkx_ref, wvt_ref,
                        qt_ref, qrt_ref, gate_ref, cmp_ref, slc_ref, win_ref, slcb_ref, winb_ref, slcr_ref,
                        winr_ref, kc_ref, ga_ref, gb_ref, ma_ref, mb_ref, qmt_ref, c_ref, kmla_ref, vmt_ref,
                        kr_ref, kn_ref):
    tm = x_ref.shape[1]
    xb = _rms(x_ref[0], npre_ref[...]).astype(BF16)
    segt = lambda lohi: _dot_nt(wt_ref[lohi[0]:lohi[1], :], xb)
    seg = lambda lohi: _dot(xb, w_ref[:, lohi[0]:lohi[1]])
    cq, sq = ropeq_ref[0], ropeq_ref[1]
    cm, sm = ropem_ref[0], ropem_ref[1]
    hq, hm = ROT_DIM // 2, MLA_ROPE // 2

    qt = segt(PT_Q)
    qt_ref[0] = (qt * (NSA_SCALE * LOG2E)).astype(BF16)
    qrt = jnp.concatenate([_rope_rows(qt[h * HD:(h + 1) * HD], cq, sq, hq) for h in range(NSA_HEADS)], axis=0)
    qrt_ref[0] = (qrt * (NSA_SCALE * LOG2E)).astype(BF16)

    cmpt = segt(PT_CMP)
    cmp_ref[0] = cmpt
    hi, lo = _split_bf16(cmpt)
    pool = pool_ref[...]
    pooled = (_dot_nt(pool, hi) + _dot_nt(pool, lo))[0:tm // CMP_BLOCK]
    kc_ref[0] = (pooled + jnp.sum(pe_ref[...], axis=0, keepdims=True)) * (1.0 / CMP_BLOCK)

    def rope_kv(v):
        parts = []
        for g in range(NSA_KV_HEADS):
            parts.append(_rope_rows(v[g * LANES:g * LANES + HD], cq, sq, hq))
            parts.append(v[g * LANES + HD:(g + 1) * LANES])
        return jnp.concatenate(parts, axis=0)

    slct = rope_kv(segt(PT_SLC))
    slc_ref[0] = slct
    _store_chunks(slcb_ref, slct.astype(BF16))
    slcr_ref[0] = slct.T.astype(BF16)
    wint = rope_kv(segt(PT_WIN))
    win_ref[0] = wint
    _store_chunks(winb_ref, wint.astype(BF16))
    winr_ref[0] = wint.T.astype(BF16)

    gate_ref[0] = jax.nn.sigmoid(segt(PT_GN))
    krt = _rope_rows(segt(PT_KRP), cm, sm, hm)
    kr_ref[0] = krt
    c = _rms(seg(PR_KVD), kvnorm_ref[...])
    c_ref[0] = c
    c_b = c.astype(BF16)
    kr_rows = jnp.concatenate([krt, jnp.zeros((LANES - MLA_ROPE, tm), F32)], axis=0).T
    ckr = jnp.concatenate([c_b, kr_rows.astype(BF16)], axis=1)
    kmla = _dot(ckr, wkx_ref[...]).astype(BF16)
    kmla_ref[0] = kmla
    kf = kmla.astype(F32)
    norms = []
    for h in range(MLA_HEADS):
        n2 = jnp.sum(kf[:, h * LANES:(h + 1) * LANES] ** 2, axis=1, keepdims=True)
        norms.append(jnp.broadcast_to(jnp.sqrt(jnp.max(n2, axis=0, keepdims=True)), (1, LANES)))
    kn_ref[0] = jnp.concatenate(norms, axis=0)
    _store_chunks(vmt_ref, _dot_nt(wvt_ref[...], c_b).astype(BF16))

    qd = _rms(seg(PR_QD), qnorm_ref[...]).astype(BF16)
    qht = _dot_nt(wqupt_ref[...], qd)
    n_nope = MLA_HEADS * MLA_NOPE
    zq = jnp.zeros((LANES - MLA_NOPE - MLA_ROPE, tm), F32)
    parts = []
    for h in range(MLA_HEADS):
        parts += [qht[h * MLA_NOPE:(h + 1) * MLA_NOPE],
                  _rope_rows(qht[n_nope + h * MLA_ROPE:n_nope + (h + 1) * MLA_ROPE], cm, sm, hm), zq]
    qmt_ref[0] = (jnp.concatenate(parts, axis=0) * (MLA_SCALE * LOG2E)).astype(BF16)

    ga_ref[0] = seg(PR_GA).astype(BF16)
    gb_ref[0] = seg(PR_GB).astype(BF16)
    ma_ref[0] = seg(PR_MA).astype(BF16)
    mb_ref[0] = seg(PR_MB).astype(BF16)


def _in_project_cols(x3, ropes, wts):
    b, t, _ = x3.shape
    tm = PROMPT_TM
    nt = t // tm
    ropeq, ropem = ropes
    pool = np.zeros((16, tm), np.float32)
    for s in range(tm):
        pool[s // CMP_BLOCK, s] = 1.0
    sds = jax.ShapeDtypeStruct
    rows = lambda w: pl.BlockSpec((1, tm, w), lambda i, bb: (bb, i, 0))
    cols = lambda w: pl.BlockSpec((1, w, tm), lambda i, bb: (bb, 0, i))
    chunk = lambda w, tk: pl.BlockSpec((1, tm // tk, w, tk), lambda i, bb: (bb, i, 0, 0))
    outs = [
        ("qt", sds((b, NSA_WIDTH, t), BF16), cols(NSA_WIDTH)),
        ("qrt", sds((b, NSA_WIDTH, t), BF16), cols(NSA_WIDTH)),
        ("gate", sds((b, NSA_KV_HEADS * GATE_ROWS, t), F32), cols(NSA_KV_HEADS * GATE_ROWS)),
        ("cmp", sds((b, KV_WIDTH, t), F32), cols(KV_WIDTH)),
        ("slc", sds((b, KV_WIDTH, t), F32), cols(KV_WIDTH)),
        ("win", sds((b, KV_WIDTH, t), F32), cols(KV_WIDTH)),
        ("slcb", sds((b, t // ATT_TK, KV_WIDTH, ATT_TK), BF16), chunk(KV_WIDTH, ATT_TK)),
        ("winb", sds((b, t // ATT_TK, KV_WIDTH, ATT_TK), BF16), chunk(KV_WIDTH, ATT_TK)),
        ("slcr", sds((b, t, KV_WIDTH), BF16), rows(KV_WIDTH)),
        ("winr", sds((b, t, KV_WIDTH), BF16), rows(KV_WIDTH)),
        ("kc", sds((b, t // CMP_BLOCK, KV_WIDTH), F32),
         pl.BlockSpec((1, tm // CMP_BLOCK, KV_WIDTH), lambda i, bb: (bb, i, 0))),
        ("ga", sds((b, t, NSA_WIDTH), BF16), rows(NSA_WIDTH)),
        ("gb", sds((b, t, MLA_WIDTH), BF16), rows(MLA_WIDTH)),
        ("ma", sds((b, t, D_MODEL), BF16), rows(D_MODEL)),
        ("mb", sds((b, t, D_MODEL), BF16), rows(D_MODEL)),
        ("qmt", sds((b, MLA_HEADS * LANES, t), BF16), cols(MLA_HEADS * LANES)),
        ("c", sds((b, t, MLA_KV_LORA), F32), rows(MLA_KV_LORA)),
        ("kmla", sds((b, t, MLA_HEADS * LANES), BF16), rows(MLA_HEADS * LANES)),
        ("vmt", sds((b, t // MLA_TK, MLA_WIDTH, MLA_TK), BF16), chunk(MLA_WIDTH, MLA_TK)),
        ("kr", sds((b, MLA_ROPE, t), F32), cols(MLA_ROPE)),
        ("kn", sds((b, nt * MLA_HEADS, LANES), F32), pl.BlockSpec((1, MLA_HEADS, LANES), lambda i, bb: (bb, i, 0))),
    ]
    rope_spec = lambda half: pl.BlockSpec((2, half, tm), lambda i, bb: (0, 0, i))
    in_specs = [rows(D_MODEL), _full_spec((1, D_MODEL)), _full_spec((PT_ROWS, D_MODEL)),
                _full_spec((D_MODEL, PR_COLS)), rope_spec(ROT_DIM // 2), rope_spec(MLA_ROPE // 2),
                _full_spec((CMP_BLOCK, KV_WIDTH)), _full_spec((1, MLA_Q_LORA)),
                _full_spec((MLA_HEADS * (MLA_NOPE + MLA_ROPE), MLA_Q_LORA)), _full_spec((1, MLA_KV_LORA)),
                _full_spec((16, tm)), _full_spec((MLA_KV_LORA + LANES, MLA_HEADS * LANES)),
                _full_spec((MLA_WIDTH, MLA_KV_LORA))]
    res = pl.pallas_call(
        _inproj_cols_kernel, grid=(nt, b), in_specs=in_specs, out_specs=[o[2] for o in outs],
        out_shape=[o[1] for o in outs],
        compiler_params=pltpu.CompilerParams(dimension_semantics=("arbitrary", "arbitrary"),
                                             vmem_limit_bytes=VMEM_LIMIT),
        name="in_project_cols",
    )(x3, wts["npre"], wts["w_t"], wts["w_cols"], ropeq, ropem, wts["pe"], wts["qnorm"], wts["wqupt"],
      wts["kvnorm"], jnp.asarray(pool, BF16), wts["wkx"], wts["wvt"])
    return dict(zip([o[0] for o in outs], res))


def _online_update(s, vt, m, l, acc):
    d, keys = vt.shape
    m_new = jnp.maximum(m, jnp.max(s, axis=0, keepdims=True))
    p = jnp.exp2(s - m_new).astype(BF16)
    alpha = jnp.exp2(m - m_new)
    pv = _dot(jnp.concatenate([vt, jnp.ones((PAD_ROWS, keys), BF16)], axis=0), p)
    return m_new, alpha * l + pv[d:d + 1], alpha * acc + pv[0:d]


def _nsa_prompt_kernel(n_blk, with_compress, *refs):
    if with_compress:
        (pt_ref, qt_ref, qrt_ref, gate_ref, kc_ref, ks_ref, kst_ref, kw_ref, kwt_ref, cache_ref, pool_ref, pe_ref,
         o_ref, kcall_ref, sb_ref, kn_ref, buf_ref, sem_ref) = refs
        step = (pl.program_id(0) * pl.num_programs(1) + pl.program_id(1)) * pl.num_programs(2) + pl.program_id(2)
        last = pl.num_programs(0) * pl.num_programs(1) * pl.num_programs(2) - 1
        _compress_step(pt_ref, cache_ref, pool_ref, pe_ref, kcall_ref, buf_ref, sem_ref, step, last)
    else:
        qt_ref, qrt_ref, gate_ref, kc_ref, ks_ref, kst_ref, kw_ref, kwt_ref, o_ref, sb_ref, kn_ref = refs
    i = pl.program_id(2)
    tq, tk = ATT_TQ, ATT_TK
    nl = NSA_GROUP * tq
    zq = jnp.zeros((HD, tq), BF16)

    def widen(qt):
        return jnp.concatenate([jnp.concatenate([qt[r * HD:(r + 1) * HD], zq], axis=0)
                                for r in range(NSA_GROUP)], axis=1)

    qc = widen(qt_ref[0])
    qr = widen(qrt_ref[0])

    n_cmp = 2 * n_blk
    kc = kc_ref[0]
    s_c = _dot(kc.astype(BF16), qc)
    rho = lax.broadcasted_iota(jnp.int32, (n_cmp, nl), 0)
    cmp_idx = jnp.where(rho < n_blk, 2 * rho, 2 * (rho - n_blk) + 1)
    t_l = i * tq + (lax.broadcasted_iota(jnp.int32, (n_cmp, nl), 1) & (tq - 1))
    mask_c = cmp_idx * CMP_BLOCK + (CMP_BLOCK - 1) <= t_l
    s_c = jnp.where(mask_c, s_c, NEG)
    e_c = jnp.where(mask_c, jnp.exp2(s_c - jnp.max(s_c, axis=0, keepdims=True)), 0.0)
    p_c = e_c / jnp.maximum(jnp.sum(e_c, axis=0, keepdims=True), 1e-30)
    kct = jnp.concatenate([kc, jnp.zeros((LANES - n_cmp, LANES), F32)], axis=0).T
    p_pad = jnp.concatenate([p_c, jnp.zeros((LANES - n_cmp, nl), F32)], axis=0)
    o_c = _dot(kct[HD:2 * HD].astype(BF16), p_pad.astype(BF16))

    imp = p_c[:, 0:tq]
    for r in range(1, NSA_GROUP):
        imp = imp + p_c[:, r * tq:(r + 1) * tq]
    imp_blk = imp[0:n_blk] + imp[n_blk:n_cmp]
    blk = lax.broadcasted_iota(jnp.int32, (n_blk, tq), 0)
    t_q = i * tq + lax.broadcasted_iota(jnp.int32, (n_blk, tq), 1)
    ahead_of = t_q - blk * SEL_BLOCK
    score = jnp.where(blk == 0, FORCE_SCORE,
                      jnp.where(ahead_of < 0, -FORCE_SCORE, jnp.where(ahead_of < SEL_BLOCK, FORCE_SCORE, imp_blk)))
    def count_ahead():
        rank = jnp.zeros((n_blk, tq), F32)
        for j in range(n_blk):
            other = score[j:j + 1, :]
            tie = jnp.where(blk > j, 1.0, 0.0)
            rank = rank + jnp.where(other > score, 1.0, jnp.where(other == score, tie, 0.0))
        return rank

    few_blocks = (i + 1) * tq <= N_SELECT * SEL_BLOCK
    rank = lax.cond(few_blocks, lambda: jnp.zeros((n_blk, tq), F32), count_ahead)
    sb_ref[...] = jnp.where(rank < N_SELECT, jnp.where(score > -1.0, 0.0, NEG), NEG)

    key_r = lax.broadcasted_iota(jnp.int32, (tk, tq), 0)
    t_k = i * tq + lax.broadcasted_iota(jnp.int32, (tk, tq), 1)
    n_chunks = ((i + 1) * tq + tk - 1) // tk
    c_lo = jnp.maximum(i * tq - WINDOW, 0) // tk
    per_chunk = tk // SEL_BLOCK
    tile = lambda bias: jnp.concatenate([bias] * NSA_GROUP, axis=1)
    ones = jnp.ones((PAD_ROWS, tk), BF16)

    def sel_bias(c):
        rows = [jnp.broadcast_to(sb_ref[pl.ds(c * per_chunk + j, 1), :], (SEL_BLOCK, tq)) for j in range(per_chunk)]
        return jnp.concatenate(rows, axis=0)

    def values(ref, c):
        return jnp.concatenate([ref[0, c, HD:2 * HD, :], ones], axis=0)

    qf = qrt_ref[0].astype(F32)
    q_norm2 = [jnp.sum(qf[r * HD:(r + 1) * HD] ** 2, axis=0, keepdims=True) for r in range(NSA_GROUP)]
    q_norm = jnp.sqrt(jnp.maximum(jnp.maximum(q_norm2[0], q_norm2[1]), jnp.maximum(q_norm2[2], q_norm2[3])))

    def key_norm(ref):
        kf = ref[0, :, 0:HD, :].astype(F32)
        n2 = jnp.max(jnp.sum(kf * kf, axis=1, keepdims=True), axis=0)
        return jnp.broadcast_to(jnp.sqrt(jnp.max(n2, axis=1, keepdims=True)), (1, LANES))

    @pl.when(i == 0)
    def _():
        kn_ref[0:1, :] = key_norm(kst_ref)
        kn_ref[1:2, :] = key_norm(kwt_ref)

    wide = lambda row: jnp.concatenate([row] * (tq // LANES), axis=1)
    bound_s = q_norm * (wide(kn_ref[0:1, :]) * 1.001) + 1e-3
    bound_w = q_norm * (wide(kn_ref[1:2, :]) * 1.001) + 1e-3
    safe = jnp.maximum(jnp.max(bound_s), jnp.max(bound_w)) <= SHIFT_LIMIT

    def attend_shifted():
        def past_probs(c):
            base = pl.multiple_of(c * tk, tk)
            s = _dot(ks_ref[0, pl.ds(base, tk), :], qr) + tile(sel_bias(c) - bound_s)
            return jnp.exp2(s).astype(BF16)

        def past_body(c, acc):
            return acc + _dot(values(kst_ref, c), past_probs(c))

        def past_pair(cc, acc):
            p0, p1 = past_probs(2 * cc), past_probs(2 * cc + 1)
            return acc + _dot(values(kst_ref, 2 * cc), p0) + _dot(values(kst_ref, 2 * cc + 1), p1)

        def near_probs(c):
            base = pl.multiple_of(c * tk, tk)
            s = _dot(ks_ref[0, pl.ds(base, tk), :], qr)
            w = _dot(kw_ref[0, pl.ds(base, tk), :], qr)
            dist = t_k - (base + key_r)
            causal = jnp.where(dist >= 0, 0.0, NEG)
            p_s = jnp.exp2(s + tile(sel_bias(c) + causal - bound_s)).astype(BF16)
            p_w = jnp.exp2(w + tile(jnp.where(dist <= WINDOW, causal, NEG) - bound_w)).astype(BF16)
            return p_s, p_w

        def near_body(c, carry):
            p_s, p_w = near_probs(c)
            return carry[0] + _dot(values(kst_ref, c), p_s), carry[1] + _dot(values(kwt_ref, c), p_w)

        def near_pair(cc, carry):
            c0 = c_lo + 2 * cc
            (ps0, pw0), (ps1, pw1) = near_probs(c0), near_probs(c0 + 1)
            return (carry[0] + _dot(values(kst_ref, c0), ps0) + _dot(values(kst_ref, c0 + 1), ps1),
                    carry[1] + _dot(values(kwt_ref, c0), pw0) + _dot(values(kwt_ref, c0 + 1), pw1))

        zero = jnp.zeros((HD + PAD_ROWS, nl), F32)
        far = lax.fori_loop(0, c_lo // 2, past_pair, zero)
        far = lax.fori_loop(2 * (c_lo // 2), c_lo, past_body, far)
        n_pairs = (n_chunks - c_lo) // 2
        acc = lax.fori_loop(0, n_pairs, near_pair, (far, zero))
        a_s, a_w = lax.fori_loop(c_lo + 2 * n_pairs, n_chunks, near_body, acc)
        inv_s, inv_w = 1.0 / a_s[HD:HD + 1], 1.0 / a_w[HD:HD + 1]
        return a_s[0:HD] * inv_s, a_w[0:HD] * inv_w

    def attend_online():
        init = (jnp.full((1, nl), NEG, F32), jnp.zeros((1, nl), F32), jnp.zeros((HD, nl), F32))

        def past_body(c, carry):
            base = pl.multiple_of(c * tk, tk)
            s = _dot(ks_ref[0, pl.ds(base, tk), :], qr) + tile(sel_bias(c))
            return _online_update(s, kst_ref[0, c, HD:2 * HD, :], *carry)

        def near_body(c, carry):
            base = pl.multiple_of(c * tk, tk)
            s = _dot(ks_ref[0, pl.ds(base, tk), :], qr)
            w = _dot(kw_ref[0, pl.ds(base, tk), :], qr)
            dist = t_k - (base + key_r)
            causal = jnp.where(dist >= 0, 0.0, NEG)
            s = s + tile(sel_bias(c) + causal)
            w = w + tile(jnp.where(dist <= WINDOW, causal, NEG))
            return (_online_update(s, kst_ref[0, c, HD:2 * HD, :], *carry[:3])
                    + _online_update(w, kwt_ref[0, c, HD:2 * HD, :], *carry[3:]))

        far = lax.fori_loop(0, c_lo, past_body, init)
        _, l_s, a_s, _, l_w, a_w = lax.fori_loop(c_lo, n_chunks, near_body, far + init)
        return a_s / l_s, a_w / l_w

    o_s, o_w = lax.cond(safe, attend_shifted, attend_online)

    gate = gate_ref[0]
    heads = []
    for r in range(NSA_GROUP):
        sl = slice(r * tq, (r + 1) * tq)
        heads.append(gate[3 * r:3 * r + 1] * o_c[:, sl] + gate[3 * r + 1:3 * r + 2] * o_s[:, sl]
                     + gate[3 * r + 2:3 * r + 3] * o_w[:, sl])
    o_ref[0] = jnp.concatenate(heads, axis=0).T.astype(BF16)


def _nsa_prompt(p, b, t, compress_job=None):
    tq, tk = ATT_TQ, ATT_TK
    n_blk = t // SEL_BLOCK
    assert 2 * n_blk <= LANES and t % tk == 0
    kc = p["kc"].reshape(b, n_blk, 2, KV_WIDTH).transpose(0, 2, 1, 3).reshape(b, 2 * n_blk, KV_WIDTH)
    nq = t // tq
    grid = (b, NSA_KV_HEADS, nq)
    qspec = pl.BlockSpec((1, NSA_GROUP * HD, tq), lambda bb, g, i, *_: (bb, g, i))
    rm = pl.BlockSpec((1, t, LANES), lambda bb, g, i, *_: (bb, 0, g))
    fm = pl.BlockSpec((1, t // tk, LANES, tk), lambda bb, g, i, *_: (bb, 0, g, 0))
    in_specs = [qspec, qspec, pl.BlockSpec((1, GATE_ROWS, tq), lambda bb, g, i, *_: (bb, g, i)),
                pl.BlockSpec((1, 2 * n_blk, LANES), lambda bb, g, i, *_: (bb, 0, g)), rm, fm, rm, fm]
    out_specs = [pl.BlockSpec((1, tq, NSA_GROUP * HD), lambda bb, g, i, *_: (bb, i, g))]
    out_shape = [jax.ShapeDtypeStruct((b, t, NSA_WIDTH), BF16)]
    scratch = [pltpu.VMEM((n_blk, tq), F32), pltpu.VMEM((8, LANES), F32)]
    args = [p["qt"], p["qrt"], p["gate"], kc, p["slcr"], p["slcb"], p["winr"], p["winb"]]
    params = pltpu.CompilerParams(dimension_semantics=("arbitrary",) * 3, vmem_limit_bytes=VMEM_LIMIT)
    if compress_job is not None and compress_job[3] * compress_job[4] // PAGES_PER_STEP != b * NSA_KV_HEADS * nq:
        compress_job = None
    if compress_job is None:
        o_a, = pl.pallas_call(
            functools.partial(_nsa_prompt_kernel, n_blk, False), grid=grid, in_specs=in_specs, out_specs=out_specs,
            out_shape=out_shape, scratch_shapes=scratch, compiler_params=params, name="nsa_prompt")(*args)
        return o_a, None
    cache_t, pt_flat, pe, sb, n_pages = compress_job
    page = cache_t.shape[2]
    per_step = PAGES_PER_STEP * page // CMP_BLOCK
    per_seq = n_pages // PAGES_PER_STEP
    pool = _pool_matrix(page)
    flat = lambda bb, g, i: (bb * NSA_KV_HEADS + g) * nq + i
    gs = pltpu.PrefetchScalarGridSpec(
        num_scalar_prefetch=1, grid=grid,
        in_specs=in_specs + [pl.BlockSpec(memory_space=pl.ANY), pl.BlockSpec(pool.shape, lambda *_: (0, 0)),
                             pl.BlockSpec((CMP_BLOCK, KV_WIDTH), lambda *_: (0, 0))],
        out_specs=out_specs + [pl.BlockSpec((1, per_step, KV_WIDTH),
                                            lambda bb, g, i, *_: (flat(bb, g, i) // per_seq, flat(bb, g, i) % per_seq, 0))],
        scratch_shapes=scratch + [pltpu.VMEM((2, PAGES_PER_STEP, KV_WIDTH, page), F32), pltpu.SemaphoreType.DMA((2,))])
    return pl.pallas_call(
        functools.partial(_nsa_prompt_kernel, n_blk, True), grid_spec=gs,
        out_shape=out_shape + [jax.ShapeDtypeStruct((sb, n_pages * page // CMP_BLOCK, KV_WIDTH), F32)],
        compiler_params=params, name="nsa_prompt",
    )(pt_flat, *args, cache_t, jnp.asarray(pool, BF16), pe)


MLA_TQ = 256


def _mla_prompt_kernel(rider_steps, *refs):
    if rider_steps:
        (pt_ref, qmt_ref, k_ref, vt_ref, kn_ref, lat_ref, krt_ref, qlat_ref, qpe_ref, cnew_ref, krnew_ref,
         o_ref, olat_ref, m_ref, l_ref, acc_ref, sm_ref, sl_ref, sacc_ref, lat_buf, kr_buf, lat_sem, kr_sem) = refs
        ring = pl.program_id(0) * pl.num_programs(1) + pl.program_id(1)
        _mla_sample_step(pt_ref, lat_ref, krt_ref, qlat_ref, qpe_ref, cnew_ref, krnew_ref, olat_ref, sm_ref, sl_ref,
                         sacc_ref, lat_buf, kr_buf, lat_sem, kr_sem, ring,
                         pl.num_programs(0) * pl.num_programs(1) - 1, ring % rider_steps, rider_steps)
    else:
        qmt_ref, k_ref, vt_ref, kn_ref, o_ref, m_ref, l_ref, acc_ref = refs
    i = pl.program_id(1)
    tq, tk = MLA_TQ, MLA_TK
    heads = range(MLA_HEADS)
    m_ref[...] = jnp.full(m_ref.shape, NEG, F32)
    l_ref[...] = jnp.zeros(l_ref.shape, F32)
    acc_ref[...] = jnp.zeros(acc_ref.shape, F32)
    key_r = lax.broadcasted_iota(jnp.int32, (tk, tq), 0)
    t_k = i * tq + lax.broadcasted_iota(jnp.int32, (tk, tq), 1)
    n_chunks = ((i + 1) * tq + tk - 1) // tk
    n_past = (i * tq + 1) // tk
    ones = jnp.ones((PAD_ROWS, tk), BF16)

    def keys(c_base, h):
        return k_ref[0, pl.ds(c_base, tk), h * LANES:(h + 1) * LANES]

    kn = kn_ref[0]
    k_max = kn[0:MLA_HEADS]
    for j in range(1, kn.shape[0] // MLA_HEADS):
        k_max = jnp.maximum(k_max, kn[j * MLA_HEADS:(j + 1) * MLA_HEADS])
    bounds = []
    for h in heads:
        qf = qmt_ref[0, h * LANES:(h + 1) * LANES, :].astype(F32)
        q_norm = jnp.sqrt(jnp.sum(qf * qf, axis=0, keepdims=True))
        k_row = jnp.concatenate([k_max[h:h + 1]] * (tq // LANES), axis=1)
        bounds.append(q_norm * (k_row * 1.001) + 1e-3)
    worst = bounds[0]
    for bd in bounds[1:]:
        worst = jnp.maximum(worst, bd)
    safe = jnp.max(worst) <= SHIFT_LIMIT

    def shifted_step(c, masked):
        base = pl.multiple_of(c * tk, tk)
        old = [acc_ref[h] for h in heads]
        ps = []
        for h in heads:
            s = _dot(keys(base, h), qmt_ref[0, h * LANES:(h + 1) * LANES, :]) - bounds[h]
            ps.append(jnp.exp2(jnp.where(base + key_r <= t_k, s, NEG) if masked else s).astype(BF16))
        new = [old[h] + _dot(jnp.concatenate([vt_ref[0, c, h * MLA_V:(h + 1) * MLA_V, :], ones], axis=0), ps[h])
               for h in heads]
        for h in heads:
            acc_ref[h] = new[h]
        return 0

    def online_step(c, masked):
        base = pl.multiple_of(c * tk, tk)
        old = [(m_ref[h], l_ref[h], acc_ref[h, 0:MLA_V]) for h in heads]
        scores = []
        for h in heads:
            s = _dot(keys(base, h), qmt_ref[0, h * LANES:(h + 1) * LANES, :])
            scores.append(jnp.where(base + key_r <= t_k, s, NEG) if masked else s)
        new = [_online_update(scores[h], vt_ref[0, c, h * MLA_V:(h + 1) * MLA_V, :], *old[h]) for h in heads]
        for h in heads:
            m_ref[h], l_ref[h], acc_ref[h, 0:MLA_V] = new[h]
        return 0

    def run(step):
        lax.fori_loop(0, n_past, lambda c, z: step(c, False), 0)
        lax.fori_loop(n_past, n_chunks, lambda c, z: step(c, True), 0)

    def shifted():
        run(shifted_step)
        return tuple(acc_ref[h, 0:MLA_V] * (1.0 / acc_ref[h, MLA_V:MLA_V + 1]) for h in heads)

    def online():
        run(online_step)
        return tuple(acc_ref[h, 0:MLA_V] / l_ref[h] for h in heads)

    outs = lax.cond(safe, shifted, online)
    o_ref[0] = jnp.concatenate(outs, axis=0).T.astype(BF16)


def _mla_prompt(p, wts, b, t, sample_job=None):
    del wts
    tq, tk = MLA_TQ, MLA_TK
    assert t % tq == 0
    nq = t // tq
    in_specs = [pl.BlockSpec((1, MLA_HEADS * LANES, tq), lambda bb, i, *_: (bb, 0, i)),
                pl.BlockSpec((1, t, MLA_HEADS * LANES), lambda bb, i, *_: (bb, 0, 0)),
                pl.BlockSpec((1, t // tk, MLA_WIDTH, tk), lambda bb, i, *_: (bb, 0, 0, 0)),
                pl.BlockSpec((1,) + p["kn"].shape[1:], lambda bb, i, *_: (bb, 0, 0))]
    out_specs = [pl.BlockSpec((1, tq, MLA_WIDTH), lambda bb, i, *_: (bb, i, 0))]
    out_shape = [jax.ShapeDtypeStruct((b, t, MLA_WIDTH), BF16)]
    scratch = [pltpu.VMEM((MLA_HEADS, 1, tq), F32), pltpu.VMEM((MLA_HEADS, 1, tq), F32),
               pltpu.VMEM((MLA_HEADS, MLA_V + PAD_ROWS, tq), F32)]
    args = [p["qmt"], p["kmla"], p["vmt"], p["kn"]]
    params = pltpu.CompilerParams(dimension_semantics=("arbitrary",) * 2, vmem_limit_bytes=VMEM_LIMIT)
    per_seq = 0
    if sample_job is not None:
        sb, n_pages = sample_job[4], sample_job[5]
        if (b * nq) % sb == 0 and n_pages % ((b * nq) // sb) == 0:
            per_seq = (b * nq) // sb
    if not per_seq:
        o_b, = pl.pallas_call(functools.partial(_mla_prompt_kernel, 0), grid=(b, nq), in_specs=in_specs,
                              out_specs=out_specs, out_shape=out_shape, scratch_shapes=scratch,
                              compiler_params=params, name="mla_prompt")(*args)
        return o_b, None
    lat_cache, kr_t, pt_flat, sp, sb, n_pages = sample_job
    page = lat_cache.shape[1]
    pages = n_pages // per_seq
    seq = lambda bb, i: (bb * nq + i) // per_seq
    head = lambda w: pl.BlockSpec((1, MLA_HEADS, w), lambda bb, i, *_: (seq(bb, i), 0, 0))
    row = lambda w: pl.BlockSpec((1, 1, w), lambda bb, i, *_: (seq(bb, i), 0, 0))
    gs = pltpu.PrefetchScalarGridSpec(
        num_scalar_prefetch=1, grid=(b, nq),
        in_specs=in_specs + [pl.BlockSpec(memory_space=pl.ANY), pl.BlockSpec(memory_space=pl.ANY),
                             head(MLA_KV_LORA), head(MLA_ROPE), row(MLA_KV_LORA), row(MLA_ROPE)],
        out_specs=out_specs + [head(MLA_KV_LORA)],
        scratch_shapes=scratch + [pltpu.VMEM((PAD_ROWS, 1), F32), pltpu.VMEM((PAD_ROWS, 1), F32),
                                  pltpu.VMEM((PAD_ROWS, MLA_KV_LORA), F32),
                                  pltpu.VMEM((2, pages, page, MLA_KV_LORA), F32),
                                  pltpu.VMEM((2, pages, MLA_ROPE, page), F32),
                                  pltpu.SemaphoreType.DMA((2,)), pltpu.SemaphoreType.DMA((2,))])
    qlat = sp["qlat"].reshape(sb, MLA_HEADS, MLA_KV_LORA)
    qpe = sp["qpe"].reshape(sb, MLA_HEADS, MLA_ROPE)
    return pl.pallas_call(
        functools.partial(_mla_prompt_kernel, per_seq), grid_spec=gs,
        out_shape=out_shape + [jax.ShapeDtypeStruct((sb, MLA_HEADS, MLA_KV_LORA), F32)],
        compiler_params=params, name="mla_prompt",
    )(pt_flat, *args, lat_cache, kr_t, qlat, qpe, sp["c"], sp["kr"])


def _merge_kernel(from_latent, x_ref, oa_ref, ob_ref, ga_ref, gb_ref, ma_ref, mb_ref, wpa_ref, wpb_ref, wout_ref,
                  npost_ref, wuv_ref, y_ref):
    if from_latent:
        lat = ob_ref[0].astype(BF16)
        parts = []
        for j in range(MLA_HEADS // 2):
            parts.append(_dot(lat[:, 2 * j * MLA_KV_LORA:(2 * j + 1) * MLA_KV_LORA], wuv_ref[2 * j])
                         + _dot(lat[:, (2 * j + 1) * MLA_KV_LORA:(2 * j + 2) * MLA_KV_LORA], wuv_ref[2 * j + 1]))
        o_b = jnp.concatenate(parts, axis=1)
    else:
        o_b = ob_ref[0].astype(F32)
    ga = ga_ref[0].astype(F32)
    gb = gb_ref[0].astype(F32)
    pa = _dot((oa_ref[0].astype(F32) * (ga * jax.nn.sigmoid(ga))).astype(BF16), wpa_ref[...])
    pb = _dot((o_b * (gb * jax.nn.sigmoid(gb))).astype(BF16), wpb_ref[...])
    h = jax.nn.sigmoid(ma_ref[0].astype(F32)) * pa + jax.nn.sigmoid(mb_ref[0].astype(F32)) * pb
    z = _dot(h.astype(BF16), wout_ref[...])
    y_ref[0] = x_ref[0] + _rms(z, npost_ref[...])


def _merge(x3, o_a, o_b, p, wts, tm, from_latent):
    b, t, _ = x3.shape
    tok = lambda w: pl.BlockSpec((1, tm, w), lambda bb, i: (bb, i, 0))
    return pl.pallas_call(
        functools.partial(_merge_kernel, from_latent),
        grid=(b, t // tm),
        in_specs=[tok(D_MODEL), tok(NSA_WIDTH), tok(o_b.shape[2]), tok(NSA_WIDTH), tok(MLA_WIDTH), tok(D_MODEL),
                  tok(D_MODEL), _full_spec((NSA_WIDTH, D_MODEL)), _full_spec((MLA_WIDTH, D_MODEL)),
                  _full_spec((D_MODEL, D_MODEL)), _full_spec((1, D_MODEL)),
                  _full_spec((MLA_HEADS, MLA_KV_LORA, LANES))],
        out_specs=tok(D_MODEL),
        out_shape=jax.ShapeDtypeStruct((b, t, D_MODEL), F32),
        compiler_params=pltpu.CompilerParams(dimension_semantics=("arbitrary",) * 2, vmem_limit_bytes=VMEM_LIMIT),
        name="merge",
    )(x3, o_a, o_b, p["ga"], p["gb"], p["ma"], p["mb"], wts["wpa"], wts["wpb"], wts["wout"], wts["npost"],
      wts["wuv"])


PAGES_PER_STEP = 32
POOL_PAGES = 4


def _page_copy(source, buf_ref, sem_ref, step, slot, k):
    return pltpu.make_async_copy(source(step, k), buf_ref.at[slot, k], sem_ref.at[slot])


def _table_pages(pt_ref, cache_ref, per_step=None):
    per_step = per_step or PAGES_PER_STEP
    return lambda step, k: cache_ref.at[pt_ref[step * per_step + k]]


def _paged_pipeline(streams, step=None, last=None):
    if step is None:
        step = pl.program_id(0) * pl.num_programs(1) + pl.program_id(1)
        last = pl.num_programs(0) * pl.num_programs(1) - 1
    slot = step % 2

    @pl.when(step == 0)
    def _():
        for source, buf_ref, sem_ref in streams:
            for k in range(buf_ref.shape[1]):
                _page_copy(source, buf_ref, sem_ref, 0, 0, k).start()

    @pl.when(step < last)
    def _():
        for source, buf_ref, sem_ref in streams:
            for k in range(buf_ref.shape[1]):
                _page_copy(source, buf_ref, sem_ref, step + 1, 1 - slot, k).start()

    for source, buf_ref, sem_ref in streams:
        for k in range(buf_ref.shape[1]):
            _page_copy(source, buf_ref, sem_ref, step, slot, k).wait()
    return slot


XLU_POOL_PAGES = 16
MXU_POOL_GROUP = 8


def _compress_pages_kernel(pt_ref, cache_ref, pool_ref, pe_ref, o_ref, buf_ref, sem_ref):
    _compress_step(pt_ref, cache_ref, pool_ref, pe_ref, o_ref, buf_ref, sem_ref)


def _compress_step(pt_ref, cache_ref, pool_ref, pe_ref, o_ref, buf_ref, sem_ref, step=None, last=None):
    slot = _paged_pipeline([(_table_pages(pt_ref, cache_ref), buf_ref, sem_ref)], step, last)
    pe_sum = jnp.sum(pe_ref[...], axis=0, keepdims=True)
    pool = pool_ref[...]
    groups = [jnp.concatenate([buf_ref[slot, k + j] for j in range(MXU_POOL_GROUP)], axis=1)
              for k in range(XLU_POOL_PAGES, PAGES_PER_STEP, MXU_POOL_GROUP)]
    splits = [_split_bf16(x) for x in groups]
    by_mxu = [_dot_nt(pool, hi) + _dot_nt(pool, lo) for hi, lo in splits]
    by_xlu = []
    for k in range(XLU_POOL_PAGES):
        rows = buf_ref[slot, k].T
        by_xlu.append(jnp.sum(rows.reshape(rows.shape[0] // CMP_BLOCK, CMP_BLOCK, KV_WIDTH), axis=1))
    o_ref[0] = (jnp.concatenate(by_xlu + by_mxu, axis=0) + pe_sum) * (1.0 / CMP_BLOCK)


def _pool_matrix(page):
    pool = np.zeros((MXU_POOL_GROUP * page // CMP_BLOCK, MXU_POOL_GROUP * page), np.float32)
    for s in range(pool.shape[1]):
        pool[s // CMP_BLOCK, s] = 1.0
    return pool


def _compress_pages(cache_t, pt_flat, pe, b, n_pages):
    page = cache_t.shape[2]
    per_step = PAGES_PER_STEP * page // CMP_BLOCK
    pool = _pool_matrix(page)
    gs = pltpu.PrefetchScalarGridSpec(
        num_scalar_prefetch=1, grid=(b, n_pages // PAGES_PER_STEP),
        in_specs=[pl.BlockSpec(memory_space=pl.ANY),
                  pl.BlockSpec(pool.shape, lambda bb, c, pt: (0, 0)),
                  pl.BlockSpec((CMP_BLOCK, KV_WIDTH), lambda bb, c, pt: (0, 0))],
        out_specs=pl.BlockSpec((1, per_step, KV_WIDTH), lambda bb, c, pt: (bb, c, 0)),
        scratch_shapes=[pltpu.VMEM((2, PAGES_PER_STEP, KV_WIDTH, page), F32), pltpu.SemaphoreType.DMA((2,))])
    return pl.pallas_call(
        _compress_pages_kernel, grid_spec=gs,
        out_shape=jax.ShapeDtypeStruct((b, n_pages * page // CMP_BLOCK, KV_WIDTH), F32),
        compiler_params=pltpu.CompilerParams(dimension_semantics=("arbitrary",) * 2, vmem_limit_bytes=VMEM_LIMIT),
        name="compress_pages",
    )(pt_flat, cache_t, jnp.asarray(pool, BF16), pe)


def _pad_rows(v, rows):
    return jnp.concatenate([v, jnp.zeros((rows - v.shape[0], v.shape[1]), v.dtype)], axis=0)


def _stack_heads(qv, lo):
    a, b = qv[:, :LANES], qv[:, LANES:]
    z = jnp.zeros_like(a)
    return jnp.concatenate([jnp.where(lo, a, z), jnp.where(lo, z, a),
                            jnp.where(lo, b, z), jnp.where(lo, z, b)], axis=0)


def _sample_select_kernel(n_cmp, q_ref, kc_ref, oc_ref, idx_ref, imp_ref):
    bb = pl.program_id(0)
    lo1 = _lane_lo(1)
    lo_c = _lane_lo(n_cmp)
    q = q_ref[0]
    kc = kc_ref[0]
    even = (lax.broadcasted_iota(jnp.int32, (1, LANES), 1) & 1) == 0
    for g in range(NSA_KV_HEADS):
        kk, vv = _dup_kv(kc[:, g * LANES:(g + 1) * LANES], lo_c)
        qs = _pad_rows(_stack_heads(q[:, g * 2 * LANES:(g + 1) * 2 * LANES], lo1), PAD_ROWS)
        s = _dot_nt(qs, kk.astype(BF16))
        e = jnp.exp(s - jnp.max(s, axis=-1, keepdims=True))
        p = e / jnp.sum(e, axis=-1, keepdims=True)
        oc_ref[0, g] = _dot(p.astype(BF16), vv.astype(BF16))
        imp = p[0:1] + p[1:2] + p[2:3] + p[3:4]
        chunks = []
        for k in range(n_cmp // LANES):
            a = imp[:, k * LANES:(k + 1) * LANES]
            chunks.append(a + jnp.where(even, pltpu.roll(a, LANES - 1, 1), pltpu.roll(a, 1, 1)))
        imp_ref[pl.ds(bb * NSA_KV_HEADS + g, 1), :] = jnp.concatenate(chunks, axis=1)

    @pl.when(bb == pl.num_programs(0) - 1)
    def _():
        rows = imp_ref.shape[0]
        blk = lax.broadcasted_iota(jnp.int32, (rows, n_cmp), 1) >> 1
        blk_f = blk.astype(F32)
        slot = lax.broadcasted_iota(jnp.int32, (rows, N_SELECT), 1)
        v = jnp.where(blk == 0, -1.0, imp_ref[...])
        idx = jnp.where(slot == N_SELECT - 1, n_cmp // 2, 0)
        for k in range(1, N_SELECT - 1):
            top = jnp.max(v, axis=-1, keepdims=True)
            jmin = jnp.min(jnp.where(v == top, blk_f, float(n_cmp)), axis=-1, keepdims=True).astype(jnp.int32)
            idx = jnp.where(slot == k, jmin, idx)
            v = jnp.where(blk == jmin, -1.0, v)
        idx_ref[...] = idx


def _sample_select(q, kc_all, b):
    n_cmp = kc_all.shape[1]
    rows = b * NSA_KV_HEADS
    return pl.pallas_call(
        functools.partial(_sample_select_kernel, n_cmp),
        grid=(b,),
        in_specs=[pl.BlockSpec((1, 1, NSA_WIDTH), lambda bb: (bb, 0, 0)),
                  pl.BlockSpec((1, n_cmp, KV_WIDTH), lambda bb: (bb, 0, 0))],
        out_specs=[pl.BlockSpec((1, NSA_KV_HEADS, PAD_ROWS, LANES), lambda bb: (bb, 0, 0, 0)),
                   pl.BlockSpec((rows, N_SELECT), lambda bb: (0, 0))],
        out_shape=[jax.ShapeDtypeStruct((b, NSA_KV_HEADS, PAD_ROWS, LANES), F32),
                   jax.ShapeDtypeStruct((rows, N_SELECT), jnp.int32)],
        scratch_shapes=[pltpu.VMEM((rows, n_cmp), F32)],
        compiler_params=pltpu.CompilerParams(dimension_semantics=("arbitrary",), vmem_limit_bytes=VMEM_LIMIT),
        name="sample_select",
    )(q, kc_all)


def _extra_key_softmax(s_past, vt4_b, s_new, v_new):
    m = jnp.maximum(jnp.max(s_past, axis=-1, keepdims=True), s_new)
    e = jnp.exp(s_past - m)
    e_new = jnp.exp(s_new - m)
    den = jnp.sum(e, axis=-1, keepdims=True) + e_new
    return (_dot_nt(e.astype(BF16), vt4_b) + e_new * v_new) / den


def _sample_attend_kernel(n_pages, idx_ref, pt_ref, slc_ref, qr_ref, newkv_ref, neww_ref, newwf_ref, win_ref,
                          gate_ref, oc_ref, o_ref, wout_ref, buf_ref, sem_ref):
    n_sel = N_SELECT
    bb, g = pl.program_id(0), pl.program_id(1)

    def selected_page(step, k):
        j = idx_ref[step * n_sel + k]
        page = pt_ref[(step // NSA_KV_HEADS) * n_pages + jnp.minimum(j // 2, n_pages - 1)]
        return slc_ref.at[page, pl.ds((step % NSA_KV_HEADS) * LANES, LANES), :]

    slot = _paged_pipeline([(selected_page, buf_ref, sem_ref)])
    pages = [buf_ref.at[slot, k] for k in range(n_sel)]
    lo1 = _lane_lo(1)
    q = qr_ref[0].astype(F32)
    halves = [q[:, 0:LANES], q[:, LANES:2 * LANES]]
    rows = []
    for r in range(NSA_GROUP):
        a = halves[r // 2]
        rows.append(jnp.where(lo1, a if r % 2 == 0 else pltpu.roll(a, HALF, 1), 0.0))
    qs_f = _pad_rows(jnp.concatenate(rows, axis=0), PAD_ROWS)
    qs = qs_f.astype(BF16)
    twice_rows = lambda a: jnp.concatenate([a, a], axis=0)

    s_t = jnp.concatenate([pg[...] for pg in pages], axis=1)
    s_sel = _dot(qs, s_t.astype(BF16))
    base = (bb * NSA_KV_HEADS + g) * n_sel
    biases = []
    for k in range(n_sel - 1):
        odd = (idx_ref[base + k] & 1) == 1
        biases.append(jnp.where(lo1, jnp.where(odd, NEG, 0.0), jnp.where(odd, 0.0, NEG)))
    biases.append(jnp.full((1, LANES), NEG, F32))
    s_sel = s_sel + jnp.concatenate(biases, axis=1)
    nk = newkv_ref[0].astype(F32)
    s_new = jnp.sum(qs_f * nk[:, :LANES], axis=-1, keepdims=True)
    o_s = _extra_key_softmax(s_sel, twice_rows(s_t[HD:2 * HD]).astype(BF16), s_new, nk[:, LANES:])

    w = win_ref[0]
    nw = neww_ref[0].astype(F32)
    s_w = _dot(qs, w.astype(BF16))
    s_wn = jnp.sum(qs_f * nw[:, :LANES], axis=-1, keepdims=True)
    o_w = _extra_key_softmax(s_w, twice_rows(w[HD:2 * HD]).astype(BF16), s_wn, nw[:, LANES:])

    o_c = oc_ref[0, 0]
    gates = gate_ref[0]
    gate = jnp.where(g == 0, gates[:, 0:GATE_ROWS], gates[:, GATE_ROWS:2 * GATE_ROWS])
    heads = []
    for r in range(NSA_GROUP):
        heads.append(gate[:, 3 * r:3 * r + 1] * o_c[r:r + 1] + gate[:, 3 * r + 1:3 * r + 2] * o_s[r:r + 1]
                     + gate[:, 3 * r + 2:3 * r + 3] * o_w[r:r + 1])
    o_ref[0] = jnp.concatenate([jnp.where(lo1, heads[0], heads[1]), jnp.where(lo1, heads[2], heads[3])], axis=1)

    n_feat, n_w = w.shape
    new_row = jnp.broadcast_to(newwf_ref[0], (n_feat, n_feat))
    diag = (lax.broadcasted_iota(jnp.int32, (n_feat, n_feat), 0)
            == lax.broadcasted_iota(jnp.int32, (n_feat, n_feat), 1))
    new_col = jnp.sum(jnp.where(diag, new_row, 0.0), axis=1, keepdims=True)
    last = lax.broadcasted_iota(jnp.int32, (n_feat, LANES), 1) == LANES - 1
    chunks = []
    n_ch = n_w // LANES
    for c in range(n_ch):
        cur = pltpu.roll(w[:, c * LANES:(c + 1) * LANES], LANES - 1, 1)
        if c + 1 < n_ch:
            nxt = pltpu.roll(w[:, (c + 1) * LANES:(c + 2) * LANES], LANES - 1, 1)
        else:
            nxt = jnp.broadcast_to(new_col, (n_feat, LANES))
        chunks.append(jnp.where(last, nxt, cur))
    wout_ref[0] = jnp.concatenate(chunks, axis=1)


def _sample_attend(p, oc, idx_flat, pt_flat, slc_t, win_t, b, n_pages):
    page = slc_t.shape[2]
    win_len = win_t.shape[2]
    assert page == 2 * SEL_BLOCK
    row = lambda w: pl.BlockSpec((1, 1, w), lambda bb, g, idx, pt: (bb, 0, g))
    wspec = pl.BlockSpec((1, LANES, win_len), lambda bb, g, idx, pt: (bb, g, 0))
    gs = pltpu.PrefetchScalarGridSpec(
        num_scalar_prefetch=2, grid=(b, NSA_KV_HEADS),
        in_specs=[pl.BlockSpec(memory_space=pl.ANY),
                  row(2 * LANES), row(2 * LANES), row(2 * LANES), row(LANES), wspec,
                  pl.BlockSpec((1, 1, NSA_KV_HEADS * GATE_ROWS), lambda bb, g, idx, pt: (bb, 0, 0)),
                  pl.BlockSpec((1, 1, PAD_ROWS, LANES), lambda bb, g, idx, pt: (bb, g, 0, 0))],
        out_specs=[row(2 * LANES), wspec],
        scratch_shapes=[pltpu.VMEM((2, N_SELECT, LANES, page), F32), pltpu.SemaphoreType.DMA((2,))])
    return pl.pallas_call(
        functools.partial(_sample_attend_kernel, n_pages), grid_spec=gs,
        out_shape=[jax.ShapeDtypeStruct((b, 1, NSA_WIDTH), F32),
                   jax.ShapeDtypeStruct((b, KV_WIDTH, win_len), F32)],
        compiler_params=pltpu.CompilerParams(dimension_semantics=("arbitrary",) * 2, vmem_limit_bytes=VMEM_LIMIT),
        name="sample_attend",
    )(idx_flat, pt_flat, slc_t, p["qrot"], p["slckv"], p["winkv"], p["win"], win_t, p["gate"], oc)


def _softmax_update(s, v_b, m, l, acc):
    m_new = jnp.maximum(m, jnp.max(s, axis=-1, keepdims=True))
    p = jnp.exp(s - m_new)
    alpha = jnp.exp(m - m_new)
    return m_new, alpha * l + jnp.sum(p, axis=-1, keepdims=True), alpha * acc + _dot(p.astype(BF16), v_b)


def _mla_sample_kernel(pt_ref, lat_ref, krt_ref, qlat_ref, qpe_ref, cnew_ref, krnew_ref, o_ref,
                       m_ref, l_ref, acc_ref, lat_buf, kr_buf, lat_sem, kr_sem):
    _mla_sample_step(pt_ref, lat_ref, krt_ref, qlat_ref, qpe_ref, cnew_ref, krnew_ref, o_ref,
                     m_ref, l_ref, acc_ref, lat_buf, kr_buf, lat_sem, kr_sem)


def _mla_sample_step(pt_ref, lat_ref, krt_ref, qlat_ref, qpe_ref, cnew_ref, krnew_ref, o_ref, m_ref, l_ref, acc_ref,
                     lat_buf, kr_buf, lat_sem, kr_sem, ring_step=None, ring_last=None, step=None, n_steps=None):
    n = lat_buf.shape[1]
    slot = _paged_pipeline([(_table_pages(pt_ref, lat_ref, n), lat_buf, lat_sem),
                            (_table_pages(pt_ref, krt_ref, n), kr_buf, kr_sem)], ring_step, ring_last)
    lat_pages = [lat_buf.at[slot, k] for k in range(n)]
    kr_pages = [kr_buf.at[slot, k] for k in range(n)]
    if step is None:
        step, n_steps = pl.program_id(1), pl.num_programs(1)
    qlat = _pad_rows(qlat_ref[0], PAD_ROWS)
    qpe = _pad_rows(qpe_ref[0], PAD_ROWS)

    @pl.when(step == 0)
    def _():
        c_new = cnew_ref[0]
        s_new = (jnp.sum(qlat * c_new, axis=-1, keepdims=True)
                 + jnp.sum(qpe * krnew_ref[0], axis=-1, keepdims=True))
        m_ref[...] = s_new
        l_ref[...] = jnp.ones(l_ref.shape, F32)
        acc_ref[...] = jnp.broadcast_to(c_new, acc_ref.shape)

    qlat_b, qpe_b = qlat.astype(BF16), qpe.astype(BF16)
    groups = range(0, n, POOL_PAGES)
    c_bs = [jnp.concatenate([pg[...] for pg in lat_pages[k:k + POOL_PAGES]], axis=0).astype(BF16) for k in groups]
    kr_bs = [jnp.concatenate([pg[...] for pg in kr_pages[k:k + POOL_PAGES]], axis=1).astype(BF16) for k in groups]
    s_lat = [_dot_nt(qlat_b, c_b) for c_b in c_bs]
    s_pe = [_dot(qpe_b, kr_b) for kr_b in kr_bs]
    scores = [a + r for a, r in zip(s_lat, s_pe)]
    maxes = [jnp.max(s, axis=-1, keepdims=True) for s in scores]
    ps = [jnp.exp(s - mx) for s, mx in zip(scores, maxes)]
    sums = [jnp.sum(p_g, axis=-1, keepdims=True) for p_g in ps]
    accs = [_dot(p_g.astype(BF16), c_b) for p_g, c_b in zip(ps, c_bs)]
    m_old = m_ref[...]
    m = m_old
    for mx in maxes:
        m = jnp.maximum(m, mx)
    alpha = jnp.exp(m_old - m)
    l = alpha * l_ref[...]
    acc = alpha * acc_ref[...]
    for mx, l_g, acc_g in zip(maxes, sums, accs):
        w = jnp.exp(mx - m)
        l = l + w * l_g
        acc = acc + w * acc_g
    m_ref[...] = m
    l_ref[...] = l
    acc_ref[...] = acc

    @pl.when(step == n_steps - 1)
    def _():
        o_ref[0] = (acc / l)[0:MLA_HEADS]


def _mla_sample(p, lat_cache, kr_t, pt_flat, b, n_pages):
    page = lat_cache.shape[1]
    head = lambda w: pl.BlockSpec((1, MLA_HEADS, w), lambda bb, c, pt: (bb, 0, 0))
    row = lambda w: pl.BlockSpec((1, 1, w), lambda bb, c, pt: (bb, 0, 0))
    gs = pltpu.PrefetchScalarGridSpec(
        num_scalar_prefetch=1, grid=(b, n_pages // PAGES_PER_STEP),
        in_specs=[pl.BlockSpec(memory_space=pl.ANY), pl.BlockSpec(memory_space=pl.ANY),
                  head(MLA_KV_LORA), head(MLA_ROPE), row(MLA_KV_LORA), row(MLA_ROPE)],
        out_specs=head(MLA_KV_LORA),
        scratch_shapes=[pltpu.VMEM((PAD_ROWS, 1), F32), pltpu.VMEM((PAD_ROWS, 1), F32),
                        pltpu.VMEM((PAD_ROWS, MLA_KV_LORA), F32),
                        pltpu.VMEM((2, PAGES_PER_STEP, page, MLA_KV_LORA), F32),
                        pltpu.VMEM((2, PAGES_PER_STEP, MLA_ROPE, page), F32),
                        pltpu.SemaphoreType.DMA((2,)), pltpu.SemaphoreType.DMA((2,))])
    qlat = p["qlat"].reshape(b, MLA_HEADS, MLA_KV_LORA)
    qpe = p["qpe"].reshape(b, MLA_HEADS, MLA_ROPE)
    return pl.pallas_call(
        _mla_sample_kernel, grid_spec=gs,
        out_shape=jax.ShapeDtypeStruct((b, MLA_HEADS, MLA_KV_LORA), F32),
        compiler_params=pltpu.CompilerParams(dimension_semantics=("arbitrary",) * 2, vmem_limit_bytes=VMEM_LIMIT),
        name="mla_sample",
    )(pt_flat, lat_cache, kr_t, qlat, qpe, p["c"], p["kr"])


def _rope_angles(pos, theta, dim):
    half = dim // 2
    inv = 1.0 / (float(theta) ** (np.arange(half, dtype=np.float64) / half))
    ang = np.asarray(pos, np.float64)[:, None] * inv[None, :]
    return np.cos(ang).astype(np.float32), np.sin(ang).astype(np.float32)


def _rope_lane_tables(pos, rows):
    def table(theta, dim, period, active):
        half = dim // 2
        cos, sin = _rope_angles(pos, theta, dim)
        lane = np.arange(LANES)
        d = lane % period
        is_lo = (d < half) & active(lane)
        is_hi = (d >= half) & (d < dim) & active(lane)
        fi = np.where(d < half, d, np.clip(d - half, 0, half - 1))
        cos_l, sin_l = cos[:, fi], sin[:, fi]
        tab = np.stack([np.where(is_lo | is_hi, cos_l, 1.0), np.where(is_lo, -sin_l, 0.0),
                        np.where(is_hi, sin_l, 0.0)]).astype(np.float32)
        return jnp.asarray(np.broadcast_to(tab, (3, rows, LANES)))

    every = lambda lane: np.ones_like(lane, bool)
    keys_only = lambda lane: (lane % LANES) < HD
    return (table(ROPE_THETA, ROT_DIM, HD, every), table(ROPE_THETA, ROT_DIM, HD, keys_only),
            table(MLA_ROPE_THETA, MLA_ROPE, MLA_ROPE, every))


def _rope_row_tables(pos):
    cq, sq = _rope_angles(pos, ROPE_THETA, ROT_DIM)
    cm, sm = _rope_angles(pos, MLA_ROPE_THETA, MLA_ROPE)
    return jnp.asarray(np.stack([cq.T, sq.T])), jnp.asarray(np.stack([cm.T, sm.T]))


def _pack_weights(l, norm_pre, w_in, pe_cmp, q_norm, w_q_up, kv_norm, w_kv_up, w_proj_a, w_proj_b, w_out, norm_post):
    w = w_in[l].astype(BF16)
    o = IN_OFFSETS
    seg = lambda k: w[:, o[k]:o[k + 1]]
    gn = seg(4)
    per_group = 3 * NSA_GROUP
    gn_t = jnp.zeros((NSA_KV_HEADS * GATE_ROWS, D_MODEL), w.dtype)
    for g in range(NSA_KV_HEADS):
        gn_t = gn_t.at[g * GATE_ROWS:g * GATE_ROWS + per_group].set(gn[:, g * per_group:(g + 1) * per_group].T)
    w_t = jnp.concatenate([seg(0).T, seg(1).T, seg(2).T, seg(3).T, gn_t, seg(8).T], axis=0)
    w_krp4 = jnp.tile(seg(8).T, (LANES // MLA_ROPE, 1))
    w_cols = jnp.concatenate([seg(5), seg(6), seg(7), seg(9), seg(10), seg(11)], axis=1)
    wq = w_q_up[l]
    wqup = jnp.concatenate([wq[..., :MLA_NOPE].reshape(MLA_Q_LORA, -1), wq[..., MLA_NOPE:].reshape(MLA_Q_LORA, -1)],
                           axis=1)
    wkv = w_kv_up[l]
    wuk_pad = jnp.pad(wkv[..., :MLA_NOPE], ((0, 0), (0, 0), (0, LANES - MLA_NOPE)))
    rope_copy = jnp.pad(jnp.eye(MLA_ROPE, dtype=w.dtype), ((0, LANES - MLA_ROPE), (MLA_NOPE, MLA_ROPE)))
    wkx = jnp.concatenate([wuk_pad.reshape(MLA_KV_LORA, MLA_HEADS * LANES), jnp.tile(rope_copy, (1, MLA_HEADS))],
                          axis=0)
    wvt = jnp.transpose(wkv[..., MLA_NOPE:], (1, 2, 0)).reshape(MLA_WIDTH, MLA_KV_LORA)
    w2uk =jnp.transpose(wkv[..., :MLA_NOPE], (1, 2, 0)).reshape(MLA_HEADS // 2, LANES, MLA_KV_LORA)
    wv = jnp.transpose(wkv[..., MLA_NOPE:], (1, 0, 2))
    zeros = jnp.zeros_like(wv)
    even = (jnp.arange(MLA_HEADS) % 2 == 0)[:, None, None]
    wuv = jnp.concatenate([jnp.where(even, wv, zeros), jnp.where(even, zeros, wv)], axis=2)
    return {
        "npre": norm_pre[l][None].astype(F32), "w_t": w_t, "w_krp4": w_krp4, "w_cols": w_cols, "pe": pe_cmp[l].reshape(CMP_BLOCK, KV_WIDTH).astype(F32),
        "qnorm": q_norm[l][None].astype(F32), "wqup": wqup.astype(BF16), "wqupt": wqup.T.astype(BF16),
        "kvnorm": kv_norm[l][None].astype(F32), "w2uk": w2uk.astype(BF16), "wkx": wkx.astype(BF16),
        "wvt": wvt.astype(BF16), "wuv": wuv.astype(BF16),
        "wpa": w_proj_a[l].astype(BF16), "wpb": w_proj_b[l].astype(BF16), "wout": w_out[l].astype(BF16),
        "npost": norm_post[l][None].astype(F32),
    }


def _rows_from_cols(a):
    b, _, t = a.shape
    return a.reshape(b, NSA_KV_HEADS, 2, HD, t).transpose(0, 4, 1, 2, 3)


def _cols_from_rows(a):
    n, t = a.shape[:2]
    return a.transpose(0, 2, 3, 4, 1).reshape(n, KV_WIDTH, t)


def _prompt_layer(x, wts, compress_job, sample_job):
    b, t, _ = x.shape
    assert t % PROMPT_TM == 0
    p = _in_project_cols(x, _rope_row_tables(np.arange(t)), wts)
    o_a, kc_all = _nsa_prompt(p, b, t, compress_job)
    o_b, o_lat = _mla_prompt(p, wts, b, t, sample_job)
    y = _merge(x, o_a, o_b, p, wts, 512, from_latent=False)
    win_keep = min(WINDOW, t)
    return y, (_rows_from_cols(p["cmp"]), _rows_from_cols(p["slc"]), p["c"], p["kr"].transpose(0, 2, 1),
               _rows_from_cols(p["win"][:, :, t - win_keep:])), kc_all, o_lat


def _sample_project(x, page_table, page, wts):
    b, s_new, _ = x.shape
    assert s_new == 1
    past = page_table.shape[1] * page
    tabs = _rope_lane_tables(np.full((1,), past), b)
    p = _in_project_rows(x.reshape(1, b, D_MODEL), tabs, wts)
    return {k: v.reshape(b, 1, v.shape[-1]) for k, v in p.items()}


def _sample_layer(x, p, l, caches, state_win, page_table, wts, kc_all, o_lat):
    cache_cmp, cache_slc, cache_lat, cache_kr = caches
    b = x.shape[0]
    n_pages = page_table.shape[1]
    page = cache_cmp.shape[2]
    past = n_pages * page
    assert past % SEL_BLOCK == 0 and n_pages % PAGES_PER_STEP == 0 and page == LANES
    win_len = state_win.shape[1]
    assert win_len == WINDOW and past >= WINDOW
    pt_flat = page_table.reshape(-1).astype(jnp.int32)

    if kc_all is None:
        kc_all = _compress_pages(_cols_from_rows(cache_cmp[l]), pt_flat, wts["pe"], b, n_pages)
    oc, idx = _sample_select(p["q"], kc_all, b)
    o_a, new_win = _sample_attend(p, oc, idx.reshape(-1), pt_flat, _cols_from_rows(cache_slc[l]),
                                  _cols_from_rows(state_win), b, n_pages)
    if o_lat is None:
        o_lat = _mla_sample(p, cache_lat[l], cache_kr[l].transpose(0, 2, 1), pt_flat, b, n_pages)
    pm = {k: p[k].reshape(1, b, -1) for k in ("ga", "gb", "ma", "mb")}
    y = _merge(x.reshape(1, b, D_MODEL), o_a.reshape(1, b, NSA_WIDTH),
               o_lat.reshape(1, b, MLA_HEADS * MLA_KV_LORA), pm, wts, b, from_latent=True)
    kv6 = lambda a: a.reshape(b, 1, NSA_KV_HEADS, 2, HD)
    return y.reshape(b, 1, D_MODEL), (kv6(p["cmp"]), kv6(p["slc"]), p["c"], p["kr"], _rows_from_cols(new_win))


def kernel(x_prompt, x_sample, cache_nsa_cmp, cache_nsa_slc, cache_mla_latent, cache_mla_krope, state_nsa_win,
           page_table, norm_pre, w_in, pe_cmp, q_norm, w_q_up, kv_norm, w_kv_up, w_proj_a, w_proj_b, w_out,
           norm_post):
    depth = w_in.shape[0]
    hp, hs = x_prompt, x_sample
    new_p, new_s = [], []
    for l in range(depth):
        wts = _pack_weights(l, norm_pre, w_in, pe_cmp, q_norm, w_q_up, kv_norm, w_kv_up, w_proj_a, w_proj_b,
                            w_out, norm_post)
        n_pages = page_table.shape[1]
        compress_job = None
        if n_pages % PAGES_PER_STEP == 0:
            compress_job = (_cols_from_rows(cache_nsa_cmp[l]), page_table.reshape(-1).astype(jnp.int32), wts["pe"],
                            page_table.shape[0], n_pages)
        ps = _sample_project(hs, page_table, cache_nsa_cmp.shape[2], wts)
        sample_job = (cache_mla_latent[l], cache_mla_krope[l].transpose(0, 2, 1),
                      page_table.reshape(-1).astype(jnp.int32), ps, page_table.shape[0], n_pages)
        hp, sp, kc_all, o_lat = _prompt_layer(hp, wts, compress_job, sample_job)
        hs, ss = _sample_layer(hs, ps, l, (cache_nsa_cmp, cache_nsa_slc, cache_mla_latent, cache_mla_krope),
                               state_nsa_win[l], page_table, wts, kc_all, o_lat)
        new_p.append(sp)
        new_s.append(ss)
    stack = lambda items, k: jnp.stack([s[k] for s in items])
    return (hp, hs) + tuple(stack(new_p, k) for k in range(5)) + tuple(stack(new_s, k) for k in range(5))
```

```python
import functools

import numpy as np
import jax
import jax.numpy as jnp
from jax import lax
from jax.experimental import pallas as pl
from jax.experimental.pallas import tpu as pltpu

D_MODEL = 1024
NSA_HEADS = 8
NSA_KV_HEADS = 2
NSA_GROUP = NSA_HEADS // NSA_KV_HEADS
HD = 64
NSA_WIDTH = NSA_HEADS * HD
KV_WIDTH = NSA_KV_HEADS * 2 * HD
ROT_DIM = HD // 4
ROPE_THETA = 500000.0
CMP_BLOCK = 32
SEL_BLOCK = 64
N_SELECT = 16
WINDOW = 512
NSA_SCALE = HD ** -0.5

MLA_HEADS = 8
MLA_Q_LORA = 384
MLA_KV_LORA = 256
MLA_NOPE = 64
MLA_ROPE = 32
MLA_V = 64
MLA_WIDTH = MLA_HEADS * MLA_V
MLA_ROPE_THETA = 10000.0
MLA_SCALE = (MLA_NOPE + MLA_ROPE) ** -0.5
LOG2E = 1.4426950408889634
SHIFT_LIMIT = 40.0

RMS_EPS = 1e-6
NEG = -1e30
FORCE_SCORE = 1e4

IN_SPLITS = (NSA_WIDTH, KV_WIDTH, KV_WIDTH, KV_WIDTH, 3 * NSA_HEADS, NSA_WIDTH,
             MLA_Q_LORA, MLA_KV_LORA, MLA_ROPE, MLA_WIDTH, D_MODEL, D_MODEL)
IN_OFFSETS = tuple(int(v) for v in np.cumsum((0,) + IN_SPLITS))

LANES = 128
HALF = LANES // 2
GATE_ROWS = 16
PAD_ROWS = 16

PT_Q = (0, 512)
PT_CMP = (512, 768)
PT_SLC = (768, 1024)
PT_WIN = (1024, 1280)
PT_GN = (1280, 1280 + NSA_KV_HEADS * GATE_ROWS)
PT_KRP = (PT_GN[1], PT_GN[1] + MLA_ROPE)
PT_ROWS = PT_KRP[1]
PR_GA = (0, 512)
PR_QD = (512, 896)
PR_KVD = (896, 1152)
PR_GB = (1152, 1664)
PR_MA = (1664, 2688)
PR_MB = (2688, 3712)
PR_COLS = 3712

PROMPT_TM = 512
ATT_TQ = 256
ATT_TK = 256
MLA_TK = 256

VMEM_LIMIT = 48 * 1024 * 1024
BF16 = jnp.bfloat16
F32 = jnp.float32


def _full_spec(shape):
    nd = len(shape)
    return pl.BlockSpec(shape, lambda *_: (0,) * nd)


def _lane_lo(rows):
    return lax.broadcasted_iota(jnp.int32, (rows, LANES), 1) < HALF


def _dot(a, b):
    return jnp.dot(a, b, preferred_element_type=F32)


def _dot_nt(a, b):
    return lax.dot_general(a, b, (((1,), (1,)), ((), ())), preferred_element_type=F32)


def _rms(v, gain):
    return v * lax.rsqrt(jnp.mean(v * v, axis=-1, keepdims=True) + RMS_EPS) * gain


def _split_bf16(v):
    hi = v.astype(BF16)
    return hi, (v - hi.astype(F32)).astype(BF16)


def _rope_tiles(v, tab_ref, shift):
    c, s_lo, s_hi = tab_ref[0], tab_ref[1], tab_ref[2]
    out = []
    for k in range(v.shape[1] // LANES):
        a = v[:, k * LANES:(k + 1) * LANES]
        out.append(a * c + pltpu.roll(a, LANES - shift, 1) * s_lo + pltpu.roll(a, shift, 1) * s_hi)
    return out[0] if len(out) == 1 else jnp.concatenate(out, axis=1)


def _dup_kv(a, lo):
    r = pltpu.roll(a, HALF, 1)
    return jnp.where(lo, a, r), jnp.where(lo, r, a)


def _kv_pack(v):
    lo = _lane_lo(v.shape[0])
    parts = []
    for g in range(NSA_KV_HEADS):
        kk, vv = _dup_kv(v[:, g * LANES:(g + 1) * LANES], lo)
        parts += [kk, vv]
    return jnp.concatenate(parts, axis=1).astype(BF16)


def _inproj_rows_kernel(x_ref, npre_ref, wt_ref, wkrp_ref, w_ref, tq_ref, tkv_ref, tm_ref, qnorm_ref, wqup_ref,
                        kvnorm_ref, w2uk_ref, q_ref, qrot_ref, gate_ref, cmp_ref, slc_ref, win_ref, slckv_ref,
                        winkv_ref, ga_ref, gb_ref, ma_ref, mb_ref, qpe_ref, c_ref, kr_ref, qlat_ref):
    xb = _rms(x_ref[0], npre_ref[...]).astype(BF16)
    segt = lambda lohi: _dot_nt(xb, wt_ref[lohi[0]:lohi[1], :])
    seg = lambda lohi: _dot(xb, w_ref[:, lohi[0]:lohi[1]])

    a = segt((PT_Q[0], PT_WIN[1]))
    q = a[:, :NSA_WIDTH]
    q_ref[0] = (q * NSA_SCALE).astype(BF16)
    qrot_ref[0] = (_rope_tiles(q, tq_ref, ROT_DIM // 2) * NSA_SCALE).astype(BF16)
    cmp_ref[0] = a[:, NSA_WIDTH:NSA_WIDTH + KV_WIDTH]
    kvs = _rope_tiles(a[:, NSA_WIDTH + KV_WIDTH:NSA_WIDTH + 2 * KV_WIDTH], tkv_ref, ROT_DIM // 2)
    slc_ref[0] = kvs
    slckv_ref[0] = _kv_pack(kvs)
    kvw = _rope_tiles(a[:, NSA_WIDTH + 2 * KV_WIDTH:], tkv_ref, ROT_DIM // 2)
    win_ref[0] = kvw
    winkv_ref[0] = _kv_pack(kvw)

    gate_ref[0] = jax.nn.sigmoid(segt(PT_GN))
    ga_ref[0] = seg(PR_GA).astype(BF16)
    gb_ref[0] = seg(PR_GB).astype(BF16)
    ma_ref[0] = seg(PR_MA).astype(BF16)
    mb_ref[0] = seg(PR_MB).astype(BF16)

    qd = _rms(seg(PR_QD), qnorm_ref[...]).astype(BF16)
    qh = _dot(qd, wqup_ref[...])
    qn = qh[:, :MLA_HEADS * MLA_NOPE].astype(BF16)
    qpe_ref[0] = _rope_tiles(qh[:, MLA_HEADS * MLA_NOPE:], tm_ref, MLA_ROPE // 2) * MLA_SCALE
    c_ref[0] = _rms(seg(PR_KVD), kvnorm_ref[...])
    kr_ref[0] = _rope_tiles(_dot_nt(xb, wkrp_ref[...]), tm_ref, MLA_ROPE // 2)[:, :MLA_ROPE]

    rows = qn.shape[0]
    lo = _lane_lo(rows)
    z = jnp.zeros((rows, LANES), BF16)
    parts = []
    for j in range(MLA_HEADS // 2):
        pair = qn[:, j * LANES:(j + 1) * LANES]
        parts.append(_dot(jnp.where(lo, pair, z), w2uk_ref[j]) * MLA_SCALE)
        parts.append(_dot(jnp.where(lo, z, pair), w2uk_ref[j]) * MLA_SCALE)
    qlat_ref[0] = jnp.concatenate(parts, axis=1)


def _in_project_rows(x3, tabs, wts):
    b, t, _ = x3.shape
    tq, tkv, tmla = tabs
    row = lambda w, dt: jax.ShapeDtypeStruct((b, t, w), dt)
    names = ["q", "qrot", "gate", "cmp", "slc", "win", "slckv", "winkv", "ga", "gb", "ma", "mb", "qpe", "c", "kr",
             "qlat"]
    out_shape = [row(512, BF16), row(512, BF16), row(NSA_KV_HEADS * GATE_ROWS, F32), row(256, F32), row(256, F32),
                 row(256, F32),
                 row(512, BF16), row(512, BF16), row(512, BF16), row(512, BF16), row(1024, BF16), row(1024, BF16),
                 row(MLA_HEADS * MLA_ROPE, F32), row(MLA_KV_LORA, F32), row(MLA_ROPE, F32),
                 row(MLA_HEADS * MLA_KV_LORA, F32)]
    tok = lambda w: pl.BlockSpec((1, t, w), lambda bb: (bb, 0, 0))
    tab = _full_spec((3, t, LANES))
    in_specs = [tok(D_MODEL), _full_spec((1, D_MODEL)), _full_spec((PT_ROWS, D_MODEL)), _full_spec((LANES, D_MODEL)),
                _full_spec((D_MODEL, PR_COLS)), tab, tab, tab, _full_spec((1, MLA_Q_LORA)), _full_spec((MLA_Q_LORA, MLA_HEADS * (MLA_NOPE + MLA_ROPE))),
                _full_spec((1, MLA_KV_LORA)), _full_spec((MLA_HEADS // 2, LANES, MLA_KV_LORA))]
    res = pl.pallas_call(
        _inproj_rows_kernel, grid=(b,), in_specs=in_specs, out_specs=[tok(s.shape[2]) for s in out_shape],
        out_shape=out_shape,
        compiler_params=pltpu.CompilerParams(dimension_semantics=("arbitrary",), vmem_limit_bytes=VMEM_LIMIT),
        name="in_project_rows",
    )(x3, wts["npre"], wts["w_t"], wts["w_krp4"], wts["w_cols"], tq, tkv, tmla, wts["qnorm"], wts["wqup"],
      wts["kvnorm"], wts["w2uk"])
    return dict(zip(names, res))


def _rope_rows(x, cos, sin, half):
    x1, x2 = x[0:half], x[half:2 * half]
    parts = [x1 * cos - x2 * sin, x1 * sin + x2 * cos]
    if x.shape[0] > 2 * half:
        parts.append(x[2 * half:])
    return jnp.concatenate(parts, axis=0)


def _store_chunks(ref, v):
    tk = ref.shape[3]
    for j in range(ref.shape[1]):
        ref[0, j] = v[:, j * tk:(j + 1) * tk]


def _inproj_cols_kernel(x_ref, npre_ref, wt_ref, w_ref, ropeq_ref, ropem_ref, pe_ref, qnorm_ref, wqupt_ref,
                        kvnorm_ref, pool_ref, wkx_ref, wvt_ref,
                        qt_ref, qrt_ref, gate_ref, cmp_ref, slc_ref, win_ref, slcb_ref, winb_ref, slcr_ref,
                        winr_ref, kc_ref, ga_ref, gb_ref, ma_ref, mb_ref, qmt_ref, c_ref, kmla_ref, vmt_ref,
                        kr_ref, kn_ref):
    tm = x_ref.shape[1]
    xb = _rms(x_ref[0], npre_ref[...]).astype(BF16)
    segt = lambda lohi: _dot_nt(wt_ref[lohi[0]:lohi[1], :], xb)
    seg = lambda lohi: _dot(xb, w_ref[:, lohi[0]:lohi[1]])
    cq, sq = ropeq_ref[0], ropeq_ref[1]
    cm, sm = ropem_ref[0], ropem_ref[1]
    hq, hm = ROT_DIM // 2, MLA_ROPE // 2

    qt = segt(PT_Q)
    qt_ref[0] = (qt * (NSA_SCALE * LOG2E)).astype(BF16)
    qrt = jnp.concatenate([_rope_rows(qt[h * HD:(h + 1) * HD], cq, sq, hq) for h in range(NSA_HEADS)], axis=0)
    qrt_ref[0] = (qrt * (NSA_SCALE * LOG2E)).astype(BF16)

    cmpt = segt(PT_CMP)
    cmp_ref[0] = cmpt
    hi, lo = _split_bf16(cmpt)
    pool = pool_ref[...]
    pooled = (_dot_nt(pool, hi) + _dot_nt(pool, lo))[0:tm // CMP_BLOCK]
    kc_ref[0] = (pooled + jnp.sum(pe_ref[...], axis=0, keepdims=True)) * (1.0 / CMP_BLOCK)

    def rope_kv(v):
        parts = []
        for g in range(NSA_KV_HEADS):
            parts.append(_rope_rows(v[g * LANES:g * LANES + HD], cq, sq, hq))
            parts.append(v[g * LANES + HD:(g + 1) * LANES])
        return jnp.concatenate(parts, axis=0)

    slct = rope_kv(segt(PT_SLC))
    slc_ref[0] = slct
    _store_chunks(slcb_ref, slct.astype(BF16))
    slcr_ref[0] = slct.T.astype(BF16)
    wint = rope_kv(segt(PT_WIN))
    win_ref[0] = wint
    _store_chunks(winb_ref, wint.astype(BF16))
    winr_ref[0] = wint.T.astype(BF16)

    gate_ref[0] = jax.nn.sigmoid(segt(PT_GN))
    krt = _rope_rows(segt(PT_KRP), cm, sm, hm)
    kr_ref[0] = krt
    c = _rms(seg(PR_KVD), kvnorm_ref[...])
    c_ref[0] = c
    c_b = c.astype(BF16)
    kr_rows = jnp.concatenate([krt, jnp.zeros((LANES - MLA_ROPE, tm), F32)], axis=0).T
    ckr = jnp.concatenate([c_b, kr_rows.astype(BF16)], axis=1)
    kmla = _dot(ckr, wkx_ref[...]).astype(BF16)
    kmla_ref[0] = kmla
    kf = kmla.astype(F32)
    norms = []
    for h in range(MLA_HEADS):
        n2 = jnp.sum(kf[:, h * LANES:(h + 1) * LANES] ** 2, axis=1, keepdims=True)
        norms.append(jnp.broadcast_to(jnp.sqrt(jnp.max(n2, axis=0, keepdims=True)), (1, LANES)))
    kn_ref[0] = jnp.concatenate(norms, axis=0)
    _store_chunks(vmt_ref, _dot_nt(wvt_ref[...], c_b).astype(BF16))

    qd = _rms(seg(PR_QD), qnorm_ref[...]).astype(BF16)
    qht = _dot_nt(wqupt_ref[...], qd)
    n_nope = MLA_HEADS * MLA_NOPE
    zq = jnp.zeros((LANES - MLA_NOPE - MLA_ROPE, tm), F32)
    parts = []
    for h in range(MLA_HEADS):
        parts += [qht[h * MLA_NOPE:(h + 1) * MLA_NOPE],
                  _rope_rows(qht[n_nope + h * MLA_ROPE:n_nope + (h + 1) * MLA_ROPE], cm, sm, hm), zq]
    qmt_ref[0] = (jnp.concatenate(parts, axis=0) * (MLA_SCALE * LOG2E)).astype(BF16)

    ga_ref[0] = seg(PR_GA).astype(BF16)
    gb_ref[0] = seg(PR_GB).astype(BF16)
    ma_ref[0] = seg(PR_MA).astype(BF16)
    mb_ref[0] = seg(PR_MB).astype(BF16)


def _in_project_cols(x3, ropes, wts):
    b, t, _ = x3.shape
    tm = PROMPT_TM
    nt = t // tm
    ropeq, ropem = ropes
    pool = np.zeros((16, tm), np.float32)
    for s in range(tm):
        pool[s // CMP_BLOCK, s] = 1.0
    sds = jax.ShapeDtypeStruct
    rows = lambda w: pl.BlockSpec((1, tm, w), lambda i, bb: (bb, i, 0))
    cols = lambda w: pl.BlockSpec((1, w, tm), lambda i, bb: (bb, 0, i))
    chunk = lambda w, tk: pl.BlockSpec((1, tm // tk, w, tk), lambda i, bb: (bb, i, 0, 0))
    outs = [
        ("qt", sds((b, NSA_WIDTH, t), BF16), cols(NSA_WIDTH)),
        ("qrt", sds((b, NSA_WIDTH, t), BF16), cols(NSA_WIDTH)),
        ("gate", sds((b, NSA_KV_HEADS * GATE_ROWS, t), F32), cols(NSA_KV_HEADS * GATE_ROWS)),
        ("cmp", sds((b, KV_WIDTH, t), F32), cols(KV_WIDTH)),
        ("slc", sds((b, KV_WIDTH, t), F32), cols(KV_WIDTH)),
        ("win", sds((b, KV_WIDTH, t), F32), cols(KV_WIDTH)),
        ("slcb", sds((b, t // ATT_TK, KV_WIDTH, ATT_TK), BF16), chunk(KV_WIDTH, ATT_TK)),
        ("winb", sds((b, t // ATT_TK, KV_WIDTH, ATT_TK), BF16), chunk(KV_WIDTH, ATT_TK)),
        ("slcr", sds((b, t, KV_WIDTH), BF16), rows(KV_WIDTH)),
        ("winr", sds((b, t, KV_WIDTH), BF16), rows(KV_WIDTH)),
        ("kc", sds((b, t // CMP_BLOCK, KV_WIDTH), F32),
         pl.BlockSpec((1, tm // CMP_BLOCK, KV_WIDTH), lambda i, bb: (bb, i, 0))),
        ("ga", sds((b, t, NSA_WIDTH), BF16), rows(NSA_WIDTH)),
        ("gb", sds((b, t, MLA_WIDTH), BF16), rows(MLA_WIDTH)),
        ("ma", sds((b, t, D_MODEL), BF16), rows(D_MODEL)),
        ("mb", sds((b, t, D_MODEL), BF16), rows(D_MODEL)),
        ("qmt", sds((b, MLA_HEADS * LANES, t), BF16), cols(MLA_HEADS * LANES)),
        ("c", sds((b, t, MLA_KV_LORA), F32), rows(MLA_KV_LORA)),
        ("kmla", sds((b, t, MLA_HEADS * LANES), BF16), rows(MLA_HEADS * LANES)),
        ("vmt", sds((b, t // MLA_TK, MLA_WIDTH, MLA_TK), BF16), chunk(MLA_WIDTH, MLA_TK)),
        ("kr", sds((b, MLA_ROPE, t), F32), cols(MLA_ROPE)),
        ("kn", sds((b, nt * MLA_HEADS, LANES), F32), pl.BlockSpec((1, MLA_HEADS, LANES), lambda i, bb: (bb, i, 0))),
    ]
    rope_spec = lambda half: pl.BlockSpec((2, half, tm), lambda i, bb: (0, 0, i))
    in_specs = [rows(D_MODEL), _full_spec((1, D_MODEL)), _full_spec((PT_ROWS, D_MODEL)),
                _full_spec((D_MODEL, PR_COLS)), rope_spec(ROT_DIM // 2), rope_spec(MLA_ROPE // 2),
                _full_spec((CMP_BLOCK, KV_WIDTH)), _full_spec((1, MLA_Q_LORA)),
                _full_spec((MLA_HEADS * (MLA_NOPE + MLA_ROPE), MLA_Q_LORA)), _full_spec((1, MLA_KV_LORA)),
                _full_spec((16, tm)), _full_spec((MLA_KV_LORA + LANES, MLA_HEADS * LANES)),
                _full_spec((MLA_WIDTH, MLA_KV_LORA))]
    res = pl.pallas_call(
        _inproj_cols_kernel, grid=(nt, b), in_specs=in_specs, out_specs=[o[2] for o in outs],
        out_shape=[o[1] for o in outs],
        compiler_params=pltpu.CompilerParams(dimension_semantics=("arbitrary", "arbitrary"),
                                             vmem_limit_bytes=VMEM_LIMIT),
        name="in_project_cols",
    )(x3, wts["npre"], wts["w_t"], wts["w_cols"], ropeq, ropem, wts["pe"], wts["qnorm"], wts["wqupt"],
      wts["kvnorm"], jnp.asarray(pool, BF16), wts["wkx"], wts["wvt"])
    return dict(zip([o[0] for o in outs], res))


def _online_update(s, vt, m, l, acc):
    d, keys = vt.shape
    m_new = jnp.maximum(m, jnp.max(s, axis=0, keepdims=True))
    p = jnp.exp2(s - m_new).astype(BF16)
    alpha = jnp.exp2(m - m_new)
    pv = _dot(jnp.concatenate([vt, jnp.ones((PAD_ROWS, keys), BF16)], axis=0), p)
    return m_new, alpha * l + pv[d:d + 1], alpha * acc + pv[0:d]


def _nsa_prompt_kernel(n_blk, with_compress, *refs):
    if with_compress:
        (pt_ref, qt_ref, qrt_ref, gate_ref, kc_ref, ks_ref, kst_ref, kw_ref, kwt_ref, cache_ref, pool_ref, pe_ref,
         o_ref, kcall_ref, sb_ref, kn_ref, as_ref, aw_ref, buf_ref, sem_ref) = refs
        step = (pl.program_id(0) * pl.num_programs(1) + pl.program_id(1)) * pl.num_programs(2) + pl.program_id(2)
        last = pl.num_programs(0) * pl.num_programs(1) * pl.num_programs(2) - 1
        _compress_step(pt_ref, cache_ref, pool_ref, pe_ref, kcall_ref, buf_ref, sem_ref, step, last)
    else:
        (qt_ref, qrt_ref, gate_ref, kc_ref, ks_ref, kst_ref, kw_ref, kwt_ref, o_ref, sb_ref, kn_ref, as_ref,
         aw_ref) = refs
    i = pl.program_id(2)
    tq, tk = ATT_TQ, ATT_TK
    nl = NSA_GROUP * tq
    zq = jnp.zeros((HD, tq), BF16)

    def widen(qt):
        return jnp.concatenate([jnp.concatenate([qt[r * HD:(r + 1) * HD], zq], axis=0)
                                for r in range(NSA_GROUP)], axis=1)

    qc = widen(qt_ref[0])
    qr = widen(qrt_ref[0])

    n_cmp = 2 * n_blk
    kc = kc_ref[0]
    s_c = _dot(kc.astype(BF16), qc)
    rho = lax.broadcasted_iota(jnp.int32, (n_cmp, nl), 0)
    cmp_idx = jnp.where(rho < n_blk, 2 * rho, 2 * (rho - n_blk) + 1)
    t_l = i * tq + (lax.broadcasted_iota(jnp.int32, (n_cmp, nl), 1) & (tq - 1))
    mask_c = cmp_idx * CMP_BLOCK + (CMP_BLOCK - 1) <= t_l
    s_c = jnp.where(mask_c, s_c, NEG)
    e_c = jnp.where(mask_c, jnp.exp2(s_c - jnp.max(s_c, axis=0, keepdims=True)), 0.0)
    p_c = e_c / jnp.maximum(jnp.sum(e_c, axis=0, keepdims=True), 1e-30)
    kct = jnp.concatenate([kc, jnp.zeros((LANES - n_cmp, LANES), F32)], axis=0).T
    p_pad = jnp.concatenate([p_c, jnp.zeros((LANES - n_cmp, nl), F32)], axis=0)
    o_c = _dot(kct[HD:2 * HD].astype(BF16), p_pad.astype(BF16))

    imp = p_c[:, 0:tq]
    for r in range(1, NSA_GROUP):
        imp = imp + p_c[:, r * tq:(r + 1) * tq]
    imp_blk = imp[0:n_blk] + imp[n_blk:n_cmp]
    blk = lax.broadcasted_iota(jnp.int32, (n_blk, tq), 0)
    t_q = i * tq + lax.broadcasted_iota(jnp.int32, (n_blk, tq), 1)
    ahead_of = t_q - blk * SEL_BLOCK
    score = jnp.where(blk == 0, FORCE_SCORE,
                      jnp.where(ahead_of < 0, -FORCE_SCORE, jnp.where(ahead_of < SEL_BLOCK, FORCE_SCORE, imp_blk)))
    def count_ahead():
        rank = jnp.zeros((n_blk, tq), F32)
        for j in range(n_blk):
            other = score[j:j + 1, :]
            tie = jnp.where(blk > j, 1.0, 0.0)
            rank = rank + jnp.where(other > score, 1.0, jnp.where(other == score, tie, 0.0))
        return rank

    few_blocks = (i + 1) * tq <= N_SELECT * SEL_BLOCK
    rank = lax.cond(few_blocks, lambda: jnp.zeros((n_blk, tq), F32), count_ahead)
    sb_ref[...] = jnp.where(rank < N_SELECT, jnp.where(score > -1.0, 0.0, NEG), NEG)

    key_r = lax.broadcasted_iota(jnp.int32, (tk, tq), 0)
    t_k = i * tq + lax.broadcasted_iota(jnp.int32, (tk, tq), 1)
    n_chunks = ((i + 1) * tq + tk - 1) // tk
    c_lo = jnp.maximum(i * tq - WINDOW, 0) // tk
    per_chunk = tk // SEL_BLOCK
    tile = lambda bias: jnp.concatenate([bias] * NSA_GROUP, axis=1)
    ones = jnp.ones((PAD_ROWS, tk), BF16)

    def sel_bias(c):
        rows = [jnp.broadcast_to(sb_ref[pl.ds(c * per_chunk + j, 1), :], (SEL_BLOCK, tq)) for j in range(per_chunk)]
        return jnp.concatenate(rows, axis=0)

    def values(ref, c):
        return jnp.concatenate([ref[0, c, HD:2 * HD, :], ones], axis=0)

    qf = qrt_ref[0].astype(F32)
    q_norm2 = [jnp.sum(qf[r * HD:(r + 1) * HD] ** 2, axis=0, keepdims=True) for r in range(NSA_GROUP)]
    q_norm = jnp.sqrt(jnp.maximum(jnp.maximum(q_norm2[0], q_norm2[1]), jnp.maximum(q_norm2[2], q_norm2[3])))

    def key_norm(ref):
        kf = ref[0, :, 0:HD, :].astype(F32)
        n2 = jnp.max(jnp.sum(kf * kf, axis=1, keepdims=True), axis=0)
        return jnp.broadcast_to(jnp.sqrt(jnp.max(n2, axis=1, keepdims=True)), (1, LANES))

    @pl.when(i == 0)
    def _():
        kn_ref[0:1, :] = key_norm(kst_ref)
        kn_ref[1:2, :] = key_norm(kwt_ref)

    wide = lambda row: jnp.concatenate([row] * (tq // LANES), axis=1)
    bound_s = q_norm * (wide(kn_ref[0:1, :]) * 1.001) + 1e-3
    bound_w = q_norm * (wide(kn_ref[1:2, :]) * 1.001) + 1e-3
    safe = jnp.maximum(jnp.max(bound_s), jnp.max(bound_w)) <= SHIFT_LIMIT

    def attend_shifted():
        def past_probs(c):
            base = pl.multiple_of(c * tk, tk)
            s = _dot(ks_ref[0, pl.ds(base, tk), :], qr) + tile(sel_bias(c) - bound_s)
            return jnp.exp2(s).astype(BF16)

        def past_body(c, z):
            as_ref[...] += _dot(values(kst_ref, c), past_probs(c))
            return z

        def past_pair(cc, z):
            p0, p1 = past_probs(2 * cc), past_probs(2 * cc + 1)
            as_ref[...] += _dot(values(kst_ref, 2 * cc), p0) + _dot(values(kst_ref, 2 * cc + 1), p1)
            return z

        def near_probs(c):
            base = pl.multiple_of(c * tk, tk)
            s = _dot(ks_ref[0, pl.ds(base, tk), :], qr)
            w = _dot(kw_ref[0, pl.ds(base, tk), :], qr)
            dist = t_k - (base + key_r)
            causal = jnp.where(dist >= 0, 0.0, NEG)
            p_s = jnp.exp2(s + tile(sel_bias(c) + causal - bound_s)).astype(BF16)
            p_w = jnp.exp2(w + tile(jnp.where(dist <= WINDOW, causal, NEG) - bound_w)).astype(BF16)
            return p_s, p_w

        def near_body(c, z):
            p_s, p_w = near_probs(c)
            d_s, d_w = _dot(values(kst_ref, c), p_s), _dot(values(kwt_ref, c), p_w)
            as_ref[...] += d_s
            aw_ref[...] += d_w
            return z

        def near_pair(cc, z):
            c0 = c_lo + 2 * cc
            (ps0, pw0), (ps1, pw1) = near_probs(c0), near_probs(c0 + 1)
            d_s = _dot(values(kst_ref, c0), ps0) + _dot(values(kst_ref, c0 + 1), ps1)
            d_w = _dot(values(kwt_ref, c0), pw0) + _dot(values(kwt_ref, c0 + 1), pw1)
            as_ref[...] += d_s
            aw_ref[...] += d_w
            return z

        as_ref[...] = jnp.zeros(as_ref.shape, F32)
        aw_ref[...] = jnp.zeros(aw_ref.shape, F32)
        lax.fori_loop(0, c_lo // 2, past_pair, 0)
        lax.fori_loop(2 * (c_lo // 2), c_lo, past_body, 0)
        n_pairs = (n_chunks - c_lo) // 2
        lax.fori_loop(0, n_pairs, near_pair, 0)
        lax.fori_loop(c_lo + 2 * n_pairs, n_chunks, near_body, 0)
        inv_s, inv_w = 1.0 / as_ref[HD:HD + 1, :], 1.0 / aw_ref[HD:HD + 1, :]
        return as_ref[0:HD, :] * inv_s, aw_ref[0:HD, :] * inv_w

    def attend_online():
        init = (jnp.full((1, nl), NEG, F32), jnp.zeros((1, nl), F32), jnp.zeros((HD, nl), F32))

        def past_body(c, carry):
            base = pl.multiple_of(c * tk, tk)
            s = _dot(ks_ref[0, pl.ds(base, tk), :], qr) + tile(sel_bias(c))
            return _online_update(s, kst_ref[0, c, HD:2 * HD, :], *carry)

        def near_body(c, carry):
            base = pl.multiple_of(c * tk, tk)
            s = _dot(ks_ref[0, pl.ds(base, tk), :], qr)
            w = _dot(kw_ref[0, pl.ds(base, tk), :], qr)
            dist = t_k - (base + key_r)
            causal = jnp.where(dist >= 0, 0.0, NEG)
            s = s + tile(sel_bias(c) + causal)
            w = w + tile(jnp.where(dist <= WINDOW, causal, NEG))
            return (_online_update(s, kst_ref[0, c, HD:2 * HD, :], *carry[:3])
                    + _online_update(w, kwt_ref[0, c, HD:2 * HD, :], *carry[3:]))

        far = lax.fori_loop(0, c_lo, past_body, init)
        _, l_s, a_s, _, l_w, a_w = lax.fori_loop(c_lo, n_chunks, near_body, far + init)
        return a_s / l_s, a_w / l_w

    o_s, o_w = lax.cond(safe, attend_shifted, attend_online)

    gate = gate_ref[0]
    heads = []
    for r in range(NSA_GROUP):
        sl = slice(r * tq, (r + 1) * tq)
        heads.append(gate[3 * r:3 * r + 1] * o_c[:, sl] + gate[3 * r + 1:3 * r + 2] * o_s[:, sl]
                     + gate[3 * r + 2:3 * r + 3] * o_w[:, sl])
    o_ref[0] = jnp.concatenate(heads, axis=0).T.astype(BF16)


def _nsa_prompt(p, b, t, compress_job=None):
    tq, tk = ATT_TQ, ATT_TK
    n_blk = t // SEL_BLOCK
    assert 2 * n_blk <= LANES and t % tk == 0
    kc = p["kc"].reshape(b, n_blk, 2, KV_WIDTH).transpose(0, 2, 1, 3).reshape(b, 2 * n_blk, KV_WIDTH)
    nq = t // tq
    grid = (b, NSA_KV_HEADS, nq)
    qspec = pl.BlockSpec((1, NSA_GROUP * HD, tq), lambda bb, g, i, *_: (bb, g, i))
    rm = pl.BlockSpec((1, t, LANES), lambda bb, g, i, *_: (bb, 0, g))
    fm = pl.BlockSpec((1, t // tk, LANES, tk), lambda bb, g, i, *_: (bb, 0, g, 0))
    in_specs = [qspec, qspec, pl.BlockSpec((1, GATE_ROWS, tq), lambda bb, g, i, *_: (bb, g, i)),
                pl.BlockSpec((1, 2 * n_blk, LANES), lambda bb, g, i, *_: (bb, 0, g)), rm, fm, rm, fm]
    out_specs = [pl.BlockSpec((1, tq, NSA_GROUP * HD), lambda bb, g, i, *_: (bb, i, g))]
    out_shape = [jax.ShapeDtypeStruct((b, t, NSA_WIDTH), BF16)]
    acc_shape = (HD + PAD_ROWS, NSA_GROUP * tq)
    scratch = [pltpu.VMEM((n_blk, tq), F32), pltpu.VMEM((8, LANES), F32), pltpu.VMEM(acc_shape, F32),
               pltpu.VMEM(acc_shape, F32)]
    args = [p["qt"], p["qrt"], p["gate"], kc, p["slcr"], p["slcb"], p["winr"], p["winb"]]
    params = pltpu.CompilerParams(dimension_semantics=("arbitrary",) * 3, vmem_limit_bytes=VMEM_LIMIT)
    if compress_job is not None and compress_job[3] * compress_job[4] // PAGES_PER_STEP != b * NSA_KV_HEADS * nq:
        compress_job = None
    if compress_job is None:
        o_a, = pl.pallas_call(
            functools.partial(_nsa_prompt_kernel, n_blk, False), grid=grid, in_specs=in_specs, out_specs=out_specs,
            out_shape=out_shape, scratch_shapes=scratch, compiler_params=params, name="nsa_prompt")(*args)
        return o_a, None
    cache_t, pt_flat, pe, sb, n_pages = compress_job
    page = cache_t.shape[2]
    per_step = PAGES_PER_STEP * page // CMP_BLOCK
    per_seq = n_pages // PAGES_PER_STEP
    pool = _pool_matrix(page)
    flat = lambda bb, g, i: (bb * NSA_KV_HEADS + g) * nq + i
    gs = pltpu.PrefetchScalarGridSpec(
        num_scalar_prefetch=1, grid=grid,
        in_specs=in_specs + [pl.BlockSpec(memory_space=pl.ANY), pl.BlockSpec(pool.shape, lambda *_: (0, 0)),
                             pl.BlockSpec((CMP_BLOCK, KV_WIDTH), lambda *_: (0, 0))],
        out_specs=out_specs + [pl.BlockSpec((1, per_step, KV_WIDTH),
                                            lambda bb, g, i, *_: (flat(bb, g, i) // per_seq, flat(bb, g, i) % per_seq, 0))],
        scratch_shapes=scratch + [pltpu.VMEM((2, PAGES_PER_STEP, KV_WIDTH, page), F32), pltpu.SemaphoreType.DMA((2,))])
    return pl.pallas_call(
        functools.partial(_nsa_prompt_kernel, n_blk, True), grid_spec=gs,
        out_shape=out_shape + [jax.ShapeDtypeStruct((sb, n_pages * page // CMP_BLOCK, KV_WIDTH), F32)],
        compiler_params=params, name="nsa_prompt",
    )(pt_flat, *args, cache_t, jnp.asarray(pool, BF16), pe)


MLA_TQ = 256


def _mla_prompt_kernel(rider_steps, *refs):
    if rider_steps:
        (pt_ref, qmt_ref, k_ref, vt_ref, kn_ref, lat_ref, krt_ref, qlat_ref, qpe_ref, cnew_ref, krnew_ref,
         o_ref, olat_ref, m_ref, l_ref, acc_ref, sm_ref, sl_ref, sacc_ref, lat_buf, kr_buf, lat_sem, kr_sem) = refs
        ring = pl.program_id(0) * pl.num_programs(1) + pl.program_id(1)
        _mla_sample_step(pt_ref, lat_ref, krt_ref, qlat_ref, qpe_ref, cnew_ref, krnew_ref, olat_ref, sm_ref, sl_ref,
                         sacc_ref, lat_buf, kr_buf, lat_sem, kr_sem, ring,
                         pl.num_programs(0) * pl.num_programs(1) - 1, ring % rider_steps, rider_steps)
    else:
        qmt_ref, k_ref, vt_ref, kn_ref, o_ref, m_ref, l_ref, acc_ref = refs
    i = pl.program_id(1)
    tq, tk = MLA_TQ, MLA_TK
    heads = range(MLA_HEADS)
    m_ref[...] = jnp.full(m_ref.shape, NEG, F32)
    l_ref[...] = jnp.zeros(l_ref.shape, F32)
    acc_ref[...] = jnp.zeros(acc_ref.shape, F32)
    key_r = lax.broadcasted_iota(jnp.int32, (tk, tq), 0)
    t_k = i * tq + lax.broadcasted_iota(jnp.int32, (tk, tq), 1)
    n_chunks = ((i + 1) * tq + tk - 1) // tk
    n_past = (i * tq + 1) // tk
    ones = jnp.ones((PAD_ROWS, tk), BF16)

    def keys(c_base, h):
        return k_ref[0, pl.ds(c_base, tk), h * LANES:(h + 1) * LANES]

    kn = kn_ref[0]
    k_max = kn[0:MLA_HEADS]
    for j in range(1, kn.shape[0] // MLA_HEADS):
        k_max = jnp.maximum(k_max, kn[j * MLA_HEADS:(j + 1) * MLA_HEADS])
    bounds = []
    for h in heads:
        qf = qmt_ref[0, h * LANES:(h + 1) * LANES, :].astype(F32)
        q_norm = jnp.sqrt(jnp.sum(qf * qf, axis=0, keepdims=True))
        k_row = jnp.concatenate([k_max[h:h + 1]] * (tq // LANES), axis=1)
        bounds.append(q_norm * (k_row * 1.001) + 1e-3)
    worst = bounds[0]
    for bd in bounds[1:]:
        worst = jnp.maximum(worst, bd)
    safe = jnp.max(worst) <= SHIFT_LIMIT

    def shifted_step(c, masked):
        base = pl.multiple_of(c * tk, tk)
        old = [acc_ref[h] for h in heads]
        ps = []
        for h in heads:
            s = _dot(keys(base, h), qmt_ref[0, h * LANES:(h + 1) * LANES, :]) - bounds[h]
            ps.append(jnp.exp2(jnp.where(base + key_r <= t_k, s, NEG) if masked else s).astype(BF16))
        new = [old[h] + _dot(jnp.concatenate([vt_ref[0, c, h * MLA_V:(h + 1) * MLA_V, :], ones], axis=0), ps[h])
               for h in heads]
        for h in heads:
            acc_ref[h] = new[h]
        return 0

    def online_step(c, masked):
        base = pl.multiple_of(c * tk, tk)
        old = [(m_ref[h], l_ref[h], acc_ref[h, 0:MLA_V]) for h in heads]
        scores = []
        for h in heads:
            s = _dot(keys(base, h), qmt_ref[0, h * LANES:(h + 1) * LANES, :])
            scores.append(jnp.where(base + key_r <= t_k, s, NEG) if masked else s)
        new = [_online_update(scores[h], vt_ref[0, c, h * MLA_V:(h + 1) * MLA_V, :], *old[h]) for h in heads]
        for h in heads:
            m_ref[h], l_ref[h], acc_ref[h, 0:MLA_V] = new[h]
        return 0

    def run(step):
        lax.fori_loop(0, n_past, lambda c, z: step(c, False), 0)
        lax.fori_loop(n_past, n_chunks, lambda c, z: step(c, True), 0)

    def shifted():
        run(shifted_step)
        return tuple(acc_ref[h, 0:MLA_V] * (1.0 / acc_ref[h, MLA_V:MLA_V + 1]) for h in heads)

    def online():
        run(online_step)
        return tuple(acc_ref[h, 0:MLA_V] / l_ref[h] for h in heads)

    outs = lax.cond(safe, shifted, online)
    o_ref[0] = jnp.concatenate(outs, axis=0).T.astype(BF16)


def _mla_prompt(p, wts, b, t, sample_job=None):
    del wts
    tq, tk = MLA_TQ, MLA_TK
    assert t % tq == 0
    nq = t // tq
    in_specs = [pl.BlockSpec((1, MLA_HEADS * LANES, tq), lambda bb, i, *_: (bb, 0, i)),
                pl.BlockSpec((1, t, MLA_HEADS * LANES), lambda bb, i, *_: (bb, 0, 0)),
                pl.BlockSpec((1, t // tk, MLA_WIDTH, tk), lambda bb, i, *_: (bb, 0, 0, 0)),
                pl.BlockSpec((1,) + p["kn"].shape[1:], lambda bb, i, *_: (bb, 0, 0))]
    out_specs = [pl.BlockSpec((1, tq, MLA_WIDTH), lambda bb, i, *_: (bb, i, 0))]
    out_shape = [jax.ShapeDtypeStruct((b, t, MLA_WIDTH), BF16)]
    scratch = [pltpu.VMEM((MLA_HEADS, 1, tq), F32), pltpu.VMEM((MLA_HEADS, 1, tq), F32),
               pltpu.VMEM((MLA_HEADS, MLA_V + PAD_ROWS, tq), F32)]
    args = [p["qmt"], p["kmla"], p["vmt"], p["kn"]]
    params = pltpu.CompilerParams(dimension_semantics=("arbitrary",) * 2, vmem_limit_bytes=VMEM_LIMIT)
    per_seq = 0
    if sample_job is not None:
        sb, n_pages = sample_job[4], sample_job[5]
        if (b * nq) % sb == 0 and n_pages % ((b * nq) // sb) == 0:
            per_seq = (b * nq) // sb
    if not per_seq:
        o_b, = pl.pallas_call(functools.partial(_mla_prompt_kernel, 0), grid=(b, nq), in_specs=in_specs,
                              out_specs=out_specs, out_shape=out_shape, scratch_shapes=scratch,
                              compiler_params=params, name="mla_prompt")(*args)
        return o_b, None
    lat_cache, kr_t, pt_flat, sp, sb, n_pages = sample_job
    page = lat_cache.shape[1]
    pages = n_pages // per_seq
    seq = lambda bb, i: (bb * nq + i) // per_seq
    head = lambda w: pl.BlockSpec((1, MLA_HEADS, w), lambda bb, i, *_: (seq(bb, i), 0, 0))
    row = lambda w: pl.BlockSpec((1, 1, w), lambda bb, i, *_: (seq(bb, i), 0, 0))
    gs = pltpu.PrefetchScalarGridSpec(
        num_scalar_prefetch=1, grid=(b, nq),
        in_specs=in_specs + [pl.BlockSpec(memory_space=pl.ANY), pl.BlockSpec(memory_space=pl.ANY),
                             head(MLA_KV_LORA), head(MLA_ROPE), row(MLA_KV_LORA), row(MLA_ROPE)],
        out_specs=out_specs + [head(MLA_KV_LORA)],
        scratch_shapes=scratch + [pltpu.VMEM((PAD_ROWS, 1), F32), pltpu.VMEM((PAD_ROWS, 1), F32),
                                  pltpu.VMEM((PAD_ROWS, MLA_KV_LORA), F32),
                                  pltpu.VMEM((2, pages, page, MLA_KV_LORA), F32),
                                  pltpu.VMEM((2, pages, MLA_ROPE, page), F32),
                                  pltpu.SemaphoreType.DMA((2,)), pltpu.SemaphoreType.DMA((2,))])
    qlat = sp["qlat"].reshape(sb, MLA_HEADS, MLA_KV_LORA)
    qpe = sp["qpe"].reshape(sb, MLA_HEADS, MLA_ROPE)
    return pl.pallas_call(
        functools.partial(_mla_prompt_kernel, per_seq), grid_spec=gs,
        out_shape=out_shape + [jax.ShapeDtypeStruct((sb, MLA_HEADS, MLA_KV_LORA), F32)],
        compiler_params=params, name="mla_prompt",
    )(pt_flat, *args, lat_cache, kr_t, qlat, qpe, sp["c"], sp["kr"])


def _merge_kernel(from_latent, x_ref, oa_ref, ob_ref, ga_ref, gb_ref, ma_ref, mb_ref, wpa_ref, wpb_ref, wout_ref,
                  npost_ref, wuv_ref, y_ref):
    if from_latent:
        lat = ob_ref[0].astype(BF16)
        parts = []
        for j in range(MLA_HEADS // 2):
            parts.append(_dot(lat[:, 2 * j * MLA_KV_LORA:(2 * j + 1) * MLA_KV_LORA], wuv_ref[2 * j])
                         + _dot(lat[:, (2 * j + 1) * MLA_KV_LORA:(2 * j + 2) * MLA_KV_LORA], wuv_ref[2 * j + 1]))
        o_b = jnp.concatenate(parts, axis=1)
    else:
        o_b = ob_ref[0].astype(F32)
    ga = ga_ref[0].astype(F32)
    gb = gb_ref[0].astype(F32)
    pa = _dot((oa_ref[0].astype(F32) * (ga * jax.nn.sigmoid(ga))).astype(BF16), wpa_ref[...])
    pb = _dot((o_b * (gb * jax.nn.sigmoid(gb))).astype(BF16), wpb_ref[...])
    h = jax.nn.sigmoid(ma_ref[0].astype(F32)) * pa + jax.nn.sigmoid(mb_ref[0].astype(F32)) * pb
    z = _dot(h.astype(BF16), wout_ref[...])
    y_ref[0] = x_ref[0] + _rms(z, npost_ref[...])


def _merge(x3, o_a, o_b, p, wts, tm, from_latent):
    b, t, _ = x3.shape
    tok = lambda w: pl.BlockSpec((1, tm, w), lambda bb, i: (bb, i, 0))
    return pl.pallas_call(
        functools.partial(_merge_kernel, from_latent),
        grid=(b, t // tm),
        in_specs=[tok(D_MODEL), tok(NSA_WIDTH), tok(o_b.shape[2]), tok(NSA_WIDTH), tok(MLA_WIDTH), tok(D_MODEL),
                  tok(D_MODEL), _full_spec((NSA_WIDTH, D_MODEL)), _full_spec((MLA_WIDTH, D_MODEL)),
                  _full_spec((D_MODEL, D_MODEL)), _full_spec((1, D_MODEL)),
                  _full_spec((MLA_HEADS, MLA_KV_LORA, LANES))],
        out_specs=tok(D_MODEL),
        out_shape=jax.ShapeDtypeStruct((b, t, D_MODEL), F32),
        compiler_params=pltpu.CompilerParams(dimension_semantics=("arbitrary",) * 2, vmem_limit_bytes=VMEM_LIMIT),
        name="merge",
    )(x3, o_a, o_b, p["ga"], p["gb"], p["ma"], p["mb"], wts["wpa"], wts["wpb"], wts["wout"], wts["npost"],
      wts["wuv"])


PAGES_PER_STEP = 32
POOL_PAGES = 4


def _page_copy(source, buf_ref, sem_ref, step, slot, k):
    return pltpu.make_async_copy(source(step, k), buf_ref.at[slot, k], sem_ref.at[slot])


def _table_pages(pt_ref, cache_ref, per_step=None):
    per_step = per_step or PAGES_PER_STEP
    return lambda step, k: cache_ref.at[pt_ref[step * per_step + k]]


def _paged_pipeline(streams, step=None, last=None):
    if step is None:
        step = pl.program_id(0) * pl.num_programs(1) + pl.program_id(1)
        last = pl.num_programs(0) * pl.num_programs(1) - 1
    slot = step % 2

    @pl.when(step == 0)
    def _():
        for source, buf_ref, sem_ref in streams:
            for k in range(buf_ref.shape[1]):
                _page_copy(source, buf_ref, sem_ref, 0, 0, k).start()

    @pl.when(step < last)
    def _():
        for source, buf_ref, sem_ref in streams:
            for k in range(buf_ref.shape[1]):
                _page_copy(source, buf_ref, sem_ref, step + 1, 1 - slot, k).start()

    for source, buf_ref, sem_ref in streams:
        for k in range(buf_ref.shape[1]):
            _page_copy(source, buf_ref, sem_ref, step, slot, k).wait()
    return slot


XLU_POOL_PAGES = 16
MXU_POOL_GROUP = 8


def _compress_pages_kernel(pt_ref, cache_ref, pool_ref, pe_ref, o_ref, buf_ref, sem_ref):
    _compress_step(pt_ref, cache_ref, pool_ref, pe_ref, o_ref, buf_ref, sem_ref)


def _compress_step(pt_ref, cache_ref, pool_ref, pe_ref, o_ref, buf_ref, sem_ref, step=None, last=None):
    slot = _paged_pipeline([(_table_pages(pt_ref, cache_ref), buf_ref, sem_ref)], step, last)
    pe_sum = jnp.sum(pe_ref[...], axis=0, keepdims=True)
    pool = pool_ref[...]
    groups = [jnp.concatenate([buf_ref[slot, k + j] for j in range(MXU_POOL_GROUP)], axis=1)
              for k in range(XLU_POOL_PAGES, PAGES_PER_STEP, MXU_POOL_GROUP)]
    splits = [_split_bf16(x) for x in groups]
    by_mxu = [_dot_nt(pool, hi) + _dot_nt(pool, lo) for hi, lo in splits]
    by_xlu = []
    for k in range(XLU_POOL_PAGES):
        rows = buf_ref[slot, k].T
        by_xlu.append(jnp.sum(rows.reshape(rows.shape[0] // CMP_BLOCK, CMP_BLOCK, KV_WIDTH), axis=1))
    o_ref[0] = (jnp.concatenate(by_xlu + by_mxu, axis=0) + pe_sum) * (1.0 / CMP_BLOCK)


def _pool_matrix(page):
    pool = np.zeros((MXU_POOL_GROUP * page // CMP_BLOCK, MXU_POOL_GROUP * page), np.float32)
    for s in range(pool.shape[1]):
        pool[s // CMP_BLOCK, s] = 1.0
    return pool


def _compress_pages(cache_t, pt_flat, pe, b, n_pages):
    page = cache_t.shape[2]
    per_step = PAGES_PER_STEP * page // CMP_BLOCK
    pool = _pool_matrix(page)
    gs = pltpu.PrefetchScalarGridSpec(
        num_scalar_prefetch=1, grid=(b, n_pages // PAGES_PER_STEP),
        in_specs=[pl.BlockSpec(memory_space=pl.ANY),
                  pl.BlockSpec(pool.shape, lambda bb, c, pt: (0, 0)),
                  pl.BlockSpec((CMP_BLOCK, KV_WIDTH), lambda bb, c, pt: (0, 0))],
        out_specs=pl.BlockSpec((1, per_step, KV_WIDTH), lambda bb, c, pt: (bb, c, 0)),
        scratch_shapes=[pltpu.VMEM((2, PAGES_PER_STEP, KV_WIDTH, page), F32), pltpu.SemaphoreType.DMA((2,))])
    return pl.pallas_call(
        _compress_pages_kernel, grid_spec=gs,
        out_shape=jax.ShapeDtypeStruct((b, n_pages * page // CMP_BLOCK, KV_WIDTH), F32),
        compiler_params=pltpu.CompilerParams(dimension_semantics=("arbitrary",) * 2, vmem_limit_bytes=VMEM_LIMIT),
        name="compress_pages",
    )(pt_flat, cache_t, jnp.asarray(pool, BF16), pe)


def _pad_rows(v, rows):
    return jnp.concatenate([v, jnp.zeros((rows - v.shape[0], v.shape[1]), v.dtype)], axis=0)


def _stack_heads(qv, lo):
    a, b = qv[:, :LANES], qv[:, LANES:]
    z = jnp.zeros_like(a)
    return jnp.concatenate([jnp.where(lo, a, z), jnp.where(lo, z, a),
                            jnp.where(lo, b, z), jnp.where(lo, z, b)], axis=0)


def _sample_select_kernel(n_cmp, q_ref, kc_ref, oc_ref, idx_ref, imp_ref):
    bb = pl.program_id(0)
    lo1 = _lane_lo(1)
    lo_c = _lane_lo(n_cmp)
    q = q_ref[0]
    kc = kc_ref[0]
    even = (lax.broadcasted_iota(jnp.int32, (1, LANES), 1) & 1) == 0
    for g in range(NSA_KV_HEADS):
        kk, vv = _dup_kv(kc[:, g * LANES:(g + 1) * LANES], lo_c)
        qs = _pad_rows(_stack_heads(q[:, g * 2 * LANES:(g + 1) * 2 * LANES], lo1), PAD_ROWS)
        s = _dot_nt(qs, kk.astype(BF16))
        e = jnp.exp(s - jnp.max(s, axis=-1, keepdims=True))
        p = e / jnp.sum(e, axis=-1, keepdims=True)
        oc_ref[0, g] = _dot(p.astype(BF16), vv.astype(BF16))
        imp = p[0:1] + p[1:2] + p[2:3] + p[3:4]
        chunks = []
        for k in range(n_cmp // LANES):
            a = imp[:, k * LANES:(k + 1) * LANES]
            chunks.append(a + jnp.where(even, pltpu.roll(a, LANES - 1, 1), pltpu.roll(a, 1, 1)))
        imp_ref[pl.ds(bb * NSA_KV_HEADS + g, 1), :] = jnp.concatenate(chunks, axis=1)

    @pl.when(bb == pl.num_programs(0) - 1)
    def _():
        rows = imp_ref.shape[0]
        blk = lax.broadcasted_iota(jnp.int32, (rows, n_cmp), 1) >> 1
        blk_f = blk.astype(F32)
        slot = lax.broadcasted_iota(jnp.int32, (rows, N_SELECT), 1)
        v = jnp.where(blk == 0, -1.0, imp_ref[...])
        idx = jnp.where(slot == N_SELECT - 1, n_cmp // 2, 0)
        for k in range(1, N_SELECT - 1):
            top = jnp.max(v, axis=-1, keepdims=True)
            jmin = jnp.min(jnp.where(v == top, blk_f, float(n_cmp)), axis=-1, keepdims=True).astype(jnp.int32)
            idx = jnp.where(slot == k, jmin, idx)
            v = jnp.where(blk == jmin, -1.0, v)
        idx_ref[...] = idx


def _sample_select(q, kc_all, b):
    n_cmp = kc_all.shape[1]
    rows = b * NSA_KV_HEADS
    return pl.pallas_call(
        functools.partial(_sample_select_kernel, n_cmp),
        grid=(b,),
        in_specs=[pl.BlockSpec((1, 1, NSA_WIDTH), lambda bb: (bb, 0, 0)),
                  pl.BlockSpec((1, n_cmp, KV_WIDTH), lambda bb: (bb, 0, 0))],
        out_specs=[pl.BlockSpec((1, NSA_KV_HEADS, PAD_ROWS, LANES), lambda bb: (bb, 0, 0, 0)),
                   pl.BlockSpec((rows, N_SELECT), lambda bb: (0, 0))],
        out_shape=[jax.ShapeDtypeStruct((b, NSA_KV_HEADS, PAD_ROWS, LANES), F32),
                   jax.ShapeDtypeStruct((rows, N_SELECT), jnp.int32)],
        scratch_shapes=[pltpu.VMEM((rows, n_cmp), F32)],
        compiler_params=pltpu.CompilerParams(dimension_semantics=("arbitrary",), vmem_limit_bytes=VMEM_LIMIT),
        name="sample_select",
    )(q, kc_all)


def _extra_key_softmax(s_past, vt4_b, s_new, v_new):
    m = jnp.maximum(jnp.max(s_past, axis=-1, keepdims=True), s_new)
    e = jnp.exp(s_past - m)
    e_new = jnp.exp(s_new - m)
    den = jnp.sum(e, axis=-1, keepdims=True) + e_new
    return (_dot_nt(e.astype(BF16), vt4_b) + e_new * v_new) / den


def _sample_attend_kernel(n_pages, idx_ref, pt_ref, slc_ref, qr_ref, newkv_ref, neww_ref, newwf_ref, win_ref,
                          gate_ref, oc_ref, o_ref, wout_ref, buf_ref, sem_ref):
    n_sel = N_SELECT
    bb, g = pl.program_id(0), pl.program_id(1)

    def selected_page(step, k):
        j = idx_ref[step * n_sel + k]
        page = pt_ref[(step // NSA_KV_HEADS) * n_pages + jnp.minimum(j // 2, n_pages - 1)]
        return slc_ref.at[page, pl.ds((step % NSA_KV_HEADS) * LANES, LANES), :]

    slot = _paged_pipeline([(selected_page, buf_ref, sem_ref)])
    pages = [buf_ref.at[slot, k] for k in range(n_sel)]
    lo1 = _lane_lo(1)
    q = qr_ref[0].astype(F32)
    halves = [q[:, 0:LANES], q[:, LANES:2 * LANES]]
    rows = []
    for r in range(NSA_GROUP):
        a = halves[r // 2]
        rows.append(jnp.where(lo1, a if r % 2 == 0 else pltpu.roll(a, HALF, 1), 0.0))
    qs_f = _pad_rows(jnp.concatenate(rows, axis=0), PAD_ROWS)
    qs = qs_f.astype(BF16)
    twice_rows = lambda a: jnp.concatenate([a, a], axis=0)

    s_t = jnp.concatenate([pg[...] for pg in pages], axis=1)
    s_sel = _dot(qs, s_t.astype(BF16))
    base = (bb * NSA_KV_HEADS + g) * n_sel
    biases = []
    for k in range(n_sel - 1):
        odd = (idx_ref[base + k] & 1) == 1
        biases.append(jnp.where(lo1, jnp.where(odd, NEG, 0.0), jnp.where(odd, 0.0, NEG)))
    biases.append(jnp.full((1, LANES), NEG, F32))
    s_sel = s_sel + jnp.concatenate(biases, axis=1)
    nk = newkv_ref[0].astype(F32)
    s_new = jnp.sum(qs_f * nk[:, :LANES], axis=-1, keepdims=True)
    o_s = _extra_key_softmax(s_sel, twice_rows(s_t[HD:2 * HD]).astype(BF16), s_new, nk[:, LANES:])

    w = win_ref[0]
    nw = neww_ref[0].astype(F32)
    s_w = _dot(qs, w.astype(BF16))
    s_wn = jnp.sum(qs_f * nw[:, :LANES], axis=-1, keepdims=True)
    o_w = _extra_key_softmax(s_w, twice_rows(w[HD:2 * HD]).astype(BF16), s_wn, nw[:, LANES:])

    o_c = oc_ref[0, 0]
    gates = gate_ref[0]
    gate = jnp.where(g == 0, gates[:, 0:GATE_ROWS], gates[:, GATE_ROWS:2 * GATE_ROWS])
    heads = []
    for r in range(NSA_GROUP):
        heads.append(gate[:, 3 * r:3 * r + 1] * o_c[r:r + 1] + gate[:, 3 * r + 1:3 * r + 2] * o_s[r:r + 1]
                     + gate[:, 3 * r + 2:3 * r + 3] * o_w[r:r + 1])
    o_ref[0] = jnp.concatenate([jnp.where(lo1, heads[0], heads[1]), jnp.where(lo1, heads[2], heads[3])], axis=1)

    n_feat, n_w = w.shape
    new_row = jnp.broadcast_to(newwf_ref[0], (n_feat, n_feat))
    diag = (lax.broadcasted_iota(jnp.int32, (n_feat, n_feat), 0)
            == lax.broadcasted_iota(jnp.int32, (n_feat, n_feat), 1))
    new_col = jnp.sum(jnp.where(diag, new_row, 0.0), axis=1, keepdims=True)
    last = lax.broadcasted_iota(jnp.int32, (n_feat, LANES), 1) == LANES - 1
    chunks = []
    n_ch = n_w // LANES
    for c in range(n_ch):
        cur = pltpu.roll(w[:, c * LANES:(c + 1) * LANES], LANES - 1, 1)
        if c + 1 < n_ch:
            nxt = pltpu.roll(w[:, (c + 1) * LANES:(c + 2) * LANES], LANES - 1, 1)
        else:
            nxt = jnp.broadcast_to(new_col, (n_feat, LANES))
        chunks.append(jnp.where(last, nxt, cur))
    wout_ref[0] = jnp.concatenate(chunks, axis=1)


def _sample_attend(p, oc, idx_flat, pt_flat, slc_t, win_t, b, n_pages):
    page = slc_t.shape[2]
    win_len = win_t.shape[2]
    assert page == 2 * SEL_BLOCK
    row = lambda w: pl.BlockSpec((1, 1, w), lambda bb, g, idx, pt: (bb, 0, g))
    wspec = pl.BlockSpec((1, LANES, win_len), lambda bb, g, idx, pt: (bb, g, 0))
    gs = pltpu.PrefetchScalarGridSpec(
        num_scalar_prefetch=2, grid=(b, NSA_KV_HEADS),
        in_specs=[pl.BlockSpec(memory_space=pl.ANY),
                  row(2 * LANES), row(2 * LANES), row(2 * LANES), row(LANES), wspec,
                  pl.BlockSpec((1, 1, NSA_KV_HEADS * GATE_ROWS), lambda bb, g, idx, pt: (bb, 0, 0)),
                  pl.BlockSpec((1, 1, PAD_ROWS, LANES), lambda bb, g, idx, pt: (bb, g, 0, 0))],
        out_specs=[row(2 * LANES), wspec],
        scratch_shapes=[pltpu.VMEM((2, N_SELECT, LANES, page), F32), pltpu.SemaphoreType.DMA((2,))])
    return pl.pallas_call(
        functools.partial(_sample_attend_kernel, n_pages), grid_spec=gs,
        out_shape=[jax.ShapeDtypeStruct((b, 1, NSA_WIDTH), F32),
                   jax.ShapeDtypeStruct((b, KV_WIDTH, win_len), F32)],
        compiler_params=pltpu.CompilerParams(dimension_semantics=("arbitrary",) * 2, vmem_limit_bytes=VMEM_LIMIT),
        name="sample_attend",
    )(idx_flat, pt_flat, slc_t, p["qrot"], p["slckv"], p["winkv"], p["win"], win_t, p["gate"], oc)


def _softmax_update(s, v_b, m, l, acc):
    m_new = jnp.maximum(m, jnp.max(s, axis=-1, keepdims=True))
    p = jnp.exp(s - m_new)
    alpha = jnp.exp(m - m_new)
    return m_new, alpha * l + jnp.sum(p, axis=-1, keepdims=True), alpha * acc + _dot(p.astype(BF16), v_b)


def _mla_sample_kernel(pt_ref, lat_ref, krt_ref, qlat_ref, qpe_ref, cnew_ref, krnew_ref, o_ref,
                       m_ref, l_ref, acc_ref, lat_buf, kr_buf, lat_sem, kr_sem):
    _mla_sample_step(pt_ref, lat_ref, krt_ref, qlat_ref, qpe_ref, cnew_ref, krnew_ref, o_ref,
                     m_ref, l_ref, acc_ref, lat_buf, kr_buf, lat_sem, kr_sem)


def _mla_sample_step(pt_ref, lat_ref, krt_ref, qlat_ref, qpe_ref, cnew_ref, krnew_ref, o_ref, m_ref, l_ref, acc_ref,
                     lat_buf, kr_buf, lat_sem, kr_sem, ring_step=None, ring_last=None, step=None, n_steps=None):
    n = lat_buf.shape[1]
    slot = _paged_pipeline([(_table_pages(pt_ref, lat_ref, n), lat_buf, lat_sem),
                            (_table_pages(pt_ref, krt_ref, n), kr_buf, kr_sem)], ring_step, ring_last)
    lat_pages = [lat_buf.at[slot, k] for k in range(n)]
    kr_pages = [kr_buf.at[slot, k] for k in range(n)]
    if step is None:
        step, n_steps = pl.program_id(1), pl.num_programs(1)
    qlat = _pad_rows(qlat_ref[0], PAD_ROWS)
    qpe = _pad_rows(qpe_ref[0], PAD_ROWS)

    @pl.when(step == 0)
    def _():
        c_new = cnew_ref[0]
        s_new = (jnp.sum(qlat * c_new, axis=-1, keepdims=True)
                 + jnp.sum(qpe * krnew_ref[0], axis=-1, keepdims=True))
        m_ref[...] = s_new
        l_ref[...] = jnp.ones(l_ref.shape, F32)
        acc_ref[...] = jnp.broadcast_to(c_new, acc_ref.shape)

    qlat_b, qpe_b = qlat.astype(BF16), qpe.astype(BF16)
    groups = range(0, n, POOL_PAGES)
    c_bs = [jnp.concatenate([pg[...] for pg in lat_pages[k:k + POOL_PAGES]], axis=0).astype(BF16) for k in groups]
    kr_bs = [jnp.concatenate([pg[...] for pg in kr_pages[k:k + POOL_PAGES]], axis=1).astype(BF16) for k in groups]
    s_lat = [_dot_nt(qlat_b, c_b) for c_b in c_bs]
    s_pe = [_dot(qpe_b, kr_b) for kr_b in kr_bs]
    scores = [a + r for a, r in zip(s_lat, s_pe)]
    maxes = [jnp.max(s, axis=-1, keepdims=True) for s in scores]
    ps = [jnp.exp(s - mx) for s, mx in zip(scores, maxes)]
    sums = [jnp.sum(p_g, axis=-1, keepdims=True) for p_g in ps]
    accs = [_dot(p_g.astype(BF16), c_b) for p_g, c_b in zip(ps, c_bs)]
    m_old = m_ref[...]
    m = m_old
    for mx in maxes:
        m = jnp.maximum(m, mx)
    alpha = jnp.exp(m_old - m)
    l = alpha * l_ref[...]
    acc = alpha * acc_ref[...]
    for mx, l_g, acc_g in zip(maxes, sums, accs):
        w = jnp.exp(mx - m)
        l = l + w * l_g
        acc = acc + w * acc_g
    m_ref[...] = m
    l_ref[...] = l
    acc_ref[...] = acc

    @pl.when(step == n_steps - 1)
    def _():
        o_ref[0] = (acc / l)[0:MLA_HEADS]


def _mla_sample(p, lat_cache, kr_t, pt_flat, b, n_pages):
    page = lat_cache.shape[1]
    head = lambda w: pl.BlockSpec((1, MLA_HEADS, w), lambda bb, c, pt: (bb, 0, 0))
    row = lambda w: pl.BlockSpec((1, 1, w), lambda bb, c, pt: (bb, 0, 0))
    gs = pltpu.PrefetchScalarGridSpec(
        num_scalar_prefetch=1, grid=(b, n_pages // PAGES_PER_STEP),
        in_specs=[pl.BlockSpec(memory_space=pl.ANY), pl.BlockSpec(memory_space=pl.ANY),
                  head(MLA_KV_LORA), head(MLA_ROPE), row(MLA_KV_LORA), row(MLA_ROPE)],
        out_specs=head(MLA_KV_LORA),
        scratch_shapes=[pltpu.VMEM((PAD_ROWS, 1), F32), pltpu.VMEM((PAD_ROWS, 1), F32),
                        pltpu.VMEM((PAD_ROWS, MLA_KV_LORA), F32),
                        pltpu.VMEM((2, PAGES_PER_STEP, page, MLA_KV_LORA), F32),
                        pltpu.VMEM((2, PAGES_PER_STEP, MLA_ROPE, page), F32),
                        pltpu.SemaphoreType.DMA((2,)), pltpu.SemaphoreType.DMA((2,))])
    qlat = p["qlat"].reshape(b, MLA_HEADS, MLA_KV_LORA)
    qpe = p["qpe"].reshape(b, MLA_HEADS, MLA_ROPE)
    return pl.pallas_call(
        _mla_sample_kernel, grid_spec=gs,
        out_shape=jax.ShapeDtypeStruct((b, MLA_HEADS, MLA_KV_LORA), F32),
        compiler_params=pltpu.CompilerParams(dimension_semantics=("arbitrary",) * 2, vmem_limit_bytes=VMEM_LIMIT),
        name="mla_sample",
    )(pt_flat, lat_cache, kr_t, qlat, qpe, p["c"], p["kr"])


def _rope_angles(pos, theta, dim):
    half = dim // 2
    inv = 1.0 / (float(theta) ** (np.arange(half, dtype=np.float64) / half))
    ang = np.asarray(pos, np.float64)[:, None] * inv[None, :]
    return np.cos(ang).astype(np.float32), np.sin(ang).astype(np.float32)


def _rope_lane_tables(pos, rows):
    def table(theta, dim, period, active):
        half = dim // 2
        cos, sin = _rope_angles(pos, theta, dim)
        lane = np.arange(LANES)
        d = lane % period
        is_lo = (d < half) & active(lane)
        is_hi = (d >= half) & (d < dim) & active(lane)
        fi = np.where(d < half, d, np.clip(d - half, 0, half - 1))
        cos_l, sin_l = cos[:, fi], sin[:, fi]
        tab = np.stack([np.where(is_lo | is_hi, cos_l, 1.0), np.where(is_lo, -sin_l, 0.0),
                        np.where(is_hi, sin_l, 0.0)]).astype(np.float32)
        return jnp.asarray(np.broadcast_to(tab, (3, rows, LANES)))

    every = lambda lane: np.ones_like(lane, bool)
    keys_only = lambda lane: (lane % LANES) < HD
    return (table(ROPE_THETA, ROT_DIM, HD, every), table(ROPE_THETA, ROT_DIM, HD, keys_only),
            table(MLA_ROPE_THETA, MLA_ROPE, MLA_ROPE, every))


def _rope_row_tables(pos):
    cq, sq = _rope_angles(pos, ROPE_THETA, ROT_DIM)
    cm, sm = _rope_angles(pos, MLA_ROPE_THETA, MLA_ROPE)
    return jnp.asarray(np.stack([cq.T, sq.T])), jnp.asarray(np.stack([cm.T, sm.T]))


def _pack_weights(l, norm_pre, w_in, pe_cmp, q_norm, w_q_up, kv_norm, w_kv_up, w_proj_a, w_proj_b, w_out, norm_post):
    w = w_in[l].astype(BF16)
    o = IN_OFFSETS
    seg = lambda k: w[:, o[k]:o[k + 1]]
    gn = seg(4)
    per_group = 3 * NSA_GROUP
    gn_t = jnp.zeros((NSA_KV_HEADS * GATE_ROWS, D_MODEL), w.dtype)
    for g in range(NSA_KV_HEADS):
        gn_t = gn_t.at[g * GATE_ROWS:g * GATE_ROWS + per_group].set(gn[:, g * per_group:(g + 1) * per_group].T)
    w_t = jnp.concatenate([seg(0).T, seg(1).T, seg(2).T, seg(3).T, gn_t, seg(8).T], axis=0)
    w_krp4 = jnp.tile(seg(8).T, (LANES // MLA_ROPE, 1))
    w_cols = jnp.concatenate([seg(5), seg(6), seg(7), seg(9), seg(10), seg(11)], axis=1)
    wq = w_q_up[l]
    wqup = jnp.concatenate([wq[..., :MLA_NOPE].reshape(MLA_Q_LORA, -1), wq[..., MLA_NOPE:].reshape(MLA_Q_LORA, -1)],
                           axis=1)
    wkv = w_kv_up[l]
    wuk_pad = jnp.pad(wkv[..., :MLA_NOPE], ((0, 0), (0, 0), (0, LANES - MLA_NOPE)))
    rope_copy = jnp.pad(jnp.eye(MLA_ROPE, dtype=w.dtype), ((0, LANES - MLA_ROPE), (MLA_NOPE, MLA_ROPE)))
    wkx = jnp.concatenate([wuk_pad.reshape(MLA_KV_LORA, MLA_HEADS * LANES), jnp.tile(rope_copy, (1, MLA_HEADS))],
                          axis=0)
    wvt = jnp.transpose(wkv[..., MLA_NOPE:], (1, 2, 0)).reshape(MLA_WIDTH, MLA_KV_LORA)
    w2uk =jnp.transpose(wkv[..., :MLA_NOPE], (1, 2, 0)).reshape(MLA_HEADS // 2, LANES, MLA_KV_LORA)
    wv = jnp.transpose(wkv[..., MLA_NOPE:], (1, 0, 2))
    zeros = jnp.zeros_like(wv)
    even = (jnp.arange(MLA_HEADS) % 2 == 0)[:, None, None]
    wuv = jnp.concatenate([jnp.where(even, wv, zeros), jnp.where(even, zeros, wv)], axis=2)
    return {
        "npre": norm_pre[l][None].astype(F32), "w_t": w_t, "w_krp4": w_krp4, "w_cols": w_cols, "pe": pe_cmp[l].reshape(CMP_BLOCK, KV_WIDTH).astype(F32),
        "qnorm": q_norm[l][None].astype(F32), "wqup": wqup.astype(BF16), "wqupt": wqup.T.astype(BF16),
        "kvnorm": kv_norm[l][None].astype(F32), "w2uk": w2uk.astype(BF16), "wkx": wkx.astype(BF16),
        "wvt": wvt.astype(BF16), "wuv": wuv.astype(BF16),
        "wpa": w_proj_a[l].astype(BF16), "wpb": w_proj_b[l].astype(BF16), "wout": w_out[l].astype(BF16),
        "npost": norm_post[l][None].astype(F32),
    }


def _rows_from_cols(a):
    b, _, t = a.shape
    return a.reshape(b, NSA_KV_HEADS, 2, HD, t).transpose(0, 4, 1, 2, 3)


def _cols_from_rows(a):
    n, t = a.shape[:2]
    return a.transpose(0, 2, 3, 4, 1).reshape(n, KV_WIDTH, t)


def _prompt_layer(x, wts, compress_job, sample_job):
    b, t, _ = x.shape
    assert t % PROMPT_TM == 0
    p = _in_project_cols(x, _rope_row_tables(np.arange(t)), wts)
    o_a, kc_all = _nsa_prompt(p, b, t, compress_job)
    o_b, o_lat = _mla_prompt(p, wts, b, t, sample_job)
    y = _merge(x, o_a, o_b, p, wts, 512, from_latent=False)
    win_keep = min(WINDOW, t)
    return y, (_rows_from_cols(p["cmp"]), _rows_from_cols(p["slc"]), p["c"], p["kr"].transpose(0, 2, 1),
               _rows_from_cols(p["win"][:, :, t - win_keep:])), kc_all, o_lat


def _sample_project(x, page_table, page, wts):
    b, s_new, _ = x.shape
    assert s_new == 1
    past = page_table.shape[1] * page
    tabs = _rope_lane_tables(np.full((1,), past), b)
    p = _in_project_rows(x.reshape(1, b, D_MODEL), tabs, wts)
    return {k: v.reshape(b, 1, v.shape[-1]) for k, v in p.items()}


def _sample_layer(x, p, l, caches, state_win, page_table, wts, kc_all, o_lat):
    cache_cmp, cache_slc, cache_lat, cache_kr = caches
    b = x.shape[0]
    n_pages = page_table.shape[1]
    page = cache_cmp.shape[2]
    past = n_pages * page
    assert past % SEL_BLOCK == 0 and n_pages % PAGES_PER_STEP == 0 and page == LANES
    win_len = state_win.shape[1]
    assert win_len == WINDOW and past >= WINDOW
    pt_flat = page_table.reshape(-1).astype(jnp.int32)

    if kc_all is None:
        kc_all = _compress_pages(_cols_from_rows(cache_cmp[l]), pt_flat, wts["pe"], b, n_pages)
    oc, idx = _sample_select(p["q"], kc_all, b)
    o_a, new_win = _sample_attend(p, oc, idx.reshape(-1), pt_flat, _cols_from_rows(cache_slc[l]),
                                  _cols_from_rows(state_win), b, n_pages)
    if o_lat is None:
        o_lat = _mla_sample(p, cache_lat[l], cache_kr[l].transpose(0, 2, 1), pt_flat, b, n_pages)
    pm = {k: p[k].reshape(1, b, -1) for k in ("ga", "gb", "ma", "mb")}
    y = _merge(x.reshape(1, b, D_MODEL), o_a.reshape(1, b, NSA_WIDTH),
               o_lat.reshape(1, b, MLA_HEADS * MLA_KV_LORA), pm, wts, b, from_latent=True)
    kv6 = lambda a: a.reshape(b, 1, NSA_KV_HEADS, 2, HD)
    return y.reshape(b, 1, D_MODEL), (kv6(p["cmp"]), kv6(p["slc"]), p["c"], p["kr"], _rows_from_cols(new_win))


def kernel(x_prompt, x_sample, cache_nsa_cmp, cache_nsa_slc, cache_mla_latent, cache_mla_krope, state_nsa_win,
           page_table, norm_pre, w_in, pe_cmp, q_norm, w_q_up, kv_norm, w_kv_up, w_proj_a, w_proj_b, w_out,
           norm_post):
    depth = w_in.shape[0]
    hp, hs = x_prompt, x_sample
    new_p, new_s = [], []
    for l in range(depth):
        wts = _pack_weights(l, norm_pre, w_in, pe_cmp, q_norm, w_q_up, kv_norm, w_kv_up, w_proj_a, w_proj_b,
                            w_out, norm_post)
        n_pages = page_table.shape[1]
        compress_job = None
        if n_pages % PAGES_PER_STEP == 0:
            compress_job = (_cols_from_rows(cache_nsa_cmp[l]), page_table.reshape(-1).astype(jnp.int32), wts["pe"],
                            page_table.shape[0], n_pages)
        ps = _sample_project(hs, page_table, cache_nsa_cmp.shape[2], wts)
        sample_job = (cache_mla_latent[l], cache_mla_krope[l].transpose(0, 2, 1),
                      page_table.reshape(-1).astype(jnp.int32), ps, page_table.shape[0], n_pages)
        hp, sp, kc_all, o_lat = _prompt_layer(hp, wts, compress_job, sample_job)
        hs, ss = _sample_layer(hs, ps, l, (cache_nsa_cmp, cache_nsa_slc, cache_mla_latent, cache_mla_krope),
                               state_nsa_win[l], page_table, wts, kc_all, o_lat)
        new_p.append(sp)
        new_s.append(ss)
    stack = lambda items, k: jnp.stack([s[k] for s in items])
    return (hp, hs) + tuple(stack(new_p, k) for k in range(5)) + tuple(stack(new_s, k) for k in range(5))
```

```python
import functools

import numpy as np
import jax
import jax.numpy as jnp
from jax import lax
from jax.experimental import pallas as pl
from jax.experimental.pallas import tpu as pltpu

D_MODEL = 1024
NSA_HEADS = 8
NSA_KV_HEADS = 2
NSA_GROUP = NSA_HEADS // NSA_KV_HEADS
HD = 64
NSA_WIDTH = NSA_HEADS * HD
KV_WIDTH = NSA_KV_HEADS * 2 * HD
ROT_DIM = HD // 4
ROPE_THETA = 500000.0
CMP_BLOCK = 32
SEL_BLOCK = 64
N_SELECT = 16
WINDOW = 512
NSA_SCALE = HD ** -0.5

MLA_HEADS = 8
MLA_Q_LORA = 384
MLA_KV_LORA = 256
MLA_NOPE = 64
MLA_ROPE = 32
MLA_V = 64
MLA_WIDTH = MLA_HEADS * MLA_V
MLA_ROPE_THETA = 10000.0
MLA_SCALE = (MLA_NOPE + MLA_ROPE) ** -0.5
LOG2E = 1.4426950408889634
SHIFT_LIMIT = 40.0

RMS_EPS = 1e-6
NEG = -1e30
FORCE_SCORE = 1e4

IN_SPLITS = (NSA_WIDTH, KV_WIDTH, KV_WIDTH, KV_WIDTH, 3 * NSA_HEADS, NSA_WIDTH,
             MLA_Q_LORA, MLA_KV_LORA, MLA_ROPE, MLA_WIDTH, D_MODEL, D_MODEL)
IN_OFFSETS = tuple(int(v) for v in np.cumsum((0,) + IN_SPLITS))

LANES = 128
HALF = LANES // 2
GATE_ROWS = 16
PAD_ROWS = 16

PT_Q = (0, 512)
PT_CMP = (512, 768)
PT_SLC = (768, 1024)
PT_WIN = (1024, 1280)
PT_GN = (1280, 1280 + NSA_KV_HEADS * GATE_ROWS)
PT_KRP = (PT_GN[1], PT_GN[1] + MLA_ROPE)
PT_ROWS = PT_KRP[1]
PR_GA = (0, 512)
PR_QD = (512, 896)
PR_KVD = (896, 1152)
PR_GB = (1152, 1664)
PR_MA = (1664, 2688)
PR_MB = (2688, 3712)
PR_COLS = 3712

PROMPT_TM = 512
ATT_TQ = 256
ATT_TK = 256
MLA_TK = 256

VMEM_LIMIT = 48 * 1024 * 1024
BF16 = jnp.bfloat16
F32 = jnp.float32


def _full_spec(shape):
    nd = len(shape)
    return pl.BlockSpec(shape, lambda *_: (0,) * nd)


def _lane_lo(rows):
    return lax.broadcasted_iota(jnp.int32, (rows, LANES), 1) < HALF


def _dot(a, b):
    return jnp.dot(a, b, preferred_element_type=F32)


def _dot_nt(a, b):
    return lax.dot_general(a, b, (((1,), (1,)), ((), ())), preferred_element_type=F32)


def _rms(v, gain):
    return v * lax.rsqrt(jnp.mean(v * v, axis=-1, keepdims=True) + RMS_EPS) * gain


def _split_bf16(v):
    hi = v.astype(BF16)
    return hi, (v - hi.astype(F32)).astype(BF16)


def _rope_tiles(v, tab_ref, shift):
    c, s_lo, s_hi = tab_ref[0], tab_ref[1], tab_ref[2]
    out = []
    for k in range(v.shape[1] // LANES):
        a = v[:, k * LANES:(k + 1) * LANES]
        out.append(a * c + pltpu.roll(a, LANES - shift, 1) * s_lo + pltpu.roll(a, shift, 1) * s_hi)
    return out[0] if len(out) == 1 else jnp.concatenate(out, axis=1)


def _dup_kv(a, lo):
    r = pltpu.roll(a, HALF, 1)
    return jnp.where(lo, a, r), jnp.where(lo, r, a)


def _kv_pack(v):
    lo = _lane_lo(v.shape[0])
    parts = []
    for g in range(NSA_KV_HEADS):
        kk, vv = _dup_kv(v[:, g * LANES:(g + 1) * LANES], lo)
        parts += [kk, vv]
    return jnp.concatenate(parts, axis=1).astype(BF16)


def _inproj_rows_kernel(x_ref, npre_ref, wt_ref, wkrp_ref, w_ref, tq_ref, tkv_ref, tm_ref, qnorm_ref, wqup_ref,
                        kvnorm_ref, w2uk_ref, q_ref, qrot_ref, gate_ref, cmp_ref, slc_ref, win_ref, slckv_ref,
                        winkv_ref, ga_ref, gb_ref, ma_ref, mb_ref, qpe_ref, c_ref, kr_ref, qlat_ref):
    xb = _rms(x_ref[0], npre_ref[...]).astype(BF16)
    segt = lambda lohi: _dot_nt(xb, wt_ref[lohi[0]:lohi[1], :])
    seg = lambda lohi: _dot(xb, w_ref[:, lohi[0]:lohi[1]])

    a = segt((PT_Q[0], PT_WIN[1]))
    q = a[:, :NSA_WIDTH]
    q_ref[0] = (q * NSA_SCALE).astype(BF16)
    qrot_ref[0] = (_rope_tiles(q, tq_ref, ROT_DIM // 2) * NSA_SCALE).astype(BF16)
    cmp_ref[0] = a[:, NSA_WIDTH:NSA_WIDTH + KV_WIDTH]
    kvs = _rope_tiles(a[:, NSA_WIDTH + KV_WIDTH:NSA_WIDTH + 2 * KV_WIDTH], tkv_ref, ROT_DIM // 2)
    slc_ref[0] = kvs
    slckv_ref[0] = _kv_pack(kvs)
    kvw = _rope_tiles(a[:, NSA_WIDTH + 2 * KV_WIDTH:], tkv_ref, ROT_DIM // 2)
    win_ref[0] = kvw
    winkv_ref[0] = _kv_pack(kvw)

    gate_ref[0] = jax.nn.sigmoid(segt(PT_GN))
    ga_ref[0] = seg(PR_GA).astype(BF16)
    gb_ref[0] = seg(PR_GB).astype(BF16)
    ma_ref[0] = seg(PR_MA).astype(BF16)
    mb_ref[0] = seg(PR_MB).astype(BF16)

    qd = _rms(seg(PR_QD), qnorm_ref[...]).astype(BF16)
    qh = _dot(qd, wqup_ref[...])
    qn = qh[:, :MLA_HEADS * MLA_NOPE].astype(BF16)
    qpe_ref[0] = _rope_tiles(qh[:, MLA_HEADS * MLA_NOPE:], tm_ref, MLA_ROPE // 2) * MLA_SCALE
    c_ref[0] = _rms(seg(PR_KVD), kvnorm_ref[...])
    kr_ref[0] = _rope_tiles(_dot_nt(xb, wkrp_ref[...]), tm_ref, MLA_ROPE // 2)[:, :MLA_ROPE]

    rows = qn.shape[0]
    lo = _lane_lo(rows)
    z = jnp.zeros((rows, LANES), BF16)
    parts = []
    for j in range(MLA_HEADS // 2):
        pair = qn[:, j * LANES:(j + 1) * LANES]
        parts.append(_dot(jnp.where(lo, pair, z), w2uk_ref[j]) * MLA_SCALE)
        parts.append(_dot(jnp.where(lo, z, pair), w2uk_ref[j]) * MLA_SCALE)
    qlat_ref[0] = jnp.concatenate(parts, axis=1)


def _in_project_rows(x3, tabs, wts):
    b, t, _ = x3.shape
    tq, tkv, tmla = tabs
    row = lambda w, dt: jax.ShapeDtypeStruct((b, t, w), dt)
    names = ["q", "qrot", "gate", "cmp", "slc", "win", "slckv", "winkv", "ga", "gb", "ma", "mb", "qpe", "c", "kr",
             "qlat"]
    out_shape = [row(512, BF16), row(512, BF16), row(NSA_KV_HEADS * GATE_ROWS, F32), row(256, F32), row(256, F32),
                 row(256, F32),
                 row(512, BF16), row(512, BF16), row(512, BF16), row(512, BF16), row(1024, BF16), row(1024, BF16),
                 row(MLA_HEADS * MLA_ROPE, F32), row(MLA_KV_LORA, F32), row(MLA_ROPE, F32),
                 row(MLA_HEADS * MLA_KV_LORA, F32)]
    tok = lambda w: pl.BlockSpec((1, t, w), lambda bb: (bb, 0, 0))
    tab = _full_spec((3, t, LANES))
    in_specs = [tok(D_MODEL), _full_spec((1, D_MODEL)), _full_spec((PT_ROWS, D_MODEL)), _full_spec((LANES, D_MODEL)),
                _full_spec((D_MODEL, PR_COLS)), tab, tab, tab, _full_spec((1, MLA_Q_LORA)), _full_spec((MLA_Q_LORA, MLA_HEADS * (MLA_NOPE + MLA_ROPE))),
                _full_spec((1, MLA_KV_LORA)), _full_spec((MLA_HEADS // 2, LANES, MLA_KV_LORA))]
    res = pl.pallas_call(
        _inproj_rows_kernel, grid=(b,), in_specs=in_specs, out_specs=[tok(s.shape[2]) for s in out_shape],
        out_shape=out_shape,
        compiler_params=pltpu.CompilerParams(dimension_semantics=("arbitrary",), vmem_limit_bytes=VMEM_LIMIT),
        name="in_project_rows",
    )(x3, wts["npre"], wts["w_t"], wts["w_krp4"], wts["w_cols"], tq, tkv, tmla, wts["qnorm"], wts["wqup"],
      wts["kvnorm"], wts["w2uk"])
    return dict(zip(names, res))


def _rope_rows(x, cos, sin, half):
    x1, x2 = x[0:half], x[half:2 * half]
    parts = [x1 * cos - x2 * sin, x1 * sin + x2 * cos]
    if x.shape[0] > 2 * half:
        parts.append(x[2 * half:])
    return jnp.concatenate(parts, axis=0)


def _store_chunks(ref, v):
    tk = ref.shape[3]
    for j in range(ref.shape[1]):
        ref[0, j] = v[:, j * tk:(j + 1) * tk]


def _inproj_cols_kernel(x_ref, npre_ref, wt_ref, w_ref, ropeq_ref, ropem_ref, pe_ref, qnorm_ref, wqupt_ref,
                        kvnorm_ref, pool_ref, wkx_ref, wvt_ref,
                        qt_ref, qrt_ref, gate_ref, cmp_ref, slc_ref, win_ref, slcb_ref, winb_ref, slcr_ref,
                        winr_ref, kc_ref, ga_ref, gb_ref, ma_ref, mb_ref, qmt_ref, c_ref, kmla_ref, vmt_ref,
                        kr_ref, kn_ref):
    tm = x_ref.shape[1]
    xb = _rms(x_ref[0], npre_ref[...]).astype(BF16)
    segt = lambda lohi: _dot_nt(wt_ref[lohi[0]:lohi[1], :], xb)
    seg = lambda lohi: _dot(xb, w_ref[:, lohi[0]:lohi[1]])
    cq, sq = ropeq_ref[0], ropeq_ref[1]
    cm, sm = ropem_ref[0], ropem_ref[1]
    hq, hm = ROT_DIM // 2, MLA_ROPE // 2

    qt = segt(PT_Q)
    qt_ref[0] = (qt * (NSA_SCALE * LOG2E)).astype(BF16)
    qrt = jnp.concatenate([_rope_rows(qt[h * HD:(h + 1) * HD], cq, sq, hq) for h in range(NSA_HEADS)], axis=0)
    qrt_ref[0] = (qrt * (NSA_SCALE * LOG2E)).astype(BF16)

    cmpt = segt(PT_CMP)
    cmp_ref[0] = cmpt
    hi, lo = _split_bf16(cmpt)
    pool = pool_ref[...]
    pooled = (_dot_nt(pool, hi) + _dot_nt(pool, lo))[0:tm // CMP_BLOCK]
    kc_ref[0] = (pooled + jnp.sum(pe_ref[...], axis=0, keepdims=True)) * (1.0 / CMP_BLOCK)

    def rope_kv(v):
        parts = []
        for g in range(NSA_KV_HEADS):
            parts.append(_rope_rows(v[g * LANES:g * LANES + HD], cq, sq, hq))
            parts.append(v[g * LANES + HD:(g + 1) * LANES])
        return jnp.concatenate(parts, axis=0)

    slct = rope_kv(segt(PT_SLC))
    slc_ref[0] = slct
    _store_chunks(slcb_ref, slct.astype(BF16))
    slcr_ref[0] = slct.T.astype(BF16)
    wint = rope_kv(segt(PT_WIN))
    win_ref[0] = wint
    _store_chunks(winb_ref, wint.astype(BF16))
    winr_ref[0] = wint.T.astype(BF16)

    gate_ref[0] = jax.nn.sigmoid(segt(PT_GN))
    krt = _rope_rows(segt(PT_KRP), cm, sm, hm)
    kr_ref[0] = krt
    c = _rms(seg(PR_KVD), kvnorm_ref[...])
    c_ref[0] = c
    c_b = c.astype(BF16)
    kr_rows = jnp.concatenate([krt, jnp.zeros((LANES - MLA_ROPE, tm), F32)], axis=0).T
    ckr = jnp.concatenate([c_b, kr_rows.astype(BF16)], axis=1)
    kmla = _dot(ckr, wkx_ref[...]).astype(BF16)
    kmla_ref[0] = kmla
    kf = kmla.astype(F32)
    norms = []
    for h in range(MLA_HEADS):
        n2 = jnp.sum(kf[:, h * LANES:(h + 1) * LANES] ** 2, axis=1, keepdims=True)
        norms.append(jnp.broadcast_to(jnp.sqrt(jnp.max(n2, axis=0, keepdims=True)), (1, LANES)))
    kn_ref[0] = jnp.concatenate(norms, axis=0)
    _store_chunks(vmt_ref, _dot_nt(wvt_ref[...], c_b).astype(BF16))

    qd = _rms(seg(PR_QD), qnorm_ref[...]).astype(BF16)
    qht = _dot_nt(wqupt_ref[...], qd)
    n_nope = MLA_HEADS * MLA_NOPE
    zq = jnp.zeros((LANES - MLA_NOPE - MLA_ROPE, tm), F32)
    parts = []
    for h in range(MLA_HEADS):
        parts += [qht[h * MLA_NOPE:(h + 1) * MLA_NOPE],
                  _rope_rows(qht[n_nope + h * MLA_ROPE:n_nope + (h + 1) * MLA_ROPE], cm, sm, hm), zq]
    qmt_ref[0] = (jnp.concatenate(parts, axis=0) * (MLA_SCALE * LOG2E)).astype(BF16)

    ga_ref[0] = seg(PR_GA).astype(BF16)
    gb_ref[0] = seg(PR_GB).astype(BF16)
    ma_ref[0] = seg(PR_MA).astype(BF16)
    mb_ref[0] = seg(PR_MB).astype(BF16)


def _in_project_cols(x3, ropes, wts):
    b, t, _ = x3.shape
    tm = PROMPT_TM
    nt = t // tm
    ropeq, ropem = ropes
    pool = np.zeros((16, tm), np.float32)
    for s in range(tm):
        pool[s // CMP_BLOCK, s] = 1.0
    sds = jax.ShapeDtypeStruct
    rows = lambda w: pl.BlockSpec((1, tm, w), lambda i, bb: (bb, i, 0))
    cols = lambda w: pl.BlockSpec((1, w, tm), lambda i, bb: (bb, 0, i))
    chunk = lambda w, tk: pl.BlockSpec((1, tm // tk, w, tk), lambda i, bb: (bb, i, 0, 0))
    outs = [
        ("qt", sds((b, NSA_WIDTH, t), BF16), cols(NSA_WIDTH)),
        ("qrt", sds((b, NSA_WIDTH, t), BF16), cols(NSA_WIDTH)),
        ("gate", sds((b, NSA_KV_HEADS * GATE_ROWS, t), F32), cols(NSA_KV_HEADS * GATE_ROWS)),
        ("cmp", sds((b, KV_WIDTH, t), F32), cols(KV_WIDTH)),
        ("slc", sds((b, KV_WIDTH, t), F32), cols(KV_WIDTH)),
        ("win", sds((b, KV_WIDTH, t), F32), cols(KV_WIDTH)),
        ("slcb", sds((b, t // ATT_TK, KV_WIDTH, ATT_TK), BF16), chunk(KV_WIDTH, ATT_TK)),
        ("winb", sds((b, t // ATT_TK, KV_WIDTH, ATT_TK), BF16), chunk(KV_WIDTH, ATT_TK)),
        ("slcr", sds((b, t, KV_WIDTH), BF16), rows(KV_WIDTH)),
        ("winr", sds((b, t, KV_WIDTH), BF16), rows(KV_WIDTH)),
        ("kc", sds((b, t // CMP_BLOCK, KV_WIDTH), F32),
         pl.BlockSpec((1, tm // CMP_BLOCK, KV_WIDTH), lambda i, bb: (bb, i, 0))),
        ("ga", sds((b, t, NSA_WIDTH), BF16), rows(NSA_WIDTH)),
        ("gb", sds((b, t, MLA_WIDTH), BF16), rows(MLA_WIDTH)),
        ("ma", sds((b, t, D_MODEL), BF16), rows(D_MODEL)),
        ("mb", sds((b, t, D_MODEL), BF16), rows(D_MODEL)),
        ("qmt", sds((b, MLA_HEADS * LANES, t), BF16), cols(MLA_HEADS * LANES)),
        ("c", sds((b, t, MLA_KV_LORA), F32), rows(MLA_KV_LORA)),
        ("kmla", sds((b, t, MLA_HEADS * LANES), BF16), rows(MLA_HEADS * LANES)),
        ("vmt", sds((b, t // MLA_TK, MLA_WIDTH, MLA_TK), BF16), chunk(MLA_WIDTH, MLA_TK)),
        ("kr", sds((b, MLA_ROPE, t), F32), cols(MLA_ROPE)),
        ("kn", sds((b, nt * MLA_HEADS, LANES), F32), pl.BlockSpec((1, MLA_HEADS, LANES), lambda i, bb: (bb, i, 0))),
    ]
    rope_spec = lambda half: pl.BlockSpec((2, half, tm), lambda i, bb: (0, 0, i))
    in_specs = [rows(D_MODEL), _full_spec((1, D_MODEL)), _full_spec((PT_ROWS, D_MODEL)),
                _full_spec((D_MODEL, PR_COLS)), rope_spec(ROT_DIM // 2), rope_spec(MLA_ROPE // 2),
                _full_spec((CMP_BLOCK, KV_WIDTH)), _full_spec((1, MLA_Q_LORA)),
                _full_spec((MLA_HEADS * (MLA_NOPE + MLA_ROPE), MLA_Q_LORA)), _full_spec((1, MLA_KV_LORA)),
                _full_spec((16, tm)), _full_spec((MLA_KV_LORA + LANES, MLA_HEADS * LANES)),
                _full_spec((MLA_WIDTH, MLA_KV_LORA))]
    res = pl.pallas_call(
        _inproj_cols_kernel, grid=(nt, b), in_specs=in_specs, out_specs=[o[2] for o in outs],
        out_shape=[o[1] for o in outs],
        compiler_params=pltpu.CompilerParams(dimension_semantics=("arbitrary", "arbitrary"),
                                             vmem_limit_bytes=VMEM_LIMIT),
        name="in_project_cols",
    )(x3, wts["npre"], wts["w_t"], wts["w_cols"], ropeq, ropem, wts["pe"], wts["qnorm"], wts["wqupt"],
      wts["kvnorm"], jnp.asarray(pool, BF16), wts["wkx"], wts["wvt"])
    return dict(zip([o[0] for o in outs], res))


def _online_update(s, vt, m, l, acc):
    d, keys = vt.shape
    m_new = jnp.maximum(m, jnp.max(s, axis=0, keepdims=True))
    p = jnp.exp2(s - m_new).astype(BF16)
    alpha = jnp.exp2(m - m_new)
    pv = _dot(jnp.concatenate([vt, jnp.ones((PAD_ROWS, keys), BF16)], axis=0), p)
    return m_new, alpha * l + pv[d:d + 1], alpha * acc + pv[0:d]


def _nsa_prompt_kernel(n_blk, with_compress, *refs):
    if with_compress:
        (pt_ref, qt_ref, qrt_ref, gate_ref, kc_ref, ks_ref, kst_ref, kw_ref, kwt_ref, cache_ref, pool_ref, pe_ref,
         o_ref, kcall_ref, sb_ref, kn_ref, as_ref, aw_ref, buf_ref, sem_ref) = refs
        step = (pl.program_id(0) * pl.num_programs(1) + pl.program_id(1)) * pl.num_programs(2) + pl.program_id(2)
        last = pl.num_programs(0) * pl.num_programs(1) * pl.num_programs(2) - 1
        _compress_step(pt_ref, cache_ref, pool_ref, pe_ref, kcall_ref, buf_ref, sem_ref, step, last)
    else:
        (qt_ref, qrt_ref, gate_ref, kc_ref, ks_ref, kst_ref, kw_ref, kwt_ref, o_ref, sb_ref, kn_ref, as_ref,
         aw_ref) = refs
    i = pl.program_id(2)
    tq, tk = ATT_TQ, ATT_TK
    nl = NSA_GROUP * tq
    zq = jnp.zeros((HD, tq), BF16)

    def widen(qt):
        return jnp.concatenate([jnp.concatenate([qt[r * HD:(r + 1) * HD], zq], axis=0)
                                for r in range(NSA_GROUP)], axis=1)

    qc = widen(qt_ref[0])
    qr = widen(qrt_ref[0])

    n_cmp = 2 * n_blk
    kc = kc_ref[0]
    s_c = _dot(kc.astype(BF16), qc)
    rho = lax.broadcasted_iota(jnp.int32, (n_cmp, nl), 0)
    cmp_idx = jnp.where(rho < n_blk, 2 * rho, 2 * (rho - n_blk) + 1)
    t_l = i * tq + (lax.broadcasted_iota(jnp.int32, (n_cmp, nl), 1) & (tq - 1))
    mask_c = cmp_idx * CMP_BLOCK + (CMP_BLOCK - 1) <= t_l
    s_c = jnp.where(mask_c, s_c, NEG)
    e_c = jnp.where(mask_c, jnp.exp2(s_c - jnp.max(s_c, axis=0, keepdims=True)), 0.0)
    p_c = e_c / jnp.maximum(jnp.sum(e_c, axis=0, keepdims=True), 1e-30)
    kct = jnp.concatenate([kc, jnp.zeros((LANES - n_cmp, LANES), F32)], axis=0).T
    p_pad = jnp.concatenate([p_c, jnp.zeros((LANES - n_cmp, nl), F32)], axis=0)
    o_c = _dot(kct[HD:2 * HD].astype(BF16), p_pad.astype(BF16))

    imp = p_c[:, 0:tq]
    for r in range(1, NSA_GROUP):
        imp = imp + p_c[:, r * tq:(r + 1) * tq]
    imp_blk = imp[0:n_blk] + imp[n_blk:n_cmp]
    blk = lax.broadcasted_iota(jnp.int32, (n_blk, tq), 0)
    t_q = i * tq + lax.broadcasted_iota(jnp.int32, (n_blk, tq), 1)
    ahead_of = t_q - blk * SEL_BLOCK
    score = jnp.where(blk == 0, FORCE_SCORE,
                      jnp.where(ahead_of < 0, -FORCE_SCORE, jnp.where(ahead_of < SEL_BLOCK, FORCE_SCORE, imp_blk)))
    def count_ahead():
        rank = jnp.zeros((n_blk, tq), F32)
        for j in range(n_blk):
            other = score[j:j + 1, :]
            tie = jnp.where(blk > j, 1.0, 0.0)
            rank = rank + jnp.where(other > score, 1.0, jnp.where(other == score, tie, 0.0))
        return rank

    few_blocks = (i + 1) * tq <= N_SELECT * SEL_BLOCK
    rank = lax.cond(few_blocks, lambda: jnp.zeros((n_blk, tq), F32), count_ahead)
    sb_ref[...] = jnp.where(rank < N_SELECT, jnp.where(score > -1.0, 0.0, NEG), NEG)

    key_r = lax.broadcasted_iota(jnp.int32, (tk, tq), 0)
    t_k = i * tq + lax.broadcasted_iota(jnp.int32, (tk, tq), 1)
    n_chunks = ((i + 1) * tq + tk - 1) // tk
    c_lo = jnp.maximum(i * tq - WINDOW, 0) // tk
    per_chunk = tk // SEL_BLOCK
    tile = lambda bias: jnp.concatenate([bias] * NSA_GROUP, axis=1)
    ones = jnp.ones((PAD_ROWS, tk), BF16)

    def sel_bias(c):
        rows = [jnp.broadcast_to(sb_ref[pl.ds(c * per_chunk + j, 1), :], (SEL_BLOCK, tq)) for j in range(per_chunk)]
        return jnp.concatenate(rows, axis=0)

    def values(ref, c):
        return jnp.concatenate([ref[0, c, HD:2 * HD, :], ones], axis=0)

    qf = qrt_ref[0].astype(F32)
    q_norm2 = [jnp.sum(qf[r * HD:(r + 1) * HD] ** 2, axis=0, keepdims=True) for r in range(NSA_GROUP)]
    q_norm = jnp.sqrt(jnp.maximum(jnp.maximum(q_norm2[0], q_norm2[1]), jnp.maximum(q_norm2[2], q_norm2[3])))

    def key_norm(ref):
        kf = ref[0, :, 0:HD, :].astype(F32)
        n2 = jnp.max(jnp.sum(kf * kf, axis=1, keepdims=True), axis=0)
        return jnp.broadcast_to(jnp.sqrt(jnp.max(n2, axis=1, keepdims=True)), (1, LANES))

    @pl.when(i == 0)
    def _():
        kn_ref[0:1, :] = key_norm(kst_ref)
        kn_ref[1:2, :] = key_norm(kwt_ref)

    wide = lambda row: jnp.concatenate([row] * (tq // LANES), axis=1)
    bound_s = q_norm * (wide(kn_ref[0:1, :]) * 1.001) + 1e-3
    bound_w = q_norm * (wide(kn_ref[1:2, :]) * 1.001) + 1e-3
    safe = jnp.maximum(jnp.max(bound_s), jnp.max(bound_w)) <= SHIFT_LIMIT

    def attend_shifted():
        def past_probs(c):
            base = pl.multiple_of(c * tk, tk)
            s = _dot(ks_ref[0, pl.ds(base, tk), :], qr) + tile(sel_bias(c) - bound_s)
            return jnp.exp2(s).astype(BF16)

        def past_body(c, z):
            as_ref[...] += _dot(values(kst_ref, c), past_probs(c))
            return z

        def past_pair(cc, z):
            p0, p1 = past_probs(2 * cc), past_probs(2 * cc + 1)
            as_ref[...] += _dot(values(kst_ref, 2 * cc), p0) + _dot(values(kst_ref, 2 * cc + 1), p1)
            return z

        def near_probs(c):
            base = pl.multiple_of(c * tk, tk)
            s = _dot(ks_ref[0, pl.ds(base, tk), :], qr)
            w = _dot(kw_ref[0, pl.ds(base, tk), :], qr)
            dist = t_k - (base + key_r)
            causal = jnp.where(dist >= 0, 0.0, NEG)
            p_s = jnp.exp2(s + tile(sel_bias(c) + causal - bound_s)).astype(BF16)
            p_w = jnp.exp2(w + tile(jnp.where(dist <= WINDOW, causal, NEG) - bound_w)).astype(BF16)
            return p_s, p_w

        def near_body(c, z):
            p_s, p_w = near_probs(c)
            d_s, d_w = _dot(values(kst_ref, c), p_s), _dot(values(kwt_ref, c), p_w)
            as_ref[...] += d_s
            aw_ref[...] += d_w
            return z

        def near_pair(cc, z):
            c0 = c_lo + 2 * cc
            (ps0, pw0), (ps1, pw1) = near_probs(c0), near_probs(c0 + 1)
            d_s = _dot(values(kst_ref, c0), ps0) + _dot(values(kst_ref, c0 + 1), ps1)
            d_w = _dot(values(kwt_ref, c0), pw0) + _dot(values(kwt_ref, c0 + 1), pw1)
            as_ref[...] += d_s
            aw_ref[...] += d_w
            return z

        as_ref[...] = jnp.zeros(as_ref.shape, F32)
        aw_ref[...] = jnp.zeros(aw_ref.shape, F32)
        lax.fori_loop(0, c_lo // 2, past_pair, 0)
        lax.fori_loop(2 * (c_lo // 2), c_lo, past_body, 0)
        n_pairs = (n_chunks - c_lo) // 2
        lax.fori_loop(0, n_pairs, near_pair, 0)
        lax.fori_loop(c_lo + 2 * n_pairs, n_chunks, near_body, 0)
        inv_s, inv_w = 1.0 / as_ref[HD:HD + 1, :], 1.0 / aw_ref[HD:HD + 1, :]
        return as_ref[0:HD, :] * inv_s, aw_ref[0:HD, :] * inv_w

    def attend_online():
        init = (jnp.full((1, nl), NEG, F32), jnp.zeros((1, nl), F32), jnp.zeros((HD, nl), F32))

        def past_body(c, carry):
            base = pl.multiple_of(c * tk, tk)
            s = _dot(ks_ref[0, pl.ds(base, tk), :], qr) + tile(sel_bias(c))
            return _online_update(s, kst_ref[0, c, HD:2 * HD, :], *carry)

        def near_body(c, carry):
            base = pl.multiple_of(c * tk, tk)
            s = _dot(ks_ref[0, pl.ds(base, tk), :], qr)
            w = _dot(kw_ref[0, pl.ds(base, tk), :], qr)
            dist = t_k - (base + key_r)
            causal = jnp.where(dist >= 0, 0.0, NEG)
            s = s + tile(sel_bias(c) + causal)
            w = w + tile(jnp.where(dist <= WINDOW, causal, NEG))
            return (_online_update(s, kst_ref[0, c, HD:2 * HD, :], *carry[:3])
                    + _online_update(w, kwt_ref[0, c, HD:2 * HD, :], *carry[3:]))

        far = lax.fori_loop(0, c_lo, past_body, init)
        _, l_s, a_s, _, l_w, a_w = lax.fori_loop(c_lo, n_chunks, near_body, far + init)
        return a_s / l_s, a_w / l_w

    o_s, o_w = lax.cond(safe, attend_shifted, attend_online)

    gate = gate_ref[0]
    heads = []
    for r in range(NSA_GROUP):
        sl = slice(r * tq, (r + 1) * tq)
        heads.append(gate[3 * r:3 * r + 1] * o_c[:, sl] + gate[3 * r + 1:3 * r + 2] * o_s[:, sl]
                     + gate[3 * r + 2:3 * r + 3] * o_w[:, sl])
    o_ref[0] = jnp.concatenate(heads, axis=0).T.astype(BF16)


def _nsa_prompt(p, b, t, compress_job=None):
    tq, tk = ATT_TQ, ATT_TK
    n_blk = t // SEL_BLOCK
    assert 2 * n_blk <= LANES and t % tk == 0
    kc = p["kc"].reshape(b, n_blk, 2, KV_WIDTH).transpose(0, 2, 1, 3).reshape(b, 2 * n_blk, KV_WIDTH)
    nq = t // tq
    grid = (b, NSA_KV_HEADS, nq)
    qspec = pl.BlockSpec((1, NSA_GROUP * HD, tq), lambda bb, g, i, *_: (bb, g, i))
    rm = pl.BlockSpec((1, t, LANES), lambda bb, g, i, *_: (bb, 0, g))
    fm = pl.BlockSpec((1, t // tk, LANES, tk), lambda bb, g, i, *_: (bb, 0, g, 0))
    in_specs = [qspec, qspec, pl.BlockSpec((1, GATE_ROWS, tq), lambda bb, g, i, *_: (bb, g, i)),
                pl.BlockSpec((1, 2 * n_blk, LANES), lambda bb, g, i, *_: (bb, 0, g)), rm, fm, rm, fm]
    out_specs = [pl.BlockSpec((1, tq, NSA_GROUP * HD), lambda bb, g, i, *_: (bb, i, g))]
    out_shape = [jax.ShapeDtypeStruct((b, t, NSA_WIDTH), BF16)]
    acc_shape = (HD + PAD_ROWS, NSA_GROUP * tq)
    scratch = [pltpu.VMEM((n_blk, tq), F32), pltpu.VMEM((8, LANES), F32), pltpu.VMEM(acc_shape, F32),
               pltpu.VMEM(acc_shape, F32)]
    args = [p["qt"], p["qrt"], p["gate"], kc, p["slcr"], p["slcb"], p["winr"], p["winb"]]
    params = pltpu.CompilerParams(dimension_semantics=("arbitrary",) * 3, vmem_limit_bytes=VMEM_LIMIT)
    if compress_job is not None and compress_job[3] * compress_job[4] // PAGES_PER_STEP != b * NSA_KV_HEADS * nq:
        compress_job = None
    if compress_job is None:
        o_a, = pl.pallas_call(
            functools.partial(_nsa_prompt_kernel, n_blk, False), grid=grid, in_specs=in_specs, out_specs=out_specs,
            out_shape=out_shape, scratch_shapes=scratch, compiler_params=params, name="nsa_prompt")(*args)
        return o_a, None
    cache_t, pt_flat, pe, sb, n_pages = compress_job
    page = cache_t.shape[2]
    per_step = PAGES_PER_STEP * page // CMP_BLOCK
    per_seq = n_pages // PAGES_PER_STEP
    pool = _pool_matrix(page)
    flat = lambda bb, g, i: (bb * NSA_KV_HEADS + g) * nq + i
    gs = pltpu.PrefetchScalarGridSpec(
        num_scalar_prefetch=1, grid=grid,
        in_specs=in_specs + [pl.BlockSpec(memory_space=pl.ANY), pl.BlockSpec(pool.shape, lambda *_: (0, 0)),
                             pl.BlockSpec((CMP_BLOCK, KV_WIDTH), lambda *_: (0, 0))],
        out_specs=out_specs + [pl.BlockSpec((1, per_step, KV_WIDTH),
                                            lambda bb, g, i, *_: (flat(bb, g, i) // per_seq, flat(bb, g, i) % per_seq, 0))],
        scratch_shapes=scratch + [pltpu.VMEM((2, PAGES_PER_STEP, KV_WIDTH, page), F32), pltpu.SemaphoreType.DMA((2,))])
    return pl.pallas_call(
        functools.partial(_nsa_prompt_kernel, n_blk, True), grid_spec=gs,
        out_shape=out_shape + [jax.ShapeDtypeStruct((sb, n_pages * page // CMP_BLOCK, KV_WIDTH), F32)],
        compiler_params=params, name="nsa_prompt",
    )(pt_flat, *args, cache_t, jnp.asarray(pool, BF16), pe)


MLA_TQ = 256


def _mla_prompt_kernel(rider_steps, *refs):
    if rider_steps:
        (pt_ref, qmt_ref, k_ref, vt_ref, kn_ref, lat_ref, krt_ref, qlat_ref, qpe_ref, cnew_ref, krnew_ref,
         o_ref, olat_ref, m_ref, l_ref, acc_ref, sm_ref, sl_ref, sacc_ref, lat_buf, kr_buf, lat_sem, kr_sem) = refs
        ring = pl.program_id(0) * pl.num_programs(1) + pl.program_id(1)
        _mla_sample_step(pt_ref, lat_ref, krt_ref, qlat_ref, qpe_ref, cnew_ref, krnew_ref, olat_ref, sm_ref, sl_ref,
                         sacc_ref, lat_buf, kr_buf, lat_sem, kr_sem, ring,
                         pl.num_programs(0) * pl.num_programs(1) - 1, ring % rider_steps, rider_steps)
    else:
        qmt_ref, k_ref, vt_ref, kn_ref, o_ref, m_ref, l_ref, acc_ref = refs
    i = pl.program_id(1)
    tq, tk = MLA_TQ, MLA_TK
    heads = range(MLA_HEADS)
    m_ref[...] = jnp.full(m_ref.shape, NEG, F32)
    l_ref[...] = jnp.zeros(l_ref.shape, F32)
    acc_ref[...] = jnp.zeros(acc_ref.shape, F32)
    key_r = lax.broadcasted_iota(jnp.int32, (tk, tq), 0)
    t_k = i * tq + lax.broadcasted_iota(jnp.int32, (tk, tq), 1)
    n_chunks = ((i + 1) * tq + tk - 1) // tk
    n_past = (i * tq + 1) // tk
    ones = jnp.ones((PAD_ROWS, tk), BF16)

    def keys(c_base, h):
        return k_ref[0, pl.ds(c_base, tk), h * LANES:(h + 1) * LANES]

    kn = kn_ref[0]
    k_max = kn[0:MLA_HEADS]
    for j in range(1, kn.shape[0] // MLA_HEADS):
        k_max = jnp.maximum(k_max, kn[j * MLA_HEADS:(j + 1) * MLA_HEADS])
    bounds = []
    for h in heads:
        qf = qmt_ref[0, h * LANES:(h + 1) * LANES, :].astype(F32)
        q_norm = jnp.sqrt(jnp.sum(qf * qf, axis=0, keepdims=True))
        k_row = jnp.concatenate([k_max[h:h + 1]] * (tq // LANES), axis=1)
        bounds.append(q_norm * (k_row * 1.001) + 1e-3)
    worst = bounds[0]
    for bd in bounds[1:]:
        worst = jnp.maximum(worst, bd)
    safe = jnp.max(worst) <= SHIFT_LIMIT

    def shifted_step(c, masked):
        base = pl.multiple_of(c * tk, tk)
        old = [acc_ref[h] for h in heads]
        ps = []
        for h in heads:
            s = _dot(keys(base, h), qmt_ref[0, h * LANES:(h + 1) * LANES, :]) - bounds[h]
            ps.append(jnp.exp2(jnp.where(base + key_r <= t_k, s, NEG) if masked else s).astype(BF16))
        new = [old[h] + _dot(jnp.concatenate([vt_ref[0, c, h * MLA_V:(h + 1) * MLA_V, :], ones], axis=0), ps[h])
               for h in heads]
        for h in heads:
            acc_ref[h] = new[h]
        return 0

    def online_step(c, masked):
        base = pl.multiple_of(c * tk, tk)
        old = [(m_ref[h], l_ref[h], acc_ref[h, 0:MLA_V]) for h in heads]
        scores = []
        for h in heads:
            s = _dot(keys(base, h), qmt_ref[0, h * LANES:(h + 1) * LANES, :])
            scores.append(jnp.where(base + key_r <= t_k, s, NEG) if masked else s)
        new = [_online_update(scores[h], vt_ref[0, c, h * MLA_V:(h + 1) * MLA_V, :], *old[h]) for h in heads]
        for h in heads:
            m_ref[h], l_ref[h], acc_ref[h, 0:MLA_V] = new[h]
        return 0

    def run(step):
        lax.fori_loop(0, n_past, lambda c, z: step(c, False), 0)
        lax.fori_loop(n_past, n_chunks, lambda c, z: step(c, True), 0)

    def shifted():
        run(shifted_step)
        return tuple(acc_ref[h, 0:MLA_V] * (1.0 / acc_ref[h, MLA_V:MLA_V + 1]) for h in heads)

    def online():
        run(online_step)
        return tuple(acc_ref[h, 0:MLA_V] / l_ref[h] for h in heads)

    outs = lax.cond(safe, shifted, online)
    o_ref[0] = jnp.concatenate(outs, axis=0).T.astype(BF16)


def _mla_prompt(p, wts, b, t, sample_job=None):
    del wts
    tq, tk = MLA_TQ, MLA_TK
    assert t % tq == 0
    nq = t // tq
    in_specs = [pl.BlockSpec((1, MLA_HEADS * LANES, tq), lambda bb, i, *_: (bb, 0, i)),
                pl.BlockSpec((1, t, MLA_HEADS * LANES), lambda bb, i, *_: (bb, 0, 0)),
                pl.BlockSpec((1, t // tk, MLA_WIDTH, tk), lambda bb, i, *_: (bb, 0, 0, 0)),
                pl.BlockSpec((1,) + p["kn"].shape[1:], lambda bb, i, *_: (bb, 0, 0))]
    out_specs = [pl.BlockSpec((1, tq, MLA_WIDTH), lambda bb, i, *_: (bb, i, 0))]
    out_shape = [jax.ShapeDtypeStruct((b, t, MLA_WIDTH), BF16)]
    scratch = [pltpu.VMEM((MLA_HEADS, 1, tq), F32), pltpu.VMEM((MLA_HEADS, 1, tq), F32),
               pltpu.VMEM((MLA_HEADS, MLA_V + PAD_ROWS, tq), F32)]
    args = [p["qmt"], p["kmla"], p["vmt"], p["kn"]]
    params = pltpu.CompilerParams(dimension_semantics=("arbitrary",) * 2, vmem_limit_bytes=VMEM_LIMIT)
    per_seq = 0
    if sample_job is not None:
        sb, n_pages = sample_job[4], sample_job[5]
        if (b * nq) % sb == 0 and n_pages % ((b * nq) // sb) == 0:
            per_seq = (b * nq) // sb
    if not per_seq:
        o_b, = pl.pallas_call(functools.partial(_mla_prompt_kernel, 0), grid=(b, nq), in_specs=in_specs,
                              out_specs=out_specs, out_shape=out_shape, scratch_shapes=scratch,
                              compiler_params=params, name="mla_prompt")(*args)
        return o_b, None
    lat_cache, kr_t, pt_flat, sp, sb, n_pages = sample_job
    page = lat_cache.shape[1]
    pages = n_pages // per_seq
    seq = lambda bb, i: (bb * nq + i) // per_seq
    head = lambda w: pl.BlockSpec((1, MLA_HEADS, w), lambda bb, i, *_: (seq(bb, i), 0, 0))
    row = lambda w: pl.BlockSpec((1, 1, w), lambda bb, i, *_: (seq(bb, i), 0, 0))
    gs = pltpu.PrefetchScalarGridSpec(
        num_scalar_prefetch=1, grid=(b, nq),
        in_specs=in_specs + [pl.BlockSpec(memory_space=pl.ANY), pl.BlockSpec(memory_space=pl.ANY),
                             head(MLA_KV_LORA), head(MLA_ROPE), row(MLA_KV_LORA), row(MLA_ROPE)],
        out_specs=out_specs + [head(MLA_KV_LORA)],
        scratch_shapes=scratch + [pltpu.VMEM((PAD_ROWS, 1), F32), pltpu.VMEM((PAD_ROWS, 1), F32),
                                  pltpu.VMEM((PAD_ROWS, MLA_KV_LORA), F32),
                                  pltpu.VMEM((2, pages, page, MLA_KV_LORA), F32),
                                  pltpu.VMEM((2, pages, MLA_ROPE, page), F32),
                                  pltpu.SemaphoreType.DMA((2,)), pltpu.SemaphoreType.DMA((2,))])
    qlat = sp["qlat"].reshape(sb, MLA_HEADS, MLA_KV_LORA)
    qpe = sp["qpe"].reshape(sb, MLA_HEADS, MLA_ROPE)
    return pl.pallas_call(
        functools.partial(_mla_prompt_kernel, per_seq), grid_spec=gs,
        out_shape=out_shape + [jax.ShapeDtypeStruct((sb, MLA_HEADS, MLA_KV_LORA), F32)],
        compiler_params=params, name="mla_prompt",
    )(pt_flat, *args, lat_cache, kr_t, qlat, qpe, sp["c"], sp["kr"])


def _merge_kernel(from_latent, x_ref, oa_ref, ob_ref, ga_ref, gb_ref, ma_ref, mb_ref, wpa_ref, wpb_ref, wout_ref,
                  npost_ref, wuv_ref, y_ref):
    if from_latent:
        lat = ob_ref[0].astype(BF16)
        parts = []
        for j in range(MLA_HEADS // 2):
            parts.append(_dot(lat[:, 2 * j * MLA_KV_LORA:(2 * j + 1) * MLA_KV_LORA], wuv_ref[2 * j])
                         + _dot(lat[:, (2 * j + 1) * MLA_KV_LORA:(2 * j + 2) * MLA_KV_LORA], wuv_ref[2 * j + 1]))
        o_b = jnp.concatenate(parts, axis=1)
    else:
        o_b = ob_ref[0].astype(F32)
    ga = ga_ref[0].astype(F32)
    gb = gb_ref[0].astype(F32)
    pa = _dot((oa_ref[0].astype(F32) * (ga * jax.nn.sigmoid(ga))).astype(BF16), wpa_ref[...])
    pb = _dot((o_b * (gb * jax.nn.sigmoid(gb))).astype(BF16), wpb_ref[...])
    h = jax.nn.sigmoid(ma_ref[0].astype(F32)) * pa + jax.nn.sigmoid(mb_ref[0].astype(F32)) * pb
    z = _dot(h.astype(BF16), wout_ref[...])
    y_ref[0] = x_ref[0] + _rms(z, npost_ref[...])


def _merge(x3, o_a, o_b, p, wts, tm, from_latent):
    b, t, _ = x3.shape
    tok = lambda w: pl.BlockSpec((1, tm, w), lambda bb, i: (bb, i, 0))
    return pl.pallas_call(
        functools.partial(_merge_kernel, from_latent),
        grid=(b, t // tm),
        in_specs=[tok(D_MODEL), tok(NSA_WIDTH), tok(o_b.shape[2]), tok(NSA_WIDTH), tok(MLA_WIDTH), tok(D_MODEL),
                  tok(D_MODEL), _full_spec((NSA_WIDTH, D_MODEL)), _full_spec((MLA_WIDTH, D_MODEL)),
                  _full_spec((D_MODEL, D_MODEL)), _full_spec((1, D_MODEL)),
                  _full_spec((MLA_HEADS, MLA_KV_LORA, LANES))],
        out_specs=tok(D_MODEL),
        out_shape=jax.ShapeDtypeStruct((b, t, D_MODEL), F32),
        compiler_params=pltpu.CompilerParams(dimension_semantics=("arbitrary",) * 2, vmem_limit_bytes=VMEM_LIMIT),
        name="merge",
    )(x3, o_a, o_b, p["ga"], p["gb"], p["ma"], p["mb"], wts["wpa"], wts["wpb"], wts["wout"], wts["npost"],
      wts["wuv"])


PAGES_PER_STEP = 32
POOL_PAGES = 4


def _page_copy(source, buf_ref, sem_ref, step, slot, k):
    return pltpu.make_async_copy(source(step, k), buf_ref.at[slot, k], sem_ref.at[slot])


def _table_pages(pt_ref, cache_ref, per_step=None):
    per_step = per_step or PAGES_PER_STEP
    return lambda step, k: cache_ref.at[pt_ref[step * per_step + k]]


def _paged_pipeline(streams, step=None, last=None):
    if step is None:
        step = pl.program_id(0) * pl.num_programs(1) + pl.program_id(1)
        last = pl.num_programs(0) * pl.num_programs(1) - 1
    slot = step % 2

    @pl.when(step == 0)
    def _():
        for source, buf_ref, sem_ref in streams:
            for k in range(buf_ref.shape[1]):
                _page_copy(source, buf_ref, sem_ref, 0, 0, k).start(priority=k % 2)

    @pl.when(step < last)
    def _():
        for source, buf_ref, sem_ref in streams:
            for k in range(buf_ref.shape[1]):
                _page_copy(source, buf_ref, sem_ref, step + 1, 1 - slot, k).start(priority=k % 2)

    for source, buf_ref, sem_ref in streams:
        for k in range(buf_ref.shape[1]):
            _page_copy(source, buf_ref, sem_ref, step, slot, k).wait()
    return slot


XLU_POOL_PAGES = 16
MXU_POOL_GROUP = 8


def _compress_pages_kernel(pt_ref, cache_ref, pool_ref, pe_ref, o_ref, buf_ref, sem_ref):
    _compress_step(pt_ref, cache_ref, pool_ref, pe_ref, o_ref, buf_ref, sem_ref)


def _compress_step(pt_ref, cache_ref, pool_ref, pe_ref, o_ref, buf_ref, sem_ref, step=None, last=None):
    slot = _paged_pipeline([(_table_pages(pt_ref, cache_ref), buf_ref, sem_ref)], step, last)
    pe_sum = jnp.sum(pe_ref[...], axis=0, keepdims=True)
    pool = pool_ref[...]
    groups = [jnp.concatenate([buf_ref[slot, k + j] for j in range(MXU_POOL_GROUP)], axis=1)
              for k in range(XLU_POOL_PAGES, PAGES_PER_STEP, MXU_POOL_GROUP)]
    splits = [_split_bf16(x) for x in groups]
    by_mxu = [_dot_nt(pool, hi) + _dot_nt(pool, lo) for hi, lo in splits]
    by_xlu = []
    for k in range(XLU_POOL_PAGES):
        rows = buf_ref[slot, k].T
        by_xlu.append(jnp.sum(rows.reshape(rows.shape[0] // CMP_BLOCK, CMP_BLOCK, KV_WIDTH), axis=1))
    o_ref[0] = (jnp.concatenate(by_xlu + by_mxu, axis=0) + pe_sum) * (1.0 / CMP_BLOCK)


def _pool_matrix(page):
    pool = np.zeros((MXU_POOL_GROUP * page // CMP_BLOCK, MXU_POOL_GROUP * page), np.float32)
    for s in range(pool.shape[1]):
        pool[s // CMP_BLOCK, s] = 1.0
    return pool


def _compress_pages(cache_t, pt_flat, pe, b, n_pages):
    page = cache_t.shape[2]
    per_step = PAGES_PER_STEP * page // CMP_BLOCK
    pool = _pool_matrix(page)
    gs = pltpu.PrefetchScalarGridSpec(
        num_scalar_prefetch=1, grid=(b, n_pages // PAGES_PER_STEP),
        in_specs=[pl.BlockSpec(memory_space=pl.ANY),
                  pl.BlockSpec(pool.shape, lambda bb, c, pt: (0, 0)),
                  pl.BlockSpec((CMP_BLOCK, KV_WIDTH), lambda bb, c, pt: (0, 0))],
        out_specs=pl.BlockSpec((1, per_step, KV_WIDTH), lambda bb, c, pt: (bb, c, 0)),
        scratch_shapes=[pltpu.VMEM((2, PAGES_PER_STEP, KV_WIDTH, page), F32), pltpu.SemaphoreType.DMA((2,))])
    return pl.pallas_call(
        _compress_pages_kernel, grid_spec=gs,
        out_shape=jax.ShapeDtypeStruct((b, n_pages * page // CMP_BLOCK, KV_WIDTH), F32),
        compiler_params=pltpu.CompilerParams(dimension_semantics=("arbitrary",) * 2, vmem_limit_bytes=VMEM_LIMIT),
        name="compress_pages",
    )(pt_flat, cache_t, jnp.asarray(pool, BF16), pe)


def _pad_rows(v, rows):
    return jnp.concatenate([v, jnp.zeros((rows - v.shape[0], v.shape[1]), v.dtype)], axis=0)


def _stack_heads(qv, lo):
    a, b = qv[:, :LANES], qv[:, LANES:]
    z = jnp.zeros_like(a)
    return jnp.concatenate([jnp.where(lo, a, z), jnp.where(lo, z, a),
                            jnp.where(lo, b, z), jnp.where(lo, z, b)], axis=0)


def _sample_select_kernel(n_cmp, q_ref, kc_ref, oc_ref, idx_ref, imp_ref):
    bb = pl.program_id(0)
    lo1 = _lane_lo(1)
    lo_c = _lane_lo(n_cmp)
    q = q_ref[0]
    kc = kc_ref[0]
    even = (lax.broadcasted_iota(jnp.int32, (1, LANES), 1) & 1) == 0
    for g in range(NSA_KV_HEADS):
        kk, vv = _dup_kv(kc[:, g * LANES:(g + 1) * LANES], lo_c)
        qs = _pad_rows(_stack_heads(q[:, g * 2 * LANES:(g + 1) * 2 * LANES], lo1), PAD_ROWS)
        s = _dot_nt(qs, kk.astype(BF16))
        e = jnp.exp(s - jnp.max(s, axis=-1, keepdims=True))
        p = e / jnp.sum(e, axis=-1, keepdims=True)
        oc_ref[0, g] = _dot(p.astype(BF16), vv.astype(BF16))
        imp = p[0:1] + p[1:2] + p[2:3] + p[3:4]
        chunks = []
        for k in range(n_cmp // LANES):
            a = imp[:, k * LANES:(k + 1) * LANES]
            chunks.append(a + jnp.where(even, pltpu.roll(a, LANES - 1, 1), pltpu.roll(a, 1, 1)))
        imp_ref[pl.ds(bb * NSA_KV_HEADS + g, 1), :] = jnp.concatenate(chunks, axis=1)

    @pl.when(bb == pl.num_programs(0) - 1)
    def _():
        rows = imp_ref.shape[0]
        blk = lax.broadcasted_iota(jnp.int32, (rows, n_cmp), 1) >> 1
        blk_f = blk.astype(F32)
        slot = lax.broadcasted_iota(jnp.int32, (rows, N_SELECT), 1)
        v = jnp.where(blk == 0, -1.0, imp_ref[...])
        idx = jnp.where(slot == N_SELECT - 1, n_cmp // 2, 0)
        for k in range(1, N_SELECT - 1):
            top = jnp.max(v, axis=-1, keepdims=True)
            jmin = jnp.min(jnp.where(v == top, blk_f, float(n_cmp)), axis=-1, keepdims=True).astype(jnp.int32)
            idx = jnp.where(slot == k, jmin, idx)
            v = jnp.where(blk == jmin, -1.0, v)
        idx_ref[...] = idx


def _sample_select(q, kc_all, b):
    n_cmp = kc_all.shape[1]
    rows = b * NSA_KV_HEADS
    return pl.pallas_call(
        functools.partial(_sample_select_kernel, n_cmp),
        grid=(b,),
        in_specs=[pl.BlockSpec((1, 1, NSA_WIDTH), lambda bb: (bb, 0, 0)),
                  pl.BlockSpec((1, n_cmp, KV_WIDTH), lambda bb: (bb, 0, 0))],
        out_specs=[pl.BlockSpec((1, NSA_KV_HEADS, PAD_ROWS, LANES), lambda bb: (bb, 0, 0, 0)),
                   pl.BlockSpec((rows, N_SELECT), lambda bb: (0, 0))],
        out_shape=[jax.ShapeDtypeStruct((b, NSA_KV_HEADS, PAD_ROWS, LANES), F32),
                   jax.ShapeDtypeStruct((rows, N_SELECT), jnp.int32)],
        scratch_shapes=[pltpu.VMEM((rows, n_cmp), F32)],
        compiler_params=pltpu.CompilerParams(dimension_semantics=("arbitrary",), vmem_limit_bytes=VMEM_LIMIT),
        name="sample_select",
    )(q, kc_all)


def _extra_key_softmax(s_past, vt4_b, s_new, v_new):
    m = jnp.maximum(jnp.max(s_past, axis=-1, keepdims=True), s_new)
    e = jnp.exp(s_past - m)
    e_new = jnp.exp(s_new - m)
    den = jnp.sum(e, axis=-1, keepdims=True) + e_new
    return (_dot_nt(e.astype(BF16), vt4_b) + e_new * v_new) / den


def _sample_attend_kernel(n_pages, idx_ref, pt_ref, slc_ref, qr_ref, newkv_ref, neww_ref, newwf_ref, win_ref,
                          gate_ref, oc_ref, o_ref, wout_ref, buf_ref, sem_ref):
    n_sel = N_SELECT
    bb, g = pl.program_id(0), pl.program_id(1)

    def selected_page(step, k):
        j = idx_ref[step * n_sel + k]
        page = pt_ref[(step // NSA_KV_HEADS) * n_pages + jnp.minimum(j // 2, n_pages - 1)]
        return slc_ref.at[page, pl.ds((step % NSA_KV_HEADS) * LANES, LANES), :]

    slot = _paged_pipeline([(selected_page, buf_ref, sem_ref)])
    pages = [buf_ref.at[slot, k] for k in range(n_sel)]
    lo1 = _lane_lo(1)
    q = qr_ref[0].astype(F32)
    halves = [q[:, 0:LANES], q[:, LANES:2 * LANES]]
    rows = []
    for r in range(NSA_GROUP):
        a = halves[r // 2]
        rows.append(jnp.where(lo1, a if r % 2 == 0 else pltpu.roll(a, HALF, 1), 0.0))
    qs_f = _pad_rows(jnp.concatenate(rows, axis=0), PAD_ROWS)
    qs = qs_f.astype(BF16)
    twice_rows = lambda a: jnp.concatenate([a, a], axis=0)

    s_t = jnp.concatenate([pg[...] for pg in pages], axis=1)
    s_sel = _dot(qs, s_t.astype(BF16))
    base = (bb * NSA_KV_HEADS + g) * n_sel
    biases = []
    for k in range(n_sel - 1):
        odd = (idx_ref[base + k] & 1) == 1
        biases.append(jnp.where(lo1, jnp.where(odd, NEG, 0.0), jnp.where(odd, 0.0, NEG)))
    biases.append(jnp.full((1, LANES), NEG, F32))
    s_sel = s_sel + jnp.concatenate(biases, axis=1)
    nk = newkv_ref[0].astype(F32)
    s_new = jnp.sum(qs_f * nk[:, :LANES], axis=-1, keepdims=True)
    o_s = _extra_key_softmax(s_sel, twice_rows(s_t[HD:2 * HD]).astype(BF16), s_new, nk[:, LANES:])

    w = win_ref[0]
    nw = neww_ref[0].astype(F32)
    s_w = _dot(qs, w.astype(BF16))
    s_wn = jnp.sum(qs_f * nw[:, :LANES], axis=-1, keepdims=True)
    o_w = _extra_key_softmax(s_w, twice_rows(w[HD:2 * HD]).astype(BF16), s_wn, nw[:, LANES:])

    o_c = oc_ref[0, 0]
    gates = gate_ref[0]
    gate = jnp.where(g == 0, gates[:, 0:GATE_ROWS], gates[:, GATE_ROWS:2 * GATE_ROWS])
    heads = []
    for r in range(NSA_GROUP):
        heads.append(gate[:, 3 * r:3 * r + 1] * o_c[r:r + 1] + gate[:, 3 * r + 1:3 * r + 2] * o_s[r:r + 1]
                     + gate[:, 3 * r + 2:3 * r + 3] * o_w[r:r + 1])
    o_ref[0] = jnp.concatenate([jnp.where(lo1, heads[0], heads[1]), jnp.where(lo1, heads[2], heads[3])], axis=1)

    n_feat, n_w = w.shape
    new_row = jnp.broadcast_to(newwf_ref[0], (n_feat, n_feat))
    diag = (lax.broadcasted_iota(jnp.int32, (n_feat, n_feat), 0)
            == lax.broadcasted_iota(jnp.int32, (n_feat, n_feat), 1))
    new_col = jnp.sum(jnp.where(diag, new_row, 0.0), axis=1, keepdims=True)
    last = lax.broadcasted_iota(jnp.int32, (n_feat, LANES), 1) == LANES - 1
    chunks = []
    n_ch = n_w // LANES
    for c in range(n_ch):
        cur = pltpu.roll(w[:, c * LANES:(c + 1) * LANES], LANES - 1, 1)
        if c + 1 < n_ch:
            nxt = pltpu.roll(w[:, (c + 1) * LANES:(c + 2) * LANES], LANES - 1, 1)
        else:
            nxt = jnp.broadcast_to(new_col, (n_feat, LANES))
        chunks.append(jnp.where(last, nxt, cur))
    wout_ref[0] = jnp.concatenate(chunks, axis=1)


def _sample_attend(p, oc, idx_flat, pt_flat, slc_t, win_t, b, n_pages):
    page = slc_t.shape[2]
    win_len = win_t.shape[2]
    assert page == 2 * SEL_BLOCK
    row = lambda w: pl.BlockSpec((1, 1, w), lambda bb, g, idx, pt: (bb, 0, g))
    wspec = pl.BlockSpec((1, LANES, win_len), lambda bb, g, idx, pt: (bb, g, 0))
    gs = pltpu.PrefetchScalarGridSpec(
        num_scalar_prefetch=2, grid=(b, NSA_KV_HEADS),
        in_specs=[pl.BlockSpec(memory_space=pl.ANY),
                  row(2 * LANES), row(2 * LANES), row(2 * LANES), row(LANES), wspec,
                  pl.BlockSpec((1, 1, NSA_KV_HEADS * GATE_ROWS), lambda bb, g, idx, pt: (bb, 0, 0)),
                  pl.BlockSpec((1, 1, PAD_ROWS, LANES), lambda bb, g, idx, pt: (bb, g, 0, 0))],
        out_specs=[row(2 * LANES), wspec],
        scratch_shapes=[pltpu.VMEM((2, N_SELECT, LANES, page), F32), pltpu.SemaphoreType.DMA((2,))])
    return pl.pallas_call(
        functools.partial(_sample_attend_kernel, n_pages), grid_spec=gs,
        out_shape=[jax.ShapeDtypeStruct((b, 1, NSA_WIDTH), F32),
                   jax.ShapeDtypeStruct((b, KV_WIDTH, win_len), F32)],
        compiler_params=pltpu.CompilerParams(dimension_semantics=("arbitrary",) * 2, vmem_limit_bytes=VMEM_LIMIT),
        name="sample_attend",
    )(idx_flat, pt_flat, slc_t, p["qrot"], p["slckv"], p["winkv"], p["win"], win_t, p["gate"], oc)


def _softmax_update(s, v_b, m, l, acc):
    m_new = jnp.maximum(m, jnp.max(s, axis=-1, keepdims=True))
    p = jnp.exp(s - m_new)
    alpha = jnp.exp(m - m_new)
    return m_new, alpha * l + jnp.sum(p, axis=-1, keepdims=True), alpha * acc + _dot(p.astype(BF16), v_b)


def _mla_sample_kernel(pt_ref, lat_ref, krt_ref, qlat_ref, qpe_ref, cnew_ref, krnew_ref, o_ref,
                       m_ref, l_ref, acc_ref, lat_buf, kr_buf, lat_sem, kr_sem):
    _mla_sample_step(pt_ref, lat_ref, krt_ref, qlat_ref, qpe_ref, cnew_ref, krnew_ref, o_ref,
                     m_ref, l_ref, acc_ref, lat_buf, kr_buf, lat_sem, kr_sem)


def _mla_sample_step(pt_ref, lat_ref, krt_ref, qlat_ref, qpe_ref, cnew_ref, krnew_ref, o_ref, m_ref, l_ref, acc_ref,
                     lat_buf, kr_buf, lat_sem, kr_sem, ring_step=None, ring_last=None, step=None, n_steps=None):
    n = lat_buf.shape[1]
    slot = _paged_pipeline([(_table_pages(pt_ref, lat_ref, n), lat_buf, lat_sem),
                            (_table_pages(pt_ref, krt_ref, n), kr_buf, kr_sem)], ring_step, ring_last)
    lat_pages = [lat_buf.at[slot, k] for k in range(n)]
    kr_pages = [kr_buf.at[slot, k] for k in range(n)]
    if step is None:
        step, n_steps = pl.program_id(1), pl.num_programs(1)
    qlat = _pad_rows(qlat_ref[0], PAD_ROWS)
    qpe = _pad_rows(qpe_ref[0], PAD_ROWS)

    @pl.when(step == 0)
    def _():
        c_new = cnew_ref[0]
        s_new = (jnp.sum(qlat * c_new, axis=-1, keepdims=True)
                 + jnp.sum(qpe * krnew_ref[0], axis=-1, keepdims=True))
        m_ref[...] = s_new
        l_ref[...] = jnp.ones(l_ref.shape, F32)
        acc_ref[...] = jnp.broadcast_to(c_new, acc_ref.shape)

    qlat_b, qpe_b = qlat.astype(BF16), qpe.astype(BF16)
    groups = range(0, n, POOL_PAGES)
    c_bs = [jnp.concatenate([pg[...] for pg in lat_pages[k:k + POOL_PAGES]], axis=0).astype(BF16) for k in groups]
    kr_bs = [jnp.concatenate([pg[...] for pg in kr_pages[k:k + POOL_PAGES]], axis=1).astype(BF16) for k in groups]
    s_lat = [_dot_nt(qlat_b, c_b) for c_b in c_bs]
    s_pe = [_dot(qpe_b, kr_b) for kr_b in kr_bs]
    scores = [a + r for a, r in zip(s_lat, s_pe)]
    maxes = [jnp.max(s, axis=-1, keepdims=True) for s in scores]
    ps = [jnp.exp(s - mx) for s, mx in zip(scores, maxes)]
    sums = [jnp.sum(p_g, axis=-1, keepdims=True) for p_g in ps]
    accs = [_dot(p_g.astype(BF16), c_b) for p_g, c_b in zip(ps, c_bs)]
    m_old = m_ref[...]
    m = m_old
    for mx in maxes:
        m = jnp.maximum(m, mx)
    alpha = jnp.exp(m_old - m)
    l = alpha * l_ref[...]
    acc = alpha * acc_ref[...]
    for mx, l_g, acc_g in zip(maxes, sums, accs):
        w = jnp.exp(mx - m)
        l = l + w * l_g
        acc = acc + w * acc_g
    m_ref[...] = m
    l_ref[...] = l
    acc_ref[...] = acc

    @pl.when(step == n_steps - 1)
    def _():
        o_ref[0] = (acc / l)[0:MLA_HEADS]


def _mla_sample(p, lat_cache, kr_t, pt_flat, b, n_pages):
    page = lat_cache.shape[1]
    head = lambda w: pl.BlockSpec((1, MLA_HEADS, w), lambda bb, c, pt: (bb, 0, 0))
    row = lambda w: pl.BlockSpec((1, 1, w), lambda bb, c, pt: (bb, 0, 0))
    gs = pltpu.PrefetchScalarGridSpec(
        num_scalar_prefetch=1, grid=(b, n_pages // PAGES_PER_STEP),
        in_specs=[pl.BlockSpec(memory_space=pl.ANY), pl.BlockSpec(memory_space=pl.ANY),
                  head(MLA_KV_LORA), head(MLA_ROPE), row(MLA_KV_LORA), row(MLA_ROPE)],
        out_specs=head(MLA_KV_LORA),
        scratch_shapes=[pltpu.VMEM((PAD_ROWS, 1), F32), pltpu.VMEM((PAD_ROWS, 1), F32),
                        pltpu.VMEM((PAD_ROWS, MLA_KV_LORA), F32),
                        pltpu.VMEM((2, PAGES_PER_STEP, page, MLA_KV_LORA), F32),
                        pltpu.VMEM((2, PAGES_PER_STEP, MLA_ROPE, page), F32),
                        pltpu.SemaphoreType.DMA((2,)), pltpu.SemaphoreType.DMA((2,))])
    qlat = p["qlat"].reshape(b, MLA_HEADS, MLA_KV_LORA)
    qpe = p["qpe"].reshape(b, MLA_HEADS, MLA_ROPE)
    return pl.pallas_call(
        _mla_sample_kernel, grid_spec=gs,
        out_shape=jax.ShapeDtypeStruct((b, MLA_HEADS, MLA_KV_LORA), F32),
        compiler_params=pltpu.CompilerParams(dimension_semantics=("arbitrary",) * 2, vmem_limit_bytes=VMEM_LIMIT),
        name="mla_sample",
    )(pt_flat, lat_cache, kr_t, qlat, qpe, p["c"], p["kr"])


def _rope_angles(pos, theta, dim):
    half = dim // 2
    inv = 1.0 / (float(theta) ** (np.arange(half, dtype=np.float64) / half))
    ang = np.asarray(pos, np.float64)[:, None] * inv[None, :]
    return np.cos(ang).astype(np.float32), np.sin(ang).astype(np.float32)


def _rope_lane_tables(pos, rows):
    def table(theta, dim, period, active):
        half = dim // 2
        cos, sin = _rope_angles(pos, theta, dim)
        lane = np.arange(LANES)
        d = lane % period
        is_lo = (d < half) & active(lane)
        is_hi = (d >= half) & (d < dim) & active(lane)
        fi = np.where(d < half, d, np.clip(d - half, 0, half - 1))
        cos_l, sin_l = cos[:, fi], sin[:, fi]
        tab = np.stack([np.where(is_lo | is_hi, cos_l, 1.0), np.where(is_lo, -sin_l, 0.0),
                        np.where(is_hi, sin_l, 0.0)]).astype(np.float32)
        return jnp.asarray(np.broadcast_to(tab, (3, rows, LANES)))

    every = lambda lane: np.ones_like(lane, bool)
    keys_only = lambda lane: (lane % LANES) < HD
    return (table(ROPE_THETA, ROT_DIM, HD, every), table(ROPE_THETA, ROT_DIM, HD, keys_only),
            table(MLA_ROPE_THETA, MLA_ROPE, MLA_ROPE, every))


def _rope_row_tables(pos):
    cq, sq = _rope_angles(pos, ROPE_THETA, ROT_DIM)
    cm, sm = _rope_angles(pos, MLA_ROPE_THETA, MLA_ROPE)
    return jnp.asarray(np.stack([cq.T, sq.T])), jnp.asarray(np.stack([cm.T, sm.T]))


def _pack_weights(l, norm_pre, w_in, pe_cmp, q_norm, w_q_up, kv_norm, w_kv_up, w_proj_a, w_proj_b, w_out, norm_post):
    w = w_in[l].astype(BF16)
    o = IN_OFFSETS
    seg = lambda k: w[:, o[k]:o[k + 1]]
    gn = seg(4)
    per_group = 3 * NSA_GROUP
    gn_t = jnp.zeros((NSA_KV_HEADS * GATE_ROWS, D_MODEL), w.dtype)
    for g in range(NSA_KV_HEADS):
        gn_t = gn_t.at[g * GATE_ROWS:g * GATE_ROWS + per_group].set(gn[:, g * per_group:(g + 1) * per_group].T)
    w_t = jnp.concatenate([seg(0).T, seg(1).T, seg(2).T, seg(3).T, gn_t, seg(8).T], axis=0)
    w_krp4 = jnp.tile(seg(8).T, (LANES // MLA_ROPE, 1))
    w_cols = jnp.concatenate([seg(5), seg(6), seg(7), seg(9), seg(10), seg(11)], axis=1)
    wq = w_q_up[l]
    wqup = jnp.concatenate([wq[..., :MLA_NOPE].reshape(MLA_Q_LORA, -1), wq[..., MLA_NOPE:].reshape(MLA_Q_LORA, -1)],
                           axis=1)
    wkv = w_kv_up[l]
    wuk_pad = jnp.pad(wkv[..., :MLA_NOPE], ((0, 0), (0, 0), (0, LANES - MLA_NOPE)))
    rope_copy = jnp.pad(jnp.eye(MLA_ROPE, dtype=w.dtype), ((0, LANES - MLA_ROPE), (MLA_NOPE, MLA_ROPE)))
    wkx = jnp.concatenate([wuk_pad.reshape(MLA_KV_LORA, MLA_HEADS * LANES), jnp.tile(rope_copy, (1, MLA_HEADS))],
                          axis=0)
    wvt = jnp.transpose(wkv[..., MLA_NOPE:], (1, 2, 0)).reshape(MLA_WIDTH, MLA_KV_LORA)
    w2uk =jnp.transpose(wkv[..., :MLA_NOPE], (1, 2, 0)).reshape(MLA_HEADS // 2, LANES, MLA_KV_LORA)
    wv = jnp.transpose(wkv[..., MLA_NOPE:], (1, 0, 2))
    zeros = jnp.zeros_like(wv)
    even = (jnp.arange(MLA_HEADS) % 2 == 0)[:, None, None]
    wuv = jnp.concatenate([jnp.where(even, wv, zeros), jnp.where(even, zeros, wv)], axis=2)
    return {
        "npre": norm_pre[l][None].astype(F32), "w_t": w_t, "w_krp4": w_krp4, "w_cols": w_cols, "pe": pe_cmp[l].reshape(CMP_BLOCK, KV_WIDTH).astype(F32),
        "qnorm": q_norm[l][None].astype(F32), "wqup": wqup.astype(BF16), "wqupt": wqup.T.astype(BF16),
        "kvnorm": kv_norm[l][None].astype(F32), "w2uk": w2uk.astype(BF16), "wkx": wkx.astype(BF16),
        "wvt": wvt.astype(BF16), "wuv": wuv.astype(BF16),
        "wpa": w_proj_a[l].astype(BF16), "wpb": w_proj_b[l].astype(BF16), "wout": w_out[l].astype(BF16),
        "npost": norm_post[l][None].astype(F32),
    }


def _rows_from_cols(a):
    b, _, t = a.shape
    return a.reshape(b, NSA_KV_HEADS, 2, HD, t).transpose(0, 4, 1, 2, 3)


def _cols_from_rows(a):
    n, t = a.shape[:2]
    return a.transpose(0, 2, 3, 4, 1).reshape(n, KV_WIDTH, t)


def _prompt_layer(x, wts, compress_job, sample_job):
    b, t, _ = x.shape
    assert t % PROMPT_TM == 0
    p = _in_project_cols(x, _rope_row_tables(np.arange(t)), wts)
    o_a, kc_all = _nsa_prompt(p, b, t, compress_job)
    o_b, o_lat = _mla_prompt(p, wts, b, t, sample_job)
    y = _merge(x, o_a, o_b, p, wts, 512, from_latent=False)
    win_keep = min(WINDOW, t)
    return y, (_rows_from_cols(p["cmp"]), _rows_from_cols(p["slc"]), p["c"], p["kr"].transpose(0, 2, 1),
               _rows_from_cols(p["win"][:, :, t - win_keep:])), kc_all, o_lat


def _sample_project(x, page_table, page, wts):
    b, s_new, _ = x.shape
    assert s_new == 1
    past = page_table.shape[1] * page
    tabs = _rope_lane_tables(np.full((1,), past), b)
    p = _in_project_rows(x.reshape(1, b, D_MODEL), tabs, wts)
    return {k: v.reshape(b, 1, v.shape[-1]) for k, v in p.items()}


def _sample_layer(x, p, l, caches, state_win, page_table, wts, kc_all, o_lat):
    cache_cmp, cache_slc, cache_lat, cache_kr = caches
    b = x.shape[0]
    n_pages = page_table.shape[1]
    page = cache_cmp.shape[2]
    past = n_pages * page
    assert past % SEL_BLOCK == 0 and n_pages % PAGES_PER_STEP == 0 and page == LANES
    win_len = state_win.shape[1]
    assert win_len == WINDOW and past >= WINDOW
    pt_flat = page_table.reshape(-1).astype(jnp.int32)

    if kc_all is None:
        kc_all = _compress_pages(_cols_from_rows(cache_cmp[l]), pt_flat, wts["pe"], b, n_pages)
    oc, idx = _sample_select(p["q"], kc_all, b)
    o_a, new_win = _sample_attend(p, oc, idx.reshape(-1), pt_flat, _cols_from_rows(cache_slc[l]),
                                  _cols_from_rows(state_win), b, n_pages)
    if o_lat is None:
        o_lat = _mla_sample(p, cache_lat[l], cache_kr[l].transpose(0, 2, 1), pt_flat, b, n_pages)
    pm = {k: p[k].reshape(1, b, -1) for k in ("ga", "gb", "ma", "mb")}
    y = _merge(x.reshape(1, b, D_MODEL), o_a.reshape(1, b, NSA_WIDTH),
               o_lat.reshape(1, b, MLA_HEADS * MLA_KV_LORA), pm, wts, b, from_latent=True)
    kv6 = lambda a: a.reshape(b, 1, NSA_KV_HEADS, 2, HD)
    return y.reshape(b, 1, D_MODEL), (kv6(p["cmp"]), kv6(p["slc"]), p["c"], p["kr"], _rows_from_cols(new_win))


def kernel(x_prompt, x_sample, cache_nsa_cmp, cache_nsa_slc, cache_mla_latent, cache_mla_krope, state_nsa_win,
           page_table, norm_pre, w_in, pe_cmp, q_norm, w_q_up, kv_norm, w_kv_up, w_proj_a, w_proj_b, w_out,
           norm_post):
    depth = w_in.shape[0]
    hp, hs = x_prompt, x_sample
    new_p, new_s = [], []
    for l in range(depth):
        wts = _pack_weights(l, norm_pre, w_in, pe_cmp, q_norm, w_q_up, kv_norm, w_kv_up, w_proj_a, w_proj_b,
                            w_out, norm_post)
        n_pages = page_table.shape[1]
        compress_job = None
        if n_pages % PAGES_PER_STEP == 0:
            compress_job = (_cols_from_rows(cache_nsa_cmp[l]), page_table.reshape(-1).astype(jnp.int32), wts["pe"],
                            page_table.shape[0], n_pages)
        ps = _sample_project(hs, page_table, cache_nsa_cmp.shape[2], wts)
        sample_job = (cache_mla_latent[l], cache_mla_krope[l].transpose(0, 2, 1),
                      page_table.reshape(-1).astype(jnp.int32), ps, page_table.shape[0], n_pages)
        hp, sp, kc_all, o_lat = _prompt_layer(hp, wts, compress_job, sample_job)
        hs, ss = _sample_layer(hs, ps, l, (cache_nsa_cmp, cache_nsa_slc, cache_mla_latent, cache_mla_krope),
                               state_nsa_win[l], page_table, wts, kc_all, o_lat)
        new_p.append(sp)
        new_s.append(ss)
    stack = lambda items, k: jnp.stack([s[k] for s in items])
    return (hp, hs) + tuple(stack(new_p, k) for k in range(5)) + tuple(stack(new_s, k) for k in range(5))
```
